```python
import jax, jax.numpy as jnp
from jax import lax
import numpy as np

D_MODEL = 1024
BATCH = 8
SEQ = 8192
DEPTH = 4

D_MIX = D_MODEL
N_MIXERS = 4
D_GROUP = D_MIX // N_MIXERS
HEAD_DIM = 64
N_HEADS = D_GROUP // HEAD_DIM
CONF_KERNEL = 31
SHORT_KERNEL = 3
POOL_WINDOWS = (2, 4, 8, 16)
POOL_GROUP = D_GROUP // len(POOL_WINDOWS)
CHUNK = 128
D_FF = 2816
N_IN_PIECES = 8
D_IN = N_IN_PIECES * D_GROUP
FFN_RESIDUAL = 0.5
EPS = 1e-6

kernel_name = "hybrid_macaron_parallel_conv_pool_gmlp"


def rmsnorm(x, g):
    xf = x.astype(jnp.float32)
    y = xf * lax.rsqrt(jnp.mean(xf * xf, axis=-1, keepdims=True) + EPS)
    return (y * g.astype(jnp.float32)).astype(x.dtype)


def layernorm(x, g, b):
    xf = x.astype(jnp.float32)
    mu = jnp.mean(xf, axis=-1, keepdims=True)
    xc = xf - mu
    y = xc * lax.rsqrt(jnp.mean(xc * xc, axis=-1, keepdims=True) + EPS)
    return (y * g.astype(jnp.float32) + b.astype(jnp.float32)).astype(x.dtype)


def causal_depthwise_conv(x, w):
    k, c = w.shape
    return lax.conv_general_dilated(
        x, w[:, None, :].astype(x.dtype), window_strides=(1,), padding=[(k - 1, 0)],
        dimension_numbers=("NWC", "WIO", "NWC"), feature_group_count=c)


def swiglu(h, w1, w3, w2):
    return (jax.nn.silu(h @ w1) * (h @ w3)) @ w2


def conformer_conv(val, gate, conv_w, conv_b, ln_g, ln_b):
    y = val * jax.nn.sigmoid(gate)
    y = causal_depthwise_conv(y, conv_w) + conv_b
    return jax.nn.silu(layernorm(y, ln_g, ln_b))


def short_gated_conv(b_gate, c_gate, xv, conv_w):
    return b_gate * causal_depthwise_conv(c_gate * xv, conv_w)


def multiscale_pool(xp, pool_w, pool_scale):
    bsz, s, _ = xp.shape
    xf = xp.astype(jnp.float32)
    cs = jnp.cumsum(xf, axis=1)
    pos = jnp.arange(1, s + 1, dtype=jnp.float32)[:, None]
    outs = []
    for g, w in enumerate(POOL_WINDOWS):
        sl = slice(g * POOL_GROUP, (g + 1) * POOL_GROUP)
        c = cs[..., sl]
        lagged = jnp.pad(c, ((0, 0), (w, 0), (0, 0)))[:, :s]
        mean = (c - lagged) / jnp.minimum(pos, float(w))
        outs.append(mean - xf[..., sl])
    d = jnp.stack(outs, axis=2).astype(xp.dtype)
    y = jnp.einsum("bsgc,gcd->bsgd", d, pool_w).reshape(bsz, s, D_GROUP)
    return y * pool_scale


def chunked_spatial_gating(u, v, ln_g, ln_b, w_s, b_s):
    bsz, s, _ = v.shape
    v = layernorm(v, ln_g, ln_b)
    vc = v.reshape(bsz, s // CHUNK, CHUNK, N_HEADS, HEAD_DIM)
    mask = jnp.tril(jnp.ones((CHUNK, CHUNK), dtype=bool))
    ws = jnp.where(mask[None], w_s, 0.0).astype(v.dtype)
    mixed = jnp.einsum("hts,bnshc->bnthc", ws, vc) + b_s.T[None, None, :, :, None]
    return u * mixed.reshape(bsz, s, D_GROUP)


def _fwd_setup_inputs(seed: int = 0) -> dict:
    key = jax.random.key(seed)
    ks = iter(jax.random.split(key, 32))

    def nrm(shape, scale):
        return jax.random.normal(next(ks), shape, dtype=jnp.float32) * scale

    def gain(shape):
        return 1.0 + nrm(shape, 0.02)

    L = DEPTH
    return {
        "x": nrm((BATCH, SEQ, D_MODEL), 1.0),
        "ffn1_norm": gain((L, D_MODEL)),
        "ffn1_w1": nrm((L, D_MODEL, D_FF), D_MODEL ** -0.5),
        "ffn1_w3": nrm((L, D_MODEL, D_FF), D_MODEL ** -0.5),
        "ffn1_w2": nrm((L, D_FF, D_MODEL), D_FF ** -0.5),
        "mix_norm": gain((L, D_MODEL)),
        "w_in": nrm((L, D_MODEL, D_IN), D_MODEL ** -0.5),
        "conf_conv_w": nrm((L, CONF_KERNEL, D_GROUP), CONF_KERNEL ** -0.5),
        "conf_conv_b": nrm((L, D_GROUP), 0.02),
        "conf_ln_g": gain((L, D_GROUP)),
        "conf_ln_b": nrm((L, D_GROUP), 0.02),
        "sconv_w": nrm((L, SHORT_KERNEL, D_GROUP), SHORT_KERNEL ** -0.5),
        "pool_w": nrm((L, len(POOL_WINDOWS), POOL_GROUP, POOL_GROUP), POOL_GROUP ** -0.5),
        "pool_scale": 1.0 + nrm((L, D_GROUP), 0.1),
        "gmlp_ln_g": gain((L, D_GROUP)),
        "gmlp_ln_b": nrm((L, D_GROUP), 0.02),
        "gmlp_w_s": nrm((L, N_HEADS, CHUNK, CHUNK), CHUNK ** -0.5),
        "gmlp_b_s": 1.0 + nrm((L, N_HEADS, CHUNK), 0.02),
        "w_out": nrm((L, D_MIX, D_MODEL), D_MIX ** -0.5),
        "ffn2_norm": gain((L, D_MODEL)),
        "ffn2_w1": nrm((L, D_MODEL, D_FF), D_MODEL ** -0.5),
        "ffn2_w3": nrm((L, D_MODEL, D_FF), D_MODEL ** -0.5),
        "ffn2_w2": nrm((L, D_FF, D_MODEL), D_FF ** -0.5),
        "final_norm": gain((D_MODEL,)),
    }


def _fwd_reference(x, ffn1_norm, ffn1_w1, ffn1_w3, ffn1_w2, mix_norm, w_in,
              conf_conv_w, conf_conv_b, conf_ln_g, conf_ln_b, sconv_w,
              pool_w, pool_scale, gmlp_ln_g, gmlp_ln_b, gmlp_w_s, gmlp_b_s,
              w_out, ffn2_norm, ffn2_w1, ffn2_w3, ffn2_w2, final_norm):
    for l in range(DEPTH):
        h = rmsnorm(x, ffn1_norm[l])
        x = x + FFN_RESIDUAL * swiglu(h, ffn1_w1[l], ffn1_w3[l], ffn1_w2[l])

        h = rmsnorm(x, mix_norm[l])
        p = h @ w_in[l]
        a_val, a_gate, s_b, s_c, s_x, pool_in, g_u, g_v = jnp.split(p, N_IN_PIECES, axis=-1)

        y_a = conformer_conv(a_val, a_gate, conf_conv_w[l], conf_conv_b[l],
                             conf_ln_g[l], conf_ln_b[l])
        y_b = short_gated_conv(s_b, s_c, s_x, sconv_w[l])
        y_c = multiscale_pool(pool_in, pool_w[l], pool_scale[l])
        y_d = chunked_spatial_gating(g_u, g_v, gmlp_ln_g[l], gmlp_ln_b[l],
                                     gmlp_w_s[l], gmlp_b_s[l])

        mix = jnp.concatenate([y_a, y_b, y_c, y_d], axis=-1)
        x = x + mix @ w_out[l]

        h = rmsnorm(x, ffn2_norm[l])
        x = x + FFN_RESIDUAL * swiglu(h, ffn2_w1[l], ffn2_w3[l], ffn2_w2[l])

    return rmsnorm(x, final_norm)


import jax as _jax
import jax.numpy as _jnp

TWIN_FORMAT = 'train_step'
FWD_PARAMS = ['x', 'ffn1_norm', 'ffn1_w1', 'ffn1_w3', 'ffn1_w2', 'mix_norm', 'w_in', 'conf_conv_w', 'conf_conv_b', 'conf_ln_g', 'conf_ln_b', 'sconv_w', 'pool_w', 'pool_scale', 'gmlp_ln_g', 'gmlp_ln_b', 'gmlp_w_s', 'gmlp_b_s', 'w_out', 'ffn2_norm', 'ffn2_w1', 'ffn2_w3', 'ffn2_w2', 'final_norm']
TWIN_WEIGHTS = ['ffn1_norm', 'ffn1_w1', 'ffn1_w3', 'ffn1_w2', 'mix_norm', 'w_in', 'conf_conv_w', 'conf_conv_b', 'conf_ln_g', 'conf_ln_b', 'sconv_w', 'pool_w', 'pool_scale', 'gmlp_ln_g', 'gmlp_ln_b', 'gmlp_w_s', 'gmlp_b_s', 'w_out', 'ffn2_norm', 'ffn2_w1', 'ffn2_w3', 'ffn2_w2', 'final_norm']
TWIN_DIFF_INPUT = 'x'
TWIN_INPUTS = ['x', 'ffn1_norm', 'ffn1_w1', 'ffn1_w3', 'ffn1_w2', 'mix_norm', 'w_in', 'conf_conv_w', 'conf_conv_b', 'conf_ln_g', 'conf_ln_b', 'sconv_w', 'pool_w', 'pool_scale', 'gmlp_ln_g', 'gmlp_ln_b', 'gmlp_w_s', 'gmlp_b_s', 'w_out', 'ffn2_norm', 'ffn2_w1', 'ffn2_w3', 'ffn2_w2', 'final_norm', 'loss_target', 'm_ffn1_norm', 'm_ffn1_w1', 'm_ffn1_w3', 'm_ffn1_w2', 'm_mix_norm', 'm_w_in', 'm_conf_conv_w', 'm_conf_conv_b', 'm_conf_ln_g', 'm_conf_ln_b', 'm_sconv_w', 'm_pool_w', 'm_pool_scale', 'm_gmlp_ln_g', 'm_gmlp_ln_b', 'm_gmlp_w_s', 'm_gmlp_b_s', 'm_w_out', 'm_ffn2_norm', 'm_ffn2_w1', 'm_ffn2_w3', 'm_ffn2_w2', 'm_final_norm', 'v_ffn1_norm', 'v_ffn1_w1', 'v_ffn1_w3', 'v_ffn1_w2', 'v_mix_norm', 'v_w_in', 'v_conf_conv_w', 'v_conf_conv_b', 'v_conf_ln_g', 'v_conf_ln_b', 'v_sconv_w', 'v_pool_w', 'v_pool_scale', 'v_gmlp_ln_g', 'v_gmlp_ln_b', 'v_gmlp_w_s', 'v_gmlp_b_s', 'v_w_out', 'v_ffn2_norm', 'v_ffn2_w1', 'v_ffn2_w3', 'v_ffn2_w2', 'v_final_norm']
TWIN_OUTPUTS = ['loss', 'grad_x', 'grad_ffn1_norm', 'grad_ffn1_w1', 'grad_ffn1_w3', 'grad_ffn1_w2', 'grad_mix_norm', 'grad_w_in', 'grad_conf_conv_w', 'grad_conf_conv_b', 'grad_conf_ln_g', 'grad_conf_ln_b', 'grad_sconv_w', 'grad_pool_w', 'grad_pool_scale', 'grad_gmlp_ln_g', 'grad_gmlp_ln_b', 'grad_gmlp_w_s', 'grad_gmlp_b_s', 'grad_w_out', 'grad_ffn2_norm', 'grad_ffn2_w1', 'grad_ffn2_w3', 'grad_ffn2_w2', 'grad_final_norm', 'delta_ffn1_norm', 'delta_ffn1_w1', 'delta_ffn1_w3', 'delta_ffn1_w2', 'delta_mix_norm', 'delta_w_in', 'delta_conf_conv_w', 'delta_conf_conv_b', 'delta_conf_ln_g', 'delta_conf_ln_b', 'delta_sconv_w', 'delta_pool_w', 'delta_pool_scale', 'delta_gmlp_ln_g', 'delta_gmlp_ln_b', 'delta_gmlp_w_s', 'delta_gmlp_b_s', 'delta_w_out', 'delta_ffn2_norm', 'delta_ffn2_w1', 'delta_ffn2_w3', 'delta_ffn2_w2', 'delta_final_norm', 'new_m_ffn1_norm', 'new_m_ffn1_w1', 'new_m_ffn1_w3', 'new_m_ffn1_w2', 'new_m_mix_norm', 'new_m_w_in', 'new_m_conf_conv_w', 'new_m_conf_conv_b', 'new_m_conf_ln_g', 'new_m_conf_ln_b', 'new_m_sconv_w', 'new_m_pool_w', 'new_m_pool_scale', 'new_m_gmlp_ln_g', 'new_m_gmlp_ln_b', 'new_m_gmlp_w_s', 'new_m_gmlp_b_s', 'new_m_w_out', 'new_m_ffn2_norm', 'new_m_ffn2_w1', 'new_m_ffn2_w3', 'new_m_ffn2_w2', 'new_m_final_norm', 'new_v_ffn1_norm', 'new_v_ffn1_w1', 'new_v_ffn1_w3', 'new_v_ffn1_w2', 'new_v_mix_norm', 'new_v_w_in', 'new_v_conf_conv_w', 'new_v_conf_conv_b', 'new_v_conf_ln_g', 'new_v_conf_ln_b', 'new_v_sconv_w', 'new_v_pool_w', 'new_v_pool_scale', 'new_v_gmlp_ln_g', 'new_v_gmlp_ln_b', 'new_v_gmlp_w_s', 'new_v_gmlp_b_s', 'new_v_w_out', 'new_v_ffn2_norm', 'new_v_ffn2_w1', 'new_v_ffn2_w3', 'new_v_ffn2_w2', 'new_v_final_norm']
TWIN_LEAF_KINDS = {'loss': 'loss', 'grad_x': 'grad_x', 'grad_ffn1_norm': 'grad_w', 'grad_ffn1_w1': 'grad_w', 'grad_ffn1_w3': 'grad_w', 'grad_ffn1_w2': 'grad_w', 'grad_mix_norm': 'grad_w', 'grad_w_in': 'grad_w', 'grad_conf_conv_w': 'grad_w', 'grad_conf_conv_b': 'grad_w', 'grad_conf_ln_g': 'grad_w', 'grad_conf_ln_b': 'grad_w', 'grad_sconv_w': 'grad_w', 'grad_pool_w': 'grad_w', 'grad_pool_scale': 'grad_w', 'grad_gmlp_ln_g': 'grad_w', 'grad_gmlp_ln_b': 'grad_w', 'grad_gmlp_w_s': 'grad_w', 'grad_gmlp_b_s': 'grad_w', 'grad_w_out': 'grad_w', 'grad_ffn2_norm': 'grad_w', 'grad_ffn2_w1': 'grad_w', 'grad_ffn2_w3': 'grad_w', 'grad_ffn2_w2': 'grad_w', 'grad_final_norm': 'grad_w', 'delta_ffn1_norm': 'delta_w', 'delta_ffn1_w1': 'delta_w', 'delta_ffn1_w3': 'delta_w', 'delta_ffn1_w2': 'delta_w', 'delta_mix_norm': 'delta_w', 'delta_w_in': 'delta_w', 'delta_conf_conv_w': 'delta_w', 'delta_conf_conv_b': 'delta_w', 'delta_conf_ln_g': 'delta_w', 'delta_conf_ln_b': 'delta_w', 'delta_sconv_w': 'delta_w', 'delta_pool_w': 'delta_w', 'delta_pool_scale': 'delta_w', 'delta_gmlp_ln_g': 'delta_w', 'delta_gmlp_ln_b': 'delta_w', 'delta_gmlp_w_s': 'delta_w', 'delta_gmlp_b_s': 'delta_w', 'delta_w_out': 'delta_w', 'delta_ffn2_norm': 'delta_w', 'delta_ffn2_w1': 'delta_w', 'delta_ffn2_w3': 'delta_w', 'delta_ffn2_w2': 'delta_w', 'delta_final_norm': 'delta_w', 'new_m_ffn1_norm': 'new_m', 'new_m_ffn1_w1': 'new_m', 'new_m_ffn1_w3': 'new_m', 'new_m_ffn1_w2': 'new_m', 'new_m_mix_norm': 'new_m', 'new_m_w_in': 'new_m', 'new_m_conf_conv_w': 'new_m', 'new_m_conf_conv_b': 'new_m', 'new_m_conf_ln_g': 'new_m', 'new_m_conf_ln_b': 'new_m', 'new_m_sconv_w': 'new_m', 'new_m_pool_w': 'new_m', 'new_m_pool_scale': 'new_m', 'new_m_gmlp_ln_g': 'new_m', 'new_m_gmlp_ln_b': 'new_m', 'new_m_gmlp_w_s': 'new_m', 'new_m_gmlp_b_s': 'new_m', 'new_m_w_out': 'new_m', 'new_m_ffn2_norm': 'new_m', 'new_m_ffn2_w1': 'new_m', 'new_m_ffn2_w3': 'new_m', 'new_m_ffn2_w2': 'new_m', 'new_m_final_norm': 'new_m', 'new_v_ffn1_norm': 'new_v', 'new_v_ffn1_w1': 'new_v', 'new_v_ffn1_w3': 'new_v', 'new_v_ffn1_w2': 'new_v', 'new_v_mix_norm': 'new_v', 'new_v_w_in': 'new_v', 'new_v_conf_conv_w': 'new_v', 'new_v_conf_conv_b': 'new_v', 'new_v_conf_ln_g': 'new_v', 'new_v_conf_ln_b': 'new_v', 'new_v_sconv_w': 'new_v', 'new_v_pool_w': 'new_v', 'new_v_pool_scale': 'new_v', 'new_v_gmlp_ln_g': 'new_v', 'new_v_gmlp_ln_b': 'new_v', 'new_v_gmlp_w_s': 'new_v', 'new_v_gmlp_b_s': 'new_v', 'new_v_w_out': 'new_v', 'new_v_ffn2_norm': 'new_v', 'new_v_ffn2_w1': 'new_v', 'new_v_ffn2_w3': 'new_v', 'new_v_ffn2_w2': 'new_v', 'new_v_final_norm': 'new_v'}


def _forward(args):
    return _fwd_reference(*[args[k] for k in FWD_PARAMS])


def _output_shape():
    out = _jax.eval_shape(lambda: _forward(_fwd_setup_inputs(0)))
    return out.shape, out.dtype

N_MICROBATCH = 1
ADAM_LR = 0.001
ADAM_B1 = 0.9
ADAM_B2 = 0.999
ADAM_EPS = 1e-08
ADAM_WD = 0.01
ADAM_STEP = 10
PER_EXAMPLE_BATCH_AXIS = {'x': 0, 'loss_target': 0}
SHARED_INPUTS = []
_WEIGHT_DTYPES = {'ffn1_norm': _jnp.float32, 'ffn1_w1': _jnp.float32, 'ffn1_w3': _jnp.float32, 'ffn1_w2': _jnp.float32, 'mix_norm': _jnp.float32, 'w_in': _jnp.float32, 'conf_conv_w': _jnp.float32, 'conf_conv_b': _jnp.float32, 'conf_ln_g': _jnp.float32, 'conf_ln_b': _jnp.float32, 'sconv_w': _jnp.float32, 'pool_w': _jnp.float32, 'pool_scale': _jnp.float32, 'gmlp_ln_g': _jnp.float32, 'gmlp_ln_b': _jnp.float32, 'gmlp_w_s': _jnp.float32, 'gmlp_b_s': _jnp.float32, 'w_out': _jnp.float32, 'ffn2_norm': _jnp.float32, 'ffn2_w1': _jnp.float32, 'ffn2_w3': _jnp.float32, 'ffn2_w2': _jnp.float32, 'final_norm': _jnp.float32}
MOMENT_SCALE = {'ffn1_norm': 1.077087e-01, 'ffn1_w1': 4.549492e-02, 'ffn1_w3': 4.418253e-02, 'ffn1_w2': 7.330055e-02, 'mix_norm': 2.146882e-01, 'w_in': 1.529714e-01, 'conf_conv_w': 1.081435e-01, 'conf_conv_b': 2.389706e-01, 'conf_ln_g': 1.226619e-01, 'conf_ln_b': 1.013366e-01, 'sconv_w': 1.728914e-01, 'pool_w': 1.525545e-01, 'pool_scale': 1.694180e-01, 'gmlp_ln_g': 1.384408e-01, 'gmlp_ln_b': 1.190819e-01, 'gmlp_w_s': 8.746504e-02, 'gmlp_b_s': 1.243284e-01, 'w_out': 1.648683e-01, 'ffn2_norm': 7.682028e-02, 'ffn2_w1': 3.118164e-02, 'ffn2_w3': 3.021310e-02, 'ffn2_w2': 5.004402e-02, 'final_norm': 6.403396e+01}


def _to_microbatches(a, axis):
    t = _jnp.moveaxis(a, axis, 0)
    t = t.reshape((N_MICROBATCH, t.shape[0] // N_MICROBATCH) + t.shape[1:])
    return _jnp.moveaxis(t, 1, axis + 1)


def setup_inputs(seed: int = 0) -> dict:
    inp = _fwd_setup_inputs(seed)
    key = _jax.random.fold_in(_jax.random.key(seed), 7919)
    shape, _ = _output_shape()
    out = dict(inp)
    out["loss_target"] = _jax.random.normal(_jax.random.fold_in(key, 0), shape, _jnp.float32)
    for i, name in enumerate(TWIN_WEIGHTS):
        w = inp[name].astype(_jnp.float32)
        if MOMENT_SCALE is None:
            s = _jnp.sqrt(_jnp.mean(_jnp.square(w)) + 1e-30)
        else:
            s = MOMENT_SCALE[name]
        km, kv = _jax.random.split(_jax.random.fold_in(key, i + 1))
        out[name] = w
        out["m_" + name] = s * _jax.random.normal(km, w.shape, _jnp.float32)
        out["v_" + name] = (s * s) * _jax.random.uniform(kv, w.shape, _jnp.float32, 0.5, 1.5)
    if N_MICROBATCH > 1:
        for name, axis in PER_EXAMPLE_BATCH_AXIS.items():
            out[name] = _to_microbatches(out[name], axis)
    return {'x': out['x'], 'ffn1_norm': out['ffn1_norm'], 'ffn1_w1': out['ffn1_w1'], 'ffn1_w3': out['ffn1_w3'], 'ffn1_w2': out['ffn1_w2'], 'mix_norm': out['mix_norm'], 'w_in': out['w_in'], 'conf_conv_w': out['conf_conv_w'], 'conf_conv_b': out['conf_conv_b'], 'conf_ln_g': out['conf_ln_g'], 'conf_ln_b': out['conf_ln_b'], 'sconv_w': out['sconv_w'], 'pool_w': out['pool_w'], 'pool_scale': out['pool_scale'], 'gmlp_ln_g': out['gmlp_ln_g'], 'gmlp_ln_b': out['gmlp_ln_b'], 'gmlp_w_s': out['gmlp_w_s'], 'gmlp_b_s': out['gmlp_b_s'], 'w_out': out['w_out'], 'ffn2_norm': out['ffn2_norm'], 'ffn2_w1': out['ffn2_w1'], 'ffn2_w3': out['ffn2_w3'], 'ffn2_w2': out['ffn2_w2'], 'final_norm': out['final_norm'], 'loss_target': out['loss_target'], 'm_ffn1_norm': out['m_ffn1_norm'], 'm_ffn1_w1': out['m_ffn1_w1'], 'm_ffn1_w3': out['m_ffn1_w3'], 'm_ffn1_w2': out['m_ffn1_w2'], 'm_mix_norm': out['m_mix_norm'], 'm_w_in': out['m_w_in'], 'm_conf_conv_w': out['m_conf_conv_w'], 'm_conf_conv_b': out['m_conf_conv_b'], 'm_conf_ln_g': out['m_conf_ln_g'], 'm_conf_ln_b': out['m_conf_ln_b'], 'm_sconv_w': out['m_sconv_w'], 'm_pool_w': out['m_pool_w'], 'm_pool_scale': out['m_pool_scale'], 'm_gmlp_ln_g': out['m_gmlp_ln_g'], 'm_gmlp_ln_b': out['m_gmlp_ln_b'], 'm_gmlp_w_s': out['m_gmlp_w_s'], 'm_gmlp_b_s': out['m_gmlp_b_s'], 'm_w_out': out['m_w_out'], 'm_ffn2_norm': out['m_ffn2_norm'], 'm_ffn2_w1': out['m_ffn2_w1'], 'm_ffn2_w3': out['m_ffn2_w3'], 'm_ffn2_w2': out['m_ffn2_w2'], 'm_final_norm': out['m_final_norm'], 'v_ffn1_norm': out['v_ffn1_norm'], 'v_ffn1_w1': out['v_ffn1_w1'], 'v_ffn1_w3': out['v_ffn1_w3'], 'v_ffn1_w2': out['v_ffn1_w2'], 'v_mix_norm': out['v_mix_norm'], 'v_w_in': out['v_w_in'], 'v_conf_conv_w': out['v_conf_conv_w'], 'v_conf_conv_b': out['v_conf_conv_b'], 'v_conf_ln_g': out['v_conf_ln_g'], 'v_conf_ln_b': out['v_conf_ln_b'], 'v_sconv_w': out['v_sconv_w'], 'v_pool_w': out['v_pool_w'], 'v_pool_scale': out['v_pool_scale'], 'v_gmlp_ln_g': out['v_gmlp_ln_g'], 'v_gmlp_ln_b': out['v_gmlp_ln_b'], 'v_gmlp_w_s': out['v_gmlp_w_s'], 'v_gmlp_b_s': out['v_gmlp_b_s'], 'v_w_out': out['v_w_out'], 'v_ffn2_norm': out['v_ffn2_norm'], 'v_ffn2_w1': out['v_ffn2_w1'], 'v_ffn2_w3': out['v_ffn2_w3'], 'v_ffn2_w2': out['v_ffn2_w2'], 'v_final_norm': out['v_final_norm']}


def _loss(weights, diff, rest, loss_target):
    with _jax.named_scope("forward"):
        args = {**rest, TWIN_DIFF_INPUT: diff, **{k: w.astype(_WEIGHT_DTYPES[k]) for k, w in weights.items()}}
        y = _forward(args)
    with _jax.named_scope("loss_head"):
        err = _jnp.square(y.astype(_jnp.float32) - loss_target)
        return 0.5 * _jnp.sum(_jnp.mean(err, axis=-1)) if err.ndim else 0.5 * err


def _adamw(w, g, m, v):
    m = ADAM_B1 * m + (1.0 - ADAM_B1) * g
    v = ADAM_B2 * v + (1.0 - ADAM_B2) * _jnp.square(g)
    m_hat = m / (1.0 - ADAM_B1 ** ADAM_STEP)
    v_hat = v / (1.0 - ADAM_B2 ** ADAM_STEP)
    delta = -ADAM_LR * (m_hat / (_jnp.sqrt(v_hat) + ADAM_EPS) + ADAM_WD * w)
    return delta, m, v


def reference(x, ffn1_norm, ffn1_w1, ffn1_w3, ffn1_w2, mix_norm, w_in, conf_conv_w, conf_conv_b, conf_ln_g, conf_ln_b, sconv_w, pool_w, pool_scale, gmlp_ln_g, gmlp_ln_b, gmlp_w_s, gmlp_b_s, w_out, ffn2_norm, ffn2_w1, ffn2_w3, ffn2_w2, final_norm, loss_target, m_ffn1_norm, m_ffn1_w1, m_ffn1_w3, m_ffn1_w2, m_mix_norm, m_w_in, m_conf_conv_w, m_conf_conv_b, m_conf_ln_g, m_conf_ln_b, m_sconv_w, m_pool_w, m_pool_scale, m_gmlp_ln_g, m_gmlp_ln_b, m_gmlp_w_s, m_gmlp_b_s, m_w_out, m_ffn2_norm, m_ffn2_w1, m_ffn2_w3, m_ffn2_w2, m_final_norm, v_ffn1_norm, v_ffn1_w1, v_ffn1_w3, v_ffn1_w2, v_mix_norm, v_w_in, v_conf_conv_w, v_conf_conv_b, v_conf_ln_g, v_conf_ln_b, v_sconv_w, v_pool_w, v_pool_scale, v_gmlp_ln_g, v_gmlp_ln_b, v_gmlp_w_s, v_gmlp_b_s, v_w_out, v_ffn2_norm, v_ffn2_w1, v_ffn2_w3, v_ffn2_w2, v_final_norm):
    given = dict(x=x, ffn1_norm=ffn1_norm, ffn1_w1=ffn1_w1, ffn1_w3=ffn1_w3, ffn1_w2=ffn1_w2, mix_norm=mix_norm, w_in=w_in, conf_conv_w=conf_conv_w, conf_conv_b=conf_conv_b, conf_ln_g=conf_ln_g, conf_ln_b=conf_ln_b, sconv_w=sconv_w, pool_w=pool_w, pool_scale=pool_scale, gmlp_ln_g=gmlp_ln_g, gmlp_ln_b=gmlp_ln_b, gmlp_w_s=gmlp_w_s, gmlp_b_s=gmlp_b_s, w_out=w_out, ffn2_norm=ffn2_norm, ffn2_w1=ffn2_w1, ffn2_w3=ffn2_w3, ffn2_w2=ffn2_w2, final_norm=final_norm, loss_target=loss_target, m_ffn1_norm=m_ffn1_norm, m_ffn1_w1=m_ffn1_w1, m_ffn1_w3=m_ffn1_w3, m_ffn1_w2=m_ffn1_w2, m_mix_norm=m_mix_norm, m_w_in=m_w_in, m_conf_conv_w=m_conf_conv_w, m_conf_conv_b=m_conf_conv_b, m_conf_ln_g=m_conf_ln_g, m_conf_ln_b=m_conf_ln_b, m_sconv_w=m_sconv_w, m_pool_w=m_pool_w, m_pool_scale=m_pool_scale, m_gmlp_ln_g=m_gmlp_ln_g, m_gmlp_ln_b=m_gmlp_ln_b, m_gmlp_w_s=m_gmlp_w_s, m_gmlp_b_s=m_gmlp_b_s, m_w_out=m_w_out, m_ffn2_norm=m_ffn2_norm, m_ffn2_w1=m_ffn2_w1, m_ffn2_w3=m_ffn2_w3, m_ffn2_w2=m_ffn2_w2, m_final_norm=m_final_norm, v_ffn1_norm=v_ffn1_norm, v_ffn1_w1=v_ffn1_w1, v_ffn1_w3=v_ffn1_w3, v_ffn1_w2=v_ffn1_w2, v_mix_norm=v_mix_norm, v_w_in=v_w_in, v_conf_conv_w=v_conf_conv_w, v_conf_conv_b=v_conf_conv_b, v_conf_ln_g=v_conf_ln_g, v_conf_ln_b=v_conf_ln_b, v_sconv_w=v_sconv_w, v_pool_w=v_pool_w, v_pool_scale=v_pool_scale, v_gmlp_ln_g=v_gmlp_ln_g, v_gmlp_ln_b=v_gmlp_ln_b, v_gmlp_w_s=v_gmlp_w_s, v_gmlp_b_s=v_gmlp_b_s, v_w_out=v_w_out, v_ffn2_norm=v_ffn2_norm, v_ffn2_w1=v_ffn2_w1, v_ffn2_w3=v_ffn2_w3, v_ffn2_w2=v_ffn2_w2, v_final_norm=v_final_norm)
    weights = {n: given[n] for n in TWIN_WEIGHTS}
    shared = {n: given[n] for n in SHARED_INPUTS}
    per_example = {n: given[n] for n in ['x']}
    grad_fn = _jax.value_and_grad(_loss, argnums=(0, 1))

    def one_microbatch(ex, loss_target):
        ex = dict(ex)
        diff = ex.pop(TWIN_DIFF_INPUT)
        return grad_fn(weights, diff, {**shared, **ex}, loss_target)

    if N_MICROBATCH == 1:
        loss, (grad_w, grad_x) = one_microbatch(per_example, given["loss_target"])
    else:
        def body(carry, xs):
            loss_sum, grad_sum = carry
            l_k, (gw_k, gx_k) = one_microbatch(xs[0], xs[1])
            with _jax.named_scope("update"):
                return (loss_sum + l_k, _jax.tree.map(_jnp.add, grad_sum, gw_k)), gx_k

        init = (_jnp.zeros((), _jnp.float32), _jax.tree.map(_jnp.zeros_like, weights))
        (loss, grad_w), grad_x = _jax.lax.scan(body, init, (per_example, given["loss_target"]))
    with _jax.named_scope("update"):
        delta_w, new_m, new_v = {}, {}, {}
        for n in TWIN_WEIGHTS:
            delta_w[n], new_m[n], new_v[n] = _adamw(weights[n], grad_w[n], given["m_" + n], given["v_" + n])
    return (loss, grad_x, *[grad_w[n] for n in TWIN_WEIGHTS], *[delta_w[n] for n in TWIN_WEIGHTS],
            *[new_m[n] for n in TWIN_WEIGHTS], *[new_v[n] for n in TWIN_WEIGHTS])
```

```python
import functools

import jax
import jax.numpy as jnp
from jax import lax
from jax.experimental import pallas as pl
from jax.experimental.pallas import tpu as pltpu

F32 = jnp.float32
BF16 = jnp.bfloat16
MESH = pl.DeviceIdType.MESH
ANY = pl.BlockSpec(memory_space=pl.ANY)

EPS = 1e-6
FFN_RESIDUAL = 0.5
D_GROUP = 256
CONF_KERNEL = 31
SHORT_KERNEL = 3
POOL_GROUP = 64
CHUNK = 128
N_HEADS = 4
HEAD_DIM = 64
HALO = 32
N_CHIPS = 4
N_DEV = 8
LANES = 128
MXU_TILE = 256
VMEM_LIMIT = 56 * 2**20
T_MIX_FWD = 512
T_MIX_BWD = 256

ADAM_LR = 0.001
ADAM_B1 = 0.9
ADAM_B2 = 0.999
ADAM_EPS = 1e-08
ADAM_WD = 0.01
ADAM_STEP = 10

NT = (((1,), (1,)), ((), ()))


def _params(*sem):
    return pltpu.CompilerParams(dimension_semantics=sem, vmem_limit_bytes=VMEM_LIMIT)


def _dot(a, b):
    return jnp.dot(a, b, preferred_element_type=F32)


def _dot_nt(a, b):
    return lax.dot_general(a, b, NT, preferred_element_type=F32)


def _t_bf16(v):
    return jnp.transpose(v).astype(BF16)


def _sigmoid(v):
    return 1.0 / (1.0 + jnp.exp(-v))


def _rms(x, g):
    r = lax.rsqrt(jnp.mean(x * x, axis=-1, keepdims=True) + EPS)
    n = x * r
    return n, r, n * g


def _rms_bwd(dh, n, r, g):
    dn = dh * g
    dx = r * (dn - n * jnp.mean(dn * n, axis=-1, keepdims=True))
    return dx, jnp.sum(dh * n, axis=0, keepdims=True)


def _ln_fwd(z, g, b):
    mu = jnp.mean(z, axis=-1, keepdims=True)
    zc = z - mu
    rs = lax.rsqrt(jnp.mean(zc * zc, axis=-1, keepdims=True) + EPS)
    zn = zc * rs
    return zn, rs, zn * g + b


def _ln_bwd(dl, zn, rs, g):
    dzn = dl * g
    dz = rs * (dzn - jnp.mean(dzn, axis=-1, keepdims=True) - zn * jnp.mean(dzn * zn, axis=-1, keepdims=True))
    return dz, jnp.sum(dl * zn, axis=0, keepdims=True), jnp.sum(dl, axis=0, keepdims=True)


def _tile(n, want):
    return want if n % want == 0 else n


def _ffn_fwd(x, g, w1t, w3t, w2):
    s, d = x.shape
    f = w1t.shape[0]
    tm, tf = _tile(s, 1024), MXU_TILE
    nj = f // tf

    def body(x_ref, g_ref, w1_ref, w3_ref, w2_ref, xo_ref, a_ref, b_ref, h_s, acc_s):
        j = pl.program_id(1)

        @pl.when(j == 0)
        def _():
            h_s[...] = _rms(x_ref[...], g_ref[...])[2].astype(BF16)
            acc_s[...] = jnp.zeros_like(acc_s)

        h = h_s[...]
        a = _dot_nt(h, w1_ref[...])
        b = _dot_nt(h, w3_ref[...])
        a_ref[...] = a.astype(BF16)
        b_ref[...] = b.astype(BF16)
        u = (a * _sigmoid(a)) * b
        acc_s[...] += _dot(u.astype(BF16), w2_ref[...])

        @pl.when(j == nj - 1)
        def _():
            xo_ref[...] = x_ref[...] + FFN_RESIDUAL * acc_s[...]

    wspec = pl.BlockSpec((tf, d), lambda i, j: (j, 0))
    return pl.pallas_call(
        body, name="ffn_fwd", grid=(s // tm, nj),
        in_specs=[pl.BlockSpec((tm, d), lambda i, j: (i, 0)), pl.BlockSpec((1, d), lambda i, j: (0, 0)), wspec, wspec, wspec],
        out_specs=[pl.BlockSpec((tm, d), lambda i, j: (i, 0)), pl.BlockSpec((tm, tf), lambda i, j: (i, j)),
                   pl.BlockSpec((tm, tf), lambda i, j: (i, j))],
        out_shape=[jax.ShapeDtypeStruct((s, d), F32), jax.ShapeDtypeStruct((s, f), BF16), jax.ShapeDtypeStruct((s, f), BF16)],
        scratch_shapes=[pltpu.VMEM((tm, d), BF16), pltpu.VMEM((tm, d), F32)],
        compiler_params=_params("arbitrary", "arbitrary"),
    )(x, g, w1t, w3t, w2)


def _ffn_bwd(dxo, x, g, a, b, w1t, w3t, w2):
    s, d = x.shape
    f = w1t.shape[0]
    tm, tf = _tile(s, 512), MXU_TILE
    ni, nj = s // tm, f // tf

    def body(dxo_ref, x_ref, g_ref, a_ref, b_ref, w1_ref, w3_ref, w2_ref,
             dx_ref, dg_ref, h_ref, dy_ref, dat_ref, dbt_ref, ut_ref, n_s, r_s, acc_s):
        i, j = pl.program_id(0), pl.program_id(1)

        @pl.when(j == 0)
        def _():
            n, r, h = _rms(x_ref[...], g_ref[...])
            n_s[...] = n
            r_s[...] = jnp.broadcast_to(r, r_s.shape)
            h_ref[...] = h.astype(BF16)
            dy_ref[...] = (FFN_RESIDUAL * dxo_ref[...]).astype(BF16)
            acc_s[...] = jnp.zeros_like(acc_s)

        @pl.when((i == 0) & (j == 0))
        def _():
            dg_ref[...] = jnp.zeros_like(dg_ref)

        du = _dot_nt(dy_ref[...], w2_ref[...])
        av = a_ref[...].astype(F32)
        bv = b_ref[...].astype(F32)
        sg = _sigmoid(av)
        sl = av * sg
        da = du * bv * (sg * (1.0 + av * (1.0 - sg)))
        db = du * sl
        dat_ref[...] = _t_bf16(da)
        dbt_ref[...] = _t_bf16(db)
        ut_ref[...] = _t_bf16(sl * bv)
        acc_s[...] += _dot(da.astype(BF16), w1_ref[...]) + _dot(db.astype(BF16), w3_ref[...])

        @pl.when(j == nj - 1)
        def _():
            dxr, dg = _rms_bwd(acc_s[...], n_s[...], r_s[:, 0:1], g_ref[...])
            dx_ref[...] = dxo_ref[...] + dxr
            dg_ref[0:1, :] += dg

    row = pl.BlockSpec((tm, d), lambda i, j: (i, 0))
    hid = pl.BlockSpec((tm, tf), lambda i, j: (i, j))
    hid_t = pl.BlockSpec((tf, tm), lambda i, j: (j, i))
    wspec = pl.BlockSpec((tf, d), lambda i, j: (j, 0))
    return pl.pallas_call(
        body, name="ffn_bwd", grid=(ni, nj),
        in_specs=[row, row, pl.BlockSpec((1, d), lambda i, j: (0, 0)), hid, hid, wspec, wspec, wspec],
        out_specs=[row, pl.BlockSpec((8, d), lambda i, j: (0, 0)), row, row, hid_t, hid_t, hid_t],
        out_shape=[jax.ShapeDtypeStruct((s, d), F32), jax.ShapeDtypeStruct((8, d), F32),
                   jax.ShapeDtypeStruct((s, d), BF16), jax.ShapeDtypeStruct((s, d), BF16),
                   jax.ShapeDtypeStruct((f, s), BF16), jax.ShapeDtypeStruct((f, s), BF16), jax.ShapeDtypeStruct((f, s), BF16)],
        scratch_shapes=[pltpu.VMEM((tm, d), F32), pltpu.VMEM((tm, LANES), F32), pltpu.VMEM((tm, d), F32)],
        compiler_params=_params("arbitrary", "arbitrary"),
    )(dxo, x, g, a, b, w1t, w3t, w2)


def _dw(at, bm):
    r, s = at.shape
    n = bm.shape[1]
    rb = r // N_CHIPS
    ts = _tile(s, 1024)
    ni = s // ts

    def body(at_ref, bm_ref, o_ref, acc_s):
        i = pl.program_id(1)

        @pl.when(i == 0)
        def _():
            acc_s[...] = jnp.zeros_like(acc_s)

        acc_s[...] += _dot(at_ref[...], bm_ref[...])

        @pl.when(i == ni - 1)
        def _():
            o_ref[...] = acc_s[...].astype(BF16)

    return pl.pallas_call(
        body, name="dw", grid=(r // rb, ni),
        in_specs=[pl.BlockSpec((rb, ts), lambda k, i: (k, i)), pl.BlockSpec((ts, n), lambda k, i: (i, 0))],
        out_specs=pl.BlockSpec((rb, n), lambda k, i: (k, 0)),
        out_shape=jax.ShapeDtypeStruct((r, n), BF16),
        scratch_shapes=[pltpu.VMEM((rb, n), F32)],
        compiler_params=_params("arbitrary", "arbitrary"),
    )(at, bm)


def _proj_fwd(x, g, w_int):
    s, d = x.shape
    f = w_int.shape[0]
    tm, tf = _tile(s, 1024), 512
    nj = f // tf

    def body(x_ref, g_ref, w_ref, p_ref, h_s):
        @pl.when(pl.program_id(1) == 0)
        def _():
            h_s[...] = _rms(x_ref[...], g_ref[...])[2].astype(BF16)

        p_ref[...] = _dot_nt(h_s[...], w_ref[...])

    return pl.pallas_call(
        body, name="proj_fwd", grid=(s // tm, nj),
        in_specs=[pl.BlockSpec((tm, d), lambda i, j: (i, 0)), pl.BlockSpec((1, d), lambda i, j: (0, 0)),
                  pl.BlockSpec((tf, d), lambda i, j: (j, 0))],
        out_specs=pl.BlockSpec((tm, tf), lambda i, j: (i, j)),
        out_shape=jax.ShapeDtypeStruct((s, f), F32),
        scratch_shapes=[pltpu.VMEM((tm, d), BF16)],
        compiler_params=_params("arbitrary", "arbitrary"),
    )(x, g, w_int)


def _proj_bwd(dxo, x, g, dp, w_int):
    s, d = x.shape
    f = w_int.shape[0]
    tm = _tile(s, 512)

    def body(dxo_ref, x_ref, g_ref, dp_ref, w_ref, dx_ref, dg_ref, h_ref, dpt_ref):
        @pl.when(pl.program_id(0) == 0)
        def _():
            dg_ref[...] = jnp.zeros_like(dg_ref)

        n, r, h = _rms(x_ref[...], g_ref[...])
        h_ref[...] = h.astype(BF16)
        dpv = dp_ref[...]
        dpt_ref[...] = _t_bf16(dpv.astype(F32))
        dxr, dg = _rms_bwd(_dot(dpv, w_ref[...]), n, r, g_ref[...])
        dx_ref[...] = dxo_ref[...] + dxr
        dg_ref[0:1, :] += dg

    row = pl.BlockSpec((tm, d), lambda i: (i, 0))
    return pl.pallas_call(
        body, name="proj_bwd", grid=(s // tm,),
        in_specs=[row, row, pl.BlockSpec((1, d), lambda i: (0, 0)), pl.BlockSpec((tm, f), lambda i: (i, 0)),
                  pl.BlockSpec((f, d), lambda i: (0, 0))],
        out_specs=[row, pl.BlockSpec((8, d), lambda i: (0, 0)), row, pl.BlockSpec((f, tm), lambda i: (0, i))],
        out_shape=[jax.ShapeDtypeStruct((s, d), F32), jax.ShapeDtypeStruct((8, d), F32),
                   jax.ShapeDtypeStruct((s, d), BF16), jax.ShapeDtypeStruct((f, s), BF16)],
        compiler_params=_params("arbitrary"),
    )(dxo, x, g, dp, w_int)


C = D_GROUP


def _piece(ref, k):
    return ref[:, k * C:(k + 1) * C]


def _up(v, r):
    return v if r == 0 else pltpu.roll(v, v.shape[0] - r, 0)


def _down(v, r):
    return v if r == 0 else pltpu.roll(v, r, 0)


def _lane_group():
    lane = lax.broadcasted_iota(jnp.int32, (1, C), 1)
    return (lane >= POOL_GROUP).astype(jnp.int32) + (lane >= 2 * POOL_GROUP).astype(jnp.int32) + (
        lane >= 3 * POOL_GROUP).astype(jnp.int32)


def _by_group(grp, v2, v4, v8, v16):
    return jnp.where(grp == 0, v2, jnp.where(grp == 1, v4, jnp.where(grp == 2, v8, v16)))


def _pool_count(grp, row0, t):
    pos = (row0 + lax.broadcasted_iota(jnp.int32, (t, C), 0) + 1).astype(F32)
    return jnp.minimum(pos, _by_group(grp, 2.0, 4.0, 8.0, 16.0))


def _trailing_sums(ext, grp, t):
    s2 = ext + _down(ext, 1)
    s4 = s2 + _down(s2, 2)
    s8 = s4 + _down(s4, 4)
    s16 = s8 + _down(s8, 8)
    return _by_group(grp, s2, s4, s8, s16)[HALO:HALO + t]


def _leading_sums(ext, grp, t):
    s2 = ext + _up(ext, 1)
    s4 = s2 + _up(s2, 2)
    s8 = s4 + _up(s4, 4)
    s16 = s8 + _up(s8, 8)
    return _by_group(grp, s2, s4, s8, s16)[0:t]


def _head_select(r4, grp):
    out = jnp.where(grp == 0, r4[0:CHUNK], 0.0)
    for h in range(1, N_HEADS):
        out = out + jnp.where(grp == h, r4[h * CHUNK:(h + 1) * CHUNK], 0.0)
    return out


def _conv_taps():
    return [(k, (k + 2) % 8, (k + 2) - (k + 2) % 8) for k in range(CONF_KERNEL)]


def _mixer_fwd(p, x1, cw, vec, pool_w, wstack, bias, w_out):
    s, d = x1.shape
    t = _tile(s, T_MIX_FWD)
    n_ext = t + HALO
    dm = w_out.shape[0]

    def body(p_ref, x1_ref, cw_ref, vec_ref, pw_ref, ws_ref, bias_ref, wo_ref, x2_ref, mt_ref, cy_s, cq_s, cx_s, mix_s):
        i = pl.program_id(0)

        @pl.when(i == 0)
        def _():
            cy_s[...] = jnp.zeros_like(cy_s)
            cq_s[...] = jnp.zeros_like(cq_s)
            cx_s[...] = jnp.zeros_like(cx_s)

        grp = _lane_group()
        y = _piece(p_ref, 0) * _sigmoid(_piece(p_ref, 1))
        ext = jnp.concatenate([cy_s[...], y], axis=0)
        cy_s[...] = y[t - HALO:t]
        z = jnp.broadcast_to(vec_ref[0:1, :], (t, C))
        shifted = {}
        for k, r, off in _conv_taps():
            if r not in shifted:
                shifted[r] = _up(ext, r)
            z = z + cw_ref[k:k + 1, :] * shifted[r][off:off + t]
        ln = _ln_fwd(z, vec_ref[1:2, :], vec_ref[2:3, :])[2]
        mix_s[:, 0:C] = (ln * _sigmoid(ln)).astype(BF16)
        q = _piece(p_ref, 3) * _piece(p_ref, 4)
        ext = jnp.concatenate([cq_s[...], q], axis=0)
        cq_s[...] = q[t - HALO:t]
        cz = vec_ref[8:9, :] * q + vec_ref[7:8, :] * _down(ext, 1)[HALO:] + vec_ref[6:7, :] * _down(ext, 2)[HALO:]
        mix_s[:, C:2 * C] = (_piece(p_ref, 2) * cz).astype(BF16)
        xp = _piece(p_ref, 5)
        ext = jnp.concatenate([cx_s[...], xp], axis=0)
        cx_s[...] = xp[t - HALO:t]
        dd = _trailing_sums(ext, grp, t) / _pool_count(grp, i * t, t) - xp
        mix_s[:, 2 * C:3 * C] = (_dot(dd.astype(BF16), pw_ref[...]) * vec_ref[3:4, :]).astype(BF16)
        vln = _ln_fwd(_piece(p_ref, 7), vec_ref[4:5, :], vec_ref[5:6, :])[2].astype(BF16)
        for n in range(t // CHUNK):
            rows = slice(n * CHUNK, (n + 1) * CHUNK)
            mixed = _head_select(_dot(ws_ref[...], vln[rows]), grp) + bias_ref[...]
            mix_s[rows, 3 * C:4 * C] = (p_ref[rows, 6 * C:7 * C] * mixed).astype(BF16)
        mix = mix_s[...]
        x2_ref[...] = x1_ref[...] + _dot(mix, wo_ref[...])
        mt_ref[...] = _t_bf16(mix.astype(F32))

    full = lambda a: pl.BlockSpec(a.shape, lambda i: (0, 0))
    return pl.pallas_call(
        body, name="mixer_fwd", grid=(s // t,),
        in_specs=[pl.BlockSpec((t, p.shape[1]), lambda i: (i, 0)), pl.BlockSpec((t, d), lambda i: (i, 0)),
                  full(cw), full(vec), full(pool_w), full(wstack), full(bias), full(w_out)],
        out_specs=[pl.BlockSpec((t, d), lambda i: (i, 0)), pl.BlockSpec((dm, t), lambda i: (0, i))],
        out_shape=[jax.ShapeDtypeStruct((s, d), F32), jax.ShapeDtypeStruct((dm, s), BF16)],
        scratch_shapes=[pltpu.VMEM((HALO, C), F32)] * 3 + [pltpu.VMEM((t, dm), BF16)],
        compiler_params=_params("arbitrary"),
    )(p, x1, cw, vec, pool_w, wstack, bias, w_out)


def _mixer_bwd(dx2, p, cw, vec, pool_w, wstack, wstack_t, bias, tril4, head_rows, w_out):
    s, d = dx2.shape
    t = _tile(s, T_MIX_BWD)
    nt = s // t
    n_ext = t + HALO
    hb = t // HALO

    def body(dx2_ref, p_ref, ph_ref, cw_ref, vec_ref, pw_ref, ws_ref, wst_ref, bias_ref, tril_ref, hr_ref, wo_ref,
             dp_ref, dcw_ref, dvec_ref, dpool_ref, dws_ref, dbs_ref, cdz_s, cdc_s, cf_s, vy_s, dvl_s, dbias_s):
        i = pl.program_id(0)
        tile = nt - 1 - i

        @pl.when(i == 0)
        def _():
            for ref in (cdz_s, cdc_s, cf_s, dbias_s, dcw_ref, dvec_ref, dpool_ref, dws_ref, dbs_ref):
                ref[...] = jnp.zeros_like(ref)

        grp = _lane_group()
        first = jnp.where(tile > 0, 1.0, 0.0)
        dmix = _dot_nt(dx2_ref[...].astype(BF16), wo_ref[...])
        d_a, d_b, d_c, d_d = (dmix[:, k * C:(k + 1) * C] for k in range(4))

        def acc_vec(row, v):
            dvec_ref[row:row + 1, :] += jnp.sum(v, axis=0, keepdims=True)

        val, gate = _piece(p_ref, 0), _piece(p_ref, 1)
        sgate = _sigmoid(gate)
        y = val * sgate
        y_halo = ph_ref[:, 0:C] * _sigmoid(ph_ref[:, C:2 * C]) * first
        ext = jnp.concatenate([y_halo, y], axis=0)
        for r in range(8):
            vy_s[r] = _up(ext, r)
        z = jnp.broadcast_to(vec_ref[0:1, :], (t, C))
        for k, r, off in _conv_taps():
            z = z + cw_ref[k:k + 1, :] * vy_s[r, off:off + t, :]
        zn, rs, ln = _ln_fwd(z, vec_ref[1:2, :], vec_ref[2:3, :])
        sg = _sigmoid(ln)
        dln = d_a * (sg * (1.0 + ln * (1.0 - sg)))
        dz, dg, db = _ln_bwd(dln, zn, rs, vec_ref[1:2, :])
        dvec_ref[1:2, :] += dg
        dvec_ref[2:3, :] += db
        acc_vec(0, dz)
        for k, r, off in _conv_taps():
            dcw_ref[k:k + 1, :] += jnp.sum(dz * vy_s[r, off:off + t, :], axis=0, keepdims=True)
        ext = jnp.concatenate([dz, cdz_s[...]], axis=0)
        cdz_s[...] = dz[0:HALO]
        dy = jnp.zeros((t, C), F32)
        shifted = {}
        for k in range(CONF_KERNEL):
            m = CONF_KERNEL - 1 - k
            r, off = m % 8, m - m % 8
            if r not in shifted:
                shifted[r] = _up(ext, r)
            dy = dy + cw_ref[k:k + 1, :] * shifted[r][off:off + t]
        dp_ref[:, 0:C] = (dy * sgate).astype(BF16)
        dp_ref[:, C:2 * C] = (dy * val * sgate * (1.0 - sgate)).astype(BF16)

        sb, sc, sx = _piece(p_ref, 2), _piece(p_ref, 3), _piece(p_ref, 4)
        q = sc * sx
        q_halo = ph_ref[:, 3 * C:4 * C] * ph_ref[:, 4 * C:5 * C] * first
        ext = jnp.concatenate([q_halo, q], axis=0)
        q1, q2 = _down(ext, 1)[HALO:], _down(ext, 2)[HALO:]
        cz = vec_ref[8:9, :] * q + vec_ref[7:8, :] * q1 + vec_ref[6:7, :] * q2
        dcz = d_b * sb
        dp_ref[:, 2 * C:3 * C] = (d_b * cz).astype(BF16)
        acc_vec(8, dcz * q)
        acc_vec(7, dcz * q1)
        acc_vec(6, dcz * q2)
        ext = jnp.concatenate([dcz, cdc_s[...]], axis=0)
        cdc_s[...] = dcz[0:HALO]
        dq = vec_ref[8:9, :] * dcz + vec_ref[7:8, :] * _up(ext, 1)[0:t] + vec_ref[6:7, :] * _up(ext, 2)[0:t]
        dp_ref[:, 3 * C:4 * C] = (dq * sx).astype(BF16)
        dp_ref[:, 4 * C:5 * C] = (dq * sc).astype(BF16)

        xp = _piece(p_ref, 5)
        ext = jnp.concatenate([ph_ref[:, 5 * C:6 * C] * first, xp], axis=0)
        cnt = _pool_count(grp, tile * t, t)
        dd = (_trailing_sums(ext, grp, t) / cnt - xp).astype(BF16)
        e2 = _dot(dd, pw_ref[...])
        acc_vec(3, d_c * e2)
        de = (d_c * vec_ref[3:4, :]).astype(BF16)
        dpool_ref[...] += _dot(_t_bf16(dd.astype(F32)), de)
        ddd = _dot_nt(de, pw_ref[...])
        fq = ddd / cnt
        ext = jnp.concatenate([fq, cf_s[...]], axis=0)
        cf_s[...] = fq[0:HALO]
        dp_ref[:, 5 * C:6 * C] = (_leading_sums(ext, grp, t) - ddd).astype(BF16)

        vn, vrs, vlnf = _ln_fwd(_piece(p_ref, 7), vec_ref[4:5, :], vec_ref[5:6, :])
        vln = vlnf.astype(BF16)
        for n in range(t // CHUNK):
            rows = slice(n * CHUNK, (n + 1) * CHUNK)
            mixed = _head_select(_dot(ws_ref[...], vln[rows]), grp) + bias_ref[...]
            dd_n = d_d[rows]
            dp_ref[rows, 6 * C:7 * C] = (dd_n * mixed).astype(BF16)
            dmx = dd_n * p_ref[rows, 6 * C:7 * C]
            dbias_s[...] += dmx
            dmx_b = dmx.astype(BF16)
            dvl_s[rows, :] = _head_select(_dot(wst_ref[...], dmx_b), grp)
            for h in range(N_HEADS):
                hrows = slice(h * CHUNK, (h + 1) * CHUNK)
                dws_ref[hrows, :] += _dot_nt(jnp.where(grp == h, dmx_b, jnp.zeros_like(dmx_b)), vln[rows])
        dvl = dvl_s[...]
        dv, dg, db = _ln_bwd(dvl, vn, vrs, vec_ref[4:5, :])
        dvec_ref[4:5, :] += dg
        dvec_ref[5:6, :] += db
        dp_ref[:, 7 * C:8 * C] = dv.astype(BF16)

        @pl.when(i == nt - 1)
        def _():
            dws_ref[...] = dws_ref[...] * tril_ref[...]
            dbs_ref[...] = lax.dot_general(hr_ref[...], dbias_s[...], NT, precision=lax.Precision.HIGHEST,
                                           preferred_element_type=F32)

    full = lambda a: pl.BlockSpec(a.shape, lambda i: (0, 0))
    acc = lambda shape: pl.BlockSpec(shape, lambda i: (0, 0))
    f = p.shape[1]
    return pl.pallas_call(
        body, name="mixer_bwd", grid=(nt,),
        in_specs=[pl.BlockSpec((t, d), lambda i: (nt - 1 - i, 0)), pl.BlockSpec((t, f), lambda i: (nt - 1 - i, 0)),
                  pl.BlockSpec((HALO, f), lambda i: (jnp.maximum((nt - 1 - i) * hb - 1, 0), 0)),
                  full(cw), full(vec), full(pool_w), full(wstack), full(wstack_t), full(bias), full(tril4), full(head_rows),
                  full(w_out)],
        out_specs=[pl.BlockSpec((t, f), lambda i: (nt - 1 - i, 0)), acc((32, C)), acc((16, C)), acc((C, C)),
                   acc((N_HEADS * CHUNK, CHUNK)), acc((8, CHUNK))],
        out_shape=[jax.ShapeDtypeStruct((s, f), BF16), jax.ShapeDtypeStruct((32, C), F32), jax.ShapeDtypeStruct((16, C), F32),
                   jax.ShapeDtypeStruct((C, C), F32), jax.ShapeDtypeStruct((N_HEADS * CHUNK, CHUNK), F32),
                   jax.ShapeDtypeStruct((8, CHUNK), F32)],
        scratch_shapes=[pltpu.VMEM((HALO, C), F32)] * 3 + [pltpu.VMEM((8, n_ext, C), F32), pltpu.VMEM((t, C), F32),
                                                            pltpu.VMEM((CHUNK, C), F32)],
        compiler_params=_params("arbitrary"),
    )(dx2, p, p, cw, vec, pool_w, wstack, wstack_t, bias, tril4, head_rows, w_out)


def _loss_bwd(x, g, target):
    s, d = x.shape
    tm = _tile(s, 512)

    def body(x_ref, g_ref, t_ref, dx_ref, dg_ref, loss_ref):
        @pl.when(pl.program_id(0) == 0)
        def _():
            dg_ref[...] = jnp.zeros_like(dg_ref)
            loss_ref[...] = jnp.zeros_like(loss_ref)

        n, r, y = _rms(x_ref[...], g_ref[...])
        err = y - t_ref[...]
        loss_ref[...] += 0.5 * jnp.sum(jnp.mean(err * err, axis=-1, keepdims=True), axis=0, keepdims=True)
        dxr, dg = _rms_bwd(err * (1.0 / d), n, r, g_ref[...])
        dx_ref[...] = dxr
        dg_ref[0:1, :] += dg

    row = pl.BlockSpec((tm, d), lambda i: (i, 0))
    return pl.pallas_call(
        body, name="loss_bwd", grid=(s // tm,),
        in_specs=[row, pl.BlockSpec((1, d), lambda i: (0, 0)), row],
        out_specs=[row, pl.BlockSpec((8, d), lambda i: (0, 0)), pl.BlockSpec((8, LANES), lambda i: (0, 0))],
        out_shape=[jax.ShapeDtypeStruct((s, d), F32), jax.ShapeDtypeStruct((8, d), F32), jax.ShapeDtypeStruct((8, LANES), F32)],
        compiler_params=_params("arbitrary"),
    )(x, g, target)


def _adamw_math(w, g, m, v):
    m = ADAM_B1 * m + (1.0 - ADAM_B1) * g
    v = ADAM_B2 * v + (1.0 - ADAM_B2) * (g * g)
    m_hat = m / (1.0 - ADAM_B1 ** ADAM_STEP)
    v_hat = v / (1.0 - ADAM_B2 ** ADAM_STEP)
    return -ADAM_LR * (m_hat / (jnp.sqrt(v_hat) + ADAM_EPS) + ADAM_WD * w), m, v


def _adamw(w, g, m, v):
    r, c = w.shape
    tr = r // 8 if r % 64 == 0 else r

    def body(w_ref, g_ref, m_ref, v_ref, d_ref, mo_ref, vo_ref):
        d_ref[...], mo_ref[...], vo_ref[...] = _adamw_math(w_ref[...], g_ref[...], m_ref[...], v_ref[...])

    blk = pl.BlockSpec((tr, c), lambda i: (i, 0))
    return pl.pallas_call(
        body, name="adamw", grid=(r // tr,), in_specs=[blk] * 4, out_specs=[blk] * 3,
        out_shape=[jax.ShapeDtypeStruct((r, c), F32)] * 3, compiler_params=_params("arbitrary"),
    )(w, g, m, v)


def _adamw_small(ws, gs, ms, vs):
    n = len(ws)

    def body(*refs):
        ins, outs = refs[:4 * n], refs[4 * n:]
        for k in range(n):
            dl, mo, vo = _adamw_math(ins[k][...], ins[n + k][...], ins[2 * n + k][...], ins[3 * n + k][...])
            outs[k][...], outs[n + k][...], outs[2 * n + k][...] = dl, mo, vo

    vm = pl.BlockSpec(memory_space=pltpu.VMEM)
    out = pl.pallas_call(
        body, name="adamw_small", in_specs=[vm] * (4 * n), out_specs=[vm] * (3 * n),
        out_shape=[jax.ShapeDtypeStruct(a.shape, F32) for a in ws] * 3,
        compiler_params=pltpu.CompilerParams(vmem_limit_bytes=VMEM_LIMIT),
    )(*ws, *gs, *ms, *vs)
    return out[:n], out[n:2 * n], out[2 * n:]


def _where_am_i():
    x, y, c = lax.axis_index("x"), lax.axis_index("y"), lax.axis_index("c")
    chips = [(1 - x, y), (x, 1 - y), (1 - x, 1 - y)]
    return x, y, c, chips


def _chip_id(chip):
    return 2 * chip[0] + chip[1]


def _gather_weights(shards):
    n = len(shards)

    def body(*refs):
        ins, outs = refs[:n], refs[n:2 * n]
        send_sems, recv_sems, local_sems = refs[2 * n:]
        x, y, c, chips = _where_am_i()
        me = _chip_id((x, y))
        sibling = (x, y, 1 - c)

        def half(a, who):
            hr = ins[a].shape[0] // 2
            return pl.ds(who * hr, hr)

        def copy(a, slot, src, dst, to):
            return pltpu.make_async_remote_copy(src_ref=src, dst_ref=dst, send_sem=send_sems.at[6 * a + slot],
                                                recv_sem=recv_sems.at[6 * a + slot], device_id=to, device_id_type=MESH)

        local = [pltpu.make_async_copy(ins[a], outs[a].at[me], local_sems.at[a]) for a in range(n)]
        for cp in local:
            cp.start()
        sent = []
        for a in range(n):
            for j, chip in enumerate(chips):
                sent.append(copy(a, j, ins[a].at[half(a, c)], outs[a].at[me, half(a, c)], (*chip, c)))
                sent[-1].start()
        for j, chip in enumerate(chips):
            for a in range(n):
                landed = outs[a].at[_chip_id(chip), half(a, c)]
                copy(a, j, landed, landed, (*chip, c)).wait_recv()
                sent.append(copy(a, 3 + j, landed, landed, sibling))
                sent[-1].start()
        for j, chip in enumerate(chips):
            for a in range(n):
                landed = outs[a].at[_chip_id(chip), half(a, 1 - c)]
                copy(a, 3 + j, landed, landed, sibling).wait_recv()
        for cp in sent:
            cp.wait_send()
        for cp in local:
            cp.wait()

    return pl.pallas_call(
        body, name="gather_weights", in_specs=[ANY] * n, out_specs=[ANY] * n,
        out_shape=[jax.ShapeDtypeStruct((N_CHIPS,) + a.shape, a.dtype) for a in shards],
        scratch_shapes=[pltpu.SemaphoreType.DMA((6 * n,)), pltpu.SemaphoreType.DMA((6 * n,)), pltpu.SemaphoreType.DMA((n,))],
    )(*shards)


def _reduce_pair(grads):
    n = len(grads)
    halves = [g.shape[1] // 2 for g in grads]
    total = sum(halves)
    d = grads[0].shape[2]

    def body(*refs):
        ins = refs[:n]
        mine_ref, theirs_ref, send_sems, recv_sems, local_sems = refs[n:]
        x, y, c, _ = _where_am_i()
        sibling = (x, y, 1 - c)
        copies, local = [], []
        off = 0
        for a in range(n):
            hr = halves[a]
            dst = pl.ds(off, hr)
            local.append(pltpu.make_async_copy(ins[a].at[:, pl.ds(c * hr, hr)], mine_ref.at[:, dst], local_sems.at[a]))
            copies.append(pltpu.make_async_remote_copy(
                src_ref=ins[a].at[:, pl.ds((1 - c) * hr, hr)], dst_ref=theirs_ref.at[:, dst],
                send_sem=send_sems.at[a], recv_sem=recv_sems.at[a], device_id=sibling, device_id_type=MESH))
            off += hr
        for cp in local + copies:
            cp.start()
        for cp in copies:
            cp.wait_recv()
        for cp in copies:
            cp.wait_send()
        for cp in local:
            cp.wait()

    shape = jax.ShapeDtypeStruct((N_CHIPS, total, d), BF16)
    return pl.pallas_call(
        body, name="reduce_pair", in_specs=[ANY] * n, out_specs=[ANY, ANY], out_shape=[shape, shape],
        scratch_shapes=[pltpu.SemaphoreType.DMA((n,))] * 3,
    )(*grads)


def _add_pair(mine, theirs):
    k, r, d = mine.shape
    tr = r // 3 if r % 48 == 0 else r

    def body(a_ref, b_ref, o_ref):
        o_ref[...] = (a_ref[...].astype(F32) + b_ref[...].astype(F32)).astype(BF16)

    blk = pl.BlockSpec((1, tr, d), lambda i, j: (i, j, 0))
    return pl.pallas_call(
        body, name="add_pair", grid=(k, r // tr), in_specs=[blk, blk], out_specs=blk,
        out_shape=jax.ShapeDtypeStruct(mine.shape, BF16), compiler_params=_params("arbitrary", "arbitrary"),
    )(mine, theirs)


def _reduce_chips(sums):
    def body(s_ref, o_ref, send_sems, recv_sems, local_sem):
        x, y, c, chips = _where_am_i()
        me = _chip_id((x, y))
        local = pltpu.make_async_copy(s_ref.at[me], o_ref.at[3], local_sem)
        local.start()
        copies = [pltpu.make_async_remote_copy(src_ref=s_ref.at[_chip_id(chip)], dst_ref=o_ref.at[j], send_sem=send_sems.at[j],
                                               recv_sem=recv_sems.at[j], device_id=(*chip, c), device_id_type=MESH)
                  for j, chip in enumerate(chips)]
        for cp in copies:
            cp.start()
        for cp in copies:
            cp.wait_recv()
        for cp in copies:
            cp.wait_send()
        local.wait()

    return pl.pallas_call(
        body, name="reduce_chips", in_specs=[ANY], out_specs=ANY, out_shape=jax.ShapeDtypeStruct(sums.shape, BF16),
        scratch_shapes=[pltpu.SemaphoreType.DMA((3,)), pltpu.SemaphoreType.DMA((3,)), pltpu.SemaphoreType.DMA],
    )(sums)


def _add_chips(parts):
    k, r, d = parts.shape
    tr = r // 3 if r % 48 == 0 else r

    def body(p_ref, o_ref):
        o_ref[...] = ((p_ref[3].astype(F32) + p_ref[0].astype(F32)) + p_ref[1].astype(F32)) + p_ref[2].astype(F32)

    return pl.pallas_call(
        body, name="add_chips", grid=(r // tr,), in_specs=[pl.BlockSpec((k, tr, d), lambda i: (0, i, 0))],
        out_specs=pl.BlockSpec((tr, d), lambda i: (i, 0)), out_shape=jax.ShapeDtypeStruct((r, d), F32),
        compiler_params=_params("arbitrary"),
    )(parts)


def _share_halves(per_layer, rows):
    n_l, n = len(per_layer), len(rows)
    d = per_layer[0].shape[1]

    def body(*refs):
        ins, outs = refs[:n_l], refs[n_l:n_l + n]
        send_sems, recv_sems, local_sems = refs[n_l + n:]
        x, y, c, _ = _where_am_i()
        sibling = (x, y, 1 - c)
        copies, local = [], []
        for l in range(n_l):
            off = 0
            for a in range(n):
                hr = rows[a] // 2
                src = ins[l].at[pl.ds(off, hr)]
                local.append(pltpu.make_async_copy(src, outs[a].at[l, pl.ds(c * hr, hr)], local_sems.at[l * n + a]))
                copies.append(pltpu.make_async_remote_copy(
                    src_ref=src, dst_ref=outs[a].at[l, pl.ds(c * hr, hr)], send_sem=send_sems.at[l * n + a],
                    recv_sem=recv_sems.at[l * n + a], device_id=sibling, device_id_type=MESH))
                off += hr
        for cp in local + copies:
            cp.start()
        for l in range(n_l):
            for a in range(n):
                hr = rows[a] // 2
                theirs = outs[a].at[l, pl.ds((1 - c) * hr, hr)]
                pltpu.make_async_remote_copy(src_ref=theirs, dst_ref=theirs, send_sem=send_sems.at[l * n + a],
                                             recv_sem=recv_sems.at[l * n + a], device_id=sibling, device_id_type=MESH).wait_recv()
        for cp in copies:
            cp.wait_send()
        for cp in local:
            cp.wait()

    return pl.pallas_call(
        body, name="share_halves", in_specs=[ANY] * n_l, out_specs=[ANY] * n,
        out_shape=[jax.ShapeDtypeStruct((n_l, r, d), F32) for r in rows],
        scratch_shapes=[pltpu.SemaphoreType.DMA((n_l * n,))] * 3,
    )(*per_layer)


def _all_gather_small(block, reduce):
    m, n = block.shape

    def body(x_ref, out_ref, *scratch):
        if reduce:
            all_ref, send_sems, recv_sems, local_sem = scratch
        else:
            all_ref = out_ref
            send_sems, recv_sems, local_sem = scratch
        x, y, c, chips = _where_am_i()
        me, sibling = (x, y, c), (x, y, 1 - c)

        def rows(px, py, pc):
            return all_ref.at[pl.ds((4 * px + 2 * py + pc) * m, m), :]

        def copy(k, blk, to, src=None):
            return pltpu.make_async_remote_copy(src_ref=rows(*blk) if src is None else src, dst_ref=rows(*blk),
                                                send_sem=send_sems.at[k], recv_sem=recv_sems.at[k], device_id=to,
                                                device_id_type=MESH)

        mine = pltpu.make_async_copy(x_ref, rows(*me), local_sem)
        mine.start()
        first = [copy(0, me, sibling, src=x_ref)]
        first += [copy(1 + j, me, (*chip, c), src=x_ref) for j, chip in enumerate(chips)]
        for cp in first:
            cp.start()
        passed = [copy(4 + j, (*chip, c), sibling) for j, chip in enumerate(chips)]
        for j, chip in enumerate(chips):
            copy(1 + j, (*chip, c), me).wait_recv()
            passed[j].start()
        copy(0, sibling, me).wait_recv()
        for j, chip in enumerate(chips):
            copy(4 + j, (*chip, 1 - c), me).wait_recv()
        for cp in first + passed:
            cp.wait_send()
        mine.wait()
        if reduce:
            total = all_ref[0:m, :]
            for dev in range(1, N_DEV):
                total = total + all_ref[dev * m:(dev + 1) * m, :]
            out_ref[...] = total

    vm = pl.BlockSpec(memory_space=pltpu.VMEM)
    sems = [pltpu.SemaphoreType.DMA((7,)), pltpu.SemaphoreType.DMA((7,)), pltpu.SemaphoreType.DMA]
    return pl.pallas_call(
        body, name="reduce_small" if reduce else "gather_small", in_specs=[vm], out_specs=vm,
        out_shape=jax.ShapeDtypeStruct((m, n) if reduce else (N_DEV * m, n), F32),
        scratch_shapes=([pltpu.VMEM((N_DEV * m, n), F32)] if reduce else []) + sems,
        compiler_params=pltpu.CompilerParams(vmem_limit_bytes=VMEM_LIMIT),
    )(block)


def _pack(arrays):
    flat = jnp.concatenate([a.reshape(-1) for a in arrays])
    pad = (-flat.shape[0]) % (8 * LANES)
    return jnp.pad(flat, (0, pad)).reshape(-1, LANES)


def _unpack(buf, shapes):
    flat = buf.reshape(-1)
    out, off = [], 0
    for shp in shapes:
        size = 1
        for dim in shp:
            size *= dim
        out.append(flat[off:off + size].reshape(shp))
        off += size
    return out


BIG = ("ffn1_w1", "ffn1_w3", "ffn1_w2", "w_in", "w_out", "ffn2_w1", "ffn2_w3", "ffn2_w2")
TRANSPOSED = ("ffn1_w1", "ffn1_w3", "w_in", "ffn2_w1", "ffn2_w3")
SMALL = ("ffn1_norm", "mix_norm", "conf_conv_w", "conf_conv_b", "conf_ln_g", "conf_ln_b", "sconv_w", "pool_w", "pool_scale",
         "gmlp_ln_g", "gmlp_ln_b", "gmlp_w_s", "gmlp_b_s", "ffn2_norm", "final_norm")
ORDER = ("ffn1_norm", "ffn1_w1", "ffn1_w3", "ffn1_w2", "mix_norm", "w_in", "conf_conv_w", "conf_conv_b", "conf_ln_g", "conf_ln_b",
         "sconv_w", "pool_w", "pool_scale", "gmlp_ln_g", "gmlp_ln_b", "gmlp_w_s", "gmlp_b_s", "w_out", "ffn2_norm", "ffn2_w1",
         "ffn2_w3", "ffn2_w2", "final_norm")


def _as2d(a):
    return a.reshape(-1, a.shape[-1])


def kernel(x, ffn1_norm, ffn1_w1, ffn1_w3, ffn1_w2, mix_norm, w_in, conf_conv_w, conf_conv_b, conf_ln_g, conf_ln_b, sconv_w, pool_w, pool_scale, gmlp_ln_g, gmlp_ln_b, gmlp_w_s, gmlp_b_s, w_out, ffn2_norm, ffn2_w1, ffn2_w3, ffn2_w2, final_norm, loss_target, m_ffn1_norm, m_ffn1_w1, m_ffn1_w3, m_ffn1_w2, m_mix_norm, m_w_in, m_conf_conv_w, m_conf_conv_b, m_conf_ln_g, m_conf_ln_b, m_sconv_w, m_pool_w, m_pool_scale, m_gmlp_ln_g, m_gmlp_ln_b, m_gmlp_w_s, m_gmlp_b_s, m_w_out, m_ffn2_norm, m_ffn2_w1, m_ffn2_w3, m_ffn2_w2, m_final_norm, v_ffn1_norm, v_ffn1_w1, v_ffn1_w3, v_ffn1_w2, v_mix_norm, v_w_in, v_conf_conv_w, v_conf_conv_b, v_conf_ln_g, v_conf_ln_b, v_sconv_w, v_pool_w, v_pool_scale, v_gmlp_ln_g, v_gmlp_ln_b, v_gmlp_w_s, v_gmlp_b_s, v_w_out, v_ffn2_norm, v_ffn2_w1, v_ffn2_w3, v_ffn2_w2, v_final_norm):
    given = dict(locals())
    w = {k: given[k] for k in ORDER}
    mom = {k: given["m_" + k] for k in ORDER}
    var = {k: given["v_" + k] for k in ORDER}
    n_l = ffn1_w1.shape[0]
    xs = x[0]
    d = xs.shape[1]
    chip = 2 * lax.axis_index("x") + lax.axis_index("y")

    def shard(name, l):
        a = w[name][l]
        return (a.T if name in TRANSPOSED else a).astype(BF16)

    gathered = []
    for l in range(n_l):
        out = _gather_weights([shard(name, l) for name in BIG])
        gathered.append({name: g.reshape(-1, d) for name, g in zip(BIG, out)})
    shard_rows = [w[name].shape[2] if name in TRANSPOSED else w[name].shape[1] for name in BIG]

    conv_shapes = [conf_conv_w.shape, sconv_w.shape]
    conv_all = _all_gather_small(_pack([conf_conv_w, sconv_w]), reduce=False)
    conv_all = conv_all.reshape(N_CHIPS, 2, -1)[:, 0]
    conf_full, sconv_full = [jnp.concatenate([_unpack(conv_all[k], conv_shapes)[a] for k in range(N_CHIPS)], axis=-1)
                             for a in range(2)]

    lane = jnp.arange(C) // HEAD_DIM
    head_rows = (jnp.arange(8)[:, None] == lane[None, :]).astype(F32)
    tril = jnp.tril(jnp.ones((CHUNK, CHUNK), F32))
    tril4 = jnp.tile(tril, (N_HEADS, 1))
    mixer_consts = []
    for l in range(n_l):
        cw = jnp.pad(conf_full[l], ((0, 32 - CONF_KERNEL), (0, 0)))
        vec = jnp.concatenate([conf_conv_b[l][None], conf_ln_g[l][None], conf_ln_b[l][None], pool_scale[l][None],
                               gmlp_ln_g[l][None], gmlp_ln_b[l][None], sconv_full[l], jnp.zeros((7, C), F32)], axis=0)
        eye = jnp.eye(len(pool_w[l]), dtype=F32)
        pool_blk = (eye[:, None, :, None] * pool_w[l][:, :, None, :]).reshape(C, C).astype(BF16)
        ws = gmlp_w_s[l] * tril[None]
        wstack = ws.reshape(N_HEADS * CHUNK, CHUNK).astype(BF16)
        wstack_t = jnp.swapaxes(ws, 1, 2).reshape(N_HEADS * CHUNK, CHUNK).astype(BF16)
        bias = jnp.repeat(gmlp_b_s[l].T, HEAD_DIM, axis=1)
        mixer_consts.append((cw, vec, pool_blk, wstack, wstack_t, bias))

    saved = []
    cur = xs
    for l in range(n_l):
        gw = gathered[l]
        cw, vec, pool_blk, wstack, wstack_t, bias = mixer_consts[l]
        x0 = cur
        x1, a1, b1 = _ffn_fwd(x0, ffn1_norm[l][None], gw["ffn1_w1"], gw["ffn1_w3"], gw["ffn1_w2"])
        p = _proj_fwd(x1, mix_norm[l][None], gw["w_in"])
        x2, mix_t = _mixer_fwd(p, x1, cw, vec, pool_blk, wstack, bias, gw["w_out"])
        x3, a2, b2 = _ffn_fwd(x2, ffn2_norm[l][None], gw["ffn2_w1"], gw["ffn2_w3"], gw["ffn2_w2"])
        saved.append((x0, x1, x2, a1, b1, a2, b2, p, mix_t))
        cur = x3

    dx, dg_final, loss_part = _loss_bwd(cur, final_norm[None], loss_target[0])

    small_parts = [None] * n_l
    reduced_halves = [None] * n_l
    for l in reversed(range(n_l)):
        gw = gathered[l]
        cw, vec, pool_blk, wstack, wstack_t, bias = mixer_consts[l]
        x0, x1, x2, a1, b1, a2, b2, p, mix_t = saved[l]
        big = {}
        dx, dg_ffn2, h, dy, da_t, db_t, u_t = _ffn_bwd(dx, x2, ffn2_norm[l][None], a2, b2, gw["ffn2_w1"], gw["ffn2_w3"], gw["ffn2_w2"])
        big["ffn2_w1"], big["ffn2_w3"], big["ffn2_w2"] = _dw(da_t, h), _dw(db_t, h), _dw(u_t, dy)
        big["w_out"] = _dw(mix_t, dx.astype(BF16))
        dp, dcw, dvec, dpool, dws, dbs = _mixer_bwd(dx, p, cw, vec, pool_blk, wstack, wstack_t, bias, tril4, head_rows, gw["w_out"])
        dx, dg_mix, h, dp_t = _proj_bwd(dx, x1, mix_norm[l][None], dp, gw["w_in"])
        big["w_in"] = _dw(dp_t, h)
        dx, dg_ffn1, h, dy, da_t, db_t, u_t = _ffn_bwd(dx, x0, ffn1_norm[l][None], a1, b1, gw["ffn1_w1"], gw["ffn1_w3"], gw["ffn1_w2"])
        big["ffn1_w1"], big["ffn1_w3"], big["ffn1_w2"] = _dw(da_t, h), _dw(db_t, h), _dw(u_t, dy)
        small_parts[l] = [dg_ffn1[0], dg_mix[0], dg_ffn2[0], dcw, dvec, dpool, dws, dbs]
        mine, theirs = _reduce_pair([big[name].reshape(N_CHIPS, -1, d) for name in BIG])
        reduced_halves[l] = _add_chips(_reduce_chips(_add_pair(mine, theirs)))
    grad_x = dx[None]

    full = dict(zip(BIG, _share_halves(reduced_halves, shard_rows)))
    grad = {name: (jnp.swapaxes(full[name], 1, 2) if name in TRANSPOSED else full[name]) for name in BIG}

    part_shapes = [a.shape for a in small_parts[0]]
    tail = [dg_final[0], loss_part[0]]
    packed = _pack([a for l in range(n_l) for a in small_parts[l]] + tail)
    summed = _unpack(_all_gather_small(packed, reduce=True), part_shapes * n_l + [a.shape for a in tail])
    per_layer = [summed[l * len(part_shapes):(l + 1) * len(part_shapes)] for l in range(n_l)]
    stack = lambda k: jnp.stack([per_layer[l][k] for l in range(n_l)])
    dcw_all, dvec_all, dpool_all, dws_all, dbs_all = stack(3), stack(4), stack(5), stack(6), stack(7)
    loss = summed[-1][0]
    chip_cols = lambda a: lax.dynamic_slice_in_dim(a, chip * (C // N_CHIPS), C // N_CHIPS, axis=2)
    n_pool = pool_w.shape[1]
    grad.update(
        ffn1_norm=stack(0), mix_norm=stack(1), ffn2_norm=stack(2), final_norm=summed[-2],
        conf_conv_w=chip_cols(dcw_all[:, :CONF_KERNEL]), conf_conv_b=dvec_all[:, 0], conf_ln_g=dvec_all[:, 1],
        conf_ln_b=dvec_all[:, 2], pool_scale=dvec_all[:, 3], gmlp_ln_g=dvec_all[:, 4], gmlp_ln_b=dvec_all[:, 5],
        sconv_w=chip_cols(dvec_all[:, 6:6 + SHORT_KERNEL]),
        pool_w=jnp.stack([dpool_all[:, g * POOL_GROUP:(g + 1) * POOL_GROUP, g * POOL_GROUP:(g + 1) * POOL_GROUP]
                          for g in range(n_pool)], axis=1),
        gmlp_w_s=dws_all.reshape(n_l, N_HEADS, CHUNK, CHUNK), gmlp_b_s=dbs_all[:, :N_HEADS],
    )

    delta, new_m, new_v = {}, {}, {}
    for name in BIG:
        shp = w[name].shape
        out = _adamw(_as2d(w[name]), _as2d(grad[name]), _as2d(mom[name]), _as2d(var[name]))
        delta[name], new_m[name], new_v[name] = (o.reshape(shp) for o in out)
    ds, ms, vs = _adamw_small([_as2d(w[k]) if w[k].ndim > 1 else w[k][None] for k in SMALL],
                              [_as2d(grad[k]) if grad[k].ndim > 1 else grad[k][None] for k in SMALL],
                              [_as2d(mom[k]) if mom[k].ndim > 1 else mom[k][None] for k in SMALL],
                              [_as2d(var[k]) if var[k].ndim > 1 else var[k][None] for k in SMALL])
    for k, dl, mo, vo in zip(SMALL, ds, ms, vs):
        delta[k], new_m[k], new_v[k] = dl.reshape(w[k].shape), mo.reshape(w[k].shape), vo.reshape(w[k].shape)

    return (loss, grad_x, *[grad[k] for k in ORDER], *[delta[k] for k in ORDER], *[new_m[k] for k in ORDER],
            *[new_v[k] for k in ORDER])
```

```python
import functools

import jax
import jax.numpy as jnp
from jax import lax
from jax.experimental import pallas as pl
from jax.experimental.pallas import tpu as pltpu

F32 = jnp.float32
BF16 = jnp.bfloat16
MESH = pl.DeviceIdType.MESH
ANY = pl.BlockSpec(memory_space=pl.ANY)

EPS = 1e-6
FFN_RESIDUAL = 0.5
D_GROUP = 256
CONF_KERNEL = 31
SHORT_KERNEL = 3
POOL_GROUP = 64
CHUNK = 128
N_HEADS = 4
HEAD_DIM = 64
HALO = 32
N_CHIPS = 4
N_DEV = 8
LANES = 128
MXU_TILE = 256
VMEM_LIMIT = 56 * 2**20
T_MIX_FWD = 512
T_MIX_BWD = 256

ADAM_LR = 0.001
ADAM_B1 = 0.9
ADAM_B2 = 0.999
ADAM_EPS = 1e-08
ADAM_WD = 0.01
ADAM_STEP = 10

NT = (((1,), (1,)), ((), ()))


def _params(*sem):
    return pltpu.CompilerParams(dimension_semantics=sem, vmem_limit_bytes=VMEM_LIMIT)


def _dot(a, b):
    return jnp.dot(a, b, preferred_element_type=F32)


def _dot_nt(a, b):
    return lax.dot_general(a, b, NT, preferred_element_type=F32)


def _t_bf16(v):
    return jnp.transpose(v).astype(BF16)


def _sigmoid(v):
    return 1.0 / (1.0 + jnp.exp(-v))


def _rms(x, g):
    r = lax.rsqrt(jnp.mean(x * x, axis=-1, keepdims=True) + EPS)
    n = x * r
    return n, r, n * g


def _rms_bwd(dh, n, r, g):
    dn = dh * g
    dx = r * (dn - n * jnp.mean(dn * n, axis=-1, keepdims=True))
    return dx, jnp.sum(dh * n, axis=0, keepdims=True)


def _ln_fwd(z, g, b):
    mu = jnp.mean(z, axis=-1, keepdims=True)
    zc = z - mu
    rs = lax.rsqrt(jnp.mean(zc * zc, axis=-1, keepdims=True) + EPS)
    zn = zc * rs
    return zn, rs, zn * g + b


def _ln_bwd(dl, zn, rs, g):
    dzn = dl * g
    dz = rs * (dzn - jnp.mean(dzn, axis=-1, keepdims=True) - zn * jnp.mean(dzn * zn, axis=-1, keepdims=True))
    return dz, jnp.sum(dl * zn, axis=0, keepdims=True), jnp.sum(dl, axis=0, keepdims=True)


def _tile(n, want):
    return want if n % want == 0 else n


def _resident(shape):
    return pl.BlockSpec(shape, lambda i: (0,) * len(shape), pipeline_mode=pl.Buffered(1))


def _ffn_fwd(x, g, w1t, w3t, w2):
    s, d = x.shape
    f = w1t.shape[0]
    tm, tf = _tile(s, 512), MXU_TILE
    nj = f // tf

    def body(x_ref, g_ref, w1_ref, w3_ref, w2_ref, xo_ref, a_ref, b_ref, u_s):
        h = _rms(x_ref[...], g_ref[...])[2].astype(BF16)
        for j in range(nj):
            cols = slice(j * tf, (j + 1) * tf)
            a = _dot_nt(h, w1_ref[cols, :])
            b = _dot_nt(h, w3_ref[cols, :])
            a_ref[:, cols] = a.astype(BF16)
            b_ref[:, cols] = b.astype(BF16)
            u_s[:, cols] = ((a * _sigmoid(a)) * b).astype(BF16)
        xo_ref[...] = x_ref[...] + FFN_RESIDUAL * _dot(u_s[...], w2_ref[...])

    row = pl.BlockSpec((tm, d), lambda i: (i, 0))
    hid = pl.BlockSpec((tm, f), lambda i: (i, 0))
    return pl.pallas_call(
        body, name="ffn_fwd", grid=(s // tm,),
        in_specs=[row, _resident(g.shape), _resident(w1t.shape), _resident(w3t.shape), _resident(w2.shape)],
        out_specs=[row, hid, hid],
        out_shape=[jax.ShapeDtypeStruct((s, d), F32), jax.ShapeDtypeStruct((s, f), BF16), jax.ShapeDtypeStruct((s, f), BF16)],
        scratch_shapes=[pltpu.VMEM((tm, f), BF16)],
        compiler_params=_params("arbitrary"),
    )(x, g, w1t, w3t, w2)


def _ffn_bwd(dxo, x, g, a, b, w1t, w3t, w2):
    s, d = x.shape
    f = w1t.shape[0]
    tm, tf = _tile(s, 256), MXU_TILE
    nj = f // tf

    def body(dxo_ref, x_ref, g_ref, a_ref, b_ref, w1_ref, w3_ref, w2_ref,
             dx_ref, dg_ref, h_ref, dy_ref, dat_ref, dbt_ref, ut_ref, da_s, db_s):
        @pl.when(pl.program_id(0) == 0)
        def _():
            dg_ref[...] = jnp.zeros_like(dg_ref)

        h_ref[...] = _rms(x_ref[...], g_ref[...])[2].astype(BF16)
        dy = (FFN_RESIDUAL * dxo_ref[...]).astype(BF16)
        dy_ref[...] = dy
        for j in range(nj):
            cols = slice(j * tf, (j + 1) * tf)
            du = _dot_nt(dy, w2_ref[cols, :])
            av = a_ref[:, cols].astype(F32)
            bv = b_ref[:, cols].astype(F32)
            sg = _sigmoid(av)
            sl = av * sg
            da = du * bv * (sg * (1.0 + av * (1.0 - sg)))
            db = du * sl
            da_s[:, cols] = da.astype(BF16)
            db_s[:, cols] = db.astype(BF16)
            dat_ref[cols, :] = _t_bf16(da)
            dbt_ref[cols, :] = _t_bf16(db)
            ut_ref[cols, :] = _t_bf16(sl * bv)
        dh = _dot(da_s[...], w1_ref[...]) + _dot(db_s[...], w3_ref[...])
        n, r, _ = _rms(x_ref[...], g_ref[...])
        dxr, dg = _rms_bwd(dh, n, r, g_ref[...])
        dx_ref[...] = dxo_ref[...] + dxr
        dg_ref[0:1, :] += dg

    row = pl.BlockSpec((tm, d), lambda i: (i, 0))
    hid = pl.BlockSpec((tm, f), lambda i: (i, 0))
    hid_t = pl.BlockSpec((f, tm), lambda i: (0, i))
    return pl.pallas_call(
        body, name="ffn_bwd", grid=(s // tm,),
        in_specs=[row, row, _resident(g.shape), hid, hid, _resident(w1t.shape), _resident(w3t.shape), _resident(w2.shape)],
        out_specs=[row, pl.BlockSpec((8, d), lambda i: (0, 0)), row, row, hid_t, hid_t, hid_t],
        out_shape=[jax.ShapeDtypeStruct((s, d), F32), jax.ShapeDtypeStruct((8, d), F32),
                   jax.ShapeDtypeStruct((s, d), BF16), jax.ShapeDtypeStruct((s, d), BF16),
                   jax.ShapeDtypeStruct((f, s), BF16), jax.ShapeDtypeStruct((f, s), BF16), jax.ShapeDtypeStruct((f, s), BF16)],
        scratch_shapes=[pltpu.VMEM((tm, f), BF16), pltpu.VMEM((tm, f), BF16)],
        compiler_params=_params("arbitrary"),
    )(dxo, x, g, a, b, w1t, w3t, w2)


def _dw(at, bm):
    r, s = at.shape
    n = bm.shape[1]
    rb = r // N_CHIPS
    ts = _tile(s, 1024)
    ni = s // ts

    def body(at_ref, bm_ref, o_ref, acc_s):
        i = pl.program_id(1)

        @pl.when(i == 0)
        def _():
            acc_s[...] = jnp.zeros_like(acc_s)

        acc_s[...] += _dot(at_ref[...], bm_ref[...])

        @pl.when(i == ni - 1)
        def _():
            o_ref[...] = acc_s[...].astype(BF16)

    return pl.pallas_call(
        body, name="dw", grid=(r // rb, ni),
        in_specs=[pl.BlockSpec((rb, ts), lambda k, i: (k, i)), pl.BlockSpec((ts, n), lambda k, i: (i, 0))],
        out_specs=pl.BlockSpec((rb, n), lambda k, i: (k, 0)),
        out_shape=jax.ShapeDtypeStruct((r, n), BF16),
        scratch_shapes=[pltpu.VMEM((rb, n), F32)],
        compiler_params=_params("arbitrary", "arbitrary"),
    )(at, bm)


def _proj_fwd(x, g, w_int):
    s, d = x.shape
    f = w_int.shape[0]
    tm, tf = _tile(s, 1024), 512
    nj = f // tf

    def body(x_ref, g_ref, w_ref, p_ref, h_s):
        @pl.when(pl.program_id(1) == 0)
        def _():
            h_s[...] = _rms(x_ref[...], g_ref[...])[2].astype(BF16)

        p_ref[...] = _dot_nt(h_s[...], w_ref[...])

    return pl.pallas_call(
        body, name="proj_fwd", grid=(s // tm, nj),
        in_specs=[pl.BlockSpec((tm, d), lambda i, j: (i, 0)), pl.BlockSpec((1, d), lambda i, j: (0, 0)),
                  pl.BlockSpec((tf, d), lambda i, j: (j, 0))],
        out_specs=pl.BlockSpec((tm, tf), lambda i, j: (i, j)),
        out_shape=jax.ShapeDtypeStruct((s, f), F32),
        scratch_shapes=[pltpu.VMEM((tm, d), BF16)],
        compiler_params=_params("arbitrary", "arbitrary"),
    )(x, g, w_int)


def _proj_bwd(dxo, x, g, dp, w_int):
    s, d = x.shape
    f = w_int.shape[0]
    tm = _tile(s, 512)

    def body(dxo_ref, x_ref, g_ref, dp_ref, w_ref, dx_ref, dg_ref, h_ref, dpt_ref):
        @pl.when(pl.program_id(0) == 0)
        def _():
            dg_ref[...] = jnp.zeros_like(dg_ref)

        n, r, h = _rms(x_ref[...], g_ref[...])
        h_ref[...] = h.astype(BF16)
        dpv = dp_ref[...]
        dpt_ref[...] = _t_bf16(dpv.astype(F32))
        dxr, dg = _rms_bwd(_dot(dpv, w_ref[...]), n, r, g_ref[...])
        dx_ref[...] = dxo_ref[...] + dxr
        dg_ref[0:1, :] += dg

    row = pl.BlockSpec((tm, d), lambda i: (i, 0))
    return pl.pallas_call(
        body, name="proj_bwd", grid=(s // tm,),
        in_specs=[row, row, pl.BlockSpec((1, d), lambda i: (0, 0)), pl.BlockSpec((tm, f), lambda i: (i, 0)),
                  pl.BlockSpec((f, d), lambda i: (0, 0))],
        out_specs=[row, pl.BlockSpec((8, d), lambda i: (0, 0)), row, pl.BlockSpec((f, tm), lambda i: (0, i))],
        out_shape=[jax.ShapeDtypeStruct((s, d), F32), jax.ShapeDtypeStruct((8, d), F32),
                   jax.ShapeDtypeStruct((s, d), BF16), jax.ShapeDtypeStruct((f, s), BF16)],
        compiler_params=_params("arbitrary"),
    )(dxo, x, g, dp, w_int)


C = D_GROUP


def _piece(ref, k):
    return ref[:, k * C:(k + 1) * C]


def _up(v, r):
    return v if r == 0 else pltpu.roll(v, v.shape[0] - r, 0)


def _down(v, r):
    return v if r == 0 else pltpu.roll(v, r, 0)


def _lane_group():
    lane = lax.broadcasted_iota(jnp.int32, (1, C), 1)
    return (lane >= POOL_GROUP).astype(jnp.int32) + (lane >= 2 * POOL_GROUP).astype(jnp.int32) + (
        lane >= 3 * POOL_GROUP).astype(jnp.int32)


def _by_group(grp, v2, v4, v8, v16):
    return jnp.where(grp == 0, v2, jnp.where(grp == 1, v4, jnp.where(grp == 2, v8, v16)))


def _pool_count(grp, row0, t):
    pos = (row0 + lax.broadcasted_iota(jnp.int32, (t, C), 0) + 1).astype(F32)
    return jnp.minimum(pos, _by_group(grp, 2.0, 4.0, 8.0, 16.0))


def _trailing_sums(ext, grp, t):
    s2 = ext + _down(ext, 1)
    s4 = s2 + _down(s2, 2)
    s8 = s4 + _down(s4, 4)
    s16 = s8 + _down(s8, 8)
    return _by_group(grp, s2, s4, s8, s16)[HALO:HALO + t]


def _leading_sums(ext, grp, t):
    s2 = ext + _up(ext, 1)
    s4 = s2 + _up(s2, 2)
    s8 = s4 + _up(s4, 4)
    s16 = s8 + _up(s8, 8)
    return _by_group(grp, s2, s4, s8, s16)[0:t]


def _head_select(r4, grp):
    out = jnp.where(grp == 0, r4[0:CHUNK], 0.0)
    for h in range(1, N_HEADS):
        out = out + jnp.where(grp == h, r4[h * CHUNK:(h + 1) * CHUNK], 0.0)
    return out


def _conv_taps():
    return [(k, (k + 2) % 8, (k + 2) - (k + 2) % 8) for k in range(CONF_KERNEL)]


def _mixer_fwd(p, x1, cw, vec, pool_w, wstack, bias, w_out):
    s, d = x1.shape
    t = _tile(s, T_MIX_FWD)
    n_ext = t + HALO
    dm = w_out.shape[0]

    def body(p_ref, x1_ref, cw_ref, vec_ref, pw_ref, ws_ref, bias_ref, wo_ref, x2_ref, mt_ref, cy_s, cq_s, cx_s, mix_s):
        i = pl.program_id(0)

        @pl.when(i == 0)
        def _():
            cy_s[...] = jnp.zeros_like(cy_s)
            cq_s[...] = jnp.zeros_like(cq_s)
            cx_s[...] = jnp.zeros_like(cx_s)

        grp = _lane_group()
        y = _piece(p_ref, 0) * _sigmoid(_piece(p_ref, 1))
        ext = jnp.concatenate([cy_s[...], y], axis=0)
        cy_s[...] = y[t - HALO:t]
        z = jnp.broadcast_to(vec_ref[0:1, :], (t, C))
        shifted = {}
        for k, r, off in _conv_taps():
            if r not in shifted:
                shifted[r] = _up(ext, r)
            z = z + cw_ref[k:k + 1, :] * shifted[r][off:off + t]
        ln = _ln_fwd(z, vec_ref[1:2, :], vec_ref[2:3, :])[2]
        mix_s[:, 0:C] = (ln * _sigmoid(ln)).astype(BF16)
        q = _piece(p_ref, 3) * _piece(p_ref, 4)
        ext = jnp.concatenate([cq_s[...], q], axis=0)
        cq_s[...] = q[t - HALO:t]
        cz = vec_ref[8:9, :] * q + vec_ref[7:8, :] * _down(ext, 1)[HALO:] + vec_ref[6:7, :] * _down(ext, 2)[HALO:]
        mix_s[:, C:2 * C] = (_piece(p_ref, 2) * cz).astype(BF16)
        xp = _piece(p_ref, 5)
        ext = jnp.concatenate([cx_s[...], xp], axis=0)
        cx_s[...] = xp[t - HALO:t]
        dd = _trailing_sums(ext, grp, t) / _pool_count(grp, i * t, t) - xp
        mix_s[:, 2 * C:3 * C] = (_dot(dd.astype(BF16), pw_ref[...]) * vec_ref[3:4, :]).astype(BF16)
        vln = _ln_fwd(_piece(p_ref, 7), vec_ref[4:5, :], vec_ref[5:6, :])[2].astype(BF16)
        for n in range(t // CHUNK):
            rows = slice(n * CHUNK, (n + 1) * CHUNK)
            mixed = _head_select(_dot(ws_ref[...], vln[rows]), grp) + bias_ref[...]
            mix_s[rows, 3 * C:4 * C] = (p_ref[rows, 6 * C:7 * C] * mixed).astype(BF16)
        mix = mix_s[...]
        x2_ref[...] = x1_ref[...] + _dot(mix, wo_ref[...])
        mt_ref[...] = _t_bf16(mix.astype(F32))

    full = lambda a: pl.BlockSpec(a.shape, lambda i: (0, 0))
    return pl.pallas_call(
        body, name="mixer_fwd", grid=(s // t,),
        in_specs=[pl.BlockSpec((t, p.shape[1]), lambda i: (i, 0)), pl.BlockSpec((t, d), lambda i: (i, 0)),
                  full(cw), full(vec), full(pool_w), full(wstack), full(bias), full(w_out)],
        out_specs=[pl.BlockSpec((t, d), lambda i: (i, 0)), pl.BlockSpec((dm, t), lambda i: (0, i))],
        out_shape=[jax.ShapeDtypeStruct((s, d), F32), jax.ShapeDtypeStruct((dm, s), BF16)],
        scratch_shapes=[pltpu.VMEM((HALO, C), F32)] * 3 + [pltpu.VMEM((t, dm), BF16)],
        compiler_params=_params("arbitrary"),
    )(p, x1, cw, vec, pool_w, wstack, bias, w_out)


def _mixer_bwd(dx2, p, cw, vec, pool_w, wstack, wstack_t, bias, tril4, head_rows, w_out):
    s, d = dx2.shape
    t = _tile(s, T_MIX_BWD)
    nt = s // t
    n_ext = t + HALO
    hb = t // HALO

    def body(dx2_ref, p_ref, ph_ref, cw_ref, vec_ref, pw_ref, ws_ref, wst_ref, bias_ref, tril_ref, hr_ref, wo_ref,
             dp_ref, dcw_ref, dvec_ref, dpool_ref, dws_ref, dbs_ref, cdz_s, cdc_s, cf_s, vy_s, dvl_s, dbias_s):
        i = pl.program_id(0)
        tile = nt - 1 - i

        @pl.when(i == 0)
        def _():
            for ref in (cdz_s, cdc_s, cf_s, dbias_s, dcw_ref, dvec_ref, dpool_ref, dws_ref, dbs_ref):
                ref[...] = jnp.zeros_like(ref)

        grp = _lane_group()
        first = jnp.where(tile > 0, 1.0, 0.0)
        dmix = _dot_nt(dx2_ref[...].astype(BF16), wo_ref[...])
        d_a, d_b, d_c, d_d = (dmix[:, k * C:(k + 1) * C] for k in range(4))

        def acc_vec(row, v):
            dvec_ref[row:row + 1, :] += jnp.sum(v, axis=0, keepdims=True)

        val, gate = _piece(p_ref, 0), _piece(p_ref, 1)
        sgate = _sigmoid(gate)
        y = val * sgate
        y_halo = ph_ref[:, 0:C] * _sigmoid(ph_ref[:, C:2 * C]) * first
        ext = jnp.concatenate([y_halo, y], axis=0)
        for r in range(8):
            vy_s[r] = _up(ext, r)
        z = jnp.broadcast_to(vec_ref[0:1, :], (t, C))
        for k, r, off in _conv_taps():
            z = z + cw_ref[k:k + 1, :] * vy_s[r, off:off + t, :]
        zn, rs, ln = _ln_fwd(z, vec_ref[1:2, :], vec_ref[2:3, :])
        sg = _sigmoid(ln)
        dln = d_a * (sg * (1.0 + ln * (1.0 - sg)))
        dz, dg, db = _ln_bwd(dln, zn, rs, vec_ref[1:2, :])
        dvec_ref[1:2, :] += dg
        dvec_ref[2:3, :] += db
        acc_vec(0, dz)
        for k, r, off in _conv_taps():
            dcw_ref[k:k + 1, :] += jnp.sum(dz * vy_s[r, off:off + t, :], axis=0, keepdims=True)
        ext = jnp.concatenate([dz, cdz_s[...]], axis=0)
        cdz_s[...] = dz[0:HALO]
        dy = jnp.zeros((t, C), F32)
        shifted = {}
        for k in range(CONF_KERNEL):
            m = CONF_KERNEL - 1 - k
            r, off = m % 8, m - m % 8
            if r not in shifted:
                shifted[r] = _up(ext, r)
            dy = dy + cw_ref[k:k + 1, :] * shifted[r][off:off + t]
        dp_ref[:, 0:C] = (dy * sgate).astype(BF16)
        dp_ref[:, C:2 * C] = (dy * val * sgate * (1.0 - sgate)).astype(BF16)

        sb, sc, sx = _piece(p_ref, 2), _piece(p_ref, 3), _piece(p_ref, 4)
        q = sc * sx
        q_halo = ph_ref[:, 3 * C:4 * C] * ph_ref[:, 4 * C:5 * C] * first
        ext = jnp.concatenate([q_halo, q], axis=0)
        q1, q2 = _down(ext, 1)[HALO:], _down(ext, 2)[HALO:]
        cz = vec_ref[8:9, :] * q + vec_ref[7:8, :] * q1 + vec_ref[6:7, :] * q2
        dcz = d_b * sb
        dp_ref[:, 2 * C:3 * C] = (d_b * cz).astype(BF16)
        acc_vec(8, dcz * q)
        acc_vec(7, dcz * q1)
        acc_vec(6, dcz * q2)
        ext = jnp.concatenate([dcz, cdc_s[...]], axis=0)
        cdc_s[...] = dcz[0:HALO]
        dq = vec_ref[8:9, :] * dcz + vec_ref[7:8, :] * _up(ext, 1)[0:t] + vec_ref[6:7, :] * _up(ext, 2)[0:t]
        dp_ref[:, 3 * C:4 * C] = (dq * sx).astype(BF16)
        dp_ref[:, 4 * C:5 * C] = (dq * sc).astype(BF16)

        xp = _piece(p_ref, 5)
        ext = jnp.concatenate([ph_ref[:, 5 * C:6 * C] * first, xp], axis=0)
        cnt = _pool_count(grp, tile * t, t)
        dd = (_trailing_sums(ext, grp, t) / cnt - xp).astype(BF16)
        e2 = _dot(dd, pw_ref[...])
        acc_vec(3, d_c * e2)
        de = (d_c * vec_ref[3:4, :]).astype(BF16)
        dpool_ref[...] += _dot(_t_bf16(dd.astype(F32)), de)
        ddd = _dot_nt(de, pw_ref[...])
        fq = ddd / cnt
        ext = jnp.concatenate([fq, cf_s[...]], axis=0)
        cf_s[...] = fq[0:HALO]
        dp_ref[:, 5 * C:6 * C] = (_leading_sums(ext, grp, t) - ddd).astype(BF16)

        vn, vrs, vlnf = _ln_fwd(_piece(p_ref, 7), vec_ref[4:5, :], vec_ref[5:6, :])
        vln = vlnf.astype(BF16)
        for n in range(t // CHUNK):
            rows = slice(n * CHUNK, (n + 1) * CHUNK)
            mixed = _head_select(_dot(ws_ref[...], vln[rows]), grp) + bias_ref[...]
            dd_n = d_d[rows]
            dp_ref[rows, 6 * C:7 * C] = (dd_n * mixed).astype(BF16)
            dmx = dd_n * p_ref[rows, 6 * C:7 * C]
            dbias_s[...] += dmx
            dmx_b = dmx.astype(BF16)
            dvl_s[rows, :] = _head_select(_dot(wst_ref[...], dmx_b), grp)
            for h in range(N_HEADS):
                hrows = slice(h * CHUNK, (h + 1) * CHUNK)
                dws_ref[hrows, :] += _dot_nt(jnp.where(grp == h, dmx_b, jnp.zeros_like(dmx_b)), vln[rows])
        dvl = dvl_s[...]
        dv, dg, db = _ln_bwd(dvl, vn, vrs, vec_ref[4:5, :])
        dvec_ref[4:5, :] += dg
        dvec_ref[5:6, :] += db
        dp_ref[:, 7 * C:8 * C] = dv.astype(BF16)

        @pl.when(i == nt - 1)
        def _():
            dws_ref[...] = dws_ref[...] * tril_ref[...]
            dbs_ref[...] = lax.dot_general(hr_ref[...], dbias_s[...], NT, precision=lax.Precision.HIGHEST,
                                           preferred_element_type=F32)

    full = lambda a: pl.BlockSpec(a.shape, lambda i: (0, 0))
    acc = lambda shape: pl.BlockSpec(shape, lambda i: (0, 0))
    f = p.shape[1]
    return pl.pallas_call(
        body, name="mixer_bwd", grid=(nt,),
        in_specs=[pl.BlockSpec((t, d), lambda i: (nt - 1 - i, 0)), pl.BlockSpec((t, f), lambda i: (nt - 1 - i, 0)),
                  pl.BlockSpec((HALO, f), lambda i: (jnp.maximum((nt - 1 - i) * hb - 1, 0), 0)),
                  full(cw), full(vec), full(pool_w), full(wstack), full(wstack_t), full(bias), full(tril4), full(head_rows),
                  full(w_out)],
        out_specs=[pl.BlockSpec((t, f), lambda i: (nt - 1 - i, 0)), acc((32, C)), acc((16, C)), acc((C, C)),
                   acc((N_HEADS * CHUNK, CHUNK)), acc((8, CHUNK))],
        out_shape=[jax.ShapeDtypeStruct((s, f), BF16), jax.ShapeDtypeStruct((32, C), F32), jax.ShapeDtypeStruct((16, C), F32),
                   jax.ShapeDtypeStruct((C, C), F32), jax.ShapeDtypeStruct((N_HEADS * CHUNK, CHUNK), F32),
                   jax.ShapeDtypeStruct((8, CHUNK), F32)],
        scratch_shapes=[pltpu.VMEM((HALO, C), F32)] * 3 + [pltpu.VMEM((8, n_ext, C), F32), pltpu.VMEM((t, C), F32),
                                                            pltpu.VMEM((CHUNK, C), F32)],
        compiler_params=_params("arbitrary"),
    )(dx2, p, p, cw, vec, pool_w, wstack, wstack_t, bias, tril4, head_rows, w_out)


def _loss_bwd(x, g, target):
    s, d = x.shape
    tm = _tile(s, 512)

    def body(x_ref, g_ref, t_ref, dx_ref, dg_ref, loss_ref):
        @pl.when(pl.program_id(0) == 0)
        def _():
            dg_ref[...] = jnp.zeros_like(dg_ref)
            loss_ref[...] = jnp.zeros_like(loss_ref)

        n, r, y = _rms(x_ref[...], g_ref[...])
        err = y - t_ref[...]
        loss_ref[...] += 0.5 * jnp.sum(jnp.mean(err * err, axis=-1, keepdims=True), axis=0, keepdims=True)
        dxr, dg = _rms_bwd(err * (1.0 / d), n, r, g_ref[...])
        dx_ref[...] = dxr
        dg_ref[0:1, :] += dg

    row = pl.BlockSpec((tm, d), lambda i: (i, 0))
    return pl.pallas_call(
        body, name="loss_bwd", grid=(s // tm,),
        in_specs=[row, pl.BlockSpec((1, d), lambda i: (0, 0)), row],
        out_specs=[row, pl.BlockSpec((8, d), lambda i: (0, 0)), pl.BlockSpec((8, LANES), lambda i: (0, 0))],
        out_shape=[jax.ShapeDtypeStruct((s, d), F32), jax.ShapeDtypeStruct((8, d), F32), jax.ShapeDtypeStruct((8, LANES), F32)],
        compiler_params=_params("arbitrary"),
    )(x, g, target)


def _adamw_math(w, g, m, v):
    m = ADAM_B1 * m + (1.0 - ADAM_B1) * g
    v = ADAM_B2 * v + (1.0 - ADAM_B2) * (g * g)
    m_hat = m / (1.0 - ADAM_B1 ** ADAM_STEP)
    v_hat = v / (1.0 - ADAM_B2 ** ADAM_STEP)
    return -ADAM_LR * (m_hat / (jnp.sqrt(v_hat) + ADAM_EPS) + ADAM_WD * w), m, v


def _adamw(w, g, m, v):
    r, c = w.shape
    tr = r // 8 if r % 64 == 0 else r

    def body(w_ref, g_ref, m_ref, v_ref, d_ref, mo_ref, vo_ref):
        d_ref[...], mo_ref[...], vo_ref[...] = _adamw_math(w_ref[...], g_ref[...], m_ref[...], v_ref[...])

    blk = pl.BlockSpec((tr, c), lambda i: (i, 0))
    return pl.pallas_call(
        body, name="adamw", grid=(r // tr,), in_specs=[blk] * 4, out_specs=[blk] * 3,
        out_shape=[jax.ShapeDtypeStruct((r, c), F32)] * 3, compiler_params=_params("arbitrary"),
    )(w, g, m, v)


def _adamw_small(ws, gs, ms, vs):
    n = len(ws)

    def body(*refs):
        ins, outs = refs[:4 * n], refs[4 * n:]
        for k in range(n):
            dl, mo, vo = _adamw_math(ins[k][...], ins[n + k][...], ins[2 * n + k][...], ins[3 * n + k][...])
            outs[k][...], outs[n + k][...], outs[2 * n + k][...] = dl, mo, vo

    vm = pl.BlockSpec(memory_space=pltpu.VMEM)
    out = pl.pallas_call(
        body, name="adamw_small", in_specs=[vm] * (4 * n), out_specs=[vm] * (3 * n),
        out_shape=[jax.ShapeDtypeStruct(a.shape, F32) for a in ws] * 3,
        compiler_params=pltpu.CompilerParams(vmem_limit_bytes=VMEM_LIMIT),
    )(*ws, *gs, *ms, *vs)
    return out[:n], out[n:2 * n], out[2 * n:]


def _where_am_i():
    x, y, c = lax.axis_index("x"), lax.axis_index("y"), lax.axis_index("c")
    chips = [(1 - x, y), (x, 1 - y), (1 - x, 1 - y)]
    return x, y, c, chips


def _chip_id(chip):
    return 2 * chip[0] + chip[1]


def _gather_weights(shards):
    n = len(shards)

    def body(*refs):
        ins, outs = refs[:n], refs[n:2 * n]
        send_sems, recv_sems, local_sems = refs[2 * n:]
        x, y, c, chips = _where_am_i()
        me = _chip_id((x, y))
        sibling = (x, y, 1 - c)

        def half(a, who):
            hr = ins[a].shape[0] // 2
            return pl.ds(who * hr, hr)

        def copy(a, slot, src, dst, to):
            return pltpu.make_async_remote_copy(src_ref=src, dst_ref=dst, send_sem=send_sems.at[6 * a + slot],
                                                recv_sem=recv_sems.at[6 * a + slot], device_id=to, device_id_type=MESH)

        local = [pltpu.make_async_copy(ins[a], outs[a].at[me], local_sems.at[a]) for a in range(n)]
        for cp in local:
            cp.start()
        sent = []
        for a in range(n):
            for j, chip in enumerate(chips):
                sent.append(copy(a, j, ins[a].at[half(a, c)], outs[a].at[me, half(a, c)], (*chip, c)))
                sent[-1].start()
        for j, chip in enumerate(chips):
            for a in range(n):
                landed = outs[a].at[_chip_id(chip), half(a, c)]
                copy(a, j, landed, landed, (*chip, c)).wait_recv()
                sent.append(copy(a, 3 + j, landed, landed, sibling))
                sent[-1].start()
        for j, chip in enumerate(chips):
            for a in range(n):
                landed = outs[a].at[_chip_id(chip), half(a, 1 - c)]
                copy(a, 3 + j, landed, landed, sibling).wait_recv()
        for cp in sent:
            cp.wait_send()
        for cp in local:
            cp.wait()

    return pl.pallas_call(
        body, name="gather_weights", in_specs=[ANY] * n, out_specs=[ANY] * n,
        out_shape=[jax.ShapeDtypeStruct((N_CHIPS,) + a.shape, a.dtype) for a in shards],
        scratch_shapes=[pltpu.SemaphoreType.DMA((6 * n,)), pltpu.SemaphoreType.DMA((6 * n,)), pltpu.SemaphoreType.DMA((n,))],
    )(*shards)


SLOTS = 3


def _pair_sum(grads):
    n = len(grads)
    halves = [g.shape[1] // 2 for g in grads]
    total, rows = sum(halves), max(halves)
    d = grads[0].shape[2]
    chunks, off = [], 0
    for a in range(n):
        chunks += [(a, k, off, halves[a]) for k in range(N_CHIPS)]
        off += halves[a]
    nc = len(chunks)

    def body(*refs):
        ins, out_ref = refs[:n], refs[n]
        sbuf, rbuf, mbuf, obuf, ls_sems, lm_sems, st_sems, send_sems, recv_sems, credits = refs[n + 1:]
        x, y, c, _ = _where_am_i()
        sibling = (x, y, 1 - c)

        def load_theirs(i):
            a, k, _, hr = chunks[i]
            return pltpu.make_async_copy(ins[a].at[k, pl.ds((1 - c) * hr, hr)], sbuf.at[i % SLOTS, pl.ds(0, hr)], ls_sems.at[i % SLOTS])

        def load_mine(i):
            a, k, _, hr = chunks[i]
            return pltpu.make_async_copy(ins[a].at[k, pl.ds(c * hr, hr)], mbuf.at[i % SLOTS, pl.ds(0, hr)], lm_sems.at[i % SLOTS])

        def push(i):
            hr, slot = chunks[i][3], i % SLOTS
            return pltpu.make_async_remote_copy(src_ref=sbuf.at[slot, pl.ds(0, hr)], dst_ref=rbuf.at[slot, pl.ds(0, hr)],
                                                send_sem=send_sems.at[slot], recv_sem=recv_sems.at[slot], device_id=sibling,
                                                device_id_type=MESH)

        def store(i):
            _, k, o, hr = chunks[i]
            slot = i % SLOTS
            return pltpu.make_async_copy(obuf.at[slot, pl.ds(0, hr)], out_ref.at[k, pl.ds(o, hr)], st_sems.at[slot])

        def start_push(i):
            load_theirs(i).wait()
            if i >= SLOTS:
                pl.semaphore_wait(credits.at[i % SLOTS], 1)
            push(i).start()

        for i in range(min(2, nc)):
            load_theirs(i).start()
            load_mine(i).start()
        start_push(0)
        for i in range(nc):
            hr, slot = chunks[i][3], i % SLOTS
            if i + 2 < nc:
                load_theirs(i + 2).start()
                load_mine(i + 2).start()
            if i + 1 < nc:
                start_push(i + 1)
            push(i).wait_recv()
            push(i).wait_send()
            load_mine(i).wait()
            if i >= SLOTS:
                store(i - SLOTS).wait()
            obuf[slot, 0:hr, :] = (mbuf[slot, 0:hr, :].astype(F32) + rbuf[slot, 0:hr, :].astype(F32)).astype(BF16)
            if i + SLOTS < nc:
                pl.semaphore_signal(credits.at[slot], inc=1, device_id=sibling, device_id_type=MESH)
            store(i).start()
        for i in range(max(0, nc - SLOTS), nc):
            store(i).wait()

    stage = pltpu.VMEM((SLOTS, rows, d), BF16)
    dma = pltpu.SemaphoreType.DMA((SLOTS,))
    return pl.pallas_call(
        body, name="pair_sum", in_specs=[ANY] * n, out_specs=ANY, out_shape=jax.ShapeDtypeStruct((N_CHIPS, total, d), BF16),
        scratch_shapes=[stage, stage, stage, stage, dma, dma, dma, dma, dma, pltpu.SemaphoreType.REGULAR((SLOTS,))],
        compiler_params=pltpu.CompilerParams(vmem_limit_bytes=VMEM_LIMIT),
    )(*grads)


def _reduce_chips(sums):
    def body(s_ref, o_ref, send_sems, recv_sems, local_sem):
        x, y, c, chips = _where_am_i()
        me = _chip_id((x, y))
        local = pltpu.make_async_copy(s_ref.at[me], o_ref.at[3], local_sem)
        local.start()
        copies = [pltpu.make_async_remote_copy(src_ref=s_ref.at[_chip_id(chip)], dst_ref=o_ref.at[j], send_sem=send_sems.at[j],
                                               recv_sem=recv_sems.at[j], device_id=(*chip, c), device_id_type=MESH)
                  for j, chip in enumerate(chips)]
        for cp in copies:
            cp.start()
        for cp in copies:
            cp.wait_recv()
        for cp in copies:
            cp.wait_send()
        local.wait()

    return pl.pallas_call(
        body, name="reduce_chips", in_specs=[ANY], out_specs=ANY, out_shape=jax.ShapeDtypeStruct(sums.shape, BF16),
        scratch_shapes=[pltpu.SemaphoreType.DMA((3,)), pltpu.SemaphoreType.DMA((3,)), pltpu.SemaphoreType.DMA],
    )(sums)


def _sum_share(parts, rows):
    n_l, n = len(parts), len(rows)
    d = parts[0].shape[2]
    halves = [r // 2 for r in rows]
    hmax = max(halves)
    chunks = []
    for l in range(n_l):
        off = 0
        for a in range(n):
            chunks.append((l, a, off, halves[a]))
            off += halves[a]
    nc = len(chunks)

    def body(*refs):
        ins, outs = refs[:n_l], refs[n_l:n_l + n]
        pbuf, obuf, rbuf, ld_sems, keep_sems, take_sems, send_sems, recv_sems, credits = refs[n_l + n:]
        x, y, c, _ = _where_am_i()
        sibling = (x, y, 1 - c)

        def load(i):
            l, _, off, hr = chunks[i]
            return pltpu.make_async_copy(ins[l].at[:, pl.ds(off, hr)], pbuf.at[i % SLOTS, :, pl.ds(0, hr)], ld_sems.at[i % SLOTS])

        def keep(i):
            l, a, _, hr = chunks[i]
            return pltpu.make_async_copy(obuf.at[i % SLOTS, pl.ds(0, hr)], outs[a].at[l, pl.ds(c * hr, hr)], keep_sems.at[i % SLOTS])

        def push(i):
            hr, slot = chunks[i][3], i % SLOTS
            return pltpu.make_async_remote_copy(src_ref=obuf.at[slot, pl.ds(0, hr)], dst_ref=rbuf.at[slot, pl.ds(0, hr)],
                                                send_sem=send_sems.at[slot], recv_sem=recv_sems.at[slot], device_id=sibling,
                                                device_id_type=MESH)

        def take(i):
            l, a, _, hr = chunks[i]
            return pltpu.make_async_copy(rbuf.at[i % SLOTS, pl.ds(0, hr)], outs[a].at[l, pl.ds((1 - c) * hr, hr)],
                                         take_sems.at[i % SLOTS])

        for i in range(min(2, nc)):
            load(i).start()
        for i in range(nc):
            hr, slot = chunks[i][3], i % SLOTS
            if i + 2 < nc:
                load(i + 2).start()
            load(i).wait()
            if i >= SLOTS:
                keep(i - SLOTS).wait()
                push(i - SLOTS).wait_send()
            part = lambda k: pbuf[slot, k, 0:hr, :].astype(F32)
            obuf[slot, 0:hr, :] = ((part(3) + part(0)) + part(1)) + part(2)
            keep(i).start()
            if i >= SLOTS:
                pl.semaphore_wait(credits.at[slot], 1)
            push(i).start()
            if i >= 1:
                push(i - 1).wait_recv()
                take(i - 1).start()
            if i >= 2:
                take(i - 2).wait()
                if i - 2 + SLOTS < nc:
                    pl.semaphore_signal(credits.at[(i - 2) % SLOTS], inc=1, device_id=sibling, device_id_type=MESH)
        push(nc - 1).wait_recv()
        take(nc - 1).start()
        for i in range(max(0, nc - 2), nc):
            take(i).wait()
        for i in range(max(0, nc - SLOTS), nc):
            keep(i).wait()
            push(i).wait_send()

    dma = pltpu.SemaphoreType.DMA((SLOTS,))
    return pl.pallas_call(
        body, name="sum_share", in_specs=[ANY] * n_l, out_specs=[ANY] * n,
        out_shape=[jax.ShapeDtypeStruct((n_l, r, d), F32) for r in rows],
        scratch_shapes=[pltpu.VMEM((SLOTS, N_CHIPS, hmax, d), BF16), pltpu.VMEM((SLOTS, hmax, d), F32),
                        pltpu.VMEM((SLOTS, hmax, d), F32), dma, dma, dma, dma, dma, pltpu.SemaphoreType.REGULAR((SLOTS,))],
        compiler_params=pltpu.CompilerParams(vmem_limit_bytes=VMEM_LIMIT),
    )(*parts)


def _all_gather_small(block, reduce):
    m, n = block.shape

    def body(x_ref, out_ref, *scratch):
        if reduce:
            all_ref, send_sems, recv_sems, local_sem = scratch
        else:
            all_ref = out_ref
            send_sems, recv_sems, local_sem = scratch
        x, y, c, chips = _where_am_i()
        me, sibling = (x, y, c), (x, y, 1 - c)

        def rows(px, py, pc):
            return all_ref.at[pl.ds((4 * px + 2 * py + pc) * m, m), :]

        def copy(k, blk, to, src=None):
            return pltpu.make_async_remote_copy(src_ref=rows(*blk) if src is None else src, dst_ref=rows(*blk),
                                                send_sem=send_sems.at[k], recv_sem=recv_sems.at[k], device_id=to,
                                                device_id_type=MESH)

        mine = pltpu.make_async_copy(x_ref, rows(*me), local_sem)
        mine.start()
        first = [copy(0, me, sibling, src=x_ref)]
        first += [copy(1 + j, me, (*chip, c), src=x_ref) for j, chip in enumerate(chips)]
        for cp in first:
            cp.start()
        passed = [copy(4 + j, (*chip, c), sibling) for j, chip in enumerate(chips)]
        for j, chip in enumerate(chips):
            copy(1 + j, (*chip, c), me).wait_recv()
            passed[j].start()
        copy(0, sibling, me).wait_recv()
        for j, chip in enumerate(chips):
            copy(4 + j, (*chip, 1 - c), me).wait_recv()
        for cp in first + passed:
            cp.wait_send()
        mine.wait()
        if reduce:
            total = all_ref[0:m, :]
            for dev in range(1, N_DEV):
                total = total + all_ref[dev * m:(dev + 1) * m, :]
            out_ref[...] = total

    vm = pl.BlockSpec(memory_space=pltpu.VMEM)
    sems = [pltpu.SemaphoreType.DMA((7,)), pltpu.SemaphoreType.DMA((7,)), pltpu.SemaphoreType.DMA]
    return pl.pallas_call(
        body, name="reduce_small" if reduce else "gather_small", in_specs=[vm], out_specs=vm,
        out_shape=jax.ShapeDtypeStruct((m, n) if reduce else (N_DEV * m, n), F32),
        scratch_shapes=([pltpu.VMEM((N_DEV * m, n), F32)] if reduce else []) + sems,
        compiler_params=pltpu.CompilerParams(vmem_limit_bytes=VMEM_LIMIT),
    )(block)


def _pack(arrays):
    flat = jnp.concatenate([a.reshape(-1) for a in arrays])
    pad = (-flat.shape[0]) % (8 * LANES)
    return jnp.pad(flat, (0, pad)).reshape(-1, LANES)


def _unpack(buf, shapes):
    flat = buf.reshape(-1)
    out, off = [], 0
    for shp in shapes:
        size = 1
        for dim in shp:
            size *= dim
        out.append(flat[off:off + size].reshape(shp))
        off += size
    return out


BIG = ("ffn1_w1", "ffn1_w3", "ffn1_w2", "w_in", "w_out", "ffn2_w1", "ffn2_w3", "ffn2_w2")
TRANSPOSED = ("ffn1_w1", "ffn1_w3", "w_in", "ffn2_w1", "ffn2_w3")
SMALL = ("ffn1_norm", "mix_norm", "conf_conv_w", "conf_conv_b", "conf_ln_g", "conf_ln_b", "sconv_w", "pool_w", "pool_scale",
         "gmlp_ln_g", "gmlp_ln_b", "gmlp_w_s", "gmlp_b_s", "ffn2_norm", "final_norm")
ORDER = ("ffn1_norm", "ffn1_w1", "ffn1_w3", "ffn1_w2", "mix_norm", "w_in", "conf_conv_w", "conf_conv_b", "conf_ln_g", "conf_ln_b",
         "sconv_w", "pool_w", "pool_scale", "gmlp_ln_g", "gmlp_ln_b", "gmlp_w_s", "gmlp_b_s", "w_out", "ffn2_norm", "ffn2_w1",
         "ffn2_w3", "ffn2_w2", "final_norm")


def _as2d(a):
    return a.reshape(-1, a.shape[-1])


def kernel(x, ffn1_norm, ffn1_w1, ffn1_w3, ffn1_w2, mix_norm, w_in, conf_conv_w, conf_conv_b, conf_ln_g, conf_ln_b, sconv_w, pool_w, pool_scale, gmlp_ln_g, gmlp_ln_b, gmlp_w_s, gmlp_b_s, w_out, ffn2_norm, ffn2_w1, ffn2_w3, ffn2_w2, final_norm, loss_target, m_ffn1_norm, m_ffn1_w1, m_ffn1_w3, m_ffn1_w2, m_mix_norm, m_w_in, m_conf_conv_w, m_conf_conv_b, m_conf_ln_g, m_conf_ln_b, m_sconv_w, m_pool_w, m_pool_scale, m_gmlp_ln_g, m_gmlp_ln_b, m_gmlp_w_s, m_gmlp_b_s, m_w_out, m_ffn2_norm, m_ffn2_w1, m_ffn2_w3, m_ffn2_w2, m_final_norm, v_ffn1_norm, v_ffn1_w1, v_ffn1_w3, v_ffn1_w2, v_mix_norm, v_w_in, v_conf_conv_w, v_conf_conv_b, v_conf_ln_g, v_conf_ln_b, v_sconv_w, v_pool_w, v_pool_scale, v_gmlp_ln_g, v_gmlp_ln_b, v_gmlp_w_s, v_gmlp_b_s, v_w_out, v_ffn2_norm, v_ffn2_w1, v_ffn2_w3, v_ffn2_w2, v_final_norm):
    given = dict(locals())
    w = {k: given[k] for k in ORDER}
    mom = {k: given["m_" + k] for k in ORDER}
    var = {k: given["v_" + k] for k in ORDER}
    n_l = ffn1_w1.shape[0]
    xs = x[0]
    d = xs.shape[1]
    chip = 2 * lax.axis_index("x") + lax.axis_index("y")

    def shard(name, l):
        a = w[name][l]
        return (a.T if name in TRANSPOSED else a).astype(BF16)

    gathered = []
    for l in range(n_l):
        out = _gather_weights([shard(name, l) for name in BIG])
        gathered.append({name: g.reshape(-1, d) for name, g in zip(BIG, out)})
    shard_rows = [w[name].shape[2] if name in TRANSPOSED else w[name].shape[1] for name in BIG]

    conv_shapes = [conf_conv_w.shape, sconv_w.shape]
    conv_all = _all_gather_small(_pack([conf_conv_w, sconv_w]), reduce=False)
    conv_all = conv_all.reshape(N_CHIPS, 2, -1)[:, 0]
    conf_full, sconv_full = [jnp.concatenate([_unpack(conv_all[k], conv_shapes)[a] for k in range(N_CHIPS)], axis=-1)
                             for a in range(2)]

    lane = jnp.arange(C) // HEAD_DIM
    head_rows = (jnp.arange(8)[:, None] == lane[None, :]).astype(F32)
    tril = jnp.tril(jnp.ones((CHUNK, CHUNK), F32))
    tril4 = jnp.tile(tril, (N_HEADS, 1))
    mixer_consts = []
    for l in range(n_l):
        cw = jnp.pad(conf_full[l], ((0, 32 - CONF_KERNEL), (0, 0)))
        vec = jnp.concatenate([conf_conv_b[l][None], conf_ln_g[l][None], conf_ln_b[l][None], pool_scale[l][None],
                               gmlp_ln_g[l][None], gmlp_ln_b[l][None], sconv_full[l], jnp.zeros((7, C), F32)], axis=0)
        eye = jnp.eye(len(pool_w[l]), dtype=F32)
        pool_blk = (eye[:, None, :, None] * pool_w[l][:, :, None, :]).reshape(C, C).astype(BF16)
        ws = gmlp_w_s[l] * tril[None]
        wstack = ws.reshape(N_HEADS * CHUNK, CHUNK).astype(BF16)
        wstack_t = jnp.swapaxes(ws, 1, 2).reshape(N_HEADS * CHUNK, CHUNK).astype(BF16)
        bias = jnp.repeat(gmlp_b_s[l].T, HEAD_DIM, axis=1)
        mixer_consts.append((cw, vec, pool_blk, wstack, wstack_t, bias))

    saved = []
    cur = xs
    for l in range(n_l):
        gw = gathered[l]
        cw, vec, pool_blk, wstack, wstack_t, bias = mixer_consts[l]
        x0 = cur
        x1, a1, b1 = _ffn_fwd(x0, ffn1_norm[l][None], gw["ffn1_w1"], gw["ffn1_w3"], gw["ffn1_w2"])
        p = _proj_fwd(x1, mix_norm[l][None], gw["w_in"])
        x2, mix_t = _mixer_fwd(p, x1, cw, vec, pool_blk, wstack, bias, gw["w_out"])
        x3, a2, b2 = _ffn_fwd(x2, ffn2_norm[l][None], gw["ffn2_w1"], gw["ffn2_w3"], gw["ffn2_w2"])
        saved.append((x0, x1, x2, a1, b1, a2, b2, p, mix_t))
        cur = x3

    dx, dg_final, loss_part = _loss_bwd(cur, final_norm[None], loss_target[0])

    small_parts = [None] * n_l
    reduced_halves = [None] * n_l
    for l in reversed(range(n_l)):
        gw = gathered[l]
        cw, vec, pool_blk, wstack, wstack_t, bias = mixer_consts[l]
        x0, x1, x2, a1, b1, a2, b2, p, mix_t = saved[l]
        big = {}
        dx, dg_ffn2, h, dy, da_t, db_t, u_t = _ffn_bwd(dx, x2, ffn2_norm[l][None], a2, b2, gw["ffn2_w1"], gw["ffn2_w3"], gw["ffn2_w2"])
        big["ffn2_w1"], big["ffn2_w3"], big["ffn2_w2"] = _dw(da_t, h), _dw(db_t, h), _dw(u_t, dy)
        big["w_out"] = _dw(mix_t, dx.astype(BF16))
        dp, dcw, dvec, dpool, dws, dbs = _mixer_bwd(dx, p, cw, vec, pool_blk, wstack, wstack_t, bias, tril4, head_rows, gw["w_out"])
        dx, dg_mix, h, dp_t = _proj_bwd(dx, x1, mix_norm[l][None], dp, gw["w_in"])
        big["w_in"] = _dw(dp_t, h)
        dx, dg_ffn1, h, dy, da_t, db_t, u_t = _ffn_bwd(dx, x0, ffn1_norm[l][None], a1, b1, gw["ffn1_w1"], gw["ffn1_w3"], gw["ffn1_w2"])
        big["ffn1_w1"], big["ffn1_w3"], big["ffn1_w2"] = _dw(da_t, h), _dw(db_t, h), _dw(u_t, dy)
        small_parts[l] = [dg_ffn1[0], dg_mix[0], dg_ffn2[0], dcw, dvec, dpool, dws, dbs]
        reduced_halves[l] = _reduce_chips(_pair_sum([big[name].reshape(N_CHIPS, -1, d) for name in BIG]))
    grad_x = dx[None]

    full = dict(zip(BIG, _sum_share(reduced_halves, shard_rows)))
    grad = {name: (jnp.swapaxes(full[name], 1, 2) if name in TRANSPOSED else full[name]) for name in BIG}

    part_shapes = [a.shape for a in small_parts[0]]
    tail = [dg_final[0], loss_part[0]]
    packed = _pack([a for l in range(n_l) for a in small_parts[l]] + tail)
    summed = _unpack(_all_gather_small(packed, reduce=True), part_shapes * n_l + [a.shape for a in tail])
    per_layer = [summed[l * len(part_shapes):(l + 1) * len(part_shapes)] for l in range(n_l)]
    stack = lambda k: jnp.stack([per_layer[l][k] for l in range(n_l)])
    dcw_all, dvec_all, dpool_all, dws_all, dbs_all = stack(3), stack(4), stack(5), stack(6), stack(7)
    loss = summed[-1][0]
    chip_cols = lambda a: lax.dynamic_slice_in_dim(a, chip * (C // N_CHIPS), C // N_CHIPS, axis=2)
    n_pool = pool_w.shape[1]
    grad.update(
        ffn1_norm=stack(0), mix_norm=stack(1), ffn2_norm=stack(2), final_norm=summed[-2],
        conf_conv_w=chip_cols(dcw_all[:, :CONF_KERNEL]), conf_conv_b=dvec_all[:, 0], conf_ln_g=dvec_all[:, 1],
        conf_ln_b=dvec_all[:, 2], pool_scale=dvec_all[:, 3], gmlp_ln_g=dvec_all[:, 4], gmlp_ln_b=dvec_all[:, 5],
        sconv_w=chip_cols(dvec_all[:, 6:6 + SHORT_KERNEL]),
        pool_w=jnp.stack([dpool_all[:, g * POOL_GROUP:(g + 1) * POOL_GROUP, g * POOL_GROUP:(g + 1) * POOL_GROUP]
                          for g in range(n_pool)], axis=1),
        gmlp_w_s=dws_all.reshape(n_l, N_HEADS, CHUNK, CHUNK), gmlp_b_s=dbs_all[:, :N_HEADS],
    )

    delta, new_m, new_v = {}, {}, {}
    for name in BIG:
        shp = w[name].shape
        out = _adamw(_as2d(w[name]), _as2d(grad[name]), _as2d(mom[name]), _as2d(var[name]))
        delta[name], new_m[name], new_v[name] = (o.reshape(shp) for o in out)
    ds, ms, vs = _adamw_small([_as2d(w[k]) if w[k].ndim > 1 else w[k][None] for k in SMALL],
                              [_as2d(grad[k]) if grad[k].ndim > 1 else grad[k][None] for k in SMALL],
                              [_as2d(mom[k]) if mom[k].ndim > 1 else mom[k][None] for k in SMALL],
                              [_as2d(var[k]) if var[k].ndim > 1 else var[k][None] for k in SMALL])
    for k, dl, mo, vo in zip(SMALL, ds, ms, vs):
        delta[k], new_m[k], new_v[k] = dl.reshape(w[k].shape), mo.reshape(w[k].shape), vo.reshape(w[k].shape)

    return (loss, grad_x, *[grad[k] for k in ORDER], *[delta[k] for k in ORDER], *[new_m[k] for k in ORDER],
            *[new_v[k] for k in ORDER])
```

```python
import functools

import jax
import jax.numpy as jnp
from jax import lax
from jax.experimental import pallas as pl
from jax.experimental.pallas import tpu as pltpu

F32 = jnp.float32
BF16 = jnp.bfloat16
MESH = pl.DeviceIdType.MESH
ANY = pl.BlockSpec(memory_space=pl.ANY)

EPS = 1e-6
FFN_RESIDUAL = 0.5
D_GROUP = 256
CONF_KERNEL = 31
SHORT_KERNEL = 3
POOL_GROUP = 64
CHUNK = 128
N_HEADS = 4
HEAD_DIM = 64
HALO = 32
N_CHIPS = 4
N_DEV = 8
LANES = 128
MXU_TILE = 256
VMEM_LIMIT = 56 * 2**20
T_MIX_FWD = 512
T_MIX_BWD = 256

ADAM_LR = 0.001
ADAM_B1 = 0.9
ADAM_B2 = 0.999
ADAM_EPS = 1e-08
ADAM_WD = 0.01
ADAM_STEP = 10

NT = (((1,), (1,)), ((), ()))


def _params(*sem):
    return pltpu.CompilerParams(dimension_semantics=sem, vmem_limit_bytes=VMEM_LIMIT)


def _dot(a, b):
    return jnp.dot(a, b, preferred_element_type=F32)


def _dot_nt(a, b):
    return lax.dot_general(a, b, NT, preferred_element_type=F32)


def _t_bf16(v):
    return jnp.transpose(v).astype(BF16)


def _sigmoid(v):
    return 1.0 / (1.0 + jnp.exp(-v))


def _rms(x, g):
    r = lax.rsqrt(jnp.mean(x * x, axis=-1, keepdims=True) + EPS)
    n = x * r
    return n, r, n * g


def _rms_bwd(dh, n, r, g):
    dn = dh * g
    dx = r * (dn - n * jnp.mean(dn * n, axis=-1, keepdims=True))
    return dx, jnp.sum(dh * n, axis=0, keepdims=True)


def _ln_fwd(z, g, b):
    mu = jnp.mean(z, axis=-1, keepdims=True)
    zc = z - mu
    rs = lax.rsqrt(jnp.mean(zc * zc, axis=-1, keepdims=True) + EPS)
    zn = zc * rs
    return zn, rs, zn * g + b


def _ln_bwd(dl, zn, rs, g):
    dzn = dl * g
    dz = rs * (dzn - jnp.mean(dzn, axis=-1, keepdims=True) - zn * jnp.mean(dzn * zn, axis=-1, keepdims=True))
    return dz, jnp.sum(dl * zn, axis=0, keepdims=True), jnp.sum(dl, axis=0, keepdims=True)


def _tile(n, want):
    return want if n % want == 0 else n


def _resident(shape):
    return pl.BlockSpec(shape, lambda i: (0,) * len(shape), pipeline_mode=pl.Buffered(1))


def _ffn_fwd(x, g, w1t, w3t, w2, shards=()):
    s, d = x.shape
    f = w1t.shape[0]
    tm, tf = _tile(s, 512), MXU_TILE
    nj = f // tf
    ni = s // tm
    n_c = len(shards)

    def body(*refs):
        x_ref, g_ref, w1_ref, w3_ref, w2_ref = refs[:5]
        xo_ref, a_ref, b_ref = refs[5 + n_c:8 + n_c]
        u_s = refs[8 + 2 * n_c]
        if n_c:
            gather = lambda: _ici_gather_copies(refs[5:5 + n_c], refs[8 + n_c:8 + 2 * n_c], refs[9 + 2 * n_c], refs[10 + 2 * n_c])

            @pl.when(pl.program_id(0) == 0)
            def _():
                for cp in gather():
                    cp.start()

        h = _rms(x_ref[...], g_ref[...])[2].astype(BF16)
        for j in range(nj):
            cols = slice(j * tf, (j + 1) * tf)
            a = _dot_nt(h, w1_ref[cols, :])
            b = _dot_nt(h, w3_ref[cols, :])
            a_ref[:, cols] = a.astype(BF16)
            b_ref[:, cols] = b.astype(BF16)
            u_s[:, cols] = ((a * _sigmoid(a)) * b).astype(BF16)
        xo_ref[...] = x_ref[...] + FFN_RESIDUAL * _dot(u_s[...], w2_ref[...])
        if n_c:
            @pl.when(pl.program_id(0) == ni - 1)
            def _():
                _wait_all(gather())

    row = pl.BlockSpec((tm, d), lambda i: (i, 0))
    hid = pl.BlockSpec((tm, f), lambda i: (i, 0))
    sems = [pltpu.SemaphoreType.DMA((3 * n_c,))] * 2 if n_c else []
    out = pl.pallas_call(
        body, name="ffn_fwd_gather" if n_c else "ffn_fwd", grid=(ni,),
        in_specs=[row, _resident(g.shape), _resident(w1t.shape), _resident(w3t.shape), _resident(w2.shape)] + [ANY] * n_c,
        out_specs=[row, hid, hid] + [ANY] * n_c,
        out_shape=[jax.ShapeDtypeStruct((s, d), F32), jax.ShapeDtypeStruct((s, f), BF16), jax.ShapeDtypeStruct((s, f), BF16)]
        + _landing_shapes(shards),
        scratch_shapes=[pltpu.VMEM((tm, f), BF16)] + sems,
        compiler_params=_params("arbitrary"),
    )(x, g, w1t, w3t, w2, *shards)
    return out[0], out[1], out[2], list(out[3:])


def _ffn_bwd(dxo, x, g, a, b, w1t, w3t, w2, sums=None):
    s, d = x.shape
    f = w1t.shape[0]
    tm, tf = _tile(s, 256), MXU_TILE
    nj = f // tf
    ni = s // tm
    n_c = 0 if sums is None else 1

    def body(*refs):
        dxo_ref, x_ref, g_ref, a_ref, b_ref, w1_ref, w3_ref, w2_ref = refs[:8]
        dx_ref, dg_ref, h_ref, dy_ref, dat_ref, dbt_ref, ut_ref = refs[8 + n_c:15 + n_c]
        da_s, db_s = refs[15 + 2 * n_c:17 + 2 * n_c]
        if n_c:
            exchange = lambda: _ici_reduce_copies(refs[8], refs[15 + n_c], *refs[17 + 2 * n_c:])

        @pl.when(pl.program_id(0) == 0)
        def _():
            dg_ref[...] = jnp.zeros_like(dg_ref)
            if n_c:
                copies, local = exchange()
                local.start()
                for cp in copies:
                    cp.start()

        h_ref[...] = _rms(x_ref[...], g_ref[...])[2].astype(BF16)
        dy = (FFN_RESIDUAL * dxo_ref[...]).astype(BF16)
        dy_ref[...] = dy
        for j in range(nj):
            cols = slice(j * tf, (j + 1) * tf)
            du = _dot_nt(dy, w2_ref[cols, :])
            av = a_ref[:, cols].astype(F32)
            bv = b_ref[:, cols].astype(F32)
            sg = _sigmoid(av)
            sl = av * sg
            da = du * bv * (sg * (1.0 + av * (1.0 - sg)))
            db = du * sl
            da_s[:, cols] = da.astype(BF16)
            db_s[:, cols] = db.astype(BF16)
            dat_ref[cols, :] = _t_bf16(da)
            dbt_ref[cols, :] = _t_bf16(db)
            ut_ref[cols, :] = _t_bf16(sl * bv)
        dh = _dot(da_s[...], w1_ref[...]) + _dot(db_s[...], w3_ref[...])
        n, r, _ = _rms(x_ref[...], g_ref[...])
        dxr, dg = _rms_bwd(dh, n, r, g_ref[...])
        dx_ref[...] = dxo_ref[...] + dxr
        dg_ref[0:1, :] += dg
        if n_c:
            @pl.when(pl.program_id(0) == ni - 1)
            def _():
                copies, local = exchange()
                _wait_all(copies)
                local.wait()

    row = pl.BlockSpec((tm, d), lambda i: (i, 0))
    hid = pl.BlockSpec((tm, f), lambda i: (i, 0))
    hid_t = pl.BlockSpec((f, tm), lambda i: (0, i))
    extra = [] if sums is None else [sums]
    sems = [pltpu.SemaphoreType.DMA((3,)), pltpu.SemaphoreType.DMA((3,)), pltpu.SemaphoreType.DMA] if n_c else []
    out = pl.pallas_call(
        body, name="ffn_bwd_reduce" if n_c else "ffn_bwd", grid=(ni,),
        in_specs=[row, row, _resident(g.shape), hid, hid, _resident(w1t.shape), _resident(w3t.shape), _resident(w2.shape)]
        + [ANY] * n_c,
        out_specs=[row, pl.BlockSpec((8, d), lambda i: (0, 0)), row, row, hid_t, hid_t, hid_t] + [ANY] * n_c,
        out_shape=[jax.ShapeDtypeStruct((s, d), F32), jax.ShapeDtypeStruct((8, d), F32),
                   jax.ShapeDtypeStruct((s, d), BF16), jax.ShapeDtypeStruct((s, d), BF16),
                   jax.ShapeDtypeStruct((f, s), BF16), jax.ShapeDtypeStruct((f, s), BF16), jax.ShapeDtypeStruct((f, s), BF16)]
        + [jax.ShapeDtypeStruct(e.shape, e.dtype) for e in extra],
        scratch_shapes=[pltpu.VMEM((tm, f), BF16), pltpu.VMEM((tm, f), BF16)] + sems,
        compiler_params=_params("arbitrary"),
    )(dxo, x, g, a, b, w1t, w3t, w2, *extra)
    return out


def _dw(at, bm):
    r, s = at.shape
    n = bm.shape[1]
    rb = r // N_CHIPS
    ts = _tile(s, 1024)
    ni = s // ts

    def body(at_ref, bm_ref, o_ref, acc_s):
        i = pl.program_id(1)

        @pl.when(i == 0)
        def _():
            acc_s[...] = jnp.zeros_like(acc_s)

        acc_s[...] += _dot(at_ref[...], bm_ref[...])

        @pl.when(i == ni - 1)
        def _():
            o_ref[...] = acc_s[...].astype(BF16)

    return pl.pallas_call(
        body, name="dw", grid=(r // rb, ni),
        in_specs=[pl.BlockSpec((rb, ts), lambda k, i: (k, i)), pl.BlockSpec((ts, n), lambda k, i: (i, 0))],
        out_specs=pl.BlockSpec((rb, n), lambda k, i: (k, 0)),
        out_shape=jax.ShapeDtypeStruct((r, n), BF16),
        scratch_shapes=[pltpu.VMEM((rb, n), F32)],
        compiler_params=_params("arbitrary", "arbitrary"),
    )(at, bm)


def _proj_fwd(x, g, w_int):
    s, d = x.shape
    f = w_int.shape[0]
    tm, tf = _tile(s, 1024), 512
    nj = f // tf

    def body(x_ref, g_ref, w_ref, p_ref, h_s):
        @pl.when(pl.program_id(1) == 0)
        def _():
            h_s[...] = _rms(x_ref[...], g_ref[...])[2].astype(BF16)

        p_ref[...] = _dot_nt(h_s[...], w_ref[...])

    return pl.pallas_call(
        body, name="proj_fwd", grid=(s // tm, nj),
        in_specs=[pl.BlockSpec((tm, d), lambda i, j: (i, 0)), pl.BlockSpec((1, d), lambda i, j: (0, 0)),
                  pl.BlockSpec((tf, d), lambda i, j: (j, 0))],
        out_specs=pl.BlockSpec((tm, tf), lambda i, j: (i, j)),
        out_shape=jax.ShapeDtypeStruct((s, f), F32),
        scratch_shapes=[pltpu.VMEM((tm, d), BF16)],
        compiler_params=_params("arbitrary", "arbitrary"),
    )(x, g, w_int)


def _proj_bwd(dxo, x, g, dp, w_int):
    s, d = x.shape
    f = w_int.shape[0]
    tm = _tile(s, 512)

    def body(dxo_ref, x_ref, g_ref, dp_ref, w_ref, dx_ref, dg_ref, h_ref, dpt_ref):
        @pl.when(pl.program_id(0) == 0)
        def _():
            dg_ref[...] = jnp.zeros_like(dg_ref)

        n, r, h = _rms(x_ref[...], g_ref[...])
        h_ref[...] = h.astype(BF16)
        dpv = dp_ref[...]
        dpt_ref[...] = _t_bf16(dpv.astype(F32))
        dxr, dg = _rms_bwd(_dot(dpv, w_ref[...]), n, r, g_ref[...])
        dx_ref[...] = dxo_ref[...] + dxr
        dg_ref[0:1, :] += dg

    row = pl.BlockSpec((tm, d), lambda i: (i, 0))
    return pl.pallas_call(
        body, name="proj_bwd", grid=(s // tm,),
        in_specs=[row, row, pl.BlockSpec((1, d), lambda i: (0, 0)), pl.BlockSpec((tm, f), lambda i: (i, 0)),
                  pl.BlockSpec((f, d), lambda i: (0, 0))],
        out_specs=[row, pl.BlockSpec((8, d), lambda i: (0, 0)), row, pl.BlockSpec((f, tm), lambda i: (0, i))],
        out_shape=[jax.ShapeDtypeStruct((s, d), F32), jax.ShapeDtypeStruct((8, d), F32),
                   jax.ShapeDtypeStruct((s, d), BF16), jax.ShapeDtypeStruct((f, s), BF16)],
        compiler_params=_params("arbitrary"),
    )(dxo, x, g, dp, w_int)


C = D_GROUP


def _piece(ref, k):
    return ref[:, k * C:(k + 1) * C]


def _up(v, r):
    return v if r == 0 else pltpu.roll(v, v.shape[0] - r, 0)


def _down(v, r):
    return v if r == 0 else pltpu.roll(v, r, 0)


def _lane_group():
    lane = lax.broadcasted_iota(jnp.int32, (1, C), 1)
    return (lane >= POOL_GROUP).astype(jnp.int32) + (lane >= 2 * POOL_GROUP).astype(jnp.int32) + (
        lane >= 3 * POOL_GROUP).astype(jnp.int32)


def _by_group(grp, v2, v4, v8, v16):
    return jnp.where(grp == 0, v2, jnp.where(grp == 1, v4, jnp.where(grp == 2, v8, v16)))


def _pool_count(grp, row0, t):
    pos = (row0 + lax.broadcasted_iota(jnp.int32, (t, C), 0) + 1).astype(F32)
    return jnp.minimum(pos, _by_group(grp, 2.0, 4.0, 8.0, 16.0))


def _trailing_sums(ext, grp, t):
    s2 = ext + _down(ext, 1)
    s4 = s2 + _down(s2, 2)
    s8 = s4 + _down(s4, 4)
    s16 = s8 + _down(s8, 8)
    return _by_group(grp, s2, s4, s8, s16)[HALO:HALO + t]


def _leading_sums(ext, grp, t):
    s2 = ext + _up(ext, 1)
    s4 = s2 + _up(s2, 2)
    s8 = s4 + _up(s4, 4)
    s16 = s8 + _up(s8, 8)
    return _by_group(grp, s2, s4, s8, s16)[0:t]


def _head_select(r4, grp):
    out = jnp.where(grp == 0, r4[0:CHUNK], 0.0)
    for h in range(1, N_HEADS):
        out = out + jnp.where(grp == h, r4[h * CHUNK:(h + 1) * CHUNK], 0.0)
    return out


def _conv_taps():
    return [(k, (k + 2) % 8, (k + 2) - (k + 2) % 8) for k in range(CONF_KERNEL)]


def _mixer_fwd(p, x1, cw, vec, pool_w, wstack, bias, w_out):
    s, d = x1.shape
    t = _tile(s, T_MIX_FWD)
    n_ext = t + HALO
    dm = w_out.shape[0]

    def body(p_ref, x1_ref, cw_ref, vec_ref, pw_ref, ws_ref, bias_ref, wo_ref, x2_ref, mt_ref, cy_s, cq_s, cx_s, mix_s):
        i = pl.program_id(0)

        @pl.when(i == 0)
        def _():
            cy_s[...] = jnp.zeros_like(cy_s)
            cq_s[...] = jnp.zeros_like(cq_s)
            cx_s[...] = jnp.zeros_like(cx_s)

        grp = _lane_group()
        y = _piece(p_ref, 0) * _sigmoid(_piece(p_ref, 1))
        ext = jnp.concatenate([cy_s[...], y], axis=0)
        cy_s[...] = y[t - HALO:t]
        z = jnp.broadcast_to(vec_ref[0:1, :], (t, C))
        shifted = {}
        for k, r, off in _conv_taps():
            if r not in shifted:
                shifted[r] = _up(ext, r)
            z = z + cw_ref[k:k + 1, :] * shifted[r][off:off + t]
        ln = _ln_fwd(z, vec_ref[1:2, :], vec_ref[2:3, :])[2]
        mix_s[:, 0:C] = (ln * _sigmoid(ln)).astype(BF16)
        q = _piece(p_ref, 3) * _piece(p_ref, 4)
        ext = jnp.concatenate([cq_s[...], q], axis=0)
        cq_s[...] = q[t - HALO:t]
        cz = vec_ref[8:9, :] * q + vec_ref[7:8, :] * _down(ext, 1)[HALO:] + vec_ref[6:7, :] * _down(ext, 2)[HALO:]
        mix_s[:, C:2 * C] = (_piece(p_ref, 2) * cz).astype(BF16)
        xp = _piece(p_ref, 5)
        ext = jnp.concatenate([cx_s[...], xp], axis=0)
        cx_s[...] = xp[t - HALO:t]
        dd = _trailing_sums(ext, grp, t) / _pool_count(grp, i * t, t) - xp
        mix_s[:, 2 * C:3 * C] = (_dot(dd.astype(BF16), pw_ref[...]) * vec_ref[3:4, :]).astype(BF16)
        vln = _ln_fwd(_piece(p_ref, 7), vec_ref[4:5, :], vec_ref[5:6, :])[2].astype(BF16)
        for n in range(t // CHUNK):
            rows = slice(n * CHUNK, (n + 1) * CHUNK)
            mixed = _head_select(_dot(ws_ref[...], vln[rows]), grp) + bias_ref[...]
            mix_s[rows, 3 * C:4 * C] = (p_ref[rows, 6 * C:7 * C] * mixed).astype(BF16)
        mix = mix_s[...]
        x2_ref[...] = x1_ref[...] + _dot(mix, wo_ref[...])
        mt_ref[...] = _t_bf16(mix.astype(F32))

    full = lambda a: pl.BlockSpec(a.shape, lambda i: (0, 0))
    return pl.pallas_call(
        body, name="mixer_fwd", grid=(s // t,),
        in_specs=[pl.BlockSpec((t, p.shape[1]), lambda i: (i, 0)), pl.BlockSpec((t, d), lambda i: (i, 0)),
                  full(cw), full(vec), full(pool_w), full(wstack), full(bias), full(w_out)],
        out_specs=[pl.BlockSpec((t, d), lambda i: (i, 0)), pl.BlockSpec((dm, t), lambda i: (0, i))],
        out_shape=[jax.ShapeDtypeStruct((s, d), F32), jax.ShapeDtypeStruct((dm, s), BF16)],
        scratch_shapes=[pltpu.VMEM((HALO, C), F32)] * 3 + [pltpu.VMEM((t, dm), BF16)],
        compiler_params=_params("arbitrary"),
    )(p, x1, cw, vec, pool_w, wstack, bias, w_out)


def _mixer_bwd(dx2, p, cw, vec, pool_w, wstack, wstack_t, bias, tril4, head_rows, w_out):
    s, d = dx2.shape
    t = _tile(s, T_MIX_BWD)
    nt = s // t
    n_ext = t + HALO
    hb = t // HALO

    def body(dx2_ref, p_ref, ph_ref, cw_ref, vec_ref, pw_ref, ws_ref, wst_ref, bias_ref, tril_ref, hr_ref, wo_ref,
             dp_ref, dcw_ref, dvec_ref, dpool_ref, dws_ref, dbs_ref, cdz_s, cdc_s, cf_s, vy_s, dvl_s, dbias_s):
        i = pl.program_id(0)
        tile = nt - 1 - i

        @pl.when(i == 0)
        def _():
            for ref in (cdz_s, cdc_s, cf_s, dbias_s, dcw_ref, dvec_ref, dpool_ref, dws_ref, dbs_ref):
                ref[...] = jnp.zeros_like(ref)

        grp = _lane_group()
        first = jnp.where(tile > 0, 1.0, 0.0)
        dmix = _dot_nt(dx2_ref[...].astype(BF16), wo_ref[...])
        d_a, d_b, d_c, d_d = (dmix[:, k * C:(k + 1) * C] for k in range(4))

        def acc_vec(row, v):
            dvec_ref[row:row + 1, :] += jnp.sum(v, axis=0, keepdims=True)

        val, gate = _piece(p_ref, 0), _piece(p_ref, 1)
        sgate = _sigmoid(gate)
        y = val * sgate
        y_halo = ph_ref[:, 0:C] * _sigmoid(ph_ref[:, C:2 * C]) * first
        ext = jnp.concatenate([y_halo, y], axis=0)
        for r in range(8):
            vy_s[r] = _up(ext, r)
        z = jnp.broadcast_to(vec_ref[0:1, :], (t, C))
        for k, r, off in _conv_taps():
            z = z + cw_ref[k:k + 1, :] * vy_s[r, off:off + t, :]
        zn, rs, ln = _ln_fwd(z, vec_ref[1:2, :], vec_ref[2:3, :])
        sg = _sigmoid(ln)
        dln = d_a * (sg * (1.0 + ln * (1.0 - sg)))
        dz, dg, db = _ln_bwd(dln, zn, rs, vec_ref[1:2, :])
        dvec_ref[1:2, :] += dg
        dvec_ref[2:3, :] += db
        acc_vec(0, dz)
        for k, r, off in _conv_taps():
            dcw_ref[k:k + 1, :] += jnp.sum(dz * vy_s[r, off:off + t, :], axis=0, keepdims=True)
        ext = jnp.concatenate([dz, cdz_s[...]], axis=0)
        cdz_s[...] = dz[0:HALO]
        dy = jnp.zeros((t, C), F32)
        shifted = {}
        for k in range(CONF_KERNEL):
            m = CONF_KERNEL - 1 - k
            r, off = m % 8, m - m % 8
            if r not in shifted:
                shifted[r] = _up(ext, r)
            dy = dy + cw_ref[k:k + 1, :] * shifted[r][off:off + t]
        dp_ref[:, 0:C] = (dy * sgate).astype(BF16)
        dp_ref[:, C:2 * C] = (dy * val * sgate * (1.0 - sgate)).astype(BF16)

        sb, sc, sx = _piece(p_ref, 2), _piece(p_ref, 3), _piece(p_ref, 4)
        q = sc * sx
        q_halo = ph_ref[:, 3 * C:4 * C] * ph_ref[:, 4 * C:5 * C] * first
        ext = jnp.concatenate([q_halo, q], axis=0)
        q1, q2 = _down(ext, 1)[HALO:], _down(ext, 2)[HALO:]
        cz = vec_ref[8:9, :] * q + vec_ref[7:8, :] * q1 + vec_ref[6:7, :] * q2
        dcz = d_b * sb
        dp_ref[:, 2 * C:3 * C] = (d_b * cz).astype(BF16)
        acc_vec(8, dcz * q)
        acc_vec(7, dcz * q1)
        acc_vec(6, dcz * q2)
        ext = jnp.concatenate([dcz, cdc_s[...]], axis=0)
        cdc_s[...] = dcz[0:HALO]
        dq = vec_ref[8:9, :] * dcz + vec_ref[7:8, :] * _up(ext, 1)[0:t] + vec_ref[6:7, :] * _up(ext, 2)[0:t]
        dp_ref[:, 3 * C:4 * C] = (dq * sx).astype(BF16)
        dp_ref[:, 4 * C:5 * C] = (dq * sc).astype(BF16)

        xp = _piece(p_ref, 5)
        ext = jnp.concatenate([ph_ref[:, 5 * C:6 * C] * first, xp], axis=0)
        cnt = _pool_count(grp, tile * t, t)
        dd = (_trailing_sums(ext, grp, t) / cnt - xp).astype(BF16)
        e2 = _dot(dd, pw_ref[...])
        acc_vec(3, d_c * e2)
        de = (d_c * vec_ref[3:4, :]).astype(BF16)
        dpool_ref[...] += _dot(_t_bf16(dd.astype(F32)), de)
        ddd = _dot_nt(de, pw_ref[...])
        fq = ddd / cnt
        ext = jnp.concatenate([fq, cf_s[...]], axis=0)
        cf_s[...] = fq[0:HALO]
        dp_ref[:, 5 * C:6 * C] = (_leading_sums(ext, grp, t) - ddd).astype(BF16)

        vn, vrs, vlnf = _ln_fwd(_piece(p_ref, 7), vec_ref[4:5, :], vec_ref[5:6, :])
        vln = vlnf.astype(BF16)
        for n in range(t // CHUNK):
            rows = slice(n * CHUNK, (n + 1) * CHUNK)
            mixed = _head_select(_dot(ws_ref[...], vln[rows]), grp) + bias_ref[...]
            dd_n = d_d[rows]
            dp_ref[rows, 6 * C:7 * C] = (dd_n * mixed).astype(BF16)
            dmx = dd_n * p_ref[rows, 6 * C:7 * C]
            dbias_s[...] += dmx
            dmx_b = dmx.astype(BF16)
            dvl_s[rows, :] = _head_select(_dot(wst_ref[...], dmx_b), grp)
            for h in range(N_HEADS):
                hrows = slice(h * CHUNK, (h + 1) * CHUNK)
                dws_ref[hrows, :] += _dot_nt(jnp.where(grp == h, dmx_b, jnp.zeros_like(dmx_b)), vln[rows])
        dvl = dvl_s[...]
        dv, dg, db = _ln_bwd(dvl, vn, vrs, vec_ref[4:5, :])
        dvec_ref[4:5, :] += dg
        dvec_ref[5:6, :] += db
        dp_ref[:, 7 * C:8 * C] = dv.astype(BF16)

        @pl.when(i == nt - 1)
        def _():
            dws_ref[...] = dws_ref[...] * tril_ref[...]
            dbs_ref[...] = lax.dot_general(hr_ref[...], dbias_s[...], NT, precision=lax.Precision.HIGHEST,
                                           preferred_element_type=F32)

    full = lambda a: pl.BlockSpec(a.shape, lambda i: (0, 0))
    acc = lambda shape: pl.BlockSpec(shape, lambda i: (0, 0))
    f = p.shape[1]
    return pl.pallas_call(
        body, name="mixer_bwd", grid=(nt,),
        in_specs=[pl.BlockSpec((t, d), lambda i: (nt - 1 - i, 0)), pl.BlockSpec((t, f), lambda i: (nt - 1 - i, 0)),
                  pl.BlockSpec((HALO, f), lambda i: (jnp.maximum((nt - 1 - i) * hb - 1, 0), 0)),
                  full(cw), full(vec), full(pool_w), full(wstack), full(wstack_t), full(bias), full(tril4), full(head_rows),
                  full(w_out)],
        out_specs=[pl.BlockSpec((t, f), lambda i: (nt - 1 - i, 0)), acc((32, C)), acc((16, C)), acc((C, C)),
                   acc((N_HEADS * CHUNK, CHUNK)), acc((8, CHUNK))],
        out_shape=[jax.ShapeDtypeStruct((s, f), BF16), jax.ShapeDtypeStruct((32, C), F32), jax.ShapeDtypeStruct((16, C), F32),
                   jax.ShapeDtypeStruct((C, C), F32), jax.ShapeDtypeStruct((N_HEADS * CHUNK, CHUNK), F32),
                   jax.ShapeDtypeStruct((8, CHUNK), F32)],
        scratch_shapes=[pltpu.VMEM((HALO, C), F32)] * 3 + [pltpu.VMEM((8, n_ext, C), F32), pltpu.VMEM((t, C), F32),
                                                            pltpu.VMEM((CHUNK, C), F32)],
        compiler_params=_params("arbitrary"),
    )(dx2, p, p, cw, vec, pool_w, wstack, wstack_t, bias, tril4, head_rows, w_out)


def _loss_bwd(x, g, target):
    s, d = x.shape
    tm = _tile(s, 512)

    def body(x_ref, g_ref, t_ref, dx_ref, dg_ref, loss_ref):
        @pl.when(pl.program_id(0) == 0)
        def _():
            dg_ref[...] = jnp.zeros_like(dg_ref)
            loss_ref[...] = jnp.zeros_like(loss_ref)

        n, r, y = _rms(x_ref[...], g_ref[...])
        err = y - t_ref[...]
        loss_ref[...] += 0.5 * jnp.sum(jnp.mean(err * err, axis=-1, keepdims=True), axis=0, keepdims=True)
        dxr, dg = _rms_bwd(err * (1.0 / d), n, r, g_ref[...])
        dx_ref[...] = dxr
        dg_ref[0:1, :] += dg

    row = pl.BlockSpec((tm, d), lambda i: (i, 0))
    return pl.pallas_call(
        body, name="loss_bwd", grid=(s // tm,),
        in_specs=[row, pl.BlockSpec((1, d), lambda i: (0, 0)), row],
        out_specs=[row, pl.BlockSpec((8, d), lambda i: (0, 0)), pl.BlockSpec((8, LANES), lambda i: (0, 0))],
        out_shape=[jax.ShapeDtypeStruct((s, d), F32), jax.ShapeDtypeStruct((8, d), F32), jax.ShapeDtypeStruct((8, LANES), F32)],
        compiler_params=_params("arbitrary"),
    )(x, g, target)


def _adamw_math(w, g, m, v):
    m = ADAM_B1 * m + (1.0 - ADAM_B1) * g
    v = ADAM_B2 * v + (1.0 - ADAM_B2) * (g * g)
    m_hat = m / (1.0 - ADAM_B1 ** ADAM_STEP)
    v_hat = v / (1.0 - ADAM_B2 ** ADAM_STEP)
    return -ADAM_LR * (m_hat / (jnp.sqrt(v_hat) + ADAM_EPS) + ADAM_WD * w), m, v


def _adamw(w, g, m, v):
    r, c = w.shape
    tr = r // 8 if r % 64 == 0 else r

    def body(w_ref, g_ref, m_ref, v_ref, d_ref, mo_ref, vo_ref):
        d_ref[...], mo_ref[...], vo_ref[...] = _adamw_math(w_ref[...], g_ref[...], m_ref[...], v_ref[...])

    blk = pl.BlockSpec((tr, c), lambda i: (i, 0))
    return pl.pallas_call(
        body, name="adamw", grid=(r // tr,), in_specs=[blk] * 4, out_specs=[blk] * 3,
        out_shape=[jax.ShapeDtypeStruct((r, c), F32)] * 3, compiler_params=_params("arbitrary"),
    )(w, g, m, v)


def _adamw_small(ws, gs, ms, vs):
    n = len(ws)

    def body(*refs):
        ins, outs = refs[:4 * n], refs[4 * n:]
        for k in range(n):
            dl, mo, vo = _adamw_math(ins[k][...], ins[n + k][...], ins[2 * n + k][...], ins[3 * n + k][...])
            outs[k][...], outs[n + k][...], outs[2 * n + k][...] = dl, mo, vo

    vm = pl.BlockSpec(memory_space=pltpu.VMEM)
    out = pl.pallas_call(
        body, name="adamw_small", in_specs=[vm] * (4 * n), out_specs=[vm] * (3 * n),
        out_shape=[jax.ShapeDtypeStruct(a.shape, F32) for a in ws] * 3,
        compiler_params=pltpu.CompilerParams(vmem_limit_bytes=VMEM_LIMIT),
    )(*ws, *gs, *ms, *vs)
    return out[:n], out[n:2 * n], out[2 * n:]


def _where_am_i():
    x, y, c = lax.axis_index("x"), lax.axis_index("y"), lax.axis_index("c")
    chips = [(1 - x, y), (x, 1 - y), (1 - x, 1 - y)]
    return x, y, c, chips


def _chip_id(chip):
    return 2 * chip[0] + chip[1]


def _landing_shapes(shards):
    return [jax.ShapeDtypeStruct((3, a.shape[0] // 2, a.shape[1]), a.dtype) for a in shards]


def _ici_gather_copies(ins, lands, send_sems, recv_sems):
    _, _, c, chips = _where_am_i()
    copies = []
    for a, src in enumerate(ins):
        hr = src.shape[0] // 2
        for j, chip in enumerate(chips):
            copies.append(pltpu.make_async_remote_copy(
                src_ref=src.at[pl.ds(c * hr, hr)], dst_ref=lands[a].at[j], send_sem=send_sems.at[3 * a + j],
                recv_sem=recv_sems.at[3 * a + j], device_id=(*chip, c), device_id_type=MESH))
    return copies


def _ici_reduce_copies(s_ref, o_ref, send_sems, recv_sems, local_sem):
    x, y, c, chips = _where_am_i()
    copies = [pltpu.make_async_remote_copy(src_ref=s_ref.at[_chip_id(chip)], dst_ref=o_ref.at[j], send_sem=send_sems.at[j],
                                           recv_sem=recv_sems.at[j], device_id=(*chip, c), device_id_type=MESH)
              for j, chip in enumerate(chips)]
    return copies, pltpu.make_async_copy(s_ref.at[_chip_id((x, y))], o_ref.at[3], local_sem)


def _wait_all(copies):
    for cp in copies:
        cp.wait_recv()
    for cp in copies:
        cp.wait_send()


def _ici_gather(shards):
    n = len(shards)

    def body(*refs):
        copies = _ici_gather_copies(refs[:n], refs[n:2 * n], refs[2 * n], refs[2 * n + 1])
        for cp in copies:
            cp.start()
        _wait_all(copies)

    return pl.pallas_call(
        body, name="ici_gather", in_specs=[ANY] * n, out_specs=[ANY] * n, out_shape=_landing_shapes(shards),
        scratch_shapes=[pltpu.SemaphoreType.DMA((3 * n,))] * 2,
    )(*shards)


SLOTS = 3


def _sibling_gather(shards, lands):
    n = len(shards)
    d = shards[0].shape[1]
    halves = [a.shape[0] // 2 for a in shards]
    hmax = max(halves)
    chunks = [(a, j, halves[a]) for a in range(n) for j in range(3)]
    nc = len(chunks)

    def body(*refs):
        ins, lnd, outs = refs[:n], refs[n:2 * n], refs[2 * n:3 * n]
        sbuf, rbuf, ld_sems, take_sems, send_sems, recv_sems, own_sems, place_sems, credits = refs[3 * n:]
        x, y, c, chips = _where_am_i()
        sibling = (x, y, 1 - c)
        me = _chip_id((x, y))

        def load(i):
            a, j, hr = chunks[i]
            return pltpu.make_async_copy(lnd[a].at[j], sbuf.at[i % SLOTS, pl.ds(0, hr)], ld_sems.at[i % SLOTS])

        def push(i):
            hr, slot = chunks[i][2], i % SLOTS
            return pltpu.make_async_remote_copy(src_ref=sbuf.at[slot, pl.ds(0, hr)], dst_ref=rbuf.at[slot, pl.ds(0, hr)],
                                                send_sem=send_sems.at[slot], recv_sem=recv_sems.at[slot], device_id=sibling,
                                                device_id_type=MESH)

        def take(i):
            a, j, hr = chunks[i]
            return pltpu.make_async_copy(rbuf.at[i % SLOTS, pl.ds(0, hr)],
                                         outs[a].at[_chip_id(chips[j]), pl.ds((1 - c) * hr, hr)], take_sems.at[i % SLOTS])

        local = [pltpu.make_async_copy(ins[a], outs[a].at[me], own_sems.at[a]) for a in range(n)]
        local += [pltpu.make_async_copy(lnd[a].at[j], outs[a].at[_chip_id(chips[j]), pl.ds(c * hr, hr)], place_sems.at[i])
                  for i, (a, j, hr) in enumerate(chunks)]
        for cp in local:
            cp.start()
        for i in range(min(2, nc)):
            load(i).start()
        for i in range(nc):
            if i >= 1:
                push(i - 1).wait_send()
            if i + 2 < nc:
                load(i + 2).start()
            load(i).wait()
            if i >= SLOTS:
                pl.semaphore_wait(credits.at[i % SLOTS], 1)
            push(i).start()
            if i >= 1:
                push(i - 1).wait_recv()
                take(i - 1).start()
            if i >= 2:
                take(i - 2).wait()
                if i - 2 + SLOTS < nc:
                    pl.semaphore_signal(credits.at[(i - 2) % SLOTS], inc=1, device_id=sibling, device_id_type=MESH)
        push(nc - 1).wait_send()
        push(nc - 1).wait_recv()
        take(nc - 1).start()
        for i in range(max(0, nc - 2), nc):
            take(i).wait()
        for cp in local:
            cp.wait()

    dma = pltpu.SemaphoreType.DMA((SLOTS,))
    return pl.pallas_call(
        body, name="sibling_gather", in_specs=[ANY] * (2 * n), out_specs=[ANY] * n,
        out_shape=[jax.ShapeDtypeStruct((N_CHIPS,) + a.shape, a.dtype) for a in shards],
        scratch_shapes=[pltpu.VMEM((SLOTS, hmax, d), BF16), pltpu.VMEM((SLOTS, hmax, d), BF16), dma, dma, dma, dma,
                        pltpu.SemaphoreType.DMA((n,)), pltpu.SemaphoreType.DMA((nc,)), pltpu.SemaphoreType.REGULAR((SLOTS,))],
        compiler_params=pltpu.CompilerParams(vmem_limit_bytes=VMEM_LIMIT),
    )(*shards, *lands)


def _pair_sum(grads):
    n = len(grads)
    halves = [g.shape[1] // 2 for g in grads]
    total, rows = sum(halves), max(halves)
    d = grads[0].shape[2]
    chunks, off = [], 0
    for a in range(n):
        chunks += [(a, k, off, halves[a]) for k in range(N_CHIPS)]
        off += halves[a]
    nc = len(chunks)

    def body(*refs):
        ins, out_ref = refs[:n], refs[n]
        sbuf, rbuf, mbuf, obuf, ls_sems, lm_sems, st_sems, send_sems, recv_sems, credits = refs[n + 1:]
        x, y, c, _ = _where_am_i()
        sibling = (x, y, 1 - c)

        def load_theirs(i):
            a, k, _, hr = chunks[i]
            return pltpu.make_async_copy(ins[a].at[k, pl.ds((1 - c) * hr, hr)], sbuf.at[i % SLOTS, pl.ds(0, hr)], ls_sems.at[i % SLOTS])

        def load_mine(i):
            a, k, _, hr = chunks[i]
            return pltpu.make_async_copy(ins[a].at[k, pl.ds(c * hr, hr)], mbuf.at[i % SLOTS, pl.ds(0, hr)], lm_sems.at[i % SLOTS])

        def push(i):
            hr, slot = chunks[i][3], i % SLOTS
            return pltpu.make_async_remote_copy(src_ref=sbuf.at[slot, pl.ds(0, hr)], dst_ref=rbuf.at[slot, pl.ds(0, hr)],
                                                send_sem=send_sems.at[slot], recv_sem=recv_sems.at[slot], device_id=sibling,
                                                device_id_type=MESH)

        def store(i):
            _, k, o, hr = chunks[i]
            slot = i % SLOTS
            return pltpu.make_async_copy(obuf.at[slot, pl.ds(0, hr)], out_ref.at[k, pl.ds(o, hr)], st_sems.at[slot])

        def start_push(i):
            load_theirs(i).wait()
            if i >= SLOTS:
                pl.semaphore_wait(credits.at[i % SLOTS], 1)
            push(i).start()

        for i in range(min(2, nc)):
            load_theirs(i).start()
            load_mine(i).start()
        start_push(0)
        for i in range(nc):
            hr, slot = chunks[i][3], i % SLOTS
            if i + 2 < nc:
                load_theirs(i + 2).start()
                load_mine(i + 2).start()
            if i + 1 < nc:
                start_push(i + 1)
            push(i).wait_recv()
            push(i).wait_send()
            load_mine(i).wait()
            if i >= SLOTS:
                store(i - SLOTS).wait()
            obuf[slot, 0:hr, :] = (mbuf[slot, 0:hr, :].astype(F32) + rbuf[slot, 0:hr, :].astype(F32)).astype(BF16)
            if i + SLOTS < nc:
                pl.semaphore_signal(credits.at[slot], inc=1, device_id=sibling, device_id_type=MESH)
            store(i).start()
        for i in range(max(0, nc - SLOTS), nc):
            store(i).wait()

    stage = pltpu.VMEM((SLOTS, rows, d), BF16)
    dma = pltpu.SemaphoreType.DMA((SLOTS,))
    return pl.pallas_call(
        body, name="pair_sum", in_specs=[ANY] * n, out_specs=ANY, out_shape=jax.ShapeDtypeStruct((N_CHIPS, total, d), BF16),
        scratch_shapes=[stage, stage, stage, stage, dma, dma, dma, dma, dma, pltpu.SemaphoreType.REGULAR((SLOTS,))],
        compiler_params=pltpu.CompilerParams(vmem_limit_bytes=VMEM_LIMIT),
    )(*grads)


def _reduce_chips(sums):
    def body(s_ref, o_ref, send_sems, recv_sems, local_sem):
        copies, local = _ici_reduce_copies(s_ref, o_ref, send_sems, recv_sems, local_sem)
        local.start()
        for cp in copies:
            cp.start()
        _wait_all(copies)
        local.wait()

    return pl.pallas_call(
        body, name="reduce_chips", in_specs=[ANY], out_specs=ANY, out_shape=jax.ShapeDtypeStruct(sums.shape, BF16),
        scratch_shapes=[pltpu.SemaphoreType.DMA((3,)), pltpu.SemaphoreType.DMA((3,)), pltpu.SemaphoreType.DMA],
    )(sums)


def _sum_share(parts, rows):
    n_l, n = len(parts), len(rows)
    d = parts[0].shape[2]
    halves = [r // 2 for r in rows]
    hmax = max(halves)
    chunks = []
    for l in range(n_l):
        off = 0
        for a in range(n):
            chunks.append((l, a, off, halves[a]))
            off += halves[a]
    nc = len(chunks)

    def body(*refs):
        ins, outs = refs[:n_l], refs[n_l:n_l + n]
        pbuf, obuf, rbuf, ld_sems, keep_sems, take_sems, send_sems, recv_sems, credits = refs[n_l + n:]
        x, y, c, _ = _where_am_i()
        sibling = (x, y, 1 - c)

        def load(i):
            l, _, off, hr = chunks[i]
            return pltpu.make_async_copy(ins[l].at[:, pl.ds(off, hr)], pbuf.at[i % SLOTS, :, pl.ds(0, hr)], ld_sems.at[i % SLOTS])

        def keep(i):
            l, a, _, hr = chunks[i]
            return pltpu.make_async_copy(obuf.at[i % SLOTS, pl.ds(0, hr)], outs[a].at[l, pl.ds(c * hr, hr)], keep_sems.at[i % SLOTS])

        def push(i):
            hr, slot = chunks[i][3], i % SLOTS
            return pltpu.make_async_remote_copy(src_ref=obuf.at[slot, pl.ds(0, hr)], dst_ref=rbuf.at[slot, pl.ds(0, hr)],
                                                send_sem=send_sems.at[slot], recv_sem=recv_sems.at[slot], device_id=sibling,
                                                device_id_type=MESH)

        def take(i):
            l, a, _, hr = chunks[i]
            return pltpu.make_async_copy(rbuf.at[i % SLOTS, pl.ds(0, hr)], outs[a].at[l, pl.ds((1 - c) * hr, hr)],
                                         take_sems.at[i % SLOTS])

        for i in range(min(2, nc)):
            load(i).start()
        for i in range(nc):
            hr, slot = chunks[i][3], i % SLOTS
            if i + 2 < nc:
                load(i + 2).start()
            load(i).wait()
            if i >= SLOTS:
                keep(i - SLOTS).wait()
                push(i - SLOTS).wait_send()
            part = lambda k: pbuf[slot, k, 0:hr, :].astype(F32)
            obuf[slot, 0:hr, :] = ((part(3) + part(0)) + part(1)) + part(2)
            keep(i).start()
            if i >= SLOTS:
                pl.semaphore_wait(credits.at[slot], 1)
            push(i).start()
            if i >= 1:
                push(i - 1).wait_recv()
                take(i - 1).start()
            if i >= 2:
                take(i - 2).wait()
                if i - 2 + SLOTS < nc:
                    pl.semaphore_signal(credits.at[(i - 2) % SLOTS], inc=1, device_id=sibling, device_id_type=MESH)
        push(nc - 1).wait_recv()
        take(nc - 1).start()
        for i in range(max(0, nc - 2), nc):
            take(i).wait()
        for i in range(max(0, nc - SLOTS), nc):
            keep(i).wait()
            push(i).wait_send()

    dma = pltpu.SemaphoreType.DMA((SLOTS,))
    return pl.pallas_call(
        body, name="sum_share", in_specs=[ANY] * n_l, out_specs=[ANY] * n,
        out_shape=[jax.ShapeDtypeStruct((n_l, r, d), F32) for r in rows],
        scratch_shapes=[pltpu.VMEM((SLOTS, N_CHIPS, hmax, d), BF16), pltpu.VMEM((SLOTS, hmax, d), F32),
                        pltpu.VMEM((SLOTS, hmax, d), F32), dma, dma, dma, dma, dma, pltpu.SemaphoreType.REGULAR((SLOTS,))],
        compiler_params=pltpu.CompilerParams(vmem_limit_bytes=VMEM_LIMIT),
    )(*parts)


def _all_gather_small(block, reduce):
    m, n = block.shape

    def body(x_ref, out_ref, *scratch):
        if reduce:
            all_ref, send_sems, recv_sems, local_sem = scratch
        else:
            all_ref = out_ref
            send_sems, recv_sems, local_sem = scratch
        x, y, c, chips = _where_am_i()
        me, sibling = (x, y, c), (x, y, 1 - c)

        def rows(px, py, pc):
            return all_ref.at[pl.ds((4 * px + 2 * py + pc) * m, m), :]

        def copy(k, blk, to, src=None):
            return pltpu.make_async_remote_copy(src_ref=rows(*blk) if src is None else src, dst_ref=rows(*blk),
                                                send_sem=send_sems.at[k], recv_sem=recv_sems.at[k], device_id=to,
                                                device_id_type=MESH)

        mine = pltpu.make_async_copy(x_ref, rows(*me), local_sem)
        mine.start()
        first = [copy(0, me, sibling, src=x_ref)]
        first += [copy(1 + j, me, (*chip, c), src=x_ref) for j, chip in enumerate(chips)]
        for cp in first:
            cp.start()
        passed = [copy(4 + j, (*chip, c), sibling) for j, chip in enumerate(chips)]
        for j, chip in enumerate(chips):
            copy(1 + j, (*chip, c), me).wait_recv()
            passed[j].start()
        copy(0, sibling, me).wait_recv()
        for j, chip in enumerate(chips):
            copy(4 + j, (*chip, 1 - c), me).wait_recv()
        for cp in first + passed:
            cp.wait_send()
        mine.wait()
        if reduce:
            total = all_ref[0:m, :]
            for dev in range(1, N_DEV):
                total = total + all_ref[dev * m:(dev + 1) * m, :]
            out_ref[...] = total

    vm = pl.BlockSpec(memory_space=pltpu.VMEM)
    sems = [pltpu.SemaphoreType.DMA((7,)), pltpu.SemaphoreType.DMA((7,)), pltpu.SemaphoreType.DMA]
    return pl.pallas_call(
        body, name="reduce_small" if reduce else "gather_small", in_specs=[vm], out_specs=vm,
        out_shape=jax.ShapeDtypeStruct((m, n) if reduce else (N_DEV * m, n), F32),
        scratch_shapes=([pltpu.VMEM((N_DEV * m, n), F32)] if reduce else []) + sems,
        compiler_params=pltpu.CompilerParams(vmem_limit_bytes=VMEM_LIMIT),
    )(block)


def _pack(arrays):
    flat = jnp.concatenate([a.reshape(-1) for a in arrays])
    pad = (-flat.shape[0]) % (8 * LANES)
    return jnp.pad(flat, (0, pad)).reshape(-1, LANES)


def _unpack(buf, shapes):
    flat = buf.reshape(-1)
    out, off = [], 0
    for shp in shapes:
        size = 1
        for dim in shp:
            size *= dim
        out.append(flat[off:off + size].reshape(shp))
        off += size
    return out


BIG = ("ffn1_w1", "ffn1_w3", "ffn1_w2", "w_in", "w_out", "ffn2_w1", "ffn2_w3", "ffn2_w2")
TRANSPOSED = ("ffn1_w1", "ffn1_w3", "w_in", "ffn2_w1", "ffn2_w3")
SMALL = ("ffn1_norm", "mix_norm", "conf_conv_w", "conf_conv_b", "conf_ln_g", "conf_ln_b", "sconv_w", "pool_w", "pool_scale",
         "gmlp_ln_g", "gmlp_ln_b", "gmlp_w_s", "gmlp_b_s", "ffn2_norm", "final_norm")
ORDER = ("ffn1_norm", "ffn1_w1", "ffn1_w3", "ffn1_w2", "mix_norm", "w_in", "conf_conv_w", "conf_conv_b", "conf_ln_g", "conf_ln_b",
         "sconv_w", "pool_w", "pool_scale", "gmlp_ln_g", "gmlp_ln_b", "gmlp_w_s", "gmlp_b_s", "w_out", "ffn2_norm", "ffn2_w1",
         "ffn2_w3", "ffn2_w2", "final_norm")


def _as2d(a):
    return a.reshape(-1, a.shape[-1])


def kernel(x, ffn1_norm, ffn1_w1, ffn1_w3, ffn1_w2, mix_norm, w_in, conf_conv_w, conf_conv_b, conf_ln_g, conf_ln_b, sconv_w, pool_w, pool_scale, gmlp_ln_g, gmlp_ln_b, gmlp_w_s, gmlp_b_s, w_out, ffn2_norm, ffn2_w1, ffn2_w3, ffn2_w2, final_norm, loss_target, m_ffn1_norm, m_ffn1_w1, m_ffn1_w3, m_ffn1_w2, m_mix_norm, m_w_in, m_conf_conv_w, m_conf_conv_b, m_conf_ln_g, m_conf_ln_b, m_sconv_w, m_pool_w, m_pool_scale, m_gmlp_ln_g, m_gmlp_ln_b, m_gmlp_w_s, m_gmlp_b_s, m_w_out, m_ffn2_norm, m_ffn2_w1, m_ffn2_w3, m_ffn2_w2, m_final_norm, v_ffn1_norm, v_ffn1_w1, v_ffn1_w3, v_ffn1_w2, v_mix_norm, v_w_in, v_conf_conv_w, v_conf_conv_b, v_conf_ln_g, v_conf_ln_b, v_sconv_w, v_pool_w, v_pool_scale, v_gmlp_ln_g, v_gmlp_ln_b, v_gmlp_w_s, v_gmlp_b_s, v_w_out, v_ffn2_norm, v_ffn2_w1, v_ffn2_w3, v_ffn2_w2, v_final_norm):
    given = dict(locals())
    w = {k: given[k] for k in ORDER}
    mom = {k: given["m_" + k] for k in ORDER}
    var = {k: given["v_" + k] for k in ORDER}
    n_l = ffn1_w1.shape[0]
    xs = x[0]
    d = xs.shape[1]
    chip = 2 * lax.axis_index("x") + lax.axis_index("y")

    def shard(name, l):
        a = w[name][l]
        return (a.T if name in TRANSPOSED else a).astype(BF16)

    def finish_gather(l, lands):
        out = _sibling_gather([shard(name, l) for name in BIG], lands)
        return {name: g.reshape(-1, d) for name, g in zip(BIG, out)}

    half = len(BIG) // 2
    shard_rows = [w[name].shape[2] if name in TRANSPOSED else w[name].shape[1] for name in BIG]

    conv_shapes = [conf_conv_w.shape, sconv_w.shape]
    conv_all = _all_gather_small(_pack([conf_conv_w, sconv_w]), reduce=False)
    conv_all = conv_all.reshape(N_CHIPS, 2, -1)[:, 0]
    conf_full, sconv_full = [jnp.concatenate([_unpack(conv_all[k], conv_shapes)[a] for k in range(N_CHIPS)], axis=-1)
                             for a in range(2)]

    lane = jnp.arange(C) // HEAD_DIM
    head_rows = (jnp.arange(8)[:, None] == lane[None, :]).astype(F32)
    tril = jnp.tril(jnp.ones((CHUNK, CHUNK), F32))
    tril4 = jnp.tile(tril, (N_HEADS, 1))
    mixer_consts = []
    for l in range(n_l):
        cw = jnp.pad(conf_full[l], ((0, 32 - CONF_KERNEL), (0, 0)))
        vec = jnp.concatenate([conf_conv_b[l][None], conf_ln_g[l][None], conf_ln_b[l][None], pool_scale[l][None],
                               gmlp_ln_g[l][None], gmlp_ln_b[l][None], sconv_full[l], jnp.zeros((7, C), F32)], axis=0)
        eye = jnp.eye(len(pool_w[l]), dtype=F32)
        pool_blk = (eye[:, None, :, None] * pool_w[l][:, :, None, :]).reshape(C, C).astype(BF16)
        ws = gmlp_w_s[l] * tril[None]
        wstack = ws.reshape(N_HEADS * CHUNK, CHUNK).astype(BF16)
        wstack_t = jnp.swapaxes(ws, 1, 2).reshape(N_HEADS * CHUNK, CHUNK).astype(BF16)
        bias = jnp.repeat(gmlp_b_s[l].T, HEAD_DIM, axis=1)
        mixer_consts.append((cw, vec, pool_blk, wstack, wstack_t, bias))

    saved = []
    cur = xs
    gathered = [finish_gather(0, _ici_gather([shard(name, 0) for name in BIG]))]
    for l in range(n_l):
        gw = gathered[l]
        cw, vec, pool_blk, wstack, wstack_t, bias = mixer_consts[l]
        nxt = [shard(name, l + 1) for name in BIG] if l + 1 < n_l else []
        x0 = cur
        x1, a1, b1, lands1 = _ffn_fwd(x0, ffn1_norm[l][None], gw["ffn1_w1"], gw["ffn1_w3"], gw["ffn1_w2"], nxt[:half])
        p = _proj_fwd(x1, mix_norm[l][None], gw["w_in"])
        x2, mix_t = _mixer_fwd(p, x1, cw, vec, pool_blk, wstack, bias, gw["w_out"])
        x3, a2, b2, lands2 = _ffn_fwd(x2, ffn2_norm[l][None], gw["ffn2_w1"], gw["ffn2_w3"], gw["ffn2_w2"], nxt[half:])
        if nxt:
            gathered.append(finish_gather(l + 1, lands1 + lands2))
        saved.append((x0, x1, x2, a1, b1, a2, b2, p, mix_t))
        cur = x3

    dx, dg_final, loss_part = _loss_bwd(cur, final_norm[None], loss_target[0])

    small_parts = [None] * n_l
    reduced_halves = [None] * n_l
    pending = None
    for l in reversed(range(n_l)):
        gw = gathered[l]
        cw, vec, pool_blk, wstack, wstack_t, bias = mixer_consts[l]
        x0, x1, x2, a1, b1, a2, b2, p, mix_t = saved[l]
        big = {}
        out = _ffn_bwd(dx, x2, ffn2_norm[l][None], a2, b2, gw["ffn2_w1"], gw["ffn2_w3"], gw["ffn2_w2"], pending)
        dx, dg_ffn2, h, dy, da_t, db_t, u_t = out[:7]
        if pending is not None:
            reduced_halves[l + 1] = out[7]
        big["ffn2_w1"], big["ffn2_w3"], big["ffn2_w2"] = _dw(da_t, h), _dw(db_t, h), _dw(u_t, dy)
        big["w_out"] = _dw(mix_t, dx.astype(BF16))
        dp, dcw, dvec, dpool, dws, dbs = _mixer_bwd(dx, p, cw, vec, pool_blk, wstack, wstack_t, bias, tril4, head_rows, gw["w_out"])
        dx, dg_mix, h, dp_t = _proj_bwd(dx, x1, mix_norm[l][None], dp, gw["w_in"])
        big["w_in"] = _dw(dp_t, h)
        dx, dg_ffn1, h, dy, da_t, db_t, u_t = _ffn_bwd(dx, x0, ffn1_norm[l][None], a1, b1, gw["ffn1_w1"], gw["ffn1_w3"], gw["ffn1_w2"])
        big["ffn1_w1"], big["ffn1_w3"], big["ffn1_w2"] = _dw(da_t, h), _dw(db_t, h), _dw(u_t, dy)
        small_parts[l] = [dg_ffn1[0], dg_mix[0], dg_ffn2[0], dcw, dvec, dpool, dws, dbs]
        pending = _pair_sum([big[name].reshape(N_CHIPS, -1, d) for name in BIG])
    reduced_halves[0] = _reduce_chips(pending)
    grad_x = dx[None]

    full = dict(zip(BIG, _sum_share(reduced_halves, shard_rows)))
    grad = {name: (jnp.swapaxes(full[name], 1, 2) if name in TRANSPOSED else full[name]) for name in BIG}

    part_shapes = [a.shape for a in small_parts[0]]
    tail = [dg_final[0], loss_part[0]]
    packed = _pack([a for l in range(n_l) for a in small_parts[l]] + tail)
    summed = _unpack(_all_gather_small(packed, reduce=True), part_shapes * n_l + [a.shape for a in tail])
    per_layer = [summed[l * len(part_shapes):(l + 1) * len(part_shapes)] for l in range(n_l)]
    stack = lambda k: jnp.stack([per_layer[l][k] for l in range(n_l)])
    dcw_all, dvec_all, dpool_all, dws_all, dbs_all = stack(3), stack(4), stack(5), stack(6), stack(7)
    loss = summed[-1][0]
    chip_cols = lambda a: lax.dynamic_slice_in_dim(a, chip * (C // N_CHIPS), C // N_CHIPS, axis=2)
    n_pool = pool_w.shape[1]
    grad.update(
        ffn1_norm=stack(0), mix_norm=stack(1), ffn2_norm=stack(2), final_norm=summed[-2],
        conf_conv_w=chip_cols(dcw_all[:, :CONF_KERNEL]), conf_conv_b=dvec_all[:, 0], conf_ln_g=dvec_all[:, 1],
        conf_ln_b=dvec_all[:, 2], pool_scale=dvec_all[:, 3], gmlp_ln_g=dvec_all[:, 4], gmlp_ln_b=dvec_all[:, 5],
        sconv_w=chip_cols(dvec_all[:, 6:6 + SHORT_KERNEL]),
        pool_w=jnp.stack([dpool_all[:, g * POOL_GROUP:(g + 1) * POOL_GROUP, g * POOL_GROUP:(g + 1) * POOL_GROUP]
                          for g in range(n_pool)], axis=1),
        gmlp_w_s=dws_all.reshape(n_l, N_HEADS, CHUNK, CHUNK), gmlp_b_s=dbs_all[:, :N_HEADS],
    )

    delta, new_m, new_v = {}, {}, {}
    for name in BIG:
        shp = w[name].shape
        out = _adamw(_as2d(w[name]), _as2d(grad[name]), _as2d(mom[name]), _as2d(var[name]))
        delta[name], new_m[name], new_v[name] = (o.reshape(shp) for o in out)
    ds, ms, vs = _adamw_small([_as2d(w[k]) if w[k].ndim > 1 else w[k][None] for k in SMALL],
                              [_as2d(grad[k]) if grad[k].ndim > 1 else grad[k][None] for k in SMALL],
                              [_as2d(mom[k]) if mom[k].ndim > 1 else mom[k][None] for k in SMALL],
                              [_as2d(var[k]) if var[k].ndim > 1 else var[k][None] for k in SMALL])
    for k, dl, mo, vo in zip(SMALL, ds, ms, vs):
        delta[k], new_m[k], new_v[k] = dl.reshape(w[k].shape), mo.reshape(w[k].shape), vo.reshape(w[k].shape)

    return (loss, grad_x, *[grad[k] for k in ORDER], *[delta[k] for k in ORDER], *[new_m[k] for k in ORDER],
            *[new_v[k] for k in ORDER])
```

```python
import functools

import jax
import jax.numpy as jnp
from jax import lax
from jax.experimental import pallas as pl
from jax.experimental.pallas import tpu as pltpu

F32 = jnp.float32
BF16 = jnp.bfloat16
MESH = pl.DeviceIdType.MESH
ANY = pl.BlockSpec(memory_space=pl.ANY)

EPS = 1e-6
FFN_RESIDUAL = 0.5
D_GROUP = 256
CONF_KERNEL = 31
SHORT_KERNEL = 3
POOL_GROUP = 64
CHUNK = 128
N_HEADS = 4
HEAD_DIM = 64
HALO = 32
N_CHIPS = 4
N_DEV = 8
LANES = 128
MXU_TILE = 256
VMEM_LIMIT = 56 * 2**20
T_MIX_FWD = 512
T_MIX_BWD = 256

ADAM_LR = 0.001
ADAM_B1 = 0.9
ADAM_B2 = 0.999
ADAM_EPS = 1e-08
ADAM_WD = 0.01
ADAM_STEP = 10

NT = (((1,), (1,)), ((), ()))


def _params(*sem):
    return pltpu.CompilerParams(dimension_semantics=sem, vmem_limit_bytes=VMEM_LIMIT)


def _dot(a, b):
    return jnp.dot(a, b, preferred_element_type=F32)


def _dot_nt(a, b):
    return lax.dot_general(a, b, NT, preferred_element_type=F32)


def _t_bf16(v):
    return jnp.transpose(v).astype(BF16)


def _sigmoid(v):
    return 1.0 / (1.0 + jnp.exp(-v))


def _rms(x, g):
    r = lax.rsqrt(jnp.mean(x * x, axis=-1, keepdims=True) + EPS)
    n = x * r
    return n, r, n * g


def _rms_bwd(dh, n, r, g):
    dn = dh * g
    dx = r * (dn - n * jnp.mean(dn * n, axis=-1, keepdims=True))
    return dx, jnp.sum(dh * n, axis=0, keepdims=True)


def _ln_fwd(z, g, b):
    mu = jnp.mean(z, axis=-1, keepdims=True)
    zc = z - mu
    rs = lax.rsqrt(jnp.mean(zc * zc, axis=-1, keepdims=True) + EPS)
    zn = zc * rs
    return zn, rs, zn * g + b


def _ln_bwd(dl, zn, rs, g):
    dzn = dl * g
    dz = rs * (dzn - jnp.mean(dzn, axis=-1, keepdims=True) - zn * jnp.mean(dzn * zn, axis=-1, keepdims=True))
    return dz, jnp.sum(dl * zn, axis=0, keepdims=True), jnp.sum(dl, axis=0, keepdims=True)


def _tile(n, want):
    return want if n % want == 0 else n


def _resident(shape):
    return pl.BlockSpec(shape, lambda i: (0,) * len(shape), pipeline_mode=pl.Buffered(1))


def _ffn_fwd(x, g, w1t, w3t, w2, shards=()):
    s, d = x.shape
    f = w1t.shape[0]
    tm, tf = _tile(s, 512), MXU_TILE
    nj = f // tf
    ni = s // tm
    n_c = len(shards)

    def body(*refs):
        x_ref, g_ref, w1_ref, w3_ref, w2_ref = refs[:5]
        xo_ref, a_ref, b_ref = refs[5 + n_c:8 + n_c]
        u_s = refs[8 + 2 * n_c]
        if n_c:
            gather = lambda: _ici_gather_copies(refs[5:5 + n_c], refs[8 + n_c:8 + 2 * n_c], refs[9 + 2 * n_c], refs[10 + 2 * n_c])

            @pl.when(pl.program_id(0) == 0)
            def _():
                for cp in gather():
                    cp.start()

        h = _rms(x_ref[...], g_ref[...])[2].astype(BF16)
        for j in range(nj):
            cols = slice(j * tf, (j + 1) * tf)
            a = _dot_nt(h, w1_ref[cols, :])
            b = _dot_nt(h, w3_ref[cols, :])
            a_ref[:, cols] = a.astype(BF16)
            b_ref[:, cols] = b.astype(BF16)
            u_s[:, cols] = ((a * _sigmoid(a)) * b).astype(BF16)
        xo_ref[...] = x_ref[...] + FFN_RESIDUAL * _dot(u_s[...], w2_ref[...])
        if n_c:
            @pl.when(pl.program_id(0) == ni - 1)
            def _():
                _wait_all(gather())

    row = pl.BlockSpec((tm, d), lambda i: (i, 0))
    hid = pl.BlockSpec((tm, f), lambda i: (i, 0))
    sems = [pltpu.SemaphoreType.DMA((3 * n_c,))] * 2 if n_c else []
    out = pl.pallas_call(
        body, name="ffn_fwd_gather" if n_c else "ffn_fwd", grid=(ni,),
        in_specs=[row, _resident(g.shape), _resident(w1t.shape), _resident(w3t.shape), _resident(w2.shape)] + [ANY] * n_c,
        out_specs=[row, hid, hid] + [ANY] * n_c,
        out_shape=[jax.ShapeDtypeStruct((s, d), F32), jax.ShapeDtypeStruct((s, f), BF16), jax.ShapeDtypeStruct((s, f), BF16)]
        + _landing_shapes(shards),
        scratch_shapes=[pltpu.VMEM((tm, f), BF16)] + sems,
        compiler_params=_params("arbitrary"),
    )(x, g, w1t, w3t, w2, *shards)
    return out[0], out[1], out[2], list(out[3:])


def _ffn_bwd(dxo, x, g, a, b, w1t, w3t, w2, sums=None):
    s, d = x.shape
    f = w1t.shape[0]
    tm, tf = _tile(s, 256), MXU_TILE
    nj = f // tf
    ni = s // tm
    n_c = 0 if sums is None else 1

    def body(*refs):
        dxo_ref, x_ref, g_ref, a_ref, b_ref, w1_ref, w3_ref, w2_ref = refs[:8]
        dx_ref, dg_ref, h_ref, dy_ref, dat_ref, dbt_ref, ut_ref = refs[8 + n_c:15 + n_c]
        da_s, db_s = refs[15 + 2 * n_c:17 + 2 * n_c]
        if n_c:
            exchange = lambda: _ici_reduce_copies(refs[8], refs[15 + n_c], *refs[17 + 2 * n_c:])

        @pl.when(pl.program_id(0) == 0)
        def _():
            dg_ref[...] = jnp.zeros_like(dg_ref)
            if n_c:
                copies, local = exchange()
                local.start()
                for cp in copies:
                    cp.start()

        h_ref[...] = _rms(x_ref[...], g_ref[...])[2].astype(BF16)
        dy = (FFN_RESIDUAL * dxo_ref[...]).astype(BF16)
        dy_ref[...] = dy
        for j in range(nj):
            cols = slice(j * tf, (j + 1) * tf)
            du = _dot_nt(dy, w2_ref[cols, :])
            av = a_ref[:, cols].astype(F32)
            bv = b_ref[:, cols].astype(F32)
            sg = _sigmoid(av)
            sl = av * sg
            da = du * bv * (sg * (1.0 + av * (1.0 - sg)))
            db = du * sl
            da_s[:, cols] = da.astype(BF16)
            db_s[:, cols] = db.astype(BF16)
            dat_ref[cols, :] = _t_bf16(da)
            dbt_ref[cols, :] = _t_bf16(db)
            ut_ref[cols, :] = _t_bf16(sl * bv)
        dh = _dot(da_s[...], w1_ref[...]) + _dot(db_s[...], w3_ref[...])
        n, r, _ = _rms(x_ref[...], g_ref[...])
        dxr, dg = _rms_bwd(dh, n, r, g_ref[...])
        dx_ref[...] = dxo_ref[...] + dxr
        dg_ref[0:1, :] += dg
        if n_c:
            @pl.when(pl.program_id(0) == ni - 1)
            def _():
                copies, local = exchange()
                _wait_all(copies)
                local.wait()

    row = pl.BlockSpec((tm, d), lambda i: (i, 0))
    hid = pl.BlockSpec((tm, f), lambda i: (i, 0))
    hid_t = pl.BlockSpec((f, tm), lambda i: (0, i))
    extra = [] if sums is None else [sums]
    sems = [pltpu.SemaphoreType.DMA((3,)), pltpu.SemaphoreType.DMA((3,)), pltpu.SemaphoreType.DMA] if n_c else []
    out = pl.pallas_call(
        body, name="ffn_bwd_reduce" if n_c else "ffn_bwd", grid=(ni,),
        in_specs=[row, row, _resident(g.shape), hid, hid, _resident(w1t.shape), _resident(w3t.shape), _resident(w2.shape)]
        + [ANY] * n_c,
        out_specs=[row, pl.BlockSpec((8, d), lambda i: (0, 0)), row, row, hid_t, hid_t, hid_t] + [ANY] * n_c,
        out_shape=[jax.ShapeDtypeStruct((s, d), F32), jax.ShapeDtypeStruct((8, d), F32),
                   jax.ShapeDtypeStruct((s, d), BF16), jax.ShapeDtypeStruct((s, d), BF16),
                   jax.ShapeDtypeStruct((f, s), BF16), jax.ShapeDtypeStruct((f, s), BF16), jax.ShapeDtypeStruct((f, s), BF16)]
        + [jax.ShapeDtypeStruct(e.shape, e.dtype) for e in extra],
        scratch_shapes=[pltpu.VMEM((tm, f), BF16), pltpu.VMEM((tm, f), BF16)] + sems,
        compiler_params=_params("arbitrary"),
    )(dxo, x, g, a, b, w1t, w3t, w2, *extra)
    return out


def _dw(at, bm):
    r, s = at.shape
    n = bm.shape[1]
    rb = r // N_CHIPS
    ts = _tile(s, 1024)
    ni = s // ts

    def body(at_ref, bm_ref, o_ref, acc_s):
        i = pl.program_id(1)

        @pl.when(i == 0)
        def _():
            acc_s[...] = jnp.zeros_like(acc_s)

        acc_s[...] += _dot(at_ref[...], bm_ref[...])

        @pl.when(i == ni - 1)
        def _():
            o_ref[...] = acc_s[...].astype(BF16)

    return pl.pallas_call(
        body, name="dw", grid=(r // rb, ni),
        in_specs=[pl.BlockSpec((rb, ts), lambda k, i: (k, i)), pl.BlockSpec((ts, n), lambda k, i: (i, 0))],
        out_specs=pl.BlockSpec((rb, n), lambda k, i: (k, 0)),
        out_shape=jax.ShapeDtypeStruct((r, n), BF16),
        scratch_shapes=[pltpu.VMEM((rb, n), F32)],
        compiler_params=_params("arbitrary", "arbitrary"),
    )(at, bm)


def _proj_fwd(x, g, w_int):
    s, d = x.shape
    f = w_int.shape[0]
    tm, tf = _tile(s, 1024), 512
    nj = f // tf

    def body(x_ref, g_ref, w_ref, p_ref, h_s):
        @pl.when(pl.program_id(1) == 0)
        def _():
            h_s[...] = _rms(x_ref[...], g_ref[...])[2].astype(BF16)

        p_ref[...] = _dot_nt(h_s[...], w_ref[...])

    return pl.pallas_call(
        body, name="proj_fwd", grid=(s // tm, nj),
        in_specs=[pl.BlockSpec((tm, d), lambda i, j: (i, 0)), pl.BlockSpec((1, d), lambda i, j: (0, 0)),
                  pl.BlockSpec((tf, d), lambda i, j: (j, 0))],
        out_specs=pl.BlockSpec((tm, tf), lambda i, j: (i, j)),
        out_shape=jax.ShapeDtypeStruct((s, f), F32),
        scratch_shapes=[pltpu.VMEM((tm, d), BF16)],
        compiler_params=_params("arbitrary", "arbitrary"),
    )(x, g, w_int)


def _proj_bwd(dxo, x, g, dp, w_int):
    s, d = x.shape
    f = w_int.shape[0]
    tm = _tile(s, 512)

    def body(dxo_ref, x_ref, g_ref, dp_ref, w_ref, dx_ref, dg_ref, h_ref, dpt_ref):
        @pl.when(pl.program_id(0) == 0)
        def _():
            dg_ref[...] = jnp.zeros_like(dg_ref)

        n, r, h = _rms(x_ref[...], g_ref[...])
        h_ref[...] = h.astype(BF16)
        dpv = dp_ref[...]
        dpt_ref[...] = _t_bf16(dpv.astype(F32))
        dxr, dg = _rms_bwd(_dot(dpv, w_ref[...]), n, r, g_ref[...])
        dx_ref[...] = dxo_ref[...] + dxr
        dg_ref[0:1, :] += dg

    row = pl.BlockSpec((tm, d), lambda i: (i, 0))
    return pl.pallas_call(
        body, name="proj_bwd", grid=(s // tm,),
        in_specs=[row, row, pl.BlockSpec((1, d), lambda i: (0, 0)), pl.BlockSpec((tm, f), lambda i: (i, 0)),
                  pl.BlockSpec((f, d), lambda i: (0, 0))],
        out_specs=[row, pl.BlockSpec((8, d), lambda i: (0, 0)), row, pl.BlockSpec((f, tm), lambda i: (0, i))],
        out_shape=[jax.ShapeDtypeStruct((s, d), F32), jax.ShapeDtypeStruct((8, d), F32),
                   jax.ShapeDtypeStruct((s, d), BF16), jax.ShapeDtypeStruct((f, s), BF16)],
        compiler_params=_params("arbitrary"),
    )(dxo, x, g, dp, w_int)


C = D_GROUP


def _piece(ref, k):
    return ref[:, k * C:(k + 1) * C]


def _up(v, r):
    return v if r == 0 else pltpu.roll(v, v.shape[0] - r, 0)


def _down(v, r):
    return v if r == 0 else pltpu.roll(v, r, 0)


def _lane_group():
    lane = lax.broadcasted_iota(jnp.int32, (1, C), 1)
    return (lane >= POOL_GROUP).astype(jnp.int32) + (lane >= 2 * POOL_GROUP).astype(jnp.int32) + (
        lane >= 3 * POOL_GROUP).astype(jnp.int32)


def _by_group(grp, v2, v4, v8, v16):
    return jnp.where(grp == 0, v2, jnp.where(grp == 1, v4, jnp.where(grp == 2, v8, v16)))


def _pool_count(grp, row0, t):
    pos = (row0 + lax.broadcasted_iota(jnp.int32, (t, C), 0) + 1).astype(F32)
    return jnp.minimum(pos, _by_group(grp, 2.0, 4.0, 8.0, 16.0))


def _trailing_sums(ext, grp, t):
    s2 = ext + _down(ext, 1)
    s4 = s2 + _down(s2, 2)
    s8 = s4 + _down(s4, 4)
    s16 = s8 + _down(s8, 8)
    return _by_group(grp, s2, s4, s8, s16)[HALO:HALO + t]


def _leading_sums(ext, grp, t):
    s2 = ext + _up(ext, 1)
    s4 = s2 + _up(s2, 2)
    s8 = s4 + _up(s4, 4)
    s16 = s8 + _up(s8, 8)
    return _by_group(grp, s2, s4, s8, s16)[0:t]


def _head_select(r4, grp):
    out = jnp.where(grp == 0, r4[0:CHUNK], 0.0)
    for h in range(1, N_HEADS):
        out = out + jnp.where(grp == h, r4[h * CHUNK:(h + 1) * CHUNK], 0.0)
    return out


def _conv_taps():
    return [(k, (k + 2) % 8, (k + 2) - (k + 2) % 8) for k in range(CONF_KERNEL)]


def _mixer_fwd(p, x1, cw, vec, pool_w, wstack, bias, w_out):
    s, d = x1.shape
    t = _tile(s, T_MIX_FWD)
    n_ext = t + HALO
    dm = w_out.shape[0]

    def body(p_ref, x1_ref, cw_ref, vec_ref, pw_ref, ws_ref, bias_ref, wo_ref, x2_ref, mt_ref, cy_s, cq_s, cx_s, mix_s):
        i = pl.program_id(0)

        @pl.when(i == 0)
        def _():
            cy_s[...] = jnp.zeros_like(cy_s)
            cq_s[...] = jnp.zeros_like(cq_s)
            cx_s[...] = jnp.zeros_like(cx_s)

        grp = _lane_group()
        y = _piece(p_ref, 0) * _sigmoid(_piece(p_ref, 1))
        ext = jnp.concatenate([cy_s[...], y], axis=0)
        cy_s[...] = y[t - HALO:t]
        z = jnp.broadcast_to(vec_ref[0:1, :], (t, C))
        shifted = {}
        for k, r, off in _conv_taps():
            if r not in shifted:
                shifted[r] = _up(ext, r)
            z = z + cw_ref[k:k + 1, :] * shifted[r][off:off + t]
        ln = _ln_fwd(z, vec_ref[1:2, :], vec_ref[2:3, :])[2]
        mix_s[:, 0:C] = (ln * _sigmoid(ln)).astype(BF16)
        q = _piece(p_ref, 3) * _piece(p_ref, 4)
        ext = jnp.concatenate([cq_s[...], q], axis=0)
        cq_s[...] = q[t - HALO:t]
        cz = vec_ref[8:9, :] * q + vec_ref[7:8, :] * _down(ext, 1)[HALO:] + vec_ref[6:7, :] * _down(ext, 2)[HALO:]
        mix_s[:, C:2 * C] = (_piece(p_ref, 2) * cz).astype(BF16)
        xp = _piece(p_ref, 5)
        ext = jnp.concatenate([cx_s[...], xp], axis=0)
        cx_s[...] = xp[t - HALO:t]
        dd = _trailing_sums(ext, grp, t) / _pool_count(grp, i * t, t) - xp
        mix_s[:, 2 * C:3 * C] = (_dot(dd.astype(BF16), pw_ref[...]) * vec_ref[3:4, :]).astype(BF16)
        vln = _ln_fwd(_piece(p_ref, 7), vec_ref[4:5, :], vec_ref[5:6, :])[2].astype(BF16)
        for n in range(t // CHUNK):
            rows = slice(n * CHUNK, (n + 1) * CHUNK)
            mixed = _head_select(_dot(ws_ref[...], vln[rows]), grp) + bias_ref[...]
            mix_s[rows, 3 * C:4 * C] = (p_ref[rows, 6 * C:7 * C] * mixed).astype(BF16)
        mix = mix_s[...]
        x2_ref[...] = x1_ref[...] + _dot(mix, wo_ref[...])
        mt_ref[...] = _t_bf16(mix.astype(F32))

    full = lambda a: pl.BlockSpec(a.shape, lambda i: (0, 0))
    return pl.pallas_call(
        body, name="mixer_fwd", grid=(s // t,),
        in_specs=[pl.BlockSpec((t, p.shape[1]), lambda i: (i, 0)), pl.BlockSpec((t, d), lambda i: (i, 0)),
                  full(cw), full(vec), full(pool_w), full(wstack), full(bias), full(w_out)],
        out_specs=[pl.BlockSpec((t, d), lambda i: (i, 0)), pl.BlockSpec((dm, t), lambda i: (0, i))],
        out_shape=[jax.ShapeDtypeStruct((s, d), F32), jax.ShapeDtypeStruct((dm, s), BF16)],
        scratch_shapes=[pltpu.VMEM((HALO, C), F32)] * 3 + [pltpu.VMEM((t, dm), BF16)],
        compiler_params=_params("arbitrary"),
    )(p, x1, cw, vec, pool_w, wstack, bias, w_out)


def _mixer_bwd(dx2, p, cw, vec, pool_w, wstack, wstack_t, bias, tril4, head_rows, w_out):
    s, d = dx2.shape
    t = _tile(s, T_MIX_BWD)
    nt = s // t
    n_ext = t + HALO
    hb = t // HALO

    def body(dx2_ref, p_ref, ph_ref, cw_ref, vec_ref, pw_ref, ws_ref, wst_ref, bias_ref, tril_ref, hr_ref, wo_ref,
             dp_ref, dcw_ref, dvec_ref, dpool_ref, dws_ref, dbs_ref, cdz_s, cdc_s, cf_s, vy_s, dvl_s, dbias_s):
        i = pl.program_id(0)
        tile = nt - 1 - i

        @pl.when(i == 0)
        def _():
            for ref in (cdz_s, cdc_s, cf_s, dbias_s, dcw_ref, dvec_ref, dpool_ref, dws_ref, dbs_ref):
                ref[...] = jnp.zeros_like(ref)

        grp = _lane_group()
        first = jnp.where(tile > 0, 1.0, 0.0)
        dmix = _dot_nt(dx2_ref[...].astype(BF16), wo_ref[...])
        d_a, d_b, d_c, d_d = (dmix[:, k * C:(k + 1) * C] for k in range(4))

        def acc_vec(row, v):
            dvec_ref[row:row + 1, :] += jnp.sum(v, axis=0, keepdims=True)

        val, gate = _piece(p_ref, 0), _piece(p_ref, 1)
        sgate = _sigmoid(gate)
        y = val * sgate
        y_halo = ph_ref[:, 0:C] * _sigmoid(ph_ref[:, C:2 * C]) * first
        ext = jnp.concatenate([y_halo, y], axis=0)
        for r in range(8):
            vy_s[r] = _up(ext, r)
        z = jnp.broadcast_to(vec_ref[0:1, :], (t, C))
        for k, r, off in _conv_taps():
            z = z + cw_ref[k:k + 1, :] * vy_s[r, off:off + t, :]
        zn, rs, ln = _ln_fwd(z, vec_ref[1:2, :], vec_ref[2:3, :])
        sg = _sigmoid(ln)
        dln = d_a * (sg * (1.0 + ln * (1.0 - sg)))
        dz, dg, db = _ln_bwd(dln, zn, rs, vec_ref[1:2, :])
        dvec_ref[1:2, :] += dg
        dvec_ref[2:3, :] += db
        acc_vec(0, dz)
        for k, r, off in _conv_taps():
            dcw_ref[k:k + 1, :] += jnp.sum(dz * vy_s[r, off:off + t, :], axis=0, keepdims=True)
        ext = jnp.concatenate([dz, cdz_s[...]], axis=0)
        cdz_s[...] = dz[0:HALO]
        dy = jnp.zeros((t, C), F32)
        shifted = {}
        for k in range(CONF_KERNEL):
            m = CONF_KERNEL - 1 - k
            r, off = m % 8, m - m % 8
            if r not in shifted:
                shifted[r] = _up(ext, r)
            dy = dy + cw_ref[k:k + 1, :] * shifted[r][off:off + t]
        dp_ref[:, 0:C] = (dy * sgate).astype(BF16)
        dp_ref[:, C:2 * C] = (dy * val * sgate * (1.0 - sgate)).astype(BF16)

        sb, sc, sx = _piece(p_ref, 2), _piece(p_ref, 3), _piece(p_ref, 4)
        q = sc * sx
        q_halo = ph_ref[:, 3 * C:4 * C] * ph_ref[:, 4 * C:5 * C] * first
        ext = jnp.concatenate([q_halo, q], axis=0)
        q1, q2 = _down(ext, 1)[HALO:], _down(ext, 2)[HALO:]
        cz = vec_ref[8:9, :] * q + vec_ref[7:8, :] * q1 + vec_ref[6:7, :] * q2
        dcz = d_b * sb
        dp_ref[:, 2 * C:3 * C] = (d_b * cz).astype(BF16)
        acc_vec(8, dcz * q)
        acc_vec(7, dcz * q1)
        acc_vec(6, dcz * q2)
        ext = jnp.concatenate([dcz, cdc_s[...]], axis=0)
        cdc_s[...] = dcz[0:HALO]
        dq = vec_ref[8:9, :] * dcz + vec_ref[7:8, :] * _up(ext, 1)[0:t] + vec_ref[6:7, :] * _up(ext, 2)[0:t]
        dp_ref[:, 3 * C:4 * C] = (dq * sx).astype(BF16)
        dp_ref[:, 4 * C:5 * C] = (dq * sc).astype(BF16)

        xp = _piece(p_ref, 5)
        ext = jnp.concatenate([ph_ref[:, 5 * C:6 * C] * first, xp], axis=0)
        cnt = _pool_count(grp, tile * t, t)
        dd = (_trailing_sums(ext, grp, t) / cnt - xp).astype(BF16)
        e2 = _dot(dd, pw_ref[...])
        acc_vec(3, d_c * e2)
        de = (d_c * vec_ref[3:4, :]).astype(BF16)
        dpool_ref[...] += _dot(_t_bf16(dd.astype(F32)), de)
        ddd = _dot_nt(de, pw_ref[...])
        fq = ddd / cnt
        ext = jnp.concatenate([fq, cf_s[...]], axis=0)
        cf_s[...] = fq[0:HALO]
        dp_ref[:, 5 * C:6 * C] = (_leading_sums(ext, grp, t) - ddd).astype(BF16)

        vn, vrs, vlnf = _ln_fwd(_piece(p_ref, 7), vec_ref[4:5, :], vec_ref[5:6, :])
        vln = vlnf.astype(BF16)
        for n in range(t // CHUNK):
            rows = slice(n * CHUNK, (n + 1) * CHUNK)
            mixed = _head_select(_dot(ws_ref[...], vln[rows]), grp) + bias_ref[...]
            dd_n = d_d[rows]
            dp_ref[rows, 6 * C:7 * C] = (dd_n * mixed).astype(BF16)
            dmx = dd_n * p_ref[rows, 6 * C:7 * C]
            dbias_s[...] += dmx
            dmx_b = dmx.astype(BF16)
            dvl_s[rows, :] = _head_select(_dot(wst_ref[...], dmx_b), grp)
            for h in range(N_HEADS):
                hrows = slice(h * CHUNK, (h + 1) * CHUNK)
                dws_ref[hrows, :] += _dot_nt(jnp.where(grp == h, dmx_b, jnp.zeros_like(dmx_b)), vln[rows])
        dvl = dvl_s[...]
        dv, dg, db = _ln_bwd(dvl, vn, vrs, vec_ref[4:5, :])
        dvec_ref[4:5, :] += dg
        dvec_ref[5:6, :] += db
        dp_ref[:, 7 * C:8 * C] = dv.astype(BF16)

        @pl.when(i == nt - 1)
        def _():
            dws_ref[...] = dws_ref[...] * tril_ref[...]
            dbs_ref[...] = lax.dot_general(hr_ref[...], dbias_s[...], NT, precision=lax.Precision.HIGHEST,
                                           preferred_element_type=F32)

    full = lambda a: pl.BlockSpec(a.shape, lambda i: (0, 0))
    acc = lambda shape: pl.BlockSpec(shape, lambda i: (0, 0))
    f = p.shape[1]
    return pl.pallas_call(
        body, name="mixer_bwd", grid=(nt,),
        in_specs=[pl.BlockSpec((t, d), lambda i: (nt - 1 - i, 0)), pl.BlockSpec((t, f), lambda i: (nt - 1 - i, 0)),
                  pl.BlockSpec((HALO, f), lambda i: (jnp.maximum((nt - 1 - i) * hb - 1, 0), 0)),
                  full(cw), full(vec), full(pool_w), full(wstack), full(wstack_t), full(bias), full(tril4), full(head_rows),
                  full(w_out)],
        out_specs=[pl.BlockSpec((t, f), lambda i: (nt - 1 - i, 0)), acc((32, C)), acc((16, C)), acc((C, C)),
                   acc((N_HEADS * CHUNK, CHUNK)), acc((8, CHUNK))],
        out_shape=[jax.ShapeDtypeStruct((s, f), BF16), jax.ShapeDtypeStruct((32, C), F32), jax.ShapeDtypeStruct((16, C), F32),
                   jax.ShapeDtypeStruct((C, C), F32), jax.ShapeDtypeStruct((N_HEADS * CHUNK, CHUNK), F32),
                   jax.ShapeDtypeStruct((8, CHUNK), F32)],
        scratch_shapes=[pltpu.VMEM((HALO, C), F32)] * 3 + [pltpu.VMEM((8, n_ext, C), F32), pltpu.VMEM((t, C), F32),
                                                            pltpu.VMEM((CHUNK, C), F32)],
        compiler_params=_params("arbitrary"),
    )(dx2, p, p, cw, vec, pool_w, wstack, wstack_t, bias, tril4, head_rows, w_out)


def _loss_bwd(x, g, target):
    s, d = x.shape
    tm = _tile(s, 512)

    def body(x_ref, g_ref, t_ref, dx_ref, dg_ref, loss_ref):
        @pl.when(pl.program_id(0) == 0)
        def _():
            dg_ref[...] = jnp.zeros_like(dg_ref)
            loss_ref[...] = jnp.zeros_like(loss_ref)

        n, r, y = _rms(x_ref[...], g_ref[...])
        err = y - t_ref[...]
        loss_ref[...] += 0.5 * jnp.sum(jnp.mean(err * err, axis=-1, keepdims=True), axis=0, keepdims=True)
        dxr, dg = _rms_bwd(err * (1.0 / d), n, r, g_ref[...])
        dx_ref[...] = dxr
        dg_ref[0:1, :] += dg

    row = pl.BlockSpec((tm, d), lambda i: (i, 0))
    return pl.pallas_call(
        body, name="loss_bwd", grid=(s // tm,),
        in_specs=[row, pl.BlockSpec((1, d), lambda i: (0, 0)), row],
        out_specs=[row, pl.BlockSpec((8, d), lambda i: (0, 0)), pl.BlockSpec((8, LANES), lambda i: (0, 0))],
        out_shape=[jax.ShapeDtypeStruct((s, d), F32), jax.ShapeDtypeStruct((8, d), F32), jax.ShapeDtypeStruct((8, LANES), F32)],
        compiler_params=_params("arbitrary"),
    )(x, g, target)


def _adamw_math(w, g, m, v):
    m = ADAM_B1 * m + (1.0 - ADAM_B1) * g
    v = ADAM_B2 * v + (1.0 - ADAM_B2) * (g * g)
    m_hat = m / (1.0 - ADAM_B1 ** ADAM_STEP)
    v_hat = v / (1.0 - ADAM_B2 ** ADAM_STEP)
    return -ADAM_LR * (m_hat / (jnp.sqrt(v_hat) + ADAM_EPS) + ADAM_WD * w), m, v


def _adamw(w, g, m, v):
    r, c = w.shape
    tr = r // 8 if r % 64 == 0 else r

    def body(w_ref, g_ref, m_ref, v_ref, d_ref, mo_ref, vo_ref):
        d_ref[...], mo_ref[...], vo_ref[...] = _adamw_math(w_ref[...], g_ref[...], m_ref[...], v_ref[...])

    blk = pl.BlockSpec((tr, c), lambda i: (i, 0))
    return pl.pallas_call(
        body, name="adamw", grid=(r // tr,), in_specs=[blk] * 4, out_specs=[blk] * 3,
        out_shape=[jax.ShapeDtypeStruct((r, c), F32)] * 3, compiler_params=_params("arbitrary"),
    )(w, g, m, v)


def _adamw_small(ws, gs, ms, vs):
    n = len(ws)

    def body(*refs):
        ins, outs = refs[:4 * n], refs[4 * n:]
        for k in range(n):
            dl, mo, vo = _adamw_math(ins[k][...], ins[n + k][...], ins[2 * n + k][...], ins[3 * n + k][...])
            outs[k][...], outs[n + k][...], outs[2 * n + k][...] = dl, mo, vo

    vm = pl.BlockSpec(memory_space=pltpu.VMEM)
    out = pl.pallas_call(
        body, name="adamw_small", in_specs=[vm] * (4 * n), out_specs=[vm] * (3 * n),
        out_shape=[jax.ShapeDtypeStruct(a.shape, F32) for a in ws] * 3,
        compiler_params=pltpu.CompilerParams(vmem_limit_bytes=VMEM_LIMIT),
    )(*ws, *gs, *ms, *vs)
    return out[:n], out[n:2 * n], out[2 * n:]


def _where_am_i():
    x, y, c = lax.axis_index("x"), lax.axis_index("y"), lax.axis_index("c")
    chips = [(1 - x, y), (x, 1 - y), (1 - x, 1 - y)]
    return x, y, c, chips


def _chip_id(chip):
    return 2 * chip[0] + chip[1]


def _landing_shapes(shards):
    return [jax.ShapeDtypeStruct((3, a.shape[0] // 2, a.shape[1]), a.dtype) for a in shards]


def _ici_gather_copies(ins, lands, send_sems, recv_sems):
    _, _, c, chips = _where_am_i()
    copies = []
    for a, src in enumerate(ins):
        hr = src.shape[0] // 2
        for j, chip in enumerate(chips):
            copies.append(pltpu.make_async_remote_copy(
                src_ref=src.at[pl.ds(c * hr, hr)], dst_ref=lands[a].at[j], send_sem=send_sems.at[3 * a + j],
                recv_sem=recv_sems.at[3 * a + j], device_id=(*chip, c), device_id_type=MESH))
    return copies


def _ici_reduce_copies(s_ref, o_ref, send_sems, recv_sems, local_sem):
    x, y, c, chips = _where_am_i()
    copies = [pltpu.make_async_remote_copy(src_ref=s_ref.at[_chip_id(chip)], dst_ref=o_ref.at[j], send_sem=send_sems.at[j],
                                           recv_sem=recv_sems.at[j], device_id=(*chip, c), device_id_type=MESH)
              for j, chip in enumerate(chips)]
    return copies, pltpu.make_async_copy(s_ref.at[_chip_id((x, y))], o_ref.at[3], local_sem)


def _wait_all(copies):
    for cp in copies:
        cp.wait_recv()
    for cp in copies:
        cp.wait_send()


def _ici_gather(shards):
    n = len(shards)

    def body(*refs):
        copies = _ici_gather_copies(refs[:n], refs[n:2 * n], refs[2 * n], refs[2 * n + 1])
        for cp in copies:
            cp.start()
        _wait_all(copies)

    return pl.pallas_call(
        body, name="ici_gather", in_specs=[ANY] * n, out_specs=[ANY] * n, out_shape=_landing_shapes(shards),
        scratch_shapes=[pltpu.SemaphoreType.DMA((3 * n,))] * 2,
    )(*shards)


SLOTS = 3


def _sibling_gather(shards, lands):
    n = len(shards)
    d = shards[0].shape[1]
    halves = [a.shape[0] // 2 for a in shards]
    hmax = max(halves)
    chunks = [(a, j, halves[a]) for a in range(n) for j in range(3)]
    nc = len(chunks)

    def body(*refs):
        ins, lnd, outs = refs[:n], refs[n:2 * n], refs[2 * n:3 * n]
        sbuf, rbuf, obuf, ld_sems, take_sems, send_sems, recv_sems, place_sems, own_ld_sems, own_st_sems, credits = refs[3 * n:]
        x, y, c, chips = _where_am_i()
        sibling = (x, y, 1 - c)
        me = _chip_id((x, y))

        def load(i):
            a, j, hr = chunks[i]
            return pltpu.make_async_copy(lnd[a].at[j], sbuf.at[i % SLOTS, pl.ds(0, hr)], ld_sems.at[i % SLOTS])

        def push(i):
            hr, slot = chunks[i][2], i % SLOTS
            return pltpu.make_async_remote_copy(src_ref=sbuf.at[slot, pl.ds(0, hr)], dst_ref=rbuf.at[slot, pl.ds(0, hr)],
                                                send_sem=send_sems.at[slot], recv_sem=recv_sems.at[slot], device_id=sibling,
                                                device_id_type=MESH)

        def take(i):
            a, j, hr = chunks[i]
            return pltpu.make_async_copy(rbuf.at[i % SLOTS, pl.ds(0, hr)],
                                         outs[a].at[_chip_id(chips[j]), pl.ds((1 - c) * hr, hr)], take_sems.at[i % SLOTS])

        def place(i):
            a, j, hr = chunks[i]
            return pltpu.make_async_copy(sbuf.at[i % SLOTS, pl.ds(0, hr)], outs[a].at[_chip_id(chips[j]), pl.ds(c * hr, hr)],
                                         place_sems.at[i % SLOTS])

        own = [(a, h, halves[a]) for a in range(n) for h in range(2)]

        def own_load(k):
            a, h, hr = own[k]
            return pltpu.make_async_copy(ins[a].at[pl.ds(h * hr, hr)], obuf.at[k % 2, pl.ds(0, hr)], own_ld_sems.at[k % 2])

        def own_store(k):
            a, h, hr = own[k]
            return pltpu.make_async_copy(obuf.at[k % 2, pl.ds(0, hr)], outs[a].at[me, pl.ds(h * hr, hr)], own_st_sems.at[k % 2])

        def own_step(k):
            if k < len(own):
                if k >= 2:
                    own_store(k - 2).wait()
                own_load(k).start()
            if 1 <= k <= len(own):
                own_load(k - 1).wait()
                own_store(k - 1).start()

        for i in range(min(2, nc)):
            load(i).start()
        for i in range(nc):
            own_step(i)
            if i >= 1:
                push(i - 1).wait_send()
                place(i - 1).wait()
            if i + 2 < nc:
                load(i + 2).start()
            load(i).wait()
            place(i).start()
            if i >= SLOTS:
                pl.semaphore_wait(credits.at[i % SLOTS], 1)
            push(i).start()
            if i >= 1:
                push(i - 1).wait_recv()
                take(i - 1).start()
            if i >= 2:
                take(i - 2).wait()
                if i - 2 + SLOTS < nc:
                    pl.semaphore_signal(credits.at[(i - 2) % SLOTS], inc=1, device_id=sibling, device_id_type=MESH)
        push(nc - 1).wait_send()
        place(nc - 1).wait()
        push(nc - 1).wait_recv()
        take(nc - 1).start()
        for i in range(max(0, nc - 2), nc):
            take(i).wait()
        for k in range(nc, len(own) + 1):
            own_step(k)
        for k in range(max(0, len(own) - 2), len(own)):
            own_store(k).wait()

    dma = pltpu.SemaphoreType.DMA((SLOTS,))
    dma2 = pltpu.SemaphoreType.DMA((2,))
    return pl.pallas_call(
        body, name="sibling_gather", in_specs=[ANY] * (2 * n), out_specs=[ANY] * n,
        out_shape=[jax.ShapeDtypeStruct((N_CHIPS,) + a.shape, a.dtype) for a in shards],
        scratch_shapes=[pltpu.VMEM((SLOTS, hmax, d), BF16), pltpu.VMEM((SLOTS, hmax, d), BF16), pltpu.VMEM((2, hmax, d), BF16),
                        dma, dma, dma, dma, dma, dma2, dma2, pltpu.SemaphoreType.REGULAR((SLOTS,))],
        compiler_params=pltpu.CompilerParams(vmem_limit_bytes=VMEM_LIMIT),
    )(*shards, *lands)


def _pair_sum(grads):
    n = len(grads)
    halves = [g.shape[1] // 2 for g in grads]
    total, rows = sum(halves), max(halves)
    d = grads[0].shape[2]
    chunks, off = [], 0
    for a in range(n):
        chunks += [(a, k, off, halves[a]) for k in range(N_CHIPS)]
        off += halves[a]
    nc = len(chunks)

    def body(*refs):
        ins, out_ref = refs[:n], refs[n]
        sbuf, rbuf, mbuf, obuf, ls_sems, lm_sems, st_sems, send_sems, recv_sems, credits = refs[n + 1:]
        x, y, c, _ = _where_am_i()
        sibling = (x, y, 1 - c)

        def load_theirs(i):
            a, k, _, hr = chunks[i]
            return pltpu.make_async_copy(ins[a].at[k, pl.ds((1 - c) * hr, hr)], sbuf.at[i % SLOTS, pl.ds(0, hr)], ls_sems.at[i % SLOTS])

        def load_mine(i):
            a, k, _, hr = chunks[i]
            return pltpu.make_async_copy(ins[a].at[k, pl.ds(c * hr, hr)], mbuf.at[i % SLOTS, pl.ds(0, hr)], lm_sems.at[i % SLOTS])

        def push(i):
            hr, slot = chunks[i][3], i % SLOTS
            return pltpu.make_async_remote_copy(src_ref=sbuf.at[slot, pl.ds(0, hr)], dst_ref=rbuf.at[slot, pl.ds(0, hr)],
                                                send_sem=send_sems.at[slot], recv_sem=recv_sems.at[slot], device_id=sibling,
                                                device_id_type=MESH)

        def store(i):
            _, k, o, hr = chunks[i]
            slot = i % SLOTS
            return pltpu.make_async_copy(obuf.at[slot, pl.ds(0, hr)], out_ref.at[k, pl.ds(o, hr)], st_sems.at[slot])

        def start_push(i):
            load_theirs(i).wait()
            if i >= SLOTS:
                pl.semaphore_wait(credits.at[i % SLOTS], 1)
            push(i).start()

        for i in range(min(2, nc)):
            load_theirs(i).start()
            load_mine(i).start()
        start_push(0)
        for i in range(nc):
            hr, slot = chunks[i][3], i % SLOTS
            if i + 2 < nc:
                load_theirs(i + 2).start()
                load_mine(i + 2).start()
            if i + 1 < nc:
                start_push(i + 1)
            push(i).wait_recv()
            push(i).wait_send()
            load_mine(i).wait()
            if i >= SLOTS:
                store(i - SLOTS).wait()
            obuf[slot, 0:hr, :] = (mbuf[slot, 0:hr, :].astype(F32) + rbuf[slot, 0:hr, :].astype(F32)).astype(BF16)
            if i + SLOTS < nc:
                pl.semaphore_signal(credits.at[slot], inc=1, device_id=sibling, device_id_type=MESH)
            store(i).start()
        for i in range(max(0, nc - SLOTS), nc):
            store(i).wait()

    stage = pltpu.VMEM((SLOTS, rows, d), BF16)
    dma = pltpu.SemaphoreType.DMA((SLOTS,))
    return pl.pallas_call(
        body, name="pair_sum", in_specs=[ANY] * n, out_specs=ANY, out_shape=jax.ShapeDtypeStruct((N_CHIPS, total, d), BF16),
        scratch_shapes=[stage, stage, stage, stage, dma, dma, dma, dma, dma, pltpu.SemaphoreType.REGULAR((SLOTS,))],
        compiler_params=pltpu.CompilerParams(vmem_limit_bytes=VMEM_LIMIT),
    )(*grads)


def _reduce_chips(sums):
    def body(s_ref, o_ref, send_sems, recv_sems, local_sem):
        copies, local = _ici_reduce_copies(s_ref, o_ref, send_sems, recv_sems, local_sem)
        local.start()
        for cp in copies:
            cp.start()
        _wait_all(copies)
        local.wait()

    return pl.pallas_call(
        body, name="reduce_chips", in_specs=[ANY], out_specs=ANY, out_shape=jax.ShapeDtypeStruct(sums.shape, BF16),
        scratch_shapes=[pltpu.SemaphoreType.DMA((3,)), pltpu.SemaphoreType.DMA((3,)), pltpu.SemaphoreType.DMA],
    )(sums)


def _sum_share(parts, rows):
    n_l, n = len(parts), len(rows)
    d = parts[0].shape[2]
    halves = [r // 2 for r in rows]
    hmax = max(halves)
    chunks = []
    for l in range(n_l):
        off = 0
        for a in range(n):
            chunks.append((l, a, off, halves[a]))
            off += halves[a]
    nc = len(chunks)

    def body(*refs):
        ins, outs = refs[:n_l], refs[n_l:n_l + n]
        pbuf, obuf, rbuf, ld_sems, keep_sems, take_sems, send_sems, recv_sems, credits = refs[n_l + n:]
        x, y, c, _ = _where_am_i()
        sibling = (x, y, 1 - c)

        def load(i):
            l, _, off, hr = chunks[i]
            return pltpu.make_async_copy(ins[l].at[:, pl.ds(off, hr)], pbuf.at[i % SLOTS, :, pl.ds(0, hr)], ld_sems.at[i % SLOTS])

        def keep(i):
            l, a, _, hr = chunks[i]
            return pltpu.make_async_copy(obuf.at[i % SLOTS, pl.ds(0, hr)], outs[a].at[l, pl.ds(c * hr, hr)], keep_sems.at[i % SLOTS])

        def push(i):
            hr, slot = chunks[i][3], i % SLOTS
            return pltpu.make_async_remote_copy(src_ref=obuf.at[slot, pl.ds(0, hr)], dst_ref=rbuf.at[slot, pl.ds(0, hr)],
                                                send_sem=send_sems.at[slot], recv_sem=recv_sems.at[slot], device_id=sibling,
                                                device_id_type=MESH)

        def take(i):
            l, a, _, hr = chunks[i]
            return pltpu.make_async_copy(rbuf.at[i % SLOTS, pl.ds(0, hr)], outs[a].at[l, pl.ds((1 - c) * hr, hr)],
                                         take_sems.at[i % SLOTS])

        for i in range(min(2, nc)):
            load(i).start()
        for i in range(nc):
            hr, slot = chunks[i][3], i % SLOTS
            if i + 2 < nc:
                load(i + 2).start()
            load(i).wait()
            if i >= SLOTS:
                keep(i - SLOTS).wait()
                push(i - SLOTS).wait_send()
            part = lambda k: pbuf[slot, k, 0:hr, :].astype(F32)
            obuf[slot, 0:hr, :] = ((part(3) + part(0)) + part(1)) + part(2)
            keep(i).start()
            if i >= SLOTS:
                pl.semaphore_wait(credits.at[slot], 1)
            push(i).start()
            if i >= 1:
                push(i - 1).wait_recv()
                take(i - 1).start()
            if i >= 2:
                take(i - 2).wait()
                if i - 2 + SLOTS < nc:
                    pl.semaphore_signal(credits.at[(i - 2) % SLOTS], inc=1, device_id=sibling, device_id_type=MESH)
        push(nc - 1).wait_recv()
        take(nc - 1).start()
        for i in range(max(0, nc - 2), nc):
            take(i).wait()
        for i in range(max(0, nc - SLOTS), nc):
            keep(i).wait()
            push(i).wait_send()

    dma = pltpu.SemaphoreType.DMA((SLOTS,))
    return pl.pallas_call(
        body, name="sum_share", in_specs=[ANY] * n_l, out_specs=[ANY] * n,
        out_shape=[jax.ShapeDtypeStruct((n_l, r, d), F32) for r in rows],
        scratch_shapes=[pltpu.VMEM((SLOTS, N_CHIPS, hmax, d), BF16), pltpu.VMEM((SLOTS, hmax, d), F32),
                        pltpu.VMEM((SLOTS, hmax, d), F32), dma, dma, dma, dma, dma, pltpu.SemaphoreType.REGULAR((SLOTS,))],
        compiler_params=pltpu.CompilerParams(vmem_limit_bytes=VMEM_LIMIT),
    )(*parts)


def _all_gather_small(block, reduce):
    m, n = block.shape

    def body(x_ref, out_ref, *scratch):
        if reduce:
            all_ref, send_sems, recv_sems, local_sem = scratch
        else:
            all_ref = out_ref
            send_sems, recv_sems, local_sem = scratch
        x, y, c, chips = _where_am_i()
        me, sibling = (x, y, c), (x, y, 1 - c)

        def rows(px, py, pc):
            return all_ref.at[pl.ds((4 * px + 2 * py + pc) * m, m), :]

        def copy(k, blk, to, src=None):
            return pltpu.make_async_remote_copy(src_ref=rows(*blk) if src is None else src, dst_ref=rows(*blk),
                                                send_sem=send_sems.at[k], recv_sem=recv_sems.at[k], device_id=to,
                                                device_id_type=MESH)

        mine = pltpu.make_async_copy(x_ref, rows(*me), local_sem)
        mine.start()
        first = [copy(0, me, sibling, src=x_ref)]
        first += [copy(1 + j, me, (*chip, c), src=x_ref) for j, chip in enumerate(chips)]
        for cp in first:
            cp.start()
        passed = [copy(4 + j, (*chip, c), sibling) for j, chip in enumerate(chips)]
        for j, chip in enumerate(chips):
            copy(1 + j, (*chip, c), me).wait_recv()
            passed[j].start()
        copy(0, sibling, me).wait_recv()
        for j, chip in enumerate(chips):
            copy(4 + j, (*chip, 1 - c), me).wait_recv()
        for cp in first + passed:
            cp.wait_send()
        mine.wait()
        if reduce:
            total = all_ref[0:m, :]
            for dev in range(1, N_DEV):
                total = total + all_ref[dev * m:(dev + 1) * m, :]
            out_ref[...] = total

    vm = pl.BlockSpec(memory_space=pltpu.VMEM)
    sems = [pltpu.SemaphoreType.DMA((7,)), pltpu.SemaphoreType.DMA((7,)), pltpu.SemaphoreType.DMA]
    return pl.pallas_call(
        body, name="reduce_small" if reduce else "gather_small", in_specs=[vm], out_specs=vm,
        out_shape=jax.ShapeDtypeStruct((m, n) if reduce else (N_DEV * m, n), F32),
        scratch_shapes=([pltpu.VMEM((N_DEV * m, n), F32)] if reduce else []) + sems,
        compiler_params=pltpu.CompilerParams(vmem_limit_bytes=VMEM_LIMIT),
    )(block)


def _pack(arrays):
    flat = jnp.concatenate([a.reshape(-1) for a in arrays])
    pad = (-flat.shape[0]) % (8 * LANES)
    return jnp.pad(flat, (0, pad)).reshape(-1, LANES)


def _unpack(buf, shapes):
    flat = buf.reshape(-1)
    out, off = [], 0
    for shp in shapes:
        size = 1
        for dim in shp:
            size *= dim
        out.append(flat[off:off + size].reshape(shp))
        off += size
    return out


BIG = ("ffn1_w1", "ffn1_w3", "ffn1_w2", "w_in", "w_out", "ffn2_w1", "ffn2_w3", "ffn2_w2")
TRANSPOSED = ("ffn1_w1", "ffn1_w3", "w_in", "ffn2_w1", "ffn2_w3")
SMALL = ("ffn1_norm", "mix_norm", "conf_conv_w", "conf_conv_b", "conf_ln_g", "conf_ln_b", "sconv_w", "pool_w", "pool_scale",
         "gmlp_ln_g", "gmlp_ln_b", "gmlp_w_s", "gmlp_b_s", "ffn2_norm", "final_norm")
ORDER = ("ffn1_norm", "ffn1_w1", "ffn1_w3", "ffn1_w2", "mix_norm", "w_in", "conf_conv_w", "conf_conv_b", "conf_ln_g", "conf_ln_b",
         "sconv_w", "pool_w", "pool_scale", "gmlp_ln_g", "gmlp_ln_b", "gmlp_w_s", "gmlp_b_s", "w_out", "ffn2_norm", "ffn2_w1",
         "ffn2_w3", "ffn2_w2", "final_norm")


def _as2d(a):
    return a.reshape(-1, a.shape[-1])


def kernel(x, ffn1_norm, ffn1_w1, ffn1_w3, ffn1_w2, mix_norm, w_in, conf_conv_w, conf_conv_b, conf_ln_g, conf_ln_b, sconv_w, pool_w, pool_scale, gmlp_ln_g, gmlp_ln_b, gmlp_w_s, gmlp_b_s, w_out, ffn2_norm, ffn2_w1, ffn2_w3, ffn2_w2, final_norm, loss_target, m_ffn1_norm, m_ffn1_w1, m_ffn1_w3, m_ffn1_w2, m_mix_norm, m_w_in, m_conf_conv_w, m_conf_conv_b, m_conf_ln_g, m_conf_ln_b, m_sconv_w, m_pool_w, m_pool_scale, m_gmlp_ln_g, m_gmlp_ln_b, m_gmlp_w_s, m_gmlp_b_s, m_w_out, m_ffn2_norm, m_ffn2_w1, m_ffn2_w3, m_ffn2_w2, m_final_norm, v_ffn1_norm, v_ffn1_w1, v_ffn1_w3, v_ffn1_w2, v_mix_norm, v_w_in, v_conf_conv_w, v_conf_conv_b, v_conf_ln_g, v_conf_ln_b, v_sconv_w, v_pool_w, v_pool_scale, v_gmlp_ln_g, v_gmlp_ln_b, v_gmlp_w_s, v_gmlp_b_s, v_w_out, v_ffn2_norm, v_ffn2_w1, v_ffn2_w3, v_ffn2_w2, v_final_norm):
    given = dict(locals())
    w = {k: given[k] for k in ORDER}
    mom = {k: given["m_" + k] for k in ORDER}
    var = {k: given["v_" + k] for k in ORDER}
    n_l = ffn1_w1.shape[0]
    xs = x[0]
    d = xs.shape[1]
    chip = 2 * lax.axis_index("x") + lax.axis_index("y")

    def shard(name, l):
        a = w[name][l]
        return (a.T if name in TRANSPOSED else a).astype(BF16)

    def finish_gather(l, lands):
        out = _sibling_gather([shard(name, l) for name in BIG], lands)
        return {name: g.reshape(-1, d) for name, g in zip(BIG, out)}

    half = len(BIG) // 2
    shard_rows = [w[name].shape[2] if name in TRANSPOSED else w[name].shape[1] for name in BIG]

    conv_shapes = [conf_conv_w.shape, sconv_w.shape]
    conv_all = _all_gather_small(_pack([conf_conv_w, sconv_w]), reduce=False)
    conv_all = conv_all.reshape(N_CHIPS, 2, -1)[:, 0]
    conf_full, sconv_full = [jnp.concatenate([_unpack(conv_all[k], conv_shapes)[a] for k in range(N_CHIPS)], axis=-1)
                             for a in range(2)]

    lane = jnp.arange(C) // HEAD_DIM
    head_rows = (jnp.arange(8)[:, None] == lane[None, :]).astype(F32)
    tril = jnp.tril(jnp.ones((CHUNK, CHUNK), F32))
    tril4 = jnp.tile(tril, (N_HEADS, 1))
    mixer_consts = []
    for l in range(n_l):
        cw = jnp.pad(conf_full[l], ((0, 32 - CONF_KERNEL), (0, 0)))
        vec = jnp.concatenate([conf_conv_b[l][None], conf_ln_g[l][None], conf_ln_b[l][None], pool_scale[l][None],
                               gmlp_ln_g[l][None], gmlp_ln_b[l][None], sconv_full[l], jnp.zeros((7, C), F32)], axis=0)
        eye = jnp.eye(len(pool_w[l]), dtype=F32)
        pool_blk = (eye[:, None, :, None] * pool_w[l][:, :, None, :]).reshape(C, C).astype(BF16)
        ws = gmlp_w_s[l] * tril[None]
        wstack = ws.reshape(N_HEADS * CHUNK, CHUNK).astype(BF16)
        wstack_t = jnp.swapaxes(ws, 1, 2).reshape(N_HEADS * CHUNK, CHUNK).astype(BF16)
        bias = jnp.repeat(gmlp_b_s[l].T, HEAD_DIM, axis=1)
        mixer_consts.append((cw, vec, pool_blk, wstack, wstack_t, bias))

    saved = []
    cur = xs
    gathered = [finish_gather(0, _ici_gather([shard(name, 0) for name in BIG]))]
    for l in range(n_l):
        gw = gathered[l]
        cw, vec, pool_blk, wstack, wstack_t, bias = mixer_consts[l]
        nxt = [shard(name, l + 1) for name in BIG] if l + 1 < n_l else []
        x0 = cur
        x1, a1, b1, lands1 = _ffn_fwd(x0, ffn1_norm[l][None], gw["ffn1_w1"], gw["ffn1_w3"], gw["ffn1_w2"], nxt[:half])
        p = _proj_fwd(x1, mix_norm[l][None], gw["w_in"])
        x2, mix_t = _mixer_fwd(p, x1, cw, vec, pool_blk, wstack, bias, gw["w_out"])
        x3, a2, b2, lands2 = _ffn_fwd(x2, ffn2_norm[l][None], gw["ffn2_w1"], gw["ffn2_w3"], gw["ffn2_w2"], nxt[half:])
        if nxt:
            gathered.append(finish_gather(l + 1, lands1 + lands2))
        saved.append((x0, x1, x2, a1, b1, a2, b2, p, mix_t))
        cur = x3

    dx, dg_final, loss_part = _loss_bwd(cur, final_norm[None], loss_target[0])

    small_parts = [None] * n_l
    reduced_halves = [None] * n_l
    pending = None
    for l in reversed(range(n_l)):
        gw = gathered[l]
        cw, vec, pool_blk, wstack, wstack_t, bias = mixer_consts[l]
        x0, x1, x2, a1, b1, a2, b2, p, mix_t = saved[l]
        big = {}
        out = _ffn_bwd(dx, x2, ffn2_norm[l][None], a2, b2, gw["ffn2_w1"], gw["ffn2_w3"], gw["ffn2_w2"], pending)
        dx, dg_ffn2, h, dy, da_t, db_t, u_t = out[:7]
        if pending is not None:
            reduced_halves[l + 1] = out[7]
        big["ffn2_w1"], big["ffn2_w3"], big["ffn2_w2"] = _dw(da_t, h), _dw(db_t, h), _dw(u_t, dy)
        big["w_out"] = _dw(mix_t, dx.astype(BF16))
        dp, dcw, dvec, dpool, dws, dbs = _mixer_bwd(dx, p, cw, vec, pool_blk, wstack, wstack_t, bias, tril4, head_rows, gw["w_out"])
        dx, dg_mix, h, dp_t = _proj_bwd(dx, x1, mix_norm[l][None], dp, gw["w_in"])
        big["w_in"] = _dw(dp_t, h)
        dx, dg_ffn1, h, dy, da_t, db_t, u_t = _ffn_bwd(dx, x0, ffn1_norm[l][None], a1, b1, gw["ffn1_w1"], gw["ffn1_w3"], gw["ffn1_w2"])
        big["ffn1_w1"], big["ffn1_w3"], big["ffn1_w2"] = _dw(da_t, h), _dw(db_t, h), _dw(u_t, dy)
        small_parts[l] = [dg_ffn1[0], dg_mix[0], dg_ffn2[0], dcw, dvec, dpool, dws, dbs]
        pending = _pair_sum([big[name].reshape(N_CHIPS, -1, d) for name in BIG])
    reduced_halves[0] = _reduce_chips(pending)
    grad_x = dx[None]

    full = dict(zip(BIG, _sum_share(reduced_halves, shard_rows)))
    grad = {name: (jnp.swapaxes(full[name], 1, 2) if name in TRANSPOSED else full[name]) for name in BIG}

    part_shapes = [a.shape for a in small_parts[0]]
    tail = [dg_final[0], loss_part[0]]
    packed = _pack([a for l in range(n_l) for a in small_parts[l]] + tail)
    summed = _unpack(_all_gather_small(packed, reduce=True), part_shapes * n_l + [a.shape for a in tail])
    per_layer = [summed[l * len(part_shapes):(l + 1) * len(part_shapes)] for l in range(n_l)]
    stack = lambda k: jnp.stack([per_layer[l][k] for l in range(n_l)])
    dcw_all, dvec_all, dpool_all, dws_all, dbs_all = stack(3), stack(4), stack(5), stack(6), stack(7)
    loss = summed[-1][0]
    chip_cols = lambda a: lax.dynamic_slice_in_dim(a, chip * (C // N_CHIPS), C // N_CHIPS, axis=2)
    n_pool = pool_w.shape[1]
    grad.update(
        ffn1_norm=stack(0), mix_norm=stack(1), ffn2_norm=stack(2), final_norm=summed[-2],
        conf_conv_w=chip_cols(dcw_all[:, :CONF_KERNEL]), conf_conv_b=dvec_all[:, 0], conf_ln_g=dvec_all[:, 1],
        conf_ln_b=dvec_all[:, 2], pool_scale=dvec_all[:, 3], gmlp_ln_g=dvec_all[:, 4], gmlp_ln_b=dvec_all[:, 5],
        sconv_w=chip_cols(dvec_all[:, 6:6 + SHORT_KERNEL]),
        pool_w=jnp.stack([dpool_all[:, g * POOL_GROUP:(g + 1) * POOL_GROUP, g * POOL_GROUP:(g + 1) * POOL_GROUP]
                          for g in range(n_pool)], axis=1),
        gmlp_w_s=dws_all.reshape(n_l, N_HEADS, CHUNK, CHUNK), gmlp_b_s=dbs_all[:, :N_HEADS],
    )

    delta, new_m, new_v = {}, {}, {}
    for name in BIG:
        shp = w[name].shape
        out = _adamw(_as2d(w[name]), _as2d(grad[name]), _as2d(mom[name]), _as2d(var[name]))
        delta[name], new_m[name], new_v[name] = (o.reshape(shp) for o in out)
    ds, ms, vs = _adamw_small([_as2d(w[k]) if w[k].ndim > 1 else w[k][None] for k in SMALL],
                              [_as2d(grad[k]) if grad[k].ndim > 1 else grad[k][None] for k in SMALL],
                              [_as2d(mom[k]) if mom[k].ndim > 1 else mom[k][None] for k in SMALL],
                              [_as2d(var[k]) if var[k].ndim > 1 else var[k][None] for k in SMALL])
    for k, dl, mo, vo in zip(SMALL, ds, ms, vs):
        delta[k], new_m[k], new_v[k] = dl.reshape(w[k].shape), mo.reshape(w[k].shape), vo.reshape(w[k].shape)

    return (loss, grad_x, *[grad[k] for k in ORDER], *[delta[k] for k in ORDER], *[new_m[k] for k in ORDER],
            *[new_v[k] for k in ORDER])
```

```python
import functools

import jax
import jax.numpy as jnp
from jax import lax
from jax.experimental import pallas as pl
from jax.experimental.pallas import tpu as pltpu

F32 = jnp.float32
BF16 = jnp.bfloat16
MESH = pl.DeviceIdType.MESH
ANY = pl.BlockSpec(memory_space=pl.ANY)

EPS = 1e-6
FFN_RESIDUAL = 0.5
D_GROUP = 256
CONF_KERNEL = 31
SHORT_KERNEL = 3
POOL_GROUP = 64
CHUNK = 128
N_HEADS = 4
HEAD_DIM = 64
HALO = 32
N_CHIPS = 4
N_DEV = 8
LANES = 128
MXU_TILE = 256
VMEM_LIMIT = 56 * 2**20
T_MIX_FWD = 512
T_MIX_BWD = 256

ADAM_LR = 0.001
ADAM_B1 = 0.9
ADAM_B2 = 0.999
ADAM_EPS = 1e-08
ADAM_WD = 0.01
ADAM_STEP = 10

NT = (((1,), (1,)), ((), ()))
TN = (((0,), (0,)), ((), ()))


def _params(*sem):
    return pltpu.CompilerParams(dimension_semantics=sem, vmem_limit_bytes=VMEM_LIMIT)


def _dot(a, b):
    return jnp.dot(a, b, preferred_element_type=F32)


def _dot_nt(a, b):
    return lax.dot_general(a, b, NT, preferred_element_type=F32)


def _t_bf16(v):
    return jnp.transpose(v).astype(BF16)


def _sigmoid(v):
    return 1.0 / (1.0 + jnp.exp(-v))


def _rms(x, g):
    r = lax.rsqrt(jnp.mean(x * x, axis=-1, keepdims=True) + EPS)
    n = x * r
    return n, r, n * g


def _rms_bwd(dh, n, r, g):
    dn = dh * g
    dx = r * (dn - n * jnp.mean(dn * n, axis=-1, keepdims=True))
    return dx, jnp.sum(dh * n, axis=0, keepdims=True)


def _ln_fwd(z, g, b):
    mu = jnp.mean(z, axis=-1, keepdims=True)
    zc = z - mu
    rs = lax.rsqrt(jnp.mean(zc * zc, axis=-1, keepdims=True) + EPS)
    zn = zc * rs
    return zn, rs, zn * g + b


def _ln_bwd(dl, zn, rs, g):
    dzn = dl * g
    dz = rs * (dzn - jnp.mean(dzn, axis=-1, keepdims=True) - zn * jnp.mean(dzn * zn, axis=-1, keepdims=True))
    return dz, jnp.sum(dl * zn, axis=0, keepdims=True), jnp.sum(dl, axis=0, keepdims=True)


def _tile(n, want):
    return want if n % want == 0 else n


def _resident(shape):
    return pl.BlockSpec(shape, lambda i: (0,) * len(shape), pipeline_mode=pl.Buffered(1))


def _ffn_fwd(x, g, w1t, w3t, w2, shards=()):
    s, d = x.shape
    f = w1t.shape[0]
    tm, tf = _tile(s, 512), MXU_TILE
    nj = f // tf
    ni = s // tm
    n_c = len(shards)

    def body(*refs):
        x_ref, g_ref, w1_ref, w3_ref, w2_ref = refs[:5]
        xo_ref, a_ref, b_ref = refs[5 + n_c:8 + n_c]
        u_s = refs[8 + 2 * n_c]
        if n_c:
            gather = lambda: _ici_gather_copies(refs[5:5 + n_c], refs[8 + n_c:8 + 2 * n_c], refs[9 + 2 * n_c], refs[10 + 2 * n_c])

            @pl.when(pl.program_id(0) == 0)
            def _():
                for cp in gather():
                    cp.start()

        h = _rms(x_ref[...], g_ref[...])[2].astype(BF16)
        for j in range(nj):
            cols = slice(j * tf, (j + 1) * tf)
            a = _dot_nt(h, w1_ref[cols, :])
            b = _dot_nt(h, w3_ref[cols, :])
            a_ref[:, cols] = a.astype(BF16)
            b_ref[:, cols] = b.astype(BF16)
            u_s[:, cols] = ((a * _sigmoid(a)) * b).astype(BF16)
        xo_ref[...] = x_ref[...] + FFN_RESIDUAL * _dot(u_s[...], w2_ref[...])
        if n_c:
            @pl.when(pl.program_id(0) == ni - 1)
            def _():
                _wait_all(gather())

    row = pl.BlockSpec((tm, d), lambda i: (i, 0))
    hid = pl.BlockSpec((tm, f), lambda i: (i, 0))
    sems = [pltpu.SemaphoreType.DMA((3 * n_c,))] * 2 if n_c else []
    out = pl.pallas_call(
        body, name="ffn_fwd_gather" if n_c else "ffn_fwd", grid=(ni,),
        in_specs=[row, _resident(g.shape), _resident(w1t.shape), _resident(w3t.shape), _resident(w2.shape)] + [ANY] * n_c,
        out_specs=[row, hid, hid] + [ANY] * n_c,
        out_shape=[jax.ShapeDtypeStruct((s, d), F32), jax.ShapeDtypeStruct((s, f), BF16), jax.ShapeDtypeStruct((s, f), BF16)]
        + _landing_shapes(shards),
        scratch_shapes=[pltpu.VMEM((tm, f), BF16)] + sems,
        compiler_params=_params("arbitrary"),
    )(x, g, w1t, w3t, w2, *shards)
    return out[0], out[1], out[2], list(out[3:])


def _ffn_bwd(dxo, x, g, a, b, w1t, w3t, w2, sums=None):
    s, d = x.shape
    f = w1t.shape[0]
    tm, tf = _tile(s, 256), MXU_TILE
    nj = f // tf
    ni = s // tm
    n_c = 0 if sums is None else 1

    def body(*refs):
        dxo_ref, x_ref, g_ref, a_ref, b_ref, w1_ref, w3_ref, w2_ref = refs[:8]
        dx_ref, dg_ref, h_ref, dy_ref, da_ref, db_ref, u_ref, dxb_ref = refs[8 + n_c:16 + n_c]
        if n_c:
            exchange = lambda: _ici_reduce_copies(refs[8], refs[16 + n_c], *refs[16 + 2 * n_c:])

        @pl.when(pl.program_id(0) == 0)
        def _():
            dg_ref[...] = jnp.zeros_like(dg_ref)
            if n_c:
                copies, local = exchange()
                local.start()
                for cp in copies:
                    cp.start()

        h_ref[...] = _rms(x_ref[...], g_ref[...])[2].astype(BF16)
        dy = (FFN_RESIDUAL * dxo_ref[...]).astype(BF16)
        dy_ref[...] = dy
        for j in range(nj):
            cols = slice(j * tf, (j + 1) * tf)
            du = _dot_nt(dy, w2_ref[cols, :])
            av = a_ref[:, cols].astype(F32)
            bv = b_ref[:, cols].astype(F32)
            sg = _sigmoid(av)
            sl = av * sg
            da = du * bv * (sg * (1.0 + av * (1.0 - sg)))
            db = du * sl
            da_ref[:, cols] = da.astype(BF16)
            db_ref[:, cols] = db.astype(BF16)
            u_ref[:, cols] = (sl * bv).astype(BF16)
        dh = _dot(da_ref[...], w1_ref[...]) + _dot(db_ref[...], w3_ref[...])
        n, r, _ = _rms(x_ref[...], g_ref[...])
        dxr, dg = _rms_bwd(dh, n, r, g_ref[...])
        dx = dxo_ref[...] + dxr
        dx_ref[...] = dx
        dxb_ref[...] = dx.astype(BF16)
        dg_ref[0:1, :] += dg
        if n_c:
            @pl.when(pl.program_id(0) == ni - 1)
            def _():
                copies, local = exchange()
                _wait_all(copies)
                local.wait()

    row = pl.BlockSpec((tm, d), lambda i: (i, 0))
    hid = pl.BlockSpec((tm, f), lambda i: (i, 0))
    extra = [] if sums is None else [sums]
    sems = [pltpu.SemaphoreType.DMA((3,)), pltpu.SemaphoreType.DMA((3,)), pltpu.SemaphoreType.DMA] if n_c else []
    out = pl.pallas_call(
        body, name="ffn_bwd_reduce" if n_c else "ffn_bwd", grid=(ni,),
        in_specs=[row, row, _resident(g.shape), hid, hid, _resident(w1t.shape), _resident(w3t.shape), _resident(w2.shape)]
        + [ANY] * n_c,
        out_specs=[row, pl.BlockSpec((8, d), lambda i: (0, 0)), row, row, hid, hid, hid, row] + [ANY] * n_c,
        out_shape=[jax.ShapeDtypeStruct((s, d), F32), jax.ShapeDtypeStruct((8, d), F32),
                   jax.ShapeDtypeStruct((s, d), BF16), jax.ShapeDtypeStruct((s, d), BF16),
                   jax.ShapeDtypeStruct((s, f), BF16), jax.ShapeDtypeStruct((s, f), BF16), jax.ShapeDtypeStruct((s, f), BF16),
                   jax.ShapeDtypeStruct((s, d), BF16)]
        + [jax.ShapeDtypeStruct(e.shape, e.dtype) for e in extra],
        scratch_shapes=sems,
        compiler_params=_params("arbitrary"),
    )(dxo, x, g, a, b, w1t, w3t, w2, *extra)
    return out


def _dw(am, bm):
    s, r = am.shape
    n = bm.shape[1]
    rb = r // 2 if r > 1024 and (r // 2) % LANES == 0 else r
    ts = _tile(s, 2048)
    ni = s // ts

    def body(am_ref, bm_ref, o_ref, acc_s):
        i = pl.program_id(1)

        @pl.when(i == 0)
        def _():
            acc_s[...] = jnp.zeros_like(acc_s)

        acc_s[...] += lax.dot_general(am_ref[...], bm_ref[...], TN, preferred_element_type=F32)

        @pl.when(i == ni - 1)
        def _():
            o_ref[...] = acc_s[...].astype(BF16)

    return pl.pallas_call(
        body, name="dw", grid=(r // rb, ni),
        in_specs=[pl.BlockSpec((ts, rb), lambda k, i: (i, k)), pl.BlockSpec((ts, n), lambda k, i: (i, 0))],
        out_specs=pl.BlockSpec((rb, n), lambda k, i: (k, 0)),
        out_shape=jax.ShapeDtypeStruct((r, n), BF16),
        scratch_shapes=[pltpu.VMEM((rb, n), F32)],
        compiler_params=_params("arbitrary", "arbitrary"),
    )(am, bm)


def _proj_fwd(x, g, w_int):
    s, d = x.shape
    f = w_int.shape[0]
    tm, tf = _tile(s, 1024), 512
    nj = f // tf

    def body(x_ref, g_ref, w_ref, p_ref, h_s):
        @pl.when(pl.program_id(1) == 0)
        def _():
            h_s[...] = _rms(x_ref[...], g_ref[...])[2].astype(BF16)

        p_ref[...] = _dot_nt(h_s[...], w_ref[...])

    return pl.pallas_call(
        body, name="proj_fwd", grid=(s // tm, nj),
        in_specs=[pl.BlockSpec((tm, d), lambda i, j: (i, 0)), pl.BlockSpec((1, d), lambda i, j: (0, 0)),
                  pl.BlockSpec((tf, d), lambda i, j: (j, 0))],
        out_specs=pl.BlockSpec((tm, tf), lambda i, j: (i, j)),
        out_shape=jax.ShapeDtypeStruct((s, f), F32),
        scratch_shapes=[pltpu.VMEM((tm, d), BF16)],
        compiler_params=_params("arbitrary", "arbitrary"),
    )(x, g, w_int)


def _proj_bwd(dxo, x, g, dp, w_int):
    s, d = x.shape
    f = w_int.shape[0]
    tm = _tile(s, 512)

    def body(dxo_ref, x_ref, g_ref, dp_ref, w_ref, dx_ref, dg_ref, h_ref):
        @pl.when(pl.program_id(0) == 0)
        def _():
            dg_ref[...] = jnp.zeros_like(dg_ref)

        n, r, h = _rms(x_ref[...], g_ref[...])
        h_ref[...] = h.astype(BF16)
        dxr, dg = _rms_bwd(_dot(dp_ref[...], w_ref[...]), n, r, g_ref[...])
        dx_ref[...] = dxo_ref[...] + dxr
        dg_ref[0:1, :] += dg

    row = pl.BlockSpec((tm, d), lambda i: (i, 0))
    return pl.pallas_call(
        body, name="proj_bwd", grid=(s // tm,),
        in_specs=[row, row, _resident(g.shape), pl.BlockSpec((tm, f), lambda i: (i, 0)), _resident(w_int.shape)],
        out_specs=[row, pl.BlockSpec((8, d), lambda i: (0, 0)), row],
        out_shape=[jax.ShapeDtypeStruct((s, d), F32), jax.ShapeDtypeStruct((8, d), F32), jax.ShapeDtypeStruct((s, d), BF16)],
        compiler_params=_params("arbitrary"),
    )(dxo, x, g, dp, w_int)


C = D_GROUP


def _piece(ref, k):
    return ref[:, k * C:(k + 1) * C]


def _up(v, r):
    return v if r == 0 else pltpu.roll(v, v.shape[0] - r, 0)


def _down(v, r):
    return v if r == 0 else pltpu.roll(v, r, 0)


def _lane_group():
    lane = lax.broadcasted_iota(jnp.int32, (1, C), 1)
    return (lane >= POOL_GROUP).astype(jnp.int32) + (lane >= 2 * POOL_GROUP).astype(jnp.int32) + (
        lane >= 3 * POOL_GROUP).astype(jnp.int32)


def _by_group(grp, v2, v4, v8, v16):
    return jnp.where(grp == 0, v2, jnp.where(grp == 1, v4, jnp.where(grp == 2, v8, v16)))


def _pool_count(grp, row0, t):
    pos = (row0 + lax.broadcasted_iota(jnp.int32, (t, C), 0) + 1).astype(F32)
    return jnp.minimum(pos, _by_group(grp, 2.0, 4.0, 8.0, 16.0))


def _trailing_sums(ext, grp, t):
    s2 = ext + _down(ext, 1)
    s4 = s2 + _down(s2, 2)
    s8 = s4 + _down(s4, 4)
    s16 = s8 + _down(s8, 8)
    return _by_group(grp, s2, s4, s8, s16)[HALO:HALO + t]


def _leading_sums(ext, grp, t):
    s2 = ext + _up(ext, 1)
    s4 = s2 + _up(s2, 2)
    s8 = s4 + _up(s4, 4)
    s16 = s8 + _up(s8, 8)
    return _by_group(grp, s2, s4, s8, s16)[0:t]


def _head_select(r4, grp):
    out = jnp.where(grp == 0, r4[0:CHUNK], 0.0)
    for h in range(1, N_HEADS):
        out = out + jnp.where(grp == h, r4[h * CHUNK:(h + 1) * CHUNK], 0.0)
    return out


def _conv_taps():
    return [(k, (k + 2) % 8, (k + 2) - (k + 2) % 8) for k in range(CONF_KERNEL)]


def _mixer_fwd(p, x1, cw, vec, pool_w, wstack, bias, w_out):
    s, d = x1.shape
    t = _tile(s, T_MIX_FWD)
    n_ext = t + HALO
    dm = w_out.shape[0]

    def body(p_ref, x1_ref, cw_ref, vec_ref, pw_ref, ws_ref, bias_ref, wo_ref, x2_ref, mix_s, cy_s, cq_s, cx_s):
        i = pl.program_id(0)

        @pl.when(i == 0)
        def _():
            cy_s[...] = jnp.zeros_like(cy_s)
            cq_s[...] = jnp.zeros_like(cq_s)
            cx_s[...] = jnp.zeros_like(cx_s)

        grp = _lane_group()
        y = _piece(p_ref, 0) * _sigmoid(_piece(p_ref, 1))
        ext = jnp.concatenate([cy_s[...], y], axis=0)
        cy_s[...] = y[t - HALO:t]
        z = jnp.broadcast_to(vec_ref[0:1, :], (t, C))
        shifted = {}
        for k, r, off in _conv_taps():
            if r not in shifted:
                shifted[r] = _up(ext, r)
            z = z + cw_ref[k:k + 1, :] * shifted[r][off:off + t]
        ln = _ln_fwd(z, vec_ref[1:2, :], vec_ref[2:3, :])[2]
        mix_s[:, 0:C] = (ln * _sigmoid(ln)).astype(BF16)
        q = _piece(p_ref, 3) * _piece(p_ref, 4)
        ext = jnp.concatenate([cq_s[...], q], axis=0)
        cq_s[...] = q[t - HALO:t]
        cz = vec_ref[8:9, :] * q + vec_ref[7:8, :] * _down(ext, 1)[HALO:] + vec_ref[6:7, :] * _down(ext, 2)[HALO:]
        mix_s[:, C:2 * C] = (_piece(p_ref, 2) * cz).astype(BF16)
        xp = _piece(p_ref, 5)
        ext = jnp.concatenate([cx_s[...], xp], axis=0)
        cx_s[...] = xp[t - HALO:t]
        dd = _trailing_sums(ext, grp, t) / _pool_count(grp, i * t, t) - xp
        mix_s[:, 2 * C:3 * C] = (_dot(dd.astype(BF16), pw_ref[...]) * vec_ref[3:4, :]).astype(BF16)
        vln = _ln_fwd(_piece(p_ref, 7), vec_ref[4:5, :], vec_ref[5:6, :])[2].astype(BF16)
        for n in range(t // CHUNK):
            rows = slice(n * CHUNK, (n + 1) * CHUNK)
            mixed = _head_select(_dot(ws_ref[...], vln[rows]), grp) + bias_ref[...]
            mix_s[rows, 3 * C:4 * C] = (p_ref[rows, 6 * C:7 * C] * mixed).astype(BF16)
        x2_ref[...] = x1_ref[...] + _dot(mix_s[...], wo_ref[...])

    full = lambda a: pl.BlockSpec(a.shape, lambda i: (0, 0))
    return pl.pallas_call(
        body, name="mixer_fwd", grid=(s // t,),
        in_specs=[pl.BlockSpec((t, p.shape[1]), lambda i: (i, 0)), pl.BlockSpec((t, d), lambda i: (i, 0)),
                  full(cw), full(vec), full(pool_w), full(wstack), full(bias), full(w_out)],
        out_specs=[pl.BlockSpec((t, d), lambda i: (i, 0)), pl.BlockSpec((t, dm), lambda i: (i, 0))],
        out_shape=[jax.ShapeDtypeStruct((s, d), F32), jax.ShapeDtypeStruct((s, dm), BF16)],
        scratch_shapes=[pltpu.VMEM((HALO, C), F32)] * 3,
        compiler_params=_params("arbitrary"),
    )(p, x1, cw, vec, pool_w, wstack, bias, w_out)


def _mixer_bwd(dx2, p, cw, vec, pool_w, wstack, wstack_t, bias, tril4, head_rows, w_out):
    s, d = dx2.shape
    t = _tile(s, T_MIX_BWD)
    nt = s // t
    n_ext = t + HALO
    hb = t // HALO

    def body(dx2_ref, p_ref, ph_ref, cw_ref, vec_ref, pw_ref, ws_ref, wst_ref, bias_ref, tril_ref, hr_ref, wo_ref,
             dp_ref, dcw_ref, dvec_ref, dpool_ref, dws_ref, dbs_ref, cdz_s, cdc_s, cf_s, vy_s, dvl_s, dbias_s):
        i = pl.program_id(0)
        tile = nt - 1 - i

        @pl.when(i == 0)
        def _():
            for ref in (cdz_s, cdc_s, cf_s, dbias_s, dcw_ref, dvec_ref, dpool_ref, dws_ref, dbs_ref):
                ref[...] = jnp.zeros_like(ref)

        grp = _lane_group()
        first = jnp.where(tile > 0, 1.0, 0.0)
        dmix = _dot_nt(dx2_ref[...].astype(BF16), wo_ref[...])
        d_a, d_b, d_c, d_d = (dmix[:, k * C:(k + 1) * C] for k in range(4))

        def acc_vec(row, v):
            dvec_ref[row:row + 1, :] += jnp.sum(v, axis=0, keepdims=True)

        val, gate = _piece(p_ref, 0), _piece(p_ref, 1)
        sgate = _sigmoid(gate)
        y = val * sgate
        y_halo = ph_ref[:, 0:C] * _sigmoid(ph_ref[:, C:2 * C]) * first
        ext = jnp.concatenate([y_halo, y], axis=0)
        for r in range(8):
            vy_s[r] = _up(ext, r)
        z = jnp.broadcast_to(vec_ref[0:1, :], (t, C))
        for k, r, off in _conv_taps():
            z = z + cw_ref[k:k + 1, :] * vy_s[r, off:off + t, :]
        zn, rs, ln = _ln_fwd(z, vec_ref[1:2, :], vec_ref[2:3, :])
        sg = _sigmoid(ln)
        dln = d_a * (sg * (1.0 + ln * (1.0 - sg)))
        dz, dg, db = _ln_bwd(dln, zn, rs, vec_ref[1:2, :])
        dvec_ref[1:2, :] += dg
        dvec_ref[2:3, :] += db
        acc_vec(0, dz)
        for k, r, off in _conv_taps():
            dcw_ref[k:k + 1, :] += jnp.sum(dz * vy_s[r, off:off + t, :], axis=0, keepdims=True)
        ext = jnp.concatenate([dz, cdz_s[...]], axis=0)
        cdz_s[...] = dz[0:HALO]
        dy = jnp.zeros((t, C), F32)
        shifted = {}
        for k in range(CONF_KERNEL):
            m = CONF_KERNEL - 1 - k
            r, off = m % 8, m - m % 8
            if r not in shifted:
                shifted[r] = _up(ext, r)
            dy = dy + cw_ref[k:k + 1, :] * shifted[r][off:off + t]
        dp_ref[:, 0:C] = (dy * sgate).astype(BF16)
        dp_ref[:, C:2 * C] = (dy * val * sgate * (1.0 - sgate)).astype(BF16)

        sb, sc, sx = _piece(p_ref, 2), _piece(p_ref, 3), _piece(p_ref, 4)
        q = sc * sx
        q_halo = ph_ref[:, 3 * C:4 * C] * ph_ref[:, 4 * C:5 * C] * first
        ext = jnp.concatenate([q_halo, q], axis=0)
        q1, q2 = _down(ext, 1)[HALO:], _down(ext, 2)[HALO:]
        cz = vec_ref[8:9, :] * q + vec_ref[7:8, :] * q1 + vec_ref[6:7, :] * q2
        dcz = d_b * sb
        dp_ref[:, 2 * C:3 * C] = (d_b * cz).astype(BF16)
        acc_vec(8, dcz * q)
        acc_vec(7, dcz * q1)
        acc_vec(6, dcz * q2)
        ext = jnp.concatenate([dcz, cdc_s[...]], axis=0)
        cdc_s[...] = dcz[0:HALO]
        dq = vec_ref[8:9, :] * dcz + vec_ref[7:8, :] * _up(ext, 1)[0:t] + vec_ref[6:7, :] * _up(ext, 2)[0:t]
        dp_ref[:, 3 * C:4 * C] = (dq * sx).astype(BF16)
        dp_ref[:, 4 * C:5 * C] = (dq * sc).astype(BF16)

        xp = _piece(p_ref, 5)
        ext = jnp.concatenate([ph_ref[:, 5 * C:6 * C] * first, xp], axis=0)
        cnt = _pool_count(grp, tile * t, t)
        dd = (_trailing_sums(ext, grp, t) / cnt - xp).astype(BF16)
        e2 = _dot(dd, pw_ref[...])
        acc_vec(3, d_c * e2)
        de = (d_c * vec_ref[3:4, :]).astype(BF16)
        dpool_ref[...] += _dot(_t_bf16(dd.astype(F32)), de)
        ddd = _dot_nt(de, pw_ref[...])
        fq = ddd / cnt
        ext = jnp.concatenate([fq, cf_s[...]], axis=0)
        cf_s[...] = fq[0:HALO]
        dp_ref[:, 5 * C:6 * C] = (_leading_sums(ext, grp, t) - ddd).astype(BF16)

        vn, vrs, vlnf = _ln_fwd(_piece(p_ref, 7), vec_ref[4:5, :], vec_ref[5:6, :])
        vln = vlnf.astype(BF16)
        for n in range(t // CHUNK):
            rows = slice(n * CHUNK, (n + 1) * CHUNK)
            mixed = _head_select(_dot(ws_ref[...], vln[rows]), grp) + bias_ref[...]
            dd_n = d_d[rows]
            dp_ref[rows, 6 * C:7 * C] = (dd_n * mixed).astype(BF16)
            dmx = dd_n * p_ref[rows, 6 * C:7 * C]
            dbias_s[...] += dmx
            dmx_b = dmx.astype(BF16)
            dvl_s[rows, :] = _head_select(_dot(wst_ref[...], dmx_b), grp)
            for h in range(N_HEADS):
                hrows = slice(h * CHUNK, (h + 1) * CHUNK)
                dws_ref[hrows, :] += _dot_nt(jnp.where(grp == h, dmx_b, jnp.zeros_like(dmx_b)), vln[rows])
        dvl = dvl_s[...]
        dv, dg, db = _ln_bwd(dvl, vn, vrs, vec_ref[4:5, :])
        dvec_ref[4:5, :] += dg
        dvec_ref[5:6, :] += db
        dp_ref[:, 7 * C:8 * C] = dv.astype(BF16)

        @pl.when(i == nt - 1)
        def _():
            dws_ref[...] = dws_ref[...] * tril_ref[...]
            dbs_ref[...] = lax.dot_general(hr_ref[...], dbias_s[...], NT, precision=lax.Precision.HIGHEST,
                                           preferred_element_type=F32)

    full = lambda a: pl.BlockSpec(a.shape, lambda i: (0, 0))
    acc = lambda shape: pl.BlockSpec(shape, lambda i: (0, 0))
    f = p.shape[1]
    return pl.pallas_call(
        body, name="mixer_bwd", grid=(nt,),
        in_specs=[pl.BlockSpec((t, d), lambda i: (nt - 1 - i, 0)), pl.BlockSpec((t, f), lambda i: (nt - 1 - i, 0)),
                  pl.BlockSpec((HALO, f), lambda i: (jnp.maximum((nt - 1 - i) * hb - 1, 0), 0)),
                  full(cw), full(vec), full(pool_w), full(wstack), full(wstack_t), full(bias), full(tril4), full(head_rows),
                  full(w_out)],
        out_specs=[pl.BlockSpec((t, f), lambda i: (nt - 1 - i, 0)), acc((32, C)), acc((16, C)), acc((C, C)),
                   acc((N_HEADS * CHUNK, CHUNK)), acc((8, CHUNK))],
        out_shape=[jax.ShapeDtypeStruct((s, f), BF16), jax.ShapeDtypeStruct((32, C), F32), jax.ShapeDtypeStruct((16, C), F32),
                   jax.ShapeDtypeStruct((C, C), F32), jax.ShapeDtypeStruct((N_HEADS * CHUNK, CHUNK), F32),
                   jax.ShapeDtypeStruct((8, CHUNK), F32)],
        scratch_shapes=[pltpu.VMEM((HALO, C), F32)] * 3 + [pltpu.VMEM((8, n_ext, C), F32), pltpu.VMEM((t, C), F32),
                                                            pltpu.VMEM((CHUNK, C), F32)],
        compiler_params=_params("arbitrary"),
    )(dx2, p, p, cw, vec, pool_w, wstack, wstack_t, bias, tril4, head_rows, w_out)


def _loss_bwd(x, g, target):
    s, d = x.shape
    tm = _tile(s, 512)

    def body(x_ref, g_ref, t_ref, dx_ref, dg_ref, loss_ref):
        @pl.when(pl.program_id(0) == 0)
        def _():
            dg_ref[...] = jnp.zeros_like(dg_ref)
            loss_ref[...] = jnp.zeros_like(loss_ref)

        n, r, y = _rms(x_ref[...], g_ref[...])
        err = y - t_ref[...]
        loss_ref[...] += 0.5 * jnp.sum(jnp.mean(err * err, axis=-1, keepdims=True), axis=0, keepdims=True)
        dxr, dg = _rms_bwd(err * (1.0 / d), n, r, g_ref[...])
        dx_ref[...] = dxr
        dg_ref[0:1, :] += dg

    row = pl.BlockSpec((tm, d), lambda i: (i, 0))
    return pl.pallas_call(
        body, name="loss_bwd", grid=(s // tm,),
        in_specs=[row, pl.BlockSpec((1, d), lambda i: (0, 0)), row],
        out_specs=[row, pl.BlockSpec((8, d), lambda i: (0, 0)), pl.BlockSpec((8, LANES), lambda i: (0, 0))],
        out_shape=[jax.ShapeDtypeStruct((s, d), F32), jax.ShapeDtypeStruct((8, d), F32), jax.ShapeDtypeStruct((8, LANES), F32)],
        compiler_params=_params("arbitrary"),
    )(x, g, target)


def _adamw_math(w, g, m, v):
    m = ADAM_B1 * m + (1.0 - ADAM_B1) * g
    v = ADAM_B2 * v + (1.0 - ADAM_B2) * (g * g)
    m_hat = m / (1.0 - ADAM_B1 ** ADAM_STEP)
    v_hat = v / (1.0 - ADAM_B2 ** ADAM_STEP)
    return -ADAM_LR * (m_hat / (jnp.sqrt(v_hat) + ADAM_EPS) + ADAM_WD * w), m, v


def _adamw(w, g, m, v):
    r, c = w.shape
    tr = r // 8 if r % 64 == 0 else r

    def body(w_ref, g_ref, m_ref, v_ref, d_ref, mo_ref, vo_ref):
        d_ref[...], mo_ref[...], vo_ref[...] = _adamw_math(w_ref[...], g_ref[...], m_ref[...], v_ref[...])

    blk = pl.BlockSpec((tr, c), lambda i: (i, 0))
    return pl.pallas_call(
        body, name="adamw", grid=(r // tr,), in_specs=[blk] * 4, out_specs=[blk] * 3,
        out_shape=[jax.ShapeDtypeStruct((r, c), F32)] * 3, compiler_params=_params("arbitrary"),
    )(w, g, m, v)


def _adamw_small(ws, gs, ms, vs):
    n = len(ws)

    def body(*refs):
        ins, outs = refs[:4 * n], refs[4 * n:]
        for k in range(n):
            dl, mo, vo = _adamw_math(ins[k][...], ins[n + k][...], ins[2 * n + k][...], ins[3 * n + k][...])
            outs[k][...], outs[n + k][...], outs[2 * n + k][...] = dl, mo, vo

    vm = pl.BlockSpec(memory_space=pltpu.VMEM)
    out = pl.pallas_call(
        body, name="adamw_small", in_specs=[vm] * (4 * n), out_specs=[vm] * (3 * n),
        out_shape=[jax.ShapeDtypeStruct(a.shape, F32) for a in ws] * 3,
        compiler_params=pltpu.CompilerParams(vmem_limit_bytes=VMEM_LIMIT),
    )(*ws, *gs, *ms, *vs)
    return out[:n], out[n:2 * n], out[2 * n:]


def _where_am_i():
    x, y, c = lax.axis_index("x"), lax.axis_index("y"), lax.axis_index("c")
    chips = [(1 - x, y), (x, 1 - y), (1 - x, 1 - y)]
    return x, y, c, chips


def _chip_id(chip):
    return 2 * chip[0] + chip[1]


def _landing_shapes(shards):
    return [jax.ShapeDtypeStruct((3, a.shape[0] // 2, a.shape[1]), a.dtype) for a in shards]


def _ici_gather_copies(ins, lands, send_sems, recv_sems):
    _, _, c, chips = _where_am_i()
    copies = []
    for a, src in enumerate(ins):
        hr = src.shape[0] // 2
        for j, chip in enumerate(chips):
            copies.append(pltpu.make_async_remote_copy(
                src_ref=src.at[pl.ds(c * hr, hr)], dst_ref=lands[a].at[j], send_sem=send_sems.at[3 * a + j],
                recv_sem=recv_sems.at[3 * a + j], device_id=(*chip, c), device_id_type=MESH))
    return copies


def _ici_reduce_copies(s_ref, o_ref, send_sems, recv_sems, local_sem):
    x, y, c, chips = _where_am_i()
    copies = [pltpu.make_async_remote_copy(src_ref=s_ref.at[_chip_id(chip)], dst_ref=o_ref.at[j], send_sem=send_sems.at[j],
                                           recv_sem=recv_sems.at[j], device_id=(*chip, c), device_id_type=MESH)
              for j, chip in enumerate(chips)]
    return copies, pltpu.make_async_copy(s_ref.at[_chip_id((x, y))], o_ref.at[3], local_sem)


def _wait_all(copies):
    for cp in copies:
        cp.wait_recv()
    for cp in copies:
        cp.wait_send()


def _ici_gather(shards):
    n = len(shards)

    def body(*refs):
        copies = _ici_gather_copies(refs[:n], refs[n:2 * n], refs[2 * n], refs[2 * n + 1])
        for cp in copies:
            cp.start()
        _wait_all(copies)

    return pl.pallas_call(
        body, name="ici_gather", in_specs=[ANY] * n, out_specs=[ANY] * n, out_shape=_landing_shapes(shards),
        scratch_shapes=[pltpu.SemaphoreType.DMA((3 * n,))] * 2,
    )(*shards)


SLOTS = 3


def _sibling_gather(shards, lands):
    n = len(shards)
    d = shards[0].shape[1]
    halves = [a.shape[0] // 2 for a in shards]
    hmax = max(halves)
    chunks = [(a, j, halves[a]) for a in range(n) for j in range(3)]
    nc = len(chunks)

    def body(*refs):
        ins, lnd, outs = refs[:n], refs[n:2 * n], refs[2 * n:3 * n]
        sbuf, rbuf, obuf, ld_sems, take_sems, send_sems, recv_sems, place_sems, own_ld_sems, own_st_sems, credits = refs[3 * n:]
        x, y, c, chips = _where_am_i()
        sibling = (x, y, 1 - c)
        me = _chip_id((x, y))

        def load(i):
            a, j, hr = chunks[i]
            return pltpu.make_async_copy(lnd[a].at[j], sbuf.at[i % SLOTS, pl.ds(0, hr)], ld_sems.at[i % SLOTS])

        def push(i):
            hr, slot = chunks[i][2], i % SLOTS
            return pltpu.make_async_remote_copy(src_ref=sbuf.at[slot, pl.ds(0, hr)], dst_ref=rbuf.at[slot, pl.ds(0, hr)],
                                                send_sem=send_sems.at[slot], recv_sem=recv_sems.at[slot], device_id=sibling,
                                                device_id_type=MESH)

        def take(i):
            a, j, hr = chunks[i]
            return pltpu.make_async_copy(rbuf.at[i % SLOTS, pl.ds(0, hr)],
                                         outs[a].at[_chip_id(chips[j]), pl.ds((1 - c) * hr, hr)], take_sems.at[i % SLOTS])

        def place(i):
            a, j, hr = chunks[i]
            return pltpu.make_async_copy(sbuf.at[i % SLOTS, pl.ds(0, hr)], outs[a].at[_chip_id(chips[j]), pl.ds(c * hr, hr)],
                                         place_sems.at[i % SLOTS])

        own = [(a, h, halves[a]) for a in range(n) for h in range(2)]

        def own_load(k):
            a, h, hr = own[k]
            return pltpu.make_async_copy(ins[a].at[pl.ds(h * hr, hr)], obuf.at[k % 2, pl.ds(0, hr)], own_ld_sems.at[k % 2])

        def own_store(k):
            a, h, hr = own[k]
            return pltpu.make_async_copy(obuf.at[k % 2, pl.ds(0, hr)], outs[a].at[me, pl.ds(h * hr, hr)], own_st_sems.at[k % 2])

        def own_step(k):
            if k < len(own):
                if k >= 2:
                    own_store(k - 2).wait()
                own_load(k).start()
            if 1 <= k <= len(own):
                own_load(k - 1).wait()
                own_store(k - 1).start()

        for i in range(min(2, nc)):
            load(i).start()
        for i in range(nc):
            own_step(i)
            if i >= 1:
                push(i - 1).wait_send()
                place(i - 1).wait()
            if i + 2 < nc:
                load(i + 2).start()
            load(i).wait()
            place(i).start()
            if i >= SLOTS:
                pl.semaphore_wait(credits.at[i % SLOTS], 1)
            push(i).start()
            if i >= 1:
                push(i - 1).wait_recv()
                take(i - 1).start()
            if i >= 2:
                take(i - 2).wait()
                if i - 2 + SLOTS < nc:
                    pl.semaphore_signal(credits.at[(i - 2) % SLOTS], inc=1, device_id=sibling, device_id_type=MESH)
        push(nc - 1).wait_send()
        place(nc - 1).wait()
        push(nc - 1).wait_recv()
        take(nc - 1).start()
        for i in range(max(0, nc - 2), nc):
            take(i).wait()
        for k in range(nc, len(own) + 1):
            own_step(k)
        for k in range(max(0, len(own) - 2), len(own)):
            own_store(k).wait()

    dma = pltpu.SemaphoreType.DMA((SLOTS,))
    dma2 = pltpu.SemaphoreType.DMA((2,))
    return pl.pallas_call(
        body, name="sibling_gather", in_specs=[ANY] * (2 * n), out_specs=[ANY] * n,
        out_shape=[jax.ShapeDtypeStruct((N_CHIPS,) + a.shape, a.dtype) for a in shards],
        scratch_shapes=[pltpu.VMEM((SLOTS, hmax, d), BF16), pltpu.VMEM((SLOTS, hmax, d), BF16), pltpu.VMEM((2, hmax, d), BF16),
                        dma, dma, dma, dma, dma, dma2, dma2, pltpu.SemaphoreType.REGULAR((SLOTS,))],
        compiler_params=pltpu.CompilerParams(vmem_limit_bytes=VMEM_LIMIT),
    )(*shards, *lands)


def _pair_sum(grads):
    n = len(grads)
    halves = [g.shape[1] // 2 for g in grads]
    total, rows = sum(halves), max(halves)
    d = grads[0].shape[2]
    chunks, off = [], 0
    for a in range(n):
        chunks += [(a, k, off, halves[a]) for k in range(N_CHIPS)]
        off += halves[a]
    nc = len(chunks)

    def body(*refs):
        ins, out_ref = refs[:n], refs[n]
        sbuf, rbuf, mbuf, obuf, ls_sems, lm_sems, st_sems, send_sems, recv_sems, credits = refs[n + 1:]
        x, y, c, _ = _where_am_i()
        sibling = (x, y, 1 - c)

        def load_theirs(i):
            a, k, _, hr = chunks[i]
            return pltpu.make_async_copy(ins[a].at[k, pl.ds((1 - c) * hr, hr)], sbuf.at[i % SLOTS, pl.ds(0, hr)], ls_sems.at[i % SLOTS])

        def load_mine(i):
            a, k, _, hr = chunks[i]
            return pltpu.make_async_copy(ins[a].at[k, pl.ds(c * hr, hr)], mbuf.at[i % SLOTS, pl.ds(0, hr)], lm_sems.at[i % SLOTS])

        def push(i):
            hr, slot = chunks[i][3], i % SLOTS
            return pltpu.make_async_remote_copy(src_ref=sbuf.at[slot, pl.ds(0, hr)], dst_ref=rbuf.at[slot, pl.ds(0, hr)],
                                                send_sem=send_sems.at[slot], recv_sem=recv_sems.at[slot], device_id=sibling,
                                                device_id_type=MESH)

        def store(i):
            _, k, o, hr = chunks[i]
            slot = i % SLOTS
            return pltpu.make_async_copy(obuf.at[slot, pl.ds(0, hr)], out_ref.at[k, pl.ds(o, hr)], st_sems.at[slot])

        def start_push(i):
            load_theirs(i).wait()
            if i >= SLOTS:
                pl.semaphore_wait(credits.at[i % SLOTS], 1)
            push(i).start()

        for i in range(min(2, nc)):
            load_theirs(i).start()
            load_mine(i).start()
        start_push(0)
        for i in range(nc):
            hr, slot = chunks[i][3], i % SLOTS
            if i + 2 < nc:
                load_theirs(i + 2).start()
                load_mine(i + 2).start()
            if i + 1 < nc:
                start_push(i + 1)
            push(i).wait_recv()
            push(i).wait_send()
            load_mine(i).wait()
            if i >= SLOTS:
                store(i - SLOTS).wait()
            obuf[slot, 0:hr, :] = (mbuf[slot, 0:hr, :].astype(F32) + rbuf[slot, 0:hr, :].astype(F32)).astype(BF16)
            if i + SLOTS < nc:
                pl.semaphore_signal(credits.at[slot], inc=1, device_id=sibling, device_id_type=MESH)
            store(i).start()
        for i in range(max(0, nc - SLOTS), nc):
            store(i).wait()

    stage = pltpu.VMEM((SLOTS, rows, d), BF16)
    dma = pltpu.SemaphoreType.DMA((SLOTS,))
    return pl.pallas_call(
        body, name="pair_sum", in_specs=[ANY] * n, out_specs=ANY, out_shape=jax.ShapeDtypeStruct((N_CHIPS, total, d), BF16),
        scratch_shapes=[stage, stage, stage, stage, dma, dma, dma, dma, dma, pltpu.SemaphoreType.REGULAR((SLOTS,))],
        compiler_params=pltpu.CompilerParams(vmem_limit_bytes=VMEM_LIMIT),
    )(*grads)


def _reduce_chips(sums):
    def body(s_ref, o_ref, send_sems, recv_sems, local_sem):
        copies, local = _ici_reduce_copies(s_ref, o_ref, send_sems, recv_sems, local_sem)
        local.start()
        for cp in copies:
            cp.start()
        _wait_all(copies)
        local.wait()

    return pl.pallas_call(
        body, name="reduce_chips", in_specs=[ANY], out_specs=ANY, out_shape=jax.ShapeDtypeStruct(sums.shape, BF16),
        scratch_shapes=[pltpu.SemaphoreType.DMA((3,)), pltpu.SemaphoreType.DMA((3,)), pltpu.SemaphoreType.DMA],
    )(sums)


def _sum_share(parts, rows):
    n_l, n = len(parts), len(rows)
    d = parts[0].shape[2]
    halves = [r // 2 for r in rows]
    hmax = max(halves)
    chunks = []
    for l in range(n_l):
        off = 0
        for a in range(n):
            chunks.append((l, a, off, halves[a]))
            off += halves[a]
    nc = len(chunks)

    def body(*refs):
        ins, outs = refs[:n_l], refs[n_l:n_l + n]
        pbuf, obuf, rbuf, ld_sems, keep_sems, take_sems, send_sems, recv_sems, credits = refs[n_l + n:]
        x, y, c, _ = _where_am_i()
        sibling = (x, y, 1 - c)

        def load(i):
            l, _, off, hr = chunks[i]
            return pltpu.make_async_copy(ins[l].at[:, pl.ds(off, hr)], pbuf.at[i % SLOTS, :, pl.ds(0, hr)], ld_sems.at[i % SLOTS])

        def keep(i):
            l, a, _, hr = chunks[i]
            return pltpu.make_async_copy(obuf.at[i % SLOTS, pl.ds(0, hr)], outs[a].at[l, pl.ds(c * hr, hr)], keep_sems.at[i % SLOTS])

        def push(i):
            hr, slot = chunks[i][3], i % SLOTS
            return pltpu.make_async_remote_copy(src_ref=obuf.at[slot, pl.ds(0, hr)], dst_ref=rbuf.at[slot, pl.ds(0, hr)],
                                                send_sem=send_sems.at[slot], recv_sem=recv_sems.at[slot], device_id=sibling,
                                                device_id_type=MESH)

        def take(i):
            l, a, _, hr = chunks[i]
            return pltpu.make_async_copy(rbuf.at[i % SLOTS, pl.ds(0, hr)], outs[a].at[l, pl.ds((1 - c) * hr, hr)],
                                         take_sems.at[i % SLOTS])

        for i in range(min(2, nc)):
            load(i).start()
        for i in range(nc):
            hr, slot = chunks[i][3], i % SLOTS
            if i + 2 < nc:
                load(i + 2).start()
            load(i).wait()
            if i >= SLOTS:
                keep(i - SLOTS).wait()
                push(i - SLOTS).wait_send()
            part = lambda k: pbuf[slot, k, 0:hr, :].astype(F32)
            obuf[slot, 0:hr, :] = ((part(3) + part(0)) + part(1)) + part(2)
            keep(i).start()
            if i >= SLOTS:
                pl.semaphore_wait(credits.at[slot], 1)
            push(i).start()
            if i >= 1:
                push(i - 1).wait_recv()
                take(i - 1).start()
            if i >= 2:
                take(i - 2).wait()
                if i - 2 + SLOTS < nc:
                    pl.semaphore_signal(credits.at[(i - 2) % SLOTS], inc=1, device_id=sibling, device_id_type=MESH)
        push(nc - 1).wait_recv()
        take(nc - 1).start()
        for i in range(max(0, nc - 2), nc):
            take(i).wait()
        for i in range(max(0, nc - SLOTS), nc):
            keep(i).wait()
            push(i).wait_send()

    dma = pltpu.SemaphoreType.DMA((SLOTS,))
    return pl.pallas_call(
        body, name="sum_share", in_specs=[ANY] * n_l, out_specs=[ANY] * n,
        out_shape=[jax.ShapeDtypeStruct((n_l, r, d), F32) for r in rows],
        scratch_shapes=[pltpu.VMEM((SLOTS, N_CHIPS, hmax, d), BF16), pltpu.VMEM((SLOTS, hmax, d), F32),
                        pltpu.VMEM((SLOTS, hmax, d), F32), dma, dma, dma, dma, dma, pltpu.SemaphoreType.REGULAR((SLOTS,))],
        compiler_params=pltpu.CompilerParams(vmem_limit_bytes=VMEM_LIMIT),
    )(*parts)


def _all_gather_small(block, reduce):
    m, n = block.shape

    def body(x_ref, out_ref, *scratch):
        if reduce:
            all_ref, send_sems, recv_sems, local_sem = scratch
        else:
            all_ref = out_ref
            send_sems, recv_sems, local_sem = scratch
        x, y, c, chips = _where_am_i()
        me, sibling = (x, y, c), (x, y, 1 - c)

        def rows(px, py, pc):
            return all_ref.at[pl.ds((4 * px + 2 * py + pc) * m, m), :]

        def copy(k, blk, to, src=None):
            return pltpu.make_async_remote_copy(src_ref=rows(*blk) if src is None else src, dst_ref=rows(*blk),
                                                send_sem=send_sems.at[k], recv_sem=recv_sems.at[k], device_id=to,
                                                device_id_type=MESH)

        mine = pltpu.make_async_copy(x_ref, rows(*me), local_sem)
        mine.start()
        first = [copy(0, me, sibling, src=x_ref)]
        first += [copy(1 + j, me, (*chip, c), src=x_ref) for j, chip in enumerate(chips)]
        for cp in first:
            cp.start()
        passed = [copy(4 + j, (*chip, c), sibling) for j, chip in enumerate(chips)]
        for j, chip in enumerate(chips):
            copy(1 + j, (*chip, c), me).wait_recv()
            passed[j].start()
        copy(0, sibling, me).wait_recv()
        for j, chip in enumerate(chips):
            copy(4 + j, (*chip, 1 - c), me).wait_recv()
        for cp in first + passed:
            cp.wait_send()
        mine.wait()
        if reduce:
            total = all_ref[0:m, :]
            for dev in range(1, N_DEV):
                total = total + all_ref[dev * m:(dev + 1) * m, :]
            out_ref[...] = total

    vm = pl.BlockSpec(memory_space=pltpu.VMEM)
    sems = [pltpu.SemaphoreType.DMA((7,)), pltpu.SemaphoreType.DMA((7,)), pltpu.SemaphoreType.DMA]
    return pl.pallas_call(
        body, name="reduce_small" if reduce else "gather_small", in_specs=[vm], out_specs=vm,
        out_shape=jax.ShapeDtypeStruct((m, n) if reduce else (N_DEV * m, n), F32),
        scratch_shapes=([pltpu.VMEM((N_DEV * m, n), F32)] if reduce else []) + sems,
        compiler_params=pltpu.CompilerParams(vmem_limit_bytes=VMEM_LIMIT),
    )(block)


def _pack(arrays):
    flat = jnp.concatenate([a.reshape(-1) for a in arrays])
    pad = (-flat.shape[0]) % (8 * LANES)
    return jnp.pad(flat, (0, pad)).reshape(-1, LANES)


def _unpack(buf, shapes):
    flat = buf.reshape(-1)
    out, off = [], 0
    for shp in shapes:
        size = 1
        for dim in shp:
            size *= dim
        out.append(flat[off:off + size].reshape(shp))
        off += size
    return out


BIG = ("ffn1_w1", "ffn1_w3", "ffn1_w2", "w_in", "w_out", "ffn2_w1", "ffn2_w3", "ffn2_w2")
TRANSPOSED = ("ffn1_w1", "ffn1_w3", "w_in", "ffn2_w1", "ffn2_w3")
SMALL = ("ffn1_norm", "mix_norm", "conf_conv_w", "conf_conv_b", "conf_ln_g", "conf_ln_b", "sconv_w", "pool_w", "pool_scale",
         "gmlp_ln_g", "gmlp_ln_b", "gmlp_w_s", "gmlp_b_s", "ffn2_norm", "final_norm")
ORDER = ("ffn1_norm", "ffn1_w1", "ffn1_w3", "ffn1_w2", "mix_norm", "w_in", "conf_conv_w", "conf_conv_b", "conf_ln_g", "conf_ln_b",
         "sconv_w", "pool_w", "pool_scale", "gmlp_ln_g", "gmlp_ln_b", "gmlp_w_s", "gmlp_b_s", "w_out", "ffn2_norm", "ffn2_w1",
         "ffn2_w3", "ffn2_w2", "final_norm")


def _as2d(a):
    return a.reshape(-1, a.shape[-1])


def kernel(x, ffn1_norm, ffn1_w1, ffn1_w3, ffn1_w2, mix_norm, w_in, conf_conv_w, conf_conv_b, conf_ln_g, conf_ln_b, sconv_w, pool_w, pool_scale, gmlp_ln_g, gmlp_ln_b, gmlp_w_s, gmlp_b_s, w_out, ffn2_norm, ffn2_w1, ffn2_w3, ffn2_w2, final_norm, loss_target, m_ffn1_norm, m_ffn1_w1, m_ffn1_w3, m_ffn1_w2, m_mix_norm, m_w_in, m_conf_conv_w, m_conf_conv_b, m_conf_ln_g, m_conf_ln_b, m_sconv_w, m_pool_w, m_pool_scale, m_gmlp_ln_g, m_gmlp_ln_b, m_gmlp_w_s, m_gmlp_b_s, m_w_out, m_ffn2_norm, m_ffn2_w1, m_ffn2_w3, m_ffn2_w2, m_final_norm, v_ffn1_norm, v_ffn1_w1, v_ffn1_w3, v_ffn1_w2, v_mix_norm, v_w_in, v_conf_conv_w, v_conf_conv_b, v_conf_ln_g, v_conf_ln_b, v_sconv_w, v_pool_w, v_pool_scale, v_gmlp_ln_g, v_gmlp_ln_b, v_gmlp_w_s, v_gmlp_b_s, v_w_out, v_ffn2_norm, v_ffn2_w1, v_ffn2_w3, v_ffn2_w2, v_final_norm):
    given = dict(locals())
    w = {k: given[k] for k in ORDER}
    mom = {k: given["m_" + k] for k in ORDER}
    var = {k: given["v_" + k] for k in ORDER}
    n_l = ffn1_w1.shape[0]
    xs = x[0]
    d = xs.shape[1]
    chip = 2 * lax.axis_index("x") + lax.axis_index("y")

    def shard(name, l):
        a = w[name][l]
        return (a.T if name in TRANSPOSED else a).astype(BF16)

    def finish_gather(l, lands):
        out = _sibling_gather([shard(name, l) for name in BIG], lands)
        return {name: g.reshape(-1, d) for name, g in zip(BIG, out)}

    half = len(BIG) // 2
    shard_rows = [w[name].shape[2] if name in TRANSPOSED else w[name].shape[1] for name in BIG]

    conv_shapes = [conf_conv_w.shape, sconv_w.shape]
    conv_all = _all_gather_small(_pack([conf_conv_w, sconv_w]), reduce=False)
    conv_all = conv_all.reshape(N_CHIPS, 2, -1)[:, 0]
    conf_full, sconv_full = [jnp.concatenate([_unpack(conv_all[k], conv_shapes)[a] for k in range(N_CHIPS)], axis=-1)
                             for a in range(2)]

    lane = jnp.arange(C) // HEAD_DIM
    head_rows = (jnp.arange(8)[:, None] == lane[None, :]).astype(F32)
    tril = jnp.tril(jnp.ones((CHUNK, CHUNK), F32))
    tril4 = jnp.tile(tril, (N_HEADS, 1))
    mixer_consts = []
    for l in range(n_l):
        cw = jnp.pad(conf_full[l], ((0, 32 - CONF_KERNEL), (0, 0)))
        vec = jnp.concatenate([conf_conv_b[l][None], conf_ln_g[l][None], conf_ln_b[l][None], pool_scale[l][None],
                               gmlp_ln_g[l][None], gmlp_ln_b[l][None], sconv_full[l], jnp.zeros((7, C), F32)], axis=0)
        eye = jnp.eye(len(pool_w[l]), dtype=F32)
        pool_blk = (eye[:, None, :, None] * pool_w[l][:, :, None, :]).reshape(C, C).astype(BF16)
        ws = gmlp_w_s[l] * tril[None]
        wstack = ws.reshape(N_HEADS * CHUNK, CHUNK).astype(BF16)
        wstack_t = jnp.swapaxes(ws, 1, 2).reshape(N_HEADS * CHUNK, CHUNK).astype(BF16)
        bias = jnp.repeat(gmlp_b_s[l].T, HEAD_DIM, axis=1)
        mixer_consts.append((cw, vec, pool_blk, wstack, wstack_t, bias))

    saved = []
    cur = xs
    gathered = [finish_gather(0, _ici_gather([shard(name, 0) for name in BIG]))]
    for l in range(n_l):
        gw = gathered[l]
        cw, vec, pool_blk, wstack, wstack_t, bias = mixer_consts[l]
        nxt = [shard(name, l + 1) for name in BIG] if l + 1 < n_l else []
        x0 = cur
        x1, a1, b1, lands1 = _ffn_fwd(x0, ffn1_norm[l][None], gw["ffn1_w1"], gw["ffn1_w3"], gw["ffn1_w2"], nxt[:half])
        p = _proj_fwd(x1, mix_norm[l][None], gw["w_in"])
        x2, mix = _mixer_fwd(p, x1, cw, vec, pool_blk, wstack, bias, gw["w_out"])
        x3, a2, b2, lands2 = _ffn_fwd(x2, ffn2_norm[l][None], gw["ffn2_w1"], gw["ffn2_w3"], gw["ffn2_w2"], nxt[half:])
        if nxt:
            gathered.append(finish_gather(l + 1, lands1 + lands2))
        saved.append((x0, x1, x2, a1, b1, a2, b2, p, mix))
        cur = x3

    dx, dg_final, loss_part = _loss_bwd(cur, final_norm[None], loss_target[0])

    small_parts = [None] * n_l
    reduced_halves = [None] * n_l
    pending = None
    for l in reversed(range(n_l)):
        gw = gathered[l]
        cw, vec, pool_blk, wstack, wstack_t, bias = mixer_consts[l]
        x0, x1, x2, a1, b1, a2, b2, p, mix = saved[l]
        big = {}
        out = _ffn_bwd(dx, x2, ffn2_norm[l][None], a2, b2, gw["ffn2_w1"], gw["ffn2_w3"], gw["ffn2_w2"], pending)
        dx, dg_ffn2, h, dy, da, db, u, dx_bf = out[:8]
        if pending is not None:
            reduced_halves[l + 1] = out[8]
        big["ffn2_w1"], big["ffn2_w3"], big["ffn2_w2"] = _dw(da, h), _dw(db, h), _dw(u, dy)
        big["w_out"] = _dw(mix, dx_bf)
        dp, dcw, dvec, dpool, dws, dbs = _mixer_bwd(dx, p, cw, vec, pool_blk, wstack, wstack_t, bias, tril4, head_rows, gw["w_out"])
        dx, dg_mix, h = _proj_bwd(dx, x1, mix_norm[l][None], dp, gw["w_in"])
        big["w_in"] = _dw(dp, h)
        dx, dg_ffn1, h, dy, da, db, u, _ = _ffn_bwd(dx, x0, ffn1_norm[l][None], a1, b1, gw["ffn1_w1"], gw["ffn1_w3"], gw["ffn1_w2"])
        big["ffn1_w1"], big["ffn1_w3"], big["ffn1_w2"] = _dw(da, h), _dw(db, h), _dw(u, dy)
        small_parts[l] = [dg_ffn1[0], dg_mix[0], dg_ffn2[0], dcw, dvec, dpool, dws, dbs]
        pending = _pair_sum([big[name].reshape(N_CHIPS, -1, d) for name in BIG])
    reduced_halves[0] = _reduce_chips(pending)
    grad_x = dx[None]

    full = dict(zip(BIG, _sum_share(reduced_halves, shard_rows)))
    grad = {name: (jnp.swapaxes(full[name], 1, 2) if name in TRANSPOSED else full[name]) for name in BIG}

    part_shapes = [a.shape for a in small_parts[0]]
    tail = [dg_final[0], loss_part[0]]
    packed = _pack([a for l in range(n_l) for a in small_parts[l]] + tail)
    summed = _unpack(_all_gather_small(packed, reduce=True), part_shapes * n_l + [a.shape for a in tail])
    per_layer = [summed[l * len(part_shapes):(l + 1) * len(part_shapes)] for l in range(n_l)]
    stack = lambda k: jnp.stack([per_layer[l][k] for l in range(n_l)])
    dcw_all, dvec_all, dpool_all, dws_all, dbs_all = stack(3), stack(4), stack(5), stack(6), stack(7)
    loss = summed[-1][0]
    chip_cols = lambda a: lax.dynamic_slice_in_dim(a, chip * (C // N_CHIPS), C // N_CHIPS, axis=2)
    n_pool = pool_w.shape[1]
    grad.update(
        ffn1_norm=stack(0), mix_norm=stack(1), ffn2_norm=stack(2), final_norm=summed[-2],
        conf_conv_w=chip_cols(dcw_all[:, :CONF_KERNEL]), conf_conv_b=dvec_all[:, 0], conf_ln_g=dvec_all[:, 1],
        conf_ln_b=dvec_all[:, 2], pool_scale=dvec_all[:, 3], gmlp_ln_g=dvec_all[:, 4], gmlp_ln_b=dvec_all[:, 5],
        sconv_w=chip_cols(dvec_all[:, 6:6 + SHORT_KERNEL]),
        pool_w=jnp.stack([dpool_all[:, g * POOL_GROUP:(g + 1) * POOL_GROUP, g * POOL_GROUP:(g + 1) * POOL_GROUP]
                          for g in range(n_pool)], axis=1),
        gmlp_w_s=dws_all.reshape(n_l, N_HEADS, CHUNK, CHUNK), gmlp_b_s=dbs_all[:, :N_HEADS],
    )

    delta, new_m, new_v = {}, {}, {}
    for name in BIG:
        shp = w[name].shape
        out = _adamw(_as2d(w[name]), _as2d(grad[name]), _as2d(mom[name]), _as2d(var[name]))
        delta[name], new_m[name], new_v[name] = (o.reshape(shp) for o in out)
    ds, ms, vs = _adamw_small([_as2d(w[k]) if w[k].ndim > 1 else w[k][None] for k in SMALL],
                              [_as2d(grad[k]) if grad[k].ndim > 1 else grad[k][None] for k in SMALL],
                              [_as2d(mom[k]) if mom[k].ndim > 1 else mom[k][None] for k in SMALL],
                              [_as2d(var[k]) if var[k].ndim > 1 else var[k][None] for k in SMALL])
    for k, dl, mo, vo in zip(SMALL, ds, ms, vs):
        delta[k], new_m[k], new_v[k] = dl.reshape(w[k].shape), mo.reshape(w[k].shape), vo.reshape(w[k].shape)

    return (loss, grad_x, *[grad[k] for k in ORDER], *[delta[k] for k in ORDER], *[new_m[k] for k in ORDER],
            *[new_v[k] for k in ORDER])
```

```python
import functools

import jax
import jax.numpy as jnp
from jax import lax
from jax.experimental import pallas as pl
from jax.experimental.pallas import tpu as pltpu

F32 = jnp.float32
BF16 = jnp.bfloat16
MESH = pl.DeviceIdType.MESH
ANY = pl.BlockSpec(memory_space=pl.ANY)

EPS = 1e-6
FFN_RESIDUAL = 0.5
D_GROUP = 256
CONF_KERNEL = 31
SHORT_KERNEL = 3
POOL_GROUP = 64
CHUNK = 128
N_HEADS = 4
HEAD_DIM = 64
HALO = 32
N_CHIPS = 4
N_DEV = 8
LANES = 128
MXU_TILE = 256
VMEM_LIMIT = 56 * 2**20
T_MIX_FWD = 512
T_MIX_BWD = 256

ADAM_LR = 0.001
ADAM_B1 = 0.9
ADAM_B2 = 0.999
ADAM_EPS = 1e-08
ADAM_WD = 0.01
ADAM_STEP = 10

NT = (((1,), (1,)), ((), ()))
TN = (((0,), (0,)), ((), ()))


def _params(*sem):
    return pltpu.CompilerParams(dimension_semantics=sem, vmem_limit_bytes=VMEM_LIMIT)


def _dot(a, b):
    return jnp.dot(a, b, preferred_element_type=F32)


def _dot_nt(a, b):
    return lax.dot_general(a, b, NT, preferred_element_type=F32)


def _t_bf16(v):
    return jnp.transpose(v).astype(BF16)


def _sigmoid(v):
    return 1.0 / (1.0 + jnp.exp(-v))


def _rms(x, g):
    r = lax.rsqrt(jnp.mean(x * x, axis=-1, keepdims=True) + EPS)
    n = x * r
    return n, r, n * g


def _rms_bwd(dh, n, r, g):
    dn = dh * g
    dx = r * (dn - n * jnp.mean(dn * n, axis=-1, keepdims=True))
    return dx, jnp.sum(dh * n, axis=0, keepdims=True)


def _ln_fwd(z, g, b):
    mu = jnp.mean(z, axis=-1, keepdims=True)
    zc = z - mu
    rs = lax.rsqrt(jnp.mean(zc * zc, axis=-1, keepdims=True) + EPS)
    zn = zc * rs
    return zn, rs, zn * g + b


def _ln_bwd(dl, zn, rs, g):
    dzn = dl * g
    dz = rs * (dzn - jnp.mean(dzn, axis=-1, keepdims=True) - zn * jnp.mean(dzn * zn, axis=-1, keepdims=True))
    return dz, jnp.sum(dl * zn, axis=0, keepdims=True), jnp.sum(dl, axis=0, keepdims=True)


def _tile(n, want):
    return want if n % want == 0 else n


def _resident(shape):
    return pl.BlockSpec(shape, lambda i: (0,) * len(shape), pipeline_mode=pl.Buffered(1))


def _ffn_fwd(x, g, w1t, w3t, w2, shards=()):
    s, d = x.shape
    f = w1t.shape[0]
    tm, tf = _tile(s, 512), MXU_TILE
    nj = f // tf
    ni = s // tm
    n_c = len(shards)

    def body(*refs):
        x_ref, g_ref, w1_ref, w3_ref, w2_ref = refs[:5]
        xo_ref, a_ref, b_ref = refs[5 + n_c:8 + n_c]
        u_s = refs[8 + 2 * n_c]
        if n_c:
            gather = lambda: _ici_gather_copies(refs[5:5 + n_c], refs[8 + n_c:8 + 2 * n_c], refs[9 + 2 * n_c], refs[10 + 2 * n_c])

            @pl.when(pl.program_id(0) == 0)
            def _():
                for cp in gather():
                    cp.start()

        h = _rms(x_ref[...], g_ref[...])[2].astype(BF16)
        for j in range(nj):
            cols = slice(j * tf, (j + 1) * tf)
            a = _dot_nt(h, w1_ref[cols, :])
            b = _dot_nt(h, w3_ref[cols, :])
            a_ref[:, cols] = a.astype(BF16)
            b_ref[:, cols] = b.astype(BF16)
            u_s[:, cols] = ((a * _sigmoid(a)) * b).astype(BF16)
        xo_ref[...] = x_ref[...] + FFN_RESIDUAL * _dot(u_s[...], w2_ref[...])
        if n_c:
            @pl.when(pl.program_id(0) == ni - 1)
            def _():
                _wait_all(gather())

    row = pl.BlockSpec((tm, d), lambda i: (i, 0))
    hid = pl.BlockSpec((tm, f), lambda i: (i, 0))
    sems = [pltpu.SemaphoreType.DMA((3 * n_c,))] * 2 if n_c else []
    out = pl.pallas_call(
        body, name="ffn_fwd_gather" if n_c else "ffn_fwd", grid=(ni,),
        in_specs=[row, _resident(g.shape), _resident(w1t.shape), _resident(w3t.shape), _resident(w2.shape)] + [ANY] * n_c,
        out_specs=[row, hid, hid] + [ANY] * n_c,
        out_shape=[jax.ShapeDtypeStruct((s, d), F32), jax.ShapeDtypeStruct((s, f), BF16), jax.ShapeDtypeStruct((s, f), BF16)]
        + _landing_shapes(shards),
        scratch_shapes=[pltpu.VMEM((tm, f), BF16)] + sems,
        compiler_params=_params("arbitrary"),
    )(x, g, w1t, w3t, w2, *shards)
    return out[0], out[1], out[2], list(out[3:])


def _ffn_bwd(dxo, x, g, a, b, w1t, w3t, w2, sums=None):
    s, d = x.shape
    f = w1t.shape[0]
    tm, tf = _tile(s, 256), MXU_TILE
    nj = f // tf
    ni = s // tm
    n_c = 0 if sums is None else 1

    def body(*refs):
        dxo_ref, x_ref, g_ref, a_ref, b_ref, w1_ref, w3_ref, w2_ref = refs[:8]
        dx_ref, dg_ref, h_ref, dy_ref, da_ref, db_ref, u_ref, dxb_ref = refs[8 + n_c:16 + n_c]
        if n_c:
            exchange = lambda: _ici_reduce_copies(refs[8], refs[16 + n_c], *refs[16 + 2 * n_c:])

        @pl.when(pl.program_id(0) == 0)
        def _():
            dg_ref[...] = jnp.zeros_like(dg_ref)
            if n_c:
                copies, local = exchange()
                local.start()
                for cp in copies:
                    cp.start()

        h_ref[...] = _rms(x_ref[...], g_ref[...])[2].astype(BF16)
        dy = (FFN_RESIDUAL * dxo_ref[...]).astype(BF16)
        dy_ref[...] = dy
        for j in range(nj):
            cols = slice(j * tf, (j + 1) * tf)
            du = _dot_nt(dy, w2_ref[cols, :])
            av = a_ref[:, cols].astype(F32)
            bv = b_ref[:, cols].astype(F32)
            sg = _sigmoid(av)
            sl = av * sg
            da = du * bv * (sg * (1.0 + av * (1.0 - sg)))
            db = du * sl
            da_ref[:, cols] = da.astype(BF16)
            db_ref[:, cols] = db.astype(BF16)
            u_ref[:, cols] = (sl * bv).astype(BF16)
        dh = _dot(da_ref[...], w1_ref[...]) + _dot(db_ref[...], w3_ref[...])
        n, r, _ = _rms(x_ref[...], g_ref[...])
        dxr, dg = _rms_bwd(dh, n, r, g_ref[...])
        dx = dxo_ref[...] + dxr
        dx_ref[...] = dx
        dxb_ref[...] = dx.astype(BF16)
        dg_ref[0:1, :] += dg
        if n_c:
            @pl.when(pl.program_id(0) == ni - 1)
            def _():
                copies, local = exchange()
                _wait_all(copies)
                local.wait()

    row = pl.BlockSpec((tm, d), lambda i: (i, 0))
    hid = pl.BlockSpec((tm, f), lambda i: (i, 0))
    extra = [] if sums is None else [sums]
    sems = [pltpu.SemaphoreType.DMA((3,)), pltpu.SemaphoreType.DMA((3,)), pltpu.SemaphoreType.DMA] if n_c else []
    out = pl.pallas_call(
        body, name="ffn_bwd_reduce" if n_c else "ffn_bwd", grid=(ni,),
        in_specs=[row, row, _resident(g.shape), hid, hid, _resident(w1t.shape), _resident(w3t.shape), _resident(w2.shape)]
        + [ANY] * n_c,
        out_specs=[row, pl.BlockSpec((8, d), lambda i: (0, 0)), row, row, hid, hid, hid, row] + [ANY] * n_c,
        out_shape=[jax.ShapeDtypeStruct((s, d), F32), jax.ShapeDtypeStruct((8, d), F32),
                   jax.ShapeDtypeStruct((s, d), BF16), jax.ShapeDtypeStruct((s, d), BF16),
                   jax.ShapeDtypeStruct((s, f), BF16), jax.ShapeDtypeStruct((s, f), BF16), jax.ShapeDtypeStruct((s, f), BF16),
                   jax.ShapeDtypeStruct((s, d), BF16)]
        + [jax.ShapeDtypeStruct(e.shape, e.dtype) for e in extra],
        scratch_shapes=sems,
        compiler_params=_params("arbitrary"),
    )(dxo, x, g, a, b, w1t, w3t, w2, *extra)
    return out


def _dw(am, bm):
    s, r = am.shape
    n = bm.shape[1]
    rb = r // 2 if r > 1024 and (r // 2) % LANES == 0 else r
    ts = _tile(s, 2048)
    ni = s // ts

    def body(am_ref, bm_ref, o_ref, acc_s):
        i = pl.program_id(1)

        @pl.when(i == 0)
        def _():
            acc_s[...] = jnp.zeros_like(acc_s)

        acc_s[...] += lax.dot_general(am_ref[...], bm_ref[...], TN, preferred_element_type=F32)

        @pl.when(i == ni - 1)
        def _():
            o_ref[...] = acc_s[...].astype(BF16)

    return pl.pallas_call(
        body, name="dw", grid=(r // rb, ni),
        in_specs=[pl.BlockSpec((ts, rb), lambda k, i: (i, k)), pl.BlockSpec((ts, n), lambda k, i: (i, 0))],
        out_specs=pl.BlockSpec((rb, n), lambda k, i: (k, 0)),
        out_shape=jax.ShapeDtypeStruct((r, n), BF16),
        scratch_shapes=[pltpu.VMEM((rb, n), F32)],
        compiler_params=_params("arbitrary", "arbitrary"),
    )(am, bm)


def _proj_fwd(x, g, w_int):
    s, d = x.shape
    f = w_int.shape[0]
    tm, tf = _tile(s, 1024), 512
    nj = f // tf

    def body(x_ref, g_ref, w_ref, p_ref, h_s):
        @pl.when(pl.program_id(1) == 0)
        def _():
            h_s[...] = _rms(x_ref[...], g_ref[...])[2].astype(BF16)

        p_ref[...] = _dot_nt(h_s[...], w_ref[...])

    return pl.pallas_call(
        body, name="proj_fwd", grid=(s // tm, nj),
        in_specs=[pl.BlockSpec((tm, d), lambda i, j: (i, 0)), pl.BlockSpec((1, d), lambda i, j: (0, 0)),
                  pl.BlockSpec((tf, d), lambda i, j: (j, 0))],
        out_specs=pl.BlockSpec((tm, tf), lambda i, j: (i, j)),
        out_shape=jax.ShapeDtypeStruct((s, f), F32),
        scratch_shapes=[pltpu.VMEM((tm, d), BF16)],
        compiler_params=_params("arbitrary", "arbitrary"),
    )(x, g, w_int)


def _proj_bwd(dxo, x, g, dp, w_int):
    s, d = x.shape
    f = w_int.shape[0]
    tm = _tile(s, 512)

    def body(dxo_ref, x_ref, g_ref, dp_ref, w_ref, dx_ref, dg_ref, h_ref):
        @pl.when(pl.program_id(0) == 0)
        def _():
            dg_ref[...] = jnp.zeros_like(dg_ref)

        n, r, h = _rms(x_ref[...], g_ref[...])
        h_ref[...] = h.astype(BF16)
        dxr, dg = _rms_bwd(_dot(dp_ref[...], w_ref[...]), n, r, g_ref[...])
        dx_ref[...] = dxo_ref[...] + dxr
        dg_ref[0:1, :] += dg

    row = pl.BlockSpec((tm, d), lambda i: (i, 0))
    return pl.pallas_call(
        body, name="proj_bwd", grid=(s // tm,),
        in_specs=[row, row, _resident(g.shape), pl.BlockSpec((tm, f), lambda i: (i, 0)), _resident(w_int.shape)],
        out_specs=[row, pl.BlockSpec((8, d), lambda i: (0, 0)), row],
        out_shape=[jax.ShapeDtypeStruct((s, d), F32), jax.ShapeDtypeStruct((8, d), F32), jax.ShapeDtypeStruct((s, d), BF16)],
        compiler_params=_params("arbitrary"),
    )(dxo, x, g, dp, w_int)


C = D_GROUP


def _piece(ref, k):
    return ref[:, k * C:(k + 1) * C]


def _up(v, r):
    return v if r == 0 else pltpu.roll(v, v.shape[0] - r, 0)


def _down(v, r):
    return v if r == 0 else pltpu.roll(v, r, 0)


def _lane_group():
    lane = lax.broadcasted_iota(jnp.int32, (1, C), 1)
    return (lane >= POOL_GROUP).astype(jnp.int32) + (lane >= 2 * POOL_GROUP).astype(jnp.int32) + (
        lane >= 3 * POOL_GROUP).astype(jnp.int32)


def _by_group(grp, v2, v4, v8, v16):
    return jnp.where(grp == 0, v2, jnp.where(grp == 1, v4, jnp.where(grp == 2, v8, v16)))


def _pool_count(grp, row0, t):
    pos = (row0 + lax.broadcasted_iota(jnp.int32, (t, C), 0) + 1).astype(F32)
    return jnp.minimum(pos, _by_group(grp, 2.0, 4.0, 8.0, 16.0))


def _trailing_sums(ext, grp, t):
    s2 = ext + _down(ext, 1)
    s4 = s2 + _down(s2, 2)
    s8 = s4 + _down(s4, 4)
    s16 = s8 + _down(s8, 8)
    return _by_group(grp, s2, s4, s8, s16)[HALO:HALO + t]


def _leading_sums(ext, grp, t):
    s2 = ext + _up(ext, 1)
    s4 = s2 + _up(s2, 2)
    s8 = s4 + _up(s4, 4)
    s16 = s8 + _up(s8, 8)
    return _by_group(grp, s2, s4, s8, s16)[0:t]


def _head_select(r4, grp):
    out = jnp.where(grp == 0, r4[0:CHUNK], 0.0)
    for h in range(1, N_HEADS):
        out = out + jnp.where(grp == h, r4[h * CHUNK:(h + 1) * CHUNK], 0.0)
    return out


def _conv_taps():
    return [(k, (k + 2) % 8, (k + 2) - (k + 2) % 8) for k in range(CONF_KERNEL)]


def _mixer_fwd(p, x1, cw, vec, pool_w, wstack, bias, w_out):
    s, d = x1.shape
    t = _tile(s, T_MIX_FWD)
    n_ext = t + HALO
    dm = w_out.shape[0]

    def body(p_ref, x1_ref, cw_ref, vec_ref, pw_ref, ws_ref, bias_ref, wo_ref, x2_ref, mix_s, z_ref, cy_s, cq_s, cx_s):
        i = pl.program_id(0)

        @pl.when(i == 0)
        def _():
            cy_s[...] = jnp.zeros_like(cy_s)
            cq_s[...] = jnp.zeros_like(cq_s)
            cx_s[...] = jnp.zeros_like(cx_s)

        grp = _lane_group()
        y = _piece(p_ref, 0) * _sigmoid(_piece(p_ref, 1))
        ext = jnp.concatenate([cy_s[...], y], axis=0)
        cy_s[...] = y[t - HALO:t]
        z = jnp.broadcast_to(vec_ref[0:1, :], (t, C))
        shifted = {}
        for k, r, off in _conv_taps():
            if r not in shifted:
                shifted[r] = _up(ext, r)
            z = z + cw_ref[k:k + 1, :] * shifted[r][off:off + t]
        z_ref[...] = z
        ln = _ln_fwd(z, vec_ref[1:2, :], vec_ref[2:3, :])[2]
        mix_s[:, 0:C] = (ln * _sigmoid(ln)).astype(BF16)
        q = _piece(p_ref, 3) * _piece(p_ref, 4)
        ext = jnp.concatenate([cq_s[...], q], axis=0)
        cq_s[...] = q[t - HALO:t]
        cz = vec_ref[8:9, :] * q + vec_ref[7:8, :] * _down(ext, 1)[HALO:] + vec_ref[6:7, :] * _down(ext, 2)[HALO:]
        mix_s[:, C:2 * C] = (_piece(p_ref, 2) * cz).astype(BF16)
        xp = _piece(p_ref, 5)
        ext = jnp.concatenate([cx_s[...], xp], axis=0)
        cx_s[...] = xp[t - HALO:t]
        dd = _trailing_sums(ext, grp, t) / _pool_count(grp, i * t, t) - xp
        mix_s[:, 2 * C:3 * C] = (_dot(dd.astype(BF16), pw_ref[...]) * vec_ref[3:4, :]).astype(BF16)
        vln = _ln_fwd(_piece(p_ref, 7), vec_ref[4:5, :], vec_ref[5:6, :])[2].astype(BF16)
        for n in range(t // CHUNK):
            rows = slice(n * CHUNK, (n + 1) * CHUNK)
            mixed = _head_select(_dot(ws_ref[...], vln[rows]), grp) + bias_ref[...]
            mix_s[rows, 3 * C:4 * C] = (p_ref[rows, 6 * C:7 * C] * mixed).astype(BF16)
        x2_ref[...] = x1_ref[...] + _dot(mix_s[...], wo_ref[...])

    full = lambda a: pl.BlockSpec(a.shape, lambda i: (0, 0))
    return pl.pallas_call(
        body, name="mixer_fwd", grid=(s // t,),
        in_specs=[pl.BlockSpec((t, p.shape[1]), lambda i: (i, 0)), pl.BlockSpec((t, d), lambda i: (i, 0)),
                  full(cw), full(vec), full(pool_w), full(wstack), full(bias), full(w_out)],
        out_specs=[pl.BlockSpec((t, d), lambda i: (i, 0)), pl.BlockSpec((t, dm), lambda i: (i, 0)),
                   pl.BlockSpec((t, C), lambda i: (i, 0))],
        out_shape=[jax.ShapeDtypeStruct((s, d), F32), jax.ShapeDtypeStruct((s, dm), BF16), jax.ShapeDtypeStruct((s, C), F32)],
        scratch_shapes=[pltpu.VMEM((HALO, C), F32)] * 3,
        compiler_params=_params("arbitrary"),
    )(p, x1, cw, vec, pool_w, wstack, bias, w_out)


def _mixer_bwd(dx2, p, z, cw, vec, pool_w, wstack, wstack_t, bias, tril4, head_rows, w_out):
    s, d = dx2.shape
    t = _tile(s, T_MIX_BWD)
    nt = s // t
    n_ext = t + HALO
    hb = t // HALO

    def body(dx2_ref, p_ref, ph_ref, z_ref, cw_ref, vec_ref, pw_ref, ws_ref, wst_ref, bias_ref, tril_ref, hr_ref, wo_ref,
             dp_ref, dcw_ref, dvec_ref, dpool_ref, dws_ref, dbs_ref, cdz_s, cdc_s, cf_s, vy_s, dvl_s, dbias_s):
        i = pl.program_id(0)
        tile = nt - 1 - i

        @pl.when(i == 0)
        def _():
            for ref in (cdz_s, cdc_s, cf_s, dbias_s, dcw_ref, dvec_ref, dpool_ref, dws_ref, dbs_ref):
                ref[...] = jnp.zeros_like(ref)

        grp = _lane_group()
        first = jnp.where(tile > 0, 1.0, 0.0)
        dmix = _dot_nt(dx2_ref[...].astype(BF16), wo_ref[...])
        d_a, d_b, d_c, d_d = (dmix[:, k * C:(k + 1) * C] for k in range(4))

        def acc_vec(row, v):
            dvec_ref[row:row + 1, :] += jnp.sum(v, axis=0, keepdims=True)

        val, gate = _piece(p_ref, 0), _piece(p_ref, 1)
        sgate = _sigmoid(gate)
        y = val * sgate
        y_halo = ph_ref[:, 0:C] * _sigmoid(ph_ref[:, C:2 * C]) * first
        ext = jnp.concatenate([y_halo, y], axis=0)
        for r in range(8):
            vy_s[r] = _up(ext, r)
        zn, rs, ln = _ln_fwd(z_ref[...], vec_ref[1:2, :], vec_ref[2:3, :])
        sg = _sigmoid(ln)
        dln = d_a * (sg * (1.0 + ln * (1.0 - sg)))
        dz, dg, db = _ln_bwd(dln, zn, rs, vec_ref[1:2, :])
        dvec_ref[1:2, :] += dg
        dvec_ref[2:3, :] += db
        acc_vec(0, dz)
        for k, r, off in _conv_taps():
            dcw_ref[k:k + 1, :] += jnp.sum(dz * vy_s[r, off:off + t, :], axis=0, keepdims=True)
        ext = jnp.concatenate([dz, cdz_s[...]], axis=0)
        cdz_s[...] = dz[0:HALO]
        dy = jnp.zeros((t, C), F32)
        shifted = {}
        for k in range(CONF_KERNEL):
            m = CONF_KERNEL - 1 - k
            r, off = m % 8, m - m % 8
            if r not in shifted:
                shifted[r] = _up(ext, r)
            dy = dy + cw_ref[k:k + 1, :] * shifted[r][off:off + t]
        dp_ref[:, 0:C] = (dy * sgate).astype(BF16)
        dp_ref[:, C:2 * C] = (dy * val * sgate * (1.0 - sgate)).astype(BF16)

        sb, sc, sx = _piece(p_ref, 2), _piece(p_ref, 3), _piece(p_ref, 4)
        q = sc * sx
        q_halo = ph_ref[:, 3 * C:4 * C] * ph_ref[:, 4 * C:5 * C] * first
        ext = jnp.concatenate([q_halo, q], axis=0)
        q1, q2 = _down(ext, 1)[HALO:], _down(ext, 2)[HALO:]
        cz = vec_ref[8:9, :] * q + vec_ref[7:8, :] * q1 + vec_ref[6:7, :] * q2
        dcz = d_b * sb
        dp_ref[:, 2 * C:3 * C] = (d_b * cz).astype(BF16)
        acc_vec(8, dcz * q)
        acc_vec(7, dcz * q1)
        acc_vec(6, dcz * q2)
        ext = jnp.concatenate([dcz, cdc_s[...]], axis=0)
        cdc_s[...] = dcz[0:HALO]
        dq = vec_ref[8:9, :] * dcz + vec_ref[7:8, :] * _up(ext, 1)[0:t] + vec_ref[6:7, :] * _up(ext, 2)[0:t]
        dp_ref[:, 3 * C:4 * C] = (dq * sx).astype(BF16)
        dp_ref[:, 4 * C:5 * C] = (dq * sc).astype(BF16)

        xp = _piece(p_ref, 5)
        ext = jnp.concatenate([ph_ref[:, 5 * C:6 * C] * first, xp], axis=0)
        cnt = _pool_count(grp, tile * t, t)
        dd = (_trailing_sums(ext, grp, t) / cnt - xp).astype(BF16)
        e2 = _dot(dd, pw_ref[...])
        acc_vec(3, d_c * e2)
        de = (d_c * vec_ref[3:4, :]).astype(BF16)
        dpool_ref[...] += _dot(_t_bf16(dd.astype(F32)), de)
        ddd = _dot_nt(de, pw_ref[...])
        fq = ddd / cnt
        ext = jnp.concatenate([fq, cf_s[...]], axis=0)
        cf_s[...] = fq[0:HALO]
        dp_ref[:, 5 * C:6 * C] = (_leading_sums(ext, grp, t) - ddd).astype(BF16)

        vn, vrs, vlnf = _ln_fwd(_piece(p_ref, 7), vec_ref[4:5, :], vec_ref[5:6, :])
        vln = vlnf.astype(BF16)
        for n in range(t // CHUNK):
            rows = slice(n * CHUNK, (n + 1) * CHUNK)
            mixed = _head_select(_dot(ws_ref[...], vln[rows]), grp) + bias_ref[...]
            dd_n = d_d[rows]
            dp_ref[rows, 6 * C:7 * C] = (dd_n * mixed).astype(BF16)
            dmx = dd_n * p_ref[rows, 6 * C:7 * C]
            dbias_s[...] += dmx
            dmx_b = dmx.astype(BF16)
            dvl_s[rows, :] = _head_select(_dot(wst_ref[...], dmx_b), grp)
            for h in range(N_HEADS):
                hrows = slice(h * CHUNK, (h + 1) * CHUNK)
                dws_ref[hrows, :] += _dot_nt(jnp.where(grp == h, dmx_b, jnp.zeros_like(dmx_b)), vln[rows])
        dvl = dvl_s[...]
        dv, dg, db = _ln_bwd(dvl, vn, vrs, vec_ref[4:5, :])
        dvec_ref[4:5, :] += dg
        dvec_ref[5:6, :] += db
        dp_ref[:, 7 * C:8 * C] = dv.astype(BF16)

        @pl.when(i == nt - 1)
        def _():
            dws_ref[...] = dws_ref[...] * tril_ref[...]
            dbs_ref[...] = lax.dot_general(hr_ref[...], dbias_s[...], NT, precision=lax.Precision.HIGHEST,
                                           preferred_element_type=F32)

    full = lambda a: pl.BlockSpec(a.shape, lambda i: (0, 0))
    acc = lambda shape: pl.BlockSpec(shape, lambda i: (0, 0))
    f = p.shape[1]
    return pl.pallas_call(
        body, name="mixer_bwd", grid=(nt,),
        in_specs=[pl.BlockSpec((t, d), lambda i: (nt - 1 - i, 0)), pl.BlockSpec((t, f), lambda i: (nt - 1 - i, 0)),
                  pl.BlockSpec((HALO, f), lambda i: (jnp.maximum((nt - 1 - i) * hb - 1, 0), 0)),
                  pl.BlockSpec((t, C), lambda i: (nt - 1 - i, 0)), full(cw), full(vec), full(pool_w), full(wstack), full(wstack_t), full(bias), full(tril4), full(head_rows),
                  full(w_out)],
        out_specs=[pl.BlockSpec((t, f), lambda i: (nt - 1 - i, 0)), acc((32, C)), acc((16, C)), acc((C, C)),
                   acc((N_HEADS * CHUNK, CHUNK)), acc((8, CHUNK))],
        out_shape=[jax.ShapeDtypeStruct((s, f), BF16), jax.ShapeDtypeStruct((32, C), F32), jax.ShapeDtypeStruct((16, C), F32),
                   jax.ShapeDtypeStruct((C, C), F32), jax.ShapeDtypeStruct((N_HEADS * CHUNK, CHUNK), F32),
                   jax.ShapeDtypeStruct((8, CHUNK), F32)],
        scratch_shapes=[pltpu.VMEM((HALO, C), F32)] * 3 + [pltpu.VMEM((8, n_ext, C), F32), pltpu.VMEM((t, C), F32),
                                                            pltpu.VMEM((CHUNK, C), F32)],
        compiler_params=_params("arbitrary"),
    )(dx2, p, p, z, cw, vec, pool_w, wstack, wstack_t, bias, tril4, head_rows, w_out)


def _loss_bwd(x, g, target):
    s, d = x.shape
    tm = _tile(s, 512)

    def body(x_ref, g_ref, t_ref, dx_ref, dg_ref, loss_ref):
        @pl.when(pl.program_id(0) == 0)
        def _():
            dg_ref[...] = jnp.zeros_like(dg_ref)
            loss_ref[...] = jnp.zeros_like(loss_ref)

        n, r, y = _rms(x_ref[...], g_ref[...])
        err = y - t_ref[...]
        loss_ref[...] += 0.5 * jnp.sum(jnp.mean(err * err, axis=-1, keepdims=True), axis=0, keepdims=True)
        dxr, dg = _rms_bwd(err * (1.0 / d), n, r, g_ref[...])
        dx_ref[...] = dxr
        dg_ref[0:1, :] += dg

    row = pl.BlockSpec((tm, d), lambda i: (i, 0))
    return pl.pallas_call(
        body, name="loss_bwd", grid=(s // tm,),
        in_specs=[row, pl.BlockSpec((1, d), lambda i: (0, 0)), row],
        out_specs=[row, pl.BlockSpec((8, d), lambda i: (0, 0)), pl.BlockSpec((8, LANES), lambda i: (0, 0))],
        out_shape=[jax.ShapeDtypeStruct((s, d), F32), jax.ShapeDtypeStruct((8, d), F32), jax.ShapeDtypeStruct((8, LANES), F32)],
        compiler_params=_params("arbitrary"),
    )(x, g, target)


def _adamw_math(w, g, m, v):
    m = ADAM_B1 * m + (1.0 - ADAM_B1) * g
    v = ADAM_B2 * v + (1.0 - ADAM_B2) * (g * g)
    m_hat = m / (1.0 - ADAM_B1 ** ADAM_STEP)
    v_hat = v / (1.0 - ADAM_B2 ** ADAM_STEP)
    return -ADAM_LR * (m_hat / (jnp.sqrt(v_hat) + ADAM_EPS) + ADAM_WD * w), m, v


def _adamw(w, g, m, v):
    r, c = w.shape
    tr = r // 8 if r % 64 == 0 else r

    def body(w_ref, g_ref, m_ref, v_ref, d_ref, mo_ref, vo_ref):
        d_ref[...], mo_ref[...], vo_ref[...] = _adamw_math(w_ref[...], g_ref[...], m_ref[...], v_ref[...])

    blk = pl.BlockSpec((tr, c), lambda i: (i, 0))
    return pl.pallas_call(
        body, name="adamw", grid=(r // tr,), in_specs=[blk] * 4, out_specs=[blk] * 3,
        out_shape=[jax.ShapeDtypeStruct((r, c), F32)] * 3, compiler_params=_params("arbitrary"),
    )(w, g, m, v)


def _adamw_small(ws, gs, ms, vs):
    n = len(ws)

    def body(*refs):
        ins, outs = refs[:4 * n], refs[4 * n:]
        for k in range(n):
            dl, mo, vo = _adamw_math(ins[k][...], ins[n + k][...], ins[2 * n + k][...], ins[3 * n + k][...])
            outs[k][...], outs[n + k][...], outs[2 * n + k][...] = dl, mo, vo

    vm = pl.BlockSpec(memory_space=pltpu.VMEM)
    out = pl.pallas_call(
        body, name="adamw_small", in_specs=[vm] * (4 * n), out_specs=[vm] * (3 * n),
        out_shape=[jax.ShapeDtypeStruct(a.shape, F32) for a in ws] * 3,
        compiler_params=pltpu.CompilerParams(vmem_limit_bytes=VMEM_LIMIT),
    )(*ws, *gs, *ms, *vs)
    return out[:n], out[n:2 * n], out[2 * n:]


def _where_am_i():
    x, y, c = lax.axis_index("x"), lax.axis_index("y"), lax.axis_index("c")
    chips = [(1 - x, y), (x, 1 - y), (1 - x, 1 - y)]
    return x, y, c, chips


def _chip_id(chip):
    return 2 * chip[0] + chip[1]


def _landing_shapes(shards):
    return [jax.ShapeDtypeStruct((3, a.shape[0] // 2, a.shape[1]), a.dtype) for a in shards]


def _ici_gather_copies(ins, lands, send_sems, recv_sems):
    _, _, c, chips = _where_am_i()
    copies = []
    for a, src in enumerate(ins):
        hr = src.shape[0] // 2
        for j, chip in enumerate(chips):
            copies.append(pltpu.make_async_remote_copy(
                src_ref=src.at[pl.ds(c * hr, hr)], dst_ref=lands[a].at[j], send_sem=send_sems.at[3 * a + j],
                recv_sem=recv_sems.at[3 * a + j], device_id=(*chip, c), device_id_type=MESH))
    return copies


def _ici_reduce_copies(s_ref, o_ref, send_sems, recv_sems, local_sem):
    x, y, c, chips = _where_am_i()
    copies = [pltpu.make_async_remote_copy(src_ref=s_ref.at[_chip_id(chip)], dst_ref=o_ref.at[j], send_sem=send_sems.at[j],
                                           recv_sem=recv_sems.at[j], device_id=(*chip, c), device_id_type=MESH)
              for j, chip in enumerate(chips)]
    return copies, pltpu.make_async_copy(s_ref.at[_chip_id((x, y))], o_ref.at[3], local_sem)


def _wait_all(copies):
    for cp in copies:
        cp.wait_recv()
    for cp in copies:
        cp.wait_send()


def _ici_gather(shards):
    n = len(shards)

    def body(*refs):
        copies = _ici_gather_copies(refs[:n], refs[n:2 * n], refs[2 * n], refs[2 * n + 1])
        for cp in copies:
            cp.start()
        _wait_all(copies)

    return pl.pallas_call(
        body, name="ici_gather", in_specs=[ANY] * n, out_specs=[ANY] * n, out_shape=_landing_shapes(shards),
        scratch_shapes=[pltpu.SemaphoreType.DMA((3 * n,))] * 2,
    )(*shards)


SLOTS = 3


def _sibling_gather(shards, lands):
    n = len(shards)
    d = shards[0].shape[1]
    halves = [a.shape[0] // 2 for a in shards]
    hmax = max(halves)
    chunks = [(a, j, halves[a]) for a in range(n) for j in range(3)]
    nc = len(chunks)

    def body(*refs):
        ins, lnd, outs = refs[:n], refs[n:2 * n], refs[2 * n:3 * n]
        sbuf, rbuf, obuf, ld_sems, take_sems, send_sems, recv_sems, place_sems, own_ld_sems, own_st_sems, credits = refs[3 * n:]
        x, y, c, chips = _where_am_i()
        sibling = (x, y, 1 - c)
        me = _chip_id((x, y))

        def load(i):
            a, j, hr = chunks[i]
            return pltpu.make_async_copy(lnd[a].at[j], sbuf.at[i % SLOTS, pl.ds(0, hr)], ld_sems.at[i % SLOTS])

        def push(i):
            hr, slot = chunks[i][2], i % SLOTS
            return pltpu.make_async_remote_copy(src_ref=sbuf.at[slot, pl.ds(0, hr)], dst_ref=rbuf.at[slot, pl.ds(0, hr)],
                                                send_sem=send_sems.at[slot], recv_sem=recv_sems.at[slot], device_id=sibling,
                                                device_id_type=MESH)

        def take(i):
            a, j, hr = chunks[i]
            return pltpu.make_async_copy(rbuf.at[i % SLOTS, pl.ds(0, hr)],
                                         outs[a].at[_chip_id(chips[j]), pl.ds((1 - c) * hr, hr)], take_sems.at[i % SLOTS])

        def place(i):
            a, j, hr = chunks[i]
            return pltpu.make_async_copy(sbuf.at[i % SLOTS, pl.ds(0, hr)], outs[a].at[_chip_id(chips[j]), pl.ds(c * hr, hr)],
                                         place_sems.at[i % SLOTS])

        own = [(a, h, halves[a]) for a in range(n) for h in range(2)]

        def own_load(k):
            a, h, hr = own[k]
            return pltpu.make_async_copy(ins[a].at[pl.ds(h * hr, hr)], obuf.at[k % 2, pl.ds(0, hr)], own_ld_sems.at[k % 2])

        def own_store(k):
            a, h, hr = own[k]
            return pltpu.make_async_copy(obuf.at[k % 2, pl.ds(0, hr)], outs[a].at[me, pl.ds(h * hr, hr)], own_st_sems.at[k % 2])

        def own_step(k):
            if k < len(own):
                if k >= 2:
                    own_store(k - 2).wait()
                own_load(k).start()
            if 1 <= k <= len(own):
                own_load(k - 1).wait()
                own_store(k - 1).start()

        for i in range(min(2, nc)):
            load(i).start()
        for i in range(nc):
            own_step(i)
            if i >= 1:
                push(i - 1).wait_send()
                place(i - 1).wait()
            if i + 2 < nc:
                load(i + 2).start()
            load(i).wait()
            place(i).start()
            if i >= SLOTS:
                pl.semaphore_wait(credits.at[i % SLOTS], 1)
            push(i).start()
            if i >= 1:
                push(i - 1).wait_recv()
                take(i - 1).start()
            if i >= 2:
                take(i - 2).wait()
                if i - 2 + SLOTS < nc:
                    pl.semaphore_signal(credits.at[(i - 2) % SLOTS], inc=1, device_id=sibling, device_id_type=MESH)
        push(nc - 1).wait_send()
        place(nc - 1).wait()
        push(nc - 1).wait_recv()
        take(nc - 1).start()
        for i in range(max(0, nc - 2), nc):
            take(i).wait()
        for k in range(nc, len(own) + 1):
            own_step(k)
        for k in range(max(0, len(own) - 2), len(own)):
            own_store(k).wait()

    dma = pltpu.SemaphoreType.DMA((SLOTS,))
    dma2 = pltpu.SemaphoreType.DMA((2,))
    return pl.pallas_call(
        body, name="sibling_gather", in_specs=[ANY] * (2 * n), out_specs=[ANY] * n,
        out_shape=[jax.ShapeDtypeStruct((N_CHIPS,) + a.shape, a.dtype) for a in shards],
        scratch_shapes=[pltpu.VMEM((SLOTS, hmax, d), BF16), pltpu.VMEM((SLOTS, hmax, d), BF16), pltpu.VMEM((2, hmax, d), BF16),
                        dma, dma, dma, dma, dma, dma2, dma2, pltpu.SemaphoreType.REGULAR((SLOTS,))],
        compiler_params=pltpu.CompilerParams(vmem_limit_bytes=VMEM_LIMIT),
    )(*shards, *lands)


def _pair_sum(grads):
    n = len(grads)
    halves = [g.shape[1] // 2 for g in grads]
    total, rows = sum(halves), max(halves)
    d = grads[0].shape[2]
    chunks, off = [], 0
    for a in range(n):
        chunks += [(a, k, off, halves[a]) for k in range(N_CHIPS)]
        off += halves[a]
    nc = len(chunks)

    def body(*refs):
        ins, out_ref = refs[:n], refs[n]
        sbuf, rbuf, mbuf, obuf, ls_sems, lm_sems, st_sems, send_sems, recv_sems, credits = refs[n + 1:]
        x, y, c, _ = _where_am_i()
        sibling = (x, y, 1 - c)

        def load_theirs(i):
            a, k, _, hr = chunks[i]
            return pltpu.make_async_copy(ins[a].at[k, pl.ds((1 - c) * hr, hr)], sbuf.at[i % SLOTS, pl.ds(0, hr)], ls_sems.at[i % SLOTS])

        def load_mine(i):
            a, k, _, hr = chunks[i]
            return pltpu.make_async_copy(ins[a].at[k, pl.ds(c * hr, hr)], mbuf.at[i % SLOTS, pl.ds(0, hr)], lm_sems.at[i % SLOTS])

        def push(i):
            hr, slot = chunks[i][3], i % SLOTS
            return pltpu.make_async_remote_copy(src_ref=sbuf.at[slot, pl.ds(0, hr)], dst_ref=rbuf.at[slot, pl.ds(0, hr)],
                                                send_sem=send_sems.at[slot], recv_sem=recv_sems.at[slot], device_id=sibling,
                                                device_id_type=MESH)

        def store(i):
            _, k, o, hr = chunks[i]
            slot = i % SLOTS
            return pltpu.make_async_copy(obuf.at[slot, pl.ds(0, hr)], out_ref.at[k, pl.ds(o, hr)], st_sems.at[slot])

        def start_push(i):
            load_theirs(i).wait()
            if i >= SLOTS:
                pl.semaphore_wait(credits.at[i % SLOTS], 1)
            push(i).start()

        for i in range(min(2, nc)):
            load_theirs(i).start()
            load_mine(i).start()
        start_push(0)
        for i in range(nc):
            hr, slot = chunks[i][3], i % SLOTS
            if i + 2 < nc:
                load_theirs(i + 2).start()
                load_mine(i + 2).start()
            if i + 1 < nc:
                start_push(i + 1)
            push(i).wait_recv()
            push(i).wait_send()
            load_mine(i).wait()
            if i >= SLOTS:
                store(i - SLOTS).wait()
            obuf[slot, 0:hr, :] = (mbuf[slot, 0:hr, :].astype(F32) + rbuf[slot, 0:hr, :].astype(F32)).astype(BF16)
            if i + SLOTS < nc:
                pl.semaphore_signal(credits.at[slot], inc=1, device_id=sibling, device_id_type=MESH)
            store(i).start()
        for i in range(max(0, nc - SLOTS), nc):
            store(i).wait()

    stage = pltpu.VMEM((SLOTS, rows, d), BF16)
    dma = pltpu.SemaphoreType.DMA((SLOTS,))
    return pl.pallas_call(
        body, name="pair_sum", in_specs=[ANY] * n, out_specs=ANY, out_shape=jax.ShapeDtypeStruct((N_CHIPS, total, d), BF16),
        scratch_shapes=[stage, stage, stage, stage, dma, dma, dma, dma, dma, pltpu.SemaphoreType.REGULAR((SLOTS,))],
        compiler_params=pltpu.CompilerParams(vmem_limit_bytes=VMEM_LIMIT),
    )(*grads)


def _reduce_chips(sums):
    def body(s_ref, o_ref, send_sems, recv_sems, local_sem):
        copies, local = _ici_reduce_copies(s_ref, o_ref, send_sems, recv_sems, local_sem)
        local.start()
        for cp in copies:
            cp.start()
        _wait_all(copies)
        local.wait()

    return pl.pallas_call(
        body, name="reduce_chips", in_specs=[ANY], out_specs=ANY, out_shape=jax.ShapeDtypeStruct(sums.shape, BF16),
        scratch_shapes=[pltpu.SemaphoreType.DMA((3,)), pltpu.SemaphoreType.DMA((3,)), pltpu.SemaphoreType.DMA],
    )(sums)


def _sum_share(parts, groups, rows):
    layers, n, n_g = len(parts), len(rows), len(groups)
    parts = [part for layer in parts for part in layer]
    n_l = len(parts)
    d = parts[0].shape[2]
    halves = [r // 2 for r in rows]
    hmax = max(halves)
    chunks = []
    for l in range(layers):
        for g, members in enumerate(groups):
            off = 0
            for a in members:
                chunks.append((l * n_g + g, a, off, halves[a], l))
                off += halves[a]
    nc = len(chunks)

    def body(*refs):
        ins, outs = refs[:n_l], refs[n_l:n_l + n]
        pbuf, obuf, rbuf, ld_sems, keep_sems, take_sems, send_sems, recv_sems, credits = refs[n_l + n:]
        x, y, c, _ = _where_am_i()
        sibling = (x, y, 1 - c)

        def load(i):
            part, _, off, hr, _ = chunks[i]
            return pltpu.make_async_copy(ins[part].at[:, pl.ds(off, hr)], pbuf.at[i % SLOTS, :, pl.ds(0, hr)], ld_sems.at[i % SLOTS])

        def keep(i):
            _, a, _, hr, l = chunks[i]
            return pltpu.make_async_copy(obuf.at[i % SLOTS, pl.ds(0, hr)], outs[a].at[l, pl.ds(c * hr, hr)], keep_sems.at[i % SLOTS])

        def push(i):
            hr, slot = chunks[i][3], i % SLOTS
            return pltpu.make_async_remote_copy(src_ref=obuf.at[slot, pl.ds(0, hr)], dst_ref=rbuf.at[slot, pl.ds(0, hr)],
                                                send_sem=send_sems.at[slot], recv_sem=recv_sems.at[slot], device_id=sibling,
                                                device_id_type=MESH)

        def take(i):
            _, a, _, hr, l = chunks[i]
            return pltpu.make_async_copy(rbuf.at[i % SLOTS, pl.ds(0, hr)], outs[a].at[l, pl.ds((1 - c) * hr, hr)],
                                         take_sems.at[i % SLOTS])

        for i in range(min(2, nc)):
            load(i).start()
        for i in range(nc):
            hr, slot = chunks[i][3], i % SLOTS
            if i + 2 < nc:
                load(i + 2).start()
            load(i).wait()
            if i >= SLOTS:
                keep(i - SLOTS).wait()
                push(i - SLOTS).wait_send()
            part = lambda k: pbuf[slot, k, 0:hr, :].astype(F32)
            obuf[slot, 0:hr, :] = ((part(3) + part(0)) + part(1)) + part(2)
            keep(i).start()
            if i >= SLOTS:
                pl.semaphore_wait(credits.at[slot], 1)
            push(i).start()
            if i >= 1:
                push(i - 1).wait_recv()
                take(i - 1).start()
            if i >= 2:
                take(i - 2).wait()
                if i - 2 + SLOTS < nc:
                    pl.semaphore_signal(credits.at[(i - 2) % SLOTS], inc=1, device_id=sibling, device_id_type=MESH)
        push(nc - 1).wait_recv()
        take(nc - 1).start()
        for i in range(max(0, nc - 2), nc):
            take(i).wait()
        for i in range(max(0, nc - SLOTS), nc):
            keep(i).wait()
            push(i).wait_send()

    dma = pltpu.SemaphoreType.DMA((SLOTS,))
    return pl.pallas_call(
        body, name="sum_share", in_specs=[ANY] * n_l, out_specs=[ANY] * n,
        out_shape=[jax.ShapeDtypeStruct((layers, r, d), F32) for r in rows],
        scratch_shapes=[pltpu.VMEM((SLOTS, N_CHIPS, hmax, d), BF16), pltpu.VMEM((SLOTS, hmax, d), F32),
                        pltpu.VMEM((SLOTS, hmax, d), F32), dma, dma, dma, dma, dma, pltpu.SemaphoreType.REGULAR((SLOTS,))],
        compiler_params=pltpu.CompilerParams(vmem_limit_bytes=VMEM_LIMIT),
    )(*parts)


def _all_gather_small(block, reduce):
    m, n = block.shape

    def body(x_ref, out_ref, *scratch):
        if reduce:
            all_ref, send_sems, recv_sems, local_sem = scratch
        else:
            all_ref = out_ref
            send_sems, recv_sems, local_sem = scratch
        x, y, c, chips = _where_am_i()
        me, sibling = (x, y, c), (x, y, 1 - c)

        def rows(px, py, pc):
            return all_ref.at[pl.ds((4 * px + 2 * py + pc) * m, m), :]

        def copy(k, blk, to, src=None):
            return pltpu.make_async_remote_copy(src_ref=rows(*blk) if src is None else src, dst_ref=rows(*blk),
                                                send_sem=send_sems.at[k], recv_sem=recv_sems.at[k], device_id=to,
                                                device_id_type=MESH)

        mine = pltpu.make_async_copy(x_ref, rows(*me), local_sem)
        mine.start()
        first = [copy(0, me, sibling, src=x_ref)]
        first += [copy(1 + j, me, (*chip, c), src=x_ref) for j, chip in enumerate(chips)]
        for cp in first:
            cp.start()
        passed = [copy(4 + j, (*chip, c), sibling) for j, chip in enumerate(chips)]
        for j, chip in enumerate(chips):
            copy(1 + j, (*chip, c), me).wait_recv()
            passed[j].start()
        copy(0, sibling, me).wait_recv()
        for j, chip in enumerate(chips):
            copy(4 + j, (*chip, 1 - c), me).wait_recv()
        for cp in first + passed:
            cp.wait_send()
        mine.wait()
        if reduce:
            total = all_ref[0:m, :]
            for dev in range(1, N_DEV):
                total = total + all_ref[dev * m:(dev + 1) * m, :]
            out_ref[...] = total

    vm = pl.BlockSpec(memory_space=pltpu.VMEM)
    sems = [pltpu.SemaphoreType.DMA((7,)), pltpu.SemaphoreType.DMA((7,)), pltpu.SemaphoreType.DMA]
    return pl.pallas_call(
        body, name="reduce_small" if reduce else "gather_small", in_specs=[vm], out_specs=vm,
        out_shape=jax.ShapeDtypeStruct((m, n) if reduce else (N_DEV * m, n), F32),
        scratch_shapes=([pltpu.VMEM((N_DEV * m, n), F32)] if reduce else []) + sems,
        compiler_params=pltpu.CompilerParams(vmem_limit_bytes=VMEM_LIMIT),
    )(block)


def _pack(arrays):
    flat = jnp.concatenate([a.reshape(-1) for a in arrays])
    pad = (-flat.shape[0]) % (8 * LANES)
    return jnp.pad(flat, (0, pad)).reshape(-1, LANES)


def _unpack(buf, shapes):
    flat = buf.reshape(-1)
    out, off = [], 0
    for shp in shapes:
        size = 1
        for dim in shp:
            size *= dim
        out.append(flat[off:off + size].reshape(shp))
        off += size
    return out


BIG = ("ffn1_w1", "ffn1_w3", "ffn1_w2", "w_in", "w_out", "ffn2_w1", "ffn2_w3", "ffn2_w2")
TRANSPOSED = ("ffn1_w1", "ffn1_w3", "w_in", "ffn2_w1", "ffn2_w3")
SMALL = ("ffn1_norm", "mix_norm", "conf_conv_w", "conf_conv_b", "conf_ln_g", "conf_ln_b", "sconv_w", "pool_w", "pool_scale",
         "gmlp_ln_g", "gmlp_ln_b", "gmlp_w_s", "gmlp_b_s", "ffn2_norm", "final_norm")
ORDER = ("ffn1_norm", "ffn1_w1", "ffn1_w3", "ffn1_w2", "mix_norm", "w_in", "conf_conv_w", "conf_conv_b", "conf_ln_g", "conf_ln_b",
         "sconv_w", "pool_w", "pool_scale", "gmlp_ln_g", "gmlp_ln_b", "gmlp_w_s", "gmlp_b_s", "w_out", "ffn2_norm", "ffn2_w1",
         "ffn2_w3", "ffn2_w2", "final_norm")


def _as2d(a):
    return a.reshape(-1, a.shape[-1])


def kernel(x, ffn1_norm, ffn1_w1, ffn1_w3, ffn1_w2, mix_norm, w_in, conf_conv_w, conf_conv_b, conf_ln_g, conf_ln_b, sconv_w, pool_w, pool_scale, gmlp_ln_g, gmlp_ln_b, gmlp_w_s, gmlp_b_s, w_out, ffn2_norm, ffn2_w1, ffn2_w3, ffn2_w2, final_norm, loss_target, m_ffn1_norm, m_ffn1_w1, m_ffn1_w3, m_ffn1_w2, m_mix_norm, m_w_in, m_conf_conv_w, m_conf_conv_b, m_conf_ln_g, m_conf_ln_b, m_sconv_w, m_pool_w, m_pool_scale, m_gmlp_ln_g, m_gmlp_ln_b, m_gmlp_w_s, m_gmlp_b_s, m_w_out, m_ffn2_norm, m_ffn2_w1, m_ffn2_w3, m_ffn2_w2, m_final_norm, v_ffn1_norm, v_ffn1_w1, v_ffn1_w3, v_ffn1_w2, v_mix_norm, v_w_in, v_conf_conv_w, v_conf_conv_b, v_conf_ln_g, v_conf_ln_b, v_sconv_w, v_pool_w, v_pool_scale, v_gmlp_ln_g, v_gmlp_ln_b, v_gmlp_w_s, v_gmlp_b_s, v_w_out, v_ffn2_norm, v_ffn2_w1, v_ffn2_w3, v_ffn2_w2, v_final_norm):
    given = dict(locals())
    w = {k: given[k] for k in ORDER}
    mom = {k: given["m_" + k] for k in ORDER}
    var = {k: given["v_" + k] for k in ORDER}
    n_l = ffn1_w1.shape[0]
    xs = x[0]
    d = xs.shape[1]
    chip = 2 * lax.axis_index("x") + lax.axis_index("y")

    def shard(name, l):
        a = w[name][l]
        return (jnp.swapaxes(w[name], 1, 2)[l] if name in TRANSPOSED else a).astype(BF16)

    groups = (BIG[:len(BIG) // 2], BIG[len(BIG) // 2:])

    def shards_of(l, g):
        return [shard(name, l) for name in groups[g]] if l < n_l else []

    def finish_gather(l, g, lands):
        out = _sibling_gather(shards_of(l, g), lands)
        return {name: a.reshape(-1, d) for name, a in zip(groups[g], out)}

    shard_rows = [w[name].shape[2] if name in TRANSPOSED else w[name].shape[1] for name in BIG]

    conv_shapes = [conf_conv_w.shape, sconv_w.shape]
    conv_all = _all_gather_small(_pack([conf_conv_w, sconv_w]), reduce=False)
    conv_all = conv_all.reshape(N_CHIPS, 2, -1)[:, 0]
    conf_full, sconv_full = [jnp.concatenate([_unpack(conv_all[k], conv_shapes)[a] for k in range(N_CHIPS)], axis=-1)
                             for a in range(2)]

    lane = jnp.arange(C) // HEAD_DIM
    head_rows = (jnp.arange(8)[:, None] == lane[None, :]).astype(F32)
    tril = jnp.tril(jnp.ones((CHUNK, CHUNK), F32))
    tril4 = jnp.tile(tril, (N_HEADS, 1))
    mixer_consts = []
    for l in range(n_l):
        cw = jnp.pad(conf_full[l], ((0, 32 - CONF_KERNEL), (0, 0)))
        vec = jnp.concatenate([conf_conv_b[l][None], conf_ln_g[l][None], conf_ln_b[l][None], pool_scale[l][None],
                               gmlp_ln_g[l][None], gmlp_ln_b[l][None], sconv_full[l], jnp.zeros((7, C), F32)], axis=0)
        eye = jnp.eye(len(pool_w[l]), dtype=F32)
        pool_blk = (eye[:, None, :, None] * pool_w[l][:, :, None, :]).reshape(C, C).astype(BF16)
        ws = gmlp_w_s[l] * tril[None]
        wstack = ws.reshape(N_HEADS * CHUNK, CHUNK).astype(BF16)
        wstack_t = jnp.swapaxes(ws, 1, 2).reshape(N_HEADS * CHUNK, CHUNK).astype(BF16)
        bias = jnp.repeat(gmlp_b_s[l].T, HEAD_DIM, axis=1)
        mixer_consts.append((cw, vec, pool_blk, wstack, wstack_t, bias))

    saved = []
    cur = xs
    gathered = [finish_gather(0, 0, _ici_gather(shards_of(0, 0)))]
    for l in range(n_l):
        gw = gathered[l]
        cw, vec, pool_blk, wstack, wstack_t, bias = mixer_consts[l]
        x0 = cur
        x1, a1, b1, lands = _ffn_fwd(x0, ffn1_norm[l][None], gw["ffn1_w1"], gw["ffn1_w3"], gw["ffn1_w2"], shards_of(l, 1))
        gw.update(finish_gather(l, 1, lands))
        p = _proj_fwd(x1, mix_norm[l][None], gw["w_in"])
        x2, mix, z = _mixer_fwd(p, x1, cw, vec, pool_blk, wstack, bias, gw["w_out"])
        x3, a2, b2, lands = _ffn_fwd(x2, ffn2_norm[l][None], gw["ffn2_w1"], gw["ffn2_w3"], gw["ffn2_w2"], shards_of(l + 1, 0))
        if l + 1 < n_l:
            gathered.append(finish_gather(l + 1, 0, lands))
        saved.append((x0, x1, x2, a1, b1, a2, b2, p, mix, z))
        cur = x3

    dx, dg_final, loss_part = _loss_bwd(cur, final_norm[None], loss_target[0])

    small_parts = [None] * n_l
    reduced_halves = [[None, None] for _ in range(n_l)]
    pair_sum = lambda big, g: _pair_sum([big[name].reshape(N_CHIPS, -1, d) for name in groups[g]])
    pending = None
    for l in reversed(range(n_l)):
        gw = gathered[l]
        cw, vec, pool_blk, wstack, wstack_t, bias = mixer_consts[l]
        x0, x1, x2, a1, b1, a2, b2, p, mix, z = saved[l]
        big = {}
        out = _ffn_bwd(dx, x2, ffn2_norm[l][None], a2, b2, gw["ffn2_w1"], gw["ffn2_w3"], gw["ffn2_w2"], pending)
        dx, dg_ffn2, h, dy, da, db, u, dx_bf = out[:8]
        if pending is not None:
            reduced_halves[l + 1][0] = out[8]
        big["ffn2_w1"], big["ffn2_w3"], big["ffn2_w2"] = _dw(da, h), _dw(db, h), _dw(u, dy)
        big["w_out"] = _dw(mix, dx_bf)
        second = pair_sum(big, 1)
        dp, dcw, dvec, dpool, dws, dbs = _mixer_bwd(dx, p, z, cw, vec, pool_blk, wstack, wstack_t, bias, tril4, head_rows, gw["w_out"])
        dx, dg_mix, h = _proj_bwd(dx, x1, mix_norm[l][None], dp, gw["w_in"])
        big["w_in"] = _dw(dp, h)
        out = _ffn_bwd(dx, x0, ffn1_norm[l][None], a1, b1, gw["ffn1_w1"], gw["ffn1_w3"], gw["ffn1_w2"], second)
        dx, dg_ffn1, h, dy, da, db, u = out[:7]
        reduced_halves[l][1] = out[8]
        big["ffn1_w1"], big["ffn1_w3"], big["ffn1_w2"] = _dw(da, h), _dw(db, h), _dw(u, dy)
        small_parts[l] = [dg_ffn1[0], dg_mix[0], dg_ffn2[0], dcw, dvec, dpool, dws, dbs]
        pending = pair_sum(big, 0)
    reduced_halves[0][0] = _reduce_chips(pending)
    grad_x = dx[None]

    index_of = {name: a for a, name in enumerate(BIG)}
    full = dict(zip(BIG, _sum_share(reduced_halves, [[index_of[name] for name in g] for g in groups], shard_rows)))
    grad = {name: (jnp.swapaxes(full[name], 1, 2) if name in TRANSPOSED else full[name]) for name in BIG}

    part_shapes = [a.shape for a in small_parts[0]]
    tail = [dg_final[0], loss_part[0]]
    packed = _pack([a for l in range(n_l) for a in small_parts[l]] + tail)
    summed = _unpack(_all_gather_small(packed, reduce=True), part_shapes * n_l + [a.shape for a in tail])
    per_layer = [summed[l * len(part_shapes):(l + 1) * len(part_shapes)] for l in range(n_l)]
    stack = lambda k: jnp.stack([per_layer[l][k] for l in range(n_l)])
    dcw_all, dvec_all, dpool_all, dws_all, dbs_all = stack(3), stack(4), stack(5), stack(6), stack(7)
    loss = summed[-1][0]
    chip_cols = lambda a: lax.dynamic_slice_in_dim(a, chip * (C // N_CHIPS), C // N_CHIPS, axis=2)
    n_pool = pool_w.shape[1]
    grad.update(
        ffn1_norm=stack(0), mix_norm=stack(1), ffn2_norm=stack(2), final_norm=summed[-2],
        conf_conv_w=chip_cols(dcw_all[:, :CONF_KERNEL]), conf_conv_b=dvec_all[:, 0], conf_ln_g=dvec_all[:, 1],
        conf_ln_b=dvec_all[:, 2], pool_scale=dvec_all[:, 3], gmlp_ln_g=dvec_all[:, 4], gmlp_ln_b=dvec_all[:, 5],
        sconv_w=chip_cols(dvec_all[:, 6:6 + SHORT_KERNEL]),
        pool_w=jnp.stack([dpool_all[:, g * POOL_GROUP:(g + 1) * POOL_GROUP, g * POOL_GROUP:(g + 1) * POOL_GROUP]
                          for g in range(n_pool)], axis=1),
        gmlp_w_s=dws_all.reshape(n_l, N_HEADS, CHUNK, CHUNK), gmlp_b_s=dbs_all[:, :N_HEADS],
    )

    delta, new_m, new_v = {}, {}, {}
    for name in BIG:
        rows_of = (lambda a: jnp.swapaxes(a, 1, 2)) if name in TRANSPOSED else (lambda a: a)
        shp = full[name].shape
        out = _adamw(_as2d(rows_of(w[name])), _as2d(full[name]), _as2d(rows_of(mom[name])), _as2d(rows_of(var[name])))
        delta[name], new_m[name], new_v[name] = (rows_of(o.reshape(shp)) for o in out)
    ds, ms, vs = _adamw_small([_as2d(w[k]) if w[k].ndim > 1 else w[k][None] for k in SMALL],
                              [_as2d(grad[k]) if grad[k].ndim > 1 else grad[k][None] for k in SMALL],
                              [_as2d(mom[k]) if mom[k].ndim > 1 else mom[k][None] for k in SMALL],
                              [_as2d(var[k]) if var[k].ndim > 1 else var[k][None] for k in SMALL])
    for k, dl, mo, vo in zip(SMALL, ds, ms, vs):
        delta[k], new_m[k], new_v[k] = dl.reshape(w[k].shape), mo.reshape(w[k].shape), vo.reshape(w[k].shape)

    return (loss, grad_x, *[grad[k] for k in ORDER], *[delta[k] for k in ORDER], *[new_m[k] for k in ORDER],
            *[new_v[k] for k in ORDER])
```

```python
import functools

import jax
import jax.numpy as jnp
from jax import lax
from jax.experimental import pallas as pl
from jax.experimental.pallas import tpu as pltpu

F32 = jnp.float32
BF16 = jnp.bfloat16
MESH = pl.DeviceIdType.MESH
ANY = pl.BlockSpec(memory_space=pl.ANY)

EPS = 1e-6
FFN_RESIDUAL = 0.5
D_GROUP = 256
CONF_KERNEL = 31
SHORT_KERNEL = 3
POOL_GROUP = 64
CHUNK = 128
N_HEADS = 4
HEAD_DIM = 64
HALO = 32
N_CHIPS = 4
N_DEV = 8
LANES = 128
MXU_TILE = 256
VMEM_LIMIT = 56 * 2**20
T_MIX_FWD = 512
T_MIX_BWD = 256

ADAM_LR = 0.001
ADAM_B1 = 0.9
ADAM_B2 = 0.999
ADAM_EPS = 1e-08
ADAM_WD = 0.01
ADAM_STEP = 10

NT = (((1,), (1,)), ((), ()))
TN = (((0,), (0,)), ((), ()))


def _params(*sem):
    return pltpu.CompilerParams(dimension_semantics=sem, vmem_limit_bytes=VMEM_LIMIT)


def _dot(a, b):
    return jnp.dot(a, b, preferred_element_type=F32)


def _dot_nt(a, b):
    return lax.dot_general(a, b, NT, preferred_element_type=F32)


def _t_bf16(v):
    return jnp.transpose(v).astype(BF16)


def _sigmoid(v):
    return 1.0 / (1.0 + jnp.exp(-v))


def _rms(x, g):
    r = lax.rsqrt(jnp.mean(x * x, axis=-1, keepdims=True) + EPS)
    n = x * r
    return n, r, n * g


def _rms_bwd(dh, n, r, g):
    dn = dh * g
    dx = r * (dn - n * jnp.mean(dn * n, axis=-1, keepdims=True))
    return dx, jnp.sum(dh * n, axis=0, keepdims=True)


def _ln_fwd(z, g, b):
    mu = jnp.mean(z, axis=-1, keepdims=True)
    zc = z - mu
    rs = lax.rsqrt(jnp.mean(zc * zc, axis=-1, keepdims=True) + EPS)
    zn = zc * rs
    return zn, rs, zn * g + b


def _ln_bwd(dl, zn, rs, g):
    dzn = dl * g
    dz = rs * (dzn - jnp.mean(dzn, axis=-1, keepdims=True) - zn * jnp.mean(dzn * zn, axis=-1, keepdims=True))
    return dz, jnp.sum(dl * zn, axis=0, keepdims=True), jnp.sum(dl, axis=0, keepdims=True)


def _tile(n, want):
    return want if n % want == 0 else n


def _resident(shape):
    return pl.BlockSpec(shape, lambda i: (0,) * len(shape), pipeline_mode=pl.Buffered(1))


def _ffn_fwd(x, g, w1t, w3t, w2, shards=()):
    s, d = x.shape
    f = w1t.shape[0]
    tm, tf = _tile(s, 512), MXU_TILE
    nj = f // tf
    ni = s // tm
    n_c = len(shards)

    def body(*refs):
        x_ref, g_ref, w1_ref, w3_ref, w2_ref = refs[:5]
        xo_ref, a_ref, b_ref = refs[5 + n_c:8 + n_c]
        u_s = refs[8 + 2 * n_c]
        if n_c:
            gather = lambda: _ici_gather_copies(refs[5:5 + n_c], refs[8 + n_c:8 + 2 * n_c], refs[9 + 2 * n_c], refs[10 + 2 * n_c])

            @pl.when(pl.program_id(0) == 0)
            def _():
                for cp in gather():
                    cp.start()

        h = _rms(x_ref[...], g_ref[...])[2].astype(BF16)
        for j in range(nj):
            cols = slice(j * tf, (j + 1) * tf)
            a = _dot_nt(h, w1_ref[cols, :])
            b = _dot_nt(h, w3_ref[cols, :])
            a_ref[:, cols] = a.astype(BF16)
            b_ref[:, cols] = b.astype(BF16)
            u_s[:, cols] = ((a * _sigmoid(a)) * b).astype(BF16)
        xo_ref[...] = x_ref[...] + FFN_RESIDUAL * _dot(u_s[...], w2_ref[...])
        if n_c:
            @pl.when(pl.program_id(0) == ni - 1)
            def _():
                _wait_all(gather())

    row = pl.BlockSpec((tm, d), lambda i: (i, 0))
    hid = pl.BlockSpec((tm, f), lambda i: (i, 0))
    sems = [pltpu.SemaphoreType.DMA((3 * n_c,))] * 2 if n_c else []
    out = pl.pallas_call(
        body, name="ffn_fwd_gather" if n_c else "ffn_fwd", grid=(ni,),
        in_specs=[row, _resident(g.shape), _resident(w1t.shape), _resident(w3t.shape), _resident(w2.shape)] + [ANY] * n_c,
        out_specs=[row, hid, hid] + [ANY] * n_c,
        out_shape=[jax.ShapeDtypeStruct((s, d), F32), jax.ShapeDtypeStruct((s, f), BF16), jax.ShapeDtypeStruct((s, f), BF16)]
        + _landing_shapes(shards),
        scratch_shapes=[pltpu.VMEM((tm, f), BF16)] + sems,
        compiler_params=_params("arbitrary"),
    )(x, g, w1t, w3t, w2, *shards)
    return out[0], out[1], out[2], list(out[3:])


def _ffn_bwd(dxo, x, g, a, b, w1t, w3t, w2, sums=None):
    s, d = x.shape
    f = w1t.shape[0]
    tm, tf = _tile(s, 256), MXU_TILE
    nj = f // tf
    ni = s // tm
    n_c = 0 if sums is None else 1

    def body(*refs):
        dxo_ref, x_ref, g_ref, a_ref, b_ref, w1_ref, w3_ref, w2_ref = refs[:8]
        dx_ref, dg_ref, h_ref, dy_ref, da_ref, db_ref, u_ref, dxb_ref = refs[8 + n_c:16 + n_c]
        if n_c:
            exchange = lambda: _ici_reduce_copies(refs[8], refs[16 + n_c], *refs[16 + 2 * n_c:])

        @pl.when(pl.program_id(0) == 0)
        def _():
            dg_ref[...] = jnp.zeros_like(dg_ref)
            if n_c:
                copies, local = exchange()
                local.start()
                for cp in copies:
                    cp.start()

        h_ref[...] = _rms(x_ref[...], g_ref[...])[2].astype(BF16)
        dy = (FFN_RESIDUAL * dxo_ref[...]).astype(BF16)
        dy_ref[...] = dy
        for j in range(nj):
            cols = slice(j * tf, (j + 1) * tf)
            du = _dot_nt(dy, w2_ref[cols, :])
            av = a_ref[:, cols].astype(F32)
            bv = b_ref[:, cols].astype(F32)
            sg = _sigmoid(av)
            sl = av * sg
            da = du * bv * (sg * (1.0 + av * (1.0 - sg)))
            db = du * sl
            da_ref[:, cols] = da.astype(BF16)
            db_ref[:, cols] = db.astype(BF16)
            u_ref[:, cols] = (sl * bv).astype(BF16)
        dh = _dot(da_ref[...], w1_ref[...]) + _dot(db_ref[...], w3_ref[...])
        n, r, _ = _rms(x_ref[...], g_ref[...])
        dxr, dg = _rms_bwd(dh, n, r, g_ref[...])
        dx = dxo_ref[...] + dxr
        dx_ref[...] = dx
        dxb_ref[...] = dx.astype(BF16)
        dg_ref[0:1, :] += dg
        if n_c:
            @pl.when(pl.program_id(0) == ni - 1)
            def _():
                copies, local = exchange()
                _wait_all(copies)
                local.wait()

    row = pl.BlockSpec((tm, d), lambda i: (i, 0))
    hid = pl.BlockSpec((tm, f), lambda i: (i, 0))
    extra = [] if sums is None else [sums]
    sems = [pltpu.SemaphoreType.DMA((3,)), pltpu.SemaphoreType.DMA((3,)), pltpu.SemaphoreType.DMA] if n_c else []
    out = pl.pallas_call(
        body, name="ffn_bwd_reduce" if n_c else "ffn_bwd", grid=(ni,),
        in_specs=[row, row, _resident(g.shape), hid, hid, _resident(w1t.shape), _resident(w3t.shape), _resident(w2.shape)]
        + [ANY] * n_c,
        out_specs=[row, pl.BlockSpec((8, d), lambda i: (0, 0)), row, row, hid, hid, hid, row] + [ANY] * n_c,
        out_shape=[jax.ShapeDtypeStruct((s, d), F32), jax.ShapeDtypeStruct((8, d), F32),
                   jax.ShapeDtypeStruct((s, d), BF16), jax.ShapeDtypeStruct((s, d), BF16),
                   jax.ShapeDtypeStruct((s, f), BF16), jax.ShapeDtypeStruct((s, f), BF16), jax.ShapeDtypeStruct((s, f), BF16),
                   jax.ShapeDtypeStruct((s, d), BF16)]
        + [jax.ShapeDtypeStruct(e.shape, e.dtype) for e in extra],
        scratch_shapes=sems,
        compiler_params=_params("arbitrary"),
    )(dxo, x, g, a, b, w1t, w3t, w2, *extra)
    return out


def _dw(am, bm):
    s, r = am.shape
    n = bm.shape[1]
    rb = r // 2 if r > 1024 and (r // 2) % LANES == 0 else r
    ts = _tile(s, 2048)
    ni = s // ts

    def body(am_ref, bm_ref, o_ref, acc_s):
        i = pl.program_id(1)

        @pl.when(i == 0)
        def _():
            acc_s[...] = jnp.zeros_like(acc_s)

        acc_s[...] += lax.dot_general(am_ref[...], bm_ref[...], TN, preferred_element_type=F32)

        @pl.when(i == ni - 1)
        def _():
            o_ref[...] = acc_s[...].astype(BF16)

    return pl.pallas_call(
        body, name="dw", grid=(r // rb, ni),
        in_specs=[pl.BlockSpec((ts, rb), lambda k, i: (i, k)), pl.BlockSpec((ts, n), lambda k, i: (i, 0))],
        out_specs=pl.BlockSpec((rb, n), lambda k, i: (k, 0)),
        out_shape=jax.ShapeDtypeStruct((r, n), BF16),
        scratch_shapes=[pltpu.VMEM((rb, n), F32)],
        compiler_params=_params("arbitrary", "arbitrary"),
    )(am, bm)


def _proj_fwd(x, g, w_int):
    s, d = x.shape
    f = w_int.shape[0]
    tm = _tile(s, 512)

    def body(x_ref, g_ref, w_ref, p_ref):
        p_ref[...] = _dot_nt(_rms(x_ref[...], g_ref[...])[2].astype(BF16), w_ref[...])

    return pl.pallas_call(
        body, name="proj_fwd", grid=(s // tm,),
        in_specs=[pl.BlockSpec((tm, d), lambda i: (i, 0)), _resident(g.shape), _resident(w_int.shape)],
        out_specs=pl.BlockSpec((tm, f), lambda i: (i, 0)),
        out_shape=jax.ShapeDtypeStruct((s, f), F32),
        compiler_params=_params("arbitrary"),
    )(x, g, w_int)


def _proj_bwd(dxo, x, g, dp, w_int):
    s, d = x.shape
    f = w_int.shape[0]
    tm = _tile(s, 512)

    def body(dxo_ref, x_ref, g_ref, dp_ref, w_ref, dx_ref, dg_ref, h_ref):
        @pl.when(pl.program_id(0) == 0)
        def _():
            dg_ref[...] = jnp.zeros_like(dg_ref)

        n, r, h = _rms(x_ref[...], g_ref[...])
        h_ref[...] = h.astype(BF16)
        dxr, dg = _rms_bwd(_dot(dp_ref[...], w_ref[...]), n, r, g_ref[...])
        dx_ref[...] = dxo_ref[...] + dxr
        dg_ref[0:1, :] += dg

    row = pl.BlockSpec((tm, d), lambda i: (i, 0))
    return pl.pallas_call(
        body, name="proj_bwd", grid=(s // tm,),
        in_specs=[row, row, _resident(g.shape), pl.BlockSpec((tm, f), lambda i: (i, 0)), _resident(w_int.shape)],
        out_specs=[row, pl.BlockSpec((8, d), lambda i: (0, 0)), row],
        out_shape=[jax.ShapeDtypeStruct((s, d), F32), jax.ShapeDtypeStruct((8, d), F32), jax.ShapeDtypeStruct((s, d), BF16)],
        compiler_params=_params("arbitrary"),
    )(dxo, x, g, dp, w_int)


C = D_GROUP


def _piece(ref, k):
    return ref[:, k * C:(k + 1) * C]


def _up(v, r):
    return v if r == 0 else pltpu.roll(v, v.shape[0] - r, 0)


def _down(v, r):
    return v if r == 0 else pltpu.roll(v, r, 0)


def _lane_group():
    lane = lax.broadcasted_iota(jnp.int32, (1, C), 1)
    return (lane >= POOL_GROUP).astype(jnp.int32) + (lane >= 2 * POOL_GROUP).astype(jnp.int32) + (
        lane >= 3 * POOL_GROUP).astype(jnp.int32)


def _by_group(grp, v2, v4, v8, v16):
    return jnp.where(grp == 0, v2, jnp.where(grp == 1, v4, jnp.where(grp == 2, v8, v16)))


def _pool_count(grp, row0, t):
    pos = (row0 + lax.broadcasted_iota(jnp.int32, (t, C), 0) + 1).astype(F32)
    return jnp.minimum(pos, _by_group(grp, 2.0, 4.0, 8.0, 16.0))


def _trailing_sums(ext, grp, t):
    s2 = ext + _down(ext, 1)
    s4 = s2 + _down(s2, 2)
    s8 = s4 + _down(s4, 4)
    s16 = s8 + _down(s8, 8)
    return _by_group(grp, s2, s4, s8, s16)[HALO:HALO + t]


def _leading_sums(ext, grp, t):
    s2 = ext + _up(ext, 1)
    s4 = s2 + _up(s2, 2)
    s8 = s4 + _up(s4, 4)
    s16 = s8 + _up(s8, 8)
    return _by_group(grp, s2, s4, s8, s16)[0:t]


def _head_select(r4, grp):
    out = jnp.where(grp == 0, r4[0:CHUNK], 0.0)
    for h in range(1, N_HEADS):
        out = out + jnp.where(grp == h, r4[h * CHUNK:(h + 1) * CHUNK], 0.0)
    return out


def _conv_taps():
    return [(k, (k + 2) % 8, (k + 2) - (k + 2) % 8) for k in range(CONF_KERNEL)]


def _mixer_fwd(p, x1, cw, vec, pool_w, wstack, bias, w_out):
    s, d = x1.shape
    t = _tile(s, T_MIX_FWD)
    n_ext = t + HALO
    dm = w_out.shape[0]

    def body(p_ref, x1_ref, cw_ref, vec_ref, pw_ref, ws_ref, bias_ref, wo_ref, x2_ref, mix_s, z_ref, cy_s, cq_s, cx_s):
        i = pl.program_id(0)

        @pl.when(i == 0)
        def _():
            cy_s[...] = jnp.zeros_like(cy_s)
            cq_s[...] = jnp.zeros_like(cq_s)
            cx_s[...] = jnp.zeros_like(cx_s)

        grp = _lane_group()
        y = _piece(p_ref, 0) * _sigmoid(_piece(p_ref, 1))
        ext = jnp.concatenate([cy_s[...], y], axis=0)
        cy_s[...] = y[t - HALO:t]
        z = jnp.broadcast_to(vec_ref[0:1, :], (t, C))
        shifted = {}
        for k, r, off in _conv_taps():
            if r not in shifted:
                shifted[r] = _up(ext, r)
            z = z + cw_ref[k:k + 1, :] * shifted[r][off:off + t]
        z_ref[...] = z
        ln = _ln_fwd(z, vec_ref[1:2, :], vec_ref[2:3, :])[2]
        mix_s[:, 0:C] = (ln * _sigmoid(ln)).astype(BF16)
        q = _piece(p_ref, 3) * _piece(p_ref, 4)
        ext = jnp.concatenate([cq_s[...], q], axis=0)
        cq_s[...] = q[t - HALO:t]
        cz = vec_ref[8:9, :] * q + vec_ref[7:8, :] * _down(ext, 1)[HALO:] + vec_ref[6:7, :] * _down(ext, 2)[HALO:]
        mix_s[:, C:2 * C] = (_piece(p_ref, 2) * cz).astype(BF16)
        xp = _piece(p_ref, 5)
        ext = jnp.concatenate([cx_s[...], xp], axis=0)
        cx_s[...] = xp[t - HALO:t]
        dd = _trailing_sums(ext, grp, t) / _pool_count(grp, i * t, t) - xp
        mix_s[:, 2 * C:3 * C] = (_dot(dd.astype(BF16), pw_ref[...]) * vec_ref[3:4, :]).astype(BF16)
        vln = _ln_fwd(_piece(p_ref, 7), vec_ref[4:5, :], vec_ref[5:6, :])[2].astype(BF16)
        for n in range(t // CHUNK):
            rows = slice(n * CHUNK, (n + 1) * CHUNK)
            mixed = _head_select(_dot(ws_ref[...], vln[rows]), grp) + bias_ref[...]
            mix_s[rows, 3 * C:4 * C] = (p_ref[rows, 6 * C:7 * C] * mixed).astype(BF16)
        x2_ref[...] = x1_ref[...] + _dot(mix_s[...], wo_ref[...])

    full = lambda a: pl.BlockSpec(a.shape, lambda i: (0, 0))
    return pl.pallas_call(
        body, name="mixer_fwd", grid=(s // t,),
        in_specs=[pl.BlockSpec((t, p.shape[1]), lambda i: (i, 0)), pl.BlockSpec((t, d), lambda i: (i, 0)),
                  full(cw), full(vec), full(pool_w), full(wstack), full(bias), full(w_out)],
        out_specs=[pl.BlockSpec((t, d), lambda i: (i, 0)), pl.BlockSpec((t, dm), lambda i: (i, 0)),
                   pl.BlockSpec((t, C), lambda i: (i, 0))],
        out_shape=[jax.ShapeDtypeStruct((s, d), F32), jax.ShapeDtypeStruct((s, dm), BF16), jax.ShapeDtypeStruct((s, C), F32)],
        scratch_shapes=[pltpu.VMEM((HALO, C), F32)] * 3,
        compiler_params=_params("arbitrary"),
    )(p, x1, cw, vec, pool_w, wstack, bias, w_out)


def _mixer_bwd(dx2, p, z, cw, vec, pool_w, wstack, wstack_t, bias, tril4, head_rows, w_out):
    s, d = dx2.shape
    t = _tile(s, T_MIX_BWD)
    nt = s // t
    n_ext = t + HALO
    hb = t // HALO

    def body(dx2_ref, p_ref, ph_ref, z_ref, cw_ref, vec_ref, pw_ref, ws_ref, wst_ref, bias_ref, tril_ref, hr_ref, wo_ref,
             dp_ref, dcw_ref, dvec_ref, dpool_ref, dws_ref, dbs_ref, cdz_s, cdc_s, cf_s, vy_s, dvl_s, dbias_s):
        i = pl.program_id(0)
        tile = nt - 1 - i

        @pl.when(i == 0)
        def _():
            for ref in (cdz_s, cdc_s, cf_s, dbias_s, dcw_ref, dvec_ref, dpool_ref, dws_ref, dbs_ref):
                ref[...] = jnp.zeros_like(ref)

        grp = _lane_group()
        first = jnp.where(tile > 0, 1.0, 0.0)
        dmix = _dot_nt(dx2_ref[...].astype(BF16), wo_ref[...])
        d_a, d_b, d_c, d_d = (dmix[:, k * C:(k + 1) * C] for k in range(4))

        def acc_vec(row, v):
            dvec_ref[row:row + 1, :] += jnp.sum(v, axis=0, keepdims=True)

        val, gate = _piece(p_ref, 0), _piece(p_ref, 1)
        sgate = _sigmoid(gate)
        y = val * sgate
        y_halo = ph_ref[:, 0:C] * _sigmoid(ph_ref[:, C:2 * C]) * first
        ext = jnp.concatenate([y_halo, y], axis=0)
        for r in range(8):
            vy_s[r] = _up(ext, r)
        zn, rs, ln = _ln_fwd(z_ref[...], vec_ref[1:2, :], vec_ref[2:3, :])
        sg = _sigmoid(ln)
        dln = d_a * (sg * (1.0 + ln * (1.0 - sg)))
        dz, dg, db = _ln_bwd(dln, zn, rs, vec_ref[1:2, :])
        dvec_ref[1:2, :] += dg
        dvec_ref[2:3, :] += db
        acc_vec(0, dz)
        for k, r, off in _conv_taps():
            dcw_ref[k:k + 1, :] += jnp.sum(dz * vy_s[r, off:off + t, :], axis=0, keepdims=True)
        ext = jnp.concatenate([dz, cdz_s[...]], axis=0)
        cdz_s[...] = dz[0:HALO]
        dy = jnp.zeros((t, C), F32)
        shifted = {}
        for k in range(CONF_KERNEL):
            m = CONF_KERNEL - 1 - k
            r, off = m % 8, m - m % 8
            if r not in shifted:
                shifted[r] = _up(ext, r)
            dy = dy + cw_ref[k:k + 1, :] * shifted[r][off:off + t]
        dp_ref[:, 0:C] = (dy * sgate).astype(BF16)
        dp_ref[:, C:2 * C] = (dy * val * sgate * (1.0 - sgate)).astype(BF16)

        sb, sc, sx = _piece(p_ref, 2), _piece(p_ref, 3), _piece(p_ref, 4)
        q = sc * sx
        q_halo = ph_ref[:, 3 * C:4 * C] * ph_ref[:, 4 * C:5 * C] * first
        ext = jnp.concatenate([q_halo, q], axis=0)
        q1, q2 = _down(ext, 1)[HALO:], _down(ext, 2)[HALO:]
        cz = vec_ref[8:9, :] * q + vec_ref[7:8, :] * q1 + vec_ref[6:7, :] * q2
        dcz = d_b * sb
        dp_ref[:, 2 * C:3 * C] = (d_b * cz).astype(BF16)
        acc_vec(8, dcz * q)
        acc_vec(7, dcz * q1)
        acc_vec(6, dcz * q2)
        ext = jnp.concatenate([dcz, cdc_s[...]], axis=0)
        cdc_s[...] = dcz[0:HALO]
        dq = vec_ref[8:9, :] * dcz + vec_ref[7:8, :] * _up(ext, 1)[0:t] + vec_ref[6:7, :] * _up(ext, 2)[0:t]
        dp_ref[:, 3 * C:4 * C] = (dq * sx).astype(BF16)
        dp_ref[:, 4 * C:5 * C] = (dq * sc).astype(BF16)

        xp = _piece(p_ref, 5)
        ext = jnp.concatenate([ph_ref[:, 5 * C:6 * C] * first, xp], axis=0)
        cnt = _pool_count(grp, tile * t, t)
        dd = (_trailing_sums(ext, grp, t) / cnt - xp).astype(BF16)
        e2 = _dot(dd, pw_ref[...])
        acc_vec(3, d_c * e2)
        de = (d_c * vec_ref[3:4, :]).astype(BF16)
        dpool_ref[...] += _dot(_t_bf16(dd.astype(F32)), de)
        ddd = _dot_nt(de, pw_ref[...])
        fq = ddd / cnt
        ext = jnp.concatenate([fq, cf_s[...]], axis=0)
        cf_s[...] = fq[0:HALO]
        dp_ref[:, 5 * C:6 * C] = (_leading_sums(ext, grp, t) - ddd).astype(BF16)

        vn, vrs, vlnf = _ln_fwd(_piece(p_ref, 7), vec_ref[4:5, :], vec_ref[5:6, :])
        vln = vlnf.astype(BF16)
        for n in range(t // CHUNK):
            rows = slice(n * CHUNK, (n + 1) * CHUNK)
            mixed = _head_select(_dot(ws_ref[...], vln[rows]), grp) + bias_ref[...]
            dd_n = d_d[rows]
            dp_ref[rows, 6 * C:7 * C] = (dd_n * mixed).astype(BF16)
            dmx = dd_n * p_ref[rows, 6 * C:7 * C]
            dbias_s[...] += dmx
            dmx_b = dmx.astype(BF16)
            dvl_s[rows, :] = _head_select(_dot(wst_ref[...], dmx_b), grp)
            for h in range(N_HEADS):
                hrows = slice(h * CHUNK, (h + 1) * CHUNK)
                dws_ref[hrows, :] += _dot_nt(jnp.where(grp == h, dmx_b, jnp.zeros_like(dmx_b)), vln[rows])
        dvl = dvl_s[...]
        dv, dg, db = _ln_bwd(dvl, vn, vrs, vec_ref[4:5, :])
        dvec_ref[4:5, :] += dg
        dvec_ref[5:6, :] += db
        dp_ref[:, 7 * C:8 * C] = dv.astype(BF16)

        @pl.when(i == nt - 1)
        def _():
            dws_ref[...] = dws_ref[...] * tril_ref[...]
            dbs_ref[...] = lax.dot_general(hr_ref[...], dbias_s[...], NT, precision=lax.Precision.HIGHEST,
                                           preferred_element_type=F32)

    full = lambda a: pl.BlockSpec(a.shape, lambda i: (0, 0))
    acc = lambda shape: pl.BlockSpec(shape, lambda i: (0, 0))
    f = p.shape[1]
    return pl.pallas_call(
        body, name="mixer_bwd", grid=(nt,),
        in_specs=[pl.BlockSpec((t, d), lambda i: (nt - 1 - i, 0)), pl.BlockSpec((t, f), lambda i: (nt - 1 - i, 0)),
                  pl.BlockSpec((HALO, f), lambda i: (jnp.maximum((nt - 1 - i) * hb - 1, 0), 0)),
                  pl.BlockSpec((t, C), lambda i: (nt - 1 - i, 0)), full(cw), full(vec), full(pool_w), full(wstack), full(wstack_t), full(bias), full(tril4), full(head_rows),
                  full(w_out)],
        out_specs=[pl.BlockSpec((t, f), lambda i: (nt - 1 - i, 0)), acc((32, C)), acc((16, C)), acc((C, C)),
                   acc((N_HEADS * CHUNK, CHUNK)), acc((8, CHUNK))],
        out_shape=[jax.ShapeDtypeStruct((s, f), BF16), jax.ShapeDtypeStruct((32, C), F32), jax.ShapeDtypeStruct((16, C), F32),
                   jax.ShapeDtypeStruct((C, C), F32), jax.ShapeDtypeStruct((N_HEADS * CHUNK, CHUNK), F32),
                   jax.ShapeDtypeStruct((8, CHUNK), F32)],
        scratch_shapes=[pltpu.VMEM((HALO, C), F32)] * 3 + [pltpu.VMEM((8, n_ext, C), F32), pltpu.VMEM((t, C), F32),
                                                            pltpu.VMEM((CHUNK, C), F32)],
        compiler_params=_params("arbitrary"),
    )(dx2, p, p, z, cw, vec, pool_w, wstack, wstack_t, bias, tril4, head_rows, w_out)


def _loss_bwd(x, g, target):
    s, d = x.shape
    tm = _tile(s, 512)

    def body(x_ref, g_ref, t_ref, dx_ref, dg_ref, loss_ref):
        @pl.when(pl.program_id(0) == 0)
        def _():
            dg_ref[...] = jnp.zeros_like(dg_ref)
            loss_ref[...] = jnp.zeros_like(loss_ref)

        n, r, y = _rms(x_ref[...], g_ref[...])
        err = y - t_ref[...]
        loss_ref[...] += 0.5 * jnp.sum(jnp.mean(err * err, axis=-1, keepdims=True), axis=0, keepdims=True)
        dxr, dg = _rms_bwd(err * (1.0 / d), n, r, g_ref[...])
        dx_ref[...] = dxr
        dg_ref[0:1, :] += dg

    row = pl.BlockSpec((tm, d), lambda i: (i, 0))
    return pl.pallas_call(
        body, name="loss_bwd", grid=(s // tm,),
        in_specs=[row, pl.BlockSpec((1, d), lambda i: (0, 0)), row],
        out_specs=[row, pl.BlockSpec((8, d), lambda i: (0, 0)), pl.BlockSpec((8, LANES), lambda i: (0, 0))],
        out_shape=[jax.ShapeDtypeStruct((s, d), F32), jax.ShapeDtypeStruct((8, d), F32), jax.ShapeDtypeStruct((8, LANES), F32)],
        compiler_params=_params("arbitrary"),
    )(x, g, target)


def _adamw_math(w, g, m, v):
    m = ADAM_B1 * m + (1.0 - ADAM_B1) * g
    v = ADAM_B2 * v + (1.0 - ADAM_B2) * (g * g)
    m_hat = m / (1.0 - ADAM_B1 ** ADAM_STEP)
    v_hat = v / (1.0 - ADAM_B2 ** ADAM_STEP)
    return -ADAM_LR * (m_hat / (jnp.sqrt(v_hat) + ADAM_EPS) + ADAM_WD * w), m, v


def _adamw(w, g, m, v):
    r, c = w.shape
    tr = r // 8 if r % 64 == 0 else r

    def body(w_ref, g_ref, m_ref, v_ref, d_ref, mo_ref, vo_ref):
        d_ref[...], mo_ref[...], vo_ref[...] = _adamw_math(w_ref[...], g_ref[...], m_ref[...], v_ref[...])

    blk = pl.BlockSpec((tr, c), lambda i: (i, 0))
    return pl.pallas_call(
        body, name="adamw", grid=(r // tr,), in_specs=[blk] * 4, out_specs=[blk] * 3,
        out_shape=[jax.ShapeDtypeStruct((r, c), F32)] * 3, compiler_params=_params("arbitrary"),
    )(w, g, m, v)


def _adamw_small(ws, gs, ms, vs):
    n = len(ws)

    def body(*refs):
        ins, outs = refs[:4 * n], refs[4 * n:]
        for k in range(n):
            dl, mo, vo = _adamw_math(ins[k][...], ins[n + k][...], ins[2 * n + k][...], ins[3 * n + k][...])
            outs[k][...], outs[n + k][...], outs[2 * n + k][...] = dl, mo, vo

    vm = pl.BlockSpec(memory_space=pltpu.VMEM)
    out = pl.pallas_call(
        body, name="adamw_small", in_specs=[vm] * (4 * n), out_specs=[vm] * (3 * n),
        out_shape=[jax.ShapeDtypeStruct(a.shape, F32) for a in ws] * 3,
        compiler_params=pltpu.CompilerParams(vmem_limit_bytes=VMEM_LIMIT),
    )(*ws, *gs, *ms, *vs)
    return out[:n], out[n:2 * n], out[2 * n:]


def _where_am_i():
    x, y, c = lax.axis_index("x"), lax.axis_index("y"), lax.axis_index("c")
    chips = [(1 - x, y), (x, 1 - y), (1 - x, 1 - y)]
    return x, y, c, chips


def _chip_id(chip):
    return 2 * chip[0] + chip[1]


def _landing_shapes(shards):
    return [jax.ShapeDtypeStruct((3, a.shape[0] // 2, a.shape[1]), a.dtype) for a in shards]


def _ici_gather_copies(ins, lands, send_sems, recv_sems):
    _, _, c, chips = _where_am_i()
    copies = []
    for a, src in enumerate(ins):
        hr = src.shape[0] // 2
        for j, chip in enumerate(chips):
            copies.append(pltpu.make_async_remote_copy(
                src_ref=src.at[pl.ds(c * hr, hr)], dst_ref=lands[a].at[j], send_sem=send_sems.at[3 * a + j],
                recv_sem=recv_sems.at[3 * a + j], device_id=(*chip, c), device_id_type=MESH))
    return copies


def _ici_reduce_copies(s_ref, o_ref, send_sems, recv_sems, local_sem):
    x, y, c, chips = _where_am_i()
    copies = [pltpu.make_async_remote_copy(src_ref=s_ref.at[_chip_id(chip)], dst_ref=o_ref.at[j], send_sem=send_sems.at[j],
                                           recv_sem=recv_sems.at[j], device_id=(*chip, c), device_id_type=MESH)
              for j, chip in enumerate(chips)]
    return copies, pltpu.make_async_copy(s_ref.at[_chip_id((x, y))], o_ref.at[3], local_sem)


def _wait_all(copies):
    for cp in copies:
        cp.wait_recv()
    for cp in copies:
        cp.wait_send()


def _ici_gather(shards):
    n = len(shards)

    def body(*refs):
        copies = _ici_gather_copies(refs[:n], refs[n:2 * n], refs[2 * n], refs[2 * n + 1])
        for cp in copies:
            cp.start()
        _wait_all(copies)

    return pl.pallas_call(
        body, name="ici_gather", in_specs=[ANY] * n, out_specs=[ANY] * n, out_shape=_landing_shapes(shards),
        scratch_shapes=[pltpu.SemaphoreType.DMA((3 * n,))] * 2,
    )(*shards)


SLOTS = 4


def _sibling_gather(shards, lands):
    n = len(shards)
    d = shards[0].shape[1]
    halves = [a.shape[0] // 2 for a in shards]
    hmax = max(halves)
    chunks = [(a, j, halves[a]) for a in range(n) for j in range(3)]
    nc = len(chunks)

    def body(*refs):
        ins, lnd, outs = refs[:n], refs[n:2 * n], refs[2 * n:3 * n]
        sbuf, rbuf, obuf, ld_sems, take_sems, send_sems, recv_sems, place_sems, own_ld_sems, own_st_sems, credits = refs[3 * n:]
        x, y, c, chips = _where_am_i()
        sibling = (x, y, 1 - c)
        me = _chip_id((x, y))

        def load(i):
            a, j, hr = chunks[i]
            return pltpu.make_async_copy(lnd[a].at[j], sbuf.at[i % SLOTS, pl.ds(0, hr)], ld_sems.at[i % SLOTS])

        def push(i):
            hr, slot = chunks[i][2], i % SLOTS
            return pltpu.make_async_remote_copy(src_ref=sbuf.at[slot, pl.ds(0, hr)], dst_ref=rbuf.at[slot, pl.ds(0, hr)],
                                                send_sem=send_sems.at[slot], recv_sem=recv_sems.at[slot], device_id=sibling,
                                                device_id_type=MESH)

        def take(i):
            a, j, hr = chunks[i]
            return pltpu.make_async_copy(rbuf.at[i % SLOTS, pl.ds(0, hr)],
                                         outs[a].at[_chip_id(chips[j]), pl.ds((1 - c) * hr, hr)], take_sems.at[i % SLOTS])

        def place(i):
            a, j, hr = chunks[i]
            return pltpu.make_async_copy(sbuf.at[i % SLOTS, pl.ds(0, hr)], outs[a].at[_chip_id(chips[j]), pl.ds(c * hr, hr)],
                                         place_sems.at[i % SLOTS])

        own = [(a, h, halves[a]) for a in range(n) for h in range(2)]

        def own_load(k):
            a, h, hr = own[k]
            return pltpu.make_async_copy(ins[a].at[pl.ds(h * hr, hr)], obuf.at[k % 2, pl.ds(0, hr)], own_ld_sems.at[k % 2])

        def own_store(k):
            a, h, hr = own[k]
            return pltpu.make_async_copy(obuf.at[k % 2, pl.ds(0, hr)], outs[a].at[me, pl.ds(h * hr, hr)], own_st_sems.at[k % 2])

        def own_step(k):
            if k < len(own):
                if k >= 2:
                    own_store(k - 2).wait()
                own_load(k).start()
            if 1 <= k <= len(own):
                own_load(k - 1).wait()
                own_store(k - 1).start()

        for i in range(min(2, nc)):
            load(i).start()
        for i in range(nc):
            own_step(i)
            if i >= 2:
                push(i - 2).wait_send()
                place(i - 2).wait()
            if i + 2 < nc:
                load(i + 2).start()
            load(i).wait()
            place(i).start()
            if i >= SLOTS:
                pl.semaphore_wait(credits.at[i % SLOTS], 1)
            push(i).start()
            if i >= 1:
                push(i - 1).wait_recv()
                take(i - 1).start()
            if i >= 2:
                take(i - 2).wait()
                if i - 2 + SLOTS < nc:
                    pl.semaphore_signal(credits.at[(i - 2) % SLOTS], inc=1, device_id=sibling, device_id_type=MESH)
        for i in range(max(0, nc - 2), nc):
            push(i).wait_send()
            place(i).wait()
        push(nc - 1).wait_recv()
        take(nc - 1).start()
        for i in range(max(0, nc - 2), nc):
            take(i).wait()
        for k in range(nc, len(own) + 1):
            own_step(k)
        for k in range(max(0, len(own) - 2), len(own)):
            own_store(k).wait()

    dma = pltpu.SemaphoreType.DMA((SLOTS,))
    dma2 = pltpu.SemaphoreType.DMA((2,))
    return pl.pallas_call(
        body, name="sibling_gather", in_specs=[ANY] * (2 * n), out_specs=[ANY] * n,
        out_shape=[jax.ShapeDtypeStruct((N_CHIPS,) + a.shape, a.dtype) for a in shards],
        scratch_shapes=[pltpu.VMEM((SLOTS, hmax, d), BF16), pltpu.VMEM((SLOTS, hmax, d), BF16), pltpu.VMEM((2, hmax, d), BF16),
                        dma, dma, dma, dma, dma, dma2, dma2, pltpu.SemaphoreType.REGULAR((SLOTS,))],
        compiler_params=pltpu.CompilerParams(vmem_limit_bytes=VMEM_LIMIT),
    )(*shards, *lands)


def _pair_sum(grads):
    n = len(grads)
    halves = [g.shape[1] // 2 for g in grads]
    total, rows = sum(halves), max(halves)
    d = grads[0].shape[2]
    chunks, off = [], 0
    for a in range(n):
        chunks += [(a, k, off, halves[a]) for k in range(N_CHIPS)]
        off += halves[a]
    nc = len(chunks)

    def body(*refs):
        ins, out_ref = refs[:n], refs[n]
        sbuf, rbuf, mbuf, obuf, ls_sems, lm_sems, st_sems, send_sems, recv_sems, credits = refs[n + 1:]
        x, y, c, _ = _where_am_i()
        sibling = (x, y, 1 - c)

        def load_theirs(i):
            a, k, _, hr = chunks[i]
            return pltpu.make_async_copy(ins[a].at[k, pl.ds((1 - c) * hr, hr)], sbuf.at[i % SLOTS, pl.ds(0, hr)], ls_sems.at[i % SLOTS])

        def load_mine(i):
            a, k, _, hr = chunks[i]
            return pltpu.make_async_copy(ins[a].at[k, pl.ds(c * hr, hr)], mbuf.at[i % SLOTS, pl.ds(0, hr)], lm_sems.at[i % SLOTS])

        def push(i):
            hr, slot = chunks[i][3], i % SLOTS
            return pltpu.make_async_remote_copy(src_ref=sbuf.at[slot, pl.ds(0, hr)], dst_ref=rbuf.at[slot, pl.ds(0, hr)],
                                                send_sem=send_sems.at[slot], recv_sem=recv_sems.at[slot], device_id=sibling,
                                                device_id_type=MESH)

        def store(i):
            _, k, o, hr = chunks[i]
            slot = i % SLOTS
            return pltpu.make_async_copy(obuf.at[slot, pl.ds(0, hr)], out_ref.at[k, pl.ds(o, hr)], st_sems.at[slot])

        def start_push(i):
            load_theirs(i).wait()
            if i >= SLOTS:
                pl.semaphore_wait(credits.at[i % SLOTS], 1)
            push(i).start()

        for i in range(min(2, nc)):
            load_theirs(i).start()
            load_mine(i).start()
        start_push(0)
        for i in range(nc):
            hr, slot = chunks[i][3], i % SLOTS
            if i + 2 < nc:
                load_theirs(i + 2).start()
                load_mine(i + 2).start()
            if i + 1 < nc:
                start_push(i + 1)
            push(i).wait_recv()
            push(i).wait_send()
            load_mine(i).wait()
            if i >= SLOTS:
                store(i - SLOTS).wait()
            obuf[slot, 0:hr, :] = (mbuf[slot, 0:hr, :].astype(F32) + rbuf[slot, 0:hr, :].astype(F32)).astype(BF16)
            if i + SLOTS < nc:
                pl.semaphore_signal(credits.at[slot], inc=1, device_id=sibling, device_id_type=MESH)
            store(i).start()
        for i in range(max(0, nc - SLOTS), nc):
            store(i).wait()

    stage = pltpu.VMEM((SLOTS, rows, d), BF16)
    dma = pltpu.SemaphoreType.DMA((SLOTS,))
    return pl.pallas_call(
        body, name="pair_sum", in_specs=[ANY] * n, out_specs=ANY, out_shape=jax.ShapeDtypeStruct((N_CHIPS, total, d), BF16),
        scratch_shapes=[stage, stage, stage, stage, dma, dma, dma, dma, dma, pltpu.SemaphoreType.REGULAR((SLOTS,))],
        compiler_params=pltpu.CompilerParams(vmem_limit_bytes=VMEM_LIMIT),
    )(*grads)


def _reduce_chips(sums):
    def body(s_ref, o_ref, send_sems, recv_sems, local_sem):
        copies, local = _ici_reduce_copies(s_ref, o_ref, send_sems, recv_sems, local_sem)
        local.start()
        for cp in copies:
            cp.start()
        _wait_all(copies)
        local.wait()

    return pl.pallas_call(
        body, name="reduce_chips", in_specs=[ANY], out_specs=ANY, out_shape=jax.ShapeDtypeStruct(sums.shape, BF16),
        scratch_shapes=[pltpu.SemaphoreType.DMA((3,)), pltpu.SemaphoreType.DMA((3,)), pltpu.SemaphoreType.DMA],
    )(sums)


def _sum_share(parts, groups, rows):
    layers, n, n_g = len(parts), len(rows), len(groups)
    parts = [part for layer in parts for part in layer]
    n_l = len(parts)
    d = parts[0].shape[2]
    halves = [r // 2 for r in rows]
    hmax = max(halves)
    chunks = []
    for l in range(layers):
        for g, members in enumerate(groups):
            off = 0
            for a in members:
                chunks.append((l * n_g + g, a, off, halves[a], l))
                off += halves[a]
    nc = len(chunks)

    def body(*refs):
        ins, outs = refs[:n_l], refs[n_l:n_l + n]
        pbuf, obuf, rbuf, ld_sems, keep_sems, take_sems, send_sems, recv_sems, credits = refs[n_l + n:]
        x, y, c, _ = _where_am_i()
        sibling = (x, y, 1 - c)

        def load(i):
            part, _, off, hr, _ = chunks[i]
            return pltpu.make_async_copy(ins[part].at[:, pl.ds(off, hr)], pbuf.at[i % SLOTS, :, pl.ds(0, hr)], ld_sems.at[i % SLOTS])

        def keep(i):
            _, a, _, hr, l = chunks[i]
            return pltpu.make_async_copy(obuf.at[i % SLOTS, pl.ds(0, hr)], outs[a].at[l, pl.ds(c * hr, hr)], keep_sems.at[i % SLOTS])

        def push(i):
            hr, slot = chunks[i][3], i % SLOTS
            return pltpu.make_async_remote_copy(src_ref=obuf.at[slot, pl.ds(0, hr)], dst_ref=rbuf.at[slot, pl.ds(0, hr)],
                                                send_sem=send_sems.at[slot], recv_sem=recv_sems.at[slot], device_id=sibling,
                                                device_id_type=MESH)

        def take(i):
            _, a, _, hr, l = chunks[i]
            return pltpu.make_async_copy(rbuf.at[i % SLOTS, pl.ds(0, hr)], outs[a].at[l, pl.ds((1 - c) * hr, hr)],
                                         take_sems.at[i % SLOTS])

        for i in range(min(2, nc)):
            load(i).start()
        for i in range(nc):
            hr, slot = chunks[i][3], i % SLOTS
            if i + 2 < nc:
                load(i + 2).start()
            load(i).wait()
            if i >= SLOTS:
                keep(i - SLOTS).wait()
                push(i - SLOTS).wait_send()
            part = lambda k: pbuf[slot, k, 0:hr, :].astype(F32)
            obuf[slot, 0:hr, :] = ((part(3) + part(0)) + part(1)) + part(2)
            keep(i).start()
            if i >= SLOTS:
                pl.semaphore_wait(credits.at[slot], 1)
            push(i).start()
            if i >= 1:
                push(i - 1).wait_recv()
                take(i - 1).start()
            if i >= 2:
                take(i - 2).wait()
                if i - 2 + SLOTS < nc:
                    pl.semaphore_signal(credits.at[(i - 2) % SLOTS], inc=1, device_id=sibling, device_id_type=MESH)
        push(nc - 1).wait_recv()
        take(nc - 1).start()
        for i in range(max(0, nc - 2), nc):
            take(i).wait()
        for i in range(max(0, nc - SLOTS), nc):
            keep(i).wait()
            push(i).wait_send()

    dma = pltpu.SemaphoreType.DMA((SLOTS,))
    return pl.pallas_call(
        body, name="sum_share", in_specs=[ANY] * n_l, out_specs=[ANY] * n,
        out_shape=[jax.ShapeDtypeStruct((layers, r, d), F32) for r in rows],
        scratch_shapes=[pltpu.VMEM((SLOTS, N_CHIPS, hmax, d), BF16), pltpu.VMEM((SLOTS, hmax, d), F32),
                        pltpu.VMEM((SLOTS, hmax, d), F32), dma, dma, dma, dma, dma, pltpu.SemaphoreType.REGULAR((SLOTS,))],
        compiler_params=pltpu.CompilerParams(vmem_limit_bytes=VMEM_LIMIT),
    )(*parts)


def _all_gather_small(block, reduce):
    m, n = block.shape

    def body(x_ref, out_ref, *scratch):
        if reduce:
            all_ref, send_sems, recv_sems, local_sem = scratch
        else:
            all_ref = out_ref
            send_sems, recv_sems, local_sem = scratch
        x, y, c, chips = _where_am_i()
        me, sibling = (x, y, c), (x, y, 1 - c)

        def rows(px, py, pc):
            return all_ref.at[pl.ds((4 * px + 2 * py + pc) * m, m), :]

        def copy(k, blk, to, src=None):
            return pltpu.make_async_remote_copy(src_ref=rows(*blk) if src is None else src, dst_ref=rows(*blk),
                                                send_sem=send_sems.at[k], recv_sem=recv_sems.at[k], device_id=to,
                                                device_id_type=MESH)

        mine = pltpu.make_async_copy(x_ref, rows(*me), local_sem)
        mine.start()
        first = [copy(0, me, sibling, src=x_ref)]
        first += [copy(1 + j, me, (*chip, c), src=x_ref) for j, chip in enumerate(chips)]
        for cp in first:
            cp.start()
        passed = [copy(4 + j, (*chip, c), sibling) for j, chip in enumerate(chips)]
        for j, chip in enumerate(chips):
            copy(1 + j, (*chip, c), me).wait_recv()
            passed[j].start()
        copy(0, sibling, me).wait_recv()
        for j, chip in enumerate(chips):
            copy(4 + j, (*chip, 1 - c), me).wait_recv()
        for cp in first + passed:
            cp.wait_send()
        mine.wait()
        if reduce:
            total = all_ref[0:m, :]
            for dev in range(1, N_DEV):
                total = total + all_ref[dev * m:(dev + 1) * m, :]
            out_ref[...] = total

    vm = pl.BlockSpec(memory_space=pltpu.VMEM)
    sems = [pltpu.SemaphoreType.DMA((7,)), pltpu.SemaphoreType.DMA((7,)), pltpu.SemaphoreType.DMA]
    return pl.pallas_call(
        body, name="reduce_small" if reduce else "gather_small", in_specs=[vm], out_specs=vm,
        out_shape=jax.ShapeDtypeStruct((m, n) if reduce else (N_DEV * m, n), F32),
        scratch_shapes=([pltpu.VMEM((N_DEV * m, n), F32)] if reduce else []) + sems,
        compiler_params=pltpu.CompilerParams(vmem_limit_bytes=VMEM_LIMIT),
    )(block)


def _pack(arrays):
    flat = jnp.concatenate([a.reshape(-1) for a in arrays])
    pad = (-flat.shape[0]) % (8 * LANES)
    return jnp.pad(flat, (0, pad)).reshape(-1, LANES)


def _unpack(buf, shapes):
    flat = buf.reshape(-1)
    out, off = [], 0
    for shp in shapes:
        size = 1
        for dim in shp:
            size *= dim
        out.append(flat[off:off + size].reshape(shp))
        off += size
    return out


BIG = ("ffn1_w1", "ffn1_w3", "ffn1_w2", "w_in", "w_out", "ffn2_w1", "ffn2_w3", "ffn2_w2")
TRANSPOSED = ("ffn1_w1", "ffn1_w3", "w_in", "ffn2_w1", "ffn2_w3")
SMALL = ("ffn1_norm", "mix_norm", "conf_conv_w", "conf_conv_b", "conf_ln_g", "conf_ln_b", "sconv_w", "pool_w", "pool_scale",
         "gmlp_ln_g", "gmlp_ln_b", "gmlp_w_s", "gmlp_b_s", "ffn2_norm", "final_norm")
ORDER = ("ffn1_norm", "ffn1_w1", "ffn1_w3", "ffn1_w2", "mix_norm", "w_in", "conf_conv_w", "conf_conv_b", "conf_ln_g", "conf_ln_b",
         "sconv_w", "pool_w", "pool_scale", "gmlp_ln_g", "gmlp_ln_b", "gmlp_w_s", "gmlp_b_s", "w_out", "ffn2_norm", "ffn2_w1",
         "ffn2_w3", "ffn2_w2", "final_norm")


def _as2d(a):
    return a.reshape(-1, a.shape[-1])


def kernel(x, ffn1_norm, ffn1_w1, ffn1_w3, ffn1_w2, mix_norm, w_in, conf_conv_w, conf_conv_b, conf_ln_g, conf_ln_b, sconv_w, pool_w, pool_scale, gmlp_ln_g, gmlp_ln_b, gmlp_w_s, gmlp_b_s, w_out, ffn2_norm, ffn2_w1, ffn2_w3, ffn2_w2, final_norm, loss_target, m_ffn1_norm, m_ffn1_w1, m_ffn1_w3, m_ffn1_w2, m_mix_norm, m_w_in, m_conf_conv_w, m_conf_conv_b, m_conf_ln_g, m_conf_ln_b, m_sconv_w, m_pool_w, m_pool_scale, m_gmlp_ln_g, m_gmlp_ln_b, m_gmlp_w_s, m_gmlp_b_s, m_w_out, m_ffn2_norm, m_ffn2_w1, m_ffn2_w3, m_ffn2_w2, m_final_norm, v_ffn1_norm, v_ffn1_w1, v_ffn1_w3, v_ffn1_w2, v_mix_norm, v_w_in, v_conf_conv_w, v_conf_conv_b, v_conf_ln_g, v_conf_ln_b, v_sconv_w, v_pool_w, v_pool_scale, v_gmlp_ln_g, v_gmlp_ln_b, v_gmlp_w_s, v_gmlp_b_s, v_w_out, v_ffn2_norm, v_ffn2_w1, v_ffn2_w3, v_ffn2_w2, v_final_norm):
    given = dict(locals())
    w = {k: given[k] for k in ORDER}
    mom = {k: given["m_" + k] for k in ORDER}
    var = {k: given["v_" + k] for k in ORDER}
    n_l = ffn1_w1.shape[0]
    xs = x[0]
    d = xs.shape[1]
    chip = 2 * lax.axis_index("x") + lax.axis_index("y")

    def shard(name, l):
        a = w[name][l]
        return (jnp.swapaxes(w[name], 1, 2)[l] if name in TRANSPOSED else a).astype(BF16)

    groups = (BIG[:3], BIG[3:])

    def shards_of(l, g):
        return [shard(name, l) for name in groups[g]] if l < n_l else []

    def finish_gather(l, g, lands):
        out = _sibling_gather(shards_of(l, g), lands)
        return {name: a.reshape(-1, d) for name, a in zip(groups[g], out)}

    shard_rows = [w[name].shape[2] if name in TRANSPOSED else w[name].shape[1] for name in BIG]

    conv_shapes = [conf_conv_w.shape, sconv_w.shape]
    conv_all = _all_gather_small(_pack([conf_conv_w, sconv_w]), reduce=False)
    conv_all = conv_all.reshape(N_CHIPS, 2, -1)[:, 0]
    conf_full, sconv_full = [jnp.concatenate([_unpack(conv_all[k], conv_shapes)[a] for k in range(N_CHIPS)], axis=-1)
                             for a in range(2)]

    lane = jnp.arange(C) // HEAD_DIM
    head_rows = (jnp.arange(8)[:, None] == lane[None, :]).astype(F32)
    tril = jnp.tril(jnp.ones((CHUNK, CHUNK), F32))
    tril4 = jnp.tile(tril, (N_HEADS, 1))
    mixer_consts = []
    for l in range(n_l):
        cw = jnp.pad(conf_full[l], ((0, 32 - CONF_KERNEL), (0, 0)))
        vec = jnp.concatenate([conf_conv_b[l][None], conf_ln_g[l][None], conf_ln_b[l][None], pool_scale[l][None],
                               gmlp_ln_g[l][None], gmlp_ln_b[l][None], sconv_full[l], jnp.zeros((7, C), F32)], axis=0)
        eye = jnp.eye(len(pool_w[l]), dtype=F32)
        pool_blk = (eye[:, None, :, None] * pool_w[l][:, :, None, :]).reshape(C, C).astype(BF16)
        ws = gmlp_w_s[l] * tril[None]
        wstack = ws.reshape(N_HEADS * CHUNK, CHUNK).astype(BF16)
        wstack_t = jnp.swapaxes(ws, 1, 2).reshape(N_HEADS * CHUNK, CHUNK).astype(BF16)
        bias = jnp.repeat(gmlp_b_s[l].T, HEAD_DIM, axis=1)
        mixer_consts.append((cw, vec, pool_blk, wstack, wstack_t, bias))

    saved = []
    cur = xs
    gathered = [finish_gather(0, 0, _ici_gather(shards_of(0, 0)))]
    for l in range(n_l):
        gw = gathered[l]
        cw, vec, pool_blk, wstack, wstack_t, bias = mixer_consts[l]
        x0 = cur
        x1, a1, b1, lands = _ffn_fwd(x0, ffn1_norm[l][None], gw["ffn1_w1"], gw["ffn1_w3"], gw["ffn1_w2"], shards_of(l, 1))
        gw.update(finish_gather(l, 1, lands))
        p = _proj_fwd(x1, mix_norm[l][None], gw["w_in"])
        x2, mix, z = _mixer_fwd(p, x1, cw, vec, pool_blk, wstack, bias, gw["w_out"])
        x3, a2, b2, lands = _ffn_fwd(x2, ffn2_norm[l][None], gw["ffn2_w1"], gw["ffn2_w3"], gw["ffn2_w2"], shards_of(l + 1, 0))
        if l + 1 < n_l:
            gathered.append(finish_gather(l + 1, 0, lands))
        saved.append((x0, x1, x2, a1, b1, a2, b2, p, mix, z))
        cur = x3

    dx, dg_final, loss_part = _loss_bwd(cur, final_norm[None], loss_target[0])

    small_parts = [None] * n_l
    reduced_halves = [[None, None] for _ in range(n_l)]
    pair_sum = lambda big, g: _pair_sum([big[name].reshape(N_CHIPS, -1, d) for name in groups[g]])
    pending = None
    for l in reversed(range(n_l)):
        gw = gathered[l]
        cw, vec, pool_blk, wstack, wstack_t, bias = mixer_consts[l]
        x0, x1, x2, a1, b1, a2, b2, p, mix, z = saved[l]
        big = {}
        out = _ffn_bwd(dx, x2, ffn2_norm[l][None], a2, b2, gw["ffn2_w1"], gw["ffn2_w3"], gw["ffn2_w2"], pending)
        dx, dg_ffn2, h, dy, da, db, u, dx_bf = out[:8]
        if pending is not None:
            reduced_halves[l + 1][0] = out[8]
        big["ffn2_w1"], big["ffn2_w3"], big["ffn2_w2"] = _dw(da, h), _dw(db, h), _dw(u, dy)
        big["w_out"] = _dw(mix, dx_bf)
        dp, dcw, dvec, dpool, dws, dbs = _mixer_bwd(dx, p, z, cw, vec, pool_blk, wstack, wstack_t, bias, tril4, head_rows, gw["w_out"])
        dx, dg_mix, h = _proj_bwd(dx, x1, mix_norm[l][None], dp, gw["w_in"])
        big["w_in"] = _dw(dp, h)
        second = pair_sum(big, 1)
        out = _ffn_bwd(dx, x0, ffn1_norm[l][None], a1, b1, gw["ffn1_w1"], gw["ffn1_w3"], gw["ffn1_w2"], second)
        dx, dg_ffn1, h, dy, da, db, u = out[:7]
        reduced_halves[l][1] = out[8]
        big["ffn1_w1"], big["ffn1_w3"], big["ffn1_w2"] = _dw(da, h), _dw(db, h), _dw(u, dy)
        small_parts[l] = [dg_ffn1[0], dg_mix[0], dg_ffn2[0], dcw, dvec, dpool, dws, dbs]
        pending = pair_sum(big, 0)
    reduced_halves[0][0] = _reduce_chips(pending)
    grad_x = dx[None]

    index_of = {name: a for a, name in enumerate(BIG)}
    full = dict(zip(BIG, _sum_share(reduced_halves, [[index_of[name] for name in g] for g in groups], shard_rows)))
    grad = {name: (jnp.swapaxes(full[name], 1, 2) if name in TRANSPOSED else full[name]) for name in BIG}

    part_shapes = [a.shape for a in small_parts[0]]
    tail = [dg_final[0], loss_part[0]]
    packed = _pack([a for l in range(n_l) for a in small_parts[l]] + tail)
    summed = _unpack(_all_gather_small(packed, reduce=True), part_shapes * n_l + [a.shape for a in tail])
    per_layer = [summed[l * len(part_shapes):(l + 1) * len(part_shapes)] for l in range(n_l)]
    stack = lambda k: jnp.stack([per_layer[l][k] for l in range(n_l)])
    dcw_all, dvec_all, dpool_all, dws_all, dbs_all = stack(3), stack(4), stack(5), stack(6), stack(7)
    loss = summed[-1][0]
    chip_cols = lambda a: lax.dynamic_slice_in_dim(a, chip * (C // N_CHIPS), C // N_CHIPS, axis=2)
    n_pool = pool_w.shape[1]
    grad.update(
        ffn1_norm=stack(0), mix_norm=stack(1), ffn2_norm=stack(2), final_norm=summed[-2],
        conf_conv_w=chip_cols(dcw_all[:, :CONF_KERNEL]), conf_conv_b=dvec_all[:, 0], conf_ln_g=dvec_all[:, 1],
        conf_ln_b=dvec_all[:, 2], pool_scale=dvec_all[:, 3], gmlp_ln_g=dvec_all[:, 4], gmlp_ln_b=dvec_all[:, 5],
        sconv_w=chip_cols(dvec_all[:, 6:6 + SHORT_KERNEL]),
        pool_w=jnp.stack([dpool_all[:, g * POOL_GROUP:(g + 1) * POOL_GROUP, g * POOL_GROUP:(g + 1) * POOL_GROUP]
                          for g in range(n_pool)], axis=1),
        gmlp_w_s=dws_all.reshape(n_l, N_HEADS, CHUNK, CHUNK), gmlp_b_s=dbs_all[:, :N_HEADS],
    )

    delta, new_m, new_v = {}, {}, {}
    for name in BIG:
        rows_of = (lambda a: jnp.swapaxes(a, 1, 2)) if name in TRANSPOSED else (lambda a: a)
        shp = full[name].shape
        out = _adamw(_as2d(rows_of(w[name])), _as2d(full[name]), _as2d(rows_of(mom[name])), _as2d(rows_of(var[name])))
        delta[name], new_m[name], new_v[name] = (rows_of(o.reshape(shp)) for o in out)
    ds, ms, vs = _adamw_small([_as2d(w[k]) if w[k].ndim > 1 else w[k][None] for k in SMALL],
                              [_as2d(grad[k]) if grad[k].ndim > 1 else grad[k][None] for k in SMALL],
                              [_as2d(mom[k]) if mom[k].ndim > 1 else mom[k][None] for k in SMALL],
                              [_as2d(var[k]) if var[k].ndim > 1 else var[k][None] for k in SMALL])
    for k, dl, mo, vo in zip(SMALL, ds, ms, vs):
        delta[k], new_m[k], new_v[k] = dl.reshape(w[k].shape), mo.reshape(w[k].shape), vo.reshape(w[k].shape)

    return (loss, grad_x, *[grad[k] for k in ORDER], *[delta[k] for k in ORDER], *[new_m[k] for k in ORDER],
            *[new_v[k] for k in ORDER])
```

```python
import functools

import jax
import jax.numpy as jnp
from jax import lax
from jax.experimental import pallas as pl
from jax.experimental.pallas import tpu as pltpu

F32 = jnp.float32
BF16 = jnp.bfloat16
MESH = pl.DeviceIdType.MESH
ANY = pl.BlockSpec(memory_space=pl.ANY)

EPS = 1e-6
FFN_RESIDUAL = 0.5
D_GROUP = 256
CONF_KERNEL = 31
SHORT_KERNEL = 3
POOL_GROUP = 64
CHUNK = 128
N_HEADS = 4
HEAD_DIM = 64
HALO = 32
N_CHIPS = 4
N_DEV = 8
LANES = 128
MXU_TILE = 256
VMEM_LIMIT = 56 * 2**20
T_MIX_FWD = 512
T_MIX_BWD = 256

ADAM_LR = 0.001
ADAM_B1 = 0.9
ADAM_B2 = 0.999
ADAM_EPS = 1e-08
ADAM_WD = 0.01
ADAM_STEP = 10

NT = (((1,), (1,)), ((), ()))
TN = (((0,), (0,)), ((), ()))


def _params(*sem):
    return pltpu.CompilerParams(dimension_semantics=sem, vmem_limit_bytes=VMEM_LIMIT)


def _dot(a, b):
    return jnp.dot(a, b, preferred_element_type=F32)


def _dot_nt(a, b):
    return lax.dot_general(a, b, NT, preferred_element_type=F32)


def _t_bf16(v):
    return jnp.transpose(v).astype(BF16)


def _sigmoid(v):
    return 1.0 / (1.0 + jnp.exp(-v))


def _rms(x, g):
    r = lax.rsqrt(jnp.mean(x * x, axis=-1, keepdims=True) + EPS)
    n = x * r
    return n, r, n * g


def _rms_bwd(dh, n, r, g):
    dn = dh * g
    dx = r * (dn - n * jnp.mean(dn * n, axis=-1, keepdims=True))
    return dx, jnp.sum(dh * n, axis=0, keepdims=True)


def _ln_fwd(z, g, b):
    mu = jnp.mean(z, axis=-1, keepdims=True)
    zc = z - mu
    rs = lax.rsqrt(jnp.mean(zc * zc, axis=-1, keepdims=True) + EPS)
    zn = zc * rs
    return zn, rs, zn * g + b


def _ln_bwd(dl, zn, rs, g):
    dzn = dl * g
    dz = rs * (dzn - jnp.mean(dzn, axis=-1, keepdims=True) - zn * jnp.mean(dzn * zn, axis=-1, keepdims=True))
    return dz, jnp.sum(dl * zn, axis=0, keepdims=True), jnp.sum(dl, axis=0, keepdims=True)


def _tile(n, want):
    return want if n % want == 0 else n


def _resident(shape):
    return pl.BlockSpec(shape, lambda i: (0,) * len(shape), pipeline_mode=pl.Buffered(1))


def _ffn_fwd(x, g, w1t, w3t, w2, shards=()):
    s, d = x.shape
    f = w1t.shape[0]
    tm, tf = _tile(s, 512), MXU_TILE
    nj = f // tf
    ni = s // tm
    n_c = len(shards)

    def body(*refs):
        x_ref, g_ref, w1_ref, w3_ref, w2_ref = refs[:5]
        xo_ref, a_ref, b_ref, h_ref = refs[5 + n_c:9 + n_c]
        u_s = refs[9 + 2 * n_c]
        if n_c:
            gather = lambda: _ici_gather_copies(refs[5:5 + n_c], refs[9 + n_c:9 + 2 * n_c], refs[10 + 2 * n_c], refs[11 + 2 * n_c])

            @pl.when(pl.program_id(0) == 0)
            def _():
                for cp in gather():
                    cp.start()

        h = _rms(x_ref[...], g_ref[...])[2].astype(BF16)
        h_ref[...] = h
        for j in range(nj):
            cols = slice(j * tf, (j + 1) * tf)
            a = _dot_nt(h, w1_ref[cols, :])
            b = _dot_nt(h, w3_ref[cols, :])
            a_ref[:, cols] = a.astype(BF16)
            b_ref[:, cols] = b.astype(BF16)
            u_s[:, cols] = ((a * _sigmoid(a)) * b).astype(BF16)
        xo_ref[...] = x_ref[...] + FFN_RESIDUAL * _dot(u_s[...], w2_ref[...])
        if n_c:
            @pl.when(pl.program_id(0) == ni - 1)
            def _():
                _wait_all(gather())

    row = pl.BlockSpec((tm, d), lambda i: (i, 0))
    hid = pl.BlockSpec((tm, f), lambda i: (i, 0))
    sems = [pltpu.SemaphoreType.DMA((3 * n_c,))] * 2 if n_c else []
    out = pl.pallas_call(
        body, name="ffn_fwd_gather" if n_c else "ffn_fwd", grid=(ni,),
        in_specs=[row, _resident(g.shape), _resident(w1t.shape), _resident(w3t.shape), _resident(w2.shape)] + [ANY] * n_c,
        out_specs=[row, hid, hid, row] + [ANY] * n_c,
        out_shape=[jax.ShapeDtypeStruct((s, d), F32), jax.ShapeDtypeStruct((s, f), BF16), jax.ShapeDtypeStruct((s, f), BF16),
                   jax.ShapeDtypeStruct((s, d), BF16)] + _landing_shapes(shards),
        scratch_shapes=[pltpu.VMEM((tm, f), BF16)] + sems,
        compiler_params=_params("arbitrary"),
    )(x, g, w1t, w3t, w2, *shards)
    return out[0], out[1], out[2], out[3], list(out[4:])


def _ffn_bwd(dxo, x, g, a, b, w1t, w3t, w2, sums=None):
    s, d = x.shape
    f = w1t.shape[0]
    tm, tf = _tile(s, 256), MXU_TILE
    nj = f // tf
    ni = s // tm
    n_c = 0 if sums is None else 1

    def body(*refs):
        dxo_ref, x_ref, g_ref, a_ref, b_ref, w1_ref, w3_ref, w2_ref = refs[:8]
        dx_ref, dg_ref, dy_ref, da_ref, db_ref, u_ref, dxb_ref = refs[8 + n_c:15 + n_c]
        if n_c:
            exchange = lambda: _ici_reduce_copies(refs[8], refs[15 + n_c], *refs[15 + 2 * n_c:])

        @pl.when(pl.program_id(0) == 0)
        def _():
            dg_ref[...] = jnp.zeros_like(dg_ref)
            if n_c:
                copies, local = exchange()
                local.start()
                for cp in copies:
                    cp.start()

        dy = (FFN_RESIDUAL * dxo_ref[...]).astype(BF16)
        dy_ref[...] = dy
        for j in range(nj):
            cols = slice(j * tf, (j + 1) * tf)
            du = _dot_nt(dy, w2_ref[cols, :])
            av = a_ref[:, cols].astype(F32)
            bv = b_ref[:, cols].astype(F32)
            sg = _sigmoid(av)
            sl = av * sg
            da = du * bv * (sg * (1.0 + av * (1.0 - sg)))
            db = du * sl
            da_ref[:, cols] = da.astype(BF16)
            db_ref[:, cols] = db.astype(BF16)
            u_ref[:, cols] = (sl * bv).astype(BF16)
        dh = _dot(da_ref[...], w1_ref[...]) + _dot(db_ref[...], w3_ref[...])
        n, r, _ = _rms(x_ref[...], g_ref[...])
        dxr, dg = _rms_bwd(dh, n, r, g_ref[...])
        dx = dxo_ref[...] + dxr
        dx_ref[...] = dx
        dxb_ref[...] = dx.astype(BF16)
        dg_ref[0:1, :] += dg
        if n_c:
            @pl.when(pl.program_id(0) == ni - 1)
            def _():
                copies, local = exchange()
                _wait_all(copies)
                local.wait()

    row = pl.BlockSpec((tm, d), lambda i: (i, 0))
    hid = pl.BlockSpec((tm, f), lambda i: (i, 0))
    extra = [] if sums is None else [sums]
    sems = [pltpu.SemaphoreType.DMA((3,)), pltpu.SemaphoreType.DMA((3,)), pltpu.SemaphoreType.DMA] if n_c else []
    out = pl.pallas_call(
        body, name="ffn_bwd_reduce" if n_c else "ffn_bwd", grid=(ni,),
        in_specs=[row, row, _resident(g.shape), hid, hid, _resident(w1t.shape), _resident(w3t.shape), _resident(w2.shape)]
        + [ANY] * n_c,
        out_specs=[row, pl.BlockSpec((8, d), lambda i: (0, 0)), row, hid, hid, hid, row] + [ANY] * n_c,
        out_shape=[jax.ShapeDtypeStruct((s, d), F32), jax.ShapeDtypeStruct((8, d), F32), jax.ShapeDtypeStruct((s, d), BF16),
                   jax.ShapeDtypeStruct((s, f), BF16), jax.ShapeDtypeStruct((s, f), BF16), jax.ShapeDtypeStruct((s, f), BF16),
                   jax.ShapeDtypeStruct((s, d), BF16)]
        + [jax.ShapeDtypeStruct(e.shape, e.dtype) for e in extra],
        scratch_shapes=sems,
        compiler_params=_params("arbitrary"),
    )(dxo, x, g, a, b, w1t, w3t, w2, *extra)
    return out


def _dw(am, bm):
    s, r = am.shape
    n = bm.shape[1]
    rb = r // 2 if r > 1024 and (r // 2) % LANES == 0 else r
    ts = _tile(s, 2048)
    ni = s // ts

    def body(am_ref, bm_ref, o_ref, acc_s):
        i = pl.program_id(1)

        @pl.when(i == 0)
        def _():
            acc_s[...] = jnp.zeros_like(acc_s)

        acc_s[...] += lax.dot_general(am_ref[...], bm_ref[...], TN, preferred_element_type=F32)

        @pl.when(i == ni - 1)
        def _():
            o_ref[...] = acc_s[...].astype(BF16)

    return pl.pallas_call(
        body, name="dw", grid=(r // rb, ni),
        in_specs=[pl.BlockSpec((ts, rb), lambda k, i: (i, k)), pl.BlockSpec((ts, n), lambda k, i: (i, 0))],
        out_specs=pl.BlockSpec((rb, n), lambda k, i: (k, 0)),
        out_shape=jax.ShapeDtypeStruct((r, n), BF16),
        scratch_shapes=[pltpu.VMEM((rb, n), F32)],
        compiler_params=_params("arbitrary", "arbitrary"),
    )(am, bm)


def _proj_fwd(x, g, w_int):
    s, d = x.shape
    f = w_int.shape[0]
    tm = _tile(s, 512)

    def body(x_ref, g_ref, w_ref, p_ref):
        p_ref[...] = _dot_nt(_rms(x_ref[...], g_ref[...])[2].astype(BF16), w_ref[...])

    return pl.pallas_call(
        body, name="proj_fwd", grid=(s // tm,),
        in_specs=[pl.BlockSpec((tm, d), lambda i: (i, 0)), _resident(g.shape), _resident(w_int.shape)],
        out_specs=pl.BlockSpec((tm, f), lambda i: (i, 0)),
        out_shape=jax.ShapeDtypeStruct((s, f), F32),
        compiler_params=_params("arbitrary"),
    )(x, g, w_int)


def _proj_bwd(dxo, x, g, dp, w_int):
    s, d = x.shape
    f = w_int.shape[0]
    tm = _tile(s, 512)

    def body(dxo_ref, x_ref, g_ref, dp_ref, w_ref, dx_ref, dg_ref, h_ref):
        @pl.when(pl.program_id(0) == 0)
        def _():
            dg_ref[...] = jnp.zeros_like(dg_ref)

        n, r, h = _rms(x_ref[...], g_ref[...])
        h_ref[...] = h.astype(BF16)
        dxr, dg = _rms_bwd(_dot(dp_ref[...], w_ref[...]), n, r, g_ref[...])
        dx_ref[...] = dxo_ref[...] + dxr
        dg_ref[0:1, :] += dg

    row = pl.BlockSpec((tm, d), lambda i: (i, 0))
    return pl.pallas_call(
        body, name="proj_bwd", grid=(s // tm,),
        in_specs=[row, row, _resident(g.shape), pl.BlockSpec((tm, f), lambda i: (i, 0)), _resident(w_int.shape)],
        out_specs=[row, pl.BlockSpec((8, d), lambda i: (0, 0)), row],
        out_shape=[jax.ShapeDtypeStruct((s, d), F32), jax.ShapeDtypeStruct((8, d), F32), jax.ShapeDtypeStruct((s, d), BF16)],
        compiler_params=_params("arbitrary"),
    )(dxo, x, g, dp, w_int)


C = D_GROUP


def _piece(ref, k):
    return ref[:, k * C:(k + 1) * C]


def _up(v, r):
    return v if r == 0 else pltpu.roll(v, v.shape[0] - r, 0)


def _down(v, r):
    return v if r == 0 else pltpu.roll(v, r, 0)


def _lane_group():
    lane = lax.broadcasted_iota(jnp.int32, (1, C), 1)
    return (lane >= POOL_GROUP).astype(jnp.int32) + (lane >= 2 * POOL_GROUP).astype(jnp.int32) + (
        lane >= 3 * POOL_GROUP).astype(jnp.int32)


def _by_group(grp, v2, v4, v8, v16):
    return jnp.where(grp == 0, v2, jnp.where(grp == 1, v4, jnp.where(grp == 2, v8, v16)))


def _pool_count(grp, row0, t):
    pos = (row0 + lax.broadcasted_iota(jnp.int32, (t, C), 0) + 1).astype(F32)
    return jnp.minimum(pos, _by_group(grp, 2.0, 4.0, 8.0, 16.0))


def _trailing_sums(ext, grp, t):
    s2 = ext + _down(ext, 1)
    s4 = s2 + _down(s2, 2)
    s8 = s4 + _down(s4, 4)
    s16 = s8 + _down(s8, 8)
    return _by_group(grp, s2, s4, s8, s16)[HALO:HALO + t]


def _leading_sums(ext, grp, t):
    s2 = ext + _up(ext, 1)
    s4 = s2 + _up(s2, 2)
    s8 = s4 + _up(s4, 4)
    s16 = s8 + _up(s8, 8)
    return _by_group(grp, s2, s4, s8, s16)[0:t]


def _head_select(r4, grp):
    out = jnp.where(grp == 0, r4[0:CHUNK], 0.0)
    for h in range(1, N_HEADS):
        out = out + jnp.where(grp == h, r4[h * CHUNK:(h + 1) * CHUNK], 0.0)
    return out


def _conv_taps():
    return [(k, (k + 2) % 8, (k + 2) - (k + 2) % 8) for k in range(CONF_KERNEL)]


def _mixer_fwd(p, x1, cw, vec, pool_w, wstack, bias, w_out):
    s, d = x1.shape
    t = _tile(s, T_MIX_FWD)
    n_ext = t + HALO
    dm = w_out.shape[0]

    def body(p_ref, x1_ref, cw_ref, vec_ref, pw_ref, ws_ref, bias_ref, wo_ref, x2_ref, mix_s, z_ref, cy_s, cq_s, cx_s):
        i = pl.program_id(0)

        @pl.when(i == 0)
        def _():
            cy_s[...] = jnp.zeros_like(cy_s)
            cq_s[...] = jnp.zeros_like(cq_s)
            cx_s[...] = jnp.zeros_like(cx_s)

        grp = _lane_group()
        y = _piece(p_ref, 0) * _sigmoid(_piece(p_ref, 1))
        ext = jnp.concatenate([cy_s[...], y], axis=0)
        cy_s[...] = y[t - HALO:t]
        z = jnp.broadcast_to(vec_ref[0:1, :], (t, C))
        shifted = {}
        for k, r, off in _conv_taps():
            if r not in shifted:
                shifted[r] = _up(ext, r)
            z = z + cw_ref[k:k + 1, :] * shifted[r][off:off + t]
        z_ref[...] = z
        ln = _ln_fwd(z, vec_ref[1:2, :], vec_ref[2:3, :])[2]
        mix_s[:, 0:C] = (ln * _sigmoid(ln)).astype(BF16)
        q = _piece(p_ref, 3) * _piece(p_ref, 4)
        ext = jnp.concatenate([cq_s[...], q], axis=0)
        cq_s[...] = q[t - HALO:t]
        cz = vec_ref[8:9, :] * q + vec_ref[7:8, :] * _down(ext, 1)[HALO:] + vec_ref[6:7, :] * _down(ext, 2)[HALO:]
        mix_s[:, C:2 * C] = (_piece(p_ref, 2) * cz).astype(BF16)
        xp = _piece(p_ref, 5)
        ext = jnp.concatenate([cx_s[...], xp], axis=0)
        cx_s[...] = xp[t - HALO:t]
        dd = _trailing_sums(ext, grp, t) / _pool_count(grp, i * t, t) - xp
        mix_s[:, 2 * C:3 * C] = (_dot(dd.astype(BF16), pw_ref[...]) * vec_ref[3:4, :]).astype(BF16)
        vln = _ln_fwd(_piece(p_ref, 7), vec_ref[4:5, :], vec_ref[5:6, :])[2].astype(BF16)
        for n in range(t // CHUNK):
            rows = slice(n * CHUNK, (n + 1) * CHUNK)
            mixed = _head_select(_dot(ws_ref[...], vln[rows]), grp) + bias_ref[...]
            mix_s[rows, 3 * C:4 * C] = (p_ref[rows, 6 * C:7 * C] * mixed).astype(BF16)
        x2_ref[...] = x1_ref[...] + _dot(mix_s[...], wo_ref[...])

    full = lambda a: pl.BlockSpec(a.shape, lambda i: (0, 0))
    return pl.pallas_call(
        body, name="mixer_fwd", grid=(s // t,),
        in_specs=[pl.BlockSpec((t, p.shape[1]), lambda i: (i, 0)), pl.BlockSpec((t, d), lambda i: (i, 0)),
                  full(cw), full(vec), full(pool_w), full(wstack), full(bias), full(w_out)],
        out_specs=[pl.BlockSpec((t, d), lambda i: (i, 0)), pl.BlockSpec((t, dm), lambda i: (i, 0)),
                   pl.BlockSpec((t, C), lambda i: (i, 0))],
        out_shape=[jax.ShapeDtypeStruct((s, d), F32), jax.ShapeDtypeStruct((s, dm), BF16), jax.ShapeDtypeStruct((s, C), F32)],
        scratch_shapes=[pltpu.VMEM((HALO, C), F32)] * 3,
        compiler_params=_params("arbitrary"),
    )(p, x1, cw, vec, pool_w, wstack, bias, w_out)


def _mixer_bwd(dx2, p, z, cw, vec, pool_w, wstack, wstack_t, bias, tril4, head_rows, w_out):
    s, d = dx2.shape
    t = _tile(s, T_MIX_BWD)
    nt = s // t
    n_ext = t + HALO
    hb = t // HALO

    def body(dx2_ref, p_ref, ph_ref, z_ref, cw_ref, vec_ref, pw_ref, ws_ref, wst_ref, bias_ref, tril_ref, hr_ref, wo_ref,
             dp_ref, dcw_ref, dvec_ref, dpool_ref, dws_ref, dbs_ref, cdz_s, cdc_s, cf_s, vy_s, dvl_s, dbias_s):
        i = pl.program_id(0)
        tile = nt - 1 - i

        @pl.when(i == 0)
        def _():
            for ref in (cdz_s, cdc_s, cf_s, dbias_s, dcw_ref, dvec_ref, dpool_ref, dws_ref, dbs_ref):
                ref[...] = jnp.zeros_like(ref)

        grp = _lane_group()
        first = jnp.where(tile > 0, 1.0, 0.0)
        dmix = _dot_nt(dx2_ref[...].astype(BF16), wo_ref[...])
        d_a, d_b, d_c, d_d = (dmix[:, k * C:(k + 1) * C] for k in range(4))

        def acc_vec(row, v):
            dvec_ref[row:row + 1, :] += jnp.sum(v, axis=0, keepdims=True)

        val, gate = _piece(p_ref, 0), _piece(p_ref, 1)
        sgate = _sigmoid(gate)
        y = val * sgate
        y_halo = ph_ref[:, 0:C] * _sigmoid(ph_ref[:, C:2 * C]) * first
        ext = jnp.concatenate([y_halo, y], axis=0)
        for r in range(8):
            vy_s[r] = _up(ext, r)
        zn, rs, ln = _ln_fwd(z_ref[...], vec_ref[1:2, :], vec_ref[2:3, :])
        sg = _sigmoid(ln)
        dln = d_a * (sg * (1.0 + ln * (1.0 - sg)))
        dz, dg, db = _ln_bwd(dln, zn, rs, vec_ref[1:2, :])
        dvec_ref[1:2, :] += dg
        dvec_ref[2:3, :] += db
        acc_vec(0, dz)
        for k, r, off in _conv_taps():
            dcw_ref[k:k + 1, :] += jnp.sum(dz * vy_s[r, off:off + t, :], axis=0, keepdims=True)
        ext = jnp.concatenate([dz, cdz_s[...]], axis=0)
        cdz_s[...] = dz[0:HALO]
        dy = jnp.zeros((t, C), F32)
        shifted = {}
        for k in range(CONF_KERNEL):
            m = CONF_KERNEL - 1 - k
            r, off = m % 8, m - m % 8
            if r not in shifted:
                shifted[r] = _up(ext, r)
            dy = dy + cw_ref[k:k + 1, :] * shifted[r][off:off + t]
        dp_ref[:, 0:C] = (dy * sgate).astype(BF16)
        dp_ref[:, C:2 * C] = (dy * val * sgate * (1.0 - sgate)).astype(BF16)

        sb, sc, sx = _piece(p_ref, 2), _piece(p_ref, 3), _piece(p_ref, 4)
        q = sc * sx
        q_halo = ph_ref[:, 3 * C:4 * C] * ph_ref[:, 4 * C:5 * C] * first
        ext = jnp.concatenate([q_halo, q], axis=0)
        q1, q2 = _down(ext, 1)[HALO:], _down(ext, 2)[HALO:]
        cz = vec_ref[8:9, :] * q + vec_ref[7:8, :] * q1 + vec_ref[6:7, :] * q2
        dcz = d_b * sb
        dp_ref[:, 2 * C:3 * C] = (d_b * cz).astype(BF16)
        acc_vec(8, dcz * q)
        acc_vec(7, dcz * q1)
        acc_vec(6, dcz * q2)
        ext = jnp.concatenate([dcz, cdc_s[...]], axis=0)
        cdc_s[...] = dcz[0:HALO]
        dq = vec_ref[8:9, :] * dcz + vec_ref[7:8, :] * _up(ext, 1)[0:t] + vec_ref[6:7, :] * _up(ext, 2)[0:t]
        dp_ref[:, 3 * C:4 * C] = (dq * sx).astype(BF16)
        dp_ref[:, 4 * C:5 * C] = (dq * sc).astype(BF16)

        xp = _piece(p_ref, 5)
        ext = jnp.concatenate([ph_ref[:, 5 * C:6 * C] * first, xp], axis=0)
        cnt = _pool_count(grp, tile * t, t)
        dd = (_trailing_sums(ext, grp, t) / cnt - xp).astype(BF16)
        e2 = _dot(dd, pw_ref[...])
        acc_vec(3, d_c * e2)
        de = (d_c * vec_ref[3:4, :]).astype(BF16)
        dpool_ref[...] += _dot(_t_bf16(dd.astype(F32)), de)
        ddd = _dot_nt(de, pw_ref[...])
        fq = ddd / cnt
        ext = jnp.concatenate([fq, cf_s[...]], axis=0)
        cf_s[...] = fq[0:HALO]
        dp_ref[:, 5 * C:6 * C] = (_leading_sums(ext, grp, t) - ddd).astype(BF16)

        vn, vrs, vlnf = _ln_fwd(_piece(p_ref, 7), vec_ref[4:5, :], vec_ref[5:6, :])
        vln = vlnf.astype(BF16)
        for n in range(t // CHUNK):
            rows = slice(n * CHUNK, (n + 1) * CHUNK)
            mixed = _head_select(_dot(ws_ref[...], vln[rows]), grp) + bias_ref[...]
            dd_n = d_d[rows]
            dp_ref[rows, 6 * C:7 * C] = (dd_n * mixed).astype(BF16)
            dmx = dd_n * p_ref[rows, 6 * C:7 * C]
            dbias_s[...] += dmx
            dmx_b = dmx.astype(BF16)
            dvl_s[rows, :] = _head_select(_dot(wst_ref[...], dmx_b), grp)
            for h in range(N_HEADS):
                hrows = slice(h * CHUNK, (h + 1) * CHUNK)
                dws_ref[hrows, :] += _dot_nt(jnp.where(grp == h, dmx_b, jnp.zeros_like(dmx_b)), vln[rows])
        dvl = dvl_s[...]
        dv, dg, db = _ln_bwd(dvl, vn, vrs, vec_ref[4:5, :])
        dvec_ref[4:5, :] += dg
        dvec_ref[5:6, :] += db
        dp_ref[:, 7 * C:8 * C] = dv.astype(BF16)

        @pl.when(i == nt - 1)
        def _():
            dws_ref[...] = dws_ref[...] * tril_ref[...]
            dbs_ref[...] = lax.dot_general(hr_ref[...], dbias_s[...], NT, precision=lax.Precision.HIGHEST,
                                           preferred_element_type=F32)

    full = lambda a: pl.BlockSpec(a.shape, lambda i: (0, 0))
    acc = lambda shape: pl.BlockSpec(shape, lambda i: (0, 0))
    f = p.shape[1]
    return pl.pallas_call(
        body, name="mixer_bwd", grid=(nt,),
        in_specs=[pl.BlockSpec((t, d), lambda i: (nt - 1 - i, 0)), pl.BlockSpec((t, f), lambda i: (nt - 1 - i, 0)),
                  pl.BlockSpec((HALO, f), lambda i: (jnp.maximum((nt - 1 - i) * hb - 1, 0), 0)),
                  pl.BlockSpec((t, C), lambda i: (nt - 1 - i, 0)), full(cw), full(vec), full(pool_w), full(wstack), full(wstack_t), full(bias), full(tril4), full(head_rows),
                  full(w_out)],
        out_specs=[pl.BlockSpec((t, f), lambda i: (nt - 1 - i, 0)), acc((32, C)), acc((16, C)), acc((C, C)),
                   acc((N_HEADS * CHUNK, CHUNK)), acc((8, CHUNK))],
        out_shape=[jax.ShapeDtypeStruct((s, f), BF16), jax.ShapeDtypeStruct((32, C), F32), jax.ShapeDtypeStruct((16, C), F32),
                   jax.ShapeDtypeStruct((C, C), F32), jax.ShapeDtypeStruct((N_HEADS * CHUNK, CHUNK), F32),
                   jax.ShapeDtypeStruct((8, CHUNK), F32)],
        scratch_shapes=[pltpu.VMEM((HALO, C), F32)] * 3 + [pltpu.VMEM((8, n_ext, C), F32), pltpu.VMEM((t, C), F32),
                                                            pltpu.VMEM((CHUNK, C), F32)],
        compiler_params=_params("arbitrary"),
    )(dx2, p, p, z, cw, vec, pool_w, wstack, wstack_t, bias, tril4, head_rows, w_out)


def _loss_bwd(x, g, target):
    s, d = x.shape
    tm = _tile(s, 512)

    def body(x_ref, g_ref, t_ref, dx_ref, dg_ref, loss_ref):
        @pl.when(pl.program_id(0) == 0)
        def _():
            dg_ref[...] = jnp.zeros_like(dg_ref)
            loss_ref[...] = jnp.zeros_like(loss_ref)

        n, r, y = _rms(x_ref[...], g_ref[...])
        err = y - t_ref[...]
        loss_ref[...] += 0.5 * jnp.sum(jnp.mean(err * err, axis=-1, keepdims=True), axis=0, keepdims=True)
        dxr, dg = _rms_bwd(err * (1.0 / d), n, r, g_ref[...])
        dx_ref[...] = dxr
        dg_ref[0:1, :] += dg

    row = pl.BlockSpec((tm, d), lambda i: (i, 0))
    return pl.pallas_call(
        body, name="loss_bwd", grid=(s // tm,),
        in_specs=[row, pl.BlockSpec((1, d), lambda i: (0, 0)), row],
        out_specs=[row, pl.BlockSpec((8, d), lambda i: (0, 0)), pl.BlockSpec((8, LANES), lambda i: (0, 0))],
        out_shape=[jax.ShapeDtypeStruct((s, d), F32), jax.ShapeDtypeStruct((8, d), F32), jax.ShapeDtypeStruct((8, LANES), F32)],
        compiler_params=_params("arbitrary"),
    )(x, g, target)


def _adamw_math(w, g, m, v):
    m = ADAM_B1 * m + (1.0 - ADAM_B1) * g
    v = ADAM_B2 * v + (1.0 - ADAM_B2) * (g * g)
    m_hat = m / (1.0 - ADAM_B1 ** ADAM_STEP)
    v_hat = v / (1.0 - ADAM_B2 ** ADAM_STEP)
    return -ADAM_LR * (m_hat / (jnp.sqrt(v_hat) + ADAM_EPS) + ADAM_WD * w), m, v


def _adamw(w, g, m, v):
    r, c = w.shape
    tr = r // 8 if r % 64 == 0 else r

    def body(w_ref, g_ref, m_ref, v_ref, d_ref, mo_ref, vo_ref, go_ref):
        gv = g_ref[...]
        d_ref[...], mo_ref[...], vo_ref[...] = _adamw_math(w_ref[...], gv, m_ref[...], v_ref[...])
        go_ref[...] = gv

    blk = pl.BlockSpec((tr, c), lambda i: (i, 0))
    return pl.pallas_call(
        body, name="adamw", grid=(r // tr,), in_specs=[blk] * 4, out_specs=[blk] * 4,
        out_shape=[jax.ShapeDtypeStruct((r, c), F32)] * 4, compiler_params=_params("arbitrary"),
    )(w, g, m, v)


def _adamw_small(ws, gs, ms, vs):
    n = len(ws)

    def body(*refs):
        ins, outs = refs[:4 * n], refs[4 * n:]
        for k in range(n):
            dl, mo, vo = _adamw_math(ins[k][...], ins[n + k][...], ins[2 * n + k][...], ins[3 * n + k][...])
            outs[k][...], outs[n + k][...], outs[2 * n + k][...] = dl, mo, vo

    vm = pl.BlockSpec(memory_space=pltpu.VMEM)
    out = pl.pallas_call(
        body, name="adamw_small", in_specs=[vm] * (4 * n), out_specs=[vm] * (3 * n),
        out_shape=[jax.ShapeDtypeStruct(a.shape, F32) for a in ws] * 3,
        compiler_params=pltpu.CompilerParams(vmem_limit_bytes=VMEM_LIMIT),
    )(*ws, *gs, *ms, *vs)
    return out[:n], out[n:2 * n], out[2 * n:]


def _where_am_i():
    x, y, c = lax.axis_index("x"), lax.axis_index("y"), lax.axis_index("c")
    chips = [(1 - x, y), (x, 1 - y), (1 - x, 1 - y)]
    return x, y, c, chips


def _chip_id(chip):
    return 2 * chip[0] + chip[1]


def _landing_shapes(shards):
    return [jax.ShapeDtypeStruct((3, a.shape[0] // 2, a.shape[1]), a.dtype) for a in shards]


def _ici_gather_copies(ins, lands, send_sems, recv_sems):
    _, _, c, chips = _where_am_i()
    copies = []
    for a, src in enumerate(ins):
        hr = src.shape[0] // 2
        for j, chip in enumerate(chips):
            copies.append(pltpu.make_async_remote_copy(
                src_ref=src.at[pl.ds(c * hr, hr)], dst_ref=lands[a].at[j], send_sem=send_sems.at[3 * a + j],
                recv_sem=recv_sems.at[3 * a + j], device_id=(*chip, c), device_id_type=MESH))
    return copies


def _ici_reduce_copies(s_ref, o_ref, send_sems, recv_sems, local_sem):
    x, y, c, chips = _where_am_i()
    copies = [pltpu.make_async_remote_copy(src_ref=s_ref.at[_chip_id(chip)], dst_ref=o_ref.at[j], send_sem=send_sems.at[j],
                                           recv_sem=recv_sems.at[j], device_id=(*chip, c), device_id_type=MESH)
              for j, chip in enumerate(chips)]
    return copies, pltpu.make_async_copy(s_ref.at[_chip_id((x, y))], o_ref.at[3], local_sem)


def _wait_all(copies):
    for cp in copies:
        cp.wait_recv()
    for cp in copies:
        cp.wait_send()


def _ici_gather(shards):
    n = len(shards)

    def body(*refs):
        copies = _ici_gather_copies(refs[:n], refs[n:2 * n], refs[2 * n], refs[2 * n + 1])
        for cp in copies:
            cp.start()
        _wait_all(copies)

    return pl.pallas_call(
        body, name="ici_gather", in_specs=[ANY] * n, out_specs=[ANY] * n, out_shape=_landing_shapes(shards),
        scratch_shapes=[pltpu.SemaphoreType.DMA((3 * n,))] * 2,
    )(*shards)


SLOTS = 4


def _sibling_gather(shards, lands):
    n = len(shards)
    d = shards[0].shape[1]
    halves = [a.shape[0] // 2 for a in shards]
    hmax = max(halves)
    chunks = [(a, j, halves[a]) for a in range(n) for j in range(3)]
    nc = len(chunks)

    def body(*refs):
        ins, lnd, outs = refs[:n], refs[n:2 * n], refs[2 * n:3 * n]
        sbuf, rbuf, obuf, ld_sems, take_sems, send_sems, recv_sems, place_sems, own_ld_sems, own_st_sems, credits = refs[3 * n:]
        x, y, c, chips = _where_am_i()
        sibling = (x, y, 1 - c)
        me = _chip_id((x, y))

        def load(i):
            a, j, hr = chunks[i]
            return pltpu.make_async_copy(lnd[a].at[j], sbuf.at[i % SLOTS, pl.ds(0, hr)], ld_sems.at[i % SLOTS])

        def push(i):
            hr, slot = chunks[i][2], i % SLOTS
            return pltpu.make_async_remote_copy(src_ref=sbuf.at[slot, pl.ds(0, hr)], dst_ref=rbuf.at[slot, pl.ds(0, hr)],
                                                send_sem=send_sems.at[slot], recv_sem=recv_sems.at[slot], device_id=sibling,
                                                device_id_type=MESH)

        def take(i):
            a, j, hr = chunks[i]
            return pltpu.make_async_copy(rbuf.at[i % SLOTS, pl.ds(0, hr)],
                                         outs[a].at[_chip_id(chips[j]), pl.ds((1 - c) * hr, hr)], take_sems.at[i % SLOTS])

        def place(i):
            a, j, hr = chunks[i]
            return pltpu.make_async_copy(sbuf.at[i % SLOTS, pl.ds(0, hr)], outs[a].at[_chip_id(chips[j]), pl.ds(c * hr, hr)],
                                         place_sems.at[i % SLOTS])

        own = [(a, h, halves[a]) for a in range(n) for h in range(2)]

        def own_load(k):
            a, h, hr = own[k]
            return pltpu.make_async_copy(ins[a].at[pl.ds(h * hr, hr)], obuf.at[k % 2, pl.ds(0, hr)], own_ld_sems.at[k % 2])

        def own_store(k):
            a, h, hr = own[k]
            return pltpu.make_async_copy(obuf.at[k % 2, pl.ds(0, hr)], outs[a].at[me, pl.ds(h * hr, hr)], own_st_sems.at[k % 2])

        def own_step(k):
            if k < len(own):
                if k >= 2:
                    own_store(k - 2).wait()
                own_load(k).start()
            if 1 <= k <= len(own):
                own_load(k - 1).wait()
                own_store(k - 1).start()

        for i in range(min(2, nc)):
            load(i).start()
        for i in range(nc):
            own_step(i)
            if i >= 2:
                push(i - 2).wait_send()
                place(i - 2).wait()
            if i + 2 < nc:
                load(i + 2).start()
            load(i).wait()
            place(i).start()
            if i >= SLOTS:
                pl.semaphore_wait(credits.at[i % SLOTS], 1)
            push(i).start()
            if i >= 1:
                push(i - 1).wait_recv()
                take(i - 1).start()
            if i >= 2:
                take(i - 2).wait()
                if i - 2 + SLOTS < nc:
                    pl.semaphore_signal(credits.at[(i - 2) % SLOTS], inc=1, device_id=sibling, device_id_type=MESH)
        for i in range(max(0, nc - 2), nc):
            push(i).wait_send()
            place(i).wait()
        push(nc - 1).wait_recv()
        take(nc - 1).start()
        for i in range(max(0, nc - 2), nc):
            take(i).wait()
        for k in range(nc, len(own) + 1):
            own_step(k)
        for k in range(max(0, len(own) - 2), len(own)):
            own_store(k).wait()

    dma = pltpu.SemaphoreType.DMA((SLOTS,))
    dma2 = pltpu.SemaphoreType.DMA((2,))
    return pl.pallas_call(
        body, name="sibling_gather", in_specs=[ANY] * (2 * n), out_specs=[ANY] * n,
        out_shape=[jax.ShapeDtypeStruct((N_CHIPS,) + a.shape, a.dtype) for a in shards],
        scratch_shapes=[pltpu.VMEM((SLOTS, hmax, d), BF16), pltpu.VMEM((SLOTS, hmax, d), BF16), pltpu.VMEM((2, hmax, d), BF16),
                        dma, dma, dma, dma, dma, dma2, dma2, pltpu.SemaphoreType.REGULAR((SLOTS,))],
        compiler_params=pltpu.CompilerParams(vmem_limit_bytes=VMEM_LIMIT),
    )(*shards, *lands)


def _pair_sum(grads):
    n = len(grads)
    halves = [g.shape[1] // 2 for g in grads]
    total, rows = sum(halves), max(halves)
    d = grads[0].shape[2]
    chunks, off = [], 0
    for a in range(n):
        chunks += [(a, k, off, halves[a]) for k in range(N_CHIPS)]
        off += halves[a]
    nc = len(chunks)

    def body(*refs):
        ins, out_ref = refs[:n], refs[n]
        sbuf, rbuf, mbuf, obuf, ls_sems, lm_sems, st_sems, send_sems, recv_sems, credits = refs[n + 1:]
        x, y, c, _ = _where_am_i()
        sibling = (x, y, 1 - c)

        def load_theirs(i):
            a, k, _, hr = chunks[i]
            return pltpu.make_async_copy(ins[a].at[k, pl.ds((1 - c) * hr, hr)], sbuf.at[i % SLOTS, pl.ds(0, hr)], ls_sems.at[i % SLOTS])

        def load_mine(i):
            a, k, _, hr = chunks[i]
            return pltpu.make_async_copy(ins[a].at[k, pl.ds(c * hr, hr)], mbuf.at[i % SLOTS, pl.ds(0, hr)], lm_sems.at[i % SLOTS])

        def push(i):
            hr, slot = chunks[i][3], i % SLOTS
            return pltpu.make_async_remote_copy(src_ref=sbuf.at[slot, pl.ds(0, hr)], dst_ref=rbuf.at[slot, pl.ds(0, hr)],
                                                send_sem=send_sems.at[slot], recv_sem=recv_sems.at[slot], device_id=sibling,
                                                device_id_type=MESH)

        def store(i):
            _, k, o, hr = chunks[i]
            slot = i % SLOTS
            return pltpu.make_async_copy(obuf.at[slot, pl.ds(0, hr)], out_ref.at[k, pl.ds(o, hr)], st_sems.at[slot])

        def start_push(i):
            load_theirs(i).wait()
            if i >= SLOTS:
                pl.semaphore_wait(credits.at[i % SLOTS], 1)
            push(i).start()

        for i in range(min(2, nc)):
            load_theirs(i).start()
            load_mine(i).start()
        start_push(0)
        for i in range(nc):
            hr, slot = chunks[i][3], i % SLOTS
            if i + 2 < nc:
                load_theirs(i + 2).start()
                load_mine(i + 2).start()
            if i + 1 < nc:
                start_push(i + 1)
            push(i).wait_recv()
            push(i).wait_send()
            load_mine(i).wait()
            if i >= SLOTS:
                store(i - SLOTS).wait()
            obuf[slot, 0:hr, :] = (mbuf[slot, 0:hr, :].astype(F32) + rbuf[slot, 0:hr, :].astype(F32)).astype(BF16)
            if i + SLOTS < nc:
                pl.semaphore_signal(credits.at[slot], inc=1, device_id=sibling, device_id_type=MESH)
            store(i).start()
        for i in range(max(0, nc - SLOTS), nc):
            store(i).wait()

    stage = pltpu.VMEM((SLOTS, rows, d), BF16)
    dma = pltpu.SemaphoreType.DMA((SLOTS,))
    return pl.pallas_call(
        body, name="pair_sum", in_specs=[ANY] * n, out_specs=ANY, out_shape=jax.ShapeDtypeStruct((N_CHIPS, total, d), BF16),
        scratch_shapes=[stage, stage, stage, stage, dma, dma, dma, dma, dma, pltpu.SemaphoreType.REGULAR((SLOTS,))],
        compiler_params=pltpu.CompilerParams(vmem_limit_bytes=VMEM_LIMIT),
    )(*grads)


def _reduce_chips(sums):
    def body(s_ref, o_ref, send_sems, recv_sems, local_sem):
        copies, local = _ici_reduce_copies(s_ref, o_ref, send_sems, recv_sems, local_sem)
        local.start()
        for cp in copies:
            cp.start()
        _wait_all(copies)
        local.wait()

    return pl.pallas_call(
        body, name="reduce_chips", in_specs=[ANY], out_specs=ANY, out_shape=jax.ShapeDtypeStruct(sums.shape, BF16),
        scratch_shapes=[pltpu.SemaphoreType.DMA((3,)), pltpu.SemaphoreType.DMA((3,)), pltpu.SemaphoreType.DMA],
    )(sums)


def _sum_share(parts, groups, rows):
    layers, n, n_g = len(parts), len(rows), len(groups)
    parts = [part for layer in parts for part in layer]
    n_l = len(parts)
    d = parts[0].shape[2]
    halves = [r // 2 for r in rows]
    hmax = max(halves)
    chunks = []
    for l in range(layers):
        for g, members in enumerate(groups):
            off = 0
            for a in members:
                chunks.append((l * n_g + g, a, off, halves[a], l))
                off += halves[a]
    nc = len(chunks)

    def body(*refs):
        ins, outs = refs[:n_l], refs[n_l:n_l + n]
        pbuf, obuf, rbuf, ld_sems, keep_sems, take_sems, send_sems, recv_sems, credits = refs[n_l + n:]
        x, y, c, _ = _where_am_i()
        sibling = (x, y, 1 - c)

        def load(i):
            part, _, off, hr, _ = chunks[i]
            return pltpu.make_async_copy(ins[part].at[:, pl.ds(off, hr)], pbuf.at[i % SLOTS, :, pl.ds(0, hr)], ld_sems.at[i % SLOTS])

        def keep(i):
            _, a, _, hr, l = chunks[i]
            return pltpu.make_async_copy(obuf.at[i % SLOTS, pl.ds(0, hr)], outs[a].at[l, pl.ds(c * hr, hr)], keep_sems.at[i % SLOTS])

        def push(i):
            hr, slot = chunks[i][3], i % SLOTS
            return pltpu.make_async_remote_copy(src_ref=obuf.at[slot, pl.ds(0, hr)], dst_ref=rbuf.at[slot, pl.ds(0, hr)],
                                                send_sem=send_sems.at[slot], recv_sem=recv_sems.at[slot], device_id=sibling,
                                                device_id_type=MESH)

        def take(i):
            _, a, _, hr, l = chunks[i]
            return pltpu.make_async_copy(rbuf.at[i % SLOTS, pl.ds(0, hr)], outs[a].at[l, pl.ds((1 - c) * hr, hr)],
                                         take_sems.at[i % SLOTS])

        for i in range(min(2, nc)):
            load(i).start()
        for i in range(nc):
            hr, slot = chunks[i][3], i % SLOTS
            if i + 2 < nc:
                load(i + 2).start()
            load(i).wait()
            if i >= SLOTS:
                keep(i - SLOTS).wait()
                push(i - SLOTS).wait_send()
            part = lambda k: pbuf[slot, k, 0:hr, :].astype(F32)
            obuf[slot, 0:hr, :] = ((part(3) + part(0)) + part(1)) + part(2)
            keep(i).start()
            if i >= SLOTS:
                pl.semaphore_wait(credits.at[slot], 1)
            push(i).start()
            if i >= 1:
                push(i - 1).wait_recv()
                take(i - 1).start()
            if i >= 2:
                take(i - 2).wait()
                if i - 2 + SLOTS < nc:
                    pl.semaphore_signal(credits.at[(i - 2) % SLOTS], inc=1, device_id=sibling, device_id_type=MESH)
        push(nc - 1).wait_recv()
        take(nc - 1).start()
        for i in range(max(0, nc - 2), nc):
            take(i).wait()
        for i in range(max(0, nc - SLOTS), nc):
            keep(i).wait()
            push(i).wait_send()

    dma = pltpu.SemaphoreType.DMA((SLOTS,))
    return pl.pallas_call(
        body, name="sum_share", in_specs=[ANY] * n_l, out_specs=[ANY] * n,
        out_shape=[jax.ShapeDtypeStruct((layers, r, d), F32) for r in rows],
        scratch_shapes=[pltpu.VMEM((SLOTS, N_CHIPS, hmax, d), BF16), pltpu.VMEM((SLOTS, hmax, d), F32),
                        pltpu.VMEM((SLOTS, hmax, d), F32), dma, dma, dma, dma, dma, pltpu.SemaphoreType.REGULAR((SLOTS,))],
        compiler_params=pltpu.CompilerParams(vmem_limit_bytes=VMEM_LIMIT),
    )(*parts)


def _all_gather_small(block, reduce):
    m, n = block.shape

    def body(x_ref, out_ref, *scratch):
        if reduce:
            all_ref, send_sems, recv_sems, local_sem = scratch
        else:
            all_ref = out_ref
            send_sems, recv_sems, local_sem = scratch
        x, y, c, chips = _where_am_i()
        me, sibling = (x, y, c), (x, y, 1 - c)

        def rows(px, py, pc):
            return all_ref.at[pl.ds((4 * px + 2 * py + pc) * m, m), :]

        def copy(k, blk, to, src=None):
            return pltpu.make_async_remote_copy(src_ref=rows(*blk) if src is None else src, dst_ref=rows(*blk),
                                                send_sem=send_sems.at[k], recv_sem=recv_sems.at[k], device_id=to,
                                                device_id_type=MESH)

        mine = pltpu.make_async_copy(x_ref, rows(*me), local_sem)
        mine.start()
        first = [copy(0, me, sibling, src=x_ref)]
        first += [copy(1 + j, me, (*chip, c), src=x_ref) for j, chip in enumerate(chips)]
        for cp in first:
            cp.start()
        passed = [copy(4 + j, (*chip, c), sibling) for j, chip in enumerate(chips)]
        for j, chip in enumerate(chips):
            copy(1 + j, (*chip, c), me).wait_recv()
            passed[j].start()
        copy(0, sibling, me).wait_recv()
        for j, chip in enumerate(chips):
            copy(4 + j, (*chip, 1 - c), me).wait_recv()
        for cp in first + passed:
            cp.wait_send()
        mine.wait()
        if reduce:
            total = all_ref[0:m, :]
            for dev in range(1, N_DEV):
                total = total + all_ref[dev * m:(dev + 1) * m, :]
            out_ref[...] = total

    vm = pl.BlockSpec(memory_space=pltpu.VMEM)
    sems = [pltpu.SemaphoreType.DMA((7,)), pltpu.SemaphoreType.DMA((7,)), pltpu.SemaphoreType.DMA]
    return pl.pallas_call(
        body, name="reduce_small" if reduce else "gather_small", in_specs=[vm], out_specs=vm,
        out_shape=jax.ShapeDtypeStruct((m, n) if reduce else (N_DEV * m, n), F32),
        scratch_shapes=([pltpu.VMEM((N_DEV * m, n), F32)] if reduce else []) + sems,
        compiler_params=pltpu.CompilerParams(vmem_limit_bytes=VMEM_LIMIT),
    )(block)


def _pack(arrays):
    flat = jnp.concatenate([a.reshape(-1) for a in arrays])
    pad = (-flat.shape[0]) % (8 * LANES)
    return jnp.pad(flat, (0, pad)).reshape(-1, LANES)


def _unpack(buf, shapes):
    flat = buf.reshape(-1)
    out, off = [], 0
    for shp in shapes:
        size = 1
        for dim in shp:
            size *= dim
        out.append(flat[off:off + size].reshape(shp))
        off += size
    return out


BIG = ("ffn1_w1", "ffn1_w3", "ffn1_w2", "w_in", "w_out", "ffn2_w1", "ffn2_w3", "ffn2_w2")
TRANSPOSED = ("ffn1_w1", "ffn1_w3", "w_in", "ffn2_w1", "ffn2_w3")
SMALL = ("ffn1_norm", "mix_norm", "conf_conv_w", "conf_conv_b", "conf_ln_g", "conf_ln_b", "sconv_w", "pool_w", "pool_scale",
         "gmlp_ln_g", "gmlp_ln_b", "gmlp_w_s", "gmlp_b_s", "ffn2_norm", "final_norm")
ORDER = ("ffn1_norm", "ffn1_w1", "ffn1_w3", "ffn1_w2", "mix_norm", "w_in", "conf_conv_w", "conf_conv_b", "conf_ln_g", "conf_ln_b",
         "sconv_w", "pool_w", "pool_scale", "gmlp_ln_g", "gmlp_ln_b", "gmlp_w_s", "gmlp_b_s", "w_out", "ffn2_norm", "ffn2_w1",
         "ffn2_w3", "ffn2_w2", "final_norm")


def _as2d(a):
    return a.reshape(-1, a.shape[-1])


def kernel(x, ffn1_norm, ffn1_w1, ffn1_w3, ffn1_w2, mix_norm, w_in, conf_conv_w, conf_conv_b, conf_ln_g, conf_ln_b, sconv_w, pool_w, pool_scale, gmlp_ln_g, gmlp_ln_b, gmlp_w_s, gmlp_b_s, w_out, ffn2_norm, ffn2_w1, ffn2_w3, ffn2_w2, final_norm, loss_target, m_ffn1_norm, m_ffn1_w1, m_ffn1_w3, m_ffn1_w2, m_mix_norm, m_w_in, m_conf_conv_w, m_conf_conv_b, m_conf_ln_g, m_conf_ln_b, m_sconv_w, m_pool_w, m_pool_scale, m_gmlp_ln_g, m_gmlp_ln_b, m_gmlp_w_s, m_gmlp_b_s, m_w_out, m_ffn2_norm, m_ffn2_w1, m_ffn2_w3, m_ffn2_w2, m_final_norm, v_ffn1_norm, v_ffn1_w1, v_ffn1_w3, v_ffn1_w2, v_mix_norm, v_w_in, v_conf_conv_w, v_conf_conv_b, v_conf_ln_g, v_conf_ln_b, v_sconv_w, v_pool_w, v_pool_scale, v_gmlp_ln_g, v_gmlp_ln_b, v_gmlp_w_s, v_gmlp_b_s, v_w_out, v_ffn2_norm, v_ffn2_w1, v_ffn2_w3, v_ffn2_w2, v_final_norm):
    given = dict(locals())
    w = {k: given[k] for k in ORDER}
    mom = {k: given["m_" + k] for k in ORDER}
    var = {k: given["v_" + k] for k in ORDER}
    n_l = ffn1_w1.shape[0]
    xs = x[0]
    d = xs.shape[1]
    chip = 2 * lax.axis_index("x") + lax.axis_index("y")

    def shard(name, l):
        a = w[name][l]
        return (jnp.swapaxes(w[name], 1, 2)[l] if name in TRANSPOSED else a).astype(BF16)

    groups = (BIG[:3], BIG[3:])

    def shards_of(l, g):
        return [shard(name, l) for name in groups[g]] if l < n_l else []

    def finish_gather(l, g, lands):
        out = _sibling_gather(shards_of(l, g), lands)
        return {name: a.reshape(-1, d) for name, a in zip(groups[g], out)}

    shard_rows = [w[name].shape[2] if name in TRANSPOSED else w[name].shape[1] for name in BIG]

    conv_shapes = [conf_conv_w.shape, sconv_w.shape]
    conv_all = _all_gather_small(_pack([conf_conv_w, sconv_w]), reduce=False)
    conv_all = conv_all.reshape(N_CHIPS, 2, -1)[:, 0]
    conf_full, sconv_full = [jnp.concatenate([_unpack(conv_all[k], conv_shapes)[a] for k in range(N_CHIPS)], axis=-1)
                             for a in range(2)]

    lane = jnp.arange(C) // HEAD_DIM
    head_rows = (jnp.arange(8)[:, None] == lane[None, :]).astype(F32)
    tril = jnp.tril(jnp.ones((CHUNK, CHUNK), F32))
    tril4 = jnp.tile(tril, (N_HEADS, 1))
    mixer_consts = []
    for l in range(n_l):
        cw = jnp.pad(conf_full[l], ((0, 32 - CONF_KERNEL), (0, 0)))
        vec = jnp.concatenate([conf_conv_b[l][None], conf_ln_g[l][None], conf_ln_b[l][None], pool_scale[l][None],
                               gmlp_ln_g[l][None], gmlp_ln_b[l][None], sconv_full[l], jnp.zeros((7, C), F32)], axis=0)
        eye = jnp.eye(len(pool_w[l]), dtype=F32)
        pool_blk = (eye[:, None, :, None] * pool_w[l][:, :, None, :]).reshape(C, C).astype(BF16)
        ws = gmlp_w_s[l] * tril[None]
        wstack = ws.reshape(N_HEADS * CHUNK, CHUNK).astype(BF16)
        wstack_t = jnp.swapaxes(ws, 1, 2).reshape(N_HEADS * CHUNK, CHUNK).astype(BF16)
        bias = jnp.repeat(gmlp_b_s[l].T, HEAD_DIM, axis=1)
        mixer_consts.append((cw, vec, pool_blk, wstack, wstack_t, bias))

    saved = []
    cur = xs
    gathered = [finish_gather(0, 0, _ici_gather(shards_of(0, 0)))]
    for l in range(n_l):
        gw = gathered[l]
        cw, vec, pool_blk, wstack, wstack_t, bias = mixer_consts[l]
        x0 = cur
        x1, a1, b1, h1, lands = _ffn_fwd(x0, ffn1_norm[l][None], gw["ffn1_w1"], gw["ffn1_w3"], gw["ffn1_w2"], shards_of(l, 1))
        gw.update(finish_gather(l, 1, lands))
        p = _proj_fwd(x1, mix_norm[l][None], gw["w_in"])
        x2, mix, z = _mixer_fwd(p, x1, cw, vec, pool_blk, wstack, bias, gw["w_out"])
        x3, a2, b2, h2, lands = _ffn_fwd(x2, ffn2_norm[l][None], gw["ffn2_w1"], gw["ffn2_w3"], gw["ffn2_w2"], shards_of(l + 1, 0))
        if l + 1 < n_l:
            gathered.append(finish_gather(l + 1, 0, lands))
        saved.append((x0, x1, x2, a1, b1, a2, b2, p, mix, z, h1, h2))
        cur = x3

    dx, dg_final, loss_part = _loss_bwd(cur, final_norm[None], loss_target[0])

    small_parts = [None] * n_l
    reduced_halves = [[None, None] for _ in range(n_l)]
    pair_sum = lambda big, g: _pair_sum([big[name].reshape(N_CHIPS, -1, d) for name in groups[g]])
    pending = None
    for l in reversed(range(n_l)):
        gw = gathered[l]
        cw, vec, pool_blk, wstack, wstack_t, bias = mixer_consts[l]
        x0, x1, x2, a1, b1, a2, b2, p, mix, z, h1, h2 = saved[l]
        big = {}
        out = _ffn_bwd(dx, x2, ffn2_norm[l][None], a2, b2, gw["ffn2_w1"], gw["ffn2_w3"], gw["ffn2_w2"], pending)
        dx, dg_ffn2, dy, da, db, u, dx_bf = out[:7]
        if pending is not None:
            reduced_halves[l + 1][0] = out[7]
        big["ffn2_w1"], big["ffn2_w3"], big["ffn2_w2"] = _dw(da, h2), _dw(db, h2), _dw(u, dy)
        big["w_out"] = _dw(mix, dx_bf)
        dp, dcw, dvec, dpool, dws, dbs = _mixer_bwd(dx, p, z, cw, vec, pool_blk, wstack, wstack_t, bias, tril4, head_rows, gw["w_out"])
        dx, dg_mix, h = _proj_bwd(dx, x1, mix_norm[l][None], dp, gw["w_in"])
        big["w_in"] = _dw(dp, h)
        second = pair_sum(big, 1)
        out = _ffn_bwd(dx, x0, ffn1_norm[l][None], a1, b1, gw["ffn1_w1"], gw["ffn1_w3"], gw["ffn1_w2"], second)
        dx, dg_ffn1, dy, da, db, u = out[:6]
        reduced_halves[l][1] = out[7]
        big["ffn1_w1"], big["ffn1_w3"], big["ffn1_w2"] = _dw(da, h1), _dw(db, h1), _dw(u, dy)
        small_parts[l] = [dg_ffn1[0], dg_mix[0], dg_ffn2[0], dcw, dvec, dpool, dws, dbs]
        pending = pair_sum(big, 0)
    reduced_halves[0][0] = _reduce_chips(pending)
    grad_x = dx[None]

    index_of = {name: a for a, name in enumerate(BIG)}
    full = dict(zip(BIG, _sum_share(reduced_halves, [[index_of[name] for name in g] for g in groups], shard_rows)))
    grad = {name: (jnp.swapaxes(full[name], 1, 2) if name in TRANSPOSED else full[name]) for name in BIG}

    part_shapes = [a.shape for a in small_parts[0]]
    tail = [dg_final[0], loss_part[0]]
    packed = _pack([a for l in range(n_l) for a in small_parts[l]] + tail)
    summed = _unpack(_all_gather_small(packed, reduce=True), part_shapes * n_l + [a.shape for a in tail])
    per_layer = [summed[l * len(part_shapes):(l + 1) * len(part_shapes)] for l in range(n_l)]
    stack = lambda k: jnp.stack([per_layer[l][k] for l in range(n_l)])
    dcw_all, dvec_all, dpool_all, dws_all, dbs_all = stack(3), stack(4), stack(5), stack(6), stack(7)
    loss = summed[-1][0]
    chip_cols = lambda a: lax.dynamic_slice_in_dim(a, chip * (C // N_CHIPS), C // N_CHIPS, axis=2)
    n_pool = pool_w.shape[1]
    grad.update(
        ffn1_norm=stack(0), mix_norm=stack(1), ffn2_norm=stack(2), final_norm=summed[-2],
        conf_conv_w=chip_cols(dcw_all[:, :CONF_KERNEL]), conf_conv_b=dvec_all[:, 0], conf_ln_g=dvec_all[:, 1],
        conf_ln_b=dvec_all[:, 2], pool_scale=dvec_all[:, 3], gmlp_ln_g=dvec_all[:, 4], gmlp_ln_b=dvec_all[:, 5],
        sconv_w=chip_cols(dvec_all[:, 6:6 + SHORT_KERNEL]),
        pool_w=jnp.stack([dpool_all[:, g * POOL_GROUP:(g + 1) * POOL_GROUP, g * POOL_GROUP:(g + 1) * POOL_GROUP]
                          for g in range(n_pool)], axis=1),
        gmlp_w_s=dws_all.reshape(n_l, N_HEADS, CHUNK, CHUNK), gmlp_b_s=dbs_all[:, :N_HEADS],
    )

    delta, new_m, new_v = {}, {}, {}
    for name in BIG:
        rows_of = (lambda a: jnp.swapaxes(a, 1, 2)) if name in TRANSPOSED else (lambda a: a)
        shp = full[name].shape
        out = _adamw(_as2d(rows_of(w[name])), _as2d(full[name]), _as2d(rows_of(mom[name])), _as2d(rows_of(var[name])))
        delta[name], new_m[name], new_v[name], grad[name] = (rows_of(o.reshape(shp)) for o in out)
    ds, ms, vs = _adamw_small([_as2d(w[k]) if w[k].ndim > 1 else w[k][None] for k in SMALL],
                              [_as2d(grad[k]) if grad[k].ndim > 1 else grad[k][None] for k in SMALL],
                              [_as2d(mom[k]) if mom[k].ndim > 1 else mom[k][None] for k in SMALL],
                              [_as2d(var[k]) if var[k].ndim > 1 else var[k][None] for k in SMALL])
    for k, dl, mo, vo in zip(SMALL, ds, ms, vs):
        delta[k], new_m[k], new_v[k] = dl.reshape(w[k].shape), mo.reshape(w[k].shape), vo.reshape(w[k].shape)

    return (loss, grad_x, *[grad[k] for k in ORDER], *[delta[k] for k in ORDER], *[new_m[k] for k in ORDER],
            *[new_v[k] for k in ORDER])
```

```python
import functools

import jax
import jax.numpy as jnp
from jax import lax
from jax.experimental import pallas as pl
from jax.experimental.pallas import tpu as pltpu

F32 = jnp.float32
BF16 = jnp.bfloat16
MESH = pl.DeviceIdType.MESH
ANY = pl.BlockSpec(memory_space=pl.ANY)

EPS = 1e-6
FFN_RESIDUAL = 0.5
D_GROUP = 256
CONF_KERNEL = 31
SHORT_KERNEL = 3
POOL_GROUP = 64
CHUNK = 128
N_HEADS = 4
HEAD_DIM = 64
HALO = 32
N_CHIPS = 4
N_DEV = 8
LANES = 128
MXU_TILE = 256
VMEM_LIMIT = 56 * 2**20
T_MIX_FWD = 512
T_MIX_BWD = 256

ADAM_LR = 0.001
ADAM_B1 = 0.9
ADAM_B2 = 0.999
ADAM_EPS = 1e-08
ADAM_WD = 0.01
ADAM_STEP = 10

NT = (((1,), (1,)), ((), ()))
TN = (((0,), (0,)), ((), ()))


def _params(*sem):
    return pltpu.CompilerParams(dimension_semantics=sem, vmem_limit_bytes=VMEM_LIMIT)


def _dot(a, b):
    return jnp.dot(a, b, preferred_element_type=F32)


def _dot_nt(a, b):
    return lax.dot_general(a, b, NT, preferred_element_type=F32)


def _t_bf16(v):
    return jnp.transpose(v).astype(BF16)


def _sigmoid(v):
    return 1.0 / (1.0 + jnp.exp(-v))


def _rms(x, g):
    r = lax.rsqrt(jnp.mean(x * x, axis=-1, keepdims=True) + EPS)
    n = x * r
    return n, r, n * g


def _rms_bwd(dh, n, r, g):
    dn = dh * g
    dx = r * (dn - n * jnp.mean(dn * n, axis=-1, keepdims=True))
    return dx, jnp.sum(dh * n, axis=0, keepdims=True)


def _ln_fwd(z, g, b):
    mu = jnp.mean(z, axis=-1, keepdims=True)
    zc = z - mu
    rs = lax.rsqrt(jnp.mean(zc * zc, axis=-1, keepdims=True) + EPS)
    zn = zc * rs
    return zn, rs, zn * g + b


def _ln_bwd(dl, zn, rs, g):
    dzn = dl * g
    dz = rs * (dzn - jnp.mean(dzn, axis=-1, keepdims=True) - zn * jnp.mean(dzn * zn, axis=-1, keepdims=True))
    return dz, jnp.sum(dl * zn, axis=0, keepdims=True), jnp.sum(dl, axis=0, keepdims=True)


def _tile(n, want):
    return want if n % want == 0 else n


def _resident(shape):
    return pl.BlockSpec(shape, lambda i: (0,) * len(shape), pipeline_mode=pl.Buffered(1))


def _ffn_fwd(x, g, w1t, w3t, w2, shards=()):
    s, d = x.shape
    f = w1t.shape[0]
    tm, tf = _tile(s, 512), MXU_TILE
    nj = f // tf
    ni = s // tm
    n_c = len(shards)

    def body(*refs):
        x_ref, g_ref, w1_ref, w3_ref, w2_ref = refs[:5]
        xo_ref, a_ref, b_ref, h_ref = refs[5 + n_c:9 + n_c]
        u_s = refs[9 + 2 * n_c]
        if n_c:
            gather = lambda: _ici_gather_copies(refs[5:5 + n_c], refs[9 + n_c:9 + 2 * n_c], refs[10 + 2 * n_c], refs[11 + 2 * n_c])

            @pl.when(pl.program_id(0) == 0)
            def _():
                for cp in gather():
                    cp.start()

        h = _rms(x_ref[...], g_ref[...])[2].astype(BF16)
        h_ref[...] = h
        for j in range(nj):
            cols = slice(j * tf, (j + 1) * tf)
            a = _dot_nt(h, w1_ref[cols, :])
            b = _dot_nt(h, w3_ref[cols, :])
            a_ref[:, cols] = a.astype(BF16)
            b_ref[:, cols] = b.astype(BF16)
            u_s[:, cols] = ((a * _sigmoid(a)) * b).astype(BF16)
        xo_ref[...] = x_ref[...] + FFN_RESIDUAL * _dot(u_s[...], w2_ref[...])
        if n_c:
            @pl.when(pl.program_id(0) == ni - 1)
            def _():
                _wait_all(gather())

    row = pl.BlockSpec((tm, d), lambda i: (i, 0))
    hid = pl.BlockSpec((tm, f), lambda i: (i, 0))
    sems = [pltpu.SemaphoreType.DMA((3 * n_c,))] * 2 if n_c else []
    out = pl.pallas_call(
        body, name="ffn_fwd_gather" if n_c else "ffn_fwd", grid=(ni,),
        in_specs=[row, _resident(g.shape), _resident(w1t.shape), _resident(w3t.shape), _resident(w2.shape)] + [ANY] * n_c,
        out_specs=[row, hid, hid, row] + [ANY] * n_c,
        out_shape=[jax.ShapeDtypeStruct((s, d), F32), jax.ShapeDtypeStruct((s, f), BF16), jax.ShapeDtypeStruct((s, f), BF16),
                   jax.ShapeDtypeStruct((s, d), BF16)] + _landing_shapes(shards),
        scratch_shapes=[pltpu.VMEM((tm, f), BF16)] + sems,
        compiler_params=_params("arbitrary"),
    )(x, g, w1t, w3t, w2, *shards)
    return out[0], out[1], out[2], out[3], list(out[4:])


def _ffn_bwd(dxo, x, g, a, b, w1t, w3t, w2, raw=()):
    s, d = x.shape
    f = w1t.shape[0]
    tm, tf = _tile(s, 256), MXU_TILE
    nj = f // tf
    ni = s // tm
    n_c = len(raw)
    if n_c:
        chunks, total, rows = _pair_chunks(raw)
        per_step = max(1, -(-len(chunks) // max(1, ni // 3)))
        summed_at = -(-len(chunks) // per_step)
        assert summed_at <= ni - 1

    def body(*refs):
        dxo_ref, x_ref, g_ref, a_ref, b_ref, w1_ref, w3_ref, w2_ref = refs[:8]
        dx_ref, dg_ref, dy_ref, da_ref, db_ref, u_ref, dxb_ref = refs[8 + n_c:15 + n_c]
        step_id = pl.program_id(0)
        if n_c:
            sums_ref, parts_ref = refs[15 + n_c:17 + n_c]
            scratch = refs[17 + n_c:]
            first, step, last = _pair_sum_steps(refs[8:8 + n_c], sums_ref, scratch[:10], chunks)
            exchange = lambda: _ici_reduce_copies(sums_ref, parts_ref, *scratch[10:])

        @pl.when(step_id == 0)
        def _():
            dg_ref[...] = jnp.zeros_like(dg_ref)
            if n_c:
                first()

        if n_c:
            for i in range(len(chunks)):
                @pl.when(step_id == i // per_step)
                def _(i=i):
                    step(i)

            @pl.when(step_id == summed_at)
            def _():
                last()
                copies, local = exchange()
                local.start()
                for cp in copies:
                    cp.start()

        dy = (FFN_RESIDUAL * dxo_ref[...]).astype(BF16)
        dy_ref[...] = dy
        for j in range(nj):
            cols = slice(j * tf, (j + 1) * tf)
            du = _dot_nt(dy, w2_ref[cols, :])
            av = a_ref[:, cols].astype(F32)
            bv = b_ref[:, cols].astype(F32)
            sg = _sigmoid(av)
            sl = av * sg
            da = du * bv * (sg * (1.0 + av * (1.0 - sg)))
            db = du * sl
            da_ref[:, cols] = da.astype(BF16)
            db_ref[:, cols] = db.astype(BF16)
            u_ref[:, cols] = (sl * bv).astype(BF16)
        dh = _dot(da_ref[...], w1_ref[...]) + _dot(db_ref[...], w3_ref[...])
        n, r, _ = _rms(x_ref[...], g_ref[...])
        dxr, dg = _rms_bwd(dh, n, r, g_ref[...])
        dx = dxo_ref[...] + dxr
        dx_ref[...] = dx
        dxb_ref[...] = dx.astype(BF16)
        dg_ref[0:1, :] += dg
        if n_c:
            @pl.when(step_id == ni - 1)
            def _():
                copies, local = exchange()
                _wait_all(copies)
                local.wait()

    row = pl.BlockSpec((tm, d), lambda i: (i, 0))
    hid = pl.BlockSpec((tm, f), lambda i: (i, 0))
    carried, scratch = [], []
    if n_c:
        carried = [jax.ShapeDtypeStruct((N_CHIPS, total, raw[0].shape[2]), BF16)] * 2
        scratch = _pair_scratch(rows, raw[0].shape[2]) + [pltpu.SemaphoreType.DMA((3,)), pltpu.SemaphoreType.DMA((3,)),
                                                          pltpu.SemaphoreType.DMA]
    out = pl.pallas_call(
        body, name="ffn_bwd_reduce" if n_c else "ffn_bwd", grid=(ni,),
        in_specs=[row, row, _resident(g.shape), hid, hid, _resident(w1t.shape), _resident(w3t.shape), _resident(w2.shape)]
        + [ANY] * n_c,
        out_specs=[row, pl.BlockSpec((8, d), lambda i: (0, 0)), row, hid, hid, hid, row] + [ANY] * len(carried),
        out_shape=[jax.ShapeDtypeStruct((s, d), F32), jax.ShapeDtypeStruct((8, d), F32), jax.ShapeDtypeStruct((s, d), BF16),
                   jax.ShapeDtypeStruct((s, f), BF16), jax.ShapeDtypeStruct((s, f), BF16), jax.ShapeDtypeStruct((s, f), BF16),
                   jax.ShapeDtypeStruct((s, d), BF16)] + carried,
        scratch_shapes=scratch,
        compiler_params=_params("arbitrary"),
    )(dxo, x, g, a, b, w1t, w3t, w2, *raw)
    return out


def _dw(am, bm):
    s, r = am.shape
    n = bm.shape[1]
    rb = r // 2 if r > 1024 and (r // 2) % LANES == 0 else r
    ts = _tile(s, 2048)
    ni = s // ts

    def body(am_ref, bm_ref, o_ref, acc_s):
        i = pl.program_id(1)

        @pl.when(i == 0)
        def _():
            acc_s[...] = jnp.zeros_like(acc_s)

        acc_s[...] += lax.dot_general(am_ref[...], bm_ref[...], TN, preferred_element_type=F32)

        @pl.when(i == ni - 1)
        def _():
            o_ref[...] = acc_s[...].astype(BF16)

    return pl.pallas_call(
        body, name="dw", grid=(r // rb, ni),
        in_specs=[pl.BlockSpec((ts, rb), lambda k, i: (i, k)), pl.BlockSpec((ts, n), lambda k, i: (i, 0))],
        out_specs=pl.BlockSpec((rb, n), lambda k, i: (k, 0)),
        out_shape=jax.ShapeDtypeStruct((r, n), BF16),
        scratch_shapes=[pltpu.VMEM((rb, n), F32)],
        compiler_params=_params("arbitrary", "arbitrary"),
    )(am, bm)


def _proj_fwd(x, g, w_int):
    s, d = x.shape
    f = w_int.shape[0]
    tm = _tile(s, 512)

    def body(x_ref, g_ref, w_ref, p_ref):
        p_ref[...] = _dot_nt(_rms(x_ref[...], g_ref[...])[2].astype(BF16), w_ref[...])

    return pl.pallas_call(
        body, name="proj_fwd", grid=(s // tm,),
        in_specs=[pl.BlockSpec((tm, d), lambda i: (i, 0)), _resident(g.shape), _resident(w_int.shape)],
        out_specs=pl.BlockSpec((tm, f), lambda i: (i, 0)),
        out_shape=jax.ShapeDtypeStruct((s, f), F32),
        compiler_params=_params("arbitrary"),
    )(x, g, w_int)


def _proj_bwd(dxo, x, g, dp, w_int):
    s, d = x.shape
    f = w_int.shape[0]
    tm = _tile(s, 512)

    def body(dxo_ref, x_ref, g_ref, dp_ref, w_ref, dx_ref, dg_ref, h_ref):
        @pl.when(pl.program_id(0) == 0)
        def _():
            dg_ref[...] = jnp.zeros_like(dg_ref)

        n, r, h = _rms(x_ref[...], g_ref[...])
        h_ref[...] = h.astype(BF16)
        dxr, dg = _rms_bwd(_dot(dp_ref[...], w_ref[...]), n, r, g_ref[...])
        dx_ref[...] = dxo_ref[...] + dxr
        dg_ref[0:1, :] += dg

    row = pl.BlockSpec((tm, d), lambda i: (i, 0))
    return pl.pallas_call(
        body, name="proj_bwd", grid=(s // tm,),
        in_specs=[row, row, _resident(g.shape), pl.BlockSpec((tm, f), lambda i: (i, 0)), _resident(w_int.shape)],
        out_specs=[row, pl.BlockSpec((8, d), lambda i: (0, 0)), row],
        out_shape=[jax.ShapeDtypeStruct((s, d), F32), jax.ShapeDtypeStruct((8, d), F32), jax.ShapeDtypeStruct((s, d), BF16)],
        compiler_params=_params("arbitrary"),
    )(dxo, x, g, dp, w_int)


C = D_GROUP


def _piece(ref, k):
    return ref[:, k * C:(k + 1) * C]


def _up(v, r):
    return v if r == 0 else pltpu.roll(v, v.shape[0] - r, 0)


def _down(v, r):
    return v if r == 0 else pltpu.roll(v, r, 0)


def _lane_group():
    lane = lax.broadcasted_iota(jnp.int32, (1, C), 1)
    return (lane >= POOL_GROUP).astype(jnp.int32) + (lane >= 2 * POOL_GROUP).astype(jnp.int32) + (
        lane >= 3 * POOL_GROUP).astype(jnp.int32)


def _by_group(grp, v2, v4, v8, v16):
    return jnp.where(grp == 0, v2, jnp.where(grp == 1, v4, jnp.where(grp == 2, v8, v16)))


def _pool_count(grp, row0, t):
    pos = (row0 + lax.broadcasted_iota(jnp.int32, (t, C), 0) + 1).astype(F32)
    return jnp.minimum(pos, _by_group(grp, 2.0, 4.0, 8.0, 16.0))


def _trailing_sums(ext, grp, t):
    s2 = ext + _down(ext, 1)
    s4 = s2 + _down(s2, 2)
    s8 = s4 + _down(s4, 4)
    s16 = s8 + _down(s8, 8)
    return _by_group(grp, s2, s4, s8, s16)[HALO:HALO + t]


def _leading_sums(ext, grp, t):
    s2 = ext + _up(ext, 1)
    s4 = s2 + _up(s2, 2)
    s8 = s4 + _up(s4, 4)
    s16 = s8 + _up(s8, 8)
    return _by_group(grp, s2, s4, s8, s16)[0:t]


def _head_select(r4, grp):
    assert HEAD_DIM == POOL_GROUP and N_HEADS == 4
    return _by_group(grp, *(r4[h * CHUNK:(h + 1) * CHUNK] for h in range(N_HEADS)))


def _conv_taps():
    return [(k, (k + 2) % 8, (k + 2) - (k + 2) % 8) for k in range(CONF_KERNEL)]


def _mixer_fwd(p, x1, cw, vec, pool_w, wstack, bias, w_out):
    s, d = x1.shape
    t = _tile(s, T_MIX_FWD)
    n_ext = t + HALO
    dm = w_out.shape[0]

    def body(p_ref, x1_ref, cw_ref, vec_ref, pw_ref, ws_ref, bias_ref, wo_ref, x2_ref, mix_s, z_ref, cy_s, cq_s, cx_s):
        i = pl.program_id(0)

        @pl.when(i == 0)
        def _():
            cy_s[...] = jnp.zeros_like(cy_s)
            cq_s[...] = jnp.zeros_like(cq_s)
            cx_s[...] = jnp.zeros_like(cx_s)

        grp = _lane_group()
        y = _piece(p_ref, 0) * _sigmoid(_piece(p_ref, 1))
        ext = jnp.concatenate([cy_s[...], y], axis=0)
        cy_s[...] = y[t - HALO:t]
        z = jnp.broadcast_to(vec_ref[0:1, :], (t, C))
        shifted = {}
        for k, r, off in _conv_taps():
            if r not in shifted:
                shifted[r] = _up(ext, r)
            z = z + cw_ref[k:k + 1, :] * shifted[r][off:off + t]
        z_ref[...] = z
        ln = _ln_fwd(z, vec_ref[1:2, :], vec_ref[2:3, :])[2]
        mix_s[:, 0:C] = (ln * _sigmoid(ln)).astype(BF16)
        q = _piece(p_ref, 3) * _piece(p_ref, 4)
        ext = jnp.concatenate([cq_s[...], q], axis=0)
        cq_s[...] = q[t - HALO:t]
        cz = vec_ref[8:9, :] * q + vec_ref[7:8, :] * _down(ext, 1)[HALO:] + vec_ref[6:7, :] * _down(ext, 2)[HALO:]
        mix_s[:, C:2 * C] = (_piece(p_ref, 2) * cz).astype(BF16)
        xp = _piece(p_ref, 5)
        ext = jnp.concatenate([cx_s[...], xp], axis=0)
        cx_s[...] = xp[t - HALO:t]
        dd = _trailing_sums(ext, grp, t) / _pool_count(grp, i * t, t) - xp
        mix_s[:, 2 * C:3 * C] = (_dot(dd.astype(BF16), pw_ref[...]) * vec_ref[3:4, :]).astype(BF16)
        vln = _ln_fwd(_piece(p_ref, 7), vec_ref[4:5, :], vec_ref[5:6, :])[2].astype(BF16)
        for n in range(t // CHUNK):
            rows = slice(n * CHUNK, (n + 1) * CHUNK)
            mixed = _head_select(_dot(ws_ref[...], vln[rows]), grp) + bias_ref[...]
            mix_s[rows, 3 * C:4 * C] = (p_ref[rows, 6 * C:7 * C] * mixed).astype(BF16)
        x2_ref[...] = x1_ref[...] + _dot(mix_s[...], wo_ref[...])

    full = lambda a: pl.BlockSpec(a.shape, lambda i: (0, 0))
    return pl.pallas_call(
        body, name="mixer_fwd", grid=(s // t,),
        in_specs=[pl.BlockSpec((t, p.shape[1]), lambda i: (i, 0)), pl.BlockSpec((t, d), lambda i: (i, 0)),
                  full(cw), full(vec), full(pool_w), full(wstack), full(bias), full(w_out)],
        out_specs=[pl.BlockSpec((t, d), lambda i: (i, 0)), pl.BlockSpec((t, dm), lambda i: (i, 0)),
                   pl.BlockSpec((t, C), lambda i: (i, 0))],
        out_shape=[jax.ShapeDtypeStruct((s, d), F32), jax.ShapeDtypeStruct((s, dm), BF16), jax.ShapeDtypeStruct((s, C), F32)],
        scratch_shapes=[pltpu.VMEM((HALO, C), F32)] * 3,
        compiler_params=_params("arbitrary"),
    )(p, x1, cw, vec, pool_w, wstack, bias, w_out)


def _mixer_bwd(dx2, p, z, cw, vec, pool_w, wstack, wstack_t, bias, tril4, head_rows, w_out):
    s, d = dx2.shape
    t = _tile(s, T_MIX_BWD)
    nt = s // t
    n_ext = t + HALO
    hb = t // HALO

    def body(dx2_ref, p_ref, ph_ref, z_ref, cw_ref, vec_ref, pw_ref, ws_ref, wst_ref, bias_ref, tril_ref, hr_ref, wo_ref,
             dp_ref, dcw_ref, dvec_ref, dpool_ref, dws_ref, dbs_ref, cdz_s, cdc_s, cf_s, vy_s, dvl_s, dbias_s):
        i = pl.program_id(0)
        tile = nt - 1 - i

        @pl.when(i == 0)
        def _():
            for ref in (cdz_s, cdc_s, cf_s, dbias_s, dcw_ref, dvec_ref, dpool_ref, dws_ref, dbs_ref):
                ref[...] = jnp.zeros_like(ref)

        grp = _lane_group()
        first = jnp.where(tile > 0, 1.0, 0.0)
        dmix = _dot_nt(dx2_ref[...].astype(BF16), wo_ref[...])
        d_a, d_b, d_c, d_d = (dmix[:, k * C:(k + 1) * C] for k in range(4))

        def acc_vec(row, v):
            dvec_ref[row:row + 1, :] += jnp.sum(v, axis=0, keepdims=True)

        val, gate = _piece(p_ref, 0), _piece(p_ref, 1)
        sgate = _sigmoid(gate)
        y = val * sgate
        y_halo = ph_ref[:, 0:C] * _sigmoid(ph_ref[:, C:2 * C]) * first
        ext = jnp.concatenate([y_halo, y], axis=0)
        for r in range(8):
            vy_s[r] = _up(ext, r)
        zn, rs, ln = _ln_fwd(z_ref[...], vec_ref[1:2, :], vec_ref[2:3, :])
        sg = _sigmoid(ln)
        dln = d_a * (sg * (1.0 + ln * (1.0 - sg)))
        dz, dg, db = _ln_bwd(dln, zn, rs, vec_ref[1:2, :])
        dvec_ref[1:2, :] += dg
        dvec_ref[2:3, :] += db
        acc_vec(0, dz)
        for k, r, off in _conv_taps():
            dcw_ref[k:k + 1, :] += jnp.sum(dz * vy_s[r, off:off + t, :], axis=0, keepdims=True)
        ext = jnp.concatenate([dz, cdz_s[...]], axis=0)
        cdz_s[...] = dz[0:HALO]
        dy = jnp.zeros((t, C), F32)
        shifted = {}
        for k in range(CONF_KERNEL):
            m = CONF_KERNEL - 1 - k
            r, off = m % 8, m - m % 8
            if r not in shifted:
                shifted[r] = _up(ext, r)
            dy = dy + cw_ref[k:k + 1, :] * shifted[r][off:off + t]
        dp_ref[:, 0:C] = (dy * sgate).astype(BF16)
        dp_ref[:, C:2 * C] = (dy * val * sgate * (1.0 - sgate)).astype(BF16)

        sb, sc, sx = _piece(p_ref, 2), _piece(p_ref, 3), _piece(p_ref, 4)
        q = sc * sx
        q_halo = ph_ref[:, 3 * C:4 * C] * ph_ref[:, 4 * C:5 * C] * first
        ext = jnp.concatenate([q_halo, q], axis=0)
        q1, q2 = _down(ext, 1)[HALO:], _down(ext, 2)[HALO:]
        cz = vec_ref[8:9, :] * q + vec_ref[7:8, :] * q1 + vec_ref[6:7, :] * q2
        dcz = d_b * sb
        dp_ref[:, 2 * C:3 * C] = (d_b * cz).astype(BF16)
        acc_vec(8, dcz * q)
        acc_vec(7, dcz * q1)
        acc_vec(6, dcz * q2)
        ext = jnp.concatenate([dcz, cdc_s[...]], axis=0)
        cdc_s[...] = dcz[0:HALO]
        dq = vec_ref[8:9, :] * dcz + vec_ref[7:8, :] * _up(ext, 1)[0:t] + vec_ref[6:7, :] * _up(ext, 2)[0:t]
        dp_ref[:, 3 * C:4 * C] = (dq * sx).astype(BF16)
        dp_ref[:, 4 * C:5 * C] = (dq * sc).astype(BF16)

        xp = _piece(p_ref, 5)
        ext = jnp.concatenate([ph_ref[:, 5 * C:6 * C] * first, xp], axis=0)
        cnt = _pool_count(grp, tile * t, t)
        dd = (_trailing_sums(ext, grp, t) / cnt - xp).astype(BF16)
        e2 = _dot(dd, pw_ref[...])
        acc_vec(3, d_c * e2)
        de = (d_c * vec_ref[3:4, :]).astype(BF16)
        dpool_ref[...] += _dot(_t_bf16(dd.astype(F32)), de)
        ddd = _dot_nt(de, pw_ref[...])
        fq = ddd / cnt
        ext = jnp.concatenate([fq, cf_s[...]], axis=0)
        cf_s[...] = fq[0:HALO]
        dp_ref[:, 5 * C:6 * C] = (_leading_sums(ext, grp, t) - ddd).astype(BF16)

        vn, vrs, vlnf = _ln_fwd(_piece(p_ref, 7), vec_ref[4:5, :], vec_ref[5:6, :])
        vln = vlnf.astype(BF16)
        for n in range(t // CHUNK):
            rows = slice(n * CHUNK, (n + 1) * CHUNK)
            mixed = _head_select(_dot(ws_ref[...], vln[rows]), grp) + bias_ref[...]
            dd_n = d_d[rows]
            dp_ref[rows, 6 * C:7 * C] = (dd_n * mixed).astype(BF16)
            dmx = dd_n * p_ref[rows, 6 * C:7 * C]
            dbias_s[...] += dmx
            dmx_b = dmx.astype(BF16)
            dvl_s[rows, :] = _head_select(_dot(wst_ref[...], dmx_b), grp)
            for h in range(N_HEADS):
                hrows = slice(h * CHUNK, (h + 1) * CHUNK)
                dws_ref[hrows, :] += _dot_nt(jnp.where(grp == h, dmx_b, jnp.zeros_like(dmx_b)), vln[rows])
        dvl = dvl_s[...]
        dv, dg, db = _ln_bwd(dvl, vn, vrs, vec_ref[4:5, :])
        dvec_ref[4:5, :] += dg
        dvec_ref[5:6, :] += db
        dp_ref[:, 7 * C:8 * C] = dv.astype(BF16)

        @pl.when(i == nt - 1)
        def _():
            dws_ref[...] = dws_ref[...] * tril_ref[...]
            dbs_ref[...] = lax.dot_general(hr_ref[...], dbias_s[...], NT, precision=lax.Precision.HIGHEST,
                                           preferred_element_type=F32)

    full = lambda a: pl.BlockSpec(a.shape, lambda i: (0, 0))
    acc = lambda shape: pl.BlockSpec(shape, lambda i: (0, 0))
    f = p.shape[1]
    return pl.pallas_call(
        body, name="mixer_bwd", grid=(nt,),
        in_specs=[pl.BlockSpec((t, d), lambda i: (nt - 1 - i, 0)), pl.BlockSpec((t, f), lambda i: (nt - 1 - i, 0)),
                  pl.BlockSpec((HALO, f), lambda i: (jnp.maximum((nt - 1 - i) * hb - 1, 0), 0)),
                  pl.BlockSpec((t, C), lambda i: (nt - 1 - i, 0)), full(cw), full(vec), full(pool_w), full(wstack), full(wstack_t), full(bias), full(tril4), full(head_rows),
                  full(w_out)],
        out_specs=[pl.BlockSpec((t, f), lambda i: (nt - 1 - i, 0)), acc((32, C)), acc((16, C)), acc((C, C)),
                   acc((N_HEADS * CHUNK, CHUNK)), acc((8, CHUNK))],
        out_shape=[jax.ShapeDtypeStruct((s, f), BF16), jax.ShapeDtypeStruct((32, C), F32), jax.ShapeDtypeStruct((16, C), F32),
                   jax.ShapeDtypeStruct((C, C), F32), jax.ShapeDtypeStruct((N_HEADS * CHUNK, CHUNK), F32),
                   jax.ShapeDtypeStruct((8, CHUNK), F32)],
        scratch_shapes=[pltpu.VMEM((HALO, C), F32)] * 3 + [pltpu.VMEM((8, n_ext, C), F32), pltpu.VMEM((t, C), F32),
                                                            pltpu.VMEM((CHUNK, C), F32)],
        compiler_params=_params("arbitrary"),
    )(dx2, p, p, z, cw, vec, pool_w, wstack, wstack_t, bias, tril4, head_rows, w_out)


def _loss_bwd(x, g, target):
    s, d = x.shape
    tm = _tile(s, 512)

    def body(x_ref, g_ref, t_ref, dx_ref, dg_ref, loss_ref):
        @pl.when(pl.program_id(0) == 0)
        def _():
            dg_ref[...] = jnp.zeros_like(dg_ref)
            loss_ref[...] = jnp.zeros_like(loss_ref)

        n, r, y = _rms(x_ref[...], g_ref[...])
        err = y - t_ref[...]
        loss_ref[...] += 0.5 * jnp.sum(jnp.mean(err * err, axis=-1, keepdims=True), axis=0, keepdims=True)
        dxr, dg = _rms_bwd(err * (1.0 / d), n, r, g_ref[...])
        dx_ref[...] = dxr
        dg_ref[0:1, :] += dg

    row = pl.BlockSpec((tm, d), lambda i: (i, 0))
    return pl.pallas_call(
        body, name="loss_bwd", grid=(s // tm,),
        in_specs=[row, pl.BlockSpec((1, d), lambda i: (0, 0)), row],
        out_specs=[row, pl.BlockSpec((8, d), lambda i: (0, 0)), pl.BlockSpec((8, LANES), lambda i: (0, 0))],
        out_shape=[jax.ShapeDtypeStruct((s, d), F32), jax.ShapeDtypeStruct((8, d), F32), jax.ShapeDtypeStruct((8, LANES), F32)],
        compiler_params=_params("arbitrary"),
    )(x, g, target)


def _adamw_math(w, g, m, v):
    m = ADAM_B1 * m + (1.0 - ADAM_B1) * g
    v = ADAM_B2 * v + (1.0 - ADAM_B2) * (g * g)
    m_hat = m / (1.0 - ADAM_B1 ** ADAM_STEP)
    v_hat = v / (1.0 - ADAM_B2 ** ADAM_STEP)
    return -ADAM_LR * (m_hat / (jnp.sqrt(v_hat) + ADAM_EPS) + ADAM_WD * w), m, v


def _adamw(w, g, m, v):
    r, c = w.shape
    tr = r // 8 if r % 64 == 0 else r

    def body(w_ref, g_ref, m_ref, v_ref, d_ref, mo_ref, vo_ref, go_ref):
        gv = g_ref[...]
        d_ref[...], mo_ref[...], vo_ref[...] = _adamw_math(w_ref[...], gv, m_ref[...], v_ref[...])
        go_ref[...] = gv

    blk = pl.BlockSpec((tr, c), lambda i: (i, 0))
    return pl.pallas_call(
        body, name="adamw", grid=(r // tr,), in_specs=[blk] * 4, out_specs=[blk] * 4,
        out_shape=[jax.ShapeDtypeStruct((r, c), F32)] * 4, compiler_params=_params("arbitrary"),
    )(w, g, m, v)


def _adamw_small(ws, gs, ms, vs):
    n = len(ws)

    def body(*refs):
        ins, outs = refs[:4 * n], refs[4 * n:]
        for k in range(n):
            dl, mo, vo = _adamw_math(ins[k][...], ins[n + k][...], ins[2 * n + k][...], ins[3 * n + k][...])
            outs[k][...], outs[n + k][...], outs[2 * n + k][...] = dl, mo, vo

    vm = pl.BlockSpec(memory_space=pltpu.VMEM)
    out = pl.pallas_call(
        body, name="adamw_small", in_specs=[vm] * (4 * n), out_specs=[vm] * (3 * n),
        out_shape=[jax.ShapeDtypeStruct(a.shape, F32) for a in ws] * 3,
        compiler_params=pltpu.CompilerParams(vmem_limit_bytes=VMEM_LIMIT),
    )(*ws, *gs, *ms, *vs)
    return out[:n], out[n:2 * n], out[2 * n:]


def _where_am_i():
    x, y, c = lax.axis_index("x"), lax.axis_index("y"), lax.axis_index("c")
    chips = [(1 - x, y), (x, 1 - y), (1 - x, 1 - y)]
    return x, y, c, chips


def _chip_id(chip):
    return 2 * chip[0] + chip[1]


def _landing_shapes(shards):
    return [jax.ShapeDtypeStruct((3, a.shape[0] // 2, a.shape[1]), a.dtype) for a in shards]


def _ici_gather_copies(ins, lands, send_sems, recv_sems):
    _, _, c, chips = _where_am_i()
    copies = []
    for a, src in enumerate(ins):
        hr = src.shape[0] // 2
        for j, chip in enumerate(chips):
            copies.append(pltpu.make_async_remote_copy(
                src_ref=src.at[pl.ds(c * hr, hr)], dst_ref=lands[a].at[j], send_sem=send_sems.at[3 * a + j],
                recv_sem=recv_sems.at[3 * a + j], device_id=(*chip, c), device_id_type=MESH))
    return copies


def _ici_reduce_copies(s_ref, o_ref, send_sems, recv_sems, local_sem):
    x, y, c, chips = _where_am_i()
    copies = [pltpu.make_async_remote_copy(src_ref=s_ref.at[_chip_id(chip)], dst_ref=o_ref.at[j], send_sem=send_sems.at[j],
                                           recv_sem=recv_sems.at[j], device_id=(*chip, c), device_id_type=MESH)
              for j, chip in enumerate(chips)]
    return copies, pltpu.make_async_copy(s_ref.at[_chip_id((x, y))], o_ref.at[3], local_sem)


def _wait_all(copies):
    for cp in copies:
        cp.wait_recv()
    for cp in copies:
        cp.wait_send()


def _ici_gather(shards):
    n = len(shards)

    def body(*refs):
        copies = _ici_gather_copies(refs[:n], refs[n:2 * n], refs[2 * n], refs[2 * n + 1])
        for cp in copies:
            cp.start()
        _wait_all(copies)

    return pl.pallas_call(
        body, name="ici_gather", in_specs=[ANY] * n, out_specs=[ANY] * n, out_shape=_landing_shapes(shards),
        scratch_shapes=[pltpu.SemaphoreType.DMA((3 * n,))] * 2,
    )(*shards)


SLOTS = 4


def _sibling_gather(shards, lands):
    n = len(shards)
    d = shards[0].shape[1]
    halves = [a.shape[0] // 2 for a in shards]
    hmax = max(halves)
    chunks = [(a, j, halves[a]) for a in range(n) for j in range(3)]
    nc = len(chunks)

    def body(*refs):
        ins, lnd, outs = refs[:n], refs[n:2 * n], refs[2 * n:3 * n]
        sbuf, rbuf, obuf, ld_sems, take_sems, send_sems, recv_sems, place_sems, own_ld_sems, own_st_sems, credits = refs[3 * n:]
        x, y, c, chips = _where_am_i()
        sibling = (x, y, 1 - c)
        me = _chip_id((x, y))

        def load(i):
            a, j, hr = chunks[i]
            return pltpu.make_async_copy(lnd[a].at[j], sbuf.at[i % SLOTS, pl.ds(0, hr)], ld_sems.at[i % SLOTS])

        def push(i):
            hr, slot = chunks[i][2], i % SLOTS
            return pltpu.make_async_remote_copy(src_ref=sbuf.at[slot, pl.ds(0, hr)], dst_ref=rbuf.at[slot, pl.ds(0, hr)],
                                                send_sem=send_sems.at[slot], recv_sem=recv_sems.at[slot], device_id=sibling,
                                                device_id_type=MESH)

        def take(i):
            a, j, hr = chunks[i]
            return pltpu.make_async_copy(rbuf.at[i % SLOTS, pl.ds(0, hr)],
                                         outs[a].at[_chip_id(chips[j]), pl.ds((1 - c) * hr, hr)], take_sems.at[i % SLOTS])

        def place(i):
            a, j, hr = chunks[i]
            return pltpu.make_async_copy(sbuf.at[i % SLOTS, pl.ds(0, hr)], outs[a].at[_chip_id(chips[j]), pl.ds(c * hr, hr)],
                                         place_sems.at[i % SLOTS])

        own = [(a, h, halves[a]) for a in range(n) for h in range(2)]

        def own_load(k):
            a, h, hr = own[k]
            return pltpu.make_async_copy(ins[a].at[pl.ds(h * hr, hr)], obuf.at[k % 2, pl.ds(0, hr)], own_ld_sems.at[k % 2])

        def own_store(k):
            a, h, hr = own[k]
            return pltpu.make_async_copy(obuf.at[k % 2, pl.ds(0, hr)], outs[a].at[me, pl.ds(h * hr, hr)], own_st_sems.at[k % 2])

        def own_step(k):
            if k < len(own):
                if k >= 2:
                    own_store(k - 2).wait()
                own_load(k).start()
            if 1 <= k <= len(own):
                own_load(k - 1).wait()
                own_store(k - 1).start()

        for i in range(min(2, nc)):
            load(i).start()
        for i in range(nc):
            own_step(i)
            if i >= 2:
                push(i - 2).wait_send()
                place(i - 2).wait()
            if i + 2 < nc:
                load(i + 2).start()
            load(i).wait()
            place(i).start()
            if i >= SLOTS:
                pl.semaphore_wait(credits.at[i % SLOTS], 1)
            push(i).start()
            if i >= 1:
                push(i - 1).wait_recv()
                take(i - 1).start()
            if i >= 2:
                take(i - 2).wait()
                if i - 2 + SLOTS < nc:
                    pl.semaphore_signal(credits.at[(i - 2) % SLOTS], inc=1, device_id=sibling, device_id_type=MESH)
        for i in range(max(0, nc - 2), nc):
            push(i).wait_send()
            place(i).wait()
        push(nc - 1).wait_recv()
        take(nc - 1).start()
        for i in range(max(0, nc - 2), nc):
            take(i).wait()
        for k in range(nc, len(own) + 1):
            own_step(k)
        for k in range(max(0, len(own) - 2), len(own)):
            own_store(k).wait()

    dma = pltpu.SemaphoreType.DMA((SLOTS,))
    dma2 = pltpu.SemaphoreType.DMA((2,))
    return pl.pallas_call(
        body, name="sibling_gather", in_specs=[ANY] * (2 * n), out_specs=[ANY] * n,
        out_shape=[jax.ShapeDtypeStruct((N_CHIPS,) + a.shape, a.dtype) for a in shards],
        scratch_shapes=[pltpu.VMEM((SLOTS, hmax, d), BF16), pltpu.VMEM((SLOTS, hmax, d), BF16), pltpu.VMEM((2, hmax, d), BF16),
                        dma, dma, dma, dma, dma, dma2, dma2, pltpu.SemaphoreType.REGULAR((SLOTS,))],
        compiler_params=pltpu.CompilerParams(vmem_limit_bytes=VMEM_LIMIT),
    )(*shards, *lands)


def _pair_sum(grads):
    n = len(grads)
    chunks, total, rows = _pair_chunks(grads)

    def body(*refs):
        first, step, last = _pair_sum_steps(refs[:n], refs[n], refs[n + 1:], chunks)
        first()
        for i in range(len(chunks)):
            step(i)
        last()

    return pl.pallas_call(
        body, name="pair_sum", in_specs=[ANY] * n, out_specs=ANY,
        out_shape=jax.ShapeDtypeStruct((N_CHIPS, total, grads[0].shape[2]), BF16),
        scratch_shapes=_pair_scratch(rows, grads[0].shape[2]),
        compiler_params=pltpu.CompilerParams(vmem_limit_bytes=VMEM_LIMIT),
    )(*grads)


def _pair_chunks(grads):
    halves = [g.shape[1] // 2 for g in grads]
    chunks, off = [], 0
    for a in range(len(grads)):
        chunks += [(a, k, off, halves[a]) for k in range(N_CHIPS)]
        off += halves[a]
    return chunks, off, max(halves)


def _pair_scratch(rows, d):
    stage = pltpu.VMEM((SLOTS, rows, d), BF16)
    dma = pltpu.SemaphoreType.DMA((SLOTS,))
    return [stage, stage, stage, stage, dma, dma, dma, dma, dma, pltpu.SemaphoreType.REGULAR((SLOTS,))]


def _pair_sum_steps(ins, out_ref, scratch, chunks):
    if True:
        sbuf, rbuf, mbuf, obuf, ls_sems, lm_sems, st_sems, send_sems, recv_sems, credits = scratch
        nc = len(chunks)
        x, y, c, _ = _where_am_i()
        sibling = (x, y, 1 - c)

        def load_theirs(i):
            a, k, _, hr = chunks[i]
            return pltpu.make_async_copy(ins[a].at[k, pl.ds((1 - c) * hr, hr)], sbuf.at[i % SLOTS, pl.ds(0, hr)], ls_sems.at[i % SLOTS])

        def load_mine(i):
            a, k, _, hr = chunks[i]
            return pltpu.make_async_copy(ins[a].at[k, pl.ds(c * hr, hr)], mbuf.at[i % SLOTS, pl.ds(0, hr)], lm_sems.at[i % SLOTS])

        def push(i):
            hr, slot = chunks[i][3], i % SLOTS
            return pltpu.make_async_remote_copy(src_ref=sbuf.at[slot, pl.ds(0, hr)], dst_ref=rbuf.at[slot, pl.ds(0, hr)],
                                                send_sem=send_sems.at[slot], recv_sem=recv_sems.at[slot], device_id=sibling,
                                                device_id_type=MESH)

        def store(i):
            _, k, o, hr = chunks[i]
            slot = i % SLOTS
            return pltpu.make_async_copy(obuf.at[slot, pl.ds(0, hr)], out_ref.at[k, pl.ds(o, hr)], st_sems.at[slot])

        def start_push(i):
            load_theirs(i).wait()
            if i >= SLOTS:
                pl.semaphore_wait(credits.at[i % SLOTS], 1)
            push(i).start()

        def first():
            for i in range(min(2, nc)):
                load_theirs(i).start()
                load_mine(i).start()
            start_push(0)

        def step(i):
            hr, slot = chunks[i][3], i % SLOTS
            if i + 2 < nc:
                load_theirs(i + 2).start()
                load_mine(i + 2).start()
            if i + 1 < nc:
                start_push(i + 1)
            push(i).wait_recv()
            push(i).wait_send()
            load_mine(i).wait()
            if i >= SLOTS:
                store(i - SLOTS).wait()
            obuf[slot, 0:hr, :] = (mbuf[slot, 0:hr, :].astype(F32) + rbuf[slot, 0:hr, :].astype(F32)).astype(BF16)
            if i + SLOTS < nc:
                pl.semaphore_signal(credits.at[slot], inc=1, device_id=sibling, device_id_type=MESH)
            store(i).start()

        def last():
            for i in range(max(0, nc - SLOTS), nc):
                store(i).wait()

        return first, step, last


def _reduce_chips(sums):
    def body(s_ref, o_ref, send_sems, recv_sems, local_sem):
        copies, local = _ici_reduce_copies(s_ref, o_ref, send_sems, recv_sems, local_sem)
        local.start()
        for cp in copies:
            cp.start()
        _wait_all(copies)
        local.wait()

    return pl.pallas_call(
        body, name="reduce_chips", in_specs=[ANY], out_specs=ANY, out_shape=jax.ShapeDtypeStruct(sums.shape, BF16),
        scratch_shapes=[pltpu.SemaphoreType.DMA((3,)), pltpu.SemaphoreType.DMA((3,)), pltpu.SemaphoreType.DMA],
    )(sums)


def _sum_share(parts, groups, rows):
    layers, n, n_g = len(parts), len(rows), len(groups)
    parts = [part for layer in parts for part in layer]
    n_l = len(parts)
    d = parts[0].shape[2]
    halves = [r // 2 for r in rows]
    hmax = max(halves)
    chunks = []
    for l in range(layers):
        for g, members in enumerate(groups):
            off = 0
            for a in members:
                chunks.append((l * n_g + g, a, off, halves[a], l))
                off += halves[a]
    nc = len(chunks)

    def body(*refs):
        ins, outs = refs[:n_l], refs[n_l:n_l + n]
        pbuf, obuf, rbuf, ld_sems, keep_sems, take_sems, send_sems, recv_sems, credits = refs[n_l + n:]
        x, y, c, _ = _where_am_i()
        sibling = (x, y, 1 - c)

        def load(i):
            part, _, off, hr, _ = chunks[i]
            return pltpu.make_async_copy(ins[part].at[:, pl.ds(off, hr)], pbuf.at[i % SLOTS, :, pl.ds(0, hr)], ld_sems.at[i % SLOTS])

        def keep(i):
            _, a, _, hr, l = chunks[i]
            return pltpu.make_async_copy(obuf.at[i % SLOTS, pl.ds(0, hr)], outs[a].at[l, pl.ds(c * hr, hr)], keep_sems.at[i % SLOTS])

        def push(i):
            hr, slot = chunks[i][3], i % SLOTS
            return pltpu.make_async_remote_copy(src_ref=obuf.at[slot, pl.ds(0, hr)], dst_ref=rbuf.at[slot, pl.ds(0, hr)],
                                                send_sem=send_sems.at[slot], recv_sem=recv_sems.at[slot], device_id=sibling,
                                                device_id_type=MESH)

        def take(i):
            _, a, _, hr, l = chunks[i]
            return pltpu.make_async_copy(rbuf.at[i % SLOTS, pl.ds(0, hr)], outs[a].at[l, pl.ds((1 - c) * hr, hr)],
                                         take_sems.at[i % SLOTS])

        for i in range(min(2, nc)):
            load(i).start()
        for i in range(nc):
            hr, slot = chunks[i][3], i % SLOTS
            if i + 2 < nc:
                load(i + 2).start()
            load(i).wait()
            if i >= SLOTS:
                keep(i - SLOTS).wait()
                push(i - SLOTS).wait_send()
            part = lambda k: pbuf[slot, k, 0:hr, :].astype(F32)
            obuf[slot, 0:hr, :] = ((part(3) + part(0)) + part(1)) + part(2)
            keep(i).start()
            if i >= SLOTS:
                pl.semaphore_wait(credits.at[slot], 1)
            push(i).start()
            if i >= 1:
                push(i - 1).wait_recv()
                take(i - 1).start()
            if i >= 2:
                take(i - 2).wait()
                if i - 2 + SLOTS < nc:
                    pl.semaphore_signal(credits.at[(i - 2) % SLOTS], inc=1, device_id=sibling, device_id_type=MESH)
        push(nc - 1).wait_recv()
        take(nc - 1).start()
        for i in range(max(0, nc - 2), nc):
            take(i).wait()
        for i in range(max(0, nc - SLOTS), nc):
            keep(i).wait()
            push(i).wait_send()

    dma = pltpu.SemaphoreType.DMA((SLOTS,))
    return pl.pallas_call(
        body, name="sum_share", in_specs=[ANY] * n_l, out_specs=[ANY] * n,
        out_shape=[jax.ShapeDtypeStruct((layers, r, d), F32) for r in rows],
        scratch_shapes=[pltpu.VMEM((SLOTS, N_CHIPS, hmax, d), BF16), pltpu.VMEM((SLOTS, hmax, d), F32),
                        pltpu.VMEM((SLOTS, hmax, d), F32), dma, dma, dma, dma, dma, pltpu.SemaphoreType.REGULAR((SLOTS,))],
        compiler_params=pltpu.CompilerParams(vmem_limit_bytes=VMEM_LIMIT),
    )(*parts)


def _all_gather_small(block, reduce):
    m, n = block.shape

    def body(x_ref, out_ref, *scratch):
        if reduce:
            all_ref, send_sems, recv_sems, local_sem = scratch
        else:
            all_ref = out_ref
            send_sems, recv_sems, local_sem = scratch
        x, y, c, chips = _where_am_i()
        me, sibling = (x, y, c), (x, y, 1 - c)

        def rows(px, py, pc):
            return all_ref.at[pl.ds((4 * px + 2 * py + pc) * m, m), :]

        def copy(k, blk, to, src=None):
            return pltpu.make_async_remote_copy(src_ref=rows(*blk) if src is None else src, dst_ref=rows(*blk),
                                                send_sem=send_sems.at[k], recv_sem=recv_sems.at[k], device_id=to,
                                                device_id_type=MESH)

        mine = pltpu.make_async_copy(x_ref, rows(*me), local_sem)
        mine.start()
        first = [copy(0, me, sibling, src=x_ref)]
        first += [copy(1 + j, me, (*chip, c), src=x_ref) for j, chip in enumerate(chips)]
        for cp in first:
            cp.start()
        passed = [copy(4 + j, (*chip, c), sibling) for j, chip in enumerate(chips)]
        for j, chip in enumerate(chips):
            copy(1 + j, (*chip, c), me).wait_recv()
            passed[j].start()
        copy(0, sibling, me).wait_recv()
        for j, chip in enumerate(chips):
            copy(4 + j, (*chip, 1 - c), me).wait_recv()
        for cp in first + passed:
            cp.wait_send()
        mine.wait()
        if reduce:
            total = all_ref[0:m, :]
            for dev in range(1, N_DEV):
                total = total + all_ref[dev * m:(dev + 1) * m, :]
            out_ref[...] = total

    vm = pl.BlockSpec(memory_space=pltpu.VMEM)
    sems = [pltpu.SemaphoreType.DMA((7,)), pltpu.SemaphoreType.DMA((7,)), pltpu.SemaphoreType.DMA]
    return pl.pallas_call(
        body, name="reduce_small" if reduce else "gather_small", in_specs=[vm], out_specs=vm,
        out_shape=jax.ShapeDtypeStruct((m, n) if reduce else (N_DEV * m, n), F32),
        scratch_shapes=([pltpu.VMEM((N_DEV * m, n), F32)] if reduce else []) + sems,
        compiler_params=pltpu.CompilerParams(vmem_limit_bytes=VMEM_LIMIT),
    )(block)


def _pack(arrays):
    flat = jnp.concatenate([a.reshape(-1) for a in arrays])
    pad = (-flat.shape[0]) % (8 * LANES)
    return jnp.pad(flat, (0, pad)).reshape(-1, LANES)


def _unpack(buf, shapes):
    flat = buf.reshape(-1)
    out, off = [], 0
    for shp in shapes:
        size = 1
        for dim in shp:
            size *= dim
        out.append(flat[off:off + size].reshape(shp))
        off += size
    return out


BIG = ("ffn1_w1", "ffn1_w3", "ffn1_w2", "w_in", "w_out", "ffn2_w1", "ffn2_w3", "ffn2_w2")
TRANSPOSED = ("ffn1_w1", "ffn1_w3", "w_in", "ffn2_w1", "ffn2_w3")
SMALL = ("ffn1_norm", "mix_norm", "conf_conv_w", "conf_conv_b", "conf_ln_g", "conf_ln_b", "sconv_w", "pool_w", "pool_scale",
         "gmlp_ln_g", "gmlp_ln_b", "gmlp_w_s", "gmlp_b_s", "ffn2_norm", "final_norm")
ORDER = ("ffn1_norm", "ffn1_w1", "ffn1_w3", "ffn1_w2", "mix_norm", "w_in", "conf_conv_w", "conf_conv_b", "conf_ln_g", "conf_ln_b",
         "sconv_w", "pool_w", "pool_scale", "gmlp_ln_g", "gmlp_ln_b", "gmlp_w_s", "gmlp_b_s", "w_out", "ffn2_norm", "ffn2_w1",
         "ffn2_w3", "ffn2_w2", "final_norm")


def _as2d(a):
    return a.reshape(-1, a.shape[-1])


def kernel(x, ffn1_norm, ffn1_w1, ffn1_w3, ffn1_w2, mix_norm, w_in, conf_conv_w, conf_conv_b, conf_ln_g, conf_ln_b, sconv_w, pool_w, pool_scale, gmlp_ln_g, gmlp_ln_b, gmlp_w_s, gmlp_b_s, w_out, ffn2_norm, ffn2_w1, ffn2_w3, ffn2_w2, final_norm, loss_target, m_ffn1_norm, m_ffn1_w1, m_ffn1_w3, m_ffn1_w2, m_mix_norm, m_w_in, m_conf_conv_w, m_conf_conv_b, m_conf_ln_g, m_conf_ln_b, m_sconv_w, m_pool_w, m_pool_scale, m_gmlp_ln_g, m_gmlp_ln_b, m_gmlp_w_s, m_gmlp_b_s, m_w_out, m_ffn2_norm, m_ffn2_w1, m_ffn2_w3, m_ffn2_w2, m_final_norm, v_ffn1_norm, v_ffn1_w1, v_ffn1_w3, v_ffn1_w2, v_mix_norm, v_w_in, v_conf_conv_w, v_conf_conv_b, v_conf_ln_g, v_conf_ln_b, v_sconv_w, v_pool_w, v_pool_scale, v_gmlp_ln_g, v_gmlp_ln_b, v_gmlp_w_s, v_gmlp_b_s, v_w_out, v_ffn2_norm, v_ffn2_w1, v_ffn2_w3, v_ffn2_w2, v_final_norm):
    given = dict(locals())
    w = {k: given[k] for k in ORDER}
    mom = {k: given["m_" + k] for k in ORDER}
    var = {k: given["v_" + k] for k in ORDER}
    n_l = ffn1_w1.shape[0]
    xs = x[0]
    d = xs.shape[1]
    chip = 2 * lax.axis_index("x") + lax.axis_index("y")

    def shard(name, l):
        a = w[name][l]
        return (jnp.swapaxes(w[name], 1, 2)[l] if name in TRANSPOSED else a).astype(BF16)

    groups = (BIG[:3], BIG[3:])

    def shards_of(l, g):
        return [shard(name, l) for name in groups[g]] if l < n_l else []

    def finish_gather(l, g, lands):
        out = _sibling_gather(shards_of(l, g), lands)
        return {name: a.reshape(-1, d) for name, a in zip(groups[g], out)}

    shard_rows = [w[name].shape[2] if name in TRANSPOSED else w[name].shape[1] for name in BIG]

    conv_shapes = [conf_conv_w.shape, sconv_w.shape]
    conv_all = _all_gather_small(_pack([conf_conv_w, sconv_w]), reduce=False)
    conv_all = conv_all.reshape(N_CHIPS, 2, -1)[:, 0]
    conf_full, sconv_full = [jnp.concatenate([_unpack(conv_all[k], conv_shapes)[a] for k in range(N_CHIPS)], axis=-1)
                             for a in range(2)]

    lane = jnp.arange(C) // HEAD_DIM
    head_rows = (jnp.arange(8)[:, None] == lane[None, :]).astype(F32)
    tril = jnp.tril(jnp.ones((CHUNK, CHUNK), F32))
    tril4 = jnp.tile(tril, (N_HEADS, 1))
    mixer_consts = []
    for l in range(n_l):
        cw = jnp.pad(conf_full[l], ((0, 32 - CONF_KERNEL), (0, 0)))
        vec = jnp.concatenate([conf_conv_b[l][None], conf_ln_g[l][None], conf_ln_b[l][None], pool_scale[l][None],
                               gmlp_ln_g[l][None], gmlp_ln_b[l][None], sconv_full[l], jnp.zeros((7, C), F32)], axis=0)
        eye = jnp.eye(len(pool_w[l]), dtype=F32)
        pool_blk = (eye[:, None, :, None] * pool_w[l][:, :, None, :]).reshape(C, C).astype(BF16)
        ws = gmlp_w_s[l] * tril[None]
        wstack = ws.reshape(N_HEADS * CHUNK, CHUNK).astype(BF16)
        wstack_t = jnp.swapaxes(ws, 1, 2).reshape(N_HEADS * CHUNK, CHUNK).astype(BF16)
        bias = jnp.repeat(gmlp_b_s[l].T, HEAD_DIM, axis=1)
        mixer_consts.append((cw, vec, pool_blk, wstack, wstack_t, bias))

    saved = []
    cur = xs
    gathered = [finish_gather(0, 0, _ici_gather(shards_of(0, 0)))]
    for l in range(n_l):
        gw = gathered[l]
        cw, vec, pool_blk, wstack, wstack_t, bias = mixer_consts[l]
        x0 = cur
        x1, a1, b1, h1, lands = _ffn_fwd(x0, ffn1_norm[l][None], gw["ffn1_w1"], gw["ffn1_w3"], gw["ffn1_w2"], shards_of(l, 1))
        gw.update(finish_gather(l, 1, lands))
        p = _proj_fwd(x1, mix_norm[l][None], gw["w_in"])
        x2, mix, z = _mixer_fwd(p, x1, cw, vec, pool_blk, wstack, bias, gw["w_out"])
        x3, a2, b2, h2, lands = _ffn_fwd(x2, ffn2_norm[l][None], gw["ffn2_w1"], gw["ffn2_w3"], gw["ffn2_w2"], shards_of(l + 1, 0))
        if l + 1 < n_l:
            gathered.append(finish_gather(l + 1, 0, lands))
        saved.append((x0, x1, x2, a1, b1, a2, b2, p, mix, z, h1, h2))
        cur = x3

    dx, dg_final, loss_part = _loss_bwd(cur, final_norm[None], loss_target[0])

    small_parts = [None] * n_l
    reduced_halves = [[None, None] for _ in range(n_l)]
    partials = lambda big, g: [big[name].reshape(N_CHIPS, -1, d) for name in groups[g]]
    pending = []
    for l in reversed(range(n_l)):
        gw = gathered[l]
        cw, vec, pool_blk, wstack, wstack_t, bias = mixer_consts[l]
        x0, x1, x2, a1, b1, a2, b2, p, mix, z, h1, h2 = saved[l]
        big = {}
        out = _ffn_bwd(dx, x2, ffn2_norm[l][None], a2, b2, gw["ffn2_w1"], gw["ffn2_w3"], gw["ffn2_w2"], pending)
        dx, dg_ffn2, dy, da, db, u, dx_bf = out[:7]
        if pending:
            reduced_halves[l + 1][0] = out[8]
        big["ffn2_w1"], big["ffn2_w3"], big["ffn2_w2"] = _dw(da, h2), _dw(db, h2), _dw(u, dy)
        big["w_out"] = _dw(mix, dx_bf)
        dp, dcw, dvec, dpool, dws, dbs = _mixer_bwd(dx, p, z, cw, vec, pool_blk, wstack, wstack_t, bias, tril4, head_rows, gw["w_out"])
        dx, dg_mix, h = _proj_bwd(dx, x1, mix_norm[l][None], dp, gw["w_in"])
        big["w_in"] = _dw(dp, h)
        out = _ffn_bwd(dx, x0, ffn1_norm[l][None], a1, b1, gw["ffn1_w1"], gw["ffn1_w3"], gw["ffn1_w2"], partials(big, 1))
        dx, dg_ffn1, dy, da, db, u = out[:6]
        reduced_halves[l][1] = out[8]
        big["ffn1_w1"], big["ffn1_w3"], big["ffn1_w2"] = _dw(da, h1), _dw(db, h1), _dw(u, dy)
        small_parts[l] = [dg_ffn1[0], dg_mix[0], dg_ffn2[0], dcw, dvec, dpool, dws, dbs]
        pending = partials(big, 0)
    reduced_halves[0][0] = _reduce_chips(_pair_sum(pending))
    grad_x = dx[None]

    index_of = {name: a for a, name in enumerate(BIG)}
    full = dict(zip(BIG, _sum_share(reduced_halves, [[index_of[name] for name in g] for g in groups], shard_rows)))
    grad = {name: (jnp.swapaxes(full[name], 1, 2) if name in TRANSPOSED else full[name]) for name in BIG}

    part_shapes = [a.shape for a in small_parts[0]]
    tail = [dg_final[0], loss_part[0]]
    packed = _pack([a for l in range(n_l) for a in small_parts[l]] + tail)
    summed = _unpack(_all_gather_small(packed, reduce=True), part_shapes * n_l + [a.shape for a in tail])
    per_layer = [summed[l * len(part_shapes):(l + 1) * len(part_shapes)] for l in range(n_l)]
    stack = lambda k: jnp.stack([per_layer[l][k] for l in range(n_l)])
    dcw_all, dvec_all, dpool_all, dws_all, dbs_all = stack(3), stack(4), stack(5), stack(6), stack(7)
    loss = summed[-1][0]
    chip_cols = lambda a: lax.dynamic_slice_in_dim(a, chip * (C // N_CHIPS), C // N_CHIPS, axis=2)
    n_pool = pool_w.shape[1]
    grad.update(
        ffn1_norm=stack(0), mix_norm=stack(1), ffn2_norm=stack(2), final_norm=summed[-2],
        conf_conv_w=chip_cols(dcw_all[:, :CONF_KERNEL]), conf_conv_b=dvec_all[:, 0], conf_ln_g=dvec_all[:, 1],
        conf_ln_b=dvec_all[:, 2], pool_scale=dvec_all[:, 3], gmlp_ln_g=dvec_all[:, 4], gmlp_ln_b=dvec_all[:, 5],
        sconv_w=chip_cols(dvec_all[:, 6:6 + SHORT_KERNEL]),
        pool_w=jnp.stack([dpool_all[:, g * POOL_GROUP:(g + 1) * POOL_GROUP, g * POOL_GROUP:(g + 1) * POOL_GROUP]
                          for g in range(n_pool)], axis=1),
        gmlp_w_s=dws_all.reshape(n_l, N_HEADS, CHUNK, CHUNK), gmlp_b_s=dbs_all[:, :N_HEADS],
    )

    delta, new_m, new_v = {}, {}, {}
    for name in BIG:
        rows_of = (lambda a: jnp.swapaxes(a, 1, 2)) if name in TRANSPOSED else (lambda a: a)
        shp = full[name].shape
        out = _adamw(_as2d(rows_of(w[name])), _as2d(full[name]), _as2d(rows_of(mom[name])), _as2d(rows_of(var[name])))
        delta[name], new_m[name], new_v[name], grad[name] = (rows_of(o.reshape(shp)) for o in out)
    ds, ms, vs = _adamw_small([_as2d(w[k]) if w[k].ndim > 1 else w[k][None] for k in SMALL],
                              [_as2d(grad[k]) if grad[k].ndim > 1 else grad[k][None] for k in SMALL],
                              [_as2d(mom[k]) if mom[k].ndim > 1 else mom[k][None] for k in SMALL],
                              [_as2d(var[k]) if var[k].ndim > 1 else var[k][None] for k in SMALL])
    for k, dl, mo, vo in zip(SMALL, ds, ms, vs):
        delta[k], new_m[k], new_v[k] = dl.reshape(w[k].shape), mo.reshape(w[k].shape), vo.reshape(w[k].shape)

    return (loss, grad_x, *[grad[k] for k in ORDER], *[delta[k] for k in ORDER], *[new_m[k] for k in ORDER],
            *[new_v[k] for k in ORDER])
```

```python
import functools

import jax
import jax.numpy as jnp
from jax import lax
from jax.experimental import pallas as pl
from jax.experimental.pallas import tpu as pltpu

F32 = jnp.float32
BF16 = jnp.bfloat16
MESH = pl.DeviceIdType.MESH
ANY = pl.BlockSpec(memory_space=pl.ANY)

EPS = 1e-6
FFN_RESIDUAL = 0.5
D_GROUP = 256
CONF_KERNEL = 31
SHORT_KERNEL = 3
POOL_GROUP = 64
CHUNK = 128
N_HEADS = 4
HEAD_DIM = 64
HALO = 32
N_CHIPS = 4
N_DEV = 8
LANES = 128
MXU_TILE = 256
VMEM_LIMIT = 56 * 2**20
T_MIX_FWD = 512
T_MIX_BWD = 256

ADAM_LR = 0.001
ADAM_B1 = 0.9
ADAM_B2 = 0.999
ADAM_EPS = 1e-08
ADAM_WD = 0.01
ADAM_STEP = 10

NT = (((1,), (1,)), ((), ()))
TN = (((0,), (0,)), ((), ()))


def _params(*sem):
    return pltpu.CompilerParams(dimension_semantics=sem, vmem_limit_bytes=VMEM_LIMIT)


def _dot(a, b):
    return jnp.dot(a, b, preferred_element_type=F32)


def _dot_nt(a, b):
    return lax.dot_general(a, b, NT, preferred_element_type=F32)


def _t_bf16(v):
    return jnp.transpose(v).astype(BF16)


def _sigmoid(v):
    return 1.0 / (1.0 + jnp.exp(-v))


def _rms(x, g):
    r = lax.rsqrt(jnp.mean(x * x, axis=-1, keepdims=True) + EPS)
    n = x * r
    return n, r, n * g


def _rms_bwd(dh, n, r, g):
    dn = dh * g
    dx = r * (dn - n * jnp.mean(dn * n, axis=-1, keepdims=True))
    return dx, jnp.sum(dh * n, axis=0, keepdims=True)


def _ln_fwd(z, g, b):
    mu = jnp.mean(z, axis=-1, keepdims=True)
    zc = z - mu
    rs = lax.rsqrt(jnp.mean(zc * zc, axis=-1, keepdims=True) + EPS)
    zn = zc * rs
    return zn, rs, zn * g + b


def _ln_bwd(dl, zn, rs, g):
    dzn = dl * g
    dz = rs * (dzn - jnp.mean(dzn, axis=-1, keepdims=True) - zn * jnp.mean(dzn * zn, axis=-1, keepdims=True))
    return dz, jnp.sum(dl * zn, axis=0, keepdims=True), jnp.sum(dl, axis=0, keepdims=True)


def _tile(n, want):
    return want if n % want == 0 else n


def _resident(shape):
    return pl.BlockSpec(shape, lambda i: (0,) * len(shape), pipeline_mode=pl.Buffered(1))


def _ffn_fwd(x, g, w1t, w3t, w2, shards=()):
    s, d = x.shape
    f = w1t.shape[0]
    tm, tf = _tile(s, 512), MXU_TILE
    nj = f // tf
    ni = s // tm
    n_c = len(shards)

    def body(*refs):
        x_ref, g_ref, w1_ref, w3_ref, w2_ref = refs[:5]
        xo_ref, a_ref, b_ref, h_ref = refs[5 + n_c:9 + n_c]
        u_s = refs[9 + 2 * n_c]
        if n_c:
            gather = lambda: _ici_gather_copies(refs[5:5 + n_c], refs[9 + n_c:9 + 2 * n_c], refs[10 + 2 * n_c], refs[11 + 2 * n_c])

            @pl.when(pl.program_id(0) == 0)
            def _():
                for cp in gather():
                    cp.start()

        h = _rms(x_ref[...], g_ref[...])[2].astype(BF16)
        h_ref[...] = h
        for j in range(nj):
            cols = slice(j * tf, (j + 1) * tf)
            a = _dot_nt(h, w1_ref[cols, :])
            b = _dot_nt(h, w3_ref[cols, :])
            a_ref[:, cols] = a.astype(BF16)
            b_ref[:, cols] = b.astype(BF16)
            u_s[:, cols] = ((a * _sigmoid(a)) * b).astype(BF16)
        xo_ref[...] = x_ref[...] + FFN_RESIDUAL * _dot(u_s[...], w2_ref[...])
        if n_c:
            @pl.when(pl.program_id(0) == ni - 1)
            def _():
                _wait_all(gather())

    row = pl.BlockSpec((tm, d), lambda i: (i, 0))
    hid = pl.BlockSpec((tm, f), lambda i: (i, 0))
    sems = [pltpu.SemaphoreType.DMA((3 * n_c,))] * 2 if n_c else []
    out = pl.pallas_call(
        body, name="ffn_fwd_gather" if n_c else "ffn_fwd", grid=(ni,),
        in_specs=[row, _resident(g.shape), _resident(w1t.shape), _resident(w3t.shape), _resident(w2.shape)] + [ANY] * n_c,
        out_specs=[row, hid, hid, row] + [ANY] * n_c,
        out_shape=[jax.ShapeDtypeStruct((s, d), F32), jax.ShapeDtypeStruct((s, f), BF16), jax.ShapeDtypeStruct((s, f), BF16),
                   jax.ShapeDtypeStruct((s, d), BF16)] + _landing_shapes(shards),
        scratch_shapes=[pltpu.VMEM((tm, f), BF16)] + sems,
        compiler_params=_params("arbitrary"),
    )(x, g, w1t, w3t, w2, *shards)
    return out[0], out[1], out[2], out[3], list(out[4:])


def _ffn_bwd(dxo, x, g, a, b, w1t, w3t, w2, sums=None):
    s, d = x.shape
    f = w1t.shape[0]
    tm, tf = _tile(s, 256), MXU_TILE
    nj = f // tf
    ni = s // tm
    n_c = 0 if sums is None else 1

    def body(*refs):
        dxo_ref, x_ref, g_ref, a_ref, b_ref, w1_ref, w3_ref, w2_ref = refs[:8]
        dx_ref, dg_ref, dy_ref, da_ref, db_ref, u_ref, dxb_ref = refs[8 + n_c:15 + n_c]
        if n_c:
            exchange = lambda: _ici_reduce_copies(refs[8], refs[15 + n_c], *refs[15 + 2 * n_c:])

        @pl.when(pl.program_id(0) == 0)
        def _():
            dg_ref[...] = jnp.zeros_like(dg_ref)
            if n_c:
                copies, local = exchange()
                local.start()
                for cp in copies:
                    cp.start()

        dy = (FFN_RESIDUAL * dxo_ref[...]).astype(BF16)
        dy_ref[...] = dy
        for j in range(nj):
            cols = slice(j * tf, (j + 1) * tf)
            du = _dot_nt(dy, w2_ref[cols, :]).astype(BF16)
            av = a_ref[:, cols]
            bv = b_ref[:, cols]
            sg = _sigmoid(av.astype(F32)).astype(BF16)
            sl = av * sg
            da_ref[:, cols] = (du * bv) * (sg + sl * (1.0 - sg))
            db_ref[:, cols] = du * sl
            u_ref[:, cols] = sl * bv
        dh = _dot(da_ref[...], w1_ref[...]) + _dot(db_ref[...], w3_ref[...])
        n, r, _ = _rms(x_ref[...], g_ref[...])
        dxr, dg = _rms_bwd(dh, n, r, g_ref[...])
        dx = dxo_ref[...] + dxr
        dx_ref[...] = dx
        dxb_ref[...] = dx.astype(BF16)
        dg_ref[0:1, :] += dg
        if n_c:
            @pl.when(pl.program_id(0) == ni - 1)
            def _():
                copies, local = exchange()
                _wait_all(copies)
                local.wait()

    row = pl.BlockSpec((tm, d), lambda i: (i, 0))
    hid = pl.BlockSpec((tm, f), lambda i: (i, 0))
    extra = [] if sums is None else [sums]
    sems = [pltpu.SemaphoreType.DMA((3,)), pltpu.SemaphoreType.DMA((3,)), pltpu.SemaphoreType.DMA] if n_c else []
    out = pl.pallas_call(
        body, name="ffn_bwd_reduce" if n_c else "ffn_bwd", grid=(ni,),
        in_specs=[row, row, _resident(g.shape), hid, hid, _resident(w1t.shape), _resident(w3t.shape), _resident(w2.shape)]
        + [ANY] * n_c,
        out_specs=[row, pl.BlockSpec((8, d), lambda i: (0, 0)), row, hid, hid, hid, row] + [ANY] * n_c,
        out_shape=[jax.ShapeDtypeStruct((s, d), F32), jax.ShapeDtypeStruct((8, d), F32), jax.ShapeDtypeStruct((s, d), BF16),
                   jax.ShapeDtypeStruct((s, f), BF16), jax.ShapeDtypeStruct((s, f), BF16), jax.ShapeDtypeStruct((s, f), BF16),
                   jax.ShapeDtypeStruct((s, d), BF16)]
        + [jax.ShapeDtypeStruct(e.shape, e.dtype) for e in extra],
        scratch_shapes=sems,
        compiler_params=_params("arbitrary"),
    )(dxo, x, g, a, b, w1t, w3t, w2, *extra)
    return out


def _dw(am, bm):
    s, r = am.shape
    n = bm.shape[1]
    rb = r // 2 if r > 1024 and (r // 2) % LANES == 0 else r
    ts = _tile(s, 2048)
    ni = s // ts

    def body(am_ref, bm_ref, o_ref, acc_s):
        i = pl.program_id(1)

        @pl.when(i == 0)
        def _():
            acc_s[...] = jnp.zeros_like(acc_s)

        acc_s[...] += lax.dot_general(am_ref[...], bm_ref[...], TN, preferred_element_type=F32)

        @pl.when(i == ni - 1)
        def _():
            o_ref[...] = acc_s[...].astype(BF16)

    return pl.pallas_call(
        body, name="dw", grid=(r // rb, ni),
        in_specs=[pl.BlockSpec((ts, rb), lambda k, i: (i, k)), pl.BlockSpec((ts, n), lambda k, i: (i, 0))],
        out_specs=pl.BlockSpec((rb, n), lambda k, i: (k, 0)),
        out_shape=jax.ShapeDtypeStruct((r, n), BF16),
        scratch_shapes=[pltpu.VMEM((rb, n), F32)],
        compiler_params=_params("arbitrary", "arbitrary"),
    )(am, bm)


def _proj_fwd(x, g, w_int):
    s, d = x.shape
    f = w_int.shape[0]
    tm = _tile(s, 512)

    def body(x_ref, g_ref, w_ref, p_ref):
        p_ref[...] = _dot_nt(_rms(x_ref[...], g_ref[...])[2].astype(BF16), w_ref[...])

    return pl.pallas_call(
        body, name="proj_fwd", grid=(s // tm,),
        in_specs=[pl.BlockSpec((tm, d), lambda i: (i, 0)), _resident(g.shape), _resident(w_int.shape)],
        out_specs=pl.BlockSpec((tm, f), lambda i: (i, 0)),
        out_shape=jax.ShapeDtypeStruct((s, f), F32),
        compiler_params=_params("arbitrary"),
    )(x, g, w_int)


def _proj_bwd(dxo, x, g, dp, w_int):
    s, d = x.shape
    f = w_int.shape[0]
    tm = _tile(s, 512)

    def body(dxo_ref, x_ref, g_ref, dp_ref, w_ref, dx_ref, dg_ref, h_ref):
        @pl.when(pl.program_id(0) == 0)
        def _():
            dg_ref[...] = jnp.zeros_like(dg_ref)

        n, r, h = _rms(x_ref[...], g_ref[...])
        h_ref[...] = h.astype(BF16)
        dxr, dg = _rms_bwd(_dot(dp_ref[...], w_ref[...]), n, r, g_ref[...])
        dx_ref[...] = dxo_ref[...] + dxr
        dg_ref[0:1, :] += dg

    row = pl.BlockSpec((tm, d), lambda i: (i, 0))
    return pl.pallas_call(
        body, name="proj_bwd", grid=(s // tm,),
        in_specs=[row, row, _resident(g.shape), pl.BlockSpec((tm, f), lambda i: (i, 0)), _resident(w_int.shape)],
        out_specs=[row, pl.BlockSpec((8, d), lambda i: (0, 0)), row],
        out_shape=[jax.ShapeDtypeStruct((s, d), F32), jax.ShapeDtypeStruct((8, d), F32), jax.ShapeDtypeStruct((s, d), BF16)],
        compiler_params=_params("arbitrary"),
    )(dxo, x, g, dp, w_int)


C = D_GROUP


def _piece(ref, k):
    return ref[:, k * C:(k + 1) * C]


def _up(v, r):
    return v if r == 0 else pltpu.roll(v, v.shape[0] - r, 0)


def _down(v, r):
    return v if r == 0 else pltpu.roll(v, r, 0)


def _lane_group():
    lane = lax.broadcasted_iota(jnp.int32, (1, C), 1)
    return (lane >= POOL_GROUP).astype(jnp.int32) + (lane >= 2 * POOL_GROUP).astype(jnp.int32) + (
        lane >= 3 * POOL_GROUP).astype(jnp.int32)


def _by_group(grp, v2, v4, v8, v16):
    return jnp.where(grp == 0, v2, jnp.where(grp == 1, v4, jnp.where(grp == 2, v8, v16)))


def _pool_count(grp, row0, t):
    pos = (row0 + lax.broadcasted_iota(jnp.int32, (t, C), 0) + 1).astype(F32)
    return jnp.minimum(pos, _by_group(grp, 2.0, 4.0, 8.0, 16.0))


def _trailing_sums(ext, grp, t):
    s2 = ext + _down(ext, 1)
    s4 = s2 + _down(s2, 2)
    s8 = s4 + _down(s4, 4)
    s16 = s8 + _down(s8, 8)
    return _by_group(grp, s2, s4, s8, s16)[HALO:HALO + t]


def _leading_sums(ext, grp, t):
    s2 = ext + _up(ext, 1)
    s4 = s2 + _up(s2, 2)
    s8 = s4 + _up(s4, 4)
    s16 = s8 + _up(s8, 8)
    return _by_group(grp, s2, s4, s8, s16)[0:t]


def _head_select(r4, grp):
    assert HEAD_DIM == POOL_GROUP and N_HEADS == 4
    return _by_group(grp, *(r4[h * CHUNK:(h + 1) * CHUNK] for h in range(N_HEADS)))


def _conv_taps():
    return [(k, (k + 2) % 8, (k + 2) - (k + 2) % 8) for k in range(CONF_KERNEL)]


def _mixer_fwd(p, x1, cw, vec, pool_w, wstack, bias, w_out):
    s, d = x1.shape
    t = _tile(s, T_MIX_FWD)
    n_ext = t + HALO
    dm = w_out.shape[0]

    def body(p_ref, x1_ref, cw_ref, vec_ref, pw_ref, ws_ref, bias_ref, wo_ref, x2_ref, mix_s, z_ref, cy_s, cq_s, cx_s):
        i = pl.program_id(0)

        @pl.when(i == 0)
        def _():
            cy_s[...] = jnp.zeros_like(cy_s)
            cq_s[...] = jnp.zeros_like(cq_s)
            cx_s[...] = jnp.zeros_like(cx_s)

        grp = _lane_group()
        y = _piece(p_ref, 0) * _sigmoid(_piece(p_ref, 1))
        ext = jnp.concatenate([cy_s[...], y], axis=0)
        cy_s[...] = y[t - HALO:t]
        z = jnp.broadcast_to(vec_ref[0:1, :], (t, C))
        shifted = {}
        for k, r, off in _conv_taps():
            if r not in shifted:
                shifted[r] = _up(ext, r)
            z = z + cw_ref[k:k + 1, :] * shifted[r][off:off + t]
        z_ref[...] = z
        ln = _ln_fwd(z, vec_ref[1:2, :], vec_ref[2:3, :])[2]
        mix_s[:, 0:C] = (ln * _sigmoid(ln)).astype(BF16)
        q = _piece(p_ref, 3) * _piece(p_ref, 4)
        ext = jnp.concatenate([cq_s[...], q], axis=0)
        cq_s[...] = q[t - HALO:t]
        cz = vec_ref[8:9, :] * q + vec_ref[7:8, :] * _down(ext, 1)[HALO:] + vec_ref[6:7, :] * _down(ext, 2)[HALO:]
        mix_s[:, C:2 * C] = (_piece(p_ref, 2) * cz).astype(BF16)
        xp = _piece(p_ref, 5)
        ext = jnp.concatenate([cx_s[...], xp], axis=0)
        cx_s[...] = xp[t - HALO:t]
        dd = _trailing_sums(ext, grp, t) / _pool_count(grp, i * t, t) - xp
        mix_s[:, 2 * C:3 * C] = (_dot(dd.astype(BF16), pw_ref[...]) * vec_ref[3:4, :]).astype(BF16)
        vln = _ln_fwd(_piece(p_ref, 7), vec_ref[4:5, :], vec_ref[5:6, :])[2].astype(BF16)
        for n in range(t // CHUNK):
            rows = slice(n * CHUNK, (n + 1) * CHUNK)
            mixed = _head_select(_dot(ws_ref[...], vln[rows]), grp) + bias_ref[...]
            mix_s[rows, 3 * C:4 * C] = (p_ref[rows, 6 * C:7 * C] * mixed).astype(BF16)
        x2_ref[...] = x1_ref[...] + _dot(mix_s[...], wo_ref[...])

    full = lambda a: pl.BlockSpec(a.shape, lambda i: (0, 0))
    return pl.pallas_call(
        body, name="mixer_fwd", grid=(s // t,),
        in_specs=[pl.BlockSpec((t, p.shape[1]), lambda i: (i, 0)), pl.BlockSpec((t, d), lambda i: (i, 0)),
                  full(cw), full(vec), full(pool_w), full(wstack), full(bias), full(w_out)],
        out_specs=[pl.BlockSpec((t, d), lambda i: (i, 0)), pl.BlockSpec((t, dm), lambda i: (i, 0)),
                   pl.BlockSpec((t, C), lambda i: (i, 0))],
        out_shape=[jax.ShapeDtypeStruct((s, d), F32), jax.ShapeDtypeStruct((s, dm), BF16), jax.ShapeDtypeStruct((s, C), F32)],
        scratch_shapes=[pltpu.VMEM((HALO, C), F32)] * 3,
        compiler_params=_params("arbitrary"),
    )(p, x1, cw, vec, pool_w, wstack, bias, w_out)


def _mixer_bwd(dx2, p, z, cw, vec, pool_w, wstack, wstack_t, bias, tril4, head_rows, w_out):
    s, d = dx2.shape
    t = _tile(s, T_MIX_BWD)
    nt = s // t
    n_ext = t + HALO
    hb = t // HALO

    def body(dx2_ref, p_ref, ph_ref, z_ref, cw_ref, vec_ref, pw_ref, ws_ref, wst_ref, bias_ref, tril_ref, hr_ref, wo_ref,
             dp_ref, dcw_ref, dvec_ref, dpool_ref, dws_ref, dbs_ref, cdz_s, cdc_s, cf_s, vy_s, dvl_s, dbias_s):
        i = pl.program_id(0)
        tile = nt - 1 - i

        @pl.when(i == 0)
        def _():
            for ref in (cdz_s, cdc_s, cf_s, dbias_s, dcw_ref, dvec_ref, dpool_ref, dws_ref, dbs_ref):
                ref[...] = jnp.zeros_like(ref)

        grp = _lane_group()
        first = jnp.where(tile > 0, 1.0, 0.0)
        dmix = _dot_nt(dx2_ref[...].astype(BF16), wo_ref[...])
        d_a, d_b, d_c, d_d = (dmix[:, k * C:(k + 1) * C] for k in range(4))

        def acc_vec(row, v):
            dvec_ref[row:row + 1, :] += jnp.sum(v, axis=0, keepdims=True)

        val, gate = _piece(p_ref, 0), _piece(p_ref, 1)
        sgate = _sigmoid(gate)
        y = val * sgate
        y_halo = ph_ref[:, 0:C] * _sigmoid(ph_ref[:, C:2 * C]) * first
        ext = jnp.concatenate([y_halo, y], axis=0)
        for r in range(8):
            vy_s[r] = _up(ext, r)
        zn, rs, ln = _ln_fwd(z_ref[...], vec_ref[1:2, :], vec_ref[2:3, :])
        sg = _sigmoid(ln)
        dln = d_a * (sg * (1.0 + ln * (1.0 - sg)))
        dz, dg, db = _ln_bwd(dln, zn, rs, vec_ref[1:2, :])
        dvec_ref[1:2, :] += dg
        dvec_ref[2:3, :] += db
        acc_vec(0, dz)
        for k, r, off in _conv_taps():
            dcw_ref[k:k + 1, :] += jnp.sum(dz * vy_s[r, off:off + t, :], axis=0, keepdims=True)
        ext = jnp.concatenate([dz, cdz_s[...]], axis=0)
        cdz_s[...] = dz[0:HALO]
        dy = jnp.zeros((t, C), F32)
        shifted = {}
        for k in range(CONF_KERNEL):
            m = CONF_KERNEL - 1 - k
            r, off = m % 8, m - m % 8
            if r not in shifted:
                shifted[r] = _up(ext, r)
            dy = dy + cw_ref[k:k + 1, :] * shifted[r][off:off + t]
        dp_ref[:, 0:C] = (dy * sgate).astype(BF16)
        dp_ref[:, C:2 * C] = (dy * val * sgate * (1.0 - sgate)).astype(BF16)

        sb, sc, sx = _piece(p_ref, 2), _piece(p_ref, 3), _piece(p_ref, 4)
        q = sc * sx
        q_halo = ph_ref[:, 3 * C:4 * C] * ph_ref[:, 4 * C:5 * C] * first
        ext = jnp.concatenate([q_halo, q], axis=0)
        q1, q2 = _down(ext, 1)[HALO:], _down(ext, 2)[HALO:]
        cz = vec_ref[8:9, :] * q + vec_ref[7:8, :] * q1 + vec_ref[6:7, :] * q2
        dcz = d_b * sb
        dp_ref[:, 2 * C:3 * C] = (d_b * cz).astype(BF16)
        acc_vec(8, dcz * q)
        acc_vec(7, dcz * q1)
        acc_vec(6, dcz * q2)
        ext = jnp.concatenate([dcz, cdc_s[...]], axis=0)
        cdc_s[...] = dcz[0:HALO]
        dq = vec_ref[8:9, :] * dcz + vec_ref[7:8, :] * _up(ext, 1)[0:t] + vec_ref[6:7, :] * _up(ext, 2)[0:t]
        dp_ref[:, 3 * C:4 * C] = (dq * sx).astype(BF16)
        dp_ref[:, 4 * C:5 * C] = (dq * sc).astype(BF16)

        xp = _piece(p_ref, 5)
        ext = jnp.concatenate([ph_ref[:, 5 * C:6 * C] * first, xp], axis=0)
        cnt = _pool_count(grp, tile * t, t)
        dd = (_trailing_sums(ext, grp, t) / cnt - xp).astype(BF16)
        e2 = _dot(dd, pw_ref[...])
        acc_vec(3, d_c * e2)
        de = (d_c * vec_ref[3:4, :]).astype(BF16)
        dpool_ref[...] += _dot(_t_bf16(dd.astype(F32)), de)
        ddd = _dot_nt(de, pw_ref[...])
        fq = ddd / cnt
        ext = jnp.concatenate([fq, cf_s[...]], axis=0)
        cf_s[...] = fq[0:HALO]
        dp_ref[:, 5 * C:6 * C] = (_leading_sums(ext, grp, t) - ddd).astype(BF16)

        vn, vrs, vlnf = _ln_fwd(_piece(p_ref, 7), vec_ref[4:5, :], vec_ref[5:6, :])
        vln = vlnf.astype(BF16)
        for n in range(t // CHUNK):
            rows = slice(n * CHUNK, (n + 1) * CHUNK)
            mixed = _head_select(_dot(ws_ref[...], vln[rows]), grp) + bias_ref[...]
            dd_n = d_d[rows]
            dp_ref[rows, 6 * C:7 * C] = (dd_n * mixed).astype(BF16)
            dmx = dd_n * p_ref[rows, 6 * C:7 * C]
            dbias_s[...] += dmx
            dmx_b = dmx.astype(BF16)
            dvl_s[rows, :] = _head_select(_dot(wst_ref[...], dmx_b), grp)
            for h in range(N_HEADS):
                hrows = slice(h * CHUNK, (h + 1) * CHUNK)
                dws_ref[hrows, :] += _dot_nt(jnp.where(grp == h, dmx_b, jnp.zeros_like(dmx_b)), vln[rows])
        dvl = dvl_s[...]
        dv, dg, db = _ln_bwd(dvl, vn, vrs, vec_ref[4:5, :])
        dvec_ref[4:5, :] += dg
        dvec_ref[5:6, :] += db
        dp_ref[:, 7 * C:8 * C] = dv.astype(BF16)

        @pl.when(i == nt - 1)
        def _():
            dws_ref[...] = dws_ref[...] * tril_ref[...]
            dbs_ref[...] = lax.dot_general(hr_ref[...], dbias_s[...], NT, precision=lax.Precision.HIGHEST,
                                           preferred_element_type=F32)

    full = lambda a: pl.BlockSpec(a.shape, lambda i: (0, 0))
    acc = lambda shape: pl.BlockSpec(shape, lambda i: (0, 0))
    f = p.shape[1]
    return pl.pallas_call(
        body, name="mixer_bwd", grid=(nt,),
        in_specs=[pl.BlockSpec((t, d), lambda i: (nt - 1 - i, 0)), pl.BlockSpec((t, f), lambda i: (nt - 1 - i, 0)),
                  pl.BlockSpec((HALO, f), lambda i: (jnp.maximum((nt - 1 - i) * hb - 1, 0), 0)),
                  pl.BlockSpec((t, C), lambda i: (nt - 1 - i, 0)), full(cw), full(vec), full(pool_w), full(wstack), full(wstack_t), full(bias), full(tril4), full(head_rows),
                  full(w_out)],
        out_specs=[pl.BlockSpec((t, f), lambda i: (nt - 1 - i, 0)), acc((32, C)), acc((16, C)), acc((C, C)),
                   acc((N_HEADS * CHUNK, CHUNK)), acc((8, CHUNK))],
        out_shape=[jax.ShapeDtypeStruct((s, f), BF16), jax.ShapeDtypeStruct((32, C), F32), jax.ShapeDtypeStruct((16, C), F32),
                   jax.ShapeDtypeStruct((C, C), F32), jax.ShapeDtypeStruct((N_HEADS * CHUNK, CHUNK), F32),
                   jax.ShapeDtypeStruct((8, CHUNK), F32)],
        scratch_shapes=[pltpu.VMEM((HALO, C), F32)] * 3 + [pltpu.VMEM((8, n_ext, C), F32), pltpu.VMEM((t, C), F32),
                                                            pltpu.VMEM((CHUNK, C), F32)],
        compiler_params=_params("arbitrary"),
    )(dx2, p, p, z, cw, vec, pool_w, wstack, wstack_t, bias, tril4, head_rows, w_out)


def _loss_bwd(x, g, target):
    s, d = x.shape
    tm = _tile(s, 512)

    def body(x_ref, g_ref, t_ref, dx_ref, dg_ref, loss_ref):
        @pl.when(pl.program_id(0) == 0)
        def _():
            dg_ref[...] = jnp.zeros_like(dg_ref)
            loss_ref[...] = jnp.zeros_like(loss_ref)

        n, r, y = _rms(x_ref[...], g_ref[...])
        err = y - t_ref[...]
        loss_ref[...] += 0.5 * jnp.sum(jnp.mean(err * err, axis=-1, keepdims=True), axis=0, keepdims=True)
        dxr, dg = _rms_bwd(err * (1.0 / d), n, r, g_ref[...])
        dx_ref[...] = dxr
        dg_ref[0:1, :] += dg

    row = pl.BlockSpec((tm, d), lambda i: (i, 0))
    return pl.pallas_call(
        body, name="loss_bwd", grid=(s // tm,),
        in_specs=[row, pl.BlockSpec((1, d), lambda i: (0, 0)), row],
        out_specs=[row, pl.BlockSpec((8, d), lambda i: (0, 0)), pl.BlockSpec((8, LANES), lambda i: (0, 0))],
        out_shape=[jax.ShapeDtypeStruct((s, d), F32), jax.ShapeDtypeStruct((8, d), F32), jax.ShapeDtypeStruct((8, LANES), F32)],
        compiler_params=_params("arbitrary"),
    )(x, g, target)


def _adamw_math(w, g, m, v):
    m = ADAM_B1 * m + (1.0 - ADAM_B1) * g
    v = ADAM_B2 * v + (1.0 - ADAM_B2) * (g * g)
    m_hat = m / (1.0 - ADAM_B1 ** ADAM_STEP)
    v_hat = v / (1.0 - ADAM_B2 ** ADAM_STEP)
    return -ADAM_LR * (m_hat / (jnp.sqrt(v_hat) + ADAM_EPS) + ADAM_WD * w), m, v


def _adamw(w, g, m, v):
    r, c = w.shape
    tr = r // 8 if r % 64 == 0 else r

    def body(w_ref, g_ref, m_ref, v_ref, d_ref, mo_ref, vo_ref, go_ref):
        gv = g_ref[...]
        d_ref[...], mo_ref[...], vo_ref[...] = _adamw_math(w_ref[...], gv, m_ref[...], v_ref[...])
        go_ref[...] = gv

    blk = pl.BlockSpec((tr, c), lambda i: (i, 0))
    return pl.pallas_call(
        body, name="adamw", grid=(r // tr,), in_specs=[blk] * 4, out_specs=[blk] * 4,
        out_shape=[jax.ShapeDtypeStruct((r, c), F32)] * 4, compiler_params=_params("arbitrary"),
    )(w, g, m, v)


def _adamw_small(ws, gs, ms, vs):
    n = len(ws)

    def body(*refs):
        ins, outs = refs[:4 * n], refs[4 * n:]
        for k in range(n):
            dl, mo, vo = _adamw_math(ins[k][...], ins[n + k][...], ins[2 * n + k][...], ins[3 * n + k][...])
            outs[k][...], outs[n + k][...], outs[2 * n + k][...] = dl, mo, vo

    vm = pl.BlockSpec(memory_space=pltpu.VMEM)
    out = pl.pallas_call(
        body, name="adamw_small", in_specs=[vm] * (4 * n), out_specs=[vm] * (3 * n),
        out_shape=[jax.ShapeDtypeStruct(a.shape, F32) for a in ws] * 3,
        compiler_params=pltpu.CompilerParams(vmem_limit_bytes=VMEM_LIMIT),
    )(*ws, *gs, *ms, *vs)
    return out[:n], out[n:2 * n], out[2 * n:]


def _where_am_i():
    x, y, c = lax.axis_index("x"), lax.axis_index("y"), lax.axis_index("c")
    chips = [(1 - x, y), (x, 1 - y), (1 - x, 1 - y)]
    return x, y, c, chips


def _chip_id(chip):
    return 2 * chip[0] + chip[1]


def _landing_shapes(shards):
    return [jax.ShapeDtypeStruct((3, a.shape[0] // 2, a.shape[1]), a.dtype) for a in shards]


def _ici_gather_copies(ins, lands, send_sems, recv_sems):
    _, _, c, chips = _where_am_i()
    copies = []
    for a, src in enumerate(ins):
        hr = src.shape[0] // 2
        for j, chip in enumerate(chips):
            copies.append(pltpu.make_async_remote_copy(
                src_ref=src.at[pl.ds(c * hr, hr)], dst_ref=lands[a].at[j], send_sem=send_sems.at[3 * a + j],
                recv_sem=recv_sems.at[3 * a + j], device_id=(*chip, c), device_id_type=MESH))
    return copies


def _ici_reduce_copies(s_ref, o_ref, send_sems, recv_sems, local_sem):
    x, y, c, chips = _where_am_i()
    copies = [pltpu.make_async_remote_copy(src_ref=s_ref.at[_chip_id(chip)], dst_ref=o_ref.at[j], send_sem=send_sems.at[j],
                                           recv_sem=recv_sems.at[j], device_id=(*chip, c), device_id_type=MESH)
              for j, chip in enumerate(chips)]
    return copies, pltpu.make_async_copy(s_ref.at[_chip_id((x, y))], o_ref.at[3], local_sem)


def _wait_all(copies):
    for cp in copies:
        cp.wait_recv()
    for cp in copies:
        cp.wait_send()


def _ici_gather(shards):
    n = len(shards)

    def body(*refs):
        copies = _ici_gather_copies(refs[:n], refs[n:2 * n], refs[2 * n], refs[2 * n + 1])
        for cp in copies:
            cp.start()
        _wait_all(copies)

    return pl.pallas_call(
        body, name="ici_gather", in_specs=[ANY] * n, out_specs=[ANY] * n, out_shape=_landing_shapes(shards),
        scratch_shapes=[pltpu.SemaphoreType.DMA((3 * n,))] * 2,
    )(*shards)


SLOTS = 4


def _sibling_gather(shards, lands):
    n = len(shards)
    d = shards[0].shape[1]
    halves = [a.shape[0] // 2 for a in shards]
    hmax = max(halves)
    chunks = [(a, j, halves[a]) for a in range(n) for j in range(3)]
    nc = len(chunks)

    def body(*refs):
        ins, lnd, outs = refs[:n], refs[n:2 * n], refs[2 * n:3 * n]
        sbuf, rbuf, obuf, ld_sems, take_sems, send_sems, recv_sems, place_sems, own_ld_sems, own_st_sems, credits = refs[3 * n:]
        x, y, c, chips = _where_am_i()
        sibling = (x, y, 1 - c)
        me = _chip_id((x, y))

        def load(i):
            a, j, hr = chunks[i]
            return pltpu.make_async_copy(lnd[a].at[j], sbuf.at[i % SLOTS, pl.ds(0, hr)], ld_sems.at[i % SLOTS])

        def push(i):
            hr, slot = chunks[i][2], i % SLOTS
            return pltpu.make_async_remote_copy(src_ref=sbuf.at[slot, pl.ds(0, hr)], dst_ref=rbuf.at[slot, pl.ds(0, hr)],
                                                send_sem=send_sems.at[slot], recv_sem=recv_sems.at[slot], device_id=sibling,
                                                device_id_type=MESH)

        def take(i):
            a, j, hr = chunks[i]
            return pltpu.make_async_copy(rbuf.at[i % SLOTS, pl.ds(0, hr)],
                                         outs[a].at[_chip_id(chips[j]), pl.ds((1 - c) * hr, hr)], take_sems.at[i % SLOTS])

        def place(i):
            a, j, hr = chunks[i]
            return pltpu.make_async_copy(sbuf.at[i % SLOTS, pl.ds(0, hr)], outs[a].at[_chip_id(chips[j]), pl.ds(c * hr, hr)],
                                         place_sems.at[i % SLOTS])

        own = [(a, h, halves[a]) for a in range(n) for h in range(2)]

        def own_load(k):
            a, h, hr = own[k]
            return pltpu.make_async_copy(ins[a].at[pl.ds(h * hr, hr)], obuf.at[k % 2, pl.ds(0, hr)], own_ld_sems.at[k % 2])

        def own_store(k):
            a, h, hr = own[k]
            return pltpu.make_async_copy(obuf.at[k % 2, pl.ds(0, hr)], outs[a].at[me, pl.ds(h * hr, hr)], own_st_sems.at[k % 2])

        def own_step(k):
            if k < len(own):
                if k >= 2:
                    own_store(k - 2).wait()
                own_load(k).start()
            if 1 <= k <= len(own):
                own_load(k - 1).wait()
                own_store(k - 1).start()

        for i in range(min(2, nc)):
            load(i).start()
        for i in range(nc):
            own_step(i)
            if i >= 2:
                push(i - 2).wait_send()
                place(i - 2).wait()
            if i + 2 < nc:
                load(i + 2).start()
            load(i).wait()
            place(i).start()
            if i >= SLOTS:
                pl.semaphore_wait(credits.at[i % SLOTS], 1)
            push(i).start()
            if i >= 1:
                push(i - 1).wait_recv()
                take(i - 1).start()
            if i >= 2:
                take(i - 2).wait()
                if i - 2 + SLOTS < nc:
                    pl.semaphore_signal(credits.at[(i - 2) % SLOTS], inc=1, device_id=sibling, device_id_type=MESH)
        for i in range(max(0, nc - 2), nc):
            push(i).wait_send()
            place(i).wait()
        push(nc - 1).wait_recv()
        take(nc - 1).start()
        for i in range(max(0, nc - 2), nc):
            take(i).wait()
        for k in range(nc, len(own) + 1):
            own_step(k)
        for k in range(max(0, len(own) - 2), len(own)):
            own_store(k).wait()

    dma = pltpu.SemaphoreType.DMA((SLOTS,))
    dma2 = pltpu.SemaphoreType.DMA((2,))
    return pl.pallas_call(
        body, name="sibling_gather", in_specs=[ANY] * (2 * n), out_specs=[ANY] * n,
        out_shape=[jax.ShapeDtypeStruct((N_CHIPS,) + a.shape, a.dtype) for a in shards],
        scratch_shapes=[pltpu.VMEM((SLOTS, hmax, d), BF16), pltpu.VMEM((SLOTS, hmax, d), BF16), pltpu.VMEM((2, hmax, d), BF16),
                        dma, dma, dma, dma, dma, dma2, dma2, pltpu.SemaphoreType.REGULAR((SLOTS,))],
        compiler_params=pltpu.CompilerParams(vmem_limit_bytes=VMEM_LIMIT),
    )(*shards, *lands)


def _pair_sum(grads):
    n = len(grads)
    chunks, total, rows = _pair_chunks(grads)

    def body(*refs):
        first, step, last = _pair_sum_steps(refs[:n], refs[n], refs[n + 1:], chunks)
        first()
        for i in range(len(chunks)):
            step(i)
        last()

    return pl.pallas_call(
        body, name="pair_sum", in_specs=[ANY] * n, out_specs=ANY,
        out_shape=jax.ShapeDtypeStruct((N_CHIPS, total, grads[0].shape[2]), BF16),
        scratch_shapes=_pair_scratch(rows, grads[0].shape[2]),
        compiler_params=pltpu.CompilerParams(vmem_limit_bytes=VMEM_LIMIT),
    )(*grads)


def _pair_chunks(grads):
    halves = [g.shape[1] // 2 for g in grads]
    chunks, off = [], 0
    for a in range(len(grads)):
        chunks += [(a, k, off, halves[a]) for k in range(N_CHIPS)]
        off += halves[a]
    return chunks, off, max(halves)


def _pair_scratch(rows, d):
    stage = pltpu.VMEM((SLOTS, rows, d), BF16)
    dma = pltpu.SemaphoreType.DMA((SLOTS,))
    return [stage, stage, stage, stage, dma, dma, dma, dma, dma, pltpu.SemaphoreType.REGULAR((SLOTS,))]


def _pair_sum_steps(ins, out_ref, scratch, chunks):
    if True:
        sbuf, rbuf, mbuf, obuf, ls_sems, lm_sems, st_sems, send_sems, recv_sems, credits = scratch
        nc = len(chunks)
        x, y, c, _ = _where_am_i()
        sibling = (x, y, 1 - c)

        def load_theirs(i):
            a, k, _, hr = chunks[i]
            return pltpu.make_async_copy(ins[a].at[k, pl.ds((1 - c) * hr, hr)], sbuf.at[i % SLOTS, pl.ds(0, hr)], ls_sems.at[i % SLOTS])

        def load_mine(i):
            a, k, _, hr = chunks[i]
            return pltpu.make_async_copy(ins[a].at[k, pl.ds(c * hr, hr)], mbuf.at[i % SLOTS, pl.ds(0, hr)], lm_sems.at[i % SLOTS])

        def push(i):
            hr, slot = chunks[i][3], i % SLOTS
            return pltpu.make_async_remote_copy(src_ref=sbuf.at[slot, pl.ds(0, hr)], dst_ref=rbuf.at[slot, pl.ds(0, hr)],
                                                send_sem=send_sems.at[slot], recv_sem=recv_sems.at[slot], device_id=sibling,
                                                device_id_type=MESH)

        def store(i):
            _, k, o, hr = chunks[i]
            slot = i % SLOTS
            return pltpu.make_async_copy(obuf.at[slot, pl.ds(0, hr)], out_ref.at[k, pl.ds(o, hr)], st_sems.at[slot])

        def start_push(i):
            load_theirs(i).wait()
            if i >= SLOTS:
                pl.semaphore_wait(credits.at[i % SLOTS], 1)
            push(i).start()

        def first():
            for i in range(min(2, nc)):
                load_theirs(i).start()
                load_mine(i).start()
            start_push(0)

        def step(i):
            hr, slot = chunks[i][3], i % SLOTS
            if i + 2 < nc:
                load_theirs(i + 2).start()
                load_mine(i + 2).start()
            if i + 1 < nc:
                start_push(i + 1)
            push(i).wait_recv()
            push(i).wait_send()
            load_mine(i).wait()
            if i >= SLOTS:
                store(i - SLOTS).wait()
            obuf[slot, 0:hr, :] = (mbuf[slot, 0:hr, :].astype(F32) + rbuf[slot, 0:hr, :].astype(F32)).astype(BF16)
            if i + SLOTS < nc:
                pl.semaphore_signal(credits.at[slot], inc=1, device_id=sibling, device_id_type=MESH)
            store(i).start()

        def last():
            for i in range(max(0, nc - SLOTS), nc):
                store(i).wait()

        return first, step, last


def _reduce_chips(sums):
    def body(s_ref, o_ref, send_sems, recv_sems, local_sem):
        copies, local = _ici_reduce_copies(s_ref, o_ref, send_sems, recv_sems, local_sem)
        local.start()
        for cp in copies:
            cp.start()
        _wait_all(copies)
        local.wait()

    return pl.pallas_call(
        body, name="reduce_chips", in_specs=[ANY], out_specs=ANY, out_shape=jax.ShapeDtypeStruct(sums.shape, BF16),
        scratch_shapes=[pltpu.SemaphoreType.DMA((3,)), pltpu.SemaphoreType.DMA((3,)), pltpu.SemaphoreType.DMA],
    )(sums)


def _sum_share(parts, groups, rows):
    layers, n, n_g = len(parts), len(rows), len(groups)
    parts = [part for layer in parts for part in layer]
    n_l = len(parts)
    d = parts[0].shape[2]
    halves = [r // 2 for r in rows]
    hmax = max(halves)
    chunks = []
    for l in range(layers):
        for g, members in enumerate(groups):
            off = 0
            for a in members:
                chunks.append((l * n_g + g, a, off, halves[a], l))
                off += halves[a]
    nc = len(chunks)

    def body(*refs):
        ins, outs = refs[:n_l], refs[n_l:n_l + n]
        pbuf, obuf, rbuf, ld_sems, keep_sems, take_sems, send_sems, recv_sems, credits = refs[n_l + n:]
        x, y, c, _ = _where_am_i()
        sibling = (x, y, 1 - c)

        def load(i):
            part, _, off, hr, _ = chunks[i]
            return pltpu.make_async_copy(ins[part].at[:, pl.ds(off, hr)], pbuf.at[i % SLOTS, :, pl.ds(0, hr)], ld_sems.at[i % SLOTS])

        def keep(i):
            _, a, _, hr, l = chunks[i]
            return pltpu.make_async_copy(obuf.at[i % SLOTS, pl.ds(0, hr)], outs[a].at[l, pl.ds(c * hr, hr)], keep_sems.at[i % SLOTS])

        def push(i):
            hr, slot = chunks[i][3], i % SLOTS
            return pltpu.make_async_remote_copy(src_ref=obuf.at[slot, pl.ds(0, hr)], dst_ref=rbuf.at[slot, pl.ds(0, hr)],
                                                send_sem=send_sems.at[slot], recv_sem=recv_sems.at[slot], device_id=sibling,
                                                device_id_type=MESH)

        def take(i):
            _, a, _, hr, l = chunks[i]
            return pltpu.make_async_copy(rbuf.at[i % SLOTS, pl.ds(0, hr)], outs[a].at[l, pl.ds((1 - c) * hr, hr)],
                                         take_sems.at[i % SLOTS])

        for i in range(min(2, nc)):
            load(i).start()
        for i in range(nc):
            hr, slot = chunks[i][3], i % SLOTS
            if i + 2 < nc:
                load(i + 2).start()
            load(i).wait()
            if i >= SLOTS:
                keep(i - SLOTS).wait()
                push(i - SLOTS).wait_send()
            part = lambda k: pbuf[slot, k, 0:hr, :].astype(F32)
            obuf[slot, 0:hr, :] = ((part(3) + part(0)) + part(1)) + part(2)
            keep(i).start()
            if i >= SLOTS:
                pl.semaphore_wait(credits.at[slot], 1)
            push(i).start()
            if i >= 1:
                push(i - 1).wait_recv()
                take(i - 1).start()
            if i >= 2:
                take(i - 2).wait()
                if i - 2 + SLOTS < nc:
                    pl.semaphore_signal(credits.at[(i - 2) % SLOTS], inc=1, device_id=sibling, device_id_type=MESH)
        push(nc - 1).wait_recv()
        take(nc - 1).start()
        for i in range(max(0, nc - 2), nc):
            take(i).wait()
        for i in range(max(0, nc - SLOTS), nc):
            keep(i).wait()
            push(i).wait_send()

    dma = pltpu.SemaphoreType.DMA((SLOTS,))
    return pl.pallas_call(
        body, name="sum_share", in_specs=[ANY] * n_l, out_specs=[ANY] * n,
        out_shape=[jax.ShapeDtypeStruct((layers, r, d), F32) for r in rows],
        scratch_shapes=[pltpu.VMEM((SLOTS, N_CHIPS, hmax, d), BF16), pltpu.VMEM((SLOTS, hmax, d), F32),
                        pltpu.VMEM((SLOTS, hmax, d), F32), dma, dma, dma, dma, dma, pltpu.SemaphoreType.REGULAR((SLOTS,))],
        compiler_params=pltpu.CompilerParams(vmem_limit_bytes=VMEM_LIMIT),
    )(*parts)


def _all_gather_small(block, reduce):
    m, n = block.shape

    def body(x_ref, out_ref, *scratch):
        if reduce:
            all_ref, send_sems, recv_sems, local_sem = scratch
        else:
            all_ref = out_ref
            send_sems, recv_sems, local_sem = scratch
        x, y, c, chips = _where_am_i()
        me, sibling = (x, y, c), (x, y, 1 - c)

        def rows(px, py, pc):
            return all_ref.at[pl.ds((4 * px + 2 * py + pc) * m, m), :]

        def copy(k, blk, to, src=None):
            return pltpu.make_async_remote_copy(src_ref=rows(*blk) if src is None else src, dst_ref=rows(*blk),
                                                send_sem=send_sems.at[k], recv_sem=recv_sems.at[k], device_id=to,
                                                device_id_type=MESH)

        mine = pltpu.make_async_copy(x_ref, rows(*me), local_sem)
        mine.start()
        first = [copy(0, me, sibling, src=x_ref)]
        first += [copy(1 + j, me, (*chip, c), src=x_ref) for j, chip in enumerate(chips)]
        for cp in first:
            cp.start()
        passed = [copy(4 + j, (*chip, c), sibling) for j, chip in enumerate(chips)]
        for j, chip in enumerate(chips):
            copy(1 + j, (*chip, c), me).wait_recv()
            passed[j].start()
        copy(0, sibling, me).wait_recv()
        for j, chip in enumerate(chips):
            copy(4 + j, (*chip, 1 - c), me).wait_recv()
        for cp in first + passed:
            cp.wait_send()
        mine.wait()
        if reduce:
            total = all_ref[0:m, :]
            for dev in range(1, N_DEV):
                total = total + all_ref[dev * m:(dev + 1) * m, :]
            out_ref[...] = total

    vm = pl.BlockSpec(memory_space=pltpu.VMEM)
    sems = [pltpu.SemaphoreType.DMA((7,)), pltpu.SemaphoreType.DMA((7,)), pltpu.SemaphoreType.DMA]
    return pl.pallas_call(
        body, name="reduce_small" if reduce else "gather_small", in_specs=[vm], out_specs=vm,
        out_shape=jax.ShapeDtypeStruct((m, n) if reduce else (N_DEV * m, n), F32),
        scratch_shapes=([pltpu.VMEM((N_DEV * m, n), F32)] if reduce else []) + sems,
        compiler_params=pltpu.CompilerParams(vmem_limit_bytes=VMEM_LIMIT),
    )(block)


def _pack(arrays):
    flat = jnp.concatenate([a.reshape(-1) for a in arrays])
    pad = (-flat.shape[0]) % (8 * LANES)
    return jnp.pad(flat, (0, pad)).reshape(-1, LANES)


def _unpack(buf, shapes):
    flat = buf.reshape(-1)
    out, off = [], 0
    for shp in shapes:
        size = 1
        for dim in shp:
            size *= dim
        out.append(flat[off:off + size].reshape(shp))
        off += size
    return out


BIG = ("ffn1_w1", "ffn1_w3", "ffn1_w2", "w_in", "w_out", "ffn2_w1", "ffn2_w3", "ffn2_w2")
TRANSPOSED = ("ffn1_w1", "ffn1_w3", "w_in", "ffn2_w1", "ffn2_w3")
SMALL = ("ffn1_norm", "mix_norm", "conf_conv_w", "conf_conv_b", "conf_ln_g", "conf_ln_b", "sconv_w", "pool_w", "pool_scale",
         "gmlp_ln_g", "gmlp_ln_b", "gmlp_w_s", "gmlp_b_s", "ffn2_norm", "final_norm")
ORDER = ("ffn1_norm", "ffn1_w1", "ffn1_w3", "ffn1_w2", "mix_norm", "w_in", "conf_conv_w", "conf_conv_b", "conf_ln_g", "conf_ln_b",
         "sconv_w", "pool_w", "pool_scale", "gmlp_ln_g", "gmlp_ln_b", "gmlp_w_s", "gmlp_b_s", "w_out", "ffn2_norm", "ffn2_w1",
         "ffn2_w3", "ffn2_w2", "final_norm")


def _as2d(a):
    return a.reshape(-1, a.shape[-1])


def kernel(x, ffn1_norm, ffn1_w1, ffn1_w3, ffn1_w2, mix_norm, w_in, conf_conv_w, conf_conv_b, conf_ln_g, conf_ln_b, sconv_w, pool_w, pool_scale, gmlp_ln_g, gmlp_ln_b, gmlp_w_s, gmlp_b_s, w_out, ffn2_norm, ffn2_w1, ffn2_w3, ffn2_w2, final_norm, loss_target, m_ffn1_norm, m_ffn1_w1, m_ffn1_w3, m_ffn1_w2, m_mix_norm, m_w_in, m_conf_conv_w, m_conf_conv_b, m_conf_ln_g, m_conf_ln_b, m_sconv_w, m_pool_w, m_pool_scale, m_gmlp_ln_g, m_gmlp_ln_b, m_gmlp_w_s, m_gmlp_b_s, m_w_out, m_ffn2_norm, m_ffn2_w1, m_ffn2_w3, m_ffn2_w2, m_final_norm, v_ffn1_norm, v_ffn1_w1, v_ffn1_w3, v_ffn1_w2, v_mix_norm, v_w_in, v_conf_conv_w, v_conf_conv_b, v_conf_ln_g, v_conf_ln_b, v_sconv_w, v_pool_w, v_pool_scale, v_gmlp_ln_g, v_gmlp_ln_b, v_gmlp_w_s, v_gmlp_b_s, v_w_out, v_ffn2_norm, v_ffn2_w1, v_ffn2_w3, v_ffn2_w2, v_final_norm):
    given = dict(locals())
    w = {k: given[k] for k in ORDER}
    mom = {k: given["m_" + k] for k in ORDER}
    var = {k: given["v_" + k] for k in ORDER}
    n_l = ffn1_w1.shape[0]
    xs = x[0]
    d = xs.shape[1]
    chip = 2 * lax.axis_index("x") + lax.axis_index("y")

    def shard(name, l):
        a = w[name][l]
        return (jnp.swapaxes(w[name], 1, 2)[l] if name in TRANSPOSED else a).astype(BF16)

    groups = (BIG[:3], BIG[3:])

    def shards_of(l, g):
        return [shard(name, l) for name in groups[g]] if l < n_l else []

    def finish_gather(l, g, lands):
        out = _sibling_gather(shards_of(l, g), lands)
        return {name: a.reshape(-1, d) for name, a in zip(groups[g], out)}

    shard_rows = [w[name].shape[2] if name in TRANSPOSED else w[name].shape[1] for name in BIG]

    conv_shapes = [conf_conv_w.shape, sconv_w.shape]
    conv_all = _all_gather_small(_pack([conf_conv_w, sconv_w]), reduce=False)
    conv_all = conv_all.reshape(N_CHIPS, 2, -1)[:, 0]
    conf_full, sconv_full = [jnp.concatenate([_unpack(conv_all[k], conv_shapes)[a] for k in range(N_CHIPS)], axis=-1)
                             for a in range(2)]

    lane = jnp.arange(C) // HEAD_DIM
    head_rows = (jnp.arange(8)[:, None] == lane[None, :]).astype(F32)
    tril = jnp.tril(jnp.ones((CHUNK, CHUNK), F32))
    tril4 = jnp.tile(tril, (N_HEADS, 1))
    mixer_consts = []
    for l in range(n_l):
        cw = jnp.pad(conf_full[l], ((0, 32 - CONF_KERNEL), (0, 0)))
        vec = jnp.concatenate([conf_conv_b[l][None], conf_ln_g[l][None], conf_ln_b[l][None], pool_scale[l][None],
                               gmlp_ln_g[l][None], gmlp_ln_b[l][None], sconv_full[l], jnp.zeros((7, C), F32)], axis=0)
        eye = jnp.eye(len(pool_w[l]), dtype=F32)
        pool_blk = (eye[:, None, :, None] * pool_w[l][:, :, None, :]).reshape(C, C).astype(BF16)
        ws = gmlp_w_s[l] * tril[None]
        wstack = ws.reshape(N_HEADS * CHUNK, CHUNK).astype(BF16)
        wstack_t = jnp.swapaxes(ws, 1, 2).reshape(N_HEADS * CHUNK, CHUNK).astype(BF16)
        bias = jnp.repeat(gmlp_b_s[l].T, HEAD_DIM, axis=1)
        mixer_consts.append((cw, vec, pool_blk, wstack, wstack_t, bias))

    saved = []
    cur = xs
    gathered = [finish_gather(0, 0, _ici_gather(shards_of(0, 0)))]
    for l in range(n_l):
        gw = gathered[l]
        cw, vec, pool_blk, wstack, wstack_t, bias = mixer_consts[l]
        x0 = cur
        x1, a1, b1, h1, lands = _ffn_fwd(x0, ffn1_norm[l][None], gw["ffn1_w1"], gw["ffn1_w3"], gw["ffn1_w2"], shards_of(l, 1))
        gw.update(finish_gather(l, 1, lands))
        p = _proj_fwd(x1, mix_norm[l][None], gw["w_in"])
        x2, mix, z = _mixer_fwd(p, x1, cw, vec, pool_blk, wstack, bias, gw["w_out"])
        x3, a2, b2, h2, lands = _ffn_fwd(x2, ffn2_norm[l][None], gw["ffn2_w1"], gw["ffn2_w3"], gw["ffn2_w2"], shards_of(l + 1, 0))
        if l + 1 < n_l:
            gathered.append(finish_gather(l + 1, 0, lands))
        saved.append((x0, x1, x2, a1, b1, a2, b2, p, mix, z, h1, h2))
        cur = x3

    dx, dg_final, loss_part = _loss_bwd(cur, final_norm[None], loss_target[0])

    small_parts = [None] * n_l
    reduced_halves = [[None, None] for _ in range(n_l)]
    pair_sum = lambda big, g: _pair_sum([big[name].reshape(N_CHIPS, -1, d) for name in groups[g]])
    pending = None
    for l in reversed(range(n_l)):
        gw = gathered[l]
        cw, vec, pool_blk, wstack, wstack_t, bias = mixer_consts[l]
        x0, x1, x2, a1, b1, a2, b2, p, mix, z, h1, h2 = saved[l]
        big = {}
        out = _ffn_bwd(dx, x2, ffn2_norm[l][None], a2, b2, gw["ffn2_w1"], gw["ffn2_w3"], gw["ffn2_w2"], pending)
        dx, dg_ffn2, dy, da, db, u, dx_bf = out[:7]
        if pending is not None:
            reduced_halves[l + 1][0] = out[7]
        big["ffn2_w1"], big["ffn2_w3"], big["ffn2_w2"] = _dw(da, h2), _dw(db, h2), _dw(u, dy)
        big["w_out"] = _dw(mix, dx_bf)
        dp, dcw, dvec, dpool, dws, dbs = _mixer_bwd(dx, p, z, cw, vec, pool_blk, wstack, wstack_t, bias, tril4, head_rows, gw["w_out"])
        dx, dg_mix, h = _proj_bwd(dx, x1, mix_norm[l][None], dp, gw["w_in"])
        big["w_in"] = _dw(dp, h)
        second = pair_sum(big, 1)
        out = _ffn_bwd(dx, x0, ffn1_norm[l][None], a1, b1, gw["ffn1_w1"], gw["ffn1_w3"], gw["ffn1_w2"], second)
        dx, dg_ffn1, dy, da, db, u = out[:6]
        reduced_halves[l][1] = out[7]
        big["ffn1_w1"], big["ffn1_w3"], big["ffn1_w2"] = _dw(da, h1), _dw(db, h1), _dw(u, dy)
        small_parts[l] = [dg_ffn1[0], dg_mix[0], dg_ffn2[0], dcw, dvec, dpool, dws, dbs]
        pending = pair_sum(big, 0)
    reduced_halves[0][0] = _reduce_chips(pending)
    grad_x = dx[None]

    index_of = {name: a for a, name in enumerate(BIG)}
    full = dict(zip(BIG, _sum_share(reduced_halves, [[index_of[name] for name in g] for g in groups], shard_rows)))
    grad = {name: (jnp.swapaxes(full[name], 1, 2) if name in TRANSPOSED else full[name]) for name in BIG}

    part_shapes = [a.shape for a in small_parts[0]]
    tail = [dg_final[0], loss_part[0]]
    packed = _pack([a for l in range(n_l) for a in small_parts[l]] + tail)
    summed = _unpack(_all_gather_small(packed, reduce=True), part_shapes * n_l + [a.shape for a in tail])
    per_layer = [summed[l * len(part_shapes):(l + 1) * len(part_shapes)] for l in range(n_l)]
    stack = lambda k: jnp.stack([per_layer[l][k] for l in range(n_l)])
    dcw_all, dvec_all, dpool_all, dws_all, dbs_all = stack(3), stack(4), stack(5), stack(6), stack(7)
    loss = summed[-1][0]
    chip_cols = lambda a: lax.dynamic_slice_in_dim(a, chip * (C // N_CHIPS), C // N_CHIPS, axis=2)
    n_pool = pool_w.shape[1]
    grad.update(
        ffn1_norm=stack(0), mix_norm=stack(1), ffn2_norm=stack(2), final_norm=summed[-2],
        conf_conv_w=chip_cols(dcw_all[:, :CONF_KERNEL]), conf_conv_b=dvec_all[:, 0], conf_ln_g=dvec_all[:, 1],
        conf_ln_b=dvec_all[:, 2], pool_scale=dvec_all[:, 3], gmlp_ln_g=dvec_all[:, 4], gmlp_ln_b=dvec_all[:, 5],
        sconv_w=chip_cols(dvec_all[:, 6:6 + SHORT_KERNEL]),
        pool_w=jnp.stack([dpool_all[:, g * POOL_GROUP:(g + 1) * POOL_GROUP, g * POOL_GROUP:(g + 1) * POOL_GROUP]
                          for g in range(n_pool)], axis=1),
        gmlp_w_s=dws_all.reshape(n_l, N_HEADS, CHUNK, CHUNK), gmlp_b_s=dbs_all[:, :N_HEADS],
    )

    delta, new_m, new_v = {}, {}, {}
    for name in BIG:
        rows_of = (lambda a: jnp.swapaxes(a, 1, 2)) if name in TRANSPOSED else (lambda a: a)
        shp = full[name].shape
        out = _adamw(_as2d(rows_of(w[name])), _as2d(full[name]), _as2d(rows_of(mom[name])), _as2d(rows_of(var[name])))
        delta[name], new_m[name], new_v[name], grad[name] = (rows_of(o.reshape(shp)) for o in out)
    ds, ms, vs = _adamw_small([_as2d(w[k]) if w[k].ndim > 1 else w[k][None] for k in SMALL],
                              [_as2d(grad[k]) if grad[k].ndim > 1 else grad[k][None] for k in SMALL],
                              [_as2d(mom[k]) if mom[k].ndim > 1 else mom[k][None] for k in SMALL],
                              [_as2d(var[k]) if var[k].ndim > 1 else var[k][None] for k in SMALL])
    for k, dl, mo, vo in zip(SMALL, ds, ms, vs):
        delta[k], new_m[k], new_v[k] = dl.reshape(w[k].shape), mo.reshape(w[k].shape), vo.reshape(w[k].shape)

    return (loss, grad_x, *[grad[k] for k in ORDER], *[delta[k] for k in ORDER], *[new_m[k] for k in ORDER],
            *[new_v[k] for k in ORDER])
```

```python
import functools

import jax
import jax.numpy as jnp
from jax import lax
from jax.experimental import pallas as pl
from jax.experimental.pallas import tpu as pltpu

F32 = jnp.float32
BF16 = jnp.bfloat16
MESH = pl.DeviceIdType.MESH
ANY = pl.BlockSpec(memory_space=pl.ANY)

EPS = 1e-6
FFN_RESIDUAL = 0.5
D_GROUP = 256
CONF_KERNEL = 31
SHORT_KERNEL = 3
POOL_GROUP = 64
CHUNK = 128
N_HEADS = 4
HEAD_DIM = 64
HALO = 32
N_CHIPS = 4
N_DEV = 8
LANES = 128
MXU_TILE = 256
VMEM_LIMIT = 56 * 2**20
T_MIX_FWD = 512
T_MIX_BWD = 256

ADAM_LR = 0.001
ADAM_B1 = 0.9
ADAM_B2 = 0.999
ADAM_EPS = 1e-08
ADAM_WD = 0.01
ADAM_STEP = 10

NT = (((1,), (1,)), ((), ()))
TN = (((0,), (0,)), ((), ()))


def _params(*sem):
    return pltpu.CompilerParams(dimension_semantics=sem, vmem_limit_bytes=VMEM_LIMIT)


def _dot(a, b):
    return jnp.dot(a, b, preferred_element_type=F32)


def _dot_nt(a, b):
    return lax.dot_general(a, b, NT, preferred_element_type=F32)


def _t_bf16(v):
    return jnp.transpose(v).astype(BF16)


def _sigmoid(v):
    return 1.0 / (1.0 + jnp.exp(-v))


def _rms(x, g):
    r = lax.rsqrt(jnp.mean(x * x, axis=-1, keepdims=True) + EPS)
    n = x * r
    return n, r, n * g


def _rms_bwd(dh, n, r, g):
    dn = dh * g
    dx = r * (dn - n * jnp.mean(dn * n, axis=-1, keepdims=True))
    return dx, jnp.sum(dh * n, axis=0, keepdims=True)


def _ln_fwd(z, g, b):
    mu = jnp.mean(z, axis=-1, keepdims=True)
    zc = z - mu
    rs = lax.rsqrt(jnp.mean(zc * zc, axis=-1, keepdims=True) + EPS)
    zn = zc * rs
    return zn, rs, zn * g + b


def _ln_bwd(dl, zn, rs, g):
    dzn = dl * g
    dz = rs * (dzn - jnp.mean(dzn, axis=-1, keepdims=True) - zn * jnp.mean(dzn * zn, axis=-1, keepdims=True))
    return dz, jnp.sum(dl * zn, axis=0, keepdims=True), jnp.sum(dl, axis=0, keepdims=True)


def _tile(n, want):
    return want if n % want == 0 else n


def _resident(shape):
    return pl.BlockSpec(shape, lambda i: (0,) * len(shape), pipeline_mode=pl.Buffered(1))


def _ffn_fwd(x, g, w1t, w3t, w2, shards=()):
    s, d = x.shape
    f = w1t.shape[0]
    tm, tf = _tile(s, 512), MXU_TILE
    nj = f // tf
    ni = s // tm
    n_c = len(shards)

    def body(*refs):
        x_ref, g_ref, w1_ref, w3_ref, w2_ref = refs[:5]
        xo_ref, a_ref, b_ref, h_ref = refs[5 + n_c:9 + n_c]
        u_s = refs[9 + 2 * n_c]
        if n_c:
            gather = lambda: _ici_gather_copies(refs[5:5 + n_c], refs[9 + n_c:9 + 2 * n_c], refs[10 + 2 * n_c], refs[11 + 2 * n_c])

            @pl.when(pl.program_id(0) == 0)
            def _():
                for cp in gather():
                    cp.start()

        h = _rms(x_ref[...], g_ref[...])[2].astype(BF16)
        h_ref[...] = h
        for j in range(nj):
            cols = slice(j * tf, (j + 1) * tf)
            a = _dot_nt(h, w1_ref[cols, :])
            b = _dot_nt(h, w3_ref[cols, :])
            a_ref[:, cols] = a.astype(BF16)
            b_ref[:, cols] = b.astype(BF16)
            u_s[:, cols] = ((a * _sigmoid(a)) * b).astype(BF16)
        xo_ref[...] = x_ref[...] + FFN_RESIDUAL * _dot(u_s[...], w2_ref[...])
        if n_c:
            @pl.when(pl.program_id(0) == ni - 1)
            def _():
                _wait_all(gather())

    row = pl.BlockSpec((tm, d), lambda i: (i, 0))
    hid = pl.BlockSpec((tm, f), lambda i: (i, 0))
    sems = [pltpu.SemaphoreType.DMA((3 * n_c,))] * 2 if n_c else []
    out = pl.pallas_call(
        body, name="ffn_fwd_gather" if n_c else "ffn_fwd", grid=(ni,),
        in_specs=[row, _resident(g.shape), _resident(w1t.shape), _resident(w3t.shape), _resident(w2.shape)] + [ANY] * n_c,
        out_specs=[row, hid, hid, row] + [ANY] * n_c,
        out_shape=[jax.ShapeDtypeStruct((s, d), F32), jax.ShapeDtypeStruct((s, f), BF16), jax.ShapeDtypeStruct((s, f), BF16),
                   jax.ShapeDtypeStruct((s, d), BF16)] + _landing_shapes(shards),
        scratch_shapes=[pltpu.VMEM((tm, f), BF16)] + sems,
        compiler_params=_params("arbitrary"),
    )(x, g, w1t, w3t, w2, *shards)
    return out[0], out[1], out[2], out[3], list(out[4:])


def _ffn_bwd(dxo, x, g, a, b, w1t, w3t, w2, sums=None):
    s, d = x.shape
    f = w1t.shape[0]
    tm, tf = _tile(s, 256), MXU_TILE
    nj = f // tf
    ni = s // tm
    n_c = 0 if sums is None else 1

    def body(*refs):
        dxo_ref, x_ref, g_ref, a_ref, b_ref, w1_ref, w3_ref, w2_ref = refs[:8]
        dx_ref, dg_ref, dy_ref, da_ref, db_ref, u_ref, dxb_ref = refs[8 + n_c:15 + n_c]
        if n_c:
            exchange = lambda: _ici_reduce_copies(refs[8], refs[15 + n_c], *refs[15 + 2 * n_c:])

        @pl.when(pl.program_id(0) == 0)
        def _():
            dg_ref[...] = jnp.zeros_like(dg_ref)
            if n_c:
                copies, local = exchange()
                local.start()
                for cp in copies:
                    cp.start()

        dy = (FFN_RESIDUAL * dxo_ref[...]).astype(BF16)
        dy_ref[...] = dy
        for j in range(nj):
            cols = slice(j * tf, (j + 1) * tf)
            du = _dot_nt(dy, w2_ref[cols, :]).astype(BF16)
            av = a_ref[:, cols]
            bv = b_ref[:, cols]
            sg = _sigmoid(av.astype(F32)).astype(BF16)
            sl = av * sg
            da_ref[:, cols] = (du * bv) * (sg + sl * (1.0 - sg))
            db_ref[:, cols] = du * sl
            u_ref[:, cols] = sl * bv
        dh = _dot(da_ref[...], w1_ref[...]) + _dot(db_ref[...], w3_ref[...])
        n, r, _ = _rms(x_ref[...], g_ref[...])
        dxr, dg = _rms_bwd(dh, n, r, g_ref[...])
        dx = dxo_ref[...] + dxr
        dx_ref[...] = dx
        dxb_ref[...] = dx.astype(BF16)
        dg_ref[0:1, :] += dg
        if n_c:
            @pl.when(pl.program_id(0) == ni - 1)
            def _():
                copies, local = exchange()
                _wait_all(copies)
                local.wait()

    row = pl.BlockSpec((tm, d), lambda i: (i, 0))
    hid = pl.BlockSpec((tm, f), lambda i: (i, 0))
    extra = [] if sums is None else [sums]
    sems = [pltpu.SemaphoreType.DMA((3,)), pltpu.SemaphoreType.DMA((3,)), pltpu.SemaphoreType.DMA] if n_c else []
    out = pl.pallas_call(
        body, name="ffn_bwd_reduce" if n_c else "ffn_bwd", grid=(ni,),
        in_specs=[row, row, _resident(g.shape), hid, hid, _resident(w1t.shape), _resident(w3t.shape), _resident(w2.shape)]
        + [ANY] * n_c,
        out_specs=[row, pl.BlockSpec((8, d), lambda i: (0, 0)), row, hid, hid, hid, row] + [ANY] * n_c,
        out_shape=[jax.ShapeDtypeStruct((s, d), F32), jax.ShapeDtypeStruct((8, d), F32), jax.ShapeDtypeStruct((s, d), BF16),
                   jax.ShapeDtypeStruct((s, f), BF16), jax.ShapeDtypeStruct((s, f), BF16), jax.ShapeDtypeStruct((s, f), BF16),
                   jax.ShapeDtypeStruct((s, d), BF16)]
        + [jax.ShapeDtypeStruct(e.shape, e.dtype) for e in extra],
        scratch_shapes=sems,
        compiler_params=_params("arbitrary"),
    )(dxo, x, g, a, b, w1t, w3t, w2, *extra)
    return out


def _dw(am, bm):
    s, r = am.shape
    n = bm.shape[1]
    rb = r // 2 if r > 1024 and (r // 2) % LANES == 0 else r
    ts = _tile(s, 2048)
    ni = s // ts

    def body(am_ref, bm_ref, o_ref, acc_s):
        i = pl.program_id(1)

        @pl.when(i == 0)
        def _():
            acc_s[...] = jnp.zeros_like(acc_s)

        acc_s[...] += lax.dot_general(am_ref[...], bm_ref[...], TN, preferred_element_type=F32)

        @pl.when(i == ni - 1)
        def _():
            o_ref[...] = acc_s[...].astype(BF16)

    return pl.pallas_call(
        body, name="dw", grid=(r // rb, ni),
        in_specs=[pl.BlockSpec((ts, rb), lambda k, i: (i, k)), pl.BlockSpec((ts, n), lambda k, i: (i, 0))],
        out_specs=pl.BlockSpec((rb, n), lambda k, i: (k, 0)),
        out_shape=jax.ShapeDtypeStruct((r, n), BF16),
        scratch_shapes=[pltpu.VMEM((rb, n), F32)],
        compiler_params=_params("arbitrary", "arbitrary"),
    )(am, bm)


def _proj_fwd(x, g, w_int):
    s, d = x.shape
    f = w_int.shape[0]
    tm = _tile(s, 512)

    def body(x_ref, g_ref, w_ref, p_ref):
        p_ref[...] = _dot_nt(_rms(x_ref[...], g_ref[...])[2].astype(BF16), w_ref[...])

    return pl.pallas_call(
        body, name="proj_fwd", grid=(s // tm,),
        in_specs=[pl.BlockSpec((tm, d), lambda i: (i, 0)), _resident(g.shape), _resident(w_int.shape)],
        out_specs=pl.BlockSpec((tm, f), lambda i: (i, 0)),
        out_shape=jax.ShapeDtypeStruct((s, f), F32),
        compiler_params=_params("arbitrary"),
    )(x, g, w_int)


def _proj_bwd(dxo, x, g, dp, w_int):
    s, d = x.shape
    f = w_int.shape[0]
    tm = _tile(s, 512)

    def body(dxo_ref, x_ref, g_ref, dp_ref, w_ref, dx_ref, dg_ref, h_ref):
        @pl.when(pl.program_id(0) == 0)
        def _():
            dg_ref[...] = jnp.zeros_like(dg_ref)

        n, r, h = _rms(x_ref[...], g_ref[...])
        h_ref[...] = h.astype(BF16)
        dxr, dg = _rms_bwd(_dot(dp_ref[...], w_ref[...]), n, r, g_ref[...])
        dx_ref[...] = dxo_ref[...] + dxr
        dg_ref[0:1, :] += dg

    row = pl.BlockSpec((tm, d), lambda i: (i, 0))
    return pl.pallas_call(
        body, name="proj_bwd", grid=(s // tm,),
        in_specs=[row, row, _resident(g.shape), pl.BlockSpec((tm, f), lambda i: (i, 0)), _resident(w_int.shape)],
        out_specs=[row, pl.BlockSpec((8, d), lambda i: (0, 0)), row],
        out_shape=[jax.ShapeDtypeStruct((s, d), F32), jax.ShapeDtypeStruct((8, d), F32), jax.ShapeDtypeStruct((s, d), BF16)],
        compiler_params=_params("arbitrary"),
    )(dxo, x, g, dp, w_int)


C = D_GROUP


def _piece(ref, k):
    return ref[:, k * C:(k + 1) * C]


def _up(v, r):
    return v if r == 0 else pltpu.roll(v, v.shape[0] - r, 0)


def _down(v, r):
    return v if r == 0 else pltpu.roll(v, r, 0)


def _lane_group():
    lane = lax.broadcasted_iota(jnp.int32, (1, C), 1)
    return (lane >= POOL_GROUP).astype(jnp.int32) + (lane >= 2 * POOL_GROUP).astype(jnp.int32) + (
        lane >= 3 * POOL_GROUP).astype(jnp.int32)


def _by_group(grp, v2, v4, v8, v16):
    return jnp.where(grp == 0, v2, jnp.where(grp == 1, v4, jnp.where(grp == 2, v8, v16)))


def _pool_count(grp, row0, t):
    pos = (row0 + lax.broadcasted_iota(jnp.int32, (t, C), 0) + 1).astype(F32)
    return jnp.minimum(pos, _by_group(grp, 2.0, 4.0, 8.0, 16.0))


def _trailing_sums(ext, grp, t):
    s2 = ext + _down(ext, 1)
    s4 = s2 + _down(s2, 2)
    s8 = s4 + _down(s4, 4)
    s16 = s8 + _down(s8, 8)
    return _by_group(grp, s2, s4, s8, s16)[HALO:HALO + t]


def _leading_sums(ext, grp, t):
    s2 = ext + _up(ext, 1)
    s4 = s2 + _up(s2, 2)
    s8 = s4 + _up(s4, 4)
    s16 = s8 + _up(s8, 8)
    return _by_group(grp, s2, s4, s8, s16)[0:t]


def _head_select(r4, grp):
    assert HEAD_DIM == POOL_GROUP and N_HEADS == 4
    return _by_group(grp, *(r4[h * CHUNK:(h + 1) * CHUNK] for h in range(N_HEADS)))


def _conv_taps():
    return [(k, (k + 2) % 8, (k + 2) - (k + 2) % 8) for k in range(CONF_KERNEL)]


def _mixer_fwd(p, x1, cw, vec, pool_w, wstack, bias, w_out):
    s, d = x1.shape
    t = _tile(s, T_MIX_FWD)
    n_ext = t + HALO
    dm = w_out.shape[0]

    def body(p_ref, x1_ref, cw_ref, vec_ref, pw_ref, ws_ref, bias_ref, wo_ref, x2_ref, mix_s, z_ref, cy_s, cq_s, cx_s):
        i = pl.program_id(0)

        @pl.when(i == 0)
        def _():
            cy_s[...] = jnp.zeros_like(cy_s)
            cq_s[...] = jnp.zeros_like(cq_s)
            cx_s[...] = jnp.zeros_like(cx_s)

        grp = _lane_group()
        y = _piece(p_ref, 0) * _sigmoid(_piece(p_ref, 1))
        ext = jnp.concatenate([cy_s[...], y], axis=0)
        cy_s[...] = y[t - HALO:t]
        z = jnp.broadcast_to(vec_ref[0:1, :], (t, C))
        shifted = {}
        for k, r, off in _conv_taps():
            if r not in shifted:
                shifted[r] = _up(ext, r)
            z = z + cw_ref[k:k + 1, :] * shifted[r][off:off + t]
        z_ref[...] = z
        ln = _ln_fwd(z, vec_ref[1:2, :], vec_ref[2:3, :])[2]
        mix_s[:, 0:C] = (ln * _sigmoid(ln)).astype(BF16)
        q = _piece(p_ref, 3) * _piece(p_ref, 4)
        ext = jnp.concatenate([cq_s[...], q], axis=0)
        cq_s[...] = q[t - HALO:t]
        cz = vec_ref[8:9, :] * q + vec_ref[7:8, :] * _down(ext, 1)[HALO:] + vec_ref[6:7, :] * _down(ext, 2)[HALO:]
        mix_s[:, C:2 * C] = (_piece(p_ref, 2) * cz).astype(BF16)
        xp = _piece(p_ref, 5)
        ext = jnp.concatenate([cx_s[...], xp], axis=0)
        cx_s[...] = xp[t - HALO:t]
        dd = _trailing_sums(ext, grp, t) / _pool_count(grp, i * t, t) - xp
        mix_s[:, 2 * C:3 * C] = (_dot(dd.astype(BF16), pw_ref[...]) * vec_ref[3:4, :]).astype(BF16)
        vln = _ln_fwd(_piece(p_ref, 7), vec_ref[4:5, :], vec_ref[5:6, :])[2].astype(BF16)
        for n in range(t // CHUNK):
            rows = slice(n * CHUNK, (n + 1) * CHUNK)
            mixed = _head_select(_dot(ws_ref[...], vln[rows]), grp) + bias_ref[...]
            mix_s[rows, 3 * C:4 * C] = (p_ref[rows, 6 * C:7 * C] * mixed).astype(BF16)
        x2_ref[...] = x1_ref[...] + _dot(mix_s[...], wo_ref[...])

    full = lambda a: pl.BlockSpec(a.shape, lambda i: (0, 0))
    return pl.pallas_call(
        body, name="mixer_fwd", grid=(s // t,),
        in_specs=[pl.BlockSpec((t, p.shape[1]), lambda i: (i, 0)), pl.BlockSpec((t, d), lambda i: (i, 0)),
                  full(cw), full(vec), full(pool_w), full(wstack), full(bias), full(w_out)],
        out_specs=[pl.BlockSpec((t, d), lambda i: (i, 0)), pl.BlockSpec((t, dm), lambda i: (i, 0)),
                   pl.BlockSpec((t, C), lambda i: (i, 0))],
        out_shape=[jax.ShapeDtypeStruct((s, d), F32), jax.ShapeDtypeStruct((s, dm), BF16), jax.ShapeDtypeStruct((s, C), F32)],
        scratch_shapes=[pltpu.VMEM((HALO, C), F32)] * 3,
        compiler_params=_params("arbitrary"),
    )(p, x1, cw, vec, pool_w, wstack, bias, w_out)


def _mixer_bwd(dx2, p, z, cw, vec, pool_w, wstack, wstack_t, bias, tril4, head_rows, w_out):
    s, d = dx2.shape
    t = _tile(s, T_MIX_BWD)
    nt = s // t
    n_ext = t + HALO
    hb = t // HALO

    def body(dx2_ref, p_ref, ph_ref, z_ref, cw_ref, vec_ref, pw_ref, ws_ref, wst_ref, bias_ref, tril_ref, hr_ref, wo_ref,
             dp_ref, dcw_ref, dvec_ref, dpool_ref, dws_ref, dbs_ref, cdz_s, cdc_s, cf_s, vy_s, dvl_s, dbias_s):
        i = pl.program_id(0)
        tile = nt - 1 - i

        @pl.when(i == 0)
        def _():
            for ref in (cdz_s, cdc_s, cf_s, dbias_s, dcw_ref, dvec_ref, dpool_ref, dws_ref, dbs_ref):
                ref[...] = jnp.zeros_like(ref)

        grp = _lane_group()
        first = jnp.where(tile > 0, 1.0, 0.0)
        dmix = _dot_nt(dx2_ref[...].astype(BF16), wo_ref[...])
        d_a, d_b, d_c, d_d = (dmix[:, k * C:(k + 1) * C] for k in range(4))

        def acc_vec(row, v):
            dvec_ref[row:row + 1, :] += jnp.sum(v, axis=0, keepdims=True)

        val, gate = _piece(p_ref, 0), _piece(p_ref, 1)
        sgate = _sigmoid(gate)
        y = val * sgate
        y_halo = ph_ref[:, 0:C] * _sigmoid(ph_ref[:, C:2 * C]) * first
        ext = jnp.concatenate([y_halo, y], axis=0)
        for r in range(8):
            vy_s[r] = _up(ext, r)
        zn, rs, ln = _ln_fwd(z_ref[...], vec_ref[1:2, :], vec_ref[2:3, :])
        sg = _sigmoid(ln)
        dln = d_a * (sg * (1.0 + ln * (1.0 - sg)))
        dz, dg, db = _ln_bwd(dln, zn, rs, vec_ref[1:2, :])
        dvec_ref[1:2, :] += dg
        dvec_ref[2:3, :] += db
        acc_vec(0, dz)
        for k, r, off in _conv_taps():
            dcw_ref[k:k + 1, :] += jnp.sum(dz * vy_s[r, off:off + t, :], axis=0, keepdims=True)
        ext = jnp.concatenate([dz, cdz_s[...]], axis=0)
        cdz_s[...] = dz[0:HALO]
        dy = jnp.zeros((t, C), F32)
        shifted = {}
        for k in range(CONF_KERNEL):
            m = CONF_KERNEL - 1 - k
            r, off = m % 8, m - m % 8
            if r not in shifted:
                shifted[r] = _up(ext, r)
            dy = dy + cw_ref[k:k + 1, :] * shifted[r][off:off + t]
        dp_ref[:, 0:C] = (dy * sgate).astype(BF16)
        dp_ref[:, C:2 * C] = (dy * val * sgate * (1.0 - sgate)).astype(BF16)

        sb, sc, sx = _piece(p_ref, 2), _piece(p_ref, 3), _piece(p_ref, 4)
        q = sc * sx
        q_halo = ph_ref[:, 3 * C:4 * C] * ph_ref[:, 4 * C:5 * C] * first
        ext = jnp.concatenate([q_halo, q], axis=0)
        q1, q2 = _down(ext, 1)[HALO:], _down(ext, 2)[HALO:]
        cz = vec_ref[8:9, :] * q + vec_ref[7:8, :] * q1 + vec_ref[6:7, :] * q2
        dcz = d_b * sb
        dp_ref[:, 2 * C:3 * C] = (d_b * cz).astype(BF16)
        acc_vec(8, dcz * q)
        acc_vec(7, dcz * q1)
        acc_vec(6, dcz * q2)
        ext = jnp.concatenate([dcz, cdc_s[...]], axis=0)
        cdc_s[...] = dcz[0:HALO]
        dq = vec_ref[8:9, :] * dcz + vec_ref[7:8, :] * _up(ext, 1)[0:t] + vec_ref[6:7, :] * _up(ext, 2)[0:t]
        dp_ref[:, 3 * C:4 * C] = (dq * sx).astype(BF16)
        dp_ref[:, 4 * C:5 * C] = (dq * sc).astype(BF16)

        xp = _piece(p_ref, 5)
        ext = jnp.concatenate([ph_ref[:, 5 * C:6 * C] * first, xp], axis=0)
        cnt = _pool_count(grp, tile * t, t)
        dd = (_trailing_sums(ext, grp, t) / cnt - xp).astype(BF16)
        e2 = _dot(dd, pw_ref[...])
        acc_vec(3, d_c * e2)
        de = (d_c * vec_ref[3:4, :]).astype(BF16)
        dpool_ref[...] += _dot(_t_bf16(dd.astype(F32)), de)
        ddd = _dot_nt(de, pw_ref[...])
        fq = ddd / cnt
        ext = jnp.concatenate([fq, cf_s[...]], axis=0)
        cf_s[...] = fq[0:HALO]
        dp_ref[:, 5 * C:6 * C] = (_leading_sums(ext, grp, t) - ddd).astype(BF16)

        vn, vrs, vlnf = _ln_fwd(_piece(p_ref, 7), vec_ref[4:5, :], vec_ref[5:6, :])
        vln = vlnf.astype(BF16)
        for n in range(t // CHUNK):
            rows = slice(n * CHUNK, (n + 1) * CHUNK)
            mixed = _head_select(_dot(ws_ref[...], vln[rows]), grp) + bias_ref[...]
            dd_n = d_d[rows]
            dp_ref[rows, 6 * C:7 * C] = (dd_n * mixed).astype(BF16)
            dmx = dd_n * p_ref[rows, 6 * C:7 * C]
            dbias_s[...] += dmx
            dmx_b = dmx.astype(BF16)
            dvl_s[rows, :] = _head_select(_dot(wst_ref[...], dmx_b), grp)
            for h in range(N_HEADS):
                hrows = slice(h * CHUNK, (h + 1) * CHUNK)
                dws_ref[hrows, :] += _dot_nt(jnp.where(grp == h, dmx_b, jnp.zeros_like(dmx_b)), vln[rows])
        dvl = dvl_s[...]
        dv, dg, db = _ln_bwd(dvl, vn, vrs, vec_ref[4:5, :])
        dvec_ref[4:5, :] += dg
        dvec_ref[5:6, :] += db
        dp_ref[:, 7 * C:8 * C] = dv.astype(BF16)

        @pl.when(i == nt - 1)
        def _():
            dws_ref[...] = dws_ref[...] * tril_ref[...]
            dbs_ref[...] = lax.dot_general(hr_ref[...], dbias_s[...], NT, precision=lax.Precision.HIGHEST,
                                           preferred_element_type=F32)

    full = lambda a: pl.BlockSpec(a.shape, lambda i: (0, 0))
    acc = lambda shape: pl.BlockSpec(shape, lambda i: (0, 0))
    f = p.shape[1]
    return pl.pallas_call(
        body, name="mixer_bwd", grid=(nt,),
        in_specs=[pl.BlockSpec((t, d), lambda i: (nt - 1 - i, 0)), pl.BlockSpec((t, f), lambda i: (nt - 1 - i, 0)),
                  pl.BlockSpec((HALO, f), lambda i: (jnp.maximum((nt - 1 - i) * hb - 1, 0), 0)),
                  pl.BlockSpec((t, C), lambda i: (nt - 1 - i, 0)), full(cw), full(vec), full(pool_w), full(wstack), full(wstack_t), full(bias), full(tril4), full(head_rows),
                  full(w_out)],
        out_specs=[pl.BlockSpec((t, f), lambda i: (nt - 1 - i, 0)), acc((32, C)), acc((16, C)), acc((C, C)),
                   acc((N_HEADS * CHUNK, CHUNK)), acc((8, CHUNK))],
        out_shape=[jax.ShapeDtypeStruct((s, f), BF16), jax.ShapeDtypeStruct((32, C), F32), jax.ShapeDtypeStruct((16, C), F32),
                   jax.ShapeDtypeStruct((C, C), F32), jax.ShapeDtypeStruct((N_HEADS * CHUNK, CHUNK), F32),
                   jax.ShapeDtypeStruct((8, CHUNK), F32)],
        scratch_shapes=[pltpu.VMEM((HALO, C), F32)] * 3 + [pltpu.VMEM((8, n_ext, C), F32), pltpu.VMEM((t, C), F32),
                                                            pltpu.VMEM((CHUNK, C), F32)],
        compiler_params=_params("arbitrary"),
    )(dx2, p, p, z, cw, vec, pool_w, wstack, wstack_t, bias, tril4, head_rows, w_out)


def _loss_bwd(x, g, target):
    s, d = x.shape
    tm = _tile(s, 512)

    def body(x_ref, g_ref, t_ref, dx_ref, dg_ref, loss_ref):
        @pl.when(pl.program_id(0) == 0)
        def _():
            dg_ref[...] = jnp.zeros_like(dg_ref)
            loss_ref[...] = jnp.zeros_like(loss_ref)

        n, r, y = _rms(x_ref[...], g_ref[...])
        err = y - t_ref[...]
        loss_ref[...] += 0.5 * jnp.sum(jnp.mean(err * err, axis=-1, keepdims=True), axis=0, keepdims=True)
        dxr, dg = _rms_bwd(err * (1.0 / d), n, r, g_ref[...])
        dx_ref[...] = dxr
        dg_ref[0:1, :] += dg

    row = pl.BlockSpec((tm, d), lambda i: (i, 0))
    return pl.pallas_call(
        body, name="loss_bwd", grid=(s // tm,),
        in_specs=[row, pl.BlockSpec((1, d), lambda i: (0, 0)), row],
        out_specs=[row, pl.BlockSpec((8, d), lambda i: (0, 0)), pl.BlockSpec((8, LANES), lambda i: (0, 0))],
        out_shape=[jax.ShapeDtypeStruct((s, d), F32), jax.ShapeDtypeStruct((8, d), F32), jax.ShapeDtypeStruct((8, LANES), F32)],
        compiler_params=_params("arbitrary"),
    )(x, g, target)


def _adamw_math(w, g, m, v):
    m = ADAM_B1 * m + (1.0 - ADAM_B1) * g
    v = ADAM_B2 * v + (1.0 - ADAM_B2) * (g * g)
    m_hat = m / (1.0 - ADAM_B1 ** ADAM_STEP)
    v_hat = v / (1.0 - ADAM_B2 ** ADAM_STEP)
    return -ADAM_LR * (m_hat / (jnp.sqrt(v_hat) + ADAM_EPS) + ADAM_WD * w), m, v


def _adamw(w, g, m, v):
    r, c = w.shape
    tr = r // 8 if r % 64 == 0 else r

    def body(w_ref, g_ref, m_ref, v_ref, d_ref, mo_ref, vo_ref, go_ref):
        gv = g_ref[...]
        d_ref[...], mo_ref[...], vo_ref[...] = _adamw_math(w_ref[...], gv, m_ref[...], v_ref[...])
        go_ref[...] = gv

    blk = pl.BlockSpec((tr, c), lambda i: (i, 0))
    return pl.pallas_call(
        body, name="adamw", grid=(r // tr,), in_specs=[blk] * 4, out_specs=[blk] * 4,
        out_shape=[jax.ShapeDtypeStruct((r, c), F32)] * 4, compiler_params=_params("arbitrary"),
    )(w, g, m, v)


def _adamw_small(ws, gs, ms, vs):
    n = len(ws)

    def body(*refs):
        ins, outs = refs[:4 * n], refs[4 * n:]
        for k in range(n):
            dl, mo, vo = _adamw_math(ins[k][...], ins[n + k][...], ins[2 * n + k][...], ins[3 * n + k][...])
            outs[k][...], outs[n + k][...], outs[2 * n + k][...] = dl, mo, vo

    vm = pl.BlockSpec(memory_space=pltpu.VMEM)
    out = pl.pallas_call(
        body, name="adamw_small", in_specs=[vm] * (4 * n), out_specs=[vm] * (3 * n),
        out_shape=[jax.ShapeDtypeStruct(a.shape, F32) for a in ws] * 3,
        compiler_params=pltpu.CompilerParams(vmem_limit_bytes=VMEM_LIMIT),
    )(*ws, *gs, *ms, *vs)
    return out[:n], out[n:2 * n], out[2 * n:]


def _where_am_i():
    x, y, c = lax.axis_index("x"), lax.axis_index("y"), lax.axis_index("c")
    chips = [(1 - x, y), (x, 1 - y), (1 - x, 1 - y)]
    return x, y, c, chips


def _chip_id(chip):
    return 2 * chip[0] + chip[1]


def _landing_shapes(shards):
    return [jax.ShapeDtypeStruct((3, a.shape[0] // 2, a.shape[1]), a.dtype) for a in shards]


def _ici_gather_copies(ins, lands, send_sems, recv_sems):
    _, _, c, chips = _where_am_i()
    copies = []
    for a, src in enumerate(ins):
        hr = src.shape[0] // 2
        for j, chip in enumerate(chips):
            copies.append(pltpu.make_async_remote_copy(
                src_ref=src.at[pl.ds(c * hr, hr)], dst_ref=lands[a].at[j], send_sem=send_sems.at[3 * a + j],
                recv_sem=recv_sems.at[3 * a + j], device_id=(*chip, c), device_id_type=MESH))
    return copies


def _ici_reduce_copies(s_ref, o_ref, send_sems, recv_sems, local_sem):
    x, y, c, chips = _where_am_i()
    copies = [pltpu.make_async_remote_copy(src_ref=s_ref.at[_chip_id(chip)], dst_ref=o_ref.at[j], send_sem=send_sems.at[j],
                                           recv_sem=recv_sems.at[j], device_id=(*chip, c), device_id_type=MESH)
              for j, chip in enumerate(chips)]
    return copies, pltpu.make_async_copy(s_ref.at[_chip_id((x, y))], o_ref.at[3], local_sem)


def _wait_all(copies):
    for cp in copies:
        cp.wait_recv()
    for cp in copies:
        cp.wait_send()


def _ici_gather(shards):
    n = len(shards)

    def body(*refs):
        copies = _ici_gather_copies(refs[:n], refs[n:2 * n], refs[2 * n], refs[2 * n + 1])
        for cp in copies:
            cp.start()
        _wait_all(copies)

    return pl.pallas_call(
        body, name="ici_gather", in_specs=[ANY] * n, out_specs=[ANY] * n, out_shape=_landing_shapes(shards),
        scratch_shapes=[pltpu.SemaphoreType.DMA((3 * n,))] * 2,
    )(*shards)


SLOTS = 4


def _sibling_gather(shards, lands):
    n = len(shards)
    d = shards[0].shape[1]
    halves = [a.shape[0] // 2 for a in shards]
    hmax = max(halves)
    chunks = [(a, j, halves[a]) for a in range(n) for j in range(3)]
    nc = len(chunks)

    def body(*refs):
        ins, lnd, outs = refs[:n], refs[n:2 * n], refs[2 * n:3 * n]
        sbuf, rbuf, obuf, ld_sems, take_sems, send_sems, recv_sems, place_sems, own_ld_sems, own_st_sems, credits = refs[3 * n:]
        x, y, c, chips = _where_am_i()
        sibling = (x, y, 1 - c)
        me = _chip_id((x, y))

        def load(i):
            a, j, hr = chunks[i]
            return pltpu.make_async_copy(lnd[a].at[j], sbuf.at[i % SLOTS, pl.ds(0, hr)], ld_sems.at[i % SLOTS])

        def push(i):
            hr, slot = chunks[i][2], i % SLOTS
            return pltpu.make_async_remote_copy(src_ref=sbuf.at[slot, pl.ds(0, hr)], dst_ref=rbuf.at[slot, pl.ds(0, hr)],
                                                send_sem=send_sems.at[slot], recv_sem=recv_sems.at[slot], device_id=sibling,
                                                device_id_type=MESH)

        def take(i):
            a, j, hr = chunks[i]
            return pltpu.make_async_copy(rbuf.at[i % SLOTS, pl.ds(0, hr)],
                                         outs[a].at[_chip_id(chips[j]), pl.ds((1 - c) * hr, hr)], take_sems.at[i % SLOTS])

        def place(i):
            a, j, hr = chunks[i]
            return pltpu.make_async_copy(sbuf.at[i % SLOTS, pl.ds(0, hr)], outs[a].at[_chip_id(chips[j]), pl.ds(c * hr, hr)],
                                         place_sems.at[i % SLOTS])

        own = [(a, h, halves[a]) for a in range(n) for h in range(2)]

        def own_load(k):
            a, h, hr = own[k]
            return pltpu.make_async_copy(ins[a].at[pl.ds(h * hr, hr)], obuf.at[k % 2, pl.ds(0, hr)], own_ld_sems.at[k % 2])

        def own_store(k):
            a, h, hr = own[k]
            return pltpu.make_async_copy(obuf.at[k % 2, pl.ds(0, hr)], outs[a].at[me, pl.ds(h * hr, hr)], own_st_sems.at[k % 2])

        def own_step(k):
            if k < len(own):
                if k >= 2:
                    own_store(k - 2).wait()
                own_load(k).start()
            if 1 <= k <= len(own):
                own_load(k - 1).wait()
                own_store(k - 1).start()

        for i in range(min(2, nc)):
            load(i).start()
        for i in range(nc):
            own_step(i)
            if i >= 2:
                push(i - 2).wait_send()
                place(i - 2).wait()
            if i + 2 < nc:
                load(i + 2).start()
            load(i).wait()
            place(i).start()
            if i >= SLOTS:
                pl.semaphore_wait(credits.at[i % SLOTS], 1)
            push(i).start()
            if i >= 1:
                push(i - 1).wait_recv()
                take(i - 1).start()
            if i >= 2:
                take(i - 2).wait()
                if i - 2 + SLOTS < nc:
                    pl.semaphore_signal(credits.at[(i - 2) % SLOTS], inc=1, device_id=sibling, device_id_type=MESH)
        for i in range(max(0, nc - 2), nc):
            push(i).wait_send()
            place(i).wait()
        push(nc - 1).wait_recv()
        take(nc - 1).start()
        for i in range(max(0, nc - 2), nc):
            take(i).wait()
        for k in range(nc, len(own) + 1):
            own_step(k)
        for k in range(max(0, len(own) - 2), len(own)):
            own_store(k).wait()

    dma = pltpu.SemaphoreType.DMA((SLOTS,))
    dma2 = pltpu.SemaphoreType.DMA((2,))
    return pl.pallas_call(
        body, name="sibling_gather", in_specs=[ANY] * (2 * n), out_specs=[ANY] * n,
        out_shape=[jax.ShapeDtypeStruct((N_CHIPS,) + a.shape, a.dtype) for a in shards],
        scratch_shapes=[pltpu.VMEM((SLOTS, hmax, d), BF16), pltpu.VMEM((SLOTS, hmax, d), BF16), pltpu.VMEM((2, hmax, d), BF16),
                        dma, dma, dma, dma, dma, dma2, dma2, pltpu.SemaphoreType.REGULAR((SLOTS,))],
        compiler_params=pltpu.CompilerParams(vmem_limit_bytes=VMEM_LIMIT),
    )(*shards, *lands)


def _pair_sum(grads):
    n = len(grads)
    chunks, total, rows = _pair_chunks(grads)

    def body(*refs):
        first, step, last = _pair_sum_steps(refs[:n], refs[n], refs[n + 1:], chunks)
        first()
        for i in range(len(chunks)):
            step(i)
        last()

    return pl.pallas_call(
        body, name="pair_sum", in_specs=[ANY] * n, out_specs=ANY,
        out_shape=jax.ShapeDtypeStruct((N_CHIPS, total, grads[0].shape[2]), BF16),
        scratch_shapes=_pair_scratch(rows, grads[0].shape[2]),
        compiler_params=pltpu.CompilerParams(vmem_limit_bytes=VMEM_LIMIT),
    )(*grads)


def _pair_chunks(grads):
    halves = [g.shape[1] // 2 for g in grads]
    chunks, off = [], 0
    for a in range(len(grads)):
        chunks += [(a, k, off, halves[a]) for k in range(N_CHIPS)]
        off += halves[a]
    return chunks, off, max(halves)


def _pair_scratch(rows, d):
    stage = pltpu.VMEM((SLOTS, rows, d), BF16)
    dma = pltpu.SemaphoreType.DMA((SLOTS,))
    return [stage, stage, stage, stage, dma, dma, dma, dma, dma, pltpu.SemaphoreType.REGULAR((SLOTS,))]


def _pair_sum_steps(ins, out_ref, scratch, chunks):
    if True:
        sbuf, rbuf, mbuf, obuf, ls_sems, lm_sems, st_sems, send_sems, recv_sems, credits = scratch
        nc = len(chunks)
        x, y, c, _ = _where_am_i()
        sibling = (x, y, 1 - c)

        def load_theirs(i):
            a, k, _, hr = chunks[i]
            return pltpu.make_async_copy(ins[a].at[k, pl.ds((1 - c) * hr, hr)], sbuf.at[i % SLOTS, pl.ds(0, hr)], ls_sems.at[i % SLOTS])

        def load_mine(i):
            a, k, _, hr = chunks[i]
            return pltpu.make_async_copy(ins[a].at[k, pl.ds(c * hr, hr)], mbuf.at[i % SLOTS, pl.ds(0, hr)], lm_sems.at[i % SLOTS])

        def push(i):
            hr, slot = chunks[i][3], i % SLOTS
            return pltpu.make_async_remote_copy(src_ref=sbuf.at[slot, pl.ds(0, hr)], dst_ref=rbuf.at[slot, pl.ds(0, hr)],
                                                send_sem=send_sems.at[slot], recv_sem=recv_sems.at[slot], device_id=sibling,
                                                device_id_type=MESH)

        def store(i):
            _, k, o, hr = chunks[i]
            slot = i % SLOTS
            return pltpu.make_async_copy(obuf.at[slot, pl.ds(0, hr)], out_ref.at[k, pl.ds(o, hr)], st_sems.at[slot])

        def start_push(i):
            load_theirs(i).wait()
            if i >= SLOTS:
                pl.semaphore_wait(credits.at[i % SLOTS], 1)
            push(i).start()

        def first():
            for i in range(min(2, nc)):
                load_theirs(i).start()
                load_mine(i).start()
            start_push(0)

        def step(i):
            hr, slot = chunks[i][3], i % SLOTS
            if i + 2 < nc:
                load_theirs(i + 2).start()
                load_mine(i + 2).start()
            if i + 1 < nc:
                start_push(i + 1)
            push(i).wait_recv()
            push(i).wait_send()
            load_mine(i).wait()
            if i >= SLOTS:
                store(i - SLOTS).wait()
            obuf[slot, 0:hr, :] = (mbuf[slot, 0:hr, :].astype(F32) + rbuf[slot, 0:hr, :].astype(F32)).astype(BF16)
            if i + SLOTS < nc:
                pl.semaphore_signal(credits.at[slot], inc=1, device_id=sibling, device_id_type=MESH)
            store(i).start()

        def last():
            for i in range(max(0, nc - SLOTS), nc):
                store(i).wait()

        return first, step, last


def _reduce_chips(sums):
    def body(s_ref, o_ref, send_sems, recv_sems, local_sem):
        copies, local = _ici_reduce_copies(s_ref, o_ref, send_sems, recv_sems, local_sem)
        local.start()
        for cp in copies:
            cp.start()
        _wait_all(copies)
        local.wait()

    return pl.pallas_call(
        body, name="reduce_chips", in_specs=[ANY], out_specs=ANY, out_shape=jax.ShapeDtypeStruct(sums.shape, BF16),
        scratch_shapes=[pltpu.SemaphoreType.DMA((3,)), pltpu.SemaphoreType.DMA((3,)), pltpu.SemaphoreType.DMA],
    )(sums)


def _sum_share(parts, groups, rows, small):
    layers, n, n_g = len(parts), len(rows), len(groups)
    parts = [part for layer in parts for part in layer]
    n_l = len(parts)
    d = parts[0].shape[2]
    halves = [r // 2 for r in rows]
    hmax = max(halves)
    chunks = []
    for l in range(layers):
        for g, members in enumerate(groups):
            off = 0
            for a in members:
                chunks.append((l * n_g + g, a, off, halves[a], l))
                off += halves[a]
    nc = len(chunks)

    def body(*refs):
        ins, small_ref, outs, total_ref = refs[:n_l], refs[n_l], refs[n_l + 1:n_l + 1 + n], refs[n_l + 1 + n]
        pbuf, obuf, rbuf, ld_sems, keep_sems, take_sems, send_sems, recv_sems, credits, all_ref = refs[n_l + 2 + n:n_l + 12 + n]
        begin_small, finish_small = _small_gather_steps(small_ref, all_ref, *refs[n_l + 12 + n:])
        begin_small()
        x, y, c, _ = _where_am_i()
        sibling = (x, y, 1 - c)

        def load(i):
            part, _, off, hr, _ = chunks[i]
            return pltpu.make_async_copy(ins[part].at[:, pl.ds(off, hr)], pbuf.at[i % SLOTS, :, pl.ds(0, hr)], ld_sems.at[i % SLOTS])

        def keep(i):
            _, a, _, hr, l = chunks[i]
            return pltpu.make_async_copy(obuf.at[i % SLOTS, pl.ds(0, hr)], outs[a].at[l, pl.ds(c * hr, hr)], keep_sems.at[i % SLOTS])

        def push(i):
            hr, slot = chunks[i][3], i % SLOTS
            return pltpu.make_async_remote_copy(src_ref=obuf.at[slot, pl.ds(0, hr)], dst_ref=rbuf.at[slot, pl.ds(0, hr)],
                                                send_sem=send_sems.at[slot], recv_sem=recv_sems.at[slot], device_id=sibling,
                                                device_id_type=MESH)

        def take(i):
            _, a, _, hr, l = chunks[i]
            return pltpu.make_async_copy(rbuf.at[i % SLOTS, pl.ds(0, hr)], outs[a].at[l, pl.ds((1 - c) * hr, hr)],
                                         take_sems.at[i % SLOTS])

        for i in range(min(2, nc)):
            load(i).start()
        for i in range(nc):
            hr, slot = chunks[i][3], i % SLOTS
            if i + 2 < nc:
                load(i + 2).start()
            load(i).wait()
            if i >= SLOTS:
                keep(i - SLOTS).wait()
                push(i - SLOTS).wait_send()
            part = lambda k: pbuf[slot, k, 0:hr, :].astype(F32)
            obuf[slot, 0:hr, :] = ((part(3) + part(0)) + part(1)) + part(2)
            keep(i).start()
            if i >= SLOTS:
                pl.semaphore_wait(credits.at[slot], 1)
            push(i).start()
            if i >= 1:
                push(i - 1).wait_recv()
                take(i - 1).start()
            if i >= 2:
                take(i - 2).wait()
                if i - 2 + SLOTS < nc:
                    pl.semaphore_signal(credits.at[(i - 2) % SLOTS], inc=1, device_id=sibling, device_id_type=MESH)
        push(nc - 1).wait_recv()
        take(nc - 1).start()
        for i in range(max(0, nc - 2), nc):
            take(i).wait()
        for i in range(max(0, nc - SLOTS), nc):
            keep(i).wait()
            push(i).wait_send()
        finish_small()
        total_ref[...] = _sum_blocks(all_ref, small.shape[0])

    dma = pltpu.SemaphoreType.DMA((SLOTS,))
    vm = pl.BlockSpec(memory_space=pltpu.VMEM)
    out = pl.pallas_call(
        body, name="sum_share", in_specs=[ANY] * n_l + [vm], out_specs=[ANY] * n + [vm],
        out_shape=[jax.ShapeDtypeStruct((layers, r, d), F32) for r in rows] + [jax.ShapeDtypeStruct(small.shape, F32)],
        scratch_shapes=[pltpu.VMEM((SLOTS, N_CHIPS, hmax, d), BF16), pltpu.VMEM((SLOTS, hmax, d), F32),
                        pltpu.VMEM((SLOTS, hmax, d), F32), dma, dma, dma, dma, dma, pltpu.SemaphoreType.REGULAR((SLOTS,)),
                        pltpu.VMEM((N_DEV * small.shape[0], small.shape[1]), F32)] + _small_gather_sems(),
        compiler_params=pltpu.CompilerParams(vmem_limit_bytes=VMEM_LIMIT),
    )(*parts, small)
    return out[:n], out[n]


def _small_gather_sems():
    return [pltpu.SemaphoreType.DMA((7,)), pltpu.SemaphoreType.DMA((7,)), pltpu.SemaphoreType.DMA]


def _sum_blocks(all_ref, m):
    total = all_ref[0:m, :]
    for dev in range(1, N_DEV):
        total = total + all_ref[dev * m:(dev + 1) * m, :]
    return total


def _small_gather_steps(x_ref, all_ref, send_sems, recv_sems, local_sem):
    m = x_ref.shape[0]
    x, y, c, chips = _where_am_i()
    me, sibling = (x, y, c), (x, y, 1 - c)

    def rows(px, py, pc):
        return all_ref.at[pl.ds((4 * px + 2 * py + pc) * m, m), :]

    def copy(k, blk, to, src=None):
        return pltpu.make_async_remote_copy(src_ref=rows(*blk) if src is None else src, dst_ref=rows(*blk),
                                            send_sem=send_sems.at[k], recv_sem=recv_sems.at[k], device_id=to,
                                            device_id_type=MESH)

    def own():
        return pltpu.make_async_copy(x_ref, rows(*me), local_sem)

    def first():
        return [copy(0, me, sibling, src=x_ref)] + [copy(1 + j, me, (*chip, c), src=x_ref) for j, chip in enumerate(chips)]

    def passed():
        return [copy(4 + j, (*chip, c), sibling) for j, chip in enumerate(chips)]

    def begin():
        own().start()
        for cp in first():
            cp.start()

    def finish():
        forwards = passed()
        for j, chip in enumerate(chips):
            copy(1 + j, (*chip, c), me).wait_recv()
            forwards[j].start()
        copy(0, sibling, me).wait_recv()
        for j, chip in enumerate(chips):
            copy(4 + j, (*chip, 1 - c), me).wait_recv()
        for cp in first() + forwards:
            cp.wait_send()
        own().wait()

    return begin, finish


def _all_gather_small(block, reduce):
    m, n = block.shape

    def body(x_ref, out_ref, *scratch):
        all_ref = scratch[0] if reduce else out_ref
        begin, finish = _small_gather_steps(x_ref, all_ref, *scratch[-3:])
        begin()
        finish()
        if reduce:
            out_ref[...] = _sum_blocks(all_ref, m)

    vm = pl.BlockSpec(memory_space=pltpu.VMEM)
    sems = _small_gather_sems()
    return pl.pallas_call(
        body, name="reduce_small" if reduce else "gather_small", in_specs=[vm], out_specs=vm,
        out_shape=jax.ShapeDtypeStruct((m, n) if reduce else (N_DEV * m, n), F32),
        scratch_shapes=([pltpu.VMEM((N_DEV * m, n), F32)] if reduce else []) + sems,
        compiler_params=pltpu.CompilerParams(vmem_limit_bytes=VMEM_LIMIT),
    )(block)


def _pack(arrays):
    flat = jnp.concatenate([a.reshape(-1) for a in arrays])
    pad = (-flat.shape[0]) % (8 * LANES)
    return jnp.pad(flat, (0, pad)).reshape(-1, LANES)


def _unpack(buf, shapes):
    flat = buf.reshape(-1)
    out, off = [], 0
    for shp in shapes:
        size = 1
        for dim in shp:
            size *= dim
        out.append(flat[off:off + size].reshape(shp))
        off += size
    return out


BIG = ("ffn1_w1", "ffn1_w3", "ffn1_w2", "w_in", "w_out", "ffn2_w1", "ffn2_w3", "ffn2_w2")
TRANSPOSED = ("ffn1_w1", "ffn1_w3", "w_in", "ffn2_w1", "ffn2_w3")
SMALL = ("ffn1_norm", "mix_norm", "conf_conv_w", "conf_conv_b", "conf_ln_g", "conf_ln_b", "sconv_w", "pool_w", "pool_scale",
         "gmlp_ln_g", "gmlp_ln_b", "gmlp_w_s", "gmlp_b_s", "ffn2_norm", "final_norm")
ORDER = ("ffn1_norm", "ffn1_w1", "ffn1_w3", "ffn1_w2", "mix_norm", "w_in", "conf_conv_w", "conf_conv_b", "conf_ln_g", "conf_ln_b",
         "sconv_w", "pool_w", "pool_scale", "gmlp_ln_g", "gmlp_ln_b", "gmlp_w_s", "gmlp_b_s", "w_out", "ffn2_norm", "ffn2_w1",
         "ffn2_w3", "ffn2_w2", "final_norm")


def _as2d(a):
    return a.reshape(-1, a.shape[-1])


def kernel(x, ffn1_norm, ffn1_w1, ffn1_w3, ffn1_w2, mix_norm, w_in, conf_conv_w, conf_conv_b, conf_ln_g, conf_ln_b, sconv_w, pool_w, pool_scale, gmlp_ln_g, gmlp_ln_b, gmlp_w_s, gmlp_b_s, w_out, ffn2_norm, ffn2_w1, ffn2_w3, ffn2_w2, final_norm, loss_target, m_ffn1_norm, m_ffn1_w1, m_ffn1_w3, m_ffn1_w2, m_mix_norm, m_w_in, m_conf_conv_w, m_conf_conv_b, m_conf_ln_g, m_conf_ln_b, m_sconv_w, m_pool_w, m_pool_scale, m_gmlp_ln_g, m_gmlp_ln_b, m_gmlp_w_s, m_gmlp_b_s, m_w_out, m_ffn2_norm, m_ffn2_w1, m_ffn2_w3, m_ffn2_w2, m_final_norm, v_ffn1_norm, v_ffn1_w1, v_ffn1_w3, v_ffn1_w2, v_mix_norm, v_w_in, v_conf_conv_w, v_conf_conv_b, v_conf_ln_g, v_conf_ln_b, v_sconv_w, v_pool_w, v_pool_scale, v_gmlp_ln_g, v_gmlp_ln_b, v_gmlp_w_s, v_gmlp_b_s, v_w_out, v_ffn2_norm, v_ffn2_w1, v_ffn2_w3, v_ffn2_w2, v_final_norm):
    given = dict(locals())
    w = {k: given[k] for k in ORDER}
    mom = {k: given["m_" + k] for k in ORDER}
    var = {k: given["v_" + k] for k in ORDER}
    n_l = ffn1_w1.shape[0]
    xs = x[0]
    d = xs.shape[1]
    chip = 2 * lax.axis_index("x") + lax.axis_index("y")

    def shard(name, l):
        a = w[name][l]
        return (jnp.swapaxes(w[name], 1, 2)[l] if name in TRANSPOSED else a).astype(BF16)

    groups = (BIG[:3], BIG[3:])

    def shards_of(l, g):
        return [shard(name, l) for name in groups[g]] if l < n_l else []

    def finish_gather(l, g, lands):
        out = _sibling_gather(shards_of(l, g), lands)
        return {name: a.reshape(-1, d) for name, a in zip(groups[g], out)}

    shard_rows = [w[name].shape[2] if name in TRANSPOSED else w[name].shape[1] for name in BIG]

    conv_shapes = [conf_conv_w.shape, sconv_w.shape]
    conv_all = _all_gather_small(_pack([conf_conv_w, sconv_w]), reduce=False)
    conv_all = conv_all.reshape(N_CHIPS, 2, -1)[:, 0]
    conf_full, sconv_full = [jnp.concatenate([_unpack(conv_all[k], conv_shapes)[a] for k in range(N_CHIPS)], axis=-1)
                             for a in range(2)]

    lane = jnp.arange(C) // HEAD_DIM
    head_rows = (jnp.arange(8)[:, None] == lane[None, :]).astype(F32)
    tril = jnp.tril(jnp.ones((CHUNK, CHUNK), F32))
    tril4 = jnp.tile(tril, (N_HEADS, 1))
    mixer_consts = []
    for l in range(n_l):
        cw = jnp.pad(conf_full[l], ((0, 32 - CONF_KERNEL), (0, 0)))
        vec = jnp.concatenate([conf_conv_b[l][None], conf_ln_g[l][None], conf_ln_b[l][None], pool_scale[l][None],
                               gmlp_ln_g[l][None], gmlp_ln_b[l][None], sconv_full[l], jnp.zeros((7, C), F32)], axis=0)
        eye = jnp.eye(len(pool_w[l]), dtype=F32)
        pool_blk = (eye[:, None, :, None] * pool_w[l][:, :, None, :]).reshape(C, C).astype(BF16)
        ws = gmlp_w_s[l] * tril[None]
        wstack = ws.reshape(N_HEADS * CHUNK, CHUNK).astype(BF16)
        wstack_t = jnp.swapaxes(ws, 1, 2).reshape(N_HEADS * CHUNK, CHUNK).astype(BF16)
        bias = jnp.repeat(gmlp_b_s[l].T, HEAD_DIM, axis=1)
        mixer_consts.append((cw, vec, pool_blk, wstack, wstack_t, bias))

    saved = []
    cur = xs
    gathered = [finish_gather(0, 0, _ici_gather(shards_of(0, 0)))]
    for l in range(n_l):
        gw = gathered[l]
        cw, vec, pool_blk, wstack, wstack_t, bias = mixer_consts[l]
        x0 = cur
        x1, a1, b1, h1, lands = _ffn_fwd(x0, ffn1_norm[l][None], gw["ffn1_w1"], gw["ffn1_w3"], gw["ffn1_w2"], shards_of(l, 1))
        gw.update(finish_gather(l, 1, lands))
        p = _proj_fwd(x1, mix_norm[l][None], gw["w_in"])
        x2, mix, z = _mixer_fwd(p, x1, cw, vec, pool_blk, wstack, bias, gw["w_out"])
        x3, a2, b2, h2, lands = _ffn_fwd(x2, ffn2_norm[l][None], gw["ffn2_w1"], gw["ffn2_w3"], gw["ffn2_w2"], shards_of(l + 1, 0))
        if l + 1 < n_l:
            gathered.append(finish_gather(l + 1, 0, lands))
        saved.append((x0, x1, x2, a1, b1, a2, b2, p, mix, z, h1, h2))
        cur = x3

    dx, dg_final, loss_part = _loss_bwd(cur, final_norm[None], loss_target[0])

    small_parts = [None] * n_l
    reduced_halves = [[None, None] for _ in range(n_l)]
    pair_sum = lambda big, g: _pair_sum([big[name].reshape(N_CHIPS, -1, d) for name in groups[g]])
    pending = None
    for l in reversed(range(n_l)):
        gw = gathered[l]
        cw, vec, pool_blk, wstack, wstack_t, bias = mixer_consts[l]
        x0, x1, x2, a1, b1, a2, b2, p, mix, z, h1, h2 = saved[l]
        big = {}
        out = _ffn_bwd(dx, x2, ffn2_norm[l][None], a2, b2, gw["ffn2_w1"], gw["ffn2_w3"], gw["ffn2_w2"], pending)
        dx, dg_ffn2, dy, da, db, u, dx_bf = out[:7]
        if pending is not None:
            reduced_halves[l + 1][0] = out[7]
        big["ffn2_w1"], big["ffn2_w3"], big["ffn2_w2"] = _dw(da, h2), _dw(db, h2), _dw(u, dy)
        big["w_out"] = _dw(mix, dx_bf)
        dp, dcw, dvec, dpool, dws, dbs = _mixer_bwd(dx, p, z, cw, vec, pool_blk, wstack, wstack_t, bias, tril4, head_rows, gw["w_out"])
        dx, dg_mix, h = _proj_bwd(dx, x1, mix_norm[l][None], dp, gw["w_in"])
        big["w_in"] = _dw(dp, h)
        second = pair_sum(big, 1)
        out = _ffn_bwd(dx, x0, ffn1_norm[l][None], a1, b1, gw["ffn1_w1"], gw["ffn1_w3"], gw["ffn1_w2"], second)
        dx, dg_ffn1, dy, da, db, u = out[:6]
        reduced_halves[l][1] = out[7]
        big["ffn1_w1"], big["ffn1_w3"], big["ffn1_w2"] = _dw(da, h1), _dw(db, h1), _dw(u, dy)
        small_parts[l] = [dg_ffn1[0], dg_mix[0], dg_ffn2[0], dcw, dvec, dpool, dws, dbs]
        pending = pair_sum(big, 0)
    reduced_halves[0][0] = _reduce_chips(pending)
    grad_x = dx[None]

    part_shapes = [a.shape for a in small_parts[0]]
    tail = [dg_final[0], loss_part[0]]
    packed = _pack([a for l in range(n_l) for a in small_parts[l]] + tail)
    index_of = {name: a for a, name in enumerate(BIG)}
    full, summed = _sum_share(reduced_halves, [[index_of[name] for name in g] for g in groups], shard_rows, packed)
    full = dict(zip(BIG, full))
    grad = {}
    summed = _unpack(summed, part_shapes * n_l + [a.shape for a in tail])
    per_layer = [summed[l * len(part_shapes):(l + 1) * len(part_shapes)] for l in range(n_l)]
    stack = lambda k: jnp.stack([per_layer[l][k] for l in range(n_l)])
    dcw_all, dvec_all, dpool_all, dws_all, dbs_all = stack(3), stack(4), stack(5), stack(6), stack(7)
    loss = summed[-1][0]
    chip_cols = lambda a: lax.dynamic_slice_in_dim(a, chip * (C // N_CHIPS), C // N_CHIPS, axis=2)
    n_pool = pool_w.shape[1]
    grad.update(
        ffn1_norm=stack(0), mix_norm=stack(1), ffn2_norm=stack(2), final_norm=summed[-2],
        conf_conv_w=chip_cols(dcw_all[:, :CONF_KERNEL]), conf_conv_b=dvec_all[:, 0], conf_ln_g=dvec_all[:, 1],
        conf_ln_b=dvec_all[:, 2], pool_scale=dvec_all[:, 3], gmlp_ln_g=dvec_all[:, 4], gmlp_ln_b=dvec_all[:, 5],
        sconv_w=chip_cols(dvec_all[:, 6:6 + SHORT_KERNEL]),
        pool_w=jnp.stack([dpool_all[:, g * POOL_GROUP:(g + 1) * POOL_GROUP, g * POOL_GROUP:(g + 1) * POOL_GROUP]
                          for g in range(n_pool)], axis=1),
        gmlp_w_s=dws_all.reshape(n_l, N_HEADS, CHUNK, CHUNK), gmlp_b_s=dbs_all[:, :N_HEADS],
    )

    delta, new_m, new_v = {}, {}, {}
    for name in BIG:
        rows_of = (lambda a: jnp.swapaxes(a, 1, 2)) if name in TRANSPOSED else (lambda a: a)
        shp = full[name].shape
        out = _adamw(_as2d(rows_of(w[name])), _as2d(full[name]), _as2d(rows_of(mom[name])), _as2d(rows_of(var[name])))
        delta[name], new_m[name], new_v[name], grad[name] = (rows_of(o.reshape(shp)) for o in out)
    ds, ms, vs = _adamw_small([_as2d(w[k]) if w[k].ndim > 1 else w[k][None] for k in SMALL],
                              [_as2d(grad[k]) if grad[k].ndim > 1 else grad[k][None] for k in SMALL],
                              [_as2d(mom[k]) if mom[k].ndim > 1 else mom[k][None] for k in SMALL],
                              [_as2d(var[k]) if var[k].ndim > 1 else var[k][None] for k in SMALL])
    for k, dl, mo, vo in zip(SMALL, ds, ms, vs):
        delta[k], new_m[k], new_v[k] = dl.reshape(w[k].shape), mo.reshape(w[k].shape), vo.reshape(w[k].shape)

    return (loss, grad_x, *[grad[k] for k in ORDER], *[delta[k] for k in ORDER], *[new_m[k] for k in ORDER],
            *[new_v[k] for k in ORDER])
```

```python
import functools

import jax
import jax.numpy as jnp
from jax import lax
from jax.experimental import pallas as pl
from jax.experimental.pallas import tpu as pltpu

F32 = jnp.float32
BF16 = jnp.bfloat16
MESH = pl.DeviceIdType.MESH
ANY = pl.BlockSpec(memory_space=pl.ANY)

EPS = 1e-6
FFN_RESIDUAL = 0.5
D_GROUP = 256
CONF_KERNEL = 31
SHORT_KERNEL = 3
POOL_GROUP = 64
CHUNK = 128
N_HEADS = 4
HEAD_DIM = 64
HALO = 32
N_CHIPS = 4
N_DEV = 8
LANES = 128
MXU_TILE = 256
VMEM_LIMIT = 56 * 2**20
T_MIX_FWD = 512
T_MIX_BWD = 256

ADAM_LR = 0.001
ADAM_B1 = 0.9
ADAM_B2 = 0.999
ADAM_EPS = 1e-08
ADAM_WD = 0.01
ADAM_STEP = 10

NT = (((1,), (1,)), ((), ()))
TN = (((0,), (0,)), ((), ()))


def _params(*sem):
    return pltpu.CompilerParams(dimension_semantics=sem, vmem_limit_bytes=VMEM_LIMIT)


def _dot(a, b):
    return jnp.dot(a, b, preferred_element_type=F32)


def _dot_nt(a, b):
    return lax.dot_general(a, b, NT, preferred_element_type=F32)


def _t_bf16(v):
    return jnp.transpose(v).astype(BF16)


def _sigmoid(v):
    return 1.0 / (1.0 + jnp.exp(-v))


def _rms(x, g):
    r = lax.rsqrt(jnp.mean(x * x, axis=-1, keepdims=True) + EPS)
    n = x * r
    return n, r, n * g


def _rms_bwd(dh, n, r, g):
    dn = dh * g
    dx = r * (dn - n * jnp.mean(dn * n, axis=-1, keepdims=True))
    return dx, jnp.sum(dh * n, axis=0, keepdims=True)


def _ln_fwd(z, g, b):
    mu = jnp.mean(z, axis=-1, keepdims=True)
    zc = z - mu
    rs = lax.rsqrt(jnp.mean(zc * zc, axis=-1, keepdims=True) + EPS)
    zn = zc * rs
    return zn, rs, zn * g + b


def _ln_bwd(dl, zn, rs, g):
    dzn = dl * g
    dz = rs * (dzn - jnp.mean(dzn, axis=-1, keepdims=True) - zn * jnp.mean(dzn * zn, axis=-1, keepdims=True))
    return dz, jnp.sum(dl * zn, axis=0, keepdims=True), jnp.sum(dl, axis=0, keepdims=True)


def _tile(n, want):
    return want if n % want == 0 else n


def _resident(shape):
    return pl.BlockSpec(shape, lambda i: (0,) * len(shape), pipeline_mode=pl.Buffered(1))


def _ffn_fwd(x, g, w1t, w3t, w2, shards=()):
    s, d = x.shape
    f = w1t.shape[0]
    tm, tf = _tile(s, 512), MXU_TILE
    nj = f // tf
    ni = s // tm
    n_c = len(shards)

    def body(*refs):
        x_ref, g_ref, w1_ref, w3_ref, w2_ref = refs[:5]
        xo_ref, a_ref, b_ref, h_ref = refs[5 + n_c:9 + n_c]
        u_s = refs[9 + 2 * n_c]
        if n_c:
            gather = lambda: _ici_gather_copies(refs[5:5 + n_c], refs[9 + n_c:9 + 2 * n_c], refs[10 + 2 * n_c], refs[11 + 2 * n_c])

            @pl.when(pl.program_id(0) == 0)
            def _():
                for cp in gather():
                    cp.start()

        h = _rms(x_ref[...], g_ref[...])[2].astype(BF16)
        h_ref[...] = h
        for j in range(nj):
            cols = slice(j * tf, (j + 1) * tf)
            a = _dot_nt(h, w1_ref[cols, :])
            b = _dot_nt(h, w3_ref[cols, :])
            a_ref[:, cols] = a.astype(BF16)
            b_ref[:, cols] = b.astype(BF16)
            u_s[:, cols] = ((a * _sigmoid(a)) * b).astype(BF16)
        xo_ref[...] = x_ref[...] + FFN_RESIDUAL * _dot(u_s[...], w2_ref[...])
        if n_c:
            @pl.when(pl.program_id(0) == ni - 1)
            def _():
                _wait_all(gather())

    row = pl.BlockSpec((tm, d), lambda i: (i, 0))
    hid = pl.BlockSpec((tm, f), lambda i: (i, 0))
    sems = [pltpu.SemaphoreType.DMA((3 * n_c,))] * 2 if n_c else []
    out = pl.pallas_call(
        body, name="ffn_fwd_gather" if n_c else "ffn_fwd", grid=(ni,),
        in_specs=[row, _resident(g.shape), _resident(w1t.shape), _resident(w3t.shape), _resident(w2.shape)] + [ANY] * n_c,
        out_specs=[row, hid, hid, row] + [ANY] * n_c,
        out_shape=[jax.ShapeDtypeStruct((s, d), F32), jax.ShapeDtypeStruct((s, f), BF16), jax.ShapeDtypeStruct((s, f), BF16),
                   jax.ShapeDtypeStruct((s, d), BF16)] + _landing_shapes(shards),
        scratch_shapes=[pltpu.VMEM((tm, f), BF16)] + sems,
        compiler_params=_params("arbitrary"),
    )(x, g, w1t, w3t, w2, *shards)
    return out[0], out[1], out[2], out[3], list(out[4:])


def _ffn_bwd(dxo, x, g, a, b, w1t, w3t, w2, sums=None):
    s, d = x.shape
    f = w1t.shape[0]
    tm, tf = _tile(s, 256), MXU_TILE
    nj = f // tf
    ni = s // tm
    n_c = 0 if sums is None else 1

    def body(*refs):
        dxo_ref, x_ref, g_ref, a_ref, b_ref, w1_ref, w3_ref, w2_ref = refs[:8]
        dx_ref, dg_ref, dy_ref, da_ref, db_ref, u_ref, dxb_ref = refs[8 + n_c:15 + n_c]
        if n_c:
            exchange = lambda: _ici_reduce_copies(refs[8], refs[15 + n_c], *refs[15 + 2 * n_c:])

        @pl.when(pl.program_id(0) == 0)
        def _():
            dg_ref[...] = jnp.zeros_like(dg_ref)
            if n_c:
                copies, local = exchange()
                local.start()
                for cp in copies:
                    cp.start()

        dy = (FFN_RESIDUAL * dxo_ref[...]).astype(BF16)
        dy_ref[...] = dy
        for j in range(nj):
            cols = slice(j * tf, (j + 1) * tf)
            du = _dot_nt(dy, w2_ref[cols, :]).astype(BF16)
            av = a_ref[:, cols]
            bv = b_ref[:, cols]
            sg = _sigmoid(av.astype(F32)).astype(BF16)
            sl = av * sg
            da_ref[:, cols] = (du * bv) * (sg + sl * (1.0 - sg))
            db_ref[:, cols] = du * sl
            u_ref[:, cols] = sl * bv
        dh = _dot(da_ref[...], w1_ref[...]) + _dot(db_ref[...], w3_ref[...])
        n, r, _ = _rms(x_ref[...], g_ref[...])
        dxr, dg = _rms_bwd(dh, n, r, g_ref[...])
        dx = dxo_ref[...] + dxr
        dx_ref[...] = dx
        dxb_ref[...] = dx.astype(BF16)
        dg_ref[0:1, :] += dg
        if n_c:
            @pl.when(pl.program_id(0) == ni - 1)
            def _():
                copies, local = exchange()
                _wait_all(copies)
                local.wait()

    row = pl.BlockSpec((tm, d), lambda i: (i, 0))
    hid = pl.BlockSpec((tm, f), lambda i: (i, 0))
    extra = [] if sums is None else [sums]
    sems = [pltpu.SemaphoreType.DMA((3,)), pltpu.SemaphoreType.DMA((3,)), pltpu.SemaphoreType.DMA] if n_c else []
    out = pl.pallas_call(
        body, name="ffn_bwd_reduce" if n_c else "ffn_bwd", grid=(ni,),
        in_specs=[row, row, _resident(g.shape), hid, hid, _resident(w1t.shape), _resident(w3t.shape), _resident(w2.shape)]
        + [ANY] * n_c,
        out_specs=[row, pl.BlockSpec((8, d), lambda i: (0, 0)), row, hid, hid, hid, row] + [ANY] * n_c,
        out_shape=[jax.ShapeDtypeStruct((s, d), F32), jax.ShapeDtypeStruct((8, d), F32), jax.ShapeDtypeStruct((s, d), BF16),
                   jax.ShapeDtypeStruct((s, f), BF16), jax.ShapeDtypeStruct((s, f), BF16), jax.ShapeDtypeStruct((s, f), BF16),
                   jax.ShapeDtypeStruct((s, d), BF16)]
        + [jax.ShapeDtypeStruct(e.shape, e.dtype) for e in extra],
        scratch_shapes=sems,
        compiler_params=_params("arbitrary"),
    )(dxo, x, g, a, b, w1t, w3t, w2, *extra)
    return out


def _dw(am, bm):
    s, r = am.shape
    n = bm.shape[1]
    rb = r // 2 if r > 1024 and (r // 2) % LANES == 0 else r
    ts = _tile(s, 2048)
    ni = s // ts

    def body(am_ref, bm_ref, o_ref, acc_s):
        i = pl.program_id(1)

        @pl.when(i == 0)
        def _():
            acc_s[...] = jnp.zeros_like(acc_s)

        acc_s[...] += lax.dot_general(am_ref[...], bm_ref[...], TN, preferred_element_type=F32)

        @pl.when(i == ni - 1)
        def _():
            o_ref[...] = acc_s[...].astype(BF16)

    return pl.pallas_call(
        body, name="dw", grid=(r // rb, ni),
        in_specs=[pl.BlockSpec((ts, rb), lambda k, i: (i, k)), pl.BlockSpec((ts, n), lambda k, i: (i, 0))],
        out_specs=pl.BlockSpec((rb, n), lambda k, i: (k, 0)),
        out_shape=jax.ShapeDtypeStruct((r, n), BF16),
        scratch_shapes=[pltpu.VMEM((rb, n), F32)],
        compiler_params=_params("arbitrary", "arbitrary"),
    )(am, bm)


def _proj_fwd(x, g, w_int):
    s, d = x.shape
    f = w_int.shape[0]
    tm = _tile(s, 512)

    def body(x_ref, g_ref, w_ref, p_ref):
        p_ref[...] = _dot_nt(_rms(x_ref[...], g_ref[...])[2].astype(BF16), w_ref[...])

    return pl.pallas_call(
        body, name="proj_fwd", grid=(s // tm,),
        in_specs=[pl.BlockSpec((tm, d), lambda i: (i, 0)), _resident(g.shape), _resident(w_int.shape)],
        out_specs=pl.BlockSpec((tm, f), lambda i: (i, 0)),
        out_shape=jax.ShapeDtypeStruct((s, f), F32),
        compiler_params=_params("arbitrary"),
    )(x, g, w_int)


def _proj_bwd(dxo, x, g, dp, w_int):
    s, d = x.shape
    f = w_int.shape[0]
    tm = _tile(s, 512)

    def body(dxo_ref, x_ref, g_ref, dp_ref, w_ref, dx_ref, dg_ref, h_ref):
        @pl.when(pl.program_id(0) == 0)
        def _():
            dg_ref[...] = jnp.zeros_like(dg_ref)

        n, r, h = _rms(x_ref[...], g_ref[...])
        h_ref[...] = h.astype(BF16)
        dxr, dg = _rms_bwd(_dot(dp_ref[...], w_ref[...]), n, r, g_ref[...])
        dx_ref[...] = dxo_ref[...] + dxr
        dg_ref[0:1, :] += dg

    row = pl.BlockSpec((tm, d), lambda i: (i, 0))
    return pl.pallas_call(
        body, name="proj_bwd", grid=(s // tm,),
        in_specs=[row, row, _resident(g.shape), pl.BlockSpec((tm, f), lambda i: (i, 0)), _resident(w_int.shape)],
        out_specs=[row, pl.BlockSpec((8, d), lambda i: (0, 0)), row],
        out_shape=[jax.ShapeDtypeStruct((s, d), F32), jax.ShapeDtypeStruct((8, d), F32), jax.ShapeDtypeStruct((s, d), BF16)],
        compiler_params=_params("arbitrary"),
    )(dxo, x, g, dp, w_int)


C = D_GROUP


def _piece(ref, k):
    return ref[:, k * C:(k + 1) * C]


def _up(v, r):
    return v if r == 0 else pltpu.roll(v, v.shape[0] - r, 0)


def _down(v, r):
    return v if r == 0 else pltpu.roll(v, r, 0)


def _lane_group():
    lane = lax.broadcasted_iota(jnp.int32, (1, C), 1)
    return (lane >= POOL_GROUP).astype(jnp.int32) + (lane >= 2 * POOL_GROUP).astype(jnp.int32) + (
        lane >= 3 * POOL_GROUP).astype(jnp.int32)


def _by_group(grp, v2, v4, v8, v16):
    return jnp.where(grp == 0, v2, jnp.where(grp == 1, v4, jnp.where(grp == 2, v8, v16)))


def _pool_count(grp, row0, t):
    pos = (row0 + lax.broadcasted_iota(jnp.int32, (t, C), 0) + 1).astype(F32)
    return jnp.minimum(pos, _by_group(grp, 2.0, 4.0, 8.0, 16.0))


def _trailing_sums(ext, grp, t):
    s2 = ext + _down(ext, 1)
    s4 = s2 + _down(s2, 2)
    s8 = s4 + _down(s4, 4)
    s16 = s8 + _down(s8, 8)
    return _by_group(grp, s2, s4, s8, s16)[HALO:HALO + t]


def _leading_sums(ext, grp, t):
    s2 = ext + _up(ext, 1)
    s4 = s2 + _up(s2, 2)
    s8 = s4 + _up(s4, 4)
    s16 = s8 + _up(s8, 8)
    return _by_group(grp, s2, s4, s8, s16)[0:t]


def _head_select(r4, grp):
    assert HEAD_DIM == POOL_GROUP and N_HEADS == 4
    return _by_group(grp, *(r4[h * CHUNK:(h + 1) * CHUNK] for h in range(N_HEADS)))


def _conv_taps():
    return [(k, (k + 2) % 8, (k + 2) - (k + 2) % 8) for k in range(CONF_KERNEL)]


def _mixer_fwd(p, x1, cw, vec, pool_w, wstack, bias, w_out):
    s, d = x1.shape
    t = _tile(s, T_MIX_FWD)
    n_ext = t + HALO
    dm = w_out.shape[0]

    def body(p_ref, x1_ref, cw_ref, vec_ref, pw_ref, ws_ref, bias_ref, wo_ref, x2_ref, mix_s, z_ref, cy_s, cq_s, cx_s):
        i = pl.program_id(0)

        @pl.when(i == 0)
        def _():
            cy_s[...] = jnp.zeros_like(cy_s)
            cq_s[...] = jnp.zeros_like(cq_s)
            cx_s[...] = jnp.zeros_like(cx_s)

        grp = _lane_group()
        y = _piece(p_ref, 0) * _sigmoid(_piece(p_ref, 1))
        ext = jnp.concatenate([cy_s[...], y], axis=0)
        cy_s[...] = y[t - HALO:t]
        z = jnp.broadcast_to(vec_ref[0:1, :], (t, C))
        shifted = {}
        for k, r, off in _conv_taps():
            if r not in shifted:
                shifted[r] = _up(ext, r)
            z = z + cw_ref[k:k + 1, :] * shifted[r][off:off + t]
        z_ref[...] = z
        ln = _ln_fwd(z, vec_ref[1:2, :], vec_ref[2:3, :])[2]
        mix_s[:, 0:C] = (ln * _sigmoid(ln)).astype(BF16)
        q = _piece(p_ref, 3) * _piece(p_ref, 4)
        ext = jnp.concatenate([cq_s[...], q], axis=0)
        cq_s[...] = q[t - HALO:t]
        cz = vec_ref[8:9, :] * q + vec_ref[7:8, :] * _down(ext, 1)[HALO:] + vec_ref[6:7, :] * _down(ext, 2)[HALO:]
        mix_s[:, C:2 * C] = (_piece(p_ref, 2) * cz).astype(BF16)
        xp = _piece(p_ref, 5)
        ext = jnp.concatenate([cx_s[...], xp], axis=0)
        cx_s[...] = xp[t - HALO:t]
        dd = _trailing_sums(ext, grp, t) / _pool_count(grp, i * t, t) - xp
        mix_s[:, 2 * C:3 * C] = (_dot(dd.astype(BF16), pw_ref[...]) * vec_ref[3:4, :]).astype(BF16)
        vln = _ln_fwd(_piece(p_ref, 7), vec_ref[4:5, :], vec_ref[5:6, :])[2].astype(BF16)
        for n in range(t // CHUNK):
            rows = slice(n * CHUNK, (n + 1) * CHUNK)
            mixed = _head_select(_dot(ws_ref[...], vln[rows]), grp) + bias_ref[...]
            mix_s[rows, 3 * C:4 * C] = (p_ref[rows, 6 * C:7 * C] * mixed).astype(BF16)
        x2_ref[...] = x1_ref[...] + _dot(mix_s[...], wo_ref[...])

    full = lambda a: pl.BlockSpec(a.shape, lambda i: (0, 0))
    return pl.pallas_call(
        body, name="mixer_fwd", grid=(s // t,),
        in_specs=[pl.BlockSpec((t, p.shape[1]), lambda i: (i, 0)), pl.BlockSpec((t, d), lambda i: (i, 0)),
                  full(cw), full(vec), full(pool_w), full(wstack), full(bias), full(w_out)],
        out_specs=[pl.BlockSpec((t, d), lambda i: (i, 0)), pl.BlockSpec((t, dm), lambda i: (i, 0)),
                   pl.BlockSpec((t, C), lambda i: (i, 0))],
        out_shape=[jax.ShapeDtypeStruct((s, d), F32), jax.ShapeDtypeStruct((s, dm), BF16), jax.ShapeDtypeStruct((s, C), F32)],
        scratch_shapes=[pltpu.VMEM((HALO, C), F32)] * 3,
        compiler_params=_params("arbitrary"),
    )(p, x1, cw, vec, pool_w, wstack, bias, w_out)


def _mixer_bwd(dx2, p, z, cw, vec, pool_w, wstack, wstack_t, bias, tril4, head_rows, w_out):
    s, d = dx2.shape
    t = _tile(s, T_MIX_BWD)
    nt = s // t
    n_ext = t + HALO
    hb = t // HALO

    def body(dx2_ref, p_ref, ph_ref, z_ref, cw_ref, vec_ref, pw_ref, ws_ref, wst_ref, bias_ref, tril_ref, hr_ref, wo_ref,
             dp_ref, dcw_ref, dvec_ref, dpool_ref, dws_ref, dbs_ref, cdz_s, cdc_s, cf_s, vy_s, dvl_s, dbias_s):
        i = pl.program_id(0)
        tile = nt - 1 - i

        @pl.when(i == 0)
        def _():
            for ref in (cdz_s, cdc_s, cf_s, dbias_s, dcw_ref, dvec_ref, dpool_ref, dws_ref, dbs_ref):
                ref[...] = jnp.zeros_like(ref)

        grp = _lane_group()
        first = jnp.where(tile > 0, 1.0, 0.0)
        dmix = _dot_nt(dx2_ref[...].astype(BF16), wo_ref[...])
        d_a, d_b, d_c, d_d = (dmix[:, k * C:(k + 1) * C] for k in range(4))

        def acc_vec(row, v):
            dvec_ref[row:row + 1, :] += jnp.sum(v, axis=0, keepdims=True)

        val, gate = _piece(p_ref, 0), _piece(p_ref, 1)
        sgate = _sigmoid(gate)
        y = val * sgate
        y_halo = ph_ref[:, 0:C] * _sigmoid(ph_ref[:, C:2 * C]) * first
        ext = jnp.concatenate([y_halo, y], axis=0)
        for r in range(8):
            vy_s[r] = _up(ext, r)
        zn, rs, ln = _ln_fwd(z_ref[...], vec_ref[1:2, :], vec_ref[2:3, :])
        sg = _sigmoid(ln)
        dln = d_a * (sg * (1.0 + ln * (1.0 - sg)))
        dz, dg, db = _ln_bwd(dln, zn, rs, vec_ref[1:2, :])
        dvec_ref[1:2, :] += dg
        dvec_ref[2:3, :] += db
        acc_vec(0, dz)
        for k, r, off in _conv_taps():
            dcw_ref[k:k + 1, :] += jnp.sum(dz * vy_s[r, off:off + t, :], axis=0, keepdims=True)
        ext = jnp.concatenate([dz, cdz_s[...]], axis=0)
        cdz_s[...] = dz[0:HALO]
        dy = jnp.zeros((t, C), F32)
        shifted = {}
        for k in range(CONF_KERNEL):
            m = CONF_KERNEL - 1 - k
            r, off = m % 8, m - m % 8
            if r not in shifted:
                shifted[r] = _up(ext, r)
            dy = dy + cw_ref[k:k + 1, :] * shifted[r][off:off + t]
        dp_ref[:, 0:C] = (dy * sgate).astype(BF16)
        dp_ref[:, C:2 * C] = (dy * val * sgate * (1.0 - sgate)).astype(BF16)

        sb, sc, sx = _piece(p_ref, 2), _piece(p_ref, 3), _piece(p_ref, 4)
        q = sc * sx
        q_halo = ph_ref[:, 3 * C:4 * C] * ph_ref[:, 4 * C:5 * C] * first
        ext = jnp.concatenate([q_halo, q], axis=0)
        q1, q2 = _down(ext, 1)[HALO:], _down(ext, 2)[HALO:]
        cz = vec_ref[8:9, :] * q + vec_ref[7:8, :] * q1 + vec_ref[6:7, :] * q2
        dcz = d_b * sb
        dp_ref[:, 2 * C:3 * C] = (d_b * cz).astype(BF16)
        acc_vec(8, dcz * q)
        acc_vec(7, dcz * q1)
        acc_vec(6, dcz * q2)
        ext = jnp.concatenate([dcz, cdc_s[...]], axis=0)
        cdc_s[...] = dcz[0:HALO]
        dq = vec_ref[8:9, :] * dcz + vec_ref[7:8, :] * _up(ext, 1)[0:t] + vec_ref[6:7, :] * _up(ext, 2)[0:t]
        dp_ref[:, 3 * C:4 * C] = (dq * sx).astype(BF16)
        dp_ref[:, 4 * C:5 * C] = (dq * sc).astype(BF16)

        xp = _piece(p_ref, 5)
        ext = jnp.concatenate([ph_ref[:, 5 * C:6 * C] * first, xp], axis=0)
        cnt = _pool_count(grp, tile * t, t)
        dd = (_trailing_sums(ext, grp, t) / cnt - xp).astype(BF16)
        e2 = _dot(dd, pw_ref[...])
        acc_vec(3, d_c * e2)
        de = (d_c * vec_ref[3:4, :]).astype(BF16)
        dpool_ref[...] += _dot(_t_bf16(dd.astype(F32)), de)
        ddd = _dot_nt(de, pw_ref[...])
        fq = ddd / cnt
        ext = jnp.concatenate([fq, cf_s[...]], axis=0)
        cf_s[...] = fq[0:HALO]
        dp_ref[:, 5 * C:6 * C] = (_leading_sums(ext, grp, t) - ddd).astype(BF16)

        vn, vrs, vlnf = _ln_fwd(_piece(p_ref, 7), vec_ref[4:5, :], vec_ref[5:6, :])
        vln = vlnf.astype(BF16)
        for n in range(t // CHUNK):
            rows = slice(n * CHUNK, (n + 1) * CHUNK)
            mixed = _head_select(_dot(ws_ref[...], vln[rows]), grp) + bias_ref[...]
            dd_n = d_d[rows]
            dp_ref[rows, 6 * C:7 * C] = (dd_n * mixed).astype(BF16)
            dmx = dd_n * p_ref[rows, 6 * C:7 * C]
            dbias_s[...] += dmx
            dmx_b = dmx.astype(BF16)
            dvl_s[rows, :] = _head_select(_dot(wst_ref[...], dmx_b), grp)
            for h in range(N_HEADS):
                hrows = slice(h * CHUNK, (h + 1) * CHUNK)
                dws_ref[hrows, :] += _dot_nt(jnp.where(grp == h, dmx_b, jnp.zeros_like(dmx_b)), vln[rows])
        dvl = dvl_s[...]
        dv, dg, db = _ln_bwd(dvl, vn, vrs, vec_ref[4:5, :])
        dvec_ref[4:5, :] += dg
        dvec_ref[5:6, :] += db
        dp_ref[:, 7 * C:8 * C] = dv.astype(BF16)

        @pl.when(i == nt - 1)
        def _():
            dws_ref[...] = dws_ref[...] * tril_ref[...]
            dbs_ref[...] = lax.dot_general(hr_ref[...], dbias_s[...], NT, precision=lax.Precision.HIGHEST,
                                           preferred_element_type=F32)

    full = lambda a: pl.BlockSpec(a.shape, lambda i: (0, 0))
    acc = lambda shape: pl.BlockSpec(shape, lambda i: (0, 0))
    f = p.shape[1]
    return pl.pallas_call(
        body, name="mixer_bwd", grid=(nt,),
        in_specs=[pl.BlockSpec((t, d), lambda i: (nt - 1 - i, 0)), pl.BlockSpec((t, f), lambda i: (nt - 1 - i, 0)),
                  pl.BlockSpec((HALO, f), lambda i: (jnp.maximum((nt - 1 - i) * hb - 1, 0), 0)),
                  pl.BlockSpec((t, C), lambda i: (nt - 1 - i, 0)), full(cw), full(vec), full(pool_w), full(wstack), full(wstack_t), full(bias), full(tril4), full(head_rows),
                  full(w_out)],
        out_specs=[pl.BlockSpec((t, f), lambda i: (nt - 1 - i, 0)), acc((32, C)), acc((16, C)), acc((C, C)),
                   acc((N_HEADS * CHUNK, CHUNK)), acc((8, CHUNK))],
        out_shape=[jax.ShapeDtypeStruct((s, f), BF16), jax.ShapeDtypeStruct((32, C), F32), jax.ShapeDtypeStruct((16, C), F32),
                   jax.ShapeDtypeStruct((C, C), F32), jax.ShapeDtypeStruct((N_HEADS * CHUNK, CHUNK), F32),
                   jax.ShapeDtypeStruct((8, CHUNK), F32)],
        scratch_shapes=[pltpu.VMEM((HALO, C), F32)] * 3 + [pltpu.VMEM((8, n_ext, C), F32), pltpu.VMEM((t, C), F32),
                                                            pltpu.VMEM((CHUNK, C), F32)],
        compiler_params=_params("arbitrary"),
    )(dx2, p, p, z, cw, vec, pool_w, wstack, wstack_t, bias, tril4, head_rows, w_out)


def _loss_bwd(x, g, target):
    s, d = x.shape
    tm = _tile(s, 512)

    def body(x_ref, g_ref, t_ref, dx_ref, dg_ref, loss_ref):
        @pl.when(pl.program_id(0) == 0)
        def _():
            dg_ref[...] = jnp.zeros_like(dg_ref)
            loss_ref[...] = jnp.zeros_like(loss_ref)

        n, r, y = _rms(x_ref[...], g_ref[...])
        err = y - t_ref[...]
        loss_ref[...] += 0.5 * jnp.sum(jnp.mean(err * err, axis=-1, keepdims=True), axis=0, keepdims=True)
        dxr, dg = _rms_bwd(err * (1.0 / d), n, r, g_ref[...])
        dx_ref[...] = dxr
        dg_ref[0:1, :] += dg

    row = pl.BlockSpec((tm, d), lambda i: (i, 0))
    return pl.pallas_call(
        body, name="loss_bwd", grid=(s // tm,),
        in_specs=[row, pl.BlockSpec((1, d), lambda i: (0, 0)), row],
        out_specs=[row, pl.BlockSpec((8, d), lambda i: (0, 0)), pl.BlockSpec((8, LANES), lambda i: (0, 0))],
        out_shape=[jax.ShapeDtypeStruct((s, d), F32), jax.ShapeDtypeStruct((8, d), F32), jax.ShapeDtypeStruct((8, LANES), F32)],
        compiler_params=_params("arbitrary"),
    )(x, g, target)


def _adamw_math(w, g, m, v):
    m = ADAM_B1 * m + (1.0 - ADAM_B1) * g
    v = ADAM_B2 * v + (1.0 - ADAM_B2) * (g * g)
    m_hat = m / (1.0 - ADAM_B1 ** ADAM_STEP)
    v_hat = v / (1.0 - ADAM_B2 ** ADAM_STEP)
    return -ADAM_LR * (m_hat / (jnp.sqrt(v_hat) + ADAM_EPS) + ADAM_WD * w), m, v


def _adamw(w, g, m, v):
    r, c = w.shape
    tr = r // 8 if r % 64 == 0 else r

    def body(w_ref, g_ref, m_ref, v_ref, d_ref, mo_ref, vo_ref, go_ref):
        gv = g_ref[...]
        d_ref[...], mo_ref[...], vo_ref[...] = _adamw_math(w_ref[...], gv, m_ref[...], v_ref[...])
        go_ref[...] = gv

    blk = pl.BlockSpec((tr, c), lambda i: (i, 0))
    return pl.pallas_call(
        body, name="adamw", grid=(r // tr,), in_specs=[blk] * 4, out_specs=[blk] * 4,
        out_shape=[jax.ShapeDtypeStruct((r, c), F32)] * 4, compiler_params=_params("arbitrary"),
    )(w, g, m, v)


def _adamw_small(ws, gs, ms, vs):
    n = len(ws)

    def body(*refs):
        ins, outs = refs[:4 * n], refs[4 * n:]
        for k in range(n):
            dl, mo, vo = _adamw_math(ins[k][...], ins[n + k][...], ins[2 * n + k][...], ins[3 * n + k][...])
            outs[k][...], outs[n + k][...], outs[2 * n + k][...] = dl, mo, vo

    vm = pl.BlockSpec(memory_space=pltpu.VMEM)
    out = pl.pallas_call(
        body, name="adamw_small", in_specs=[vm] * (4 * n), out_specs=[vm] * (3 * n),
        out_shape=[jax.ShapeDtypeStruct(a.shape, F32) for a in ws] * 3,
        compiler_params=pltpu.CompilerParams(vmem_limit_bytes=VMEM_LIMIT),
    )(*ws, *gs, *ms, *vs)
    return out[:n], out[n:2 * n], out[2 * n:]


def _where_am_i():
    x, y, c = lax.axis_index("x"), lax.axis_index("y"), lax.axis_index("c")
    chips = [(1 - x, y), (x, 1 - y), (1 - x, 1 - y)]
    return x, y, c, chips


def _chip_id(chip):
    return 2 * chip[0] + chip[1]


def _landing_shapes(shards):
    return [jax.ShapeDtypeStruct((3, a.shape[0] // 2, a.shape[1]), a.dtype) for a in shards]


def _ici_gather_copies(ins, lands, send_sems, recv_sems):
    _, _, c, chips = _where_am_i()
    copies = []
    for a, src in enumerate(ins):
        hr = src.shape[0] // 2
        for j, chip in enumerate(chips):
            copies.append(pltpu.make_async_remote_copy(
                src_ref=src.at[pl.ds(c * hr, hr)], dst_ref=lands[a].at[j], send_sem=send_sems.at[3 * a + j],
                recv_sem=recv_sems.at[3 * a + j], device_id=(*chip, c), device_id_type=MESH))
    return copies


def _ici_reduce_copies(s_ref, o_ref, send_sems, recv_sems, local_sem):
    x, y, c, chips = _where_am_i()
    copies = [pltpu.make_async_remote_copy(src_ref=s_ref.at[_chip_id(chip)], dst_ref=o_ref.at[j], send_sem=send_sems.at[j],
                                           recv_sem=recv_sems.at[j], device_id=(*chip, c), device_id_type=MESH)
              for j, chip in enumerate(chips)]
    return copies, pltpu.make_async_copy(s_ref.at[_chip_id((x, y))], o_ref.at[3], local_sem)


def _wait_all(copies):
    for cp in copies:
        cp.wait_recv()
    for cp in copies:
        cp.wait_send()


def _ici_gather(shards, small):
    n = len(shards)

    def body(*refs):
        begin_small, finish_small = _small_gather_steps(refs[n], refs[2 * n + 1], *refs[2 * n + 4:])
        copies = _ici_gather_copies(refs[:n], refs[n + 1:2 * n + 1], refs[2 * n + 2], refs[2 * n + 3])
        begin_small()
        for cp in copies:
            cp.start()
        finish_small()
        _wait_all(copies)

    vm = pl.BlockSpec(memory_space=pltpu.VMEM)
    out = pl.pallas_call(
        body, name="ici_gather", in_specs=[ANY] * n + [vm], out_specs=[ANY] * n + [vm],
        out_shape=_landing_shapes(shards) + [jax.ShapeDtypeStruct((N_DEV * small.shape[0], small.shape[1]), F32)],
        scratch_shapes=[pltpu.SemaphoreType.DMA((3 * n,))] * 2 + _small_gather_sems(),
    )(*shards, small)
    return list(out[:n]), out[n]


SLOTS = 4


def _sibling_gather(shards, lands):
    n = len(shards)
    d = shards[0].shape[1]
    halves = [a.shape[0] // 2 for a in shards]
    hmax = max(halves)
    chunks = [(a, j, halves[a]) for a in range(n) for j in range(3)]
    nc = len(chunks)

    def body(*refs):
        ins, lnd, outs = refs[:n], refs[n:2 * n], refs[2 * n:3 * n]
        sbuf, rbuf, obuf, ld_sems, take_sems, send_sems, recv_sems, place_sems, own_ld_sems, own_st_sems, credits = refs[3 * n:]
        x, y, c, chips = _where_am_i()
        sibling = (x, y, 1 - c)
        me = _chip_id((x, y))

        def load(i):
            a, j, hr = chunks[i]
            return pltpu.make_async_copy(lnd[a].at[j], sbuf.at[i % SLOTS, pl.ds(0, hr)], ld_sems.at[i % SLOTS])

        def push(i):
            hr, slot = chunks[i][2], i % SLOTS
            return pltpu.make_async_remote_copy(src_ref=sbuf.at[slot, pl.ds(0, hr)], dst_ref=rbuf.at[slot, pl.ds(0, hr)],
                                                send_sem=send_sems.at[slot], recv_sem=recv_sems.at[slot], device_id=sibling,
                                                device_id_type=MESH)

        def take(i):
            a, j, hr = chunks[i]
            return pltpu.make_async_copy(rbuf.at[i % SLOTS, pl.ds(0, hr)],
                                         outs[a].at[_chip_id(chips[j]), pl.ds((1 - c) * hr, hr)], take_sems.at[i % SLOTS])

        def place(i):
            a, j, hr = chunks[i]
            return pltpu.make_async_copy(sbuf.at[i % SLOTS, pl.ds(0, hr)], outs[a].at[_chip_id(chips[j]), pl.ds(c * hr, hr)],
                                         place_sems.at[i % SLOTS])

        own = [(a, h, halves[a]) for a in range(n) for h in range(2)]

        def own_load(k):
            a, h, hr = own[k]
            return pltpu.make_async_copy(ins[a].at[pl.ds(h * hr, hr)], obuf.at[k % 2, pl.ds(0, hr)], own_ld_sems.at[k % 2])

        def own_store(k):
            a, h, hr = own[k]
            return pltpu.make_async_copy(obuf.at[k % 2, pl.ds(0, hr)], outs[a].at[me, pl.ds(h * hr, hr)], own_st_sems.at[k % 2])

        def own_step(k):
            if k < len(own):
                if k >= 2:
                    own_store(k - 2).wait()
                own_load(k).start()
            if 1 <= k <= len(own):
                own_load(k - 1).wait()
                own_store(k - 1).start()

        for i in range(min(2, nc)):
            load(i).start()
        for i in range(nc):
            own_step(i)
            if i >= 2:
                push(i - 2).wait_send()
                place(i - 2).wait()
            if i + 2 < nc:
                load(i + 2).start()
            load(i).wait()
            place(i).start()
            if i >= SLOTS:
                pl.semaphore_wait(credits.at[i % SLOTS], 1)
            push(i).start()
            if i >= 1:
                push(i - 1).wait_recv()
                take(i - 1).start()
            if i >= 2:
                take(i - 2).wait()
                if i - 2 + SLOTS < nc:
                    pl.semaphore_signal(credits.at[(i - 2) % SLOTS], inc=1, device_id=sibling, device_id_type=MESH)
        for i in range(max(0, nc - 2), nc):
            push(i).wait_send()
            place(i).wait()
        push(nc - 1).wait_recv()
        take(nc - 1).start()
        for i in range(max(0, nc - 2), nc):
            take(i).wait()
        for k in range(nc, len(own) + 1):
            own_step(k)
        for k in range(max(0, len(own) - 2), len(own)):
            own_store(k).wait()

    dma = pltpu.SemaphoreType.DMA((SLOTS,))
    dma2 = pltpu.SemaphoreType.DMA((2,))
    return pl.pallas_call(
        body, name="sibling_gather", in_specs=[ANY] * (2 * n), out_specs=[ANY] * n,
        out_shape=[jax.ShapeDtypeStruct((N_CHIPS,) + a.shape, a.dtype) for a in shards],
        scratch_shapes=[pltpu.VMEM((SLOTS, hmax, d), BF16), pltpu.VMEM((SLOTS, hmax, d), BF16), pltpu.VMEM((2, hmax, d), BF16),
                        dma, dma, dma, dma, dma, dma2, dma2, pltpu.SemaphoreType.REGULAR((SLOTS,))],
        compiler_params=pltpu.CompilerParams(vmem_limit_bytes=VMEM_LIMIT),
    )(*shards, *lands)


def _pair_sum(grads):
    n = len(grads)
    halves = [g.shape[1] // 2 for g in grads]
    total, rows = sum(halves), max(halves)
    d = grads[0].shape[2]
    chunks, off = [], 0
    for a in range(n):
        chunks += [(a, k, off, halves[a]) for k in range(N_CHIPS)]
        off += halves[a]
    nc = len(chunks)

    def body(*refs):
        ins, out_ref = refs[:n], refs[n]
        sbuf, rbuf, mbuf, obuf, ls_sems, lm_sems, st_sems, send_sems, recv_sems, credits = refs[n + 1:]
        x, y, c, _ = _where_am_i()
        sibling = (x, y, 1 - c)

        def load_theirs(i):
            a, k, _, hr = chunks[i]
            return pltpu.make_async_copy(ins[a].at[k, pl.ds((1 - c) * hr, hr)], sbuf.at[i % SLOTS, pl.ds(0, hr)], ls_sems.at[i % SLOTS])

        def load_mine(i):
            a, k, _, hr = chunks[i]
            return pltpu.make_async_copy(ins[a].at[k, pl.ds(c * hr, hr)], mbuf.at[i % SLOTS, pl.ds(0, hr)], lm_sems.at[i % SLOTS])

        def push(i):
            hr, slot = chunks[i][3], i % SLOTS
            return pltpu.make_async_remote_copy(src_ref=sbuf.at[slot, pl.ds(0, hr)], dst_ref=rbuf.at[slot, pl.ds(0, hr)],
                                                send_sem=send_sems.at[slot], recv_sem=recv_sems.at[slot], device_id=sibling,
                                                device_id_type=MESH)

        def store(i):
            _, k, o, hr = chunks[i]
            slot = i % SLOTS
            return pltpu.make_async_copy(obuf.at[slot, pl.ds(0, hr)], out_ref.at[k, pl.ds(o, hr)], st_sems.at[slot])

        def start_push(i):
            load_theirs(i).wait()
            if i >= SLOTS:
                pl.semaphore_wait(credits.at[i % SLOTS], 1)
            push(i).start()

        for i in range(min(2, nc)):
            load_theirs(i).start()
            load_mine(i).start()
        start_push(0)
        for i in range(nc):
            hr, slot = chunks[i][3], i % SLOTS
            if i + 2 < nc:
                load_theirs(i + 2).start()
                load_mine(i + 2).start()
            if i + 1 < nc:
                start_push(i + 1)
            push(i).wait_recv()
            push(i).wait_send()
            load_mine(i).wait()
            if i >= SLOTS:
                store(i - SLOTS).wait()
            obuf[slot, 0:hr, :] = (mbuf[slot, 0:hr, :].astype(F32) + rbuf[slot, 0:hr, :].astype(F32)).astype(BF16)
            if i + SLOTS < nc:
                pl.semaphore_signal(credits.at[slot], inc=1, device_id=sibling, device_id_type=MESH)
            store(i).start()
        for i in range(max(0, nc - SLOTS), nc):
            store(i).wait()

    stage = pltpu.VMEM((SLOTS, rows, d), BF16)
    dma = pltpu.SemaphoreType.DMA((SLOTS,))
    return pl.pallas_call(
        body, name="pair_sum", in_specs=[ANY] * n, out_specs=ANY, out_shape=jax.ShapeDtypeStruct((N_CHIPS, total, d), BF16),
        scratch_shapes=[stage, stage, stage, stage, dma, dma, dma, dma, dma, pltpu.SemaphoreType.REGULAR((SLOTS,))],
        compiler_params=pltpu.CompilerParams(vmem_limit_bytes=VMEM_LIMIT),
    )(*grads)


def _reduce_chips(sums):
    def body(s_ref, o_ref, send_sems, recv_sems, local_sem):
        copies, local = _ici_reduce_copies(s_ref, o_ref, send_sems, recv_sems, local_sem)
        local.start()
        for cp in copies:
            cp.start()
        _wait_all(copies)
        local.wait()

    return pl.pallas_call(
        body, name="reduce_chips", in_specs=[ANY], out_specs=ANY, out_shape=jax.ShapeDtypeStruct(sums.shape, BF16),
        scratch_shapes=[pltpu.SemaphoreType.DMA((3,)), pltpu.SemaphoreType.DMA((3,)), pltpu.SemaphoreType.DMA],
    )(sums)


def _sum_share(parts, groups, rows, small):
    layers, n, n_g = len(parts), len(rows), len(groups)
    parts = [part for layer in parts for part in layer]
    n_l = len(parts)
    d = parts[0].shape[2]
    halves = [r // 2 for r in rows]
    hmax = max(halves)
    chunks = []
    for l in range(layers):
        for g, members in enumerate(groups):
            off = 0
            for a in members:
                chunks.append((l * n_g + g, a, off, halves[a], l))
                off += halves[a]
    nc = len(chunks)

    def body(*refs):
        ins, small_ref, outs, total_ref = refs[:n_l], refs[n_l], refs[n_l + 1:n_l + 1 + n], refs[n_l + 1 + n]
        pbuf, obuf, rbuf, ld_sems, keep_sems, take_sems, send_sems, recv_sems, credits, all_ref = refs[n_l + 2 + n:n_l + 12 + n]
        begin_small, finish_small = _small_gather_steps(small_ref, all_ref, *refs[n_l + 12 + n:])
        begin_small()
        x, y, c, _ = _where_am_i()
        sibling = (x, y, 1 - c)

        def load(i):
            part, _, off, hr, _ = chunks[i]
            return pltpu.make_async_copy(ins[part].at[:, pl.ds(off, hr)], pbuf.at[i % SLOTS, :, pl.ds(0, hr)], ld_sems.at[i % SLOTS])

        def keep(i):
            _, a, _, hr, l = chunks[i]
            return pltpu.make_async_copy(obuf.at[i % SLOTS, pl.ds(0, hr)], outs[a].at[l, pl.ds(c * hr, hr)], keep_sems.at[i % SLOTS])

        def push(i):
            hr, slot = chunks[i][3], i % SLOTS
            return pltpu.make_async_remote_copy(src_ref=obuf.at[slot, pl.ds(0, hr)], dst_ref=rbuf.at[slot, pl.ds(0, hr)],
                                                send_sem=send_sems.at[slot], recv_sem=recv_sems.at[slot], device_id=sibling,
                                                device_id_type=MESH)

        def take(i):
            _, a, _, hr, l = chunks[i]
            return pltpu.make_async_copy(rbuf.at[i % SLOTS, pl.ds(0, hr)], outs[a].at[l, pl.ds((1 - c) * hr, hr)],
                                         take_sems.at[i % SLOTS])

        for i in range(min(2, nc)):
            load(i).start()
        for i in range(nc):
            hr, slot = chunks[i][3], i % SLOTS
            if i + 2 < nc:
                load(i + 2).start()
            load(i).wait()
            if i >= SLOTS:
                keep(i - SLOTS).wait()
                push(i - SLOTS).wait_send()
            part = lambda k: pbuf[slot, k, 0:hr, :].astype(F32)
            obuf[slot, 0:hr, :] = ((part(3) + part(0)) + part(1)) + part(2)
            keep(i).start()
            if i >= SLOTS:
                pl.semaphore_wait(credits.at[slot], 1)
            push(i).start()
            if i >= 1:
                push(i - 1).wait_recv()
                take(i - 1).start()
            if i >= 2:
                take(i - 2).wait()
                if i - 2 + SLOTS < nc:
                    pl.semaphore_signal(credits.at[(i - 2) % SLOTS], inc=1, device_id=sibling, device_id_type=MESH)
        push(nc - 1).wait_recv()
        take(nc - 1).start()
        for i in range(max(0, nc - 2), nc):
            take(i).wait()
        for i in range(max(0, nc - SLOTS), nc):
            keep(i).wait()
            push(i).wait_send()
        finish_small()
        total_ref[...] = _sum_blocks(all_ref, small.shape[0])

    dma = pltpu.SemaphoreType.DMA((SLOTS,))
    vm = pl.BlockSpec(memory_space=pltpu.VMEM)
    out = pl.pallas_call(
        body, name="sum_share", in_specs=[ANY] * n_l + [vm], out_specs=[ANY] * n + [vm],
        out_shape=[jax.ShapeDtypeStruct((layers, r, d), F32) for r in rows] + [jax.ShapeDtypeStruct(small.shape, F32)],
        scratch_shapes=[pltpu.VMEM((SLOTS, N_CHIPS, hmax, d), BF16), pltpu.VMEM((SLOTS, hmax, d), F32),
                        pltpu.VMEM((SLOTS, hmax, d), F32), dma, dma, dma, dma, dma, pltpu.SemaphoreType.REGULAR((SLOTS,)),
                        pltpu.VMEM((N_DEV * small.shape[0], small.shape[1]), F32)] + _small_gather_sems(),
        compiler_params=pltpu.CompilerParams(vmem_limit_bytes=VMEM_LIMIT),
    )(*parts, small)
    return out[:n], out[n]


def _small_gather_sems():
    return [pltpu.SemaphoreType.DMA((7,)), pltpu.SemaphoreType.DMA((7,)), pltpu.SemaphoreType.DMA]


def _sum_blocks(all_ref, m):
    total = all_ref[0:m, :]
    for dev in range(1, N_DEV):
        total = total + all_ref[dev * m:(dev + 1) * m, :]
    return total


def _small_gather_steps(x_ref, all_ref, send_sems, recv_sems, local_sem):
    m = x_ref.shape[0]
    x, y, c, chips = _where_am_i()
    me, sibling = (x, y, c), (x, y, 1 - c)

    def rows(px, py, pc):
        return all_ref.at[pl.ds((4 * px + 2 * py + pc) * m, m), :]

    def copy(k, blk, to, src=None):
        return pltpu.make_async_remote_copy(src_ref=rows(*blk) if src is None else src, dst_ref=rows(*blk),
                                            send_sem=send_sems.at[k], recv_sem=recv_sems.at[k], device_id=to,
                                            device_id_type=MESH)

    def own():
        return pltpu.make_async_copy(x_ref, rows(*me), local_sem)

    def first():
        return [copy(0, me, sibling, src=x_ref)] + [copy(1 + j, me, (*chip, c), src=x_ref) for j, chip in enumerate(chips)]

    def passed():
        return [copy(4 + j, (*chip, c), sibling) for j, chip in enumerate(chips)]

    def begin():
        own().start()
        for cp in first():
            cp.start()

    def finish():
        forwards = passed()
        for j, chip in enumerate(chips):
            copy(1 + j, (*chip, c), me).wait_recv()
            forwards[j].start()
        copy(0, sibling, me).wait_recv()
        for j, chip in enumerate(chips):
            copy(4 + j, (*chip, 1 - c), me).wait_recv()
        for cp in first() + forwards:
            cp.wait_send()
        own().wait()

    return begin, finish


def _pack(arrays):
    flat = jnp.concatenate([a.reshape(-1) for a in arrays])
    pad = (-flat.shape[0]) % (8 * LANES)
    return jnp.pad(flat, (0, pad)).reshape(-1, LANES)


def _unpack(buf, shapes):
    flat = buf.reshape(-1)
    out, off = [], 0
    for shp in shapes:
        size = 1
        for dim in shp:
            size *= dim
        out.append(flat[off:off + size].reshape(shp))
        off += size
    return out


BIG = ("ffn1_w1", "ffn1_w3", "ffn1_w2", "w_in", "w_out", "ffn2_w1", "ffn2_w3", "ffn2_w2")
TRANSPOSED = ("ffn1_w1", "ffn1_w3", "w_in", "ffn2_w1", "ffn2_w3")
SMALL = ("ffn1_norm", "mix_norm", "conf_conv_w", "conf_conv_b", "conf_ln_g", "conf_ln_b", "sconv_w", "pool_w", "pool_scale",
         "gmlp_ln_g", "gmlp_ln_b", "gmlp_w_s", "gmlp_b_s", "ffn2_norm", "final_norm")
ORDER = ("ffn1_norm", "ffn1_w1", "ffn1_w3", "ffn1_w2", "mix_norm", "w_in", "conf_conv_w", "conf_conv_b", "conf_ln_g", "conf_ln_b",
         "sconv_w", "pool_w", "pool_scale", "gmlp_ln_g", "gmlp_ln_b", "gmlp_w_s", "gmlp_b_s", "w_out", "ffn2_norm", "ffn2_w1",
         "ffn2_w3", "ffn2_w2", "final_norm")


def _as2d(a):
    return a.reshape(-1, a.shape[-1])


def kernel(x, ffn1_norm, ffn1_w1, ffn1_w3, ffn1_w2, mix_norm, w_in, conf_conv_w, conf_conv_b, conf_ln_g, conf_ln_b, sconv_w, pool_w, pool_scale, gmlp_ln_g, gmlp_ln_b, gmlp_w_s, gmlp_b_s, w_out, ffn2_norm, ffn2_w1, ffn2_w3, ffn2_w2, final_norm, loss_target, m_ffn1_norm, m_ffn1_w1, m_ffn1_w3, m_ffn1_w2, m_mix_norm, m_w_in, m_conf_conv_w, m_conf_conv_b, m_conf_ln_g, m_conf_ln_b, m_sconv_w, m_pool_w, m_pool_scale, m_gmlp_ln_g, m_gmlp_ln_b, m_gmlp_w_s, m_gmlp_b_s, m_w_out, m_ffn2_norm, m_ffn2_w1, m_ffn2_w3, m_ffn2_w2, m_final_norm, v_ffn1_norm, v_ffn1_w1, v_ffn1_w3, v_ffn1_w2, v_mix_norm, v_w_in, v_conf_conv_w, v_conf_conv_b, v_conf_ln_g, v_conf_ln_b, v_sconv_w, v_pool_w, v_pool_scale, v_gmlp_ln_g, v_gmlp_ln_b, v_gmlp_w_s, v_gmlp_b_s, v_w_out, v_ffn2_norm, v_ffn2_w1, v_ffn2_w3, v_ffn2_w2, v_final_norm):
    given = dict(locals())
    w = {k: given[k] for k in ORDER}
    mom = {k: given["m_" + k] for k in ORDER}
    var = {k: given["v_" + k] for k in ORDER}
    n_l = ffn1_w1.shape[0]
    xs = x[0]
    d = xs.shape[1]
    chip = 2 * lax.axis_index("x") + lax.axis_index("y")

    def shard(name, l):
        a = w[name][l]
        return (jnp.swapaxes(w[name], 1, 2)[l] if name in TRANSPOSED else a).astype(BF16)

    groups = (BIG[:3], BIG[3:])

    def shards_of(l, g):
        return [shard(name, l) for name in groups[g]] if l < n_l else []

    def finish_gather(l, g, lands):
        out = _sibling_gather(shards_of(l, g), lands)
        return {name: a.reshape(-1, d) for name, a in zip(groups[g], out)}

    shard_rows = [w[name].shape[2] if name in TRANSPOSED else w[name].shape[1] for name in BIG]

    conv_shapes = [conf_conv_w.shape, sconv_w.shape]
    first_lands, conv_all = _ici_gather(shards_of(0, 0), _pack([conf_conv_w, sconv_w]))
    conv_all = conv_all.reshape(N_CHIPS, 2, -1)[:, 0]
    conf_full, sconv_full = [jnp.concatenate([_unpack(conv_all[k], conv_shapes)[a] for k in range(N_CHIPS)], axis=-1)
                             for a in range(2)]

    lane = jnp.arange(C) // HEAD_DIM
    head_rows = (jnp.arange(8)[:, None] == lane[None, :]).astype(F32)
    tril = jnp.tril(jnp.ones((CHUNK, CHUNK), F32))
    tril4 = jnp.tile(tril, (N_HEADS, 1))
    mixer_consts = []
    for l in range(n_l):
        cw = jnp.pad(conf_full[l], ((0, 32 - CONF_KERNEL), (0, 0)))
        vec = jnp.concatenate([conf_conv_b[l][None], conf_ln_g[l][None], conf_ln_b[l][None], pool_scale[l][None],
                               gmlp_ln_g[l][None], gmlp_ln_b[l][None], sconv_full[l], jnp.zeros((7, C), F32)], axis=0)
        eye = jnp.eye(len(pool_w[l]), dtype=F32)
        pool_blk = (eye[:, None, :, None] * pool_w[l][:, :, None, :]).reshape(C, C).astype(BF16)
        ws = gmlp_w_s[l] * tril[None]
        wstack = ws.reshape(N_HEADS * CHUNK, CHUNK).astype(BF16)
        wstack_t = jnp.swapaxes(ws, 1, 2).reshape(N_HEADS * CHUNK, CHUNK).astype(BF16)
        bias = jnp.repeat(gmlp_b_s[l].T, HEAD_DIM, axis=1)
        mixer_consts.append((cw, vec, pool_blk, wstack, wstack_t, bias))

    saved = []
    cur = xs
    gathered = [finish_gather(0, 0, first_lands)]
    for l in range(n_l):
        gw = gathered[l]
        cw, vec, pool_blk, wstack, wstack_t, bias = mixer_consts[l]
        x0 = cur
        x1, a1, b1, h1, lands = _ffn_fwd(x0, ffn1_norm[l][None], gw["ffn1_w1"], gw["ffn1_w3"], gw["ffn1_w2"], shards_of(l, 1))
        gw.update(finish_gather(l, 1, lands))
        p = _proj_fwd(x1, mix_norm[l][None], gw["w_in"])
        x2, mix, z = _mixer_fwd(p, x1, cw, vec, pool_blk, wstack, bias, gw["w_out"])
        x3, a2, b2, h2, lands = _ffn_fwd(x2, ffn2_norm[l][None], gw["ffn2_w1"], gw["ffn2_w3"], gw["ffn2_w2"], shards_of(l + 1, 0))
        if l + 1 < n_l:
            gathered.append(finish_gather(l + 1, 0, lands))
        saved.append((x0, x1, x2, a1, b1, a2, b2, p, mix, z, h1, h2))
        cur = x3

    dx, dg_final, loss_part = _loss_bwd(cur, final_norm[None], loss_target[0])

    small_parts = [None] * n_l
    reduced_halves = [[None, None] for _ in range(n_l)]
    pair_sum = lambda big, g: _pair_sum([big[name].reshape(N_CHIPS, -1, d) for name in groups[g]])
    pending = None
    for l in reversed(range(n_l)):
        gw = gathered[l]
        cw, vec, pool_blk, wstack, wstack_t, bias = mixer_consts[l]
        x0, x1, x2, a1, b1, a2, b2, p, mix, z, h1, h2 = saved[l]
        big = {}
        out = _ffn_bwd(dx, x2, ffn2_norm[l][None], a2, b2, gw["ffn2_w1"], gw["ffn2_w3"], gw["ffn2_w2"], pending)
        dx, dg_ffn2, dy, da, db, u, dx_bf = out[:7]
        if pending is not None:
            reduced_halves[l + 1][0] = out[7]
        big["ffn2_w1"], big["ffn2_w3"], big["ffn2_w2"] = _dw(da, h2), _dw(db, h2), _dw(u, dy)
        big["w_out"] = _dw(mix, dx_bf)
        dp, dcw, dvec, dpool, dws, dbs = _mixer_bwd(dx, p, z, cw, vec, pool_blk, wstack, wstack_t, bias, tril4, head_rows, gw["w_out"])
        dx, dg_mix, h = _proj_bwd(dx, x1, mix_norm[l][None], dp, gw["w_in"])
        big["w_in"] = _dw(dp, h)
        second = pair_sum(big, 1)
        out = _ffn_bwd(dx, x0, ffn1_norm[l][None], a1, b1, gw["ffn1_w1"], gw["ffn1_w3"], gw["ffn1_w2"], second)
        dx, dg_ffn1, dy, da, db, u = out[:6]
        reduced_halves[l][1] = out[7]
        big["ffn1_w1"], big["ffn1_w3"], big["ffn1_w2"] = _dw(da, h1), _dw(db, h1), _dw(u, dy)
        small_parts[l] = [dg_ffn1[0], dg_mix[0], dg_ffn2[0], dcw, dvec, dpool, dws, dbs]
        pending = pair_sum(big, 0)
    reduced_halves[0][0] = _reduce_chips(pending)
    grad_x = dx[None]

    part_shapes = [a.shape for a in small_parts[0]]
    tail = [dg_final[0], loss_part[0]]
    packed = _pack([a for l in range(n_l) for a in small_parts[l]] + tail)
    index_of = {name: a for a, name in enumerate(BIG)}
    full, summed = _sum_share(reduced_halves, [[index_of[name] for name in g] for g in groups], shard_rows, packed)
    full = dict(zip(BIG, full))
    grad = {}
    summed = _unpack(summed, part_shapes * n_l + [a.shape for a in tail])
    per_layer = [summed[l * len(part_shapes):(l + 1) * len(part_shapes)] for l in range(n_l)]
    stack = lambda k: jnp.stack([per_layer[l][k] for l in range(n_l)])
    dcw_all, dvec_all, dpool_all, dws_all, dbs_all = stack(3), stack(4), stack(5), stack(6), stack(7)
    loss = summed[-1][0]
    chip_cols = lambda a: lax.dynamic_slice_in_dim(a, chip * (C // N_CHIPS), C // N_CHIPS, axis=2)
    n_pool = pool_w.shape[1]
    grad.update(
        ffn1_norm=stack(0), mix_norm=stack(1), ffn2_norm=stack(2), final_norm=summed[-2],
        conf_conv_w=chip_cols(dcw_all[:, :CONF_KERNEL]), conf_conv_b=dvec_all[:, 0], conf_ln_g=dvec_all[:, 1],
        conf_ln_b=dvec_all[:, 2], pool_scale=dvec_all[:, 3], gmlp_ln_g=dvec_all[:, 4], gmlp_ln_b=dvec_all[:, 5],
        sconv_w=chip_cols(dvec_all[:, 6:6 + SHORT_KERNEL]),
        pool_w=jnp.stack([dpool_all[:, g * POOL_GROUP:(g + 1) * POOL_GROUP, g * POOL_GROUP:(g + 1) * POOL_GROUP]
                          for g in range(n_pool)], axis=1),
        gmlp_w_s=dws_all.reshape(n_l, N_HEADS, CHUNK, CHUNK), gmlp_b_s=dbs_all[:, :N_HEADS],
    )

    delta, new_m, new_v = {}, {}, {}
    for name in BIG:
        rows_of = (lambda a: jnp.swapaxes(a, 1, 2)) if name in TRANSPOSED else (lambda a: a)
        shp = full[name].shape
        out = _adamw(_as2d(rows_of(w[name])), _as2d(full[name]), _as2d(rows_of(mom[name])), _as2d(rows_of(var[name])))
        delta[name], new_m[name], new_v[name], grad[name] = (rows_of(o.reshape(shp)) for o in out)
    ds, ms, vs = _adamw_small([_as2d(w[k]) if w[k].ndim > 1 else w[k][None] for k in SMALL],
                              [_as2d(grad[k]) if grad[k].ndim > 1 else grad[k][None] for k in SMALL],
                              [_as2d(mom[k]) if mom[k].ndim > 1 else mom[k][None] for k in SMALL],
                              [_as2d(var[k]) if var[k].ndim > 1 else var[k][None] for k in SMALL])
    for k, dl, mo, vo in zip(SMALL, ds, ms, vs):
        delta[k], new_m[k], new_v[k] = dl.reshape(w[k].shape), mo.reshape(w[k].shape), vo.reshape(w[k].shape)

    return (loss, grad_x, *[grad[k] for k in ORDER], *[delta[k] for k in ORDER], *[new_m[k] for k in ORDER],
            *[new_v[k] for k in ORDER])
```

```python
import functools

import jax
import jax.numpy as jnp
from jax import lax
from jax.experimental import pallas as pl
from jax.experimental.pallas import tpu as pltpu

F32 = jnp.float32
BF16 = jnp.bfloat16
MESH = pl.DeviceIdType.MESH
ANY = pl.BlockSpec(memory_space=pl.ANY)

EPS = 1e-6
FFN_RESIDUAL = 0.5
D_GROUP = 256
CONF_KERNEL = 31
SHORT_KERNEL = 3
POOL_GROUP = 64
CHUNK = 128
N_HEADS = 4
HEAD_DIM = 64
HALO = 32
N_CHIPS = 4
N_DEV = 8
LANES = 128
MXU_TILE = 256
VMEM_LIMIT = 56 * 2**20
T_MIX_FWD = 1024
T_MIX_BWD = 512

ADAM_LR = 0.001
ADAM_B1 = 0.9
ADAM_B2 = 0.999
ADAM_EPS = 1e-08
ADAM_WD = 0.01
ADAM_STEP = 10

NT = (((1,), (1,)), ((), ()))
TN = (((0,), (0,)), ((), ()))


def _params(*sem):
    return pltpu.CompilerParams(dimension_semantics=sem, vmem_limit_bytes=VMEM_LIMIT)


def _dot(a, b):
    return jnp.dot(a, b, preferred_element_type=F32)


def _dot_nt(a, b):
    return lax.dot_general(a, b, NT, preferred_element_type=F32)


def _t_bf16(v):
    return jnp.transpose(v).astype(BF16)


def _sigmoid(v):
    return 1.0 / (1.0 + jnp.exp(-v))


def _rms(x, g):
    r = lax.rsqrt(jnp.mean(x * x, axis=-1, keepdims=True) + EPS)
    n = x * r
    return n, r, n * g


def _rms_bwd(dh, n, r, g):
    dn = dh * g
    dx = r * (dn - n * jnp.mean(dn * n, axis=-1, keepdims=True))
    return dx, jnp.sum(dh * n, axis=0, keepdims=True)


def _ln_fwd(z, g, b):
    mu = jnp.mean(z, axis=-1, keepdims=True)
    zc = z - mu
    rs = lax.rsqrt(jnp.mean(zc * zc, axis=-1, keepdims=True) + EPS)
    zn = zc * rs
    return zn, rs, zn * g + b


def _ln_bwd(dl, zn, rs, g):
    dzn = dl * g
    dz = rs * (dzn - jnp.mean(dzn, axis=-1, keepdims=True) - zn * jnp.mean(dzn * zn, axis=-1, keepdims=True))
    return dz, jnp.sum(dl * zn, axis=0, keepdims=True), jnp.sum(dl, axis=0, keepdims=True)


def _tile(n, want):
    return want if n % want == 0 else n


def _resident(shape):
    return pl.BlockSpec(shape, lambda i: (0,) * len(shape), pipeline_mode=pl.Buffered(1))


def _ffn_fwd(x, g, w1t, w3t, w2, shards=()):
    s, d = x.shape
    f = w1t.shape[0]
    tm, tf = _tile(s, 512), MXU_TILE
    nj = f // tf
    ni = s // tm
    n_c = len(shards)

    def body(*refs):
        x_ref, g_ref, w1_ref, w3_ref, w2_ref = refs[:5]
        xo_ref, a_ref, b_ref, h_ref = refs[5 + n_c:9 + n_c]
        u_s = refs[9 + 2 * n_c]
        if n_c:
            gather = lambda: _ici_gather_copies(refs[5:5 + n_c], refs[9 + n_c:9 + 2 * n_c], refs[10 + 2 * n_c], refs[11 + 2 * n_c])

            @pl.when(pl.program_id(0) == 0)
            def _():
                for cp in gather():
                    cp.start()

        h = _rms(x_ref[...], g_ref[...])[2].astype(BF16)
        h_ref[...] = h
        for j in range(nj):
            cols = slice(j * tf, (j + 1) * tf)
            a = _dot_nt(h, w1_ref[cols, :])
            b = _dot_nt(h, w3_ref[cols, :])
            a_ref[:, cols] = a.astype(BF16)
            b_ref[:, cols] = b.astype(BF16)
            u_s[:, cols] = ((a * _sigmoid(a)) * b).astype(BF16)
        xo_ref[...] = x_ref[...] + FFN_RESIDUAL * _dot(u_s[...], w2_ref[...])
        if n_c:
            @pl.when(pl.program_id(0) == ni - 1)
            def _():
                _wait_all(gather())

    row = pl.BlockSpec((tm, d), lambda i: (i, 0))
    hid = pl.BlockSpec((tm, f), lambda i: (i, 0))
    sems = [pltpu.SemaphoreType.DMA((3 * n_c,))] * 2 if n_c else []
    out = pl.pallas_call(
        body, name="ffn_fwd_gather" if n_c else "ffn_fwd", grid=(ni,),
        in_specs=[row, _resident(g.shape), _resident(w1t.shape), _resident(w3t.shape), _resident(w2.shape)] + [ANY] * n_c,
        out_specs=[row, hid, hid, row] + [ANY] * n_c,
        out_shape=[jax.ShapeDtypeStruct((s, d), F32), jax.ShapeDtypeStruct((s, f), BF16), jax.ShapeDtypeStruct((s, f), BF16),
                   jax.ShapeDtypeStruct((s, d), BF16)] + _landing_shapes(shards),
        scratch_shapes=[pltpu.VMEM((tm, f), BF16)] + sems,
        compiler_params=_params("arbitrary"),
    )(x, g, w1t, w3t, w2, *shards)
    return out[0], out[1], out[2], out[3], list(out[4:])


def _ffn_bwd(dxo, x, g, a, b, w1t, w3t, w2, sums=None):
    s, d = x.shape
    f = w1t.shape[0]
    tm, tf = _tile(s, 256), MXU_TILE
    nj = f // tf
    ni = s // tm
    n_c = 0 if sums is None else 1

    def body(*refs):
        dxo_ref, x_ref, g_ref, a_ref, b_ref, w1_ref, w3_ref, w2_ref = refs[:8]
        dx_ref, dg_ref, dy_ref, da_ref, db_ref, u_ref, dxb_ref = refs[8 + n_c:15 + n_c]
        if n_c:
            exchange = lambda: _ici_reduce_copies(refs[8], refs[15 + n_c], *refs[15 + 2 * n_c:])

        @pl.when(pl.program_id(0) == 0)
        def _():
            dg_ref[...] = jnp.zeros_like(dg_ref)
            if n_c:
                copies, local = exchange()
                local.start()
                for cp in copies:
                    cp.start()

        dy = (FFN_RESIDUAL * dxo_ref[...]).astype(BF16)
        dy_ref[...] = dy
        for j in range(nj):
            cols = slice(j * tf, (j + 1) * tf)
            du = _dot_nt(dy, w2_ref[cols, :]).astype(BF16)
            av = a_ref[:, cols]
            bv = b_ref[:, cols]
            sg = _sigmoid(av.astype(F32)).astype(BF16)
            sl = av * sg
            da_ref[:, cols] = (du * bv) * (sg + sl * (1.0 - sg))
            db_ref[:, cols] = du * sl
            u_ref[:, cols] = sl * bv
        dh = _dot(da_ref[...], w1_ref[...]) + _dot(db_ref[...], w3_ref[...])
        n, r, _ = _rms(x_ref[...], g_ref[...])
        dxr, dg = _rms_bwd(dh, n, r, g_ref[...])
        dx = dxo_ref[...] + dxr
        dx_ref[...] = dx
        dxb_ref[...] = dx.astype(BF16)
        dg_ref[0:1, :] += dg
        if n_c:
            @pl.when(pl.program_id(0) == ni - 1)
            def _():
                copies, local = exchange()
                _wait_all(copies)
                local.wait()

    row = pl.BlockSpec((tm, d), lambda i: (i, 0))
    hid = pl.BlockSpec((tm, f), lambda i: (i, 0))
    extra = [] if sums is None else [sums]
    sems = [pltpu.SemaphoreType.DMA((3,)), pltpu.SemaphoreType.DMA((3,)), pltpu.SemaphoreType.DMA] if n_c else []
    out = pl.pallas_call(
        body, name="ffn_bwd_reduce" if n_c else "ffn_bwd", grid=(ni,),
        in_specs=[row, row, _resident(g.shape), hid, hid, _resident(w1t.shape), _resident(w3t.shape), _resident(w2.shape)]
        + [ANY] * n_c,
        out_specs=[row, pl.BlockSpec((8, d), lambda i: (0, 0)), row, hid, hid, hid, row] + [ANY] * n_c,
        out_shape=[jax.ShapeDtypeStruct((s, d), F32), jax.ShapeDtypeStruct((8, d), F32), jax.ShapeDtypeStruct((s, d), BF16),
                   jax.ShapeDtypeStruct((s, f), BF16), jax.ShapeDtypeStruct((s, f), BF16), jax.ShapeDtypeStruct((s, f), BF16),
                   jax.ShapeDtypeStruct((s, d), BF16)]
        + [jax.ShapeDtypeStruct(e.shape, e.dtype) for e in extra],
        scratch_shapes=sems,
        compiler_params=_params("arbitrary"),
    )(dxo, x, g, a, b, w1t, w3t, w2, *extra)
    return out


def _dw(am, bm):
    s, r = am.shape
    n = bm.shape[1]
    rb = r // 2 if r > 1024 and (r // 2) % LANES == 0 else r
    ts = _tile(s, 2048)
    ni = s // ts

    def body(am_ref, bm_ref, o_ref, acc_s):
        i = pl.program_id(1)

        @pl.when(i == 0)
        def _():
            acc_s[...] = jnp.zeros_like(acc_s)

        acc_s[...] += lax.dot_general(am_ref[...], bm_ref[...], TN, preferred_element_type=F32)

        @pl.when(i == ni - 1)
        def _():
            o_ref[...] = acc_s[...].astype(BF16)

    return pl.pallas_call(
        body, name="dw", grid=(r // rb, ni),
        in_specs=[pl.BlockSpec((ts, rb), lambda k, i: (i, k)), pl.BlockSpec((ts, n), lambda k, i: (i, 0))],
        out_specs=pl.BlockSpec((rb, n), lambda k, i: (k, 0)),
        out_shape=jax.ShapeDtypeStruct((r, n), BF16),
        scratch_shapes=[pltpu.VMEM((rb, n), F32)],
        compiler_params=_params("arbitrary", "arbitrary"),
    )(am, bm)


def _proj_fwd(x, g, w_int):
    s, d = x.shape
    f = w_int.shape[0]
    tm = _tile(s, 512)

    def body(x_ref, g_ref, w_ref, p_ref):
        p_ref[...] = _dot_nt(_rms(x_ref[...], g_ref[...])[2].astype(BF16), w_ref[...])

    return pl.pallas_call(
        body, name="proj_fwd", grid=(s // tm,),
        in_specs=[pl.BlockSpec((tm, d), lambda i: (i, 0)), _resident(g.shape), _resident(w_int.shape)],
        out_specs=pl.BlockSpec((tm, f), lambda i: (i, 0)),
        out_shape=jax.ShapeDtypeStruct((s, f), F32),
        compiler_params=_params("arbitrary"),
    )(x, g, w_int)


def _proj_bwd(dxo, x, g, dp, w_int):
    s, d = x.shape
    f = w_int.shape[0]
    tm = _tile(s, 512)

    def body(dxo_ref, x_ref, g_ref, dp_ref, w_ref, dx_ref, dg_ref, h_ref):
        @pl.when(pl.program_id(0) == 0)
        def _():
            dg_ref[...] = jnp.zeros_like(dg_ref)

        n, r, h = _rms(x_ref[...], g_ref[...])
        h_ref[...] = h.astype(BF16)
        dxr, dg = _rms_bwd(_dot(dp_ref[...], w_ref[...]), n, r, g_ref[...])
        dx_ref[...] = dxo_ref[...] + dxr
        dg_ref[0:1, :] += dg

    row = pl.BlockSpec((tm, d), lambda i: (i, 0))
    return pl.pallas_call(
        body, name="proj_bwd", grid=(s // tm,),
        in_specs=[row, row, _resident(g.shape), pl.BlockSpec((tm, f), lambda i: (i, 0)), _resident(w_int.shape)],
        out_specs=[row, pl.BlockSpec((8, d), lambda i: (0, 0)), row],
        out_shape=[jax.ShapeDtypeStruct((s, d), F32), jax.ShapeDtypeStruct((8, d), F32), jax.ShapeDtypeStruct((s, d), BF16)],
        compiler_params=_params("arbitrary"),
    )(dxo, x, g, dp, w_int)


C = D_GROUP


def _piece(ref, k):
    return ref[:, k * C:(k + 1) * C]


def _up(v, r):
    return v if r == 0 else pltpu.roll(v, v.shape[0] - r, 0)


def _down(v, r):
    return v if r == 0 else pltpu.roll(v, r, 0)


def _lane_group():
    lane = lax.broadcasted_iota(jnp.int32, (1, C), 1)
    return (lane >= POOL_GROUP).astype(jnp.int32) + (lane >= 2 * POOL_GROUP).astype(jnp.int32) + (
        lane >= 3 * POOL_GROUP).astype(jnp.int32)


def _by_group(grp, v2, v4, v8, v16):
    return jnp.where(grp == 0, v2, jnp.where(grp == 1, v4, jnp.where(grp == 2, v8, v16)))


def _pool_count(grp, row0, t):
    pos = (row0 + lax.broadcasted_iota(jnp.int32, (t, C), 0) + 1).astype(F32)
    return jnp.minimum(pos, _by_group(grp, 2.0, 4.0, 8.0, 16.0))


def _trailing_sums(ext, grp, t):
    s2 = ext + _down(ext, 1)
    s4 = s2 + _down(s2, 2)
    s8 = s4 + _down(s4, 4)
    s16 = s8 + _down(s8, 8)
    return _by_group(grp, s2, s4, s8, s16)[HALO:HALO + t]


def _leading_sums(ext, grp, t):
    s2 = ext + _up(ext, 1)
    s4 = s2 + _up(s2, 2)
    s8 = s4 + _up(s4, 4)
    s16 = s8 + _up(s8, 8)
    return _by_group(grp, s2, s4, s8, s16)[0:t]


def _head_select(r4, grp):
    assert HEAD_DIM == POOL_GROUP and N_HEADS == 4
    return _by_group(grp, *(r4[h * CHUNK:(h + 1) * CHUNK] for h in range(N_HEADS)))


def _conv_taps():
    return [(k, (k + 2) % 8, (k + 2) - (k + 2) % 8) for k in range(CONF_KERNEL)]


def _mixer_fwd(p, x1, cw, vec, pool_w, wstack, bias, w_out):
    s, d = x1.shape
    t = _tile(s, T_MIX_FWD)
    n_ext = t + HALO
    dm = w_out.shape[0]

    def body(p_ref, x1_ref, cw_ref, vec_ref, pw_ref, ws_ref, bias_ref, wo_ref, x2_ref, mix_s, z_ref, cy_s, cq_s, cx_s):
        i = pl.program_id(0)

        @pl.when(i == 0)
        def _():
            cy_s[...] = jnp.zeros_like(cy_s)
            cq_s[...] = jnp.zeros_like(cq_s)
            cx_s[...] = jnp.zeros_like(cx_s)

        grp = _lane_group()
        y = _piece(p_ref, 0) * _sigmoid(_piece(p_ref, 1))
        ext = jnp.concatenate([cy_s[...], y], axis=0)
        cy_s[...] = y[t - HALO:t]
        z = jnp.broadcast_to(vec_ref[0:1, :], (t, C))
        shifted = {}
        for k, r, off in _conv_taps():
            if r not in shifted:
                shifted[r] = _up(ext, r)
            z = z + cw_ref[k:k + 1, :] * shifted[r][off:off + t]
        z_ref[...] = z
        ln = _ln_fwd(z, vec_ref[1:2, :], vec_ref[2:3, :])[2]
        mix_s[:, 0:C] = (ln * _sigmoid(ln)).astype(BF16)
        q = _piece(p_ref, 3) * _piece(p_ref, 4)
        ext = jnp.concatenate([cq_s[...], q], axis=0)
        cq_s[...] = q[t - HALO:t]
        cz = vec_ref[8:9, :] * q + vec_ref[7:8, :] * _down(ext, 1)[HALO:] + vec_ref[6:7, :] * _down(ext, 2)[HALO:]
        mix_s[:, C:2 * C] = (_piece(p_ref, 2) * cz).astype(BF16)
        xp = _piece(p_ref, 5)
        ext = jnp.concatenate([cx_s[...], xp], axis=0)
        cx_s[...] = xp[t - HALO:t]
        dd = _trailing_sums(ext, grp, t) / _pool_count(grp, i * t, t) - xp
        mix_s[:, 2 * C:3 * C] = (_dot(dd.astype(BF16), pw_ref[...]) * vec_ref[3:4, :]).astype(BF16)
        vln = _ln_fwd(_piece(p_ref, 7), vec_ref[4:5, :], vec_ref[5:6, :])[2].astype(BF16)
        for n in range(t // CHUNK):
            rows = slice(n * CHUNK, (n + 1) * CHUNK)
            mixed = _head_select(_dot(ws_ref[...], vln[rows]), grp) + bias_ref[...]
            mix_s[rows, 3 * C:4 * C] = (p_ref[rows, 6 * C:7 * C] * mixed).astype(BF16)
        x2_ref[...] = x1_ref[...] + _dot(mix_s[...], wo_ref[...])

    full = lambda a: pl.BlockSpec(a.shape, lambda i: (0, 0))
    return pl.pallas_call(
        body, name="mixer_fwd", grid=(s // t,),
        in_specs=[pl.BlockSpec((t, p.shape[1]), lambda i: (i, 0)), pl.BlockSpec((t, d), lambda i: (i, 0)),
                  full(cw), full(vec), full(pool_w), full(wstack), full(bias), full(w_out)],
        out_specs=[pl.BlockSpec((t, d), lambda i: (i, 0)), pl.BlockSpec((t, dm), lambda i: (i, 0)),
                   pl.BlockSpec((t, C), lambda i: (i, 0))],
        out_shape=[jax.ShapeDtypeStruct((s, d), F32), jax.ShapeDtypeStruct((s, dm), BF16), jax.ShapeDtypeStruct((s, C), F32)],
        scratch_shapes=[pltpu.VMEM((HALO, C), F32)] * 3,
        compiler_params=_params("arbitrary"),
    )(p, x1, cw, vec, pool_w, wstack, bias, w_out)


def _mixer_bwd(dx2, p, z, cw, vec, pool_w, wstack, wstack_t, bias, tril4, head_rows, w_out):
    s, d = dx2.shape
    t = _tile(s, T_MIX_BWD)
    nt = s // t
    n_ext = t + HALO
    hb = t // HALO

    def body(dx2_ref, p_ref, ph_ref, z_ref, cw_ref, vec_ref, pw_ref, ws_ref, wst_ref, bias_ref, tril_ref, hr_ref, wo_ref,
             dp_ref, dcw_ref, dvec_ref, dpool_ref, dws_ref, dbs_ref, cdz_s, cdc_s, cf_s, vy_s, dvl_s, dbias_s):
        i = pl.program_id(0)
        tile = nt - 1 - i

        @pl.when(i == 0)
        def _():
            for ref in (cdz_s, cdc_s, cf_s, dbias_s, dcw_ref, dvec_ref, dpool_ref, dws_ref, dbs_ref):
                ref[...] = jnp.zeros_like(ref)

        grp = _lane_group()
        first = jnp.where(tile > 0, 1.0, 0.0)
        dmix = _dot_nt(dx2_ref[...].astype(BF16), wo_ref[...])
        d_a, d_b, d_c, d_d = (dmix[:, k * C:(k + 1) * C] for k in range(4))

        def acc_vec(row, v):
            dvec_ref[row:row + 1, :] += jnp.sum(v, axis=0, keepdims=True)

        val, gate = _piece(p_ref, 0), _piece(p_ref, 1)
        sgate = _sigmoid(gate)
        y = val * sgate
        y_halo = ph_ref[:, 0:C] * _sigmoid(ph_ref[:, C:2 * C]) * first
        ext = jnp.concatenate([y_halo, y], axis=0)
        for r in range(8):
            vy_s[r] = _up(ext, r)
        zn, rs, ln = _ln_fwd(z_ref[...], vec_ref[1:2, :], vec_ref[2:3, :])
        sg = _sigmoid(ln)
        dln = d_a * (sg * (1.0 + ln * (1.0 - sg)))
        dz, dg, db = _ln_bwd(dln, zn, rs, vec_ref[1:2, :])
        dvec_ref[1:2, :] += dg
        dvec_ref[2:3, :] += db
        acc_vec(0, dz)
        for k, r, off in _conv_taps():
            dcw_ref[k:k + 1, :] += jnp.sum(dz * vy_s[r, off:off + t, :], axis=0, keepdims=True)
        ext = jnp.concatenate([dz, cdz_s[...]], axis=0)
        cdz_s[...] = dz[0:HALO]
        dy = jnp.zeros((t, C), F32)
        shifted = {}
        for k in range(CONF_KERNEL):
            m = CONF_KERNEL - 1 - k
            r, off = m % 8, m - m % 8
            if r not in shifted:
                shifted[r] = _up(ext, r)
            dy = dy + cw_ref[k:k + 1, :] * shifted[r][off:off + t]
        dp_ref[:, 0:C] = (dy * sgate).astype(BF16)
        dp_ref[:, C:2 * C] = (dy * val * sgate * (1.0 - sgate)).astype(BF16)

        sb, sc, sx = _piece(p_ref, 2), _piece(p_ref, 3), _piece(p_ref, 4)
        q = sc * sx
        q_halo = ph_ref[:, 3 * C:4 * C] * ph_ref[:, 4 * C:5 * C] * first
        ext = jnp.concatenate([q_halo, q], axis=0)
        q1, q2 = _down(ext, 1)[HALO:], _down(ext, 2)[HALO:]
        cz = vec_ref[8:9, :] * q + vec_ref[7:8, :] * q1 + vec_ref[6:7, :] * q2
        dcz = d_b * sb
        dp_ref[:, 2 * C:3 * C] = (d_b * cz).astype(BF16)
        acc_vec(8, dcz * q)
        acc_vec(7, dcz * q1)
        acc_vec(6, dcz * q2)
        ext = jnp.concatenate([dcz, cdc_s[...]], axis=0)
        cdc_s[...] = dcz[0:HALO]
        dq = vec_ref[8:9, :] * dcz + vec_ref[7:8, :] * _up(ext, 1)[0:t] + vec_ref[6:7, :] * _up(ext, 2)[0:t]
        dp_ref[:, 3 * C:4 * C] = (dq * sx).astype(BF16)
        dp_ref[:, 4 * C:5 * C] = (dq * sc).astype(BF16)

        xp = _piece(p_ref, 5)
        ext = jnp.concatenate([ph_ref[:, 5 * C:6 * C] * first, xp], axis=0)
        cnt = _pool_count(grp, tile * t, t)
        dd = (_trailing_sums(ext, grp, t) / cnt - xp).astype(BF16)
        e2 = _dot(dd, pw_ref[...])
        acc_vec(3, d_c * e2)
        de = (d_c * vec_ref[3:4, :]).astype(BF16)
        dpool_ref[...] += _dot(_t_bf16(dd.astype(F32)), de)
        ddd = _dot_nt(de, pw_ref[...])
        fq = ddd / cnt
        ext = jnp.concatenate([fq, cf_s[...]], axis=0)
        cf_s[...] = fq[0:HALO]
        dp_ref[:, 5 * C:6 * C] = (_leading_sums(ext, grp, t) - ddd).astype(BF16)

        vn, vrs, vlnf = _ln_fwd(_piece(p_ref, 7), vec_ref[4:5, :], vec_ref[5:6, :])
        vln = vlnf.astype(BF16)
        for n in range(t // CHUNK):
            rows = slice(n * CHUNK, (n + 1) * CHUNK)
            mixed = _head_select(_dot(ws_ref[...], vln[rows]), grp) + bias_ref[...]
            dd_n = d_d[rows]
            dp_ref[rows, 6 * C:7 * C] = (dd_n * mixed).astype(BF16)
            dmx = dd_n * p_ref[rows, 6 * C:7 * C]
            dbias_s[...] += dmx
            dmx_b = dmx.astype(BF16)
            dvl_s[rows, :] = _head_select(_dot(wst_ref[...], dmx_b), grp)
            for h in range(N_HEADS):
                hrows = slice(h * CHUNK, (h + 1) * CHUNK)
                dws_ref[hrows, :] += _dot_nt(jnp.where(grp == h, dmx_b, jnp.zeros_like(dmx_b)), vln[rows])
        dvl = dvl_s[...]
        dv, dg, db = _ln_bwd(dvl, vn, vrs, vec_ref[4:5, :])
        dvec_ref[4:5, :] += dg
        dvec_ref[5:6, :] += db
        dp_ref[:, 7 * C:8 * C] = dv.astype(BF16)

        @pl.when(i == nt - 1)
        def _():
            dws_ref[...] = dws_ref[...] * tril_ref[...]
            dbs_ref[...] = lax.dot_general(hr_ref[...], dbias_s[...], NT, precision=lax.Precision.HIGHEST,
                                           preferred_element_type=F32)

    full = lambda a: pl.BlockSpec(a.shape, lambda i: (0, 0))
    acc = lambda shape: pl.BlockSpec(shape, lambda i: (0, 0))
    f = p.shape[1]
    return pl.pallas_call(
        body, name="mixer_bwd", grid=(nt,),
        in_specs=[pl.BlockSpec((t, d), lambda i: (nt - 1 - i, 0)), pl.BlockSpec((t, f), lambda i: (nt - 1 - i, 0)),
                  pl.BlockSpec((HALO, f), lambda i: (jnp.maximum((nt - 1 - i) * hb - 1, 0), 0)),
                  pl.BlockSpec((t, C), lambda i: (nt - 1 - i, 0)), full(cw), full(vec), full(pool_w), full(wstack), full(wstack_t), full(bias), full(tril4), full(head_rows),
                  full(w_out)],
        out_specs=[pl.BlockSpec((t, f), lambda i: (nt - 1 - i, 0)), acc((32, C)), acc((16, C)), acc((C, C)),
                   acc((N_HEADS * CHUNK, CHUNK)), acc((8, CHUNK))],
        out_shape=[jax.ShapeDtypeStruct((s, f), BF16), jax.ShapeDtypeStruct((32, C), F32), jax.ShapeDtypeStruct((16, C), F32),
                   jax.ShapeDtypeStruct((C, C), F32), jax.ShapeDtypeStruct((N_HEADS * CHUNK, CHUNK), F32),
                   jax.ShapeDtypeStruct((8, CHUNK), F32)],
        scratch_shapes=[pltpu.VMEM((HALO, C), F32)] * 3 + [pltpu.VMEM((8, n_ext, C), F32), pltpu.VMEM((t, C), F32),
                                                            pltpu.VMEM((CHUNK, C), F32)],
        compiler_params=_params("arbitrary"),
    )(dx2, p, p, z, cw, vec, pool_w, wstack, wstack_t, bias, tril4, head_rows, w_out)


def _loss_bwd(x, g, target):
    s, d = x.shape
    tm = _tile(s, 512)

    def body(x_ref, g_ref, t_ref, dx_ref, dg_ref, loss_ref):
        @pl.when(pl.program_id(0) == 0)
        def _():
            dg_ref[...] = jnp.zeros_like(dg_ref)
            loss_ref[...] = jnp.zeros_like(loss_ref)

        n, r, y = _rms(x_ref[...], g_ref[...])
        err = y - t_ref[...]
        loss_ref[...] += 0.5 * jnp.sum(jnp.mean(err * err, axis=-1, keepdims=True), axis=0, keepdims=True)
        dxr, dg = _rms_bwd(err * (1.0 / d), n, r, g_ref[...])
        dx_ref[...] = dxr
        dg_ref[0:1, :] += dg

    row = pl.BlockSpec((tm, d), lambda i: (i, 0))
    return pl.pallas_call(
        body, name="loss_bwd", grid=(s // tm,),
        in_specs=[row, pl.BlockSpec((1, d), lambda i: (0, 0)), row],
        out_specs=[row, pl.BlockSpec((8, d), lambda i: (0, 0)), pl.BlockSpec((8, LANES), lambda i: (0, 0))],
        out_shape=[jax.ShapeDtypeStruct((s, d), F32), jax.ShapeDtypeStruct((8, d), F32), jax.ShapeDtypeStruct((8, LANES), F32)],
        compiler_params=_params("arbitrary"),
    )(x, g, target)


def _adamw_math(w, g, m, v):
    m = ADAM_B1 * m + (1.0 - ADAM_B1) * g
    v = ADAM_B2 * v + (1.0 - ADAM_B2) * (g * g)
    m_hat = m / (1.0 - ADAM_B1 ** ADAM_STEP)
    v_hat = v / (1.0 - ADAM_B2 ** ADAM_STEP)
    return -ADAM_LR * (m_hat / (jnp.sqrt(v_hat) + ADAM_EPS) + ADAM_WD * w), m, v


def _adamw(w, g, m, v):
    r, c = w.shape
    tr = r // 8 if r % 64 == 0 else r

    def body(w_ref, g_ref, m_ref, v_ref, d_ref, mo_ref, vo_ref, go_ref):
        gv = g_ref[...]
        d_ref[...], mo_ref[...], vo_ref[...] = _adamw_math(w_ref[...], gv, m_ref[...], v_ref[...])
        go_ref[...] = gv

    blk = pl.BlockSpec((tr, c), lambda i: (i, 0))
    return pl.pallas_call(
        body, name="adamw", grid=(r // tr,), in_specs=[blk] * 4, out_specs=[blk] * 4,
        out_shape=[jax.ShapeDtypeStruct((r, c), F32)] * 4, compiler_params=_params("arbitrary"),
    )(w, g, m, v)


def _adamw_small(ws, gs, ms, vs):
    n = len(ws)

    def body(*refs):
        ins, outs = refs[:4 * n], refs[4 * n:]
        for k in range(n):
            dl, mo, vo = _adamw_math(ins[k][...], ins[n + k][...], ins[2 * n + k][...], ins[3 * n + k][...])
            outs[k][...], outs[n + k][...], outs[2 * n + k][...] = dl, mo, vo

    vm = pl.BlockSpec(memory_space=pltpu.VMEM)
    out = pl.pallas_call(
        body, name="adamw_small", in_specs=[vm] * (4 * n), out_specs=[vm] * (3 * n),
        out_shape=[jax.ShapeDtypeStruct(a.shape, F32) for a in ws] * 3,
        compiler_params=pltpu.CompilerParams(vmem_limit_bytes=VMEM_LIMIT),
    )(*ws, *gs, *ms, *vs)
    return out[:n], out[n:2 * n], out[2 * n:]


def _where_am_i():
    x, y, c = lax.axis_index("x"), lax.axis_index("y"), lax.axis_index("c")
    chips = [(1 - x, y), (x, 1 - y), (1 - x, 1 - y)]
    return x, y, c, chips


def _chip_id(chip):
    return 2 * chip[0] + chip[1]


def _landing_shapes(shards):
    return [jax.ShapeDtypeStruct((3, a.shape[0] // 2, a.shape[1]), a.dtype) for a in shards]


def _ici_gather_copies(ins, lands, send_sems, recv_sems):
    _, _, c, chips = _where_am_i()
    copies = []
    for a, src in enumerate(ins):
        hr = src.shape[0] // 2
        for j, chip in enumerate(chips):
            copies.append(pltpu.make_async_remote_copy(
                src_ref=src.at[pl.ds(c * hr, hr)], dst_ref=lands[a].at[j], send_sem=send_sems.at[3 * a + j],
                recv_sem=recv_sems.at[3 * a + j], device_id=(*chip, c), device_id_type=MESH))
    return copies


def _ici_reduce_copies(s_ref, o_ref, send_sems, recv_sems, local_sem):
    x, y, c, chips = _where_am_i()
    copies = [pltpu.make_async_remote_copy(src_ref=s_ref.at[_chip_id(chip)], dst_ref=o_ref.at[j], send_sem=send_sems.at[j],
                                           recv_sem=recv_sems.at[j], device_id=(*chip, c), device_id_type=MESH)
              for j, chip in enumerate(chips)]
    return copies, pltpu.make_async_copy(s_ref.at[_chip_id((x, y))], o_ref.at[3], local_sem)


def _wait_all(copies):
    for cp in copies:
        cp.wait_recv()
    for cp in copies:
        cp.wait_send()


def _ici_gather(shards, small):
    n = len(shards)

    def body(*refs):
        begin_small, finish_small = _small_gather_steps(refs[n], refs[2 * n + 1], *refs[2 * n + 4:])
        copies = _ici_gather_copies(refs[:n], refs[n + 1:2 * n + 1], refs[2 * n + 2], refs[2 * n + 3])
        begin_small()
        for cp in copies:
            cp.start()
        finish_small()
        _wait_all(copies)

    vm = pl.BlockSpec(memory_space=pltpu.VMEM)
    out = pl.pallas_call(
        body, name="ici_gather", in_specs=[ANY] * n + [vm], out_specs=[ANY] * n + [vm],
        out_shape=_landing_shapes(shards) + [jax.ShapeDtypeStruct((N_DEV * small.shape[0], small.shape[1]), F32)],
        scratch_shapes=[pltpu.SemaphoreType.DMA((3 * n,))] * 2 + _small_gather_sems(),
    )(*shards, small)
    return list(out[:n]), out[n]


SLOTS = 4


def _sibling_gather(shards, lands):
    n = len(shards)
    d = shards[0].shape[1]
    halves = [a.shape[0] // 2 for a in shards]
    hmax = max(halves)
    chunks = [(a, j, halves[a]) for a in range(n) for j in range(3)]
    nc = len(chunks)

    def body(*refs):
        ins, lnd, outs = refs[:n], refs[n:2 * n], refs[2 * n:3 * n]
        sbuf, rbuf, obuf, ld_sems, take_sems, send_sems, recv_sems, place_sems, own_ld_sems, own_st_sems, credits = refs[3 * n:]
        x, y, c, chips = _where_am_i()
        sibling = (x, y, 1 - c)
        me = _chip_id((x, y))

        def load(i):
            a, j, hr = chunks[i]
            return pltpu.make_async_copy(lnd[a].at[j], sbuf.at[i % SLOTS, pl.ds(0, hr)], ld_sems.at[i % SLOTS])

        def push(i):
            hr, slot = chunks[i][2], i % SLOTS
            return pltpu.make_async_remote_copy(src_ref=sbuf.at[slot, pl.ds(0, hr)], dst_ref=rbuf.at[slot, pl.ds(0, hr)],
                                                send_sem=send_sems.at[slot], recv_sem=recv_sems.at[slot], device_id=sibling,
                                                device_id_type=MESH)

        def take(i):
            a, j, hr = chunks[i]
            return pltpu.make_async_copy(rbuf.at[i % SLOTS, pl.ds(0, hr)],
                                         outs[a].at[_chip_id(chips[j]), pl.ds((1 - c) * hr, hr)], take_sems.at[i % SLOTS])

        def place(i):
            a, j, hr = chunks[i]
            return pltpu.make_async_copy(sbuf.at[i % SLOTS, pl.ds(0, hr)], outs[a].at[_chip_id(chips[j]), pl.ds(c * hr, hr)],
                                         place_sems.at[i % SLOTS])

        own = [(a, h, halves[a]) for a in range(n) for h in range(2)]

        def own_load(k):
            a, h, hr = own[k]
            return pltpu.make_async_copy(ins[a].at[pl.ds(h * hr, hr)], obuf.at[k % 2, pl.ds(0, hr)], own_ld_sems.at[k % 2])

        def own_store(k):
            a, h, hr = own[k]
            return pltpu.make_async_copy(obuf.at[k % 2, pl.ds(0, hr)], outs[a].at[me, pl.ds(h * hr, hr)], own_st_sems.at[k % 2])

        def own_step(k):
            if k < len(own):
                if k >= 2:
                    own_store(k - 2).wait()
                own_load(k).start()
            if 1 <= k <= len(own):
                own_load(k - 1).wait()
                own_store(k - 1).start()

        for i in range(min(2, nc)):
            load(i).start()
        for i in range(nc):
            own_step(i)
            if i >= 2:
                push(i - 2).wait_send()
                place(i - 2).wait()
            if i + 2 < nc:
                load(i + 2).start()
            load(i).wait()
            place(i).start()
            if i >= SLOTS:
                pl.semaphore_wait(credits.at[i % SLOTS], 1)
            push(i).start()
            if i >= 1:
                push(i - 1).wait_recv()
                take(i - 1).start()
            if i >= 2:
                take(i - 2).wait()
                if i - 2 + SLOTS < nc:
                    pl.semaphore_signal(credits.at[(i - 2) % SLOTS], inc=1, device_id=sibling, device_id_type=MESH)
        for i in range(max(0, nc - 2), nc):
            push(i).wait_send()
            place(i).wait()
        push(nc - 1).wait_recv()
        take(nc - 1).start()
        for i in range(max(0, nc - 2), nc):
            take(i).wait()
        for k in range(nc, len(own) + 1):
            own_step(k)
        for k in range(max(0, len(own) - 2), len(own)):
            own_store(k).wait()

    dma = pltpu.SemaphoreType.DMA((SLOTS,))
    dma2 = pltpu.SemaphoreType.DMA((2,))
    return pl.pallas_call(
        body, name="sibling_gather", in_specs=[ANY] * (2 * n), out_specs=[ANY] * n,
        out_shape=[jax.ShapeDtypeStruct((N_CHIPS,) + a.shape, a.dtype) for a in shards],
        scratch_shapes=[pltpu.VMEM((SLOTS, hmax, d), BF16), pltpu.VMEM((SLOTS, hmax, d), BF16), pltpu.VMEM((2, hmax, d), BF16),
                        dma, dma, dma, dma, dma, dma2, dma2, pltpu.SemaphoreType.REGULAR((SLOTS,))],
        compiler_params=pltpu.CompilerParams(vmem_limit_bytes=VMEM_LIMIT),
    )(*shards, *lands)


def _pair_sum(grads):
    n = len(grads)
    halves = [g.shape[1] // 2 for g in grads]
    total, rows = sum(halves), max(halves)
    d = grads[0].shape[2]
    chunks, off = [], 0
    for a in range(n):
        chunks += [(a, k, off, halves[a]) for k in range(N_CHIPS)]
        off += halves[a]
    nc = len(chunks)

    def body(*refs):
        ins, out_ref = refs[:n], refs[n]
        sbuf, rbuf, mbuf, obuf, ls_sems, lm_sems, st_sems, send_sems, recv_sems, credits = refs[n + 1:]
        x, y, c, _ = _where_am_i()
        sibling = (x, y, 1 - c)

        def load_theirs(i):
            a, k, _, hr = chunks[i]
            return pltpu.make_async_copy(ins[a].at[k, pl.ds((1 - c) * hr, hr)], sbuf.at[i % SLOTS, pl.ds(0, hr)], ls_sems.at[i % SLOTS])

        def load_mine(i):
            a, k, _, hr = chunks[i]
            return pltpu.make_async_copy(ins[a].at[k, pl.ds(c * hr, hr)], mbuf.at[i % SLOTS, pl.ds(0, hr)], lm_sems.at[i % SLOTS])

        def push(i):
            hr, slot = chunks[i][3], i % SLOTS
            return pltpu.make_async_remote_copy(src_ref=sbuf.at[slot, pl.ds(0, hr)], dst_ref=rbuf.at[slot, pl.ds(0, hr)],
                                                send_sem=send_sems.at[slot], recv_sem=recv_sems.at[slot], device_id=sibling,
                                                device_id_type=MESH)

        def store(i):
            _, k, o, hr = chunks[i]
            slot = i % SLOTS
            return pltpu.make_async_copy(obuf.at[slot, pl.ds(0, hr)], out_ref.at[k, pl.ds(o, hr)], st_sems.at[slot])

        def start_push(i):
            load_theirs(i).wait()
            if i >= SLOTS:
                pl.semaphore_wait(credits.at[i % SLOTS], 1)
            push(i).start()

        for i in range(min(2, nc)):
            load_theirs(i).start()
            load_mine(i).start()
        start_push(0)
        for i in range(nc):
            hr, slot = chunks[i][3], i % SLOTS
            if i + 2 < nc:
                load_theirs(i + 2).start()
                load_mine(i + 2).start()
            if i + 1 < nc:
                start_push(i + 1)
            push(i).wait_recv()
            push(i).wait_send()
            load_mine(i).wait()
            if i >= SLOTS:
                store(i - SLOTS).wait()
            obuf[slot, 0:hr, :] = (mbuf[slot, 0:hr, :].astype(F32) + rbuf[slot, 0:hr, :].astype(F32)).astype(BF16)
            if i + SLOTS < nc:
                pl.semaphore_signal(credits.at[slot], inc=1, device_id=sibling, device_id_type=MESH)
            store(i).start()
        for i in range(max(0, nc - SLOTS), nc):
            store(i).wait()

    stage = pltpu.VMEM((SLOTS, rows, d), BF16)
    dma = pltpu.SemaphoreType.DMA((SLOTS,))
    return pl.pallas_call(
        body, name="pair_sum", in_specs=[ANY] * n, out_specs=ANY, out_shape=jax.ShapeDtypeStruct((N_CHIPS, total, d), BF16),
        scratch_shapes=[stage, stage, stage, stage, dma, dma, dma, dma, dma, pltpu.SemaphoreType.REGULAR((SLOTS,))],
        compiler_params=pltpu.CompilerParams(vmem_limit_bytes=VMEM_LIMIT),
    )(*grads)


def _reduce_chips(sums):
    def body(s_ref, o_ref, send_sems, recv_sems, local_sem):
        copies, local = _ici_reduce_copies(s_ref, o_ref, send_sems, recv_sems, local_sem)
        local.start()
        for cp in copies:
            cp.start()
        _wait_all(copies)
        local.wait()

    return pl.pallas_call(
        body, name="reduce_chips", in_specs=[ANY], out_specs=ANY, out_shape=jax.ShapeDtypeStruct(sums.shape, BF16),
        scratch_shapes=[pltpu.SemaphoreType.DMA((3,)), pltpu.SemaphoreType.DMA((3,)), pltpu.SemaphoreType.DMA],
    )(sums)


def _sum_share(parts, groups, rows, small):
    layers, n, n_g = len(parts), len(rows), len(groups)
    parts = [part for layer in parts for part in layer]
    n_l = len(parts)
    d = parts[0].shape[2]
    halves = [r // 2 for r in rows]
    hmax = max(halves)
    chunks = []
    for l in range(layers):
        for g, members in enumerate(groups):
            off = 0
            for a in members:
                chunks.append((l * n_g + g, a, off, halves[a], l))
                off += halves[a]
    nc = len(chunks)

    def body(*refs):
        ins, small_ref, outs, total_ref = refs[:n_l], refs[n_l], refs[n_l + 1:n_l + 1 + n], refs[n_l + 1 + n]
        pbuf, obuf, rbuf, ld_sems, keep_sems, take_sems, send_sems, recv_sems, credits, all_ref = refs[n_l + 2 + n:n_l + 12 + n]
        begin_small, finish_small = _small_gather_steps(small_ref, all_ref, *refs[n_l + 12 + n:])
        begin_small()
        x, y, c, _ = _where_am_i()
        sibling = (x, y, 1 - c)

        def load(i):
            part, _, off, hr, _ = chunks[i]
            return pltpu.make_async_copy(ins[part].at[:, pl.ds(off, hr)], pbuf.at[i % SLOTS, :, pl.ds(0, hr)], ld_sems.at[i % SLOTS])

        def keep(i):
            _, a, _, hr, l = chunks[i]
            return pltpu.make_async_copy(obuf.at[i % SLOTS, pl.ds(0, hr)], outs[a].at[l, pl.ds(c * hr, hr)], keep_sems.at[i % SLOTS])

        def push(i):
            hr, slot = chunks[i][3], i % SLOTS
            return pltpu.make_async_remote_copy(src_ref=obuf.at[slot, pl.ds(0, hr)], dst_ref=rbuf.at[slot, pl.ds(0, hr)],
                                                send_sem=send_sems.at[slot], recv_sem=recv_sems.at[slot], device_id=sibling,
                                                device_id_type=MESH)

        def take(i):
            _, a, _, hr, l = chunks[i]
            return pltpu.make_async_copy(rbuf.at[i % SLOTS, pl.ds(0, hr)], outs[a].at[l, pl.ds((1 - c) * hr, hr)],
                                         take_sems.at[i % SLOTS])

        for i in range(min(2, nc)):
            load(i).start()
        for i in range(nc):
            hr, slot = chunks[i][3], i % SLOTS
            if i + 2 < nc:
                load(i + 2).start()
            load(i).wait()
            if i >= SLOTS:
                keep(i - SLOTS).wait()
                push(i - SLOTS).wait_send()
            part = lambda k: pbuf[slot, k, 0:hr, :].astype(F32)
            obuf[slot, 0:hr, :] = ((part(3) + part(0)) + part(1)) + part(2)
            keep(i).start()
            if i >= SLOTS:
                pl.semaphore_wait(credits.at[slot], 1)
            push(i).start()
            if i >= 1:
                push(i - 1).wait_recv()
                take(i - 1).start()
            if i >= 2:
                take(i - 2).wait()
                if i - 2 + SLOTS < nc:
                    pl.semaphore_signal(credits.at[(i - 2) % SLOTS], inc=1, device_id=sibling, device_id_type=MESH)
        push(nc - 1).wait_recv()
        take(nc - 1).start()
        for i in range(max(0, nc - 2), nc):
            take(i).wait()
        for i in range(max(0, nc - SLOTS), nc):
            keep(i).wait()
            push(i).wait_send()
        finish_small()
        total_ref[...] = _sum_blocks(all_ref, small.shape[0])

    dma = pltpu.SemaphoreType.DMA((SLOTS,))
    vm = pl.BlockSpec(memory_space=pltpu.VMEM)
    out = pl.pallas_call(
        body, name="sum_share", in_specs=[ANY] * n_l + [vm], out_specs=[ANY] * n + [vm],
        out_shape=[jax.ShapeDtypeStruct((layers, r, d), F32) for r in rows] + [jax.ShapeDtypeStruct(small.shape, F32)],
        scratch_shapes=[pltpu.VMEM((SLOTS, N_CHIPS, hmax, d), BF16), pltpu.VMEM((SLOTS, hmax, d), F32),
                        pltpu.VMEM((SLOTS, hmax, d), F32), dma, dma, dma, dma, dma, pltpu.SemaphoreType.REGULAR((SLOTS,)),
                        pltpu.VMEM((N_DEV * small.shape[0], small.shape[1]), F32)] + _small_gather_sems(),
        compiler_params=pltpu.CompilerParams(vmem_limit_bytes=VMEM_LIMIT),
    )(*parts, small)
    return out[:n], out[n]


def _small_gather_sems():
    return [pltpu.SemaphoreType.DMA((7,)), pltpu.SemaphoreType.DMA((7,)), pltpu.SemaphoreType.DMA]


def _sum_blocks(all_ref, m):
    total = all_ref[0:m, :]
    for dev in range(1, N_DEV):
        total = total + all_ref[dev * m:(dev + 1) * m, :]
    return total


def _small_gather_steps(x_ref, all_ref, send_sems, recv_sems, local_sem):
    m = x_ref.shape[0]
    x, y, c, chips = _where_am_i()
    me, sibling = (x, y, c), (x, y, 1 - c)

    def rows(px, py, pc):
        return all_ref.at[pl.ds((4 * px + 2 * py + pc) * m, m), :]

    def copy(k, blk, to, src=None):
        return pltpu.make_async_remote_copy(src_ref=rows(*blk) if src is None else src, dst_ref=rows(*blk),
                                            send_sem=send_sems.at[k], recv_sem=recv_sems.at[k], device_id=to,
                                            device_id_type=MESH)

    def own():
        return pltpu.make_async_copy(x_ref, rows(*me), local_sem)

    def first():
        return [copy(0, me, sibling, src=x_ref)] + [copy(1 + j, me, (*chip, c), src=x_ref) for j, chip in enumerate(chips)]

    def passed():
        return [copy(4 + j, (*chip, c), sibling) for j, chip in enumerate(chips)]

    def begin():
        own().start()
        for cp in first():
            cp.start()

    def finish():
        forwards = passed()
        for j, chip in enumerate(chips):
            copy(1 + j, (*chip, c), me).wait_recv()
            forwards[j].start()
        copy(0, sibling, me).wait_recv()
        for j, chip in enumerate(chips):
            copy(4 + j, (*chip, 1 - c), me).wait_recv()
        for cp in first() + forwards:
            cp.wait_send()
        own().wait()

    return begin, finish


def _pack(arrays):
    flat = jnp.concatenate([a.reshape(-1) for a in arrays])
    pad = (-flat.shape[0]) % (8 * LANES)
    return jnp.pad(flat, (0, pad)).reshape(-1, LANES)


def _unpack(buf, shapes):
    flat = buf.reshape(-1)
    out, off = [], 0
    for shp in shapes:
        size = 1
        for dim in shp:
            size *= dim
        out.append(flat[off:off + size].reshape(shp))
        off += size
    return out


BIG = ("ffn1_w1", "ffn1_w3", "ffn1_w2", "w_in", "w_out", "ffn2_w1", "ffn2_w3", "ffn2_w2")
TRANSPOSED = ("ffn1_w1", "ffn1_w3", "w_in", "ffn2_w1", "ffn2_w3")
SMALL = ("ffn1_norm", "mix_norm", "conf_conv_w", "conf_conv_b", "conf_ln_g", "conf_ln_b", "sconv_w", "pool_w", "pool_scale",
         "gmlp_ln_g", "gmlp_ln_b", "gmlp_w_s", "gmlp_b_s", "ffn2_norm", "final_norm")
ORDER = ("ffn1_norm", "ffn1_w1", "ffn1_w3", "ffn1_w2", "mix_norm", "w_in", "conf_conv_w", "conf_conv_b", "conf_ln_g", "conf_ln_b",
         "sconv_w", "pool_w", "pool_scale", "gmlp_ln_g", "gmlp_ln_b", "gmlp_w_s", "gmlp_b_s", "w_out", "ffn2_norm", "ffn2_w1",
         "ffn2_w3", "ffn2_w2", "final_norm")


def _as2d(a):
    return a.reshape(-1, a.shape[-1])


def kernel(x, ffn1_norm, ffn1_w1, ffn1_w3, ffn1_w2, mix_norm, w_in, conf_conv_w, conf_conv_b, conf_ln_g, conf_ln_b, sconv_w, pool_w, pool_scale, gmlp_ln_g, gmlp_ln_b, gmlp_w_s, gmlp_b_s, w_out, ffn2_norm, ffn2_w1, ffn2_w3, ffn2_w2, final_norm, loss_target, m_ffn1_norm, m_ffn1_w1, m_ffn1_w3, m_ffn1_w2, m_mix_norm, m_w_in, m_conf_conv_w, m_conf_conv_b, m_conf_ln_g, m_conf_ln_b, m_sconv_w, m_pool_w, m_pool_scale, m_gmlp_ln_g, m_gmlp_ln_b, m_gmlp_w_s, m_gmlp_b_s, m_w_out, m_ffn2_norm, m_ffn2_w1, m_ffn2_w3, m_ffn2_w2, m_final_norm, v_ffn1_norm, v_ffn1_w1, v_ffn1_w3, v_ffn1_w2, v_mix_norm, v_w_in, v_conf_conv_w, v_conf_conv_b, v_conf_ln_g, v_conf_ln_b, v_sconv_w, v_pool_w, v_pool_scale, v_gmlp_ln_g, v_gmlp_ln_b, v_gmlp_w_s, v_gmlp_b_s, v_w_out, v_ffn2_norm, v_ffn2_w1, v_ffn2_w3, v_ffn2_w2, v_final_norm):
    given = dict(locals())
    w = {k: given[k] for k in ORDER}
    mom = {k: given["m_" + k] for k in ORDER}
    var = {k: given["v_" + k] for k in ORDER}
    n_l = ffn1_w1.shape[0]
    xs = x[0]
    d = xs.shape[1]
    chip = 2 * lax.axis_index("x") + lax.axis_index("y")

    def shard(name, l):
        a = w[name][l]
        return (jnp.swapaxes(w[name], 1, 2)[l] if name in TRANSPOSED else a).astype(BF16)

    groups = (BIG[:3], BIG[3:])

    def shards_of(l, g):
        return [shard(name, l) for name in groups[g]] if l < n_l else []

    def finish_gather(l, g, lands):
        out = _sibling_gather(shards_of(l, g), lands)
        return {name: a.reshape(-1, d) for name, a in zip(groups[g], out)}

    shard_rows = [w[name].shape[2] if name in TRANSPOSED else w[name].shape[1] for name in BIG]

    conv_shapes = [conf_conv_w.shape, sconv_w.shape]
    first_lands, conv_all = _ici_gather(shards_of(0, 0), _pack([conf_conv_w, sconv_w]))
    conv_all = conv_all.reshape(N_CHIPS, 2, -1)[:, 0]
    conf_full, sconv_full = [jnp.concatenate([_unpack(conv_all[k], conv_shapes)[a] for k in range(N_CHIPS)], axis=-1)
                             for a in range(2)]

    lane = jnp.arange(C) // HEAD_DIM
    head_rows = (jnp.arange(8)[:, None] == lane[None, :]).astype(F32)
    tril = jnp.tril(jnp.ones((CHUNK, CHUNK), F32))
    tril4 = jnp.tile(tril, (N_HEADS, 1))
    mixer_consts = []
    for l in range(n_l):
        cw = jnp.pad(conf_full[l], ((0, 32 - CONF_KERNEL), (0, 0)))
        vec = jnp.concatenate([conf_conv_b[l][None], conf_ln_g[l][None], conf_ln_b[l][None], pool_scale[l][None],
                               gmlp_ln_g[l][None], gmlp_ln_b[l][None], sconv_full[l], jnp.zeros((7, C), F32)], axis=0)
        eye = jnp.eye(len(pool_w[l]), dtype=F32)
        pool_blk = (eye[:, None, :, None] * pool_w[l][:, :, None, :]).reshape(C, C).astype(BF16)
        ws = gmlp_w_s[l] * tril[None]
        wstack = ws.reshape(N_HEADS * CHUNK, CHUNK).astype(BF16)
        wstack_t = jnp.swapaxes(ws, 1, 2).reshape(N_HEADS * CHUNK, CHUNK).astype(BF16)
        bias = jnp.repeat(gmlp_b_s[l].T, HEAD_DIM, axis=1)
        mixer_consts.append((cw, vec, pool_blk, wstack, wstack_t, bias))

    saved = []
    cur = xs
    gathered = [finish_gather(0, 0, first_lands)]
    for l in range(n_l):
        gw = gathered[l]
        cw, vec, pool_blk, wstack, wstack_t, bias = mixer_consts[l]
        x0 = cur
        x1, a1, b1, h1, lands = _ffn_fwd(x0, ffn1_norm[l][None], gw["ffn1_w1"], gw["ffn1_w3"], gw["ffn1_w2"], shards_of(l, 1))
        gw.update(finish_gather(l, 1, lands))
        p = _proj_fwd(x1, mix_norm[l][None], gw["w_in"])
        x2, mix, z = _mixer_fwd(p, x1, cw, vec, pool_blk, wstack, bias, gw["w_out"])
        x3, a2, b2, h2, lands = _ffn_fwd(x2, ffn2_norm[l][None], gw["ffn2_w1"], gw["ffn2_w3"], gw["ffn2_w2"], shards_of(l + 1, 0))
        if l + 1 < n_l:
            gathered.append(finish_gather(l + 1, 0, lands))
        saved.append((x0, x1, x2, a1, b1, a2, b2, p, mix, z, h1, h2))
        cur = x3

    dx, dg_final, loss_part = _loss_bwd(cur, final_norm[None], loss_target[0])

    small_parts = [None] * n_l
    reduced_halves = [[None, None] for _ in range(n_l)]
    pair_sum = lambda big, g: _pair_sum([big[name].reshape(N_CHIPS, -1, d) for name in groups[g]])
    pending = None
    for l in reversed(range(n_l)):
        gw = gathered[l]
        cw, vec, pool_blk, wstack, wstack_t, bias = mixer_consts[l]
        x0, x1, x2, a1, b1, a2, b2, p, mix, z, h1, h2 = saved[l]
        big = {}
        out = _ffn_bwd(dx, x2, ffn2_norm[l][None], a2, b2, gw["ffn2_w1"], gw["ffn2_w3"], gw["ffn2_w2"], pending)
        dx, dg_ffn2, dy, da, db, u, dx_bf = out[:7]
        if pending is not None:
            reduced_halves[l + 1][0] = out[7]
        big["ffn2_w1"], big["ffn2_w3"], big["ffn2_w2"] = _dw(da, h2), _dw(db, h2), _dw(u, dy)
        big["w_out"] = _dw(mix, dx_bf)
        dp, dcw, dvec, dpool, dws, dbs = _mixer_bwd(dx, p, z, cw, vec, pool_blk, wstack, wstack_t, bias, tril4, head_rows, gw["w_out"])
        dx, dg_mix, h = _proj_bwd(dx, x1, mix_norm[l][None], dp, gw["w_in"])
        big["w_in"] = _dw(dp, h)
        second = pair_sum(big, 1)
        out = _ffn_bwd(dx, x0, ffn1_norm[l][None], a1, b1, gw["ffn1_w1"], gw["ffn1_w3"], gw["ffn1_w2"], second)
        dx, dg_ffn1, dy, da, db, u = out[:6]
        reduced_halves[l][1] = out[7]
        big["ffn1_w1"], big["ffn1_w3"], big["ffn1_w2"] = _dw(da, h1), _dw(db, h1), _dw(u, dy)
        small_parts[l] = [dg_ffn1[0], dg_mix[0], dg_ffn2[0], dcw, dvec, dpool, dws, dbs]
        pending = pair_sum(big, 0)
    reduced_halves[0][0] = _reduce_chips(pending)
    grad_x = dx[None]

    part_shapes = [a.shape for a in small_parts[0]]
    tail = [dg_final[0], loss_part[0]]
    packed = _pack([a for l in range(n_l) for a in small_parts[l]] + tail)
    index_of = {name: a for a, name in enumerate(BIG)}
    full, summed = _sum_share(reduced_halves, [[index_of[name] for name in g] for g in groups], shard_rows, packed)
    full = dict(zip(BIG, full))
    grad = {}
    summed = _unpack(summed, part_shapes * n_l + [a.shape for a in tail])
    per_layer = [summed[l * len(part_shapes):(l + 1) * len(part_shapes)] for l in range(n_l)]
    stack = lambda k: jnp.stack([per_layer[l][k] for l in range(n_l)])
    dcw_all, dvec_all, dpool_all, dws_all, dbs_all = stack(3), stack(4), stack(5), stack(6), stack(7)
    loss = summed[-1][0]
    chip_cols = lambda a: lax.dynamic_slice_in_dim(a, chip * (C // N_CHIPS), C // N_CHIPS, axis=2)
    n_pool = pool_w.shape[1]
    grad.update(
        ffn1_norm=stack(0), mix_norm=stack(1), ffn2_norm=stack(2), final_norm=summed[-2],
        conf_conv_w=chip_cols(dcw_all[:, :CONF_KERNEL]), conf_conv_b=dvec_all[:, 0], conf_ln_g=dvec_all[:, 1],
        conf_ln_b=dvec_all[:, 2], pool_scale=dvec_all[:, 3], gmlp_ln_g=dvec_all[:, 4], gmlp_ln_b=dvec_all[:, 5],
        sconv_w=chip_cols(dvec_all[:, 6:6 + SHORT_KERNEL]),
        pool_w=jnp.stack([dpool_all[:, g * POOL_GROUP:(g + 1) * POOL_GROUP, g * POOL_GROUP:(g + 1) * POOL_GROUP]
                          for g in range(n_pool)], axis=1),
        gmlp_w_s=dws_all.reshape(n_l, N_HEADS, CHUNK, CHUNK), gmlp_b_s=dbs_all[:, :N_HEADS],
    )

    delta, new_m, new_v = {}, {}, {}
    for name in BIG:
        rows_of = (lambda a: jnp.swapaxes(a, 1, 2)) if name in TRANSPOSED else (lambda a: a)
        shp = full[name].shape
        out = _adamw(_as2d(rows_of(w[name])), _as2d(full[name]), _as2d(rows_of(mom[name])), _as2d(rows_of(var[name])))
        delta[name], new_m[name], new_v[name], grad[name] = (rows_of(o.reshape(shp)) for o in out)
    ds, ms, vs = _adamw_small([_as2d(w[k]) if w[k].ndim > 1 else w[k][None] for k in SMALL],
                              [_as2d(grad[k]) if grad[k].ndim > 1 else grad[k][None] for k in SMALL],
                              [_as2d(mom[k]) if mom[k].ndim > 1 else mom[k][None] for k in SMALL],
                              [_as2d(var[k]) if var[k].ndim > 1 else var[k][None] for k in SMALL])
    for k, dl, mo, vo in zip(SMALL, ds, ms, vs):
        delta[k], new_m[k], new_v[k] = dl.reshape(w[k].shape), mo.reshape(w[k].shape), vo.reshape(w[k].shape)

    return (loss, grad_x, *[grad[k] for k in ORDER], *[delta[k] for k in ORDER], *[new_m[k] for k in ORDER],
            *[new_v[k] for k in ORDER])
```

```python
import functools

import jax
import jax.numpy as jnp
from jax import lax
from jax.experimental import pallas as pl
from jax.experimental.pallas import tpu as pltpu

F32 = jnp.float32
BF16 = jnp.bfloat16
MESH = pl.DeviceIdType.MESH
ANY = pl.BlockSpec(memory_space=pl.ANY)

EPS = 1e-6
FFN_RESIDUAL = 0.5
D_GROUP = 256
CONF_KERNEL = 31
SHORT_KERNEL = 3
POOL_GROUP = 64
CHUNK = 128
N_HEADS = 4
HEAD_DIM = 64
HALO = 32
N_CHIPS = 4
N_DEV = 8
LANES = 128
MXU_TILE = 256
VMEM_LIMIT = 56 * 2**20
T_MIX_FWD = 1024
T_MIX_BWD = 1024

ADAM_LR = 0.001
ADAM_B1 = 0.9
ADAM_B2 = 0.999
ADAM_EPS = 1e-08
ADAM_WD = 0.01
ADAM_STEP = 10

NT = (((1,), (1,)), ((), ()))
TN = (((0,), (0,)), ((), ()))


def _params(*sem):
    return pltpu.CompilerParams(dimension_semantics=sem, vmem_limit_bytes=VMEM_LIMIT)


def _dot(a, b):
    return jnp.dot(a, b, preferred_element_type=F32)


def _dot_nt(a, b):
    return lax.dot_general(a, b, NT, preferred_element_type=F32)


def _t_bf16(v):
    return jnp.transpose(v).astype(BF16)


def _sigmoid(v):
    return 1.0 / (1.0 + jnp.exp(-v))


def _rms(x, g):
    r = lax.rsqrt(jnp.mean(x * x, axis=-1, keepdims=True) + EPS)
    n = x * r
    return n, r, n * g


def _rms_bwd(dh, n, r, g):
    dn = dh * g
    dx = r * (dn - n * jnp.mean(dn * n, axis=-1, keepdims=True))
    return dx, jnp.sum(dh * n, axis=0, keepdims=True)


def _ln_fwd(z, g, b):
    mu = jnp.mean(z, axis=-1, keepdims=True)
    zc = z - mu
    rs = lax.rsqrt(jnp.mean(zc * zc, axis=-1, keepdims=True) + EPS)
    zn = zc * rs
    return zn, rs, zn * g + b


def _ln_bwd(dl, zn, rs, g):
    dzn = dl * g
    dz = rs * (dzn - jnp.mean(dzn, axis=-1, keepdims=True) - zn * jnp.mean(dzn * zn, axis=-1, keepdims=True))
    return dz, jnp.sum(dl * zn, axis=0, keepdims=True), jnp.sum(dl, axis=0, keepdims=True)


def _tile(n, want):
    return want if n % want == 0 else n


def _resident(shape):
    return pl.BlockSpec(shape, lambda i: (0,) * len(shape), pipeline_mode=pl.Buffered(1))


def _ffn_fwd(x, g, w1t, w3t, w2, shards=()):
    s, d = x.shape
    f = w1t.shape[0]
    tm, tf = _tile(s, 512), MXU_TILE
    nj = f // tf
    ni = s // tm
    n_c = len(shards)

    def body(*refs):
        x_ref, g_ref, w1_ref, w3_ref, w2_ref = refs[:5]
        xo_ref, a_ref, b_ref, h_ref = refs[5 + n_c:9 + n_c]
        u_s = refs[9 + 2 * n_c]
        if n_c:
            gather = lambda: _ici_gather_copies(refs[5:5 + n_c], refs[9 + n_c:9 + 2 * n_c], refs[10 + 2 * n_c], refs[11 + 2 * n_c])

            @pl.when(pl.program_id(0) == 0)
            def _():
                for cp in gather():
                    cp.start()

        h = _rms(x_ref[...], g_ref[...])[2].astype(BF16)
        h_ref[...] = h
        for j in range(nj):
            cols = slice(j * tf, (j + 1) * tf)
            a = _dot_nt(h, w1_ref[cols, :])
            b = _dot_nt(h, w3_ref[cols, :])
            a_ref[:, cols] = a.astype(BF16)
            b_ref[:, cols] = b.astype(BF16)
            u_s[:, cols] = ((a * _sigmoid(a)) * b).astype(BF16)
        xo_ref[...] = x_ref[...] + FFN_RESIDUAL * _dot(u_s[...], w2_ref[...])
        if n_c:
            @pl.when(pl.program_id(0) == ni - 1)
            def _():
                _wait_all(gather())

    row = pl.BlockSpec((tm, d), lambda i: (i, 0))
    hid = pl.BlockSpec((tm, f), lambda i: (i, 0))
    sems = [pltpu.SemaphoreType.DMA((3 * n_c,))] * 2 if n_c else []
    out = pl.pallas_call(
        body, name="ffn_fwd_gather" if n_c else "ffn_fwd", grid=(ni,),
        in_specs=[row, _resident(g.shape), _resident(w1t.shape), _resident(w3t.shape), _resident(w2.shape)] + [ANY] * n_c,
        out_specs=[row, hid, hid, row] + [ANY] * n_c,
        out_shape=[jax.ShapeDtypeStruct((s, d), F32), jax.ShapeDtypeStruct((s, f), BF16), jax.ShapeDtypeStruct((s, f), BF16),
                   jax.ShapeDtypeStruct((s, d), BF16)] + _landing_shapes(shards),
        scratch_shapes=[pltpu.VMEM((tm, f), BF16)] + sems,
        compiler_params=_params("arbitrary"),
    )(x, g, w1t, w3t, w2, *shards)
    return out[0], out[1], out[2], out[3], list(out[4:])


def _ffn_bwd(dxo, x, g, a, b, w1t, w3t, w2, sums=None):
    s, d = x.shape
    f = w1t.shape[0]
    tm, tf = _tile(s, 256), MXU_TILE
    nj = f // tf
    ni = s // tm
    n_c = 0 if sums is None else 1

    def body(*refs):
        dxo_ref, x_ref, g_ref, a_ref, b_ref, w1_ref, w3_ref, w2_ref = refs[:8]
        dx_ref, dg_ref, dy_ref, da_ref, db_ref, u_ref, dxb_ref = refs[8 + n_c:15 + n_c]
        if n_c:
            exchange = lambda: _ici_reduce_copies(refs[8], refs[15 + n_c], *refs[15 + 2 * n_c:])

        @pl.when(pl.program_id(0) == 0)
        def _():
            dg_ref[...] = jnp.zeros_like(dg_ref)
            if n_c:
                copies, local = exchange()
                local.start()
                for cp in copies:
                    cp.start()

        dy = (FFN_RESIDUAL * dxo_ref[...]).astype(BF16)
        dy_ref[...] = dy
        for j in range(nj):
            cols = slice(j * tf, (j + 1) * tf)
            du = _dot_nt(dy, w2_ref[cols, :]).astype(BF16)
            av = a_ref[:, cols]
            bv = b_ref[:, cols]
            sg = _sigmoid(av.astype(F32)).astype(BF16)
            sl = av * sg
            da_ref[:, cols] = (du * bv) * (sg + sl * (1.0 - sg))
            db_ref[:, cols] = du * sl
            u_ref[:, cols] = sl * bv
        dh = _dot(da_ref[...], w1_ref[...]) + _dot(db_ref[...], w3_ref[...])
        n, r, _ = _rms(x_ref[...], g_ref[...])
        dxr, dg = _rms_bwd(dh, n, r, g_ref[...])
        dx = dxo_ref[...] + dxr
        dx_ref[...] = dx
        dxb_ref[...] = dx.astype(BF16)
        dg_ref[0:1, :] += dg
        if n_c:
            @pl.when(pl.program_id(0) == ni - 1)
            def _():
                copies, local = exchange()
                _wait_all(copies)
                local.wait()

    row = pl.BlockSpec((tm, d), lambda i: (i, 0))
    hid = pl.BlockSpec((tm, f), lambda i: (i, 0))
    extra = [] if sums is None else [sums]
    sems = [pltpu.SemaphoreType.DMA((3,)), pltpu.SemaphoreType.DMA((3,)), pltpu.SemaphoreType.DMA] if n_c else []
    out = pl.pallas_call(
        body, name="ffn_bwd_reduce" if n_c else "ffn_bwd", grid=(ni,),
        in_specs=[row, row, _resident(g.shape), hid, hid, _resident(w1t.shape), _resident(w3t.shape), _resident(w2.shape)]
        + [ANY] * n_c,
        out_specs=[row, pl.BlockSpec((8, d), lambda i: (0, 0)), row, hid, hid, hid, row] + [ANY] * n_c,
        out_shape=[jax.ShapeDtypeStruct((s, d), F32), jax.ShapeDtypeStruct((8, d), F32), jax.ShapeDtypeStruct((s, d), BF16),
                   jax.ShapeDtypeStruct((s, f), BF16), jax.ShapeDtypeStruct((s, f), BF16), jax.ShapeDtypeStruct((s, f), BF16),
                   jax.ShapeDtypeStruct((s, d), BF16)]
        + [jax.ShapeDtypeStruct(e.shape, e.dtype) for e in extra],
        scratch_shapes=sems,
        compiler_params=_params("arbitrary"),
    )(dxo, x, g, a, b, w1t, w3t, w2, *extra)
    return out


def _dw(am, bm):
    s, r = am.shape
    n = bm.shape[1]
    rb = r // 2 if r > 1024 and (r // 2) % LANES == 0 else r
    ts = _tile(s, 2048)
    ni = s // ts

    def body(am_ref, bm_ref, o_ref, acc_s):
        i = pl.program_id(1)

        @pl.when(i == 0)
        def _():
            acc_s[...] = jnp.zeros_like(acc_s)

        acc_s[...] += lax.dot_general(am_ref[...], bm_ref[...], TN, preferred_element_type=F32)

        @pl.when(i == ni - 1)
        def _():
            o_ref[...] = acc_s[...].astype(BF16)

    return pl.pallas_call(
        body, name="dw", grid=(r // rb, ni),
        in_specs=[pl.BlockSpec((ts, rb), lambda k, i: (i, k)), pl.BlockSpec((ts, n), lambda k, i: (i, 0))],
        out_specs=pl.BlockSpec((rb, n), lambda k, i: (k, 0)),
        out_shape=jax.ShapeDtypeStruct((r, n), BF16),
        scratch_shapes=[pltpu.VMEM((rb, n), F32)],
        compiler_params=_params("arbitrary", "arbitrary"),
    )(am, bm)


def _proj_fwd(x, g, w_int):
    s, d = x.shape
    f = w_int.shape[0]
    tm = _tile(s, 512)

    def body(x_ref, g_ref, w_ref, p_ref):
        p_ref[...] = _dot_nt(_rms(x_ref[...], g_ref[...])[2].astype(BF16), w_ref[...])

    return pl.pallas_call(
        body, name="proj_fwd", grid=(s // tm,),
        in_specs=[pl.BlockSpec((tm, d), lambda i: (i, 0)), _resident(g.shape), _resident(w_int.shape)],
        out_specs=pl.BlockSpec((tm, f), lambda i: (i, 0)),
        out_shape=jax.ShapeDtypeStruct((s, f), F32),
        compiler_params=_params("arbitrary"),
    )(x, g, w_int)


def _proj_bwd(dxo, x, g, dp, w_int):
    s, d = x.shape
    f = w_int.shape[0]
    tm = _tile(s, 512)

    def body(dxo_ref, x_ref, g_ref, dp_ref, w_ref, dx_ref, dg_ref, h_ref):
        @pl.when(pl.program_id(0) == 0)
        def _():
            dg_ref[...] = jnp.zeros_like(dg_ref)

        n, r, h = _rms(x_ref[...], g_ref[...])
        h_ref[...] = h.astype(BF16)
        dxr, dg = _rms_bwd(_dot(dp_ref[...], w_ref[...]), n, r, g_ref[...])
        dx_ref[...] = dxo_ref[...] + dxr
        dg_ref[0:1, :] += dg

    row = pl.BlockSpec((tm, d), lambda i: (i, 0))
    return pl.pallas_call(
        body, name="proj_bwd", grid=(s // tm,),
        in_specs=[row, row, _resident(g.shape), pl.BlockSpec((tm, f), lambda i: (i, 0)), _resident(w_int.shape)],
        out_specs=[row, pl.BlockSpec((8, d), lambda i: (0, 0)), row],
        out_shape=[jax.ShapeDtypeStruct((s, d), F32), jax.ShapeDtypeStruct((8, d), F32), jax.ShapeDtypeStruct((s, d), BF16)],
        compiler_params=_params("arbitrary"),
    )(dxo, x, g, dp, w_int)


C = D_GROUP


def _piece(ref, k):
    return ref[:, k * C:(k + 1) * C]


def _up(v, r):
    return v if r == 0 else pltpu.roll(v, v.shape[0] - r, 0)


def _down(v, r):
    return v if r == 0 else pltpu.roll(v, r, 0)


def _lane_group():
    lane = lax.broadcasted_iota(jnp.int32, (1, C), 1)
    return (lane >= POOL_GROUP).astype(jnp.int32) + (lane >= 2 * POOL_GROUP).astype(jnp.int32) + (
        lane >= 3 * POOL_GROUP).astype(jnp.int32)


def _by_group(grp, v2, v4, v8, v16):
    return jnp.where(grp == 0, v2, jnp.where(grp == 1, v4, jnp.where(grp == 2, v8, v16)))


def _pool_count(grp, row0, t):
    pos = (row0 + lax.broadcasted_iota(jnp.int32, (t, C), 0) + 1).astype(F32)
    return jnp.minimum(pos, _by_group(grp, 2.0, 4.0, 8.0, 16.0))


def _trailing_sums(ext, grp, t):
    s2 = ext + _down(ext, 1)
    s4 = s2 + _down(s2, 2)
    s8 = s4 + _down(s4, 4)
    s16 = s8 + _down(s8, 8)
    return _by_group(grp, s2, s4, s8, s16)[HALO:HALO + t]


def _leading_sums(ext, grp, t):
    s2 = ext + _up(ext, 1)
    s4 = s2 + _up(s2, 2)
    s8 = s4 + _up(s4, 4)
    s16 = s8 + _up(s8, 8)
    return _by_group(grp, s2, s4, s8, s16)[0:t]


def _head_select(r4, grp):
    assert HEAD_DIM == POOL_GROUP and N_HEADS == 4
    return _by_group(grp, *(r4[h * CHUNK:(h + 1) * CHUNK] for h in range(N_HEADS)))


def _conv_taps():
    return [(k, (k + 2) % 8, (k + 2) - (k + 2) % 8) for k in range(CONF_KERNEL)]


def _mixer_fwd(p, x1, cw, vec, pool_w, wstack, bias, w_out):
    s, d = x1.shape
    t = _tile(s, T_MIX_FWD)
    n_ext = t + HALO
    dm = w_out.shape[0]

    def body(p_ref, x1_ref, cw_ref, vec_ref, pw_ref, ws_ref, bias_ref, wo_ref, x2_ref, mix_s, z_ref, cy_s, cq_s, cx_s):
        i = pl.program_id(0)

        @pl.when(i == 0)
        def _():
            cy_s[...] = jnp.zeros_like(cy_s)
            cq_s[...] = jnp.zeros_like(cq_s)
            cx_s[...] = jnp.zeros_like(cx_s)

        grp = _lane_group()
        y = _piece(p_ref, 0) * _sigmoid(_piece(p_ref, 1))
        ext = jnp.concatenate([cy_s[...], y], axis=0)
        cy_s[...] = y[t - HALO:t]
        z = jnp.broadcast_to(vec_ref[0:1, :], (t, C))
        shifted = {}
        for k, r, off in _conv_taps():
            if r not in shifted:
                shifted[r] = _up(ext, r)
            z = z + cw_ref[k:k + 1, :] * shifted[r][off:off + t]
        z_ref[...] = z
        ln = _ln_fwd(z, vec_ref[1:2, :], vec_ref[2:3, :])[2]
        mix_s[:, 0:C] = (ln * _sigmoid(ln)).astype(BF16)
        q = _piece(p_ref, 3) * _piece(p_ref, 4)
        ext = jnp.concatenate([cq_s[...], q], axis=0)
        cq_s[...] = q[t - HALO:t]
        cz = vec_ref[8:9, :] * q + vec_ref[7:8, :] * _down(ext, 1)[HALO:] + vec_ref[6:7, :] * _down(ext, 2)[HALO:]
        mix_s[:, C:2 * C] = (_piece(p_ref, 2) * cz).astype(BF16)
        xp = _piece(p_ref, 5)
        ext = jnp.concatenate([cx_s[...], xp], axis=0)
        cx_s[...] = xp[t - HALO:t]
        dd = _trailing_sums(ext, grp, t) / _pool_count(grp, i * t, t) - xp
        mix_s[:, 2 * C:3 * C] = (_dot(dd.astype(BF16), pw_ref[...]) * vec_ref[3:4, :]).astype(BF16)
        vln = _ln_fwd(_piece(p_ref, 7), vec_ref[4:5, :], vec_ref[5:6, :])[2].astype(BF16)
        for n in range(t // CHUNK):
            rows = slice(n * CHUNK, (n + 1) * CHUNK)
            mixed = _head_select(_dot(ws_ref[...], vln[rows]), grp) + bias_ref[...]
            mix_s[rows, 3 * C:4 * C] = (p_ref[rows, 6 * C:7 * C] * mixed).astype(BF16)
        x2_ref[...] = x1_ref[...] + _dot(mix_s[...], wo_ref[...])

    full = lambda a: pl.BlockSpec(a.shape, lambda i: (0, 0))
    return pl.pallas_call(
        body, name="mixer_fwd", grid=(s // t,),
        in_specs=[pl.BlockSpec((t, p.shape[1]), lambda i: (i, 0)), pl.BlockSpec((t, d), lambda i: (i, 0)),
                  full(cw), full(vec), full(pool_w), full(wstack), full(bias), full(w_out)],
        out_specs=[pl.BlockSpec((t, d), lambda i: (i, 0)), pl.BlockSpec((t, dm), lambda i: (i, 0)),
                   pl.BlockSpec((t, C), lambda i: (i, 0))],
        out_shape=[jax.ShapeDtypeStruct((s, d), F32), jax.ShapeDtypeStruct((s, dm), BF16), jax.ShapeDtypeStruct((s, C), F32)],
        scratch_shapes=[pltpu.VMEM((HALO, C), F32)] * 3,
        compiler_params=_params("arbitrary"),
    )(p, x1, cw, vec, pool_w, wstack, bias, w_out)


def _mixer_bwd(dx2, p, z, cw, vec, pool_w, wstack, wstack_t, bias, tril4, head_rows, w_out):
    s, d = dx2.shape
    t = _tile(s, T_MIX_BWD)
    nt = s // t
    n_ext = t + HALO
    hb = t // HALO

    def body(dx2_ref, p_ref, ph_ref, z_ref, cw_ref, vec_ref, pw_ref, ws_ref, wst_ref, bias_ref, tril_ref, hr_ref, wo_ref,
             dp_ref, dcw_ref, dvec_ref, dpool_ref, dws_ref, dbs_ref, cdz_s, cdc_s, cf_s, vy_s, dvl_s, dbias_s):
        i = pl.program_id(0)
        tile = nt - 1 - i

        @pl.when(i == 0)
        def _():
            for ref in (cdz_s, cdc_s, cf_s, dbias_s, dcw_ref, dvec_ref, dpool_ref, dws_ref, dbs_ref):
                ref[...] = jnp.zeros_like(ref)

        grp = _lane_group()
        first = jnp.where(tile > 0, 1.0, 0.0)
        dmix = _dot_nt(dx2_ref[...].astype(BF16), wo_ref[...])
        d_a, d_b, d_c, d_d = (dmix[:, k * C:(k + 1) * C] for k in range(4))

        def acc_vec(row, v):
            dvec_ref[row:row + 1, :] += jnp.sum(v, axis=0, keepdims=True)

        val, gate = _piece(p_ref, 0), _piece(p_ref, 1)
        sgate = _sigmoid(gate)
        y = val * sgate
        y_halo = ph_ref[:, 0:C] * _sigmoid(ph_ref[:, C:2 * C]) * first
        ext = jnp.concatenate([y_halo, y], axis=0)
        for r in range(8):
            vy_s[r] = _up(ext, r)
        zn, rs, ln = _ln_fwd(z_ref[...], vec_ref[1:2, :], vec_ref[2:3, :])
        sg = _sigmoid(ln)
        dln = d_a * (sg * (1.0 + ln * (1.0 - sg)))
        dz, dg, db = _ln_bwd(dln, zn, rs, vec_ref[1:2, :])
        dvec_ref[1:2, :] += dg
        dvec_ref[2:3, :] += db
        acc_vec(0, dz)
        for k, r, off in _conv_taps():
            dcw_ref[k:k + 1, :] += jnp.sum(dz * vy_s[r, off:off + t, :], axis=0, keepdims=True)
        ext = jnp.concatenate([dz, cdz_s[...]], axis=0)
        cdz_s[...] = dz[0:HALO]
        dy = jnp.zeros((t, C), F32)
        shifted = {}
        for k in range(CONF_KERNEL):
            m = CONF_KERNEL - 1 - k
            r, off = m % 8, m - m % 8
            if r not in shifted:
                shifted[r] = _up(ext, r)
            dy = dy + cw_ref[k:k + 1, :] * shifted[r][off:off + t]
        dp_ref[:, 0:C] = (dy * sgate).astype(BF16)
        dp_ref[:, C:2 * C] = (dy * val * sgate * (1.0 - sgate)).astype(BF16)

        sb, sc, sx = _piece(p_ref, 2), _piece(p_ref, 3), _piece(p_ref, 4)
        q = sc * sx
        q_halo = ph_ref[:, 3 * C:4 * C] * ph_ref[:, 4 * C:5 * C] * first
        ext = jnp.concatenate([q_halo, q], axis=0)
        q1, q2 = _down(ext, 1)[HALO:], _down(ext, 2)[HALO:]
        cz = vec_ref[8:9, :] * q + vec_ref[7:8, :] * q1 + vec_ref[6:7, :] * q2
        dcz = d_b * sb
        dp_ref[:, 2 * C:3 * C] = (d_b * cz).astype(BF16)
        acc_vec(8, dcz * q)
        acc_vec(7, dcz * q1)
        acc_vec(6, dcz * q2)
        ext = jnp.concatenate([dcz, cdc_s[...]], axis=0)
        cdc_s[...] = dcz[0:HALO]
        dq = vec_ref[8:9, :] * dcz + vec_ref[7:8, :] * _up(ext, 1)[0:t] + vec_ref[6:7, :] * _up(ext, 2)[0:t]
        dp_ref[:, 3 * C:4 * C] = (dq * sx).astype(BF16)
        dp_ref[:, 4 * C:5 * C] = (dq * sc).astype(BF16)

        xp = _piece(p_ref, 5)
        ext = jnp.concatenate([ph_ref[:, 5 * C:6 * C] * first, xp], axis=0)
        cnt = _pool_count(grp, tile * t, t)
        dd = (_trailing_sums(ext, grp, t) / cnt - xp).astype(BF16)
        e2 = _dot(dd, pw_ref[...])
        acc_vec(3, d_c * e2)
        de = (d_c * vec_ref[3:4, :]).astype(BF16)
        dpool_ref[...] += _dot(_t_bf16(dd.astype(F32)), de)
        ddd = _dot_nt(de, pw_ref[...])
        fq = ddd / cnt
        ext = jnp.concatenate([fq, cf_s[...]], axis=0)
        cf_s[...] = fq[0:HALO]
        dp_ref[:, 5 * C:6 * C] = (_leading_sums(ext, grp, t) - ddd).astype(BF16)

        vn, vrs, vlnf = _ln_fwd(_piece(p_ref, 7), vec_ref[4:5, :], vec_ref[5:6, :])
        vln = vlnf.astype(BF16)
        for n in range(t // CHUNK):
            rows = slice(n * CHUNK, (n + 1) * CHUNK)
            mixed = _head_select(_dot(ws_ref[...], vln[rows]), grp) + bias_ref[...]
            dd_n = d_d[rows]
            dp_ref[rows, 6 * C:7 * C] = (dd_n * mixed).astype(BF16)
            dmx = dd_n * p_ref[rows, 6 * C:7 * C]
            dbias_s[...] += dmx
            dmx_b = dmx.astype(BF16)
            dvl_s[rows, :] = _head_select(_dot(wst_ref[...], dmx_b), grp)
            for h in range(N_HEADS):
                hrows = slice(h * CHUNK, (h + 1) * CHUNK)
                dws_ref[hrows, :] += _dot_nt(jnp.where(grp == h, dmx_b, jnp.zeros_like(dmx_b)), vln[rows])
        dvl = dvl_s[...]
        dv, dg, db = _ln_bwd(dvl, vn, vrs, vec_ref[4:5, :])
        dvec_ref[4:5, :] += dg
        dvec_ref[5:6, :] += db
        dp_ref[:, 7 * C:8 * C] = dv.astype(BF16)

        @pl.when(i == nt - 1)
        def _():
            dws_ref[...] = dws_ref[...] * tril_ref[...]
            dbs_ref[...] = lax.dot_general(hr_ref[...], dbias_s[...], NT, precision=lax.Precision.HIGHEST,
                                           preferred_element_type=F32)

    full = lambda a: pl.BlockSpec(a.shape, lambda i: (0, 0))
    acc = lambda shape: pl.BlockSpec(shape, lambda i: (0, 0))
    f = p.shape[1]
    return pl.pallas_call(
        body, name="mixer_bwd", grid=(nt,),
        in_specs=[pl.BlockSpec((t, d), lambda i: (nt - 1 - i, 0)), pl.BlockSpec((t, f), lambda i: (nt - 1 - i, 0)),
                  pl.BlockSpec((HALO, f), lambda i: (jnp.maximum((nt - 1 - i) * hb - 1, 0), 0)),
                  pl.BlockSpec((t, C), lambda i: (nt - 1 - i, 0)), full(cw), full(vec), full(pool_w), full(wstack), full(wstack_t), full(bias), full(tril4), full(head_rows),
                  full(w_out)],
        out_specs=[pl.BlockSpec((t, f), lambda i: (nt - 1 - i, 0)), acc((32, C)), acc((16, C)), acc((C, C)),
                   acc((N_HEADS * CHUNK, CHUNK)), acc((8, CHUNK))],
        out_shape=[jax.ShapeDtypeStruct((s, f), BF16), jax.ShapeDtypeStruct((32, C), F32), jax.ShapeDtypeStruct((16, C), F32),
                   jax.ShapeDtypeStruct((C, C), F32), jax.ShapeDtypeStruct((N_HEADS * CHUNK, CHUNK), F32),
                   jax.ShapeDtypeStruct((8, CHUNK), F32)],
        scratch_shapes=[pltpu.VMEM((HALO, C), F32)] * 3 + [pltpu.VMEM((8, n_ext, C), F32), pltpu.VMEM((t, C), F32),
                                                            pltpu.VMEM((CHUNK, C), F32)],
        compiler_params=_params("arbitrary"),
    )(dx2, p, p, z, cw, vec, pool_w, wstack, wstack_t, bias, tril4, head_rows, w_out)


def _loss_bwd(x, g, target):
    s, d = x.shape
    tm = _tile(s, 512)

    def body(x_ref, g_ref, t_ref, dx_ref, dg_ref, loss_ref):
        @pl.when(pl.program_id(0) == 0)
        def _():
            dg_ref[...] = jnp.zeros_like(dg_ref)
            loss_ref[...] = jnp.zeros_like(loss_ref)

        n, r, y = _rms(x_ref[...], g_ref[...])
        err = y - t_ref[...]
        loss_ref[...] += 0.5 * jnp.sum(jnp.mean(err * err, axis=-1, keepdims=True), axis=0, keepdims=True)
        dxr, dg = _rms_bwd(err * (1.0 / d), n, r, g_ref[...])
        dx_ref[...] = dxr
        dg_ref[0:1, :] += dg

    row = pl.BlockSpec((tm, d), lambda i: (i, 0))
    return pl.pallas_call(
        body, name="loss_bwd", grid=(s // tm,),
        in_specs=[row, pl.BlockSpec((1, d), lambda i: (0, 0)), row],
        out_specs=[row, pl.BlockSpec((8, d), lambda i: (0, 0)), pl.BlockSpec((8, LANES), lambda i: (0, 0))],
        out_shape=[jax.ShapeDtypeStruct((s, d), F32), jax.ShapeDtypeStruct((8, d), F32), jax.ShapeDtypeStruct((8, LANES), F32)],
        compiler_params=_params("arbitrary"),
    )(x, g, target)


def _adamw_math(w, g, m, v):
    m = ADAM_B1 * m + (1.0 - ADAM_B1) * g
    v = ADAM_B2 * v + (1.0 - ADAM_B2) * (g * g)
    m_hat = m / (1.0 - ADAM_B1 ** ADAM_STEP)
    v_hat = v / (1.0 - ADAM_B2 ** ADAM_STEP)
    return -ADAM_LR * (m_hat / (jnp.sqrt(v_hat) + ADAM_EPS) + ADAM_WD * w), m, v


def _adamw(w, g, m, v):
    r, c = w.shape
    tr = r // 8 if r % 64 == 0 else r

    def body(w_ref, g_ref, m_ref, v_ref, d_ref, mo_ref, vo_ref, go_ref):
        gv = g_ref[...]
        d_ref[...], mo_ref[...], vo_ref[...] = _adamw_math(w_ref[...], gv, m_ref[...], v_ref[...])
        go_ref[...] = gv

    blk = pl.BlockSpec((tr, c), lambda i: (i, 0))
    return pl.pallas_call(
        body, name="adamw", grid=(r // tr,), in_specs=[blk] * 4, out_specs=[blk] * 4,
        out_shape=[jax.ShapeDtypeStruct((r, c), F32)] * 4, compiler_params=_params("arbitrary"),
    )(w, g, m, v)


def _adamw_small(ws, gs, ms, vs):
    n = len(ws)

    def body(*refs):
        ins, outs = refs[:4 * n], refs[4 * n:]
        for k in range(n):
            dl, mo, vo = _adamw_math(ins[k][...], ins[n + k][...], ins[2 * n + k][...], ins[3 * n + k][...])
            outs[k][...], outs[n + k][...], outs[2 * n + k][...] = dl, mo, vo

    vm = pl.BlockSpec(memory_space=pltpu.VMEM)
    out = pl.pallas_call(
        body, name="adamw_small", in_specs=[vm] * (4 * n), out_specs=[vm] * (3 * n),
        out_shape=[jax.ShapeDtypeStruct(a.shape, F32) for a in ws] * 3,
        compiler_params=pltpu.CompilerParams(vmem_limit_bytes=VMEM_LIMIT),
    )(*ws, *gs, *ms, *vs)
    return out[:n], out[n:2 * n], out[2 * n:]


def _where_am_i():
    x, y, c = lax.axis_index("x"), lax.axis_index("y"), lax.axis_index("c")
    chips = [(1 - x, y), (x, 1 - y), (1 - x, 1 - y)]
    return x, y, c, chips


def _chip_id(chip):
    return 2 * chip[0] + chip[1]


def _landing_shapes(shards):
    return [jax.ShapeDtypeStruct((3, a.shape[0] // 2, a.shape[1]), a.dtype) for a in shards]


def _ici_gather_copies(ins, lands, send_sems, recv_sems):
    _, _, c, chips = _where_am_i()
    copies = []
    for a, src in enumerate(ins):
        hr = src.shape[0] // 2
        for j, chip in enumerate(chips):
            copies.append(pltpu.make_async_remote_copy(
                src_ref=src.at[pl.ds(c * hr, hr)], dst_ref=lands[a].at[j], send_sem=send_sems.at[3 * a + j],
                recv_sem=recv_sems.at[3 * a + j], device_id=(*chip, c), device_id_type=MESH))
    return copies


def _ici_reduce_copies(s_ref, o_ref, send_sems, recv_sems, local_sem):
    x, y, c, chips = _where_am_i()
    copies = [pltpu.make_async_remote_copy(src_ref=s_ref.at[_chip_id(chip)], dst_ref=o_ref.at[j], send_sem=send_sems.at[j],
                                           recv_sem=recv_sems.at[j], device_id=(*chip, c), device_id_type=MESH)
              for j, chip in enumerate(chips)]
    return copies, pltpu.make_async_copy(s_ref.at[_chip_id((x, y))], o_ref.at[3], local_sem)


def _wait_all(copies):
    for cp in copies:
        cp.wait_recv()
    for cp in copies:
        cp.wait_send()


def _ici_gather(shards, small):
    n = len(shards)

    def body(*refs):
        begin_small, finish_small = _small_gather_steps(refs[n], refs[2 * n + 1], *refs[2 * n + 4:])
        copies = _ici_gather_copies(refs[:n], refs[n + 1:2 * n + 1], refs[2 * n + 2], refs[2 * n + 3])
        begin_small()
        for cp in copies:
            cp.start()
        finish_small()
        _wait_all(copies)

    vm = pl.BlockSpec(memory_space=pltpu.VMEM)
    out = pl.pallas_call(
        body, name="ici_gather", in_specs=[ANY] * n + [vm], out_specs=[ANY] * n + [vm],
        out_shape=_landing_shapes(shards) + [jax.ShapeDtypeStruct((N_DEV * small.shape[0], small.shape[1]), F32)],
        scratch_shapes=[pltpu.SemaphoreType.DMA((3 * n,))] * 2 + _small_gather_sems(),
    )(*shards, small)
    return list(out[:n]), out[n]


SLOTS = 4


def _sibling_gather(shards, lands):
    n = len(shards)
    d = shards[0].shape[1]
    halves = [a.shape[0] // 2 for a in shards]
    hmax = max(halves)
    chunks = [(a, j, halves[a]) for a in range(n) for j in range(3)]
    nc = len(chunks)

    def body(*refs):
        ins, lnd, outs = refs[:n], refs[n:2 * n], refs[2 * n:3 * n]
        sbuf, rbuf, obuf, ld_sems, take_sems, send_sems, recv_sems, place_sems, own_ld_sems, own_st_sems, credits = refs[3 * n:]
        x, y, c, chips = _where_am_i()
        sibling = (x, y, 1 - c)
        me = _chip_id((x, y))

        def load(i):
            a, j, hr = chunks[i]
            return pltpu.make_async_copy(lnd[a].at[j], sbuf.at[i % SLOTS, pl.ds(0, hr)], ld_sems.at[i % SLOTS])

        def push(i):
            hr, slot = chunks[i][2], i % SLOTS
            return pltpu.make_async_remote_copy(src_ref=sbuf.at[slot, pl.ds(0, hr)], dst_ref=rbuf.at[slot, pl.ds(0, hr)],
                                                send_sem=send_sems.at[slot], recv_sem=recv_sems.at[slot], device_id=sibling,
                                                device_id_type=MESH)

        def take(i):
            a, j, hr = chunks[i]
            return pltpu.make_async_copy(rbuf.at[i % SLOTS, pl.ds(0, hr)],
                                         outs[a].at[_chip_id(chips[j]), pl.ds((1 - c) * hr, hr)], take_sems.at[i % SLOTS])

        def place(i):
            a, j, hr = chunks[i]
            return pltpu.make_async_copy(sbuf.at[i % SLOTS, pl.ds(0, hr)], outs[a].at[_chip_id(chips[j]), pl.ds(c * hr, hr)],
                                         place_sems.at[i % SLOTS])

        own = [(a, h, halves[a]) for a in range(n) for h in range(2)]

        def own_load(k):
            a, h, hr = own[k]
            return pltpu.make_async_copy(ins[a].at[pl.ds(h * hr, hr)], obuf.at[k % 2, pl.ds(0, hr)], own_ld_sems.at[k % 2])

        def own_store(k):
            a, h, hr = own[k]
            return pltpu.make_async_copy(obuf.at[k % 2, pl.ds(0, hr)], outs[a].at[me, pl.ds(h * hr, hr)], own_st_sems.at[k % 2])

        def own_step(k):
            if k < len(own):
                if k >= 2:
                    own_store(k - 2).wait()
                own_load(k).start()
            if 1 <= k <= len(own):
                own_load(k - 1).wait()
                own_store(k - 1).start()

        for i in range(min(2, nc)):
            load(i).start()
        for i in range(nc):
            own_step(i)
            if i >= 2:
                push(i - 2).wait_send()
                place(i - 2).wait()
            if i + 2 < nc:
                load(i + 2).start()
            load(i).wait()
            place(i).start()
            if i >= SLOTS:
                pl.semaphore_wait(credits.at[i % SLOTS], 1)
            push(i).start()
            if i >= 1:
                push(i - 1).wait_recv()
                take(i - 1).start()
            if i >= 2:
                take(i - 2).wait()
                if i - 2 + SLOTS < nc:
                    pl.semaphore_signal(credits.at[(i - 2) % SLOTS], inc=1, device_id=sibling, device_id_type=MESH)
        for i in range(max(0, nc - 2), nc):
            push(i).wait_send()
            place(i).wait()
        push(nc - 1).wait_recv()
        take(nc - 1).start()
        for i in range(max(0, nc - 2), nc):
            take(i).wait()
        for k in range(nc, len(own) + 1):
            own_step(k)
        for k in range(max(0, len(own) - 2), len(own)):
            own_store(k).wait()

    dma = pltpu.SemaphoreType.DMA((SLOTS,))
    dma2 = pltpu.SemaphoreType.DMA((2,))
    return pl.pallas_call(
        body, name="sibling_gather", in_specs=[ANY] * (2 * n), out_specs=[ANY] * n,
        out_shape=[jax.ShapeDtypeStruct((N_CHIPS,) + a.shape, a.dtype) for a in shards],
        scratch_shapes=[pltpu.VMEM((SLOTS, hmax, d), BF16), pltpu.VMEM((SLOTS, hmax, d), BF16), pltpu.VMEM((2, hmax, d), BF16),
                        dma, dma, dma, dma, dma, dma2, dma2, pltpu.SemaphoreType.REGULAR((SLOTS,))],
        compiler_params=pltpu.CompilerParams(vmem_limit_bytes=VMEM_LIMIT),
    )(*shards, *lands)


def _pair_sum(grads):
    n = len(grads)
    halves = [g.shape[1] // 2 for g in grads]
    total, rows = sum(halves), max(halves)
    d = grads[0].shape[2]
    chunks, off = [], 0
    for a in range(n):
        chunks += [(a, k, off, halves[a]) for k in range(N_CHIPS)]
        off += halves[a]
    nc = len(chunks)

    def body(*refs):
        ins, out_ref = refs[:n], refs[n]
        sbuf, rbuf, mbuf, obuf, ls_sems, lm_sems, st_sems, send_sems, recv_sems, credits = refs[n + 1:]
        x, y, c, _ = _where_am_i()
        sibling = (x, y, 1 - c)

        def load_theirs(i):
            a, k, _, hr = chunks[i]
            return pltpu.make_async_copy(ins[a].at[k, pl.ds((1 - c) * hr, hr)], sbuf.at[i % SLOTS, pl.ds(0, hr)], ls_sems.at[i % SLOTS])

        def load_mine(i):
            a, k, _, hr = chunks[i]
            return pltpu.make_async_copy(ins[a].at[k, pl.ds(c * hr, hr)], mbuf.at[i % SLOTS, pl.ds(0, hr)], lm_sems.at[i % SLOTS])

        def push(i):
            hr, slot = chunks[i][3], i % SLOTS
            return pltpu.make_async_remote_copy(src_ref=sbuf.at[slot, pl.ds(0, hr)], dst_ref=rbuf.at[slot, pl.ds(0, hr)],
                                                send_sem=send_sems.at[slot], recv_sem=recv_sems.at[slot], device_id=sibling,
                                                device_id_type=MESH)

        def store(i):
            _, k, o, hr = chunks[i]
            slot = i % SLOTS
            return pltpu.make_async_copy(obuf.at[slot, pl.ds(0, hr)], out_ref.at[k, pl.ds(o, hr)], st_sems.at[slot])

        def start_push(i):
            load_theirs(i).wait()
            if i >= SLOTS:
                pl.semaphore_wait(credits.at[i % SLOTS], 1)
            push(i).start()

        for i in range(min(2, nc)):
            load_theirs(i).start()
            load_mine(i).start()
        start_push(0)
        for i in range(nc):
            hr, slot = chunks[i][3], i % SLOTS
            if i + 2 < nc:
                load_theirs(i + 2).start()
                load_mine(i + 2).start()
            if i + 1 < nc:
                start_push(i + 1)
            push(i).wait_recv()
            push(i).wait_send()
            load_mine(i).wait()
            if i >= SLOTS:
                store(i - SLOTS).wait()
            obuf[slot, 0:hr, :] = (mbuf[slot, 0:hr, :].astype(F32) + rbuf[slot, 0:hr, :].astype(F32)).astype(BF16)
            if i + SLOTS < nc:
                pl.semaphore_signal(credits.at[slot], inc=1, device_id=sibling, device_id_type=MESH)
            store(i).start()
        for i in range(max(0, nc - SLOTS), nc):
            store(i).wait()

    stage = pltpu.VMEM((SLOTS, rows, d), BF16)
    dma = pltpu.SemaphoreType.DMA((SLOTS,))
    return pl.pallas_call(
        body, name="pair_sum", in_specs=[ANY] * n, out_specs=ANY, out_shape=jax.ShapeDtypeStruct((N_CHIPS, total, d), BF16),
        scratch_shapes=[stage, stage, stage, stage, dma, dma, dma, dma, dma, pltpu.SemaphoreType.REGULAR((SLOTS,))],
        compiler_params=pltpu.CompilerParams(vmem_limit_bytes=VMEM_LIMIT),
    )(*grads)


def _reduce_chips(sums):
    def body(s_ref, o_ref, send_sems, recv_sems, local_sem):
        copies, local = _ici_reduce_copies(s_ref, o_ref, send_sems, recv_sems, local_sem)
        local.start()
        for cp in copies:
            cp.start()
        _wait_all(copies)
        local.wait()

    return pl.pallas_call(
        body, name="reduce_chips", in_specs=[ANY], out_specs=ANY, out_shape=jax.ShapeDtypeStruct(sums.shape, BF16),
        scratch_shapes=[pltpu.SemaphoreType.DMA((3,)), pltpu.SemaphoreType.DMA((3,)), pltpu.SemaphoreType.DMA],
    )(sums)


def _sum_share(parts, groups, rows, small):
    layers, n, n_g = len(parts), len(rows), len(groups)
    parts = [part for layer in parts for part in layer]
    n_l = len(parts)
    d = parts[0].shape[2]
    halves = [r // 2 for r in rows]
    hmax = max(halves)
    chunks = []
    for l in range(layers):
        for g, members in enumerate(groups):
            off = 0
            for a in members:
                chunks.append((l * n_g + g, a, off, halves[a], l))
                off += halves[a]
    nc = len(chunks)

    def body(*refs):
        ins, small_ref, outs, total_ref = refs[:n_l], refs[n_l], refs[n_l + 1:n_l + 1 + n], refs[n_l + 1 + n]
        pbuf, obuf, rbuf, ld_sems, keep_sems, take_sems, send_sems, recv_sems, credits, all_ref = refs[n_l + 2 + n:n_l + 12 + n]
        begin_small, finish_small = _small_gather_steps(small_ref, all_ref, *refs[n_l + 12 + n:])
        begin_small()
        x, y, c, _ = _where_am_i()
        sibling = (x, y, 1 - c)

        def load(i):
            part, _, off, hr, _ = chunks[i]
            return pltpu.make_async_copy(ins[part].at[:, pl.ds(off, hr)], pbuf.at[i % SLOTS, :, pl.ds(0, hr)], ld_sems.at[i % SLOTS])

        def keep(i):
            _, a, _, hr, l = chunks[i]
            return pltpu.make_async_copy(obuf.at[i % SLOTS, pl.ds(0, hr)], outs[a].at[l, pl.ds(c * hr, hr)], keep_sems.at[i % SLOTS])

        def push(i):
            hr, slot = chunks[i][3], i % SLOTS
            return pltpu.make_async_remote_copy(src_ref=obuf.at[slot, pl.ds(0, hr)], dst_ref=rbuf.at[slot, pl.ds(0, hr)],
                                                send_sem=send_sems.at[slot], recv_sem=recv_sems.at[slot], device_id=sibling,
                                                device_id_type=MESH)

        def take(i):
            _, a, _, hr, l = chunks[i]
            return pltpu.make_async_copy(rbuf.at[i % SLOTS, pl.ds(0, hr)], outs[a].at[l, pl.ds((1 - c) * hr, hr)],
                                         take_sems.at[i % SLOTS])

        for i in range(min(2, nc)):
            load(i).start()
        for i in range(nc):
            hr, slot = chunks[i][3], i % SLOTS
            if i + 2 < nc:
                load(i + 2).start()
            load(i).wait()
            if i >= SLOTS:
                keep(i - SLOTS).wait()
                push(i - SLOTS).wait_send()
            part = lambda k: pbuf[slot, k, 0:hr, :].astype(F32)
            obuf[slot, 0:hr, :] = ((part(3) + part(0)) + part(1)) + part(2)
            keep(i).start()
            if i >= SLOTS:
                pl.semaphore_wait(credits.at[slot], 1)
            push(i).start()
            if i >= 1:
                push(i - 1).wait_recv()
                take(i - 1).start()
            if i >= 2:
                take(i - 2).wait()
                if i - 2 + SLOTS < nc:
                    pl.semaphore_signal(credits.at[(i - 2) % SLOTS], inc=1, device_id=sibling, device_id_type=MESH)
        push(nc - 1).wait_recv()
        take(nc - 1).start()
        for i in range(max(0, nc - 2), nc):
            take(i).wait()
        for i in range(max(0, nc - SLOTS), nc):
            keep(i).wait()
            push(i).wait_send()
        finish_small()
        total_ref[...] = _sum_blocks(all_ref, small.shape[0])

    dma = pltpu.SemaphoreType.DMA((SLOTS,))
    vm = pl.BlockSpec(memory_space=pltpu.VMEM)
    out = pl.pallas_call(
        body, name="sum_share", in_specs=[ANY] * n_l + [vm], out_specs=[ANY] * n + [vm],
        out_shape=[jax.ShapeDtypeStruct((layers, r, d), F32) for r in rows] + [jax.ShapeDtypeStruct(small.shape, F32)],
        scratch_shapes=[pltpu.VMEM((SLOTS, N_CHIPS, hmax, d), BF16), pltpu.VMEM((SLOTS, hmax, d), F32),
                        pltpu.VMEM((SLOTS, hmax, d), F32), dma, dma, dma, dma, dma, pltpu.SemaphoreType.REGULAR((SLOTS,)),
                        pltpu.VMEM((N_DEV * small.shape[0], small.shape[1]), F32)] + _small_gather_sems(),
        compiler_params=pltpu.CompilerParams(vmem_limit_bytes=VMEM_LIMIT),
    )(*parts, small)
    return out[:n], out[n]


def _small_gather_sems():
    return [pltpu.SemaphoreType.DMA((7,)), pltpu.SemaphoreType.DMA((7,)), pltpu.SemaphoreType.DMA]


def _sum_blocks(all_ref, m):
    total = all_ref[0:m, :]
    for dev in range(1, N_DEV):
        total = total + all_ref[dev * m:(dev + 1) * m, :]
    return total


def _small_gather_steps(x_ref, all_ref, send_sems, recv_sems, local_sem):
    m = x_ref.shape[0]
    x, y, c, chips = _where_am_i()
    me, sibling = (x, y, c), (x, y, 1 - c)

    def rows(px, py, pc):
        return all_ref.at[pl.ds((4 * px + 2 * py + pc) * m, m), :]

    def copy(k, blk, to, src=None):
        return pltpu.make_async_remote_copy(src_ref=rows(*blk) if src is None else src, dst_ref=rows(*blk),
                                            send_sem=send_sems.at[k], recv_sem=recv_sems.at[k], device_id=to,
                                            device_id_type=MESH)

    def own():
        return pltpu.make_async_copy(x_ref, rows(*me), local_sem)

    def first():
        return [copy(0, me, sibling, src=x_ref)] + [copy(1 + j, me, (*chip, c), src=x_ref) for j, chip in enumerate(chips)]

    def passed():
        return [copy(4 + j, (*chip, c), sibling) for j, chip in enumerate(chips)]

    def begin():
        own().start()
        for cp in first():
            cp.start()

    def finish():
        forwards = passed()
        for j, chip in enumerate(chips):
            copy(1 + j, (*chip, c), me).wait_recv()
            forwards[j].start()
        copy(0, sibling, me).wait_recv()
        for j, chip in enumerate(chips):
            copy(4 + j, (*chip, 1 - c), me).wait_recv()
        for cp in first() + forwards:
            cp.wait_send()
        own().wait()

    return begin, finish


def _pack(arrays):
    flat = jnp.concatenate([a.reshape(-1) for a in arrays])
    pad = (-flat.shape[0]) % (8 * LANES)
    return jnp.pad(flat, (0, pad)).reshape(-1, LANES)


def _unpack(buf, shapes):
    flat = buf.reshape(-1)
    out, off = [], 0
    for shp in shapes:
        size = 1
        for dim in shp:
            size *= dim
        out.append(flat[off:off + size].reshape(shp))
        off += size
    return out


BIG = ("ffn1_w1", "ffn1_w3", "ffn1_w2", "w_in", "w_out", "ffn2_w1", "ffn2_w3", "ffn2_w2")
TRANSPOSED = ("ffn1_w1", "ffn1_w3", "w_in", "ffn2_w1", "ffn2_w3")
SMALL = ("ffn1_norm", "mix_norm", "conf_conv_w", "conf_conv_b", "conf_ln_g", "conf_ln_b", "sconv_w", "pool_w", "pool_scale",
         "gmlp_ln_g", "gmlp_ln_b", "gmlp_w_s", "gmlp_b_s", "ffn2_norm", "final_norm")
ORDER = ("ffn1_norm", "ffn1_w1", "ffn1_w3", "ffn1_w2", "mix_norm", "w_in", "conf_conv_w", "conf_conv_b", "conf_ln_g", "conf_ln_b",
         "sconv_w", "pool_w", "pool_scale", "gmlp_ln_g", "gmlp_ln_b", "gmlp_w_s", "gmlp_b_s", "w_out", "ffn2_norm", "ffn2_w1",
         "ffn2_w3", "ffn2_w2", "final_norm")


def _as2d(a):
    return a.reshape(-1, a.shape[-1])


def kernel(x, ffn1_norm, ffn1_w1, ffn1_w3, ffn1_w2, mix_norm, w_in, conf_conv_w, conf_conv_b, conf_ln_g, conf_ln_b, sconv_w, pool_w, pool_scale, gmlp_ln_g, gmlp_ln_b, gmlp_w_s, gmlp_b_s, w_out, ffn2_norm, ffn2_w1, ffn2_w3, ffn2_w2, final_norm, loss_target, m_ffn1_norm, m_ffn1_w1, m_ffn1_w3, m_ffn1_w2, m_mix_norm, m_w_in, m_conf_conv_w, m_conf_conv_b, m_conf_ln_g, m_conf_ln_b, m_sconv_w, m_pool_w, m_pool_scale, m_gmlp_ln_g, m_gmlp_ln_b, m_gmlp_w_s, m_gmlp_b_s, m_w_out, m_ffn2_norm, m_ffn2_w1, m_ffn2_w3, m_ffn2_w2, m_final_norm, v_ffn1_norm, v_ffn1_w1, v_ffn1_w3, v_ffn1_w2, v_mix_norm, v_w_in, v_conf_conv_w, v_conf_conv_b, v_conf_ln_g, v_conf_ln_b, v_sconv_w, v_pool_w, v_pool_scale, v_gmlp_ln_g, v_gmlp_ln_b, v_gmlp_w_s, v_gmlp_b_s, v_w_out, v_ffn2_norm, v_ffn2_w1, v_ffn2_w3, v_ffn2_w2, v_final_norm):
    given = dict(locals())
    w = {k: given[k] for k in ORDER}
    mom = {k: given["m_" + k] for k in ORDER}
    var = {k: given["v_" + k] for k in ORDER}
    n_l = ffn1_w1.shape[0]
    xs = x[0]
    d = xs.shape[1]
    chip = 2 * lax.axis_index("x") + lax.axis_index("y")

    def shard(name, l):
        a = w[name][l]
        return (jnp.swapaxes(w[name], 1, 2)[l] if name in TRANSPOSED else a).astype(BF16)

    groups = (BIG[:3], BIG[3:])

    def shards_of(l, g):
        return [shard(name, l) for name in groups[g]] if l < n_l else []

    def finish_gather(l, g, lands):
        out = _sibling_gather(shards_of(l, g), lands)
        return {name: a.reshape(-1, d) for name, a in zip(groups[g], out)}

    shard_rows = [w[name].shape[2] if name in TRANSPOSED else w[name].shape[1] for name in BIG]

    conv_shapes = [conf_conv_w.shape, sconv_w.shape]
    first_lands, conv_all = _ici_gather(shards_of(0, 0), _pack([conf_conv_w, sconv_w]))
    conv_all = conv_all.reshape(N_CHIPS, 2, -1)[:, 0]
    conf_full, sconv_full = [jnp.concatenate([_unpack(conv_all[k], conv_shapes)[a] for k in range(N_CHIPS)], axis=-1)
                             for a in range(2)]

    lane = jnp.arange(C) // HEAD_DIM
    head_rows = (jnp.arange(8)[:, None] == lane[None, :]).astype(F32)
    tril = jnp.tril(jnp.ones((CHUNK, CHUNK), F32))
    tril4 = jnp.tile(tril, (N_HEADS, 1))
    mixer_consts = []
    for l in range(n_l):
        cw = jnp.pad(conf_full[l], ((0, 32 - CONF_KERNEL), (0, 0)))
        vec = jnp.concatenate([conf_conv_b[l][None], conf_ln_g[l][None], conf_ln_b[l][None], pool_scale[l][None],
                               gmlp_ln_g[l][None], gmlp_ln_b[l][None], sconv_full[l], jnp.zeros((7, C), F32)], axis=0)
        eye = jnp.eye(len(pool_w[l]), dtype=F32)
        pool_blk = (eye[:, None, :, None] * pool_w[l][:, :, None, :]).reshape(C, C).astype(BF16)
        ws = gmlp_w_s[l] * tril[None]
        wstack = ws.reshape(N_HEADS * CHUNK, CHUNK).astype(BF16)
        wstack_t = jnp.swapaxes(ws, 1, 2).reshape(N_HEADS * CHUNK, CHUNK).astype(BF16)
        bias = jnp.repeat(gmlp_b_s[l].T, HEAD_DIM, axis=1)
        mixer_consts.append((cw, vec, pool_blk, wstack, wstack_t, bias))

    saved = []
    cur = xs
    gathered = [finish_gather(0, 0, first_lands)]
    for l in range(n_l):
        gw = gathered[l]
        cw, vec, pool_blk, wstack, wstack_t, bias = mixer_consts[l]
        x0 = cur
        x1, a1, b1, h1, lands = _ffn_fwd(x0, ffn1_norm[l][None], gw["ffn1_w1"], gw["ffn1_w3"], gw["ffn1_w2"], shards_of(l, 1))
        gw.update(finish_gather(l, 1, lands))
        p = _proj_fwd(x1, mix_norm[l][None], gw["w_in"])
        x2, mix, z = _mixer_fwd(p, x1, cw, vec, pool_blk, wstack, bias, gw["w_out"])
        x3, a2, b2, h2, lands = _ffn_fwd(x2, ffn2_norm[l][None], gw["ffn2_w1"], gw["ffn2_w3"], gw["ffn2_w2"], shards_of(l + 1, 0))
        if l + 1 < n_l:
            gathered.append(finish_gather(l + 1, 0, lands))
        saved.append((x0, x1, x2, a1, b1, a2, b2, p, mix, z, h1, h2))
        cur = x3

    dx, dg_final, loss_part = _loss_bwd(cur, final_norm[None], loss_target[0])

    small_parts = [None] * n_l
    reduced_halves = [[None, None] for _ in range(n_l)]
    pair_sum = lambda big, g: _pair_sum([big[name].reshape(N_CHIPS, -1, d) for name in groups[g]])
    pending = None
    for l in reversed(range(n_l)):
        gw = gathered[l]
        cw, vec, pool_blk, wstack, wstack_t, bias = mixer_consts[l]
        x0, x1, x2, a1, b1, a2, b2, p, mix, z, h1, h2 = saved[l]
        big = {}
        out = _ffn_bwd(dx, x2, ffn2_norm[l][None], a2, b2, gw["ffn2_w1"], gw["ffn2_w3"], gw["ffn2_w2"], pending)
        dx, dg_ffn2, dy, da, db, u, dx_bf = out[:7]
        if pending is not None:
            reduced_halves[l + 1][0] = out[7]
        big["ffn2_w1"], big["ffn2_w3"], big["ffn2_w2"] = _dw(da, h2), _dw(db, h2), _dw(u, dy)
        big["w_out"] = _dw(mix, dx_bf)
        dp, dcw, dvec, dpool, dws, dbs = _mixer_bwd(dx, p, z, cw, vec, pool_blk, wstack, wstack_t, bias, tril4, head_rows, gw["w_out"])
        dx, dg_mix, h = _proj_bwd(dx, x1, mix_norm[l][None], dp, gw["w_in"])
        big["w_in"] = _dw(dp, h)
        second = pair_sum(big, 1)
        out = _ffn_bwd(dx, x0, ffn1_norm[l][None], a1, b1, gw["ffn1_w1"], gw["ffn1_w3"], gw["ffn1_w2"], second)
        dx, dg_ffn1, dy, da, db, u = out[:6]
        reduced_halves[l][1] = out[7]
        big["ffn1_w1"], big["ffn1_w3"], big["ffn1_w2"] = _dw(da, h1), _dw(db, h1), _dw(u, dy)
        small_parts[l] = [dg_ffn1[0], dg_mix[0], dg_ffn2[0], dcw, dvec, dpool, dws, dbs]
        pending = pair_sum(big, 0)
    reduced_halves[0][0] = _reduce_chips(pending)
    grad_x = dx[None]

    part_shapes = [a.shape for a in small_parts[0]]
    tail = [dg_final[0], loss_part[0]]
    packed = _pack([a for l in range(n_l) for a in small_parts[l]] + tail)
    index_of = {name: a for a, name in enumerate(BIG)}
    full, summed = _sum_share(reduced_halves, [[index_of[name] for name in g] for g in groups], shard_rows, packed)
    full = dict(zip(BIG, full))
    grad = {}
    summed = _unpack(summed, part_shapes * n_l + [a.shape for a in tail])
    per_layer = [summed[l * len(part_shapes):(l + 1) * len(part_shapes)] for l in range(n_l)]
    stack = lambda k: jnp.stack([per_layer[l][k] for l in range(n_l)])
    dcw_all, dvec_all, dpool_all, dws_all, dbs_all = stack(3), stack(4), stack(5), stack(6), stack(7)
    loss = summed[-1][0]
    chip_cols = lambda a: lax.dynamic_slice_in_dim(a, chip * (C // N_CHIPS), C // N_CHIPS, axis=2)
    n_pool = pool_w.shape[1]
    grad.update(
        ffn1_norm=stack(0), mix_norm=stack(1), ffn2_norm=stack(2), final_norm=summed[-2],
        conf_conv_w=chip_cols(dcw_all[:, :CONF_KERNEL]), conf_conv_b=dvec_all[:, 0], conf_ln_g=dvec_all[:, 1],
        conf_ln_b=dvec_all[:, 2], pool_scale=dvec_all[:, 3], gmlp_ln_g=dvec_all[:, 4], gmlp_ln_b=dvec_all[:, 5],
        sconv_w=chip_cols(dvec_all[:, 6:6 + SHORT_KERNEL]),
        pool_w=jnp.stack([dpool_all[:, g * POOL_GROUP:(g + 1) * POOL_GROUP, g * POOL_GROUP:(g + 1) * POOL_GROUP]
                          for g in range(n_pool)], axis=1),
        gmlp_w_s=dws_all.reshape(n_l, N_HEADS, CHUNK, CHUNK), gmlp_b_s=dbs_all[:, :N_HEADS],
    )

    delta, new_m, new_v = {}, {}, {}
    for name in BIG:
        rows_of = (lambda a: jnp.swapaxes(a, 1, 2)) if name in TRANSPOSED else (lambda a: a)
        shp = full[name].shape
        out = _adamw(_as2d(rows_of(w[name])), _as2d(full[name]), _as2d(rows_of(mom[name])), _as2d(rows_of(var[name])))
        delta[name], new_m[name], new_v[name], grad[name] = (rows_of(o.reshape(shp)) for o in out)
    ds, ms, vs = _adamw_small([_as2d(w[k]) if w[k].ndim > 1 else w[k][None] for k in SMALL],
                              [_as2d(grad[k]) if grad[k].ndim > 1 else grad[k][None] for k in SMALL],
                              [_as2d(mom[k]) if mom[k].ndim > 1 else mom[k][None] for k in SMALL],
                              [_as2d(var[k]) if var[k].ndim > 1 else var[k][None] for k in SMALL])
    for k, dl, mo, vo in zip(SMALL, ds, ms, vs):
        delta[k], new_m[k], new_v[k] = dl.reshape(w[k].shape), mo.reshape(w[k].shape), vo.reshape(w[k].shape)

    return (loss, grad_x, *[grad[k] for k in ORDER], *[delta[k] for k in ORDER], *[new_m[k] for k in ORDER],
            *[new_v[k] for k in ORDER])
```

```python
import functools

import jax
import jax.numpy as jnp
from jax import lax
from jax.experimental import pallas as pl
from jax.experimental.pallas import tpu as pltpu

F32 = jnp.float32
BF16 = jnp.bfloat16
MESH = pl.DeviceIdType.MESH
ANY = pl.BlockSpec(memory_space=pl.ANY)

EPS = 1e-6
FFN_RESIDUAL = 0.5
D_GROUP = 256
CONF_KERNEL = 31
SHORT_KERNEL = 3
POOL_GROUP = 64
CHUNK = 128
N_HEADS = 4
HEAD_DIM = 64
HALO = 32
N_CHIPS = 4
N_DEV = 8
LANES = 128
MXU_TILE = 256
VMEM_LIMIT = 56 * 2**20
T_MIX_FWD = 1024
T_MIX_BWD = 1024

ADAM_LR = 0.001
ADAM_B1 = 0.9
ADAM_B2 = 0.999
ADAM_EPS = 1e-08
ADAM_WD = 0.01
ADAM_STEP = 10

NT = (((1,), (1,)), ((), ()))
TN = (((0,), (0,)), ((), ()))


def _params(*sem):
    return pltpu.CompilerParams(dimension_semantics=sem, vmem_limit_bytes=VMEM_LIMIT)


def _dot(a, b):
    return jnp.dot(a, b, preferred_element_type=F32)


def _dot_nt(a, b):
    return lax.dot_general(a, b, NT, preferred_element_type=F32)


def _t_bf16(v):
    return jnp.transpose(v).astype(BF16)


def _sigmoid(v):
    return 1.0 / (1.0 + jnp.exp(-v))


def _rms(x, g):
    r = lax.rsqrt(jnp.mean(x * x, axis=-1, keepdims=True) + EPS)
    n = x * r
    return n, r, n * g


def _rms_bwd(dh, n, r, g):
    dn = dh * g
    dx = r * (dn - n * jnp.mean(dn * n, axis=-1, keepdims=True))
    return dx, jnp.sum(dh * n, axis=0, keepdims=True)


def _ln_fwd(z, g, b):
    mu = jnp.mean(z, axis=-1, keepdims=True)
    zc = z - mu
    rs = lax.rsqrt(jnp.mean(zc * zc, axis=-1, keepdims=True) + EPS)
    zn = zc * rs
    return zn, rs, zn * g + b


def _ln_bwd(dl, zn, rs, g):
    dzn = dl * g
    dz = rs * (dzn - jnp.mean(dzn, axis=-1, keepdims=True) - zn * jnp.mean(dzn * zn, axis=-1, keepdims=True))
    return dz, jnp.sum(dl * zn, axis=0, keepdims=True), jnp.sum(dl, axis=0, keepdims=True)


def _tile(n, want):
    return want if n % want == 0 else n


def _resident(shape):
    return pl.BlockSpec(shape, lambda i: (0,) * len(shape), pipeline_mode=pl.Buffered(1))


def _ffn_fwd(x, g, w1t, w3t, w2, shards=()):
    s, d = x.shape
    f = w1t.shape[0]
    tm, tf = _tile(s, 512), MXU_TILE
    nj = f // tf
    ni = s // tm
    n_c = len(shards)

    def body(*refs):
        x_ref, g_ref, w1_ref, w3_ref, w2_ref = refs[:5]
        xo_ref, a_ref, b_ref, h_ref = refs[5 + n_c:9 + n_c]
        u_s = refs[9 + 2 * n_c]
        if n_c:
            gather = lambda: _ici_gather_copies(refs[5:5 + n_c], refs[9 + n_c:9 + 2 * n_c], refs[10 + 2 * n_c], refs[11 + 2 * n_c])

            @pl.when(pl.program_id(0) == 0)
            def _():
                for cp in gather():
                    cp.start()

        h = _rms(x_ref[...], g_ref[...])[2].astype(BF16)
        h_ref[...] = h
        for j in range(nj):
            cols = slice(j * tf, (j + 1) * tf)
            a = _dot_nt(h, w1_ref[cols, :])
            b = _dot_nt(h, w3_ref[cols, :])
            a_ref[:, cols] = a.astype(BF16)
            b_ref[:, cols] = b.astype(BF16)
            u_s[:, cols] = ((a * _sigmoid(a)) * b).astype(BF16)
        xo_ref[...] = x_ref[...] + FFN_RESIDUAL * _dot(u_s[...], w2_ref[...])
        if n_c:
            @pl.when(pl.program_id(0) == ni - 1)
            def _():
                _wait_all(gather())

    row = pl.BlockSpec((tm, d), lambda i: (i, 0))
    hid = pl.BlockSpec((tm, f), lambda i: (i, 0))
    sems = [pltpu.SemaphoreType.DMA((3 * n_c,))] * 2 if n_c else []
    out = pl.pallas_call(
        body, name="ffn_fwd_gather" if n_c else "ffn_fwd", grid=(ni,),
        in_specs=[row, _resident(g.shape), _resident(w1t.shape), _resident(w3t.shape), _resident(w2.shape)] + [ANY] * n_c,
        out_specs=[row, hid, hid, row] + [ANY] * n_c,
        out_shape=[jax.ShapeDtypeStruct((s, d), F32), jax.ShapeDtypeStruct((s, f), BF16), jax.ShapeDtypeStruct((s, f), BF16),
                   jax.ShapeDtypeStruct((s, d), BF16)] + _landing_shapes(shards),
        scratch_shapes=[pltpu.VMEM((tm, f), BF16)] + sems,
        compiler_params=_params("arbitrary"),
    )(x, g, w1t, w3t, w2, *shards)
    return out[0], out[1], out[2], out[3], list(out[4:])


def _ffn_bwd(dxo, x, g, a, b, w1t, w3t, w2, sums=None):
    s, d = x.shape
    f = w1t.shape[0]
    tm, tf = _tile(s, 256), MXU_TILE
    nj = f // tf
    ni = s // tm
    n_c = 0 if sums is None else 1

    def body(*refs):
        dxo_ref, x_ref, g_ref, a_ref, b_ref, w1_ref, w3_ref, w2_ref = refs[:8]
        dx_ref, dg_ref, dy_ref, da_ref, db_ref, u_ref, dxb_ref = refs[8 + n_c:15 + n_c]
        if n_c:
            exchange = lambda: _ici_reduce_copies(refs[8], refs[15 + n_c], *refs[15 + 2 * n_c:])

        @pl.when(pl.program_id(0) == 0)
        def _():
            dg_ref[...] = jnp.zeros_like(dg_ref)
            if n_c:
                copies, local = exchange()
                local.start()
                for cp in copies:
                    cp.start()

        dy = (FFN_RESIDUAL * dxo_ref[...]).astype(BF16)
        dy_ref[...] = dy
        for j in range(nj):
            cols = slice(j * tf, (j + 1) * tf)
            du = _dot_nt(dy, w2_ref[cols, :]).astype(BF16)
            av = a_ref[:, cols]
            bv = b_ref[:, cols]
            sg = _sigmoid(av.astype(F32)).astype(BF16)
            sl = av * sg
            da_ref[:, cols] = (du * bv) * (sg + sl * (1.0 - sg))
            db_ref[:, cols] = du * sl
            u_ref[:, cols] = sl * bv
        dh = _dot(da_ref[...], w1_ref[...]) + _dot(db_ref[...], w3_ref[...])
        n, r, _ = _rms(x_ref[...], g_ref[...])
        dxr, dg = _rms_bwd(dh, n, r, g_ref[...])
        dx = dxo_ref[...] + dxr
        dx_ref[...] = dx
        dxb_ref[...] = dx.astype(BF16)
        dg_ref[0:1, :] += dg
        if n_c:
            @pl.when(pl.program_id(0) == ni - 1)
            def _():
                copies, local = exchange()
                _wait_all(copies)
                local.wait()

    row = pl.BlockSpec((tm, d), lambda i: (i, 0))
    hid = pl.BlockSpec((tm, f), lambda i: (i, 0))
    extra = [] if sums is None else [sums]
    sems = [pltpu.SemaphoreType.DMA((3,)), pltpu.SemaphoreType.DMA((3,)), pltpu.SemaphoreType.DMA] if n_c else []
    out = pl.pallas_call(
        body, name="ffn_bwd_reduce" if n_c else "ffn_bwd", grid=(ni,),
        in_specs=[row, row, _resident(g.shape), hid, hid, _resident(w1t.shape), _resident(w3t.shape), _resident(w2.shape)]
        + [ANY] * n_c,
        out_specs=[row, pl.BlockSpec((8, d), lambda i: (0, 0)), row, hid, hid, hid, row] + [ANY] * n_c,
        out_shape=[jax.ShapeDtypeStruct((s, d), F32), jax.ShapeDtypeStruct((8, d), F32), jax.ShapeDtypeStruct((s, d), BF16),
                   jax.ShapeDtypeStruct((s, f), BF16), jax.ShapeDtypeStruct((s, f), BF16), jax.ShapeDtypeStruct((s, f), BF16),
                   jax.ShapeDtypeStruct((s, d), BF16)]
        + [jax.ShapeDtypeStruct(e.shape, e.dtype) for e in extra],
        scratch_shapes=sems,
        compiler_params=_params("arbitrary"),
    )(dxo, x, g, a, b, w1t, w3t, w2, *extra)
    return out


def _dw(am, bm):
    s, r = am.shape
    n = bm.shape[1]
    rb = r // 2 if r > 1024 and (r // 2) % LANES == 0 else r
    ts = _tile(s, 2048)
    ni = s // ts

    def body(am_ref, bm_ref, o_ref, acc_s):
        i = pl.program_id(1)

        @pl.when(i == 0)
        def _():
            acc_s[...] = jnp.zeros_like(acc_s)

        acc_s[...] += lax.dot_general(am_ref[...], bm_ref[...], TN, preferred_element_type=F32)

        @pl.when(i == ni - 1)
        def _():
            o_ref[...] = acc_s[...].astype(BF16)

    return pl.pallas_call(
        body, name="dw", grid=(r // rb, ni),
        in_specs=[pl.BlockSpec((ts, rb), lambda k, i: (i, k)), pl.BlockSpec((ts, n), lambda k, i: (i, 0))],
        out_specs=pl.BlockSpec((rb, n), lambda k, i: (k, 0)),
        out_shape=jax.ShapeDtypeStruct((r, n), BF16),
        scratch_shapes=[pltpu.VMEM((rb, n), F32)],
        compiler_params=_params("arbitrary", "arbitrary"),
    )(am, bm)


def _proj_fwd(x, g, w_int):
    s, d = x.shape
    f = w_int.shape[0]
    tm = _tile(s, 1024)

    def body(x_ref, g_ref, w_ref, p_ref):
        p_ref[...] = _dot_nt(_rms(x_ref[...], g_ref[...])[2].astype(BF16), w_ref[...])

    return pl.pallas_call(
        body, name="proj_fwd", grid=(s // tm,),
        in_specs=[pl.BlockSpec((tm, d), lambda i: (i, 0)), _resident(g.shape), _resident(w_int.shape)],
        out_specs=pl.BlockSpec((tm, f), lambda i: (i, 0)),
        out_shape=jax.ShapeDtypeStruct((s, f), F32),
        compiler_params=_params("arbitrary"),
    )(x, g, w_int)


def _proj_bwd(dxo, x, g, dp, w_int):
    s, d = x.shape
    f = w_int.shape[0]
    tm = _tile(s, 512)

    def body(dxo_ref, x_ref, g_ref, dp_ref, w_ref, dx_ref, dg_ref, h_ref):
        @pl.when(pl.program_id(0) == 0)
        def _():
            dg_ref[...] = jnp.zeros_like(dg_ref)

        n, r, h = _rms(x_ref[...], g_ref[...])
        h_ref[...] = h.astype(BF16)
        dxr, dg = _rms_bwd(_dot(dp_ref[...], w_ref[...]), n, r, g_ref[...])
        dx_ref[...] = dxo_ref[...] + dxr
        dg_ref[0:1, :] += dg

    row = pl.BlockSpec((tm, d), lambda i: (i, 0))
    return pl.pallas_call(
        body, name="proj_bwd", grid=(s // tm,),
        in_specs=[row, row, _resident(g.shape), pl.BlockSpec((tm, f), lambda i: (i, 0)), _resident(w_int.shape)],
        out_specs=[row, pl.BlockSpec((8, d), lambda i: (0, 0)), row],
        out_shape=[jax.ShapeDtypeStruct((s, d), F32), jax.ShapeDtypeStruct((8, d), F32), jax.ShapeDtypeStruct((s, d), BF16)],
        compiler_params=_params("arbitrary"),
    )(dxo, x, g, dp, w_int)


C = D_GROUP


def _piece(ref, k):
    return ref[:, k * C:(k + 1) * C]


def _up(v, r):
    return v if r == 0 else pltpu.roll(v, v.shape[0] - r, 0)


def _down(v, r):
    return v if r == 0 else pltpu.roll(v, r, 0)


def _lane_group():
    lane = lax.broadcasted_iota(jnp.int32, (1, C), 1)
    return (lane >= POOL_GROUP).astype(jnp.int32) + (lane >= 2 * POOL_GROUP).astype(jnp.int32) + (
        lane >= 3 * POOL_GROUP).astype(jnp.int32)


def _by_group(grp, v2, v4, v8, v16):
    return jnp.where(grp == 0, v2, jnp.where(grp == 1, v4, jnp.where(grp == 2, v8, v16)))


def _pool_count(grp, row0, t):
    pos = (row0 + lax.broadcasted_iota(jnp.int32, (t, C), 0) + 1).astype(F32)
    return jnp.minimum(pos, _by_group(grp, 2.0, 4.0, 8.0, 16.0))


def _trailing_sums(ext, grp, t):
    s2 = ext + _down(ext, 1)
    s4 = s2 + _down(s2, 2)
    s8 = s4 + _down(s4, 4)
    s16 = s8 + _down(s8, 8)
    return _by_group(grp, s2, s4, s8, s16)[HALO:HALO + t]


def _leading_sums(ext, grp, t):
    s2 = ext + _up(ext, 1)
    s4 = s2 + _up(s2, 2)
    s8 = s4 + _up(s4, 4)
    s16 = s8 + _up(s8, 8)
    return _by_group(grp, s2, s4, s8, s16)[0:t]


def _head_select(r4, grp):
    assert HEAD_DIM == POOL_GROUP and N_HEADS == 4
    return _by_group(grp, *(r4[h * CHUNK:(h + 1) * CHUNK] for h in range(N_HEADS)))


def _conv_taps():
    return [(k, (k + 2) % 8, (k + 2) - (k + 2) % 8) for k in range(CONF_KERNEL)]


def _mixer_fwd(p, x1, cw, vec, pool_w, wstack, bias, w_out):
    s, d = x1.shape
    t = _tile(s, T_MIX_FWD)
    n_ext = t + HALO
    dm = w_out.shape[0]

    def body(p_ref, x1_ref, cw_ref, vec_ref, pw_ref, ws_ref, bias_ref, wo_ref, x2_ref, mix_s, z_ref, cy_s, cq_s, cx_s):
        i = pl.program_id(0)

        @pl.when(i == 0)
        def _():
            cy_s[...] = jnp.zeros_like(cy_s)
            cq_s[...] = jnp.zeros_like(cq_s)
            cx_s[...] = jnp.zeros_like(cx_s)

        grp = _lane_group()
        y = _piece(p_ref, 0) * _sigmoid(_piece(p_ref, 1))
        ext = jnp.concatenate([cy_s[...], y], axis=0)
        cy_s[...] = y[t - HALO:t]
        z = jnp.broadcast_to(vec_ref[0:1, :], (t, C))
        shifted = {}
        for k, r, off in _conv_taps():
            if r not in shifted:
                shifted[r] = _up(ext, r)
            z = z + cw_ref[k:k + 1, :] * shifted[r][off:off + t]
        z_ref[...] = z
        ln = _ln_fwd(z, vec_ref[1:2, :], vec_ref[2:3, :])[2]
        mix_s[:, 0:C] = (ln * _sigmoid(ln)).astype(BF16)
        q = _piece(p_ref, 3) * _piece(p_ref, 4)
        ext = jnp.concatenate([cq_s[...], q], axis=0)
        cq_s[...] = q[t - HALO:t]
        cz = vec_ref[8:9, :] * q + vec_ref[7:8, :] * _down(ext, 1)[HALO:] + vec_ref[6:7, :] * _down(ext, 2)[HALO:]
        mix_s[:, C:2 * C] = (_piece(p_ref, 2) * cz).astype(BF16)
        xp = _piece(p_ref, 5)
        ext = jnp.concatenate([cx_s[...], xp], axis=0)
        cx_s[...] = xp[t - HALO:t]
        dd = _trailing_sums(ext, grp, t) / _pool_count(grp, i * t, t) - xp
        mix_s[:, 2 * C:3 * C] = (_dot(dd.astype(BF16), pw_ref[...]) * vec_ref[3:4, :]).astype(BF16)
        vln = _ln_fwd(_piece(p_ref, 7), vec_ref[4:5, :], vec_ref[5:6, :])[2].astype(BF16)
        for n in range(t // CHUNK):
            rows = slice(n * CHUNK, (n + 1) * CHUNK)
            mixed = _head_select(_dot(ws_ref[...], vln[rows]), grp) + bias_ref[...]
            mix_s[rows, 3 * C:4 * C] = (p_ref[rows, 6 * C:7 * C] * mixed).astype(BF16)
        x2_ref[...] = x1_ref[...] + _dot(mix_s[...], wo_ref[...])

    full = lambda a: pl.BlockSpec(a.shape, lambda i: (0, 0))
    return pl.pallas_call(
        body, name="mixer_fwd", grid=(s // t,),
        in_specs=[pl.BlockSpec((t, p.shape[1]), lambda i: (i, 0)), pl.BlockSpec((t, d), lambda i: (i, 0)),
                  full(cw), full(vec), full(pool_w), full(wstack), full(bias), full(w_out)],
        out_specs=[pl.BlockSpec((t, d), lambda i: (i, 0)), pl.BlockSpec((t, dm), lambda i: (i, 0)),
                   pl.BlockSpec((t, C), lambda i: (i, 0))],
        out_shape=[jax.ShapeDtypeStruct((s, d), F32), jax.ShapeDtypeStruct((s, dm), BF16), jax.ShapeDtypeStruct((s, C), F32)],
        scratch_shapes=[pltpu.VMEM((HALO, C), F32)] * 3,
        compiler_params=_params("arbitrary"),
    )(p, x1, cw, vec, pool_w, wstack, bias, w_out)


def _mixer_bwd(dx2, p, z, cw, vec, pool_w, wstack, wstack_t, bias, tril4, head_rows, w_out):
    s, d = dx2.shape
    t = _tile(s, T_MIX_BWD)
    nt = s // t
    n_ext = t + HALO
    hb = t // HALO

    def body(dx2_ref, p_ref, ph_ref, z_ref, cw_ref, vec_ref, pw_ref, ws_ref, wst_ref, bias_ref, tril_ref, hr_ref, wo_ref,
             dp_ref, dcw_ref, dvec_ref, dpool_ref, dws_ref, dbs_ref, cdz_s, cdc_s, cf_s, vy_s, dvl_s, dbias_s):
        i = pl.program_id(0)
        tile = nt - 1 - i

        @pl.when(i == 0)
        def _():
            for ref in (cdz_s, cdc_s, cf_s, dbias_s, dcw_ref, dvec_ref, dpool_ref, dws_ref, dbs_ref):
                ref[...] = jnp.zeros_like(ref)

        grp = _lane_group()
        first = jnp.where(tile > 0, 1.0, 0.0)
        dmix = _dot_nt(dx2_ref[...].astype(BF16), wo_ref[...])
        d_a, d_b, d_c, d_d = (dmix[:, k * C:(k + 1) * C] for k in range(4))

        def acc_vec(row, v):
            dvec_ref[row:row + 1, :] += jnp.sum(v, axis=0, keepdims=True)

        val, gate = _piece(p_ref, 0), _piece(p_ref, 1)
        sgate = _sigmoid(gate)
        y = val * sgate
        y_halo = ph_ref[:, 0:C] * _sigmoid(ph_ref[:, C:2 * C]) * first
        ext = jnp.concatenate([y_halo, y], axis=0)
        for r in range(8):
            vy_s[r] = _up(ext, r)
        zn, rs, ln = _ln_fwd(z_ref[...], vec_ref[1:2, :], vec_ref[2:3, :])
        sg = _sigmoid(ln)
        dln = d_a * (sg * (1.0 + ln * (1.0 - sg)))
        dz, dg, db = _ln_bwd(dln, zn, rs, vec_ref[1:2, :])
        dvec_ref[1:2, :] += dg
        dvec_ref[2:3, :] += db
        acc_vec(0, dz)
        for k, r, off in _conv_taps():
            dcw_ref[k:k + 1, :] += jnp.sum(dz * vy_s[r, off:off + t, :], axis=0, keepdims=True)
        ext = jnp.concatenate([dz, cdz_s[...]], axis=0)
        cdz_s[...] = dz[0:HALO]
        dy = jnp.zeros((t, C), F32)
        shifted = {}
        for k in range(CONF_KERNEL):
            m = CONF_KERNEL - 1 - k
            r, off = m % 8, m - m % 8
            if r not in shifted:
                shifted[r] = _up(ext, r)
            dy = dy + cw_ref[k:k + 1, :] * shifted[r][off:off + t]
        dp_ref[:, 0:C] = (dy * sgate).astype(BF16)
        dp_ref[:, C:2 * C] = (dy * val * sgate * (1.0 - sgate)).astype(BF16)

        sb, sc, sx = _piece(p_ref, 2), _piece(p_ref, 3), _piece(p_ref, 4)
        q = sc * sx
        q_halo = ph_ref[:, 3 * C:4 * C] * ph_ref[:, 4 * C:5 * C] * first
        ext = jnp.concatenate([q_halo, q], axis=0)
        q1, q2 = _down(ext, 1)[HALO:], _down(ext, 2)[HALO:]
        cz = vec_ref[8:9, :] * q + vec_ref[7:8, :] * q1 + vec_ref[6:7, :] * q2
        dcz = d_b * sb
        dp_ref[:, 2 * C:3 * C] = (d_b * cz).astype(BF16)
        acc_vec(8, dcz * q)
        acc_vec(7, dcz * q1)
        acc_vec(6, dcz * q2)
        ext = jnp.concatenate([dcz, cdc_s[...]], axis=0)
        cdc_s[...] = dcz[0:HALO]
        dq = vec_ref[8:9, :] * dcz + vec_ref[7:8, :] * _up(ext, 1)[0:t] + vec_ref[6:7, :] * _up(ext, 2)[0:t]
        dp_ref[:, 3 * C:4 * C] = (dq * sx).astype(BF16)
        dp_ref[:, 4 * C:5 * C] = (dq * sc).astype(BF16)

        xp = _piece(p_ref, 5)
        ext = jnp.concatenate([ph_ref[:, 5 * C:6 * C] * first, xp], axis=0)
        cnt = _pool_count(grp, tile * t, t)
        dd = (_trailing_sums(ext, grp, t) / cnt - xp).astype(BF16)
        e2 = _dot(dd, pw_ref[...])
        acc_vec(3, d_c * e2)
        de = (d_c * vec_ref[3:4, :]).astype(BF16)
        dpool_ref[...] += _dot(_t_bf16(dd.astype(F32)), de)
        ddd = _dot_nt(de, pw_ref[...])
        fq = ddd / cnt
        ext = jnp.concatenate([fq, cf_s[...]], axis=0)
        cf_s[...] = fq[0:HALO]
        dp_ref[:, 5 * C:6 * C] = (_leading_sums(ext, grp, t) - ddd).astype(BF16)

        vn, vrs, vlnf = _ln_fwd(_piece(p_ref, 7), vec_ref[4:5, :], vec_ref[5:6, :])
        vln = vlnf.astype(BF16)
        for n in range(t // CHUNK):
            rows = slice(n * CHUNK, (n + 1) * CHUNK)
            mixed = _head_select(_dot(ws_ref[...], vln[rows]), grp) + bias_ref[...]
            dd_n = d_d[rows]
            dp_ref[rows, 6 * C:7 * C] = (dd_n * mixed).astype(BF16)
            dmx = dd_n * p_ref[rows, 6 * C:7 * C]
            dbias_s[...] += dmx
            dmx_b = dmx.astype(BF16)
            dvl_s[rows, :] = _head_select(_dot(wst_ref[...], dmx_b), grp)
            for h in range(N_HEADS):
                hrows = slice(h * CHUNK, (h + 1) * CHUNK)
                dws_ref[hrows, :] += _dot_nt(jnp.where(grp == h, dmx_b, jnp.zeros_like(dmx_b)), vln[rows])
        dvl = dvl_s[...]
        dv, dg, db = _ln_bwd(dvl, vn, vrs, vec_ref[4:5, :])
        dvec_ref[4:5, :] += dg
        dvec_ref[5:6, :] += db
        dp_ref[:, 7 * C:8 * C] = dv.astype(BF16)

        @pl.when(i == nt - 1)
        def _():
            dws_ref[...] = dws_ref[...] * tril_ref[...]
            dbs_ref[...] = lax.dot_general(hr_ref[...], dbias_s[...], NT, precision=lax.Precision.HIGHEST,
                                           preferred_element_type=F32)

    full = lambda a: pl.BlockSpec(a.shape, lambda i: (0, 0))
    acc = lambda shape: pl.BlockSpec(shape, lambda i: (0, 0))
    f = p.shape[1]
    return pl.pallas_call(
        body, name="mixer_bwd", grid=(nt,),
        in_specs=[pl.BlockSpec((t, d), lambda i: (nt - 1 - i, 0)), pl.BlockSpec((t, f), lambda i: (nt - 1 - i, 0)),
                  pl.BlockSpec((HALO, f), lambda i: (jnp.maximum((nt - 1 - i) * hb - 1, 0), 0)),
                  pl.BlockSpec((t, C), lambda i: (nt - 1 - i, 0)), full(cw), full(vec), full(pool_w), full(wstack), full(wstack_t), full(bias), full(tril4), full(head_rows),
                  full(w_out)],
        out_specs=[pl.BlockSpec((t, f), lambda i: (nt - 1 - i, 0)), acc((32, C)), acc((16, C)), acc((C, C)),
                   acc((N_HEADS * CHUNK, CHUNK)), acc((8, CHUNK))],
        out_shape=[jax.ShapeDtypeStruct((s, f), BF16), jax.ShapeDtypeStruct((32, C), F32), jax.ShapeDtypeStruct((16, C), F32),
                   jax.ShapeDtypeStruct((C, C), F32), jax.ShapeDtypeStruct((N_HEADS * CHUNK, CHUNK), F32),
                   jax.ShapeDtypeStruct((8, CHUNK), F32)],
        scratch_shapes=[pltpu.VMEM((HALO, C), F32)] * 3 + [pltpu.VMEM((8, n_ext, C), F32), pltpu.VMEM((t, C), F32),
                                                            pltpu.VMEM((CHUNK, C), F32)],
        compiler_params=_params("arbitrary"),
    )(dx2, p, p, z, cw, vec, pool_w, wstack, wstack_t, bias, tril4, head_rows, w_out)


def _loss_bwd(x, g, target):
    s, d = x.shape
    tm = _tile(s, 1024)

    def body(x_ref, g_ref, t_ref, dx_ref, dg_ref, loss_ref):
        @pl.when(pl.program_id(0) == 0)
        def _():
            dg_ref[...] = jnp.zeros_like(dg_ref)
            loss_ref[...] = jnp.zeros_like(loss_ref)

        n, r, y = _rms(x_ref[...], g_ref[...])
        err = y - t_ref[...]
        loss_ref[...] += 0.5 * jnp.sum(jnp.mean(err * err, axis=-1, keepdims=True), axis=0, keepdims=True)
        dxr, dg = _rms_bwd(err * (1.0 / d), n, r, g_ref[...])
        dx_ref[...] = dxr
        dg_ref[0:1, :] += dg

    row = pl.BlockSpec((tm, d), lambda i: (i, 0))
    return pl.pallas_call(
        body, name="loss_bwd", grid=(s // tm,),
        in_specs=[row, pl.BlockSpec((1, d), lambda i: (0, 0)), row],
        out_specs=[row, pl.BlockSpec((8, d), lambda i: (0, 0)), pl.BlockSpec((8, LANES), lambda i: (0, 0))],
        out_shape=[jax.ShapeDtypeStruct((s, d), F32), jax.ShapeDtypeStruct((8, d), F32), jax.ShapeDtypeStruct((8, LANES), F32)],
        compiler_params=_params("arbitrary"),
    )(x, g, target)


def _adamw_math(w, g, m, v):
    m = ADAM_B1 * m + (1.0 - ADAM_B1) * g
    v = ADAM_B2 * v + (1.0 - ADAM_B2) * (g * g)
    m_hat = m / (1.0 - ADAM_B1 ** ADAM_STEP)
    v_hat = v / (1.0 - ADAM_B2 ** ADAM_STEP)
    return -ADAM_LR * (m_hat / (jnp.sqrt(v_hat) + ADAM_EPS) + ADAM_WD * w), m, v


def _adamw(w, g, m, v):
    r, c = w.shape
    tr = r // 8 if r % 64 == 0 else r

    def body(w_ref, g_ref, m_ref, v_ref, d_ref, mo_ref, vo_ref, go_ref):
        gv = g_ref[...]
        d_ref[...], mo_ref[...], vo_ref[...] = _adamw_math(w_ref[...], gv, m_ref[...], v_ref[...])
        go_ref[...] = gv

    blk = pl.BlockSpec((tr, c), lambda i: (i, 0))
    return pl.pallas_call(
        body, name="adamw", grid=(r // tr,), in_specs=[blk] * 4, out_specs=[blk] * 4,
        out_shape=[jax.ShapeDtypeStruct((r, c), F32)] * 4, compiler_params=_params("arbitrary"),
    )(w, g, m, v)


def _adamw_small(ws, gs, ms, vs):
    n = len(ws)

    def body(*refs):
        ins, outs = refs[:4 * n], refs[4 * n:]
        for k in range(n):
            dl, mo, vo = _adamw_math(ins[k][...], ins[n + k][...], ins[2 * n + k][...], ins[3 * n + k][...])
            outs[k][...], outs[n + k][...], outs[2 * n + k][...] = dl, mo, vo

    vm = pl.BlockSpec(memory_space=pltpu.VMEM)
    out = pl.pallas_call(
        body, name="adamw_small", in_specs=[vm] * (4 * n), out_specs=[vm] * (3 * n),
        out_shape=[jax.ShapeDtypeStruct(a.shape, F32) for a in ws] * 3,
        compiler_params=pltpu.CompilerParams(vmem_limit_bytes=VMEM_LIMIT),
    )(*ws, *gs, *ms, *vs)
    return out[:n], out[n:2 * n], out[2 * n:]


def _where_am_i():
    x, y, c = lax.axis_index("x"), lax.axis_index("y"), lax.axis_index("c")
    chips = [(1 - x, y), (x, 1 - y), (1 - x, 1 - y)]
    return x, y, c, chips


def _chip_id(chip):
    return 2 * chip[0] + chip[1]


def _landing_shapes(shards):
    return [jax.ShapeDtypeStruct((3, a.shape[0] // 2, a.shape[1]), a.dtype) for a in shards]


def _ici_gather_copies(ins, lands, send_sems, recv_sems):
    _, _, c, chips = _where_am_i()
    copies = []
    for a, src in enumerate(ins):
        hr = src.shape[0] // 2
        for j, chip in enumerate(chips):
            copies.append(pltpu.make_async_remote_copy(
                src_ref=src.at[pl.ds(c * hr, hr)], dst_ref=lands[a].at[j], send_sem=send_sems.at[3 * a + j],
                recv_sem=recv_sems.at[3 * a + j], device_id=(*chip, c), device_id_type=MESH))
    return copies


def _ici_reduce_copies(s_ref, o_ref, send_sems, recv_sems, local_sem):
    x, y, c, chips = _where_am_i()
    copies = [pltpu.make_async_remote_copy(src_ref=s_ref.at[_chip_id(chip)], dst_ref=o_ref.at[j], send_sem=send_sems.at[j],
                                           recv_sem=recv_sems.at[j], device_id=(*chip, c), device_id_type=MESH)
              for j, chip in enumerate(chips)]
    return copies, pltpu.make_async_copy(s_ref.at[_chip_id((x, y))], o_ref.at[3], local_sem)


def _wait_all(copies):
    for cp in copies:
        cp.wait_recv()
    for cp in copies:
        cp.wait_send()


def _ici_gather(shards, small):
    n = len(shards)

    def body(*refs):
        begin_small, finish_small = _small_gather_steps(refs[n], refs[2 * n + 1], *refs[2 * n + 4:])
        copies = _ici_gather_copies(refs[:n], refs[n + 1:2 * n + 1], refs[2 * n + 2], refs[2 * n + 3])
        begin_small()
        for cp in copies:
            cp.start()
        finish_small()
        _wait_all(copies)

    vm = pl.BlockSpec(memory_space=pltpu.VMEM)
    out = pl.pallas_call(
        body, name="ici_gather", in_specs=[ANY] * n + [vm], out_specs=[ANY] * n + [vm],
        out_shape=_landing_shapes(shards) + [jax.ShapeDtypeStruct((N_DEV * small.shape[0], small.shape[1]), F32)],
        scratch_shapes=[pltpu.SemaphoreType.DMA((3 * n,))] * 2 + _small_gather_sems(),
    )(*shards, small)
    return list(out[:n]), out[n]


SLOTS = 4


def _sibling_gather(shards, lands):
    n = len(shards)
    d = shards[0].shape[1]
    halves = [a.shape[0] // 2 for a in shards]
    hmax = max(halves)
    chunks = [(a, j, halves[a]) for a in range(n) for j in range(3)]
    nc = len(chunks)

    def body(*refs):
        ins, lnd, outs = refs[:n], refs[n:2 * n], refs[2 * n:3 * n]
        sbuf, rbuf, obuf, ld_sems, take_sems, send_sems, recv_sems, place_sems, own_ld_sems, own_st_sems, credits = refs[3 * n:]
        x, y, c, chips = _where_am_i()
        sibling = (x, y, 1 - c)
        me = _chip_id((x, y))

        def load(i):
            a, j, hr = chunks[i]
            return pltpu.make_async_copy(lnd[a].at[j], sbuf.at[i % SLOTS, pl.ds(0, hr)], ld_sems.at[i % SLOTS])

        def push(i):
            hr, slot = chunks[i][2], i % SLOTS
            return pltpu.make_async_remote_copy(src_ref=sbuf.at[slot, pl.ds(0, hr)], dst_ref=rbuf.at[slot, pl.ds(0, hr)],
                                                send_sem=send_sems.at[slot], recv_sem=recv_sems.at[slot], device_id=sibling,
                                                device_id_type=MESH)

        def take(i):
            a, j, hr = chunks[i]
            return pltpu.make_async_copy(rbuf.at[i % SLOTS, pl.ds(0, hr)],
                                         outs[a].at[_chip_id(chips[j]), pl.ds((1 - c) * hr, hr)], take_sems.at[i % SLOTS])

        def place(i):
            a, j, hr = chunks[i]
            return pltpu.make_async_copy(sbuf.at[i % SLOTS, pl.ds(0, hr)], outs[a].at[_chip_id(chips[j]), pl.ds(c * hr, hr)],
                                         place_sems.at[i % SLOTS])

        own = [(a, h, halves[a]) for a in range(n) for h in range(2)]

        def own_load(k):
            a, h, hr = own[k]
            return pltpu.make_async_copy(ins[a].at[pl.ds(h * hr, hr)], obuf.at[k % 2, pl.ds(0, hr)], own_ld_sems.at[k % 2])

        def own_store(k):
            a, h, hr = own[k]
            return pltpu.make_async_copy(obuf.at[k % 2, pl.ds(0, hr)], outs[a].at[me, pl.ds(h * hr, hr)], own_st_sems.at[k % 2])

        def own_step(k):
            if k < len(own):
                if k >= 2:
                    own_store(k - 2).wait()
                own_load(k).start()
            if 1 <= k <= len(own):
                own_load(k - 1).wait()
                own_store(k - 1).start()

        for i in range(min(2, nc)):
            load(i).start()
        for i in range(nc):
            own_step(i)
            if i >= 2:
                push(i - 2).wait_send()
                place(i - 2).wait()
            if i + 2 < nc:
                load(i + 2).start()
            load(i).wait()
            place(i).start()
            if i >= SLOTS:
                pl.semaphore_wait(credits.at[i % SLOTS], 1)
            push(i).start()
            if i >= 1:
                push(i - 1).wait_recv()
                take(i - 1).start()
            if i >= 2:
                take(i - 2).wait()
                if i - 2 + SLOTS < nc:
                    pl.semaphore_signal(credits.at[(i - 2) % SLOTS], inc=1, device_id=sibling, device_id_type=MESH)
        for i in range(max(0, nc - 2), nc):
            push(i).wait_send()
            place(i).wait()
        push(nc - 1).wait_recv()
        take(nc - 1).start()
        for i in range(max(0, nc - 2), nc):
            take(i).wait()
        for k in range(nc, len(own) + 1):
            own_step(k)
        for k in range(max(0, len(own) - 2), len(own)):
            own_store(k).wait()

    dma = pltpu.SemaphoreType.DMA((SLOTS,))
    dma2 = pltpu.SemaphoreType.DMA((2,))
    return pl.pallas_call(
        body, name="sibling_gather", in_specs=[ANY] * (2 * n), out_specs=[ANY] * n,
        out_shape=[jax.ShapeDtypeStruct((N_CHIPS,) + a.shape, a.dtype) for a in shards],
        scratch_shapes=[pltpu.VMEM((SLOTS, hmax, d), BF16), pltpu.VMEM((SLOTS, hmax, d), BF16), pltpu.VMEM((2, hmax, d), BF16),
                        dma, dma, dma, dma, dma, dma2, dma2, pltpu.SemaphoreType.REGULAR((SLOTS,))],
        compiler_params=pltpu.CompilerParams(vmem_limit_bytes=VMEM_LIMIT),
    )(*shards, *lands)


def _pair_sum(grads):
    n = len(grads)
    halves = [g.shape[1] // 2 for g in grads]
    total, rows = sum(halves), max(halves)
    d = grads[0].shape[2]
    chunks, off = [], 0
    for a in range(n):
        chunks += [(a, k, off, halves[a]) for k in range(N_CHIPS)]
        off += halves[a]
    nc = len(chunks)

    def body(*refs):
        ins, out_ref = refs[:n], refs[n]
        sbuf, rbuf, mbuf, obuf, ls_sems, lm_sems, st_sems, send_sems, recv_sems, credits = refs[n + 1:]
        x, y, c, _ = _where_am_i()
        sibling = (x, y, 1 - c)

        def load_theirs(i):
            a, k, _, hr = chunks[i]
            return pltpu.make_async_copy(ins[a].at[k, pl.ds((1 - c) * hr, hr)], sbuf.at[i % SLOTS, pl.ds(0, hr)], ls_sems.at[i % SLOTS])

        def load_mine(i):
            a, k, _, hr = chunks[i]
            return pltpu.make_async_copy(ins[a].at[k, pl.ds(c * hr, hr)], mbuf.at[i % SLOTS, pl.ds(0, hr)], lm_sems.at[i % SLOTS])

        def push(i):
            hr, slot = chunks[i][3], i % SLOTS
            return pltpu.make_async_remote_copy(src_ref=sbuf.at[slot, pl.ds(0, hr)], dst_ref=rbuf.at[slot, pl.ds(0, hr)],
                                                send_sem=send_sems.at[slot], recv_sem=recv_sems.at[slot], device_id=sibling,
                                                device_id_type=MESH)

        def store(i):
            _, k, o, hr = chunks[i]
            slot = i % SLOTS
            return pltpu.make_async_copy(obuf.at[slot, pl.ds(0, hr)], out_ref.at[k, pl.ds(o, hr)], st_sems.at[slot])

        def start_push(i):
            load_theirs(i).wait()
            if i >= SLOTS:
                pl.semaphore_wait(credits.at[i % SLOTS], 1)
            push(i).start()

        for i in range(min(2, nc)):
            load_theirs(i).start()
            load_mine(i).start()
        start_push(0)
        for i in range(nc):
            hr, slot = chunks[i][3], i % SLOTS
            if i + 2 < nc:
                load_theirs(i + 2).start()
                load_mine(i + 2).start()
            if i + 1 < nc:
                start_push(i + 1)
            push(i).wait_recv()
            push(i).wait_send()
            load_mine(i).wait()
            if i >= SLOTS:
                store(i - SLOTS).wait()
            obuf[slot, 0:hr, :] = (mbuf[slot, 0:hr, :].astype(F32) + rbuf[slot, 0:hr, :].astype(F32)).astype(BF16)
            if i + SLOTS < nc:
                pl.semaphore_signal(credits.at[slot], inc=1, device_id=sibling, device_id_type=MESH)
            store(i).start()
        for i in range(max(0, nc - SLOTS), nc):
            store(i).wait()

    stage = pltpu.VMEM((SLOTS, rows, d), BF16)
    dma = pltpu.SemaphoreType.DMA((SLOTS,))
    return pl.pallas_call(
        body, name="pair_sum", in_specs=[ANY] * n, out_specs=ANY, out_shape=jax.ShapeDtypeStruct((N_CHIPS, total, d), BF16),
        scratch_shapes=[stage, stage, stage, stage, dma, dma, dma, dma, dma, pltpu.SemaphoreType.REGULAR((SLOTS,))],
        compiler_params=pltpu.CompilerParams(vmem_limit_bytes=VMEM_LIMIT),
    )(*grads)


def _reduce_chips(sums):
    def body(s_ref, o_ref, send_sems, recv_sems, local_sem):
        copies, local = _ici_reduce_copies(s_ref, o_ref, send_sems, recv_sems, local_sem)
        local.start()
        for cp in copies:
            cp.start()
        _wait_all(copies)
        local.wait()

    return pl.pallas_call(
        body, name="reduce_chips", in_specs=[ANY], out_specs=ANY, out_shape=jax.ShapeDtypeStruct(sums.shape, BF16),
        scratch_shapes=[pltpu.SemaphoreType.DMA((3,)), pltpu.SemaphoreType.DMA((3,)), pltpu.SemaphoreType.DMA],
    )(sums)


def _sum_share(parts, groups, rows, small):
    layers, n, n_g = len(parts), len(rows), len(groups)
    parts = [part for layer in parts for part in layer]
    n_l = len(parts)
    d = parts[0].shape[2]
    halves = [r // 2 for r in rows]
    hmax = max(halves)
    chunks = []
    for l in range(layers):
        for g, members in enumerate(groups):
            off = 0
            for a in members:
                chunks.append((l * n_g + g, a, off, halves[a], l))
                off += halves[a]
    nc = len(chunks)

    def body(*refs):
        ins, small_ref, outs, total_ref = refs[:n_l], refs[n_l], refs[n_l + 1:n_l + 1 + n], refs[n_l + 1 + n]
        pbuf, obuf, rbuf, ld_sems, keep_sems, take_sems, send_sems, recv_sems, credits, all_ref = refs[n_l + 2 + n:n_l + 12 + n]
        begin_small, finish_small = _small_gather_steps(small_ref, all_ref, *refs[n_l + 12 + n:])
        begin_small()
        x, y, c, _ = _where_am_i()
        sibling = (x, y, 1 - c)

        def load(i):
            part, _, off, hr, _ = chunks[i]
            return pltpu.make_async_copy(ins[part].at[:, pl.ds(off, hr)], pbuf.at[i % SLOTS, :, pl.ds(0, hr)], ld_sems.at[i % SLOTS])

        def keep(i):
            _, a, _, hr, l = chunks[i]
            return pltpu.make_async_copy(obuf.at[i % SLOTS, pl.ds(0, hr)], outs[a].at[l, pl.ds(c * hr, hr)], keep_sems.at[i % SLOTS])

        def push(i):
            hr, slot = chunks[i][3], i % SLOTS
            return pltpu.make_async_remote_copy(src_ref=obuf.at[slot, pl.ds(0, hr)], dst_ref=rbuf.at[slot, pl.ds(0, hr)],
                                                send_sem=send_sems.at[slot], recv_sem=recv_sems.at[slot], device_id=sibling,
                                                device_id_type=MESH)

        def take(i):
            _, a, _, hr, l = chunks[i]
            return pltpu.make_async_copy(rbuf.at[i % SLOTS, pl.ds(0, hr)], outs[a].at[l, pl.ds((1 - c) * hr, hr)],
                                         take_sems.at[i % SLOTS])

        for i in range(min(2, nc)):
            load(i).start()
        for i in range(nc):
            hr, slot = chunks[i][3], i % SLOTS
            if i + 2 < nc:
                load(i + 2).start()
            load(i).wait()
            if i >= SLOTS:
                keep(i - SLOTS).wait()
                push(i - SLOTS).wait_send()
            part = lambda k: pbuf[slot, k, 0:hr, :].astype(F32)
            obuf[slot, 0:hr, :] = ((part(3) + part(0)) + part(1)) + part(2)
            keep(i).start()
            if i >= SLOTS:
                pl.semaphore_wait(credits.at[slot], 1)
            push(i).start()
            if i >= 1:
                push(i - 1).wait_recv()
                take(i - 1).start()
            if i >= 2:
                take(i - 2).wait()
                if i - 2 + SLOTS < nc:
                    pl.semaphore_signal(credits.at[(i - 2) % SLOTS], inc=1, device_id=sibling, device_id_type=MESH)
        push(nc - 1).wait_recv()
        take(nc - 1).start()
        for i in range(max(0, nc - 2), nc):
            take(i).wait()
        for i in range(max(0, nc - SLOTS), nc):
            keep(i).wait()
            push(i).wait_send()
        finish_small()
        total_ref[...] = _sum_blocks(all_ref, small.shape[0])

    dma = pltpu.SemaphoreType.DMA((SLOTS,))
    vm = pl.BlockSpec(memory_space=pltpu.VMEM)
    out = pl.pallas_call(
        body, name="sum_share", in_specs=[ANY] * n_l + [vm], out_specs=[ANY] * n + [vm],
        out_shape=[jax.ShapeDtypeStruct((layers, r, d), F32) for r in rows] + [jax.ShapeDtypeStruct(small.shape, F32)],
        scratch_shapes=[pltpu.VMEM((SLOTS, N_CHIPS, hmax, d), BF16), pltpu.VMEM((SLOTS, hmax, d), F32),
                        pltpu.VMEM((SLOTS, hmax, d), F32), dma, dma, dma, dma, dma, pltpu.SemaphoreType.REGULAR((SLOTS,)),
                        pltpu.VMEM((N_DEV * small.shape[0], small.shape[1]), F32)] + _small_gather_sems(),
        compiler_params=pltpu.CompilerParams(vmem_limit_bytes=VMEM_LIMIT),
    )(*parts, small)
    return out[:n], out[n]


def _small_gather_sems():
    return [pltpu.SemaphoreType.DMA((7,)), pltpu.SemaphoreType.DMA((7,)), pltpu.SemaphoreType.DMA]


def _sum_blocks(all_ref, m):
    total = all_ref[0:m, :]
    for dev in range(1, N_DEV):
        total = total + all_ref[dev * m:(dev + 1) * m, :]
    return total


def _small_gather_steps(x_ref, all_ref, send_sems, recv_sems, local_sem):
    m = x_ref.shape[0]
    x, y, c, chips = _where_am_i()
    me, sibling = (x, y, c), (x, y, 1 - c)

    def rows(px, py, pc):
        return all_ref.at[pl.ds((4 * px + 2 * py + pc) * m, m), :]

    def copy(k, blk, to, src=None):
        return pltpu.make_async_remote_copy(src_ref=rows(*blk) if src is None else src, dst_ref=rows(*blk),
                                            send_sem=send_sems.at[k], recv_sem=recv_sems.at[k], device_id=to,
                                            device_id_type=MESH)

    def own():
        return pltpu.make_async_copy(x_ref, rows(*me), local_sem)

    def first():
        return [copy(0, me, sibling, src=x_ref)] + [copy(1 + j, me, (*chip, c), src=x_ref) for j, chip in enumerate(chips)]

    def passed():
        return [copy(4 + j, (*chip, c), sibling) for j, chip in enumerate(chips)]

    def begin():
        own().start()
        for cp in first():
            cp.start()

    def finish():
        forwards = passed()
        for j, chip in enumerate(chips):
            copy(1 + j, (*chip, c), me).wait_recv()
            forwards[j].start()
        copy(0, sibling, me).wait_recv()
        for j, chip in enumerate(chips):
            copy(4 + j, (*chip, 1 - c), me).wait_recv()
        for cp in first() + forwards:
            cp.wait_send()
        own().wait()

    return begin, finish


def _pack(arrays):
    flat = jnp.concatenate([a.reshape(-1) for a in arrays])
    pad = (-flat.shape[0]) % (8 * LANES)
    return jnp.pad(flat, (0, pad)).reshape(-1, LANES)


def _unpack(buf, shapes):
    flat = buf.reshape(-1)
    out, off = [], 0
    for shp in shapes:
        size = 1
        for dim in shp:
            size *= dim
        out.append(flat[off:off + size].reshape(shp))
        off += size
    return out


BIG = ("ffn1_w1", "ffn1_w3", "ffn1_w2", "w_in", "w_out", "ffn2_w1", "ffn2_w3", "ffn2_w2")
TRANSPOSED = ("ffn1_w1", "ffn1_w3", "w_in", "ffn2_w1", "ffn2_w3")
SMALL = ("ffn1_norm", "mix_norm", "conf_conv_w", "conf_conv_b", "conf_ln_g", "conf_ln_b", "sconv_w", "pool_w", "pool_scale",
         "gmlp_ln_g", "gmlp_ln_b", "gmlp_w_s", "gmlp_b_s", "ffn2_norm", "final_norm")
ORDER = ("ffn1_norm", "ffn1_w1", "ffn1_w3", "ffn1_w2", "mix_norm", "w_in", "conf_conv_w", "conf_conv_b", "conf_ln_g", "conf_ln_b",
         "sconv_w", "pool_w", "pool_scale", "gmlp_ln_g", "gmlp_ln_b", "gmlp_w_s", "gmlp_b_s", "w_out", "ffn2_norm", "ffn2_w1",
         "ffn2_w3", "ffn2_w2", "final_norm")


def _as2d(a):
    return a.reshape(-1, a.shape[-1])


def kernel(x, ffn1_norm, ffn1_w1, ffn1_w3, ffn1_w2, mix_norm, w_in, conf_conv_w, conf_conv_b, conf_ln_g, conf_ln_b, sconv_w, pool_w, pool_scale, gmlp_ln_g, gmlp_ln_b, gmlp_w_s, gmlp_b_s, w_out, ffn2_norm, ffn2_w1, ffn2_w3, ffn2_w2, final_norm, loss_target, m_ffn1_norm, m_ffn1_w1, m_ffn1_w3, m_ffn1_w2, m_mix_norm, m_w_in, m_conf_conv_w, m_conf_conv_b, m_conf_ln_g, m_conf_ln_b, m_sconv_w, m_pool_w, m_pool_scale, m_gmlp_ln_g, m_gmlp_ln_b, m_gmlp_w_s, m_gmlp_b_s, m_w_out, m_ffn2_norm, m_ffn2_w1, m_ffn2_w3, m_ffn2_w2, m_final_norm, v_ffn1_norm, v_ffn1_w1, v_ffn1_w3, v_ffn1_w2, v_mix_norm, v_w_in, v_conf_conv_w, v_conf_conv_b, v_conf_ln_g, v_conf_ln_b, v_sconv_w, v_pool_w, v_pool_scale, v_gmlp_ln_g, v_gmlp_ln_b, v_gmlp_w_s, v_gmlp_b_s, v_w_out, v_ffn2_norm, v_ffn2_w1, v_ffn2_w3, v_ffn2_w2, v_final_norm):
    given = dict(locals())
    w = {k: given[k] for k in ORDER}
    mom = {k: given["m_" + k] for k in ORDER}
    var = {k: given["v_" + k] for k in ORDER}
    n_l = ffn1_w1.shape[0]
    xs = x[0]
    d = xs.shape[1]
    chip = 2 * lax.axis_index("x") + lax.axis_index("y")

    def shard(name, l):
        a = w[name][l]
        return (jnp.swapaxes(w[name], 1, 2)[l] if name in TRANSPOSED else a).astype(BF16)

    groups = (BIG[:3], BIG[3:])

    def shards_of(l, g):
        return [shard(name, l) for name in groups[g]] if l < n_l else []

    def finish_gather(l, g, lands):
        out = _sibling_gather(shards_of(l, g), lands)
        return {name: a.reshape(-1, d) for name, a in zip(groups[g], out)}

    shard_rows = [w[name].shape[2] if name in TRANSPOSED else w[name].shape[1] for name in BIG]

    conv_shapes = [conf_conv_w.shape, sconv_w.shape]
    first_lands, conv_all = _ici_gather(shards_of(0, 0), _pack([conf_conv_w, sconv_w]))
    conv_all = conv_all.reshape(N_CHIPS, 2, -1)[:, 0]
    conf_full, sconv_full = [jnp.concatenate([_unpack(conv_all[k], conv_shapes)[a] for k in range(N_CHIPS)], axis=-1)
                             for a in range(2)]

    lane = jnp.arange(C) // HEAD_DIM
    head_rows = (jnp.arange(8)[:, None] == lane[None, :]).astype(F32)
    tril = jnp.tril(jnp.ones((CHUNK, CHUNK), F32))
    tril4 = jnp.tile(tril, (N_HEADS, 1))
    mixer_consts = []
    for l in range(n_l):
        cw = jnp.pad(conf_full[l], ((0, 32 - CONF_KERNEL), (0, 0)))
        vec = jnp.concatenate([conf_conv_b[l][None], conf_ln_g[l][None], conf_ln_b[l][None], pool_scale[l][None],
                               gmlp_ln_g[l][None], gmlp_ln_b[l][None], sconv_full[l], jnp.zeros((7, C), F32)], axis=0)
        eye = jnp.eye(len(pool_w[l]), dtype=F32)
        pool_blk = (eye[:, None, :, None] * pool_w[l][:, :, None, :]).reshape(C, C).astype(BF16)
        ws = gmlp_w_s[l] * tril[None]
        wstack = ws.reshape(N_HEADS * CHUNK, CHUNK).astype(BF16)
        wstack_t = jnp.swapaxes(ws, 1, 2).reshape(N_HEADS * CHUNK, CHUNK).astype(BF16)
        bias = jnp.repeat(gmlp_b_s[l].T, HEAD_DIM, axis=1)
        mixer_consts.append((cw, vec, pool_blk, wstack, wstack_t, bias))

    saved = []
    cur = xs
    gathered = [finish_gather(0, 0, first_lands)]
    for l in range(n_l):
        gw = gathered[l]
        cw, vec, pool_blk, wstack, wstack_t, bias = mixer_consts[l]
        x0 = cur
        x1, a1, b1, h1, lands = _ffn_fwd(x0, ffn1_norm[l][None], gw["ffn1_w1"], gw["ffn1_w3"], gw["ffn1_w2"], shards_of(l, 1))
        gw.update(finish_gather(l, 1, lands))
        p = _proj_fwd(x1, mix_norm[l][None], gw["w_in"])
        x2, mix, z = _mixer_fwd(p, x1, cw, vec, pool_blk, wstack, bias, gw["w_out"])
        x3, a2, b2, h2, lands = _ffn_fwd(x2, ffn2_norm[l][None], gw["ffn2_w1"], gw["ffn2_w3"], gw["ffn2_w2"], shards_of(l + 1, 0))
        if l + 1 < n_l:
            gathered.append(finish_gather(l + 1, 0, lands))
        saved.append((x0, x1, x2, a1, b1, a2, b2, p, mix, z, h1, h2))
        cur = x3

    dx, dg_final, loss_part = _loss_bwd(cur, final_norm[None], loss_target[0])

    small_parts = [None] * n_l
    reduced_halves = [[None, None] for _ in range(n_l)]
    pair_sum = lambda big, g: _pair_sum([big[name].reshape(N_CHIPS, -1, d) for name in groups[g]])
    pending = None
    for l in reversed(range(n_l)):
        gw = gathered[l]
        cw, vec, pool_blk, wstack, wstack_t, bias = mixer_consts[l]
        x0, x1, x2, a1, b1, a2, b2, p, mix, z, h1, h2 = saved[l]
        big = {}
        out = _ffn_bwd(dx, x2, ffn2_norm[l][None], a2, b2, gw["ffn2_w1"], gw["ffn2_w3"], gw["ffn2_w2"], pending)
        dx, dg_ffn2, dy, da, db, u, dx_bf = out[:7]
        if pending is not None:
            reduced_halves[l + 1][0] = out[7]
        big["ffn2_w1"], big["ffn2_w3"], big["ffn2_w2"] = _dw(da, h2), _dw(db, h2), _dw(u, dy)
        big["w_out"] = _dw(mix, dx_bf)
        dp, dcw, dvec, dpool, dws, dbs = _mixer_bwd(dx, p, z, cw, vec, pool_blk, wstack, wstack_t, bias, tril4, head_rows, gw["w_out"])
        dx, dg_mix, h = _proj_bwd(dx, x1, mix_norm[l][None], dp, gw["w_in"])
        big["w_in"] = _dw(dp, h)
        second = pair_sum(big, 1)
        out = _ffn_bwd(dx, x0, ffn1_norm[l][None], a1, b1, gw["ffn1_w1"], gw["ffn1_w3"], gw["ffn1_w2"], second)
        dx, dg_ffn1, dy, da, db, u = out[:6]
        reduced_halves[l][1] = out[7]
        big["ffn1_w1"], big["ffn1_w3"], big["ffn1_w2"] = _dw(da, h1), _dw(db, h1), _dw(u, dy)
        small_parts[l] = [dg_ffn1[0], dg_mix[0], dg_ffn2[0], dcw, dvec, dpool, dws, dbs]
        pending = pair_sum(big, 0)
    reduced_halves[0][0] = _reduce_chips(pending)
    grad_x = dx[None]

    part_shapes = [a.shape for a in small_parts[0]]
    tail = [dg_final[0], loss_part[0]]
    packed = _pack([a for l in range(n_l) for a in small_parts[l]] + tail)
    index_of = {name: a for a, name in enumerate(BIG)}
    full, summed = _sum_share(reduced_halves, [[index_of[name] for name in g] for g in groups], shard_rows, packed)
    full = dict(zip(BIG, full))
    grad = {}
    summed = _unpack(summed, part_shapes * n_l + [a.shape for a in tail])
    per_layer = [summed[l * len(part_shapes):(l + 1) * len(part_shapes)] for l in range(n_l)]
    stack = lambda k: jnp.stack([per_layer[l][k] for l in range(n_l)])
    dcw_all, dvec_all, dpool_all, dws_all, dbs_all = stack(3), stack(4), stack(5), stack(6), stack(7)
    loss = summed[-1][0]
    chip_cols = lambda a: lax.dynamic_slice_in_dim(a, chip * (C // N_CHIPS), C // N_CHIPS, axis=2)
    n_pool = pool_w.shape[1]
    grad.update(
        ffn1_norm=stack(0), mix_norm=stack(1), ffn2_norm=stack(2), final_norm=summed[-2],
        conf_conv_w=chip_cols(dcw_all[:, :CONF_KERNEL]), conf_conv_b=dvec_all[:, 0], conf_ln_g=dvec_all[:, 1],
        conf_ln_b=dvec_all[:, 2], pool_scale=dvec_all[:, 3], gmlp_ln_g=dvec_all[:, 4], gmlp_ln_b=dvec_all[:, 5],
        sconv_w=chip_cols(dvec_all[:, 6:6 + SHORT_KERNEL]),
        pool_w=jnp.stack([dpool_all[:, g * POOL_GROUP:(g + 1) * POOL_GROUP, g * POOL_GROUP:(g + 1) * POOL_GROUP]
                          for g in range(n_pool)], axis=1),
        gmlp_w_s=dws_all.reshape(n_l, N_HEADS, CHUNK, CHUNK), gmlp_b_s=dbs_all[:, :N_HEADS],
    )

    delta, new_m, new_v = {}, {}, {}
    for name in BIG:
        rows_of = (lambda a: jnp.swapaxes(a, 1, 2)) if name in TRANSPOSED else (lambda a: a)
        shp = full[name].shape
        out = _adamw(_as2d(rows_of(w[name])), _as2d(full[name]), _as2d(rows_of(mom[name])), _as2d(rows_of(var[name])))
        delta[name], new_m[name], new_v[name], grad[name] = (rows_of(o.reshape(shp)) for o in out)
    ds, ms, vs = _adamw_small([_as2d(w[k]) if w[k].ndim > 1 else w[k][None] for k in SMALL],
                              [_as2d(grad[k]) if grad[k].ndim > 1 else grad[k][None] for k in SMALL],
                              [_as2d(mom[k]) if mom[k].ndim > 1 else mom[k][None] for k in SMALL],
                              [_as2d(var[k]) if var[k].ndim > 1 else var[k][None] for k in SMALL])
    for k, dl, mo, vo in zip(SMALL, ds, ms, vs):
        delta[k], new_m[k], new_v[k] = dl.reshape(w[k].shape), mo.reshape(w[k].shape), vo.reshape(w[k].shape)

    return (loss, grad_x, *[grad[k] for k in ORDER], *[delta[k] for k in ORDER], *[new_m[k] for k in ORDER],
            *[new_v[k] for k in ORDER])
```

```python
import functools

import jax
import jax.numpy as jnp
from jax import lax
from jax.experimental import pallas as pl
from jax.experimental.pallas import tpu as pltpu

F32 = jnp.float32
BF16 = jnp.bfloat16
MESH = pl.DeviceIdType.MESH
ANY = pl.BlockSpec(memory_space=pl.ANY)

EPS = 1e-6
FFN_RESIDUAL = 0.5
D_GROUP = 256
CONF_KERNEL = 31
SHORT_KERNEL = 3
POOL_GROUP = 64
CHUNK = 128
N_HEADS = 4
HEAD_DIM = 64
HALO = 32
N_CHIPS = 4
N_DEV = 8
LANES = 128
MXU_TILE = 256
VMEM_LIMIT = 56 * 2**20
T_MIX_FWD = 1024
T_MIX_BWD = 1024

ADAM_LR = 0.001
ADAM_B1 = 0.9
ADAM_B2 = 0.999
ADAM_EPS = 1e-08
ADAM_WD = 0.01
ADAM_STEP = 10

NT = (((1,), (1,)), ((), ()))
TN = (((0,), (0,)), ((), ()))


def _params(*sem):
    return pltpu.CompilerParams(dimension_semantics=sem, vmem_limit_bytes=VMEM_LIMIT)


def _dot(a, b):
    return jnp.dot(a, b, preferred_element_type=F32)


def _dot_nt(a, b):
    return lax.dot_general(a, b, NT, preferred_element_type=F32)


def _t_bf16(v):
    return jnp.transpose(v).astype(BF16)


def _sigmoid(v):
    return 1.0 / (1.0 + jnp.exp(-v))


def _rms(x, g):
    r = lax.rsqrt(jnp.mean(x * x, axis=-1, keepdims=True) + EPS)
    n = x * r
    return n, r, n * g


def _rms_bwd(dh, n, r, g):
    dn = dh * g
    dx = r * (dn - n * jnp.mean(dn * n, axis=-1, keepdims=True))
    return dx, jnp.sum(dh * n, axis=0, keepdims=True)


def _ln_fwd(z, g, b):
    mu = jnp.mean(z, axis=-1, keepdims=True)
    zc = z - mu
    rs = lax.rsqrt(jnp.mean(zc * zc, axis=-1, keepdims=True) + EPS)
    zn = zc * rs
    return zn, rs, zn * g + b


def _ln_bwd(dl, zn, rs, g):
    dzn = dl * g
    dz = rs * (dzn - jnp.mean(dzn, axis=-1, keepdims=True) - zn * jnp.mean(dzn * zn, axis=-1, keepdims=True))
    return dz, jnp.sum(dl * zn, axis=0, keepdims=True), jnp.sum(dl, axis=0, keepdims=True)


def _tile(n, want):
    return want if n % want == 0 else n


def _resident(shape):
    return pl.BlockSpec(shape, lambda i: (0,) * len(shape), pipeline_mode=pl.Buffered(1))


def _ffn_fwd(x, g, w1t, w3t, w2, shards=()):
    s, d = x.shape
    f = w1t.shape[0]
    tm, tf = _tile(s, 512), MXU_TILE
    nj = f // tf
    ni = s // tm
    n_c = len(shards)

    def body(*refs):
        x_ref, g_ref, w1_ref, w3_ref, w2_ref = refs[:5]
        xo_ref, a_ref, b_ref, h_ref = refs[5 + n_c:9 + n_c]
        u_s = refs[9 + 2 * n_c]
        if n_c:
            gather = lambda: _ici_gather_copies(refs[5:5 + n_c], refs[9 + n_c:9 + 2 * n_c], refs[10 + 2 * n_c], refs[11 + 2 * n_c])

            @pl.when(pl.program_id(0) == 0)
            def _():
                for cp in gather():
                    cp.start()

        h = _rms(x_ref[...], g_ref[...])[2].astype(BF16)
        h_ref[...] = h
        for j in range(nj):
            cols = slice(j * tf, (j + 1) * tf)
            a = _dot_nt(h, w1_ref[cols, :])
            b = _dot_nt(h, w3_ref[cols, :])
            a_ref[:, cols] = a.astype(BF16)
            b_ref[:, cols] = b.astype(BF16)
            u_s[:, cols] = ((a * _sigmoid(a)) * b).astype(BF16)
        xo_ref[...] = x_ref[...] + FFN_RESIDUAL * _dot(u_s[...], w2_ref[...])
        if n_c:
            @pl.when(pl.program_id(0) == ni - 1)
            def _():
                _wait_all(gather())

    row = pl.BlockSpec((tm, d), lambda i: (i, 0))
    hid = pl.BlockSpec((tm, f), lambda i: (i, 0))
    sems = [pltpu.SemaphoreType.DMA((3 * n_c,))] * 2 if n_c else []
    out = pl.pallas_call(
        body, name="ffn_fwd_gather" if n_c else "ffn_fwd", grid=(ni,),
        in_specs=[row, _resident(g.shape), _resident(w1t.shape), _resident(w3t.shape), _resident(w2.shape)] + [ANY] * n_c,
        out_specs=[row, hid, hid, row] + [ANY] * n_c,
        out_shape=[jax.ShapeDtypeStruct((s, d), F32), jax.ShapeDtypeStruct((s, f), BF16), jax.ShapeDtypeStruct((s, f), BF16),
                   jax.ShapeDtypeStruct((s, d), BF16)] + _landing_shapes(shards),
        scratch_shapes=[pltpu.VMEM((tm, f), BF16)] + sems,
        compiler_params=_params("arbitrary"),
    )(x, g, w1t, w3t, w2, *shards)
    return out[0], out[1], out[2], out[3], list(out[4:])


def _ffn_bwd(dxo, x, g, a, b, w1t, w3t, w2, sums=None):
    s, d = x.shape
    f = w1t.shape[0]
    tm, tf = _tile(s, 256), MXU_TILE
    nj = f // tf
    ni = s // tm
    n_c = 0 if sums is None else 1

    def body(*refs):
        dxo_ref, x_ref, g_ref, a_ref, b_ref, w1_ref, w3_ref, w2_ref = refs[:8]
        dx_ref, dg_ref, dy_ref, da_ref, db_ref, u_ref, dxb_ref = refs[8 + n_c:15 + n_c]
        if n_c:
            exchange = lambda: _ici_reduce_copies(refs[8], refs[15 + n_c], *refs[15 + 2 * n_c:])

        @pl.when(pl.program_id(0) == 0)
        def _():
            dg_ref[...] = jnp.zeros_like(dg_ref)
            if n_c:
                copies, local = exchange()
                local.start()
                for cp in copies:
                    cp.start()

        dy = (FFN_RESIDUAL * dxo_ref[...]).astype(BF16)
        dy_ref[...] = dy
        for j in range(nj):
            cols = slice(j * tf, (j + 1) * tf)
            du = _dot_nt(dy, w2_ref[cols, :]).astype(BF16)
            av = a_ref[:, cols]
            bv = b_ref[:, cols]
            sg = _sigmoid(av.astype(F32)).astype(BF16)
            sl = av * sg
            da_ref[:, cols] = (du * bv) * (sg + sl * (1.0 - sg))
            db_ref[:, cols] = du * sl
            u_ref[:, cols] = sl * bv
        dh = _dot(da_ref[...], w1_ref[...]) + _dot(db_ref[...], w3_ref[...])
        n, r, _ = _rms(x_ref[...], g_ref[...])
        dxr, dg = _rms_bwd(dh, n, r, g_ref[...])
        dx = dxo_ref[...] + dxr
        dx_ref[...] = dx
        dxb_ref[...] = dx.astype(BF16)
        dg_ref[0:1, :] += dg
        if n_c:
            @pl.when(pl.program_id(0) == ni - 1)
            def _():
                copies, local = exchange()
                _wait_all(copies)
                local.wait()

    row = pl.BlockSpec((tm, d), lambda i: (i, 0))
    hid = pl.BlockSpec((tm, f), lambda i: (i, 0))
    extra = [] if sums is None else [sums]
    sems = [pltpu.SemaphoreType.DMA((3,)), pltpu.SemaphoreType.DMA((3,)), pltpu.SemaphoreType.DMA] if n_c else []
    out = pl.pallas_call(
        body, name="ffn_bwd_reduce" if n_c else "ffn_bwd", grid=(ni,),
        in_specs=[row, row, _resident(g.shape), hid, hid, _resident(w1t.shape), _resident(w3t.shape), _resident(w2.shape)]
        + [ANY] * n_c,
        out_specs=[row, pl.BlockSpec((8, d), lambda i: (0, 0)), row, hid, hid, hid, row] + [ANY] * n_c,
        out_shape=[jax.ShapeDtypeStruct((s, d), F32), jax.ShapeDtypeStruct((8, d), F32), jax.ShapeDtypeStruct((s, d), BF16),
                   jax.ShapeDtypeStruct((s, f), BF16), jax.ShapeDtypeStruct((s, f), BF16), jax.ShapeDtypeStruct((s, f), BF16),
                   jax.ShapeDtypeStruct((s, d), BF16)]
        + [jax.ShapeDtypeStruct(e.shape, e.dtype) for e in extra],
        scratch_shapes=sems,
        compiler_params=_params("arbitrary"),
    )(dxo, x, g, a, b, w1t, w3t, w2, *extra)
    return out


def _dw(am, bm):
    s, r = am.shape
    n = bm.shape[1]
    rb = r // 2 if r > 1024 and (r // 2) % LANES == 0 else r
    ts = _tile(s, 2048)
    ni = s // ts

    def body(am_ref, bm_ref, o_ref, acc_s):
        i = pl.program_id(1)

        @pl.when(i == 0)
        def _():
            acc_s[...] = jnp.zeros_like(acc_s)

        acc_s[...] += lax.dot_general(am_ref[...], bm_ref[...], TN, preferred_element_type=F32)

        @pl.when(i == ni - 1)
        def _():
            o_ref[...] = acc_s[...].astype(BF16)

    return pl.pallas_call(
        body, name="dw", grid=(r // rb, ni),
        in_specs=[pl.BlockSpec((ts, rb), lambda k, i: (i, k)), pl.BlockSpec((ts, n), lambda k, i: (i, 0))],
        out_specs=pl.BlockSpec((rb, n), lambda k, i: (k, 0)),
        out_shape=jax.ShapeDtypeStruct((r, n), BF16),
        scratch_shapes=[pltpu.VMEM((rb, n), F32)],
        compiler_params=_params("arbitrary", "arbitrary"),
    )(am, bm)


def _proj_fwd(x, g, w_int):
    s, d = x.shape
    f = w_int.shape[0]
    tm = _tile(s, 1024)

    def body(x_ref, g_ref, w_ref, p_ref):
        p_ref[...] = _dot_nt(_rms(x_ref[...], g_ref[...])[2].astype(BF16), w_ref[...])

    return pl.pallas_call(
        body, name="proj_fwd", grid=(s // tm,),
        in_specs=[pl.BlockSpec((tm, d), lambda i: (i, 0)), _resident(g.shape), _resident(w_int.shape)],
        out_specs=pl.BlockSpec((tm, f), lambda i: (i, 0)),
        out_shape=jax.ShapeDtypeStruct((s, f), F32),
        compiler_params=_params("arbitrary"),
    )(x, g, w_int)


def _proj_bwd(dxo, x, g, dp, w_int):
    s, d = x.shape
    f = w_int.shape[0]
    tm = _tile(s, 1024)

    def body(dxo_ref, x_ref, g_ref, dp_ref, w_ref, dx_ref, dg_ref, h_ref):
        @pl.when(pl.program_id(0) == 0)
        def _():
            dg_ref[...] = jnp.zeros_like(dg_ref)

        n, r, h = _rms(x_ref[...], g_ref[...])
        h_ref[...] = h.astype(BF16)
        dxr, dg = _rms_bwd(_dot(dp_ref[...], w_ref[...]), n, r, g_ref[...])
        dx_ref[...] = dxo_ref[...] + dxr
        dg_ref[0:1, :] += dg

    row = pl.BlockSpec((tm, d), lambda i: (i, 0))
    return pl.pallas_call(
        body, name="proj_bwd", grid=(s // tm,),
        in_specs=[row, row, _resident(g.shape), pl.BlockSpec((tm, f), lambda i: (i, 0)), _resident(w_int.shape)],
        out_specs=[row, pl.BlockSpec((8, d), lambda i: (0, 0)), row],
        out_shape=[jax.ShapeDtypeStruct((s, d), F32), jax.ShapeDtypeStruct((8, d), F32), jax.ShapeDtypeStruct((s, d), BF16)],
        compiler_params=_params("arbitrary"),
    )(dxo, x, g, dp, w_int)


C = D_GROUP


def _piece(ref, k):
    return ref[:, k * C:(k + 1) * C]


def _up(v, r):
    return v if r == 0 else pltpu.roll(v, v.shape[0] - r, 0)


def _down(v, r):
    return v if r == 0 else pltpu.roll(v, r, 0)


def _lane_group():
    lane = lax.broadcasted_iota(jnp.int32, (1, C), 1)
    return (lane >= POOL_GROUP).astype(jnp.int32) + (lane >= 2 * POOL_GROUP).astype(jnp.int32) + (
        lane >= 3 * POOL_GROUP).astype(jnp.int32)


def _by_group(grp, v2, v4, v8, v16):
    return jnp.where(grp == 0, v2, jnp.where(grp == 1, v4, jnp.where(grp == 2, v8, v16)))


def _pool_count(grp, row0, t):
    pos = (row0 + lax.broadcasted_iota(jnp.int32, (t, C), 0) + 1).astype(F32)
    return jnp.minimum(pos, _by_group(grp, 2.0, 4.0, 8.0, 16.0))


def _trailing_sums(ext, grp, t):
    s2 = ext + _down(ext, 1)
    s4 = s2 + _down(s2, 2)
    s8 = s4 + _down(s4, 4)
    s16 = s8 + _down(s8, 8)
    return _by_group(grp, s2, s4, s8, s16)[HALO:HALO + t]


def _leading_sums(ext, grp, t):
    s2 = ext + _up(ext, 1)
    s4 = s2 + _up(s2, 2)
    s8 = s4 + _up(s4, 4)
    s16 = s8 + _up(s8, 8)
    return _by_group(grp, s2, s4, s8, s16)[0:t]


def _head_select(r4, grp):
    assert HEAD_DIM == POOL_GROUP and N_HEADS == 4
    return _by_group(grp, *(r4[h * CHUNK:(h + 1) * CHUNK] for h in range(N_HEADS)))


def _conv_taps():
    return [(k, (k + 2) % 8, (k + 2) - (k + 2) % 8) for k in range(CONF_KERNEL)]


def _mixer_fwd(p, x1, cw, vec, pool_w, wstack, bias, w_out):
    s, d = x1.shape
    t = _tile(s, T_MIX_FWD)
    n_ext = t + HALO
    dm = w_out.shape[0]

    def body(p_ref, x1_ref, cw_ref, vec_ref, pw_ref, ws_ref, bias_ref, wo_ref, x2_ref, mix_s, z_ref, cy_s, cq_s, cx_s):
        i = pl.program_id(0)

        @pl.when(i == 0)
        def _():
            cy_s[...] = jnp.zeros_like(cy_s)
            cq_s[...] = jnp.zeros_like(cq_s)
            cx_s[...] = jnp.zeros_like(cx_s)

        grp = _lane_group()
        y = _piece(p_ref, 0) * _sigmoid(_piece(p_ref, 1))
        ext = jnp.concatenate([cy_s[...], y], axis=0)
        cy_s[...] = y[t - HALO:t]
        z = jnp.broadcast_to(vec_ref[0:1, :], (t, C))
        shifted = {}
        for k, r, off in _conv_taps():
            if r not in shifted:
                shifted[r] = _up(ext, r)
            z = z + cw_ref[k:k + 1, :] * shifted[r][off:off + t]
        z_ref[...] = z
        ln = _ln_fwd(z, vec_ref[1:2, :], vec_ref[2:3, :])[2]
        mix_s[:, 0:C] = (ln * _sigmoid(ln)).astype(BF16)
        q = _piece(p_ref, 3) * _piece(p_ref, 4)
        ext = jnp.concatenate([cq_s[...], q], axis=0)
        cq_s[...] = q[t - HALO:t]
        cz = vec_ref[8:9, :] * q + vec_ref[7:8, :] * _down(ext, 1)[HALO:] + vec_ref[6:7, :] * _down(ext, 2)[HALO:]
        mix_s[:, C:2 * C] = (_piece(p_ref, 2) * cz).astype(BF16)
        xp = _piece(p_ref, 5)
        ext = jnp.concatenate([cx_s[...], xp], axis=0)
        cx_s[...] = xp[t - HALO:t]
        dd = _trailing_sums(ext, grp, t) / _pool_count(grp, i * t, t) - xp
        mix_s[:, 2 * C:3 * C] = (_dot(dd.astype(BF16), pw_ref[...]) * vec_ref[3:4, :]).astype(BF16)
        vln = _ln_fwd(_piece(p_ref, 7), vec_ref[4:5, :], vec_ref[5:6, :])[2].astype(BF16)
        for n in range(t // CHUNK):
            rows = slice(n * CHUNK, (n + 1) * CHUNK)
            mixed = _head_select(_dot(ws_ref[...], vln[rows]), grp) + bias_ref[...]
            mix_s[rows, 3 * C:4 * C] = (p_ref[rows, 6 * C:7 * C] * mixed).astype(BF16)
        x2_ref[...] = x1_ref[...] + _dot(mix_s[...], wo_ref[...])

    full = lambda a: pl.BlockSpec(a.shape, lambda i: (0, 0))
    return pl.pallas_call(
        body, name="mixer_fwd", grid=(s // t,),
        in_specs=[pl.BlockSpec((t, p.shape[1]), lambda i: (i, 0)), pl.BlockSpec((t, d), lambda i: (i, 0)),
                  full(cw), full(vec), full(pool_w), full(wstack), full(bias), full(w_out)],
        out_specs=[pl.BlockSpec((t, d), lambda i: (i, 0)), pl.BlockSpec((t, dm), lambda i: (i, 0)),
                   pl.BlockSpec((t, C), lambda i: (i, 0))],
        out_shape=[jax.ShapeDtypeStruct((s, d), F32), jax.ShapeDtypeStruct((s, dm), BF16), jax.ShapeDtypeStruct((s, C), F32)],
        scratch_shapes=[pltpu.VMEM((HALO, C), F32)] * 3,
        compiler_params=_params("arbitrary"),
    )(p, x1, cw, vec, pool_w, wstack, bias, w_out)


def _mixer_bwd(dx2, p, z, cw, vec, pool_w, wstack, wstack_t, bias, tril4, head_rows, w_out):
    s, d = dx2.shape
    t = _tile(s, T_MIX_BWD)
    nt = s // t
    n_ext = t + HALO
    hb = t // HALO

    def body(dx2_ref, p_ref, ph_ref, z_ref, cw_ref, vec_ref, pw_ref, ws_ref, wst_ref, bias_ref, tril_ref, hr_ref, wo_ref,
             dp_ref, dcw_ref, dvec_ref, dpool_ref, dws_ref, dbs_ref, cdz_s, cdc_s, cf_s, vy_s, dvl_s, dbias_s):
        i = pl.program_id(0)
        tile = nt - 1 - i

        @pl.when(i == 0)
        def _():
            for ref in (cdz_s, cdc_s, cf_s, dbias_s, dcw_ref, dvec_ref, dpool_ref, dws_ref, dbs_ref):
                ref[...] = jnp.zeros_like(ref)

        grp = _lane_group()
        first = jnp.where(tile > 0, 1.0, 0.0)
        dmix = _dot_nt(dx2_ref[...].astype(BF16), wo_ref[...])
        d_a, d_b, d_c, d_d = (dmix[:, k * C:(k + 1) * C] for k in range(4))

        def acc_vec(row, v):
            dvec_ref[row:row + 1, :] += jnp.sum(v, axis=0, keepdims=True)

        val, gate = _piece(p_ref, 0), _piece(p_ref, 1)
        sgate = _sigmoid(gate)
        y = val * sgate
        y_halo = ph_ref[:, 0:C] * _sigmoid(ph_ref[:, C:2 * C]) * first
        ext = jnp.concatenate([y_halo, y], axis=0)
        for r in range(8):
            vy_s[r] = _up(ext, r)
        zn, rs, ln = _ln_fwd(z_ref[...], vec_ref[1:2, :], vec_ref[2:3, :])
        sg = _sigmoid(ln)
        dln = d_a * (sg * (1.0 + ln * (1.0 - sg)))
        dz, dg, db = _ln_bwd(dln, zn, rs, vec_ref[1:2, :])
        dvec_ref[1:2, :] += dg
        dvec_ref[2:3, :] += db
        acc_vec(0, dz)
        for k, r, off in _conv_taps():
            dcw_ref[k:k + 1, :] += jnp.sum(dz * vy_s[r, off:off + t, :], axis=0, keepdims=True)
        ext = jnp.concatenate([dz, cdz_s[...]], axis=0)
        cdz_s[...] = dz[0:HALO]
        dy = jnp.zeros((t, C), F32)
        shifted = {}
        for k in range(CONF_KERNEL):
            m = CONF_KERNEL - 1 - k
            r, off = m % 8, m - m % 8
            if r not in shifted:
                shifted[r] = _up(ext, r)
            dy = dy + cw_ref[k:k + 1, :] * shifted[r][off:off + t]
        dp_ref[:, 0:C] = (dy * sgate).astype(BF16)
        dp_ref[:, C:2 * C] = (dy * val * sgate * (1.0 - sgate)).astype(BF16)

        sb, sc, sx = _piece(p_ref, 2), _piece(p_ref, 3), _piece(p_ref, 4)
        q = sc * sx
        q_halo = ph_ref[:, 3 * C:4 * C] * ph_ref[:, 4 * C:5 * C] * first
        ext = jnp.concatenate([q_halo, q], axis=0)
        q1, q2 = _down(ext, 1)[HALO:], _down(ext, 2)[HALO:]
        cz = vec_ref[8:9, :] * q + vec_ref[7:8, :] * q1 + vec_ref[6:7, :] * q2
        dcz = d_b * sb
        dp_ref[:, 2 * C:3 * C] = (d_b * cz).astype(BF16)
        acc_vec(8, dcz * q)
        acc_vec(7, dcz * q1)
        acc_vec(6, dcz * q2)
        ext = jnp.concatenate([dcz, cdc_s[...]], axis=0)
        cdc_s[...] = dcz[0:HALO]
        dq = vec_ref[8:9, :] * dcz + vec_ref[7:8, :] * _up(ext, 1)[0:t] + vec_ref[6:7, :] * _up(ext, 2)[0:t]
        dp_ref[:, 3 * C:4 * C] = (dq * sx).astype(BF16)
        dp_ref[:, 4 * C:5 * C] = (dq * sc).astype(BF16)

        xp = _piece(p_ref, 5)
        ext = jnp.concatenate([ph_ref[:, 5 * C:6 * C] * first, xp], axis=0)
        cnt = _pool_count(grp, tile * t, t)
        dd = (_trailing_sums(ext, grp, t) / cnt - xp).astype(BF16)
        e2 = _dot(dd, pw_ref[...])
        acc_vec(3, d_c * e2)
        de = (d_c * vec_ref[3:4, :]).astype(BF16)
        dpool_ref[...] += _dot(_t_bf16(dd.astype(F32)), de)
        ddd = _dot_nt(de, pw_ref[...])
        fq = ddd / cnt
        ext = jnp.concatenate([fq, cf_s[...]], axis=0)
        cf_s[...] = fq[0:HALO]
        dp_ref[:, 5 * C:6 * C] = (_leading_sums(ext, grp, t) - ddd).astype(BF16)

        vn, vrs, vlnf = _ln_fwd(_piece(p_ref, 7), vec_ref[4:5, :], vec_ref[5:6, :])
        vln = vlnf.astype(BF16)
        for n in range(t // CHUNK):
            rows = slice(n * CHUNK, (n + 1) * CHUNK)
            mixed = _head_select(_dot(ws_ref[...], vln[rows]), grp) + bias_ref[...]
            dd_n = d_d[rows]
            dp_ref[rows, 6 * C:7 * C] = (dd_n * mixed).astype(BF16)
            dmx = dd_n * p_ref[rows, 6 * C:7 * C]
            dbias_s[...] += dmx
            dmx_b = dmx.astype(BF16)
            dvl_s[rows, :] = _head_select(_dot(wst_ref[...], dmx_b), grp)
            for h in range(N_HEADS):
                hrows = slice(h * CHUNK, (h + 1) * CHUNK)
                dws_ref[hrows, :] += _dot_nt(jnp.where(grp == h, dmx_b, jnp.zeros_like(dmx_b)), vln[rows])
        dvl = dvl_s[...]
        dv, dg, db = _ln_bwd(dvl, vn, vrs, vec_ref[4:5, :])
        dvec_ref[4:5, :] += dg
        dvec_ref[5:6, :] += db
        dp_ref[:, 7 * C:8 * C] = dv.astype(BF16)

        @pl.when(i == nt - 1)
        def _():
            dws_ref[...] = dws_ref[...] * tril_ref[...]
            dbs_ref[...] = lax.dot_general(hr_ref[...], dbias_s[...], NT, precision=lax.Precision.HIGHEST,
                                           preferred_element_type=F32)

    full = lambda a: pl.BlockSpec(a.shape, lambda i: (0, 0))
    acc = lambda shape: pl.BlockSpec(shape, lambda i: (0, 0))
    f = p.shape[1]
    return pl.pallas_call(
        body, name="mixer_bwd", grid=(nt,),
        in_specs=[pl.BlockSpec((t, d), lambda i: (nt - 1 - i, 0)), pl.BlockSpec((t, f), lambda i: (nt - 1 - i, 0)),
                  pl.BlockSpec((HALO, f), lambda i: (jnp.maximum((nt - 1 - i) * hb - 1, 0), 0)),
                  pl.BlockSpec((t, C), lambda i: (nt - 1 - i, 0)), full(cw), full(vec), full(pool_w), full(wstack), full(wstack_t), full(bias), full(tril4), full(head_rows),
                  full(w_out)],
        out_specs=[pl.BlockSpec((t, f), lambda i: (nt - 1 - i, 0)), acc((32, C)), acc((16, C)), acc((C, C)),
                   acc((N_HEADS * CHUNK, CHUNK)), acc((8, CHUNK))],
        out_shape=[jax.ShapeDtypeStruct((s, f), BF16), jax.ShapeDtypeStruct((32, C), F32), jax.ShapeDtypeStruct((16, C), F32),
                   jax.ShapeDtypeStruct((C, C), F32), jax.ShapeDtypeStruct((N_HEADS * CHUNK, CHUNK), F32),
                   jax.ShapeDtypeStruct((8, CHUNK), F32)],
        scratch_shapes=[pltpu.VMEM((HALO, C), F32)] * 3 + [pltpu.VMEM((8, n_ext, C), F32), pltpu.VMEM((t, C), F32),
                                                            pltpu.VMEM((CHUNK, C), F32)],
        compiler_params=_params("arbitrary"),
    )(dx2, p, p, z, cw, vec, pool_w, wstack, wstack_t, bias, tril4, head_rows, w_out)


def _loss_bwd(x, g, target):
    s, d = x.shape
    tm = _tile(s, 1024)

    def body(x_ref, g_ref, t_ref, dx_ref, dg_ref, loss_ref):
        @pl.when(pl.program_id(0) == 0)
        def _():
            dg_ref[...] = jnp.zeros_like(dg_ref)
            loss_ref[...] = jnp.zeros_like(loss_ref)

        n, r, y = _rms(x_ref[...], g_ref[...])
        err = y - t_ref[...]
        loss_ref[...] += 0.5 * jnp.sum(jnp.mean(err * err, axis=-1, keepdims=True), axis=0, keepdims=True)
        dxr, dg = _rms_bwd(err * (1.0 / d), n, r, g_ref[...])
        dx_ref[...] = dxr
        dg_ref[0:1, :] += dg

    row = pl.BlockSpec((tm, d), lambda i: (i, 0))
    return pl.pallas_call(
        body, name="loss_bwd", grid=(s // tm,),
        in_specs=[row, pl.BlockSpec((1, d), lambda i: (0, 0)), row],
        out_specs=[row, pl.BlockSpec((8, d), lambda i: (0, 0)), pl.BlockSpec((8, LANES), lambda i: (0, 0))],
        out_shape=[jax.ShapeDtypeStruct((s, d), F32), jax.ShapeDtypeStruct((8, d), F32), jax.ShapeDtypeStruct((8, LANES), F32)],
        compiler_params=_params("arbitrary"),
    )(x, g, target)


def _adamw_math(w, g, m, v):
    m = ADAM_B1 * m + (1.0 - ADAM_B1) * g
    v = ADAM_B2 * v + (1.0 - ADAM_B2) * (g * g)
    m_hat = m / (1.0 - ADAM_B1 ** ADAM_STEP)
    v_hat = v / (1.0 - ADAM_B2 ** ADAM_STEP)
    return -ADAM_LR * (m_hat / (jnp.sqrt(v_hat) + ADAM_EPS) + ADAM_WD * w), m, v


def _adamw(w, g, m, v):
    r, c = w.shape
    tr = r // 8 if r % 64 == 0 else r

    def body(w_ref, g_ref, m_ref, v_ref, d_ref, mo_ref, vo_ref, go_ref):
        gv = g_ref[...]
        d_ref[...], mo_ref[...], vo_ref[...] = _adamw_math(w_ref[...], gv, m_ref[...], v_ref[...])
        go_ref[...] = gv

    blk = pl.BlockSpec((tr, c), lambda i: (i, 0))
    return pl.pallas_call(
        body, name="adamw", grid=(r // tr,), in_specs=[blk] * 4, out_specs=[blk] * 4,
        out_shape=[jax.ShapeDtypeStruct((r, c), F32)] * 4, compiler_params=_params("arbitrary"),
    )(w, g, m, v)


def _adamw_small(ws, gs, ms, vs):
    n = len(ws)

    def body(*refs):
        ins, outs = refs[:4 * n], refs[4 * n:]
        for k in range(n):
            dl, mo, vo = _adamw_math(ins[k][...], ins[n + k][...], ins[2 * n + k][...], ins[3 * n + k][...])
            outs[k][...], outs[n + k][...], outs[2 * n + k][...] = dl, mo, vo

    vm = pl.BlockSpec(memory_space=pltpu.VMEM)
    out = pl.pallas_call(
        body, name="adamw_small", in_specs=[vm] * (4 * n), out_specs=[vm] * (3 * n),
        out_shape=[jax.ShapeDtypeStruct(a.shape, F32) for a in ws] * 3,
        compiler_params=pltpu.CompilerParams(vmem_limit_bytes=VMEM_LIMIT),
    )(*ws, *gs, *ms, *vs)
    return out[:n], out[n:2 * n], out[2 * n:]


def _where_am_i():
    x, y, c = lax.axis_index("x"), lax.axis_index("y"), lax.axis_index("c")
    chips = [(1 - x, y), (x, 1 - y), (1 - x, 1 - y)]
    return x, y, c, chips


def _chip_id(chip):
    return 2 * chip[0] + chip[1]


def _landing_shapes(shards):
    return [jax.ShapeDtypeStruct((3, a.shape[0] // 2, a.shape[1]), a.dtype) for a in shards]


def _ici_gather_copies(ins, lands, send_sems, recv_sems):
    _, _, c, chips = _where_am_i()
    copies = []
    for a, src in enumerate(ins):
        hr = src.shape[0] // 2
        for j, chip in enumerate(chips):
            copies.append(pltpu.make_async_remote_copy(
                src_ref=src.at[pl.ds(c * hr, hr)], dst_ref=lands[a].at[j], send_sem=send_sems.at[3 * a + j],
                recv_sem=recv_sems.at[3 * a + j], device_id=(*chip, c), device_id_type=MESH))
    return copies


def _ici_reduce_copies(s_ref, o_ref, send_sems, recv_sems, local_sem):
    x, y, c, chips = _where_am_i()
    copies = [pltpu.make_async_remote_copy(src_ref=s_ref.at[_chip_id(chip)], dst_ref=o_ref.at[j], send_sem=send_sems.at[j],
                                           recv_sem=recv_sems.at[j], device_id=(*chip, c), device_id_type=MESH)
              for j, chip in enumerate(chips)]
    return copies, pltpu.make_async_copy(s_ref.at[_chip_id((x, y))], o_ref.at[3], local_sem)


def _wait_all(copies):
    for cp in copies:
        cp.wait_recv()
    for cp in copies:
        cp.wait_send()


def _ici_gather(shards, small):
    n = len(shards)

    def body(*refs):
        begin_small, finish_small = _small_gather_steps(refs[n], refs[2 * n + 1], *refs[2 * n + 4:])
        copies = _ici_gather_copies(refs[:n], refs[n + 1:2 * n + 1], refs[2 * n + 2], refs[2 * n + 3])
        begin_small()
        for cp in copies:
            cp.start()
        finish_small()
        _wait_all(copies)

    vm = pl.BlockSpec(memory_space=pltpu.VMEM)
    out = pl.pallas_call(
        body, name="ici_gather", in_specs=[ANY] * n + [vm], out_specs=[ANY] * n + [vm],
        out_shape=_landing_shapes(shards) + [jax.ShapeDtypeStruct((N_DEV * small.shape[0], small.shape[1]), F32)],
        scratch_shapes=[pltpu.SemaphoreType.DMA((3 * n,))] * 2 + _small_gather_sems(),
    )(*shards, small)
    return list(out[:n]), out[n]


SLOTS = 4


def _sibling_gather(shards, lands):
    n = len(shards)
    d = shards[0].shape[1]
    halves = [a.shape[0] // 2 for a in shards]
    hmax = max(halves)
    chunks = [(a, j, halves[a]) for a in range(n) for j in range(3)]
    nc = len(chunks)

    def body(*refs):
        ins, lnd, outs = refs[:n], refs[n:2 * n], refs[2 * n:3 * n]
        sbuf, rbuf, obuf, ld_sems, take_sems, send_sems, recv_sems, place_sems, own_ld_sems, own_st_sems, credits = refs[3 * n:]
        x, y, c, chips = _where_am_i()
        sibling = (x, y, 1 - c)
        me = _chip_id((x, y))

        def load(i):
            a, j, hr = chunks[i]
            return pltpu.make_async_copy(lnd[a].at[j], sbuf.at[i % SLOTS, pl.ds(0, hr)], ld_sems.at[i % SLOTS])

        def push(i):
            hr, slot = chunks[i][2], i % SLOTS
            return pltpu.make_async_remote_copy(src_ref=sbuf.at[slot, pl.ds(0, hr)], dst_ref=rbuf.at[slot, pl.ds(0, hr)],
                                                send_sem=send_sems.at[slot], recv_sem=recv_sems.at[slot], device_id=sibling,
                                                device_id_type=MESH)

        def take(i):
            a, j, hr = chunks[i]
            return pltpu.make_async_copy(rbuf.at[i % SLOTS, pl.ds(0, hr)],
                                         outs[a].at[_chip_id(chips[j]), pl.ds((1 - c) * hr, hr)], take_sems.at[i % SLOTS])

        def place(i):
            a, j, hr = chunks[i]
            return pltpu.make_async_copy(sbuf.at[i % SLOTS, pl.ds(0, hr)], outs[a].at[_chip_id(chips[j]), pl.ds(c * hr, hr)],
                                         place_sems.at[i % SLOTS])

        own = [(a, h, halves[a]) for a in range(n) for h in range(2)]

        def own_load(k):
            a, h, hr = own[k]
            return pltpu.make_async_copy(ins[a].at[pl.ds(h * hr, hr)], obuf.at[k % 2, pl.ds(0, hr)], own_ld_sems.at[k % 2])

        def own_store(k):
            a, h, hr = own[k]
            return pltpu.make_async_copy(obuf.at[k % 2, pl.ds(0, hr)], outs[a].at[me, pl.ds(h * hr, hr)], own_st_sems.at[k % 2])

        def own_step(k):
            if k < len(own):
                if k >= 2:
                    own_store(k - 2).wait()
                own_load(k).start()
            if 1 <= k <= len(own):
                own_load(k - 1).wait()
                own_store(k - 1).start()

        for i in range(min(2, nc)):
            load(i).start()
        for i in range(nc):
            own_step(i)
            if i >= 2:
                push(i - 2).wait_send()
                place(i - 2).wait()
            if i + 2 < nc:
                load(i + 2).start()
            load(i).wait()
            place(i).start()
            if i >= SLOTS:
                pl.semaphore_wait(credits.at[i % SLOTS], 1)
            push(i).start()
            if i >= 1:
                push(i - 1).wait_recv()
                take(i - 1).start()
            if i >= 2:
                take(i - 2).wait()
                if i - 2 + SLOTS < nc:
                    pl.semaphore_signal(credits.at[(i - 2) % SLOTS], inc=1, device_id=sibling, device_id_type=MESH)
        for i in range(max(0, nc - 2), nc):
            push(i).wait_send()
            place(i).wait()
        push(nc - 1).wait_recv()
        take(nc - 1).start()
        for i in range(max(0, nc - 2), nc):
            take(i).wait()
        for k in range(nc, len(own) + 1):
            own_step(k)
        for k in range(max(0, len(own) - 2), len(own)):
            own_store(k).wait()

    dma = pltpu.SemaphoreType.DMA((SLOTS,))
    dma2 = pltpu.SemaphoreType.DMA((2,))
    return pl.pallas_call(
        body, name="sibling_gather", in_specs=[ANY] * (2 * n), out_specs=[ANY] * n,
        out_shape=[jax.ShapeDtypeStruct((N_CHIPS,) + a.shape, a.dtype) for a in shards],
        scratch_shapes=[pltpu.VMEM((SLOTS, hmax, d), BF16), pltpu.VMEM((SLOTS, hmax, d), BF16), pltpu.VMEM((2, hmax, d), BF16),
                        dma, dma, dma, dma, dma, dma2, dma2, pltpu.SemaphoreType.REGULAR((SLOTS,))],
        compiler_params=pltpu.CompilerParams(vmem_limit_bytes=VMEM_LIMIT),
    )(*shards, *lands)


def _pair_sum(grads):
    n = len(grads)
    halves = [g.shape[1] // 2 for g in grads]
    total, rows = sum(halves), max(halves)
    d = grads[0].shape[2]
    chunks, off = [], 0
    for a in range(n):
        chunks += [(a, k, off, halves[a]) for k in range(N_CHIPS)]
        off += halves[a]
    nc = len(chunks)

    def body(*refs):
        ins, out_ref = refs[:n], refs[n]
        sbuf, rbuf, mbuf, obuf, ls_sems, lm_sems, st_sems, send_sems, recv_sems, credits = refs[n + 1:]
        x, y, c, _ = _where_am_i()
        sibling = (x, y, 1 - c)

        def load_theirs(i):
            a, k, _, hr = chunks[i]
            return pltpu.make_async_copy(ins[a].at[k, pl.ds((1 - c) * hr, hr)], sbuf.at[i % SLOTS, pl.ds(0, hr)], ls_sems.at[i % SLOTS])

        def load_mine(i):
            a, k, _, hr = chunks[i]
            return pltpu.make_async_copy(ins[a].at[k, pl.ds(c * hr, hr)], mbuf.at[i % SLOTS, pl.ds(0, hr)], lm_sems.at[i % SLOTS])

        def push(i):
            hr, slot = chunks[i][3], i % SLOTS
            return pltpu.make_async_remote_copy(src_ref=sbuf.at[slot, pl.ds(0, hr)], dst_ref=rbuf.at[slot, pl.ds(0, hr)],
                                                send_sem=send_sems.at[slot], recv_sem=recv_sems.at[slot], device_id=sibling,
                                                device_id_type=MESH)

        def store(i):
            _, k, o, hr = chunks[i]
            slot = i % SLOTS
            return pltpu.make_async_copy(obuf.at[slot, pl.ds(0, hr)], out_ref.at[k, pl.ds(o, hr)], st_sems.at[slot])

        def start_push(i):
            load_theirs(i).wait()
            if i >= SLOTS:
                pl.semaphore_wait(credits.at[i % SLOTS], 1)
            push(i).start()

        for i in range(min(2, nc)):
            load_theirs(i).start()
            load_mine(i).start()
        start_push(0)
        for i in range(nc):
            hr, slot = chunks[i][3], i % SLOTS
            if i + 2 < nc:
                load_theirs(i + 2).start()
                load_mine(i + 2).start()
            if i + 1 < nc:
                start_push(i + 1)
            push(i).wait_recv()
            push(i).wait_send()
            load_mine(i).wait()
            if i >= SLOTS:
                store(i - SLOTS).wait()
            obuf[slot, 0:hr, :] = (mbuf[slot, 0:hr, :].astype(F32) + rbuf[slot, 0:hr, :].astype(F32)).astype(BF16)
            if i + SLOTS < nc:
                pl.semaphore_signal(credits.at[slot], inc=1, device_id=sibling, device_id_type=MESH)
            store(i).start()
        for i in range(max(0, nc - SLOTS), nc):
            store(i).wait()

    stage = pltpu.VMEM((SLOTS, rows, d), BF16)
    dma = pltpu.SemaphoreType.DMA((SLOTS,))
    return pl.pallas_call(
        body, name="pair_sum", in_specs=[ANY] * n, out_specs=ANY, out_shape=jax.ShapeDtypeStruct((N_CHIPS, total, d), BF16),
        scratch_shapes=[stage, stage, stage, stage, dma, dma, dma, dma, dma, pltpu.SemaphoreType.REGULAR((SLOTS,))],
        compiler_params=pltpu.CompilerParams(vmem_limit_bytes=VMEM_LIMIT),
    )(*grads)


def _reduce_chips(sums):
    def body(s_ref, o_ref, send_sems, recv_sems, local_sem):
        copies, local = _ici_reduce_copies(s_ref, o_ref, send_sems, recv_sems, local_sem)
        local.start()
        for cp in copies:
            cp.start()
        _wait_all(copies)
        local.wait()

    return pl.pallas_call(
        body, name="reduce_chips", in_specs=[ANY], out_specs=ANY, out_shape=jax.ShapeDtypeStruct(sums.shape, BF16),
        scratch_shapes=[pltpu.SemaphoreType.DMA((3,)), pltpu.SemaphoreType.DMA((3,)), pltpu.SemaphoreType.DMA],
    )(sums)


def _sum_share(parts, groups, rows, small):
    layers, n, n_g = len(parts), len(rows), len(groups)
    parts = [part for layer in parts for part in layer]
    n_l = len(parts)
    d = parts[0].shape[2]
    halves = [r // 2 for r in rows]
    hmax = max(halves)
    chunks = []
    for l in range(layers):
        for g, members in enumerate(groups):
            off = 0
            for a in members:
                chunks.append((l * n_g + g, a, off, halves[a], l))
                off += halves[a]
    nc = len(chunks)

    def body(*refs):
        ins, small_ref, outs, total_ref = refs[:n_l], refs[n_l], refs[n_l + 1:n_l + 1 + n], refs[n_l + 1 + n]
        pbuf, obuf, rbuf, ld_sems, keep_sems, take_sems, send_sems, recv_sems, credits, all_ref = refs[n_l + 2 + n:n_l + 12 + n]
        begin_small, finish_small = _small_gather_steps(small_ref, all_ref, *refs[n_l + 12 + n:])
        begin_small()
        x, y, c, _ = _where_am_i()
        sibling = (x, y, 1 - c)

        def load(i):
            part, _, off, hr, _ = chunks[i]
            return pltpu.make_async_copy(ins[part].at[:, pl.ds(off, hr)], pbuf.at[i % SLOTS, :, pl.ds(0, hr)], ld_sems.at[i % SLOTS])

        def keep(i):
            _, a, _, hr, l = chunks[i]
            return pltpu.make_async_copy(obuf.at[i % SLOTS, pl.ds(0, hr)], outs[a].at[l, pl.ds(c * hr, hr)], keep_sems.at[i % SLOTS])

        def push(i):
            hr, slot = chunks[i][3], i % SLOTS
            return pltpu.make_async_remote_copy(src_ref=obuf.at[slot, pl.ds(0, hr)], dst_ref=rbuf.at[slot, pl.ds(0, hr)],
                                                send_sem=send_sems.at[slot], recv_sem=recv_sems.at[slot], device_id=sibling,
                                                device_id_type=MESH)

        def take(i):
            _, a, _, hr, l = chunks[i]
            return pltpu.make_async_copy(rbuf.at[i % SLOTS, pl.ds(0, hr)], outs[a].at[l, pl.ds((1 - c) * hr, hr)],
                                         take_sems.at[i % SLOTS])

        for i in range(min(2, nc)):
            load(i).start()
        for i in range(nc):
            hr, slot = chunks[i][3], i % SLOTS
            if i + 2 < nc:
                load(i + 2).start()
            load(i).wait()
            if i >= SLOTS:
                keep(i - SLOTS).wait()
                push(i - SLOTS).wait_send()
            part = lambda k: pbuf[slot, k, 0:hr, :].astype(F32)
            obuf[slot, 0:hr, :] = ((part(3) + part(0)) + part(1)) + part(2)
            keep(i).start()
            if i >= SLOTS:
                pl.semaphore_wait(credits.at[slot], 1)
            push(i).start()
            if i >= 1:
                push(i - 1).wait_recv()
                take(i - 1).start()
            if i >= 2:
                take(i - 2).wait()
                if i - 2 + SLOTS < nc:
                    pl.semaphore_signal(credits.at[(i - 2) % SLOTS], inc=1, device_id=sibling, device_id_type=MESH)
        push(nc - 1).wait_recv()
        take(nc - 1).start()
        for i in range(max(0, nc - 2), nc):
            take(i).wait()
        for i in range(max(0, nc - SLOTS), nc):
            keep(i).wait()
            push(i).wait_send()
        finish_small()
        total_ref[...] = _sum_blocks(all_ref, small.shape[0])

    dma = pltpu.SemaphoreType.DMA((SLOTS,))
    vm = pl.BlockSpec(memory_space=pltpu.VMEM)
    out = pl.pallas_call(
        body, name="sum_share", in_specs=[ANY] * n_l + [vm], out_specs=[ANY] * n + [vm],
        out_shape=[jax.ShapeDtypeStruct((layers, r, d), F32) for r in rows] + [jax.ShapeDtypeStruct(small.shape, F32)],
        scratch_shapes=[pltpu.VMEM((SLOTS, N_CHIPS, hmax, d), BF16), pltpu.VMEM((SLOTS, hmax, d), F32),
                        pltpu.VMEM((SLOTS, hmax, d), F32), dma, dma, dma, dma, dma, pltpu.SemaphoreType.REGULAR((SLOTS,)),
                        pltpu.VMEM((N_DEV * small.shape[0], small.shape[1]), F32)] + _small_gather_sems(),
        compiler_params=pltpu.CompilerParams(vmem_limit_bytes=VMEM_LIMIT),
    )(*parts, small)
    return out[:n], out[n]


def _small_gather_sems():
    return [pltpu.SemaphoreType.DMA((7,)), pltpu.SemaphoreType.DMA((7,)), pltpu.SemaphoreType.DMA]


def _sum_blocks(all_ref, m):
    total = all_ref[0:m, :]
    for dev in range(1, N_DEV):
        total = total + all_ref[dev * m:(dev + 1) * m, :]
    return total


def _small_gather_steps(x_ref, all_ref, send_sems, recv_sems, local_sem):
    m = x_ref.shape[0]
    x, y, c, chips = _where_am_i()
    me, sibling = (x, y, c), (x, y, 1 - c)

    def rows(px, py, pc):
        return all_ref.at[pl.ds((4 * px + 2 * py + pc) * m, m), :]

    def copy(k, blk, to, src=None):
        return pltpu.make_async_remote_copy(src_ref=rows(*blk) if src is None else src, dst_ref=rows(*blk),
                                            send_sem=send_sems.at[k], recv_sem=recv_sems.at[k], device_id=to,
                                            device_id_type=MESH)

    def own():
        return pltpu.make_async_copy(x_ref, rows(*me), local_sem)

    def first():
        return [copy(0, me, sibling, src=x_ref)] + [copy(1 + j, me, (*chip, c), src=x_ref) for j, chip in enumerate(chips)]

    def passed():
        return [copy(4 + j, (*chip, c), sibling) for j, chip in enumerate(chips)]

    def begin():
        own().start()
        for cp in first():
            cp.start()

    def finish():
        forwards = passed()
        for j, chip in enumerate(chips):
            copy(1 + j, (*chip, c), me).wait_recv()
            forwards[j].start()
        copy(0, sibling, me).wait_recv()
        for j, chip in enumerate(chips):
            copy(4 + j, (*chip, 1 - c), me).wait_recv()
        for cp in first() + forwards:
            cp.wait_send()
        own().wait()

    return begin, finish


def _pack(arrays):
    flat = jnp.concatenate([a.reshape(-1) for a in arrays])
    pad = (-flat.shape[0]) % (8 * LANES)
    return jnp.pad(flat, (0, pad)).reshape(-1, LANES)


def _unpack(buf, shapes):
    flat = buf.reshape(-1)
    out, off = [], 0
    for shp in shapes:
        size = 1
        for dim in shp:
            size *= dim
        out.append(flat[off:off + size].reshape(shp))
        off += size
    return out


BIG = ("ffn1_w1", "ffn1_w3", "ffn1_w2", "w_in", "w_out", "ffn2_w1", "ffn2_w3", "ffn2_w2")
TRANSPOSED = ("ffn1_w1", "ffn1_w3", "w_in", "ffn2_w1", "ffn2_w3")
SMALL = ("ffn1_norm", "mix_norm", "conf_conv_w", "conf_conv_b", "conf_ln_g", "conf_ln_b", "sconv_w", "pool_w", "pool_scale",
         "gmlp_ln_g", "gmlp_ln_b", "gmlp_w_s", "gmlp_b_s", "ffn2_norm", "final_norm")
ORDER = ("ffn1_norm", "ffn1_w1", "ffn1_w3", "ffn1_w2", "mix_norm", "w_in", "conf_conv_w", "conf_conv_b", "conf_ln_g", "conf_ln_b",
         "sconv_w", "pool_w", "pool_scale", "gmlp_ln_g", "gmlp_ln_b", "gmlp_w_s", "gmlp_b_s", "w_out", "ffn2_norm", "ffn2_w1",
         "ffn2_w3", "ffn2_w2", "final_norm")


def _as2d(a):
    return a.reshape(-1, a.shape[-1])


def kernel(x, ffn1_norm, ffn1_w1, ffn1_w3, ffn1_w2, mix_norm, w_in, conf_conv_w, conf_conv_b, conf_ln_g, conf_ln_b, sconv_w, pool_w, pool_scale, gmlp_ln_g, gmlp_ln_b, gmlp_w_s, gmlp_b_s, w_out, ffn2_norm, ffn2_w1, ffn2_w3, ffn2_w2, final_norm, loss_target, m_ffn1_norm, m_ffn1_w1, m_ffn1_w3, m_ffn1_w2, m_mix_norm, m_w_in, m_conf_conv_w, m_conf_conv_b, m_conf_ln_g, m_conf_ln_b, m_sconv_w, m_pool_w, m_pool_scale, m_gmlp_ln_g, m_gmlp_ln_b, m_gmlp_w_s, m_gmlp_b_s, m_w_out, m_ffn2_norm, m_ffn2_w1, m_ffn2_w3, m_ffn2_w2, m_final_norm, v_ffn1_norm, v_ffn1_w1, v_ffn1_w3, v_ffn1_w2, v_mix_norm, v_w_in, v_conf_conv_w, v_conf_conv_b, v_conf_ln_g, v_conf_ln_b, v_sconv_w, v_pool_w, v_pool_scale, v_gmlp_ln_g, v_gmlp_ln_b, v_gmlp_w_s, v_gmlp_b_s, v_w_out, v_ffn2_norm, v_ffn2_w1, v_ffn2_w3, v_ffn2_w2, v_final_norm):
    given = dict(locals())
    w = {k: given[k] for k in ORDER}
    mom = {k: given["m_" + k] for k in ORDER}
    var = {k: given["v_" + k] for k in ORDER}
    n_l = ffn1_w1.shape[0]
    xs = x[0]
    d = xs.shape[1]
    chip = 2 * lax.axis_index("x") + lax.axis_index("y")

    def shard(name, l):
        a = w[name][l]
        return (jnp.swapaxes(w[name], 1, 2)[l] if name in TRANSPOSED else a).astype(BF16)

    groups = (BIG[:3], BIG[3:])

    def shards_of(l, g):
        return [shard(name, l) for name in groups[g]] if l < n_l else []

    def finish_gather(l, g, lands):
        out = _sibling_gather(shards_of(l, g), lands)
        return {name: a.reshape(-1, d) for name, a in zip(groups[g], out)}

    shard_rows = [w[name].shape[2] if name in TRANSPOSED else w[name].shape[1] for name in BIG]

    conv_shapes = [conf_conv_w.shape, sconv_w.shape]
    first_lands, conv_all = _ici_gather(shards_of(0, 0), _pack([conf_conv_w, sconv_w]))
    conv_all = conv_all.reshape(N_CHIPS, 2, -1)[:, 0]
    conf_full, sconv_full = [jnp.concatenate([_unpack(conv_all[k], conv_shapes)[a] for k in range(N_CHIPS)], axis=-1)
                             for a in range(2)]

    lane = jnp.arange(C) // HEAD_DIM
    head_rows = (jnp.arange(8)[:, None] == lane[None, :]).astype(F32)
    tril = jnp.tril(jnp.ones((CHUNK, CHUNK), F32))
    tril4 = jnp.tile(tril, (N_HEADS, 1))
    mixer_consts = []
    for l in range(n_l):
        cw = jnp.pad(conf_full[l], ((0, 32 - CONF_KERNEL), (0, 0)))
        vec = jnp.concatenate([conf_conv_b[l][None], conf_ln_g[l][None], conf_ln_b[l][None], pool_scale[l][None],
                               gmlp_ln_g[l][None], gmlp_ln_b[l][None], sconv_full[l], jnp.zeros((7, C), F32)], axis=0)
        eye = jnp.eye(len(pool_w[l]), dtype=F32)
        pool_blk = (eye[:, None, :, None] * pool_w[l][:, :, None, :]).reshape(C, C).astype(BF16)
        ws = gmlp_w_s[l] * tril[None]
        wstack = ws.reshape(N_HEADS * CHUNK, CHUNK).astype(BF16)
        wstack_t = jnp.swapaxes(ws, 1, 2).reshape(N_HEADS * CHUNK, CHUNK).astype(BF16)
        bias = jnp.repeat(gmlp_b_s[l].T, HEAD_DIM, axis=1)
        mixer_consts.append((cw, vec, pool_blk, wstack, wstack_t, bias))

    saved = []
    cur = xs
    gathered = [finish_gather(0, 0, first_lands)]
    for l in range(n_l):
        gw = gathered[l]
        cw, vec, pool_blk, wstack, wstack_t, bias = mixer_consts[l]
        x0 = cur
        x1, a1, b1, h1, lands = _ffn_fwd(x0, ffn1_norm[l][None], gw["ffn1_w1"], gw["ffn1_w3"], gw["ffn1_w2"], shards_of(l, 1))
        gw.update(finish_gather(l, 1, lands))
        p = _proj_fwd(x1, mix_norm[l][None], gw["w_in"])
        x2, mix, z = _mixer_fwd(p, x1, cw, vec, pool_blk, wstack, bias, gw["w_out"])
        x3, a2, b2, h2, lands = _ffn_fwd(x2, ffn2_norm[l][None], gw["ffn2_w1"], gw["ffn2_w3"], gw["ffn2_w2"], shards_of(l + 1, 0))
        if l + 1 < n_l:
            gathered.append(finish_gather(l + 1, 0, lands))
        saved.append((x0, x1, x2, a1, b1, a2, b2, p, mix, z, h1, h2))
        cur = x3

    dx, dg_final, loss_part = _loss_bwd(cur, final_norm[None], loss_target[0])

    small_parts = [None] * n_l
    reduced_halves = [[None, None] for _ in range(n_l)]
    pair_sum = lambda big, g: _pair_sum([big[name].reshape(N_CHIPS, -1, d) for name in groups[g]])
    pending = None
    for l in reversed(range(n_l)):
        gw = gathered[l]
        cw, vec, pool_blk, wstack, wstack_t, bias = mixer_consts[l]
        x0, x1, x2, a1, b1, a2, b2, p, mix, z, h1, h2 = saved[l]
        big = {}
        out = _ffn_bwd(dx, x2, ffn2_norm[l][None], a2, b2, gw["ffn2_w1"], gw["ffn2_w3"], gw["ffn2_w2"], pending)
        dx, dg_ffn2, dy, da, db, u, dx_bf = out[:7]
        if pending is not None:
            reduced_halves[l + 1][0] = out[7]
        big["ffn2_w1"], big["ffn2_w3"], big["ffn2_w2"] = _dw(da, h2), _dw(db, h2), _dw(u, dy)
        big["w_out"] = _dw(mix, dx_bf)
        dp, dcw, dvec, dpool, dws, dbs = _mixer_bwd(dx, p, z, cw, vec, pool_blk, wstack, wstack_t, bias, tril4, head_rows, gw["w_out"])
        dx, dg_mix, h = _proj_bwd(dx, x1, mix_norm[l][None], dp, gw["w_in"])
        big["w_in"] = _dw(dp, h)
        second = pair_sum(big, 1)
        out = _ffn_bwd(dx, x0, ffn1_norm[l][None], a1, b1, gw["ffn1_w1"], gw["ffn1_w3"], gw["ffn1_w2"], second)
        dx, dg_ffn1, dy, da, db, u = out[:6]
        reduced_halves[l][1] = out[7]
        big["ffn1_w1"], big["ffn1_w3"], big["ffn1_w2"] = _dw(da, h1), _dw(db, h1), _dw(u, dy)
        small_parts[l] = [dg_ffn1[0], dg_mix[0], dg_ffn2[0], dcw, dvec, dpool, dws, dbs]
        pending = pair_sum(big, 0)
    reduced_halves[0][0] = _reduce_chips(pending)
    grad_x = dx[None]

    part_shapes = [a.shape for a in small_parts[0]]
    tail = [dg_final[0], loss_part[0]]
    packed = _pack([a for l in range(n_l) for a in small_parts[l]] + tail)
    index_of = {name: a for a, name in enumerate(BIG)}
    full, summed = _sum_share(reduced_halves, [[index_of[name] for name in g] for g in groups], shard_rows, packed)
    full = dict(zip(BIG, full))
    grad = {}
    summed = _unpack(summed, part_shapes * n_l + [a.shape for a in tail])
    per_layer = [summed[l * len(part_shapes):(l + 1) * len(part_shapes)] for l in range(n_l)]
    stack = lambda k: jnp.stack([per_layer[l][k] for l in range(n_l)])
    dcw_all, dvec_all, dpool_all, dws_all, dbs_all = stack(3), stack(4), stack(5), stack(6), stack(7)
    loss = summed[-1][0]
    chip_cols = lambda a: lax.dynamic_slice_in_dim(a, chip * (C // N_CHIPS), C // N_CHIPS, axis=2)
    n_pool = pool_w.shape[1]
    grad.update(
        ffn1_norm=stack(0), mix_norm=stack(1), ffn2_norm=stack(2), final_norm=summed[-2],
        conf_conv_w=chip_cols(dcw_all[:, :CONF_KERNEL]), conf_conv_b=dvec_all[:, 0], conf_ln_g=dvec_all[:, 1],
        conf_ln_b=dvec_all[:, 2], pool_scale=dvec_all[:, 3], gmlp_ln_g=dvec_all[:, 4], gmlp_ln_b=dvec_all[:, 5],
        sconv_w=chip_cols(dvec_all[:, 6:6 + SHORT_KERNEL]),
        pool_w=jnp.stack([dpool_all[:, g * POOL_GROUP:(g + 1) * POOL_GROUP, g * POOL_GROUP:(g + 1) * POOL_GROUP]
                          for g in range(n_pool)], axis=1),
        gmlp_w_s=dws_all.reshape(n_l, N_HEADS, CHUNK, CHUNK), gmlp_b_s=dbs_all[:, :N_HEADS],
    )

    delta, new_m, new_v = {}, {}, {}
    for name in BIG:
        rows_of = (lambda a: jnp.swapaxes(a, 1, 2)) if name in TRANSPOSED else (lambda a: a)
        shp = full[name].shape
        out = _adamw(_as2d(rows_of(w[name])), _as2d(full[name]), _as2d(rows_of(mom[name])), _as2d(rows_of(var[name])))
        delta[name], new_m[name], new_v[name], grad[name] = (rows_of(o.reshape(shp)) for o in out)
    ds, ms, vs = _adamw_small([_as2d(w[k]) if w[k].ndim > 1 else w[k][None] for k in SMALL],
                              [_as2d(grad[k]) if grad[k].ndim > 1 else grad[k][None] for k in SMALL],
                              [_as2d(mom[k]) if mom[k].ndim > 1 else mom[k][None] for k in SMALL],
                              [_as2d(var[k]) if var[k].ndim > 1 else var[k][None] for k in SMALL])
    for k, dl, mo, vo in zip(SMALL, ds, ms, vs):
        delta[k], new_m[k], new_v[k] = dl.reshape(w[k].shape), mo.reshape(w[k].shape), vo.reshape(w[k].shape)

    return (loss, grad_x, *[grad[k] for k in ORDER], *[delta[k] for k in ORDER], *[new_m[k] for k in ORDER],
            *[new_v[k] for k in ORDER])
```
